```python
import math
import jax, jax.numpy as jnp
from jax import lax
import numpy as np

D_MODEL = 1024
BATCH = 8
SEQ = 4096
DEPTH = 4

N_MIXERS = 4
MIXER_WIDTH = D_MODEL // 4
D_MIX = N_MIXERS * MIXER_WIDTH
SGU_HEADS = 4
SGU_CHUNK = 128
S5_GROUP = 16
S5_GROUPS = MIXER_WIDTH // S5_GROUP
S5_STATE = 64
S5_DT_MIN = 1e-3
S5_DT_MAX = 1e-1
LRU_HEADS = 4
LRU_CONV = 4
LRU_C = 8.0
FOX_HEADS = 4
FOX_HEAD_DIM = MIXER_WIDTH // FOX_HEADS
ATTN_BLOCK = 128
D_FF = 4 * D_MODEL
RMS_EPS = 1e-6
D_IN_PROJ = 8 * MIXER_WIDTH + FOX_HEADS

kernel_name = "hybrid_parallel_heads_sgu_s5_rglru_fox"


def rms_norm(x, g):
    xf = x.astype(jnp.float32)
    y = xf * lax.rsqrt(jnp.mean(jnp.square(xf), axis=-1, keepdims=True) + RMS_EPS)
    return (y * g.astype(jnp.float32)).astype(x.dtype)


def group_rms_norm(y, g):
    bsz, seq, _ = y.shape
    yf = y.astype(jnp.float32).reshape(bsz, seq, N_MIXERS, MIXER_WIDTH)
    yf = yf * lax.rsqrt(jnp.mean(jnp.square(yf), axis=-1, keepdims=True) + RMS_EPS)
    return (yf.reshape(bsz, seq, D_MIX) * g.astype(jnp.float32)).astype(y.dtype)


def _linear_combine(e1, e2):
    a1, b1 = e1
    a2, b2 = e2
    return a1 * a2, a2 * b1 + b2


def _complex_linear_combine(e1, e2):
    a1r, a1i, b1r, b1i = e1
    a2r, a2i, b2r, b2i = e2
    ar = a2r * a1r - a2i * a1i
    ai = a2r * a1i + a2i * a1r
    br = a2r * b1r - a2i * b1i + b2r
    bi = a2r * b1i + a2i * b1r + b2i
    return ar, ai, br, bi


def sgu_mixer(u, v, norm_g, w_s, b_s):
    bsz, seq, _ = u.shape
    u = jax.nn.gelu(u)
    v = rms_norm(jax.nn.gelu(v), norm_g)
    mask = jnp.tril(jnp.ones((SGU_CHUNK, SGU_CHUNK), w_s.dtype))
    vh = v.reshape(bsz, seq // SGU_CHUNK, SGU_CHUNK, SGU_HEADS, MIXER_WIDTH // SGU_HEADS)
    mixed = jnp.einsum('hts,bcshd->bcthd', w_s * mask, vh) + b_s.T[None, None, :, :, None]
    return u * mixed.reshape(bsz, seq, MIXER_WIDTH)


def s5_mixer(u, lam_re, lam_im, log_dt, b_re, b_im, c_re, c_im, d, glu_w, glu_b):
    bsz, seq, _ = u.shape
    f32 = jnp.float32
    uf = u.astype(f32)
    lam_re = lam_re.astype(f32)
    lam_im = lam_im.astype(f32)
    dt = jnp.exp(log_dt.astype(f32))[:, None]
    mag = jnp.exp(lam_re * dt)
    abar_re = mag * jnp.cos(lam_im * dt)
    abar_im = mag * jnp.sin(lam_im * dt)
    denom = jnp.square(lam_re) + jnp.square(lam_im)
    num_re = abar_re - 1.0
    num_im = abar_im
    fac_re = (num_re * lam_re + num_im * lam_im) / denom
    fac_im = (num_im * lam_re - num_re * lam_im) / denom
    b_re = b_re.astype(f32)
    b_im = b_im.astype(f32)
    bbar_re = fac_re[..., None] * b_re - fac_im[..., None] * b_im
    bbar_im = fac_re[..., None] * b_im + fac_im[..., None] * b_re
    ug = uf.reshape(bsz, seq, S5_GROUPS, S5_GROUP)
    bu_re = jnp.einsum('blgh,gph->blgp', ug, bbar_re)
    bu_im = jnp.einsum('blgh,gph->blgp', ug, bbar_im)
    a_re = jnp.broadcast_to(abar_re, bu_re.shape)
    a_im = jnp.broadcast_to(abar_im, bu_im.shape)
    _, _, s_re, s_im = lax.associative_scan(
        _complex_linear_combine, (a_re, a_im, bu_re, bu_im), axis=1)
    y = (jnp.einsum('blgp,ghp->blgh', s_re, c_re.astype(f32))
         - jnp.einsum('blgp,ghp->blgh', s_im, c_im.astype(f32)))
    y = y.reshape(bsz, seq, MIXER_WIDTH) + d.astype(f32) * uf
    y = jax.nn.gelu(y)
    y = y * jax.nn.sigmoid(y @ glu_w.astype(f32) + glu_b.astype(f32))
    return y.astype(u.dtype)


def causal_depthwise_conv(x, w, b):
    out = lax.conv_general_dilated(
        x, w[:, None, :], window_strides=(1,), padding=[(LRU_CONV - 1, 0)],
        dimension_numbers=('NWC', 'WIO', 'NWC'), feature_group_count=x.shape[-1])
    return out + b


def rglru_mixer(x, gate, conv_w, conv_b, wa, ba, wx, bx, lam):
    bsz, seq, _ = x.shape
    f32 = jnp.float32
    xc = causal_depthwise_conv(x, conv_w, conv_b)
    xh = xc.reshape(bsz, seq, LRU_HEADS, MIXER_WIDTH // LRU_HEADS)
    r = jax.nn.sigmoid(jnp.einsum('blhi,hij->blhj', xh, wa) + ba).reshape(bsz, seq, MIXER_WIDTH)
    i = jax.nn.sigmoid(jnp.einsum('blhi,hij->blhj', xh, wx) + bx).reshape(bsz, seq, MIXER_WIDTH)
    log_a = -LRU_C * r.astype(f32) * jax.nn.softplus(-lam.astype(f32))
    a = jnp.exp(log_a)
    b = jnp.sqrt(-jnp.expm1(2.0 * log_a)) * (i * xc).astype(f32)
    _, h = lax.associative_scan(_linear_combine, (a, b), axis=1)
    return h.astype(x.dtype) * jax.nn.gelu(gate)


def forgetting_attention(q, k, v, log_f):
    bsz, seq, n_heads, hd = q.shape
    c = jnp.cumsum(log_f, axis=1).transpose(0, 2, 1)
    scale = hd ** -0.5
    neg = jnp.finfo(jnp.float32).min
    out_blocks = []
    for blk in range(seq // ATTN_BLOCK):
        q0, q1 = blk * ATTN_BLOCK, (blk + 1) * ATTN_BLOCK
        s = jnp.einsum('bqhd,bkhd->bhqk', q[:, q0:q1], k[:, :q1],
                       preferred_element_type=jnp.float32) * scale
        s = s + c[:, :, q0:q1, None] - c[:, :, None, :q1]
        q_pos = jnp.arange(q0, q1)[:, None]
        k_pos = jnp.arange(q1)[None, :]
        s = jnp.where(k_pos <= q_pos, s, neg)
        p = jax.nn.softmax(s, axis=-1).astype(v.dtype)
        out_blocks.append(jnp.einsum('bhqk,bkhd->bqhd', p, v[:, :q1]))
    return jnp.concatenate(out_blocks, axis=1)


def fox_mixer(q, k, v, f_logit, b_f):
    bsz, seq, _ = q.shape
    shp = (bsz, seq, FOX_HEADS, FOX_HEAD_DIM)
    log_f = jax.nn.log_sigmoid((f_logit + b_f).astype(jnp.float32))
    o = forgetting_attention(q.reshape(shp), k.reshape(shp), v.reshape(shp), log_f)
    return o.reshape(bsz, seq, MIXER_WIDTH)


def _fwd_setup_inputs(seed: int = 0) -> dict:
    key = jax.random.key(seed)
    ks = iter(jax.random.split(key, 40))
    f32 = jnp.float32

    def nrm(shape, scale):
        return jax.random.normal(next(ks), shape, f32) * scale

    def gain(shape):
        return 1.0 + 0.02 * jax.random.normal(next(ks), shape, f32)

    L = DEPTH
    W = MIXER_WIDTH
    hd_lru = W // LRU_HEADS
    x = jax.random.normal(next(ks), (BATCH, SEQ, D_MODEL), f32)
    norm1_g = gain((L, D_MODEL))
    w_in = nrm((L, D_MODEL, D_IN_PROJ), D_MODEL ** -0.5)
    sgu_norm_g = gain((L, W))
    sgu_w = nrm((L, SGU_HEADS, SGU_CHUNK, SGU_CHUNK), SGU_CHUNK ** -0.5)
    sgu_b = 1.0 + 0.1 * jax.random.normal(next(ks), (L, SGU_HEADS, SGU_CHUNK), f32)
    s5_lambda_re = -0.5 + 0.01 * jax.random.normal(next(ks), (L, S5_GROUPS, S5_STATE), f32)
    s5_lambda_im = (jnp.pi * jnp.arange(S5_STATE, dtype=f32))[None, None, :] \
        + 0.01 * jax.random.normal(next(ks), (L, S5_GROUPS, S5_STATE), f32)
    s5_log_dt = jax.random.uniform(next(ks), (L, S5_GROUPS), f32,
                                   minval=math.log(S5_DT_MIN), maxval=math.log(S5_DT_MAX))
    s5_b_re = nrm((L, S5_GROUPS, S5_STATE, S5_GROUP), (2.0 * S5_GROUP) ** -0.5)
    s5_b_im = nrm((L, S5_GROUPS, S5_STATE, S5_GROUP), (2.0 * S5_GROUP) ** -0.5)
    s5_c_re = nrm((L, S5_GROUPS, S5_GROUP, S5_STATE), (2.0 * S5_STATE) ** -0.5)
    s5_c_im = nrm((L, S5_GROUPS, S5_GROUP, S5_STATE), (2.0 * S5_STATE) ** -0.5)
    s5_d = nrm((L, W), 0.5)
    s5_glu_w = nrm((L, W, W), W ** -0.5)
    s5_glu_b = nrm((L, W), 0.01)
    lru_conv_w = nrm((L, LRU_CONV, W), LRU_CONV ** -0.5)
    lru_conv_b = nrm((L, W), 0.01)
    lru_wa = nrm((L, LRU_HEADS, hd_lru, hd_lru), hd_lru ** -0.5)
    lru_ba = nrm((L, LRU_HEADS, hd_lru), 0.01)
    lru_wx = nrm((L, LRU_HEADS, hd_lru, hd_lru), hd_lru ** -0.5)
    lru_bx = nrm((L, LRU_HEADS, hd_lru), 0.01)
    a_pow = jax.random.uniform(next(ks), (L, W), f32, minval=0.9, maxval=0.999)
    sig = a_pow ** (1.0 / LRU_C)
    lru_lambda = jnp.log(sig) - jnp.log1p(-sig)
    fox_fgate_b = 2.0 + 0.1 * jax.random.normal(next(ks), (L, FOX_HEADS), f32)
    mix_norm_g = gain((L, D_MIX))
    w_out = nrm((L, D_MIX, D_MODEL), D_MIX ** -0.5)
    norm2_g = gain((L, D_MODEL))
    w_mlp_in = nrm((L, D_MODEL, D_FF), D_MODEL ** -0.5)
    w_mlp_out = nrm((L, D_FF, D_MODEL), D_FF ** -0.5)
    final_g = gain((D_MODEL,))
    return {
        "x": x, "norm1_g": norm1_g, "w_in": w_in,
        "sgu_norm_g": sgu_norm_g, "sgu_w": sgu_w, "sgu_b": sgu_b,
        "s5_lambda_re": s5_lambda_re, "s5_lambda_im": s5_lambda_im, "s5_log_dt": s5_log_dt,
        "s5_b_re": s5_b_re, "s5_b_im": s5_b_im, "s5_c_re": s5_c_re, "s5_c_im": s5_c_im,
        "s5_d": s5_d, "s5_glu_w": s5_glu_w, "s5_glu_b": s5_glu_b,
        "lru_conv_w": lru_conv_w, "lru_conv_b": lru_conv_b, "lru_wa": lru_wa, "lru_ba": lru_ba,
        "lru_wx": lru_wx, "lru_bx": lru_bx, "lru_lambda": lru_lambda,
        "fox_fgate_b": fox_fgate_b,
        "mix_norm_g": mix_norm_g, "w_out": w_out, "norm2_g": norm2_g,
        "w_mlp_in": w_mlp_in, "w_mlp_out": w_mlp_out, "final_g": final_g,
    }


def _fwd_reference(x, norm1_g, w_in, sgu_norm_g, sgu_w, sgu_b,
              s5_lambda_re, s5_lambda_im, s5_log_dt, s5_b_re, s5_b_im, s5_c_re, s5_c_im,
              s5_d, s5_glu_w, s5_glu_b,
              lru_conv_w, lru_conv_b, lru_wa, lru_ba, lru_wx, lru_bx, lru_lambda,
              fox_fgate_b, mix_norm_g, w_out, norm2_g, w_mlp_in, w_mlp_out, final_g):
    split_points = [MIXER_WIDTH * i for i in range(1, 9)]
    for l in range(DEPTH):
        h = rms_norm(x, norm1_g[l])
        z = h @ w_in[l]
        a_u, a_v, b_in, c_x, c_gate, d_q, d_k, d_v, d_f = jnp.split(z, split_points, axis=-1)
        y_a = sgu_mixer(a_u, a_v, sgu_norm_g[l], sgu_w[l], sgu_b[l])
        y_b = s5_mixer(b_in, s5_lambda_re[l], s5_lambda_im[l], s5_log_dt[l],
                       s5_b_re[l], s5_b_im[l], s5_c_re[l], s5_c_im[l],
                       s5_d[l], s5_glu_w[l], s5_glu_b[l])
        y_c = rglru_mixer(c_x, c_gate, lru_conv_w[l], lru_conv_b[l],
                          lru_wa[l], lru_ba[l], lru_wx[l], lru_bx[l], lru_lambda[l])
        y_d = fox_mixer(d_q, d_k, d_v, d_f, fox_fgate_b[l])
        y = jnp.concatenate([y_a, y_b, y_c, y_d], axis=-1)
        y = group_rms_norm(y, mix_norm_g[l])
        x = x + y @ w_out[l]
        h = rms_norm(x, norm2_g[l])
        x = x + jnp.square(jax.nn.relu(h @ w_mlp_in[l])) @ w_mlp_out[l]
    return rms_norm(x, final_g)


import jax as _jax
import jax.numpy as _jnp

TWIN_FORMAT = 'train_step'
FWD_PARAMS = ['x', 'norm1_g', 'w_in', 'sgu_norm_g', 'sgu_w', 'sgu_b', 's5_lambda_re', 's5_lambda_im', 's5_log_dt', 's5_b_re', 's5_b_im', 's5_c_re', 's5_c_im', 's5_d', 's5_glu_w', 's5_glu_b', 'lru_conv_w', 'lru_conv_b', 'lru_wa', 'lru_ba', 'lru_wx', 'lru_bx', 'lru_lambda', 'fox_fgate_b', 'mix_norm_g', 'w_out', 'norm2_g', 'w_mlp_in', 'w_mlp_out', 'final_g']
TWIN_WEIGHTS = ['norm1_g', 'w_in', 'sgu_norm_g', 'sgu_w', 'sgu_b', 's5_lambda_re', 's5_lambda_im', 's5_log_dt', 's5_b_re', 's5_b_im', 's5_c_re', 's5_c_im', 's5_d', 's5_glu_w', 's5_glu_b', 'lru_conv_w', 'lru_conv_b', 'lru_wa', 'lru_ba', 'lru_wx', 'lru_bx', 'lru_lambda', 'fox_fgate_b', 'mix_norm_g', 'w_out', 'norm2_g', 'w_mlp_in', 'w_mlp_out', 'final_g']
TWIN_DIFF_INPUT = 'x'
TWIN_INPUTS = ['x', 'norm1_g', 'w_in', 'sgu_norm_g', 'sgu_w', 'sgu_b', 's5_lambda_re', 's5_lambda_im', 's5_log_dt', 's5_b_re', 's5_b_im', 's5_c_re', 's5_c_im', 's5_d', 's5_glu_w', 's5_glu_b', 'lru_conv_w', 'lru_conv_b', 'lru_wa', 'lru_ba', 'lru_wx', 'lru_bx', 'lru_lambda', 'fox_fgate_b', 'mix_norm_g', 'w_out', 'norm2_g', 'w_mlp_in', 'w_mlp_out', 'final_g', 'loss_target', 'm_norm1_g', 'm_w_in', 'm_sgu_norm_g', 'm_sgu_w', 'm_sgu_b', 'm_s5_lambda_re', 'm_s5_lambda_im', 'm_s5_log_dt', 'm_s5_b_re', 'm_s5_b_im', 'm_s5_c_re', 'm_s5_c_im', 'm_s5_d', 'm_s5_glu_w', 'm_s5_glu_b', 'm_lru_conv_w', 'm_lru_conv_b', 'm_lru_wa', 'm_lru_ba', 'm_lru_wx', 'm_lru_bx', 'm_lru_lambda', 'm_fox_fgate_b', 'm_mix_norm_g', 'm_w_out', 'm_norm2_g', 'm_w_mlp_in', 'm_w_mlp_out', 'm_final_g', 'v_norm1_g', 'v_w_in', 'v_sgu_norm_g', 'v_sgu_w', 'v_sgu_b', 'v_s5_lambda_re', 'v_s5_lambda_im', 'v_s5_log_dt', 'v_s5_b_re', 'v_s5_b_im', 'v_s5_c_re', 'v_s5_c_im', 'v_s5_d', 'v_s5_glu_w', 'v_s5_glu_b', 'v_lru_conv_w', 'v_lru_conv_b', 'v_lru_wa', 'v_lru_ba', 'v_lru_wx', 'v_lru_bx', 'v_lru_lambda', 'v_fox_fgate_b', 'v_mix_norm_g', 'v_w_out', 'v_norm2_g', 'v_w_mlp_in', 'v_w_mlp_out', 'v_final_g']
TWIN_OUTPUTS = ['loss', 'grad_x', 'grad_norm1_g', 'grad_w_in', 'grad_sgu_norm_g', 'grad_sgu_w', 'grad_sgu_b', 'grad_s5_lambda_re', 'grad_s5_lambda_im', 'grad_s5_log_dt', 'grad_s5_b_re', 'grad_s5_b_im', 'grad_s5_c_re', 'grad_s5_c_im', 'grad_s5_d', 'grad_s5_glu_w', 'grad_s5_glu_b', 'grad_lru_conv_w', 'grad_lru_conv_b', 'grad_lru_wa', 'grad_lru_ba', 'grad_lru_wx', 'grad_lru_bx', 'grad_lru_lambda', 'grad_fox_fgate_b', 'grad_mix_norm_g', 'grad_w_out', 'grad_norm2_g', 'grad_w_mlp_in', 'grad_w_mlp_out', 'grad_final_g', 'delta_norm1_g', 'delta_w_in', 'delta_sgu_norm_g', 'delta_sgu_w', 'delta_sgu_b', 'delta_s5_lambda_re', 'delta_s5_lambda_im', 'delta_s5_log_dt', 'delta_s5_b_re', 'delta_s5_b_im', 'delta_s5_c_re', 'delta_s5_c_im', 'delta_s5_d', 'delta_s5_glu_w', 'delta_s5_glu_b', 'delta_lru_conv_w', 'delta_lru_conv_b', 'delta_lru_wa', 'delta_lru_ba', 'delta_lru_wx', 'delta_lru_bx', 'delta_lru_lambda', 'delta_fox_fgate_b', 'delta_mix_norm_g', 'delta_w_out', 'delta_norm2_g', 'delta_w_mlp_in', 'delta_w_mlp_out', 'delta_final_g', 'new_m_norm1_g', 'new_m_w_in', 'new_m_sgu_norm_g', 'new_m_sgu_w', 'new_m_sgu_b', 'new_m_s5_lambda_re', 'new_m_s5_lambda_im', 'new_m_s5_log_dt', 'new_m_s5_b_re', 'new_m_s5_b_im', 'new_m_s5_c_re', 'new_m_s5_c_im', 'new_m_s5_d', 'new_m_s5_glu_w', 'new_m_s5_glu_b', 'new_m_lru_conv_w', 'new_m_lru_conv_b', 'new_m_lru_wa', 'new_m_lru_ba', 'new_m_lru_wx', 'new_m_lru_bx', 'new_m_lru_lambda', 'new_m_fox_fgate_b', 'new_m_mix_norm_g', 'new_m_w_out', 'new_m_norm2_g', 'new_m_w_mlp_in', 'new_m_w_mlp_out', 'new_m_final_g', 'new_v_norm1_g', 'new_v_w_in', 'new_v_sgu_norm_g', 'new_v_sgu_w', 'new_v_sgu_b', 'new_v_s5_lambda_re', 'new_v_s5_lambda_im', 'new_v_s5_log_dt', 'new_v_s5_b_re', 'new_v_s5_b_im', 'new_v_s5_c_re', 'new_v_s5_c_im', 'new_v_s5_d', 'new_v_s5_glu_w', 'new_v_s5_glu_b', 'new_v_lru_conv_w', 'new_v_lru_conv_b', 'new_v_lru_wa', 'new_v_lru_ba', 'new_v_lru_wx', 'new_v_lru_bx', 'new_v_lru_lambda', 'new_v_fox_fgate_b', 'new_v_mix_norm_g', 'new_v_w_out', 'new_v_norm2_g', 'new_v_w_mlp_in', 'new_v_w_mlp_out', 'new_v_final_g']
TWIN_LEAF_KINDS = {'loss': 'loss', 'grad_x': 'grad_x', 'grad_norm1_g': 'grad_w', 'grad_w_in': 'grad_w', 'grad_sgu_norm_g': 'grad_w', 'grad_sgu_w': 'grad_w', 'grad_sgu_b': 'grad_w', 'grad_s5_lambda_re': 'grad_w', 'grad_s5_lambda_im': 'grad_w', 'grad_s5_log_dt': 'grad_w', 'grad_s5_b_re': 'grad_w', 'grad_s5_b_im': 'grad_w', 'grad_s5_c_re': 'grad_w', 'grad_s5_c_im': 'grad_w', 'grad_s5_d': 'grad_w', 'grad_s5_glu_w': 'grad_w', 'grad_s5_glu_b': 'grad_w', 'grad_lru_conv_w': 'grad_w', 'grad_lru_conv_b': 'grad_w', 'grad_lru_wa': 'grad_w', 'grad_lru_ba': 'grad_w', 'grad_lru_wx': 'grad_w', 'grad_lru_bx': 'grad_w', 'grad_lru_lambda': 'grad_w', 'grad_fox_fgate_b': 'grad_w', 'grad_mix_norm_g': 'grad_w', 'grad_w_out': 'grad_w', 'grad_norm2_g': 'grad_w', 'grad_w_mlp_in': 'grad_w', 'grad_w_mlp_out': 'grad_w', 'grad_final_g': 'grad_w', 'delta_norm1_g': 'delta_w', 'delta_w_in': 'delta_w', 'delta_sgu_norm_g': 'delta_w', 'delta_sgu_w': 'delta_w', 'delta_sgu_b': 'delta_w', 'delta_s5_lambda_re': 'delta_w', 'delta_s5_lambda_im': 'delta_w', 'delta_s5_log_dt': 'delta_w', 'delta_s5_b_re': 'delta_w', 'delta_s5_b_im': 'delta_w', 'delta_s5_c_re': 'delta_w', 'delta_s5_c_im': 'delta_w', 'delta_s5_d': 'delta_w', 'delta_s5_glu_w': 'delta_w', 'delta_s5_glu_b': 'delta_w', 'delta_lru_conv_w': 'delta_w', 'delta_lru_conv_b': 'delta_w', 'delta_lru_wa': 'delta_w', 'delta_lru_ba': 'delta_w', 'delta_lru_wx': 'delta_w', 'delta_lru_bx': 'delta_w', 'delta_lru_lambda': 'delta_w', 'delta_fox_fgate_b': 'delta_w', 'delta_mix_norm_g': 'delta_w', 'delta_w_out': 'delta_w', 'delta_norm2_g': 'delta_w', 'delta_w_mlp_in': 'delta_w', 'delta_w_mlp_out': 'delta_w', 'delta_final_g': 'delta_w', 'new_m_norm1_g': 'new_m', 'new_m_w_in': 'new_m', 'new_m_sgu_norm_g': 'new_m', 'new_m_sgu_w': 'new_m', 'new_m_sgu_b': 'new_m', 'new_m_s5_lambda_re': 'new_m', 'new_m_s5_lambda_im': 'new_m', 'new_m_s5_log_dt': 'new_m', 'new_m_s5_b_re': 'new_m', 'new_m_s5_b_im': 'new_m', 'new_m_s5_c_re': 'new_m', 'new_m_s5_c_im': 'new_m', 'new_m_s5_d': 'new_m', 'new_m_s5_glu_w': 'new_m', 'new_m_s5_glu_b': 'new_m', 'new_m_lru_conv_w': 'new_m', 'new_m_lru_conv_b': 'new_m', 'new_m_lru_wa': 'new_m', 'new_m_lru_ba': 'new_m', 'new_m_lru_wx': 'new_m', 'new_m_lru_bx': 'new_m', 'new_m_lru_lambda': 'new_m', 'new_m_fox_fgate_b': 'new_m', 'new_m_mix_norm_g': 'new_m', 'new_m_w_out': 'new_m', 'new_m_norm2_g': 'new_m', 'new_m_w_mlp_in': 'new_m', 'new_m_w_mlp_out': 'new_m', 'new_m_final_g': 'new_m', 'new_v_norm1_g': 'new_v', 'new_v_w_in': 'new_v', 'new_v_sgu_norm_g': 'new_v', 'new_v_sgu_w': 'new_v', 'new_v_sgu_b': 'new_v', 'new_v_s5_lambda_re': 'new_v', 'new_v_s5_lambda_im': 'new_v', 'new_v_s5_log_dt': 'new_v', 'new_v_s5_b_re': 'new_v', 'new_v_s5_b_im': 'new_v', 'new_v_s5_c_re': 'new_v', 'new_v_s5_c_im': 'new_v', 'new_v_s5_d': 'new_v', 'new_v_s5_glu_w': 'new_v', 'new_v_s5_glu_b': 'new_v', 'new_v_lru_conv_w': 'new_v', 'new_v_lru_conv_b': 'new_v', 'new_v_lru_wa': 'new_v', 'new_v_lru_ba': 'new_v', 'new_v_lru_wx': 'new_v', 'new_v_lru_bx': 'new_v', 'new_v_lru_lambda': 'new_v', 'new_v_fox_fgate_b': 'new_v', 'new_v_mix_norm_g': 'new_v', 'new_v_w_out': 'new_v', 'new_v_norm2_g': 'new_v', 'new_v_w_mlp_in': 'new_v', 'new_v_w_mlp_out': 'new_v', 'new_v_final_g': 'new_v'}


def _forward(args):
    return _fwd_reference(*[args[k] for k in FWD_PARAMS])


def _output_shape():
    def fwd():
        inp = _fwd_setup_inputs(0)
        return _fwd_reference(*[inp[k] for k in FWD_PARAMS])
    out = _jax.eval_shape(fwd)
    return out.shape, out.dtype

N_MICROBATCH = 1
ADAM_LR = 0.001
ADAM_B1 = 0.9
ADAM_B2 = 0.999
ADAM_EPS = 1e-08
ADAM_WD = 0.01
ADAM_STEP = 10
PER_EXAMPLE_BATCH_AXIS = {'x': 0, 'loss_target': 0}
SHARED_INPUTS = []
_WEIGHT_DTYPES = {'norm1_g': _jnp.float32, 'w_in': _jnp.float32, 'sgu_norm_g': _jnp.float32, 'sgu_w': _jnp.float32, 'sgu_b': _jnp.float32, 's5_lambda_re': _jnp.float32, 's5_lambda_im': _jnp.float32, 's5_log_dt': _jnp.float32, 's5_b_re': _jnp.float32, 's5_b_im': _jnp.float32, 's5_c_re': _jnp.float32, 's5_c_im': _jnp.float32, 's5_d': _jnp.float32, 's5_glu_w': _jnp.float32, 's5_glu_b': _jnp.float32, 'lru_conv_w': _jnp.float32, 'lru_conv_b': _jnp.float32, 'lru_wa': _jnp.float32, 'lru_ba': _jnp.float32, 'lru_wx': _jnp.float32, 'lru_bx': _jnp.float32, 'lru_lambda': _jnp.float32, 'fox_fgate_b': _jnp.float32, 'mix_norm_g': _jnp.float32, 'w_out': _jnp.float32, 'norm2_g': _jnp.float32, 'w_mlp_in': _jnp.float32, 'w_mlp_out': _jnp.float32, 'final_g': _jnp.float32}
MOMENT_SCALE = {'norm1_g': 1.460084e-01, 'w_in': 1.004432e-01, 'sgu_norm_g': 5.655784e-02, 'sgu_w': 3.963751e-02, 'sgu_b': 5.542446e-02, 's5_lambda_re': 1.502048e-02, 's5_lambda_im': 2.008282e-02, 's5_log_dt': 8.457063e+00, 's5_b_re': 9.572202e-03, 's5_b_im': 1.224114e-02, 's5_c_re': 1.888542e-02, 's5_c_im': 1.992346e-02, 's5_d': 2.863618e-01, 's5_glu_w': 1.568743e-02, 's5_glu_b': 6.143703e-02, 'lru_conv_w': 1.211501e-01, 'lru_conv_b': 9.593831e-01, 'lru_wa': 3.160917e-02, 'lru_ba': 2.388206e-02, 'lru_wx': 6.033629e-02, 'lru_bx': 3.998908e-02, 'lru_lambda': 5.526045e-02, 'fox_fgate_b': 2.967510e-01, 'mix_norm_g': 1.314332e-01, 'w_out': 1.306563e-01, 'norm2_g': 1.233800e-01, 'w_mlp_in': 6.094399e-02, 'w_mlp_out': 1.396129e-01, 'final_g': 3.358264e+01}


def _to_microbatches(a, axis):
    t = _jnp.moveaxis(a, axis, 0)
    t = t.reshape((N_MICROBATCH, t.shape[0] // N_MICROBATCH) + t.shape[1:])
    return _jnp.moveaxis(t, 1, axis + 1)


def setup_inputs(seed: int = 0) -> dict:
    inp = _fwd_setup_inputs(seed)
    key = _jax.random.fold_in(_jax.random.key(seed), 7919)
    shape, _ = _output_shape()
    out = dict(inp)
    out["loss_target"] = _jax.random.normal(_jax.random.fold_in(key, 0), shape, _jnp.float32)
    for i, name in enumerate(TWIN_WEIGHTS):
        w = inp[name].astype(_jnp.float32)
        if MOMENT_SCALE is None:
            s = _jnp.sqrt(_jnp.mean(_jnp.square(w)) + 1e-30)
        else:
            s = MOMENT_SCALE[name]
        km, kv = _jax.random.split(_jax.random.fold_in(key, i + 1))
        out[name] = w
        out["m_" + name] = s * _jax.random.normal(km, w.shape, _jnp.float32)
        out["v_" + name] = (s * s) * _jax.random.uniform(kv, w.shape, _jnp.float32, 0.5, 1.5)
    if N_MICROBATCH > 1:
        for name, axis in PER_EXAMPLE_BATCH_AXIS.items():
            out[name] = _to_microbatches(out[name], axis)
    return {'x': out['x'], 'norm1_g': out['norm1_g'], 'w_in': out['w_in'], 'sgu_norm_g': out['sgu_norm_g'], 'sgu_w': out['sgu_w'], 'sgu_b': out['sgu_b'], 's5_lambda_re': out['s5_lambda_re'], 's5_lambda_im': out['s5_lambda_im'], 's5_log_dt': out['s5_log_dt'], 's5_b_re': out['s5_b_re'], 's5_b_im': out['s5_b_im'], 's5_c_re': out['s5_c_re'], 's5_c_im': out['s5_c_im'], 's5_d': out['s5_d'], 's5_glu_w': out['s5_glu_w'], 's5_glu_b': out['s5_glu_b'], 'lru_conv_w': out['lru_conv_w'], 'lru_conv_b': out['lru_conv_b'], 'lru_wa': out['lru_wa'], 'lru_ba': out['lru_ba'], 'lru_wx': out['lru_wx'], 'lru_bx': out['lru_bx'], 'lru_lambda': out['lru_lambda'], 'fox_fgate_b': out['fox_fgate_b'], 'mix_norm_g': out['mix_norm_g'], 'w_out': out['w_out'], 'norm2_g': out['norm2_g'], 'w_mlp_in': out['w_mlp_in'], 'w_mlp_out': out['w_mlp_out'], 'final_g': out['final_g'], 'loss_target': out['loss_target'], 'm_norm1_g': out['m_norm1_g'], 'm_w_in': out['m_w_in'], 'm_sgu_norm_g': out['m_sgu_norm_g'], 'm_sgu_w': out['m_sgu_w'], 'm_sgu_b': out['m_sgu_b'], 'm_s5_lambda_re': out['m_s5_lambda_re'], 'm_s5_lambda_im': out['m_s5_lambda_im'], 'm_s5_log_dt': out['m_s5_log_dt'], 'm_s5_b_re': out['m_s5_b_re'], 'm_s5_b_im': out['m_s5_b_im'], 'm_s5_c_re': out['m_s5_c_re'], 'm_s5_c_im': out['m_s5_c_im'], 'm_s5_d': out['m_s5_d'], 'm_s5_glu_w': out['m_s5_glu_w'], 'm_s5_glu_b': out['m_s5_glu_b'], 'm_lru_conv_w': out['m_lru_conv_w'], 'm_lru_conv_b': out['m_lru_conv_b'], 'm_lru_wa': out['m_lru_wa'], 'm_lru_ba': out['m_lru_ba'], 'm_lru_wx': out['m_lru_wx'], 'm_lru_bx': out['m_lru_bx'], 'm_lru_lambda': out['m_lru_lambda'], 'm_fox_fgate_b': out['m_fox_fgate_b'], 'm_mix_norm_g': out['m_mix_norm_g'], 'm_w_out': out['m_w_out'], 'm_norm2_g': out['m_norm2_g'], 'm_w_mlp_in': out['m_w_mlp_in'], 'm_w_mlp_out': out['m_w_mlp_out'], 'm_final_g': out['m_final_g'], 'v_norm1_g': out['v_norm1_g'], 'v_w_in': out['v_w_in'], 'v_sgu_norm_g': out['v_sgu_norm_g'], 'v_sgu_w': out['v_sgu_w'], 'v_sgu_b': out['v_sgu_b'], 'v_s5_lambda_re': out['v_s5_lambda_re'], 'v_s5_lambda_im': out['v_s5_lambda_im'], 'v_s5_log_dt': out['v_s5_log_dt'], 'v_s5_b_re': out['v_s5_b_re'], 'v_s5_b_im': out['v_s5_b_im'], 'v_s5_c_re': out['v_s5_c_re'], 'v_s5_c_im': out['v_s5_c_im'], 'v_s5_d': out['v_s5_d'], 'v_s5_glu_w': out['v_s5_glu_w'], 'v_s5_glu_b': out['v_s5_glu_b'], 'v_lru_conv_w': out['v_lru_conv_w'], 'v_lru_conv_b': out['v_lru_conv_b'], 'v_lru_wa': out['v_lru_wa'], 'v_lru_ba': out['v_lru_ba'], 'v_lru_wx': out['v_lru_wx'], 'v_lru_bx': out['v_lru_bx'], 'v_lru_lambda': out['v_lru_lambda'], 'v_fox_fgate_b': out['v_fox_fgate_b'], 'v_mix_norm_g': out['v_mix_norm_g'], 'v_w_out': out['v_w_out'], 'v_norm2_g': out['v_norm2_g'], 'v_w_mlp_in': out['v_w_mlp_in'], 'v_w_mlp_out': out['v_w_mlp_out'], 'v_final_g': out['v_final_g']}


def _loss(weights, diff, rest, loss_target):
    with _jax.named_scope("forward"):
        args = {**rest, TWIN_DIFF_INPUT: diff, **{k: w.astype(_WEIGHT_DTYPES[k]) for k, w in weights.items()}}
        y = _forward(args)
    with _jax.named_scope("loss_head"):
        err = _jnp.square(y.astype(_jnp.float32) - loss_target)
        return 0.5 * _jnp.sum(_jnp.mean(err, axis=-1)) if err.ndim else 0.5 * err


def _adamw(w, g, m, v):
    m = ADAM_B1 * m + (1.0 - ADAM_B1) * g
    v = ADAM_B2 * v + (1.0 - ADAM_B2) * _jnp.square(g)
    m_hat = m / (1.0 - ADAM_B1 ** ADAM_STEP)
    v_hat = v / (1.0 - ADAM_B2 ** ADAM_STEP)
    delta = -ADAM_LR * (m_hat / (_jnp.sqrt(v_hat) + ADAM_EPS) + ADAM_WD * w)
    return delta, m, v


def reference(x, norm1_g, w_in, sgu_norm_g, sgu_w, sgu_b, s5_lambda_re, s5_lambda_im, s5_log_dt, s5_b_re, s5_b_im, s5_c_re, s5_c_im, s5_d, s5_glu_w, s5_glu_b, lru_conv_w, lru_conv_b, lru_wa, lru_ba, lru_wx, lru_bx, lru_lambda, fox_fgate_b, mix_norm_g, w_out, norm2_g, w_mlp_in, w_mlp_out, final_g, loss_target, m_norm1_g, m_w_in, m_sgu_norm_g, m_sgu_w, m_sgu_b, m_s5_lambda_re, m_s5_lambda_im, m_s5_log_dt, m_s5_b_re, m_s5_b_im, m_s5_c_re, m_s5_c_im, m_s5_d, m_s5_glu_w, m_s5_glu_b, m_lru_conv_w, m_lru_conv_b, m_lru_wa, m_lru_ba, m_lru_wx, m_lru_bx, m_lru_lambda, m_fox_fgate_b, m_mix_norm_g, m_w_out, m_norm2_g, m_w_mlp_in, m_w_mlp_out, m_final_g, v_norm1_g, v_w_in, v_sgu_norm_g, v_sgu_w, v_sgu_b, v_s5_lambda_re, v_s5_lambda_im, v_s5_log_dt, v_s5_b_re, v_s5_b_im, v_s5_c_re, v_s5_c_im, v_s5_d, v_s5_glu_w, v_s5_glu_b, v_lru_conv_w, v_lru_conv_b, v_lru_wa, v_lru_ba, v_lru_wx, v_lru_bx, v_lru_lambda, v_fox_fgate_b, v_mix_norm_g, v_w_out, v_norm2_g, v_w_mlp_in, v_w_mlp_out, v_final_g):
    given = dict(x=x, norm1_g=norm1_g, w_in=w_in, sgu_norm_g=sgu_norm_g, sgu_w=sgu_w, sgu_b=sgu_b, s5_lambda_re=s5_lambda_re, s5_lambda_im=s5_lambda_im, s5_log_dt=s5_log_dt, s5_b_re=s5_b_re, s5_b_im=s5_b_im, s5_c_re=s5_c_re, s5_c_im=s5_c_im, s5_d=s5_d, s5_glu_w=s5_glu_w, s5_glu_b=s5_glu_b, lru_conv_w=lru_conv_w, lru_conv_b=lru_conv_b, lru_wa=lru_wa, lru_ba=lru_ba, lru_wx=lru_wx, lru_bx=lru_bx, lru_lambda=lru_lambda, fox_fgate_b=fox_fgate_b, mix_norm_g=mix_norm_g, w_out=w_out, norm2_g=norm2_g, w_mlp_in=w_mlp_in, w_mlp_out=w_mlp_out, final_g=final_g, loss_target=loss_target, m_norm1_g=m_norm1_g, m_w_in=m_w_in, m_sgu_norm_g=m_sgu_norm_g, m_sgu_w=m_sgu_w, m_sgu_b=m_sgu_b, m_s5_lambda_re=m_s5_lambda_re, m_s5_lambda_im=m_s5_lambda_im, m_s5_log_dt=m_s5_log_dt, m_s5_b_re=m_s5_b_re, m_s5_b_im=m_s5_b_im, m_s5_c_re=m_s5_c_re, m_s5_c_im=m_s5_c_im, m_s5_d=m_s5_d, m_s5_glu_w=m_s5_glu_w, m_s5_glu_b=m_s5_glu_b, m_lru_conv_w=m_lru_conv_w, m_lru_conv_b=m_lru_conv_b, m_lru_wa=m_lru_wa, m_lru_ba=m_lru_ba, m_lru_wx=m_lru_wx, m_lru_bx=m_lru_bx, m_lru_lambda=m_lru_lambda, m_fox_fgate_b=m_fox_fgate_b, m_mix_norm_g=m_mix_norm_g, m_w_out=m_w_out, m_norm2_g=m_norm2_g, m_w_mlp_in=m_w_mlp_in, m_w_mlp_out=m_w_mlp_out, m_final_g=m_final_g, v_norm1_g=v_norm1_g, v_w_in=v_w_in, v_sgu_norm_g=v_sgu_norm_g, v_sgu_w=v_sgu_w, v_sgu_b=v_sgu_b, v_s5_lambda_re=v_s5_lambda_re, v_s5_lambda_im=v_s5_lambda_im, v_s5_log_dt=v_s5_log_dt, v_s5_b_re=v_s5_b_re, v_s5_b_im=v_s5_b_im, v_s5_c_re=v_s5_c_re, v_s5_c_im=v_s5_c_im, v_s5_d=v_s5_d, v_s5_glu_w=v_s5_glu_w, v_s5_glu_b=v_s5_glu_b, v_lru_conv_w=v_lru_conv_w, v_lru_conv_b=v_lru_conv_b, v_lru_wa=v_lru_wa, v_lru_ba=v_lru_ba, v_lru_wx=v_lru_wx, v_lru_bx=v_lru_bx, v_lru_lambda=v_lru_lambda, v_fox_fgate_b=v_fox_fgate_b, v_mix_norm_g=v_mix_norm_g, v_w_out=v_w_out, v_norm2_g=v_norm2_g, v_w_mlp_in=v_w_mlp_in, v_w_mlp_out=v_w_mlp_out, v_final_g=v_final_g)
    weights = {n: given[n] for n in TWIN_WEIGHTS}
    shared = {n: given[n] for n in SHARED_INPUTS}
    per_example = {n: given[n] for n in ['x']}
    grad_fn = _jax.value_and_grad(_loss, argnums=(0, 1))

    def one_microbatch(ex, loss_target):
        ex = dict(ex)
        diff = ex.pop(TWIN_DIFF_INPUT)
        return grad_fn(weights, diff, {**shared, **ex}, loss_target)

    if N_MICROBATCH == 1:
        loss, (grad_w, grad_x) = one_microbatch(per_example, given["loss_target"])
    else:
        def body(carry, xs):
            loss_sum, grad_sum = carry
            l_k, (gw_k, gx_k) = one_microbatch(xs[0], xs[1])
            with _jax.named_scope("update"):
                return (loss_sum + l_k, _jax.tree.map(_jnp.add, grad_sum, gw_k)), gx_k

        init = (_jnp.zeros((), _jnp.float32), _jax.tree.map(_jnp.zeros_like, weights))
        (loss, grad_w), grad_x = _jax.lax.scan(body, init, (per_example, given["loss_target"]))
    with _jax.named_scope("update"):
        delta_w, new_m, new_v = {}, {}, {}
        for n in TWIN_WEIGHTS:
            delta_w[n], new_m[n], new_v[n] = _adamw(weights[n], grad_w[n], given["m_" + n], given["v_" + n])
    return (loss, grad_x, *[grad_w[n] for n in TWIN_WEIGHTS], *[delta_w[n] for n in TWIN_WEIGHTS],
            *[new_m[n] for n in TWIN_WEIGHTS], *[new_v[n] for n in TWIN_WEIGHTS])
```

```python
import functools
import math

import jax
import jax.numpy as jnp
from jax import lax
from jax.experimental import pallas as pl
from jax.experimental.pallas import tpu as pltpu

F32 = jnp.float32
BF16 = jnp.bfloat16

DEPTH = 4
D_MODEL = 1024
MIXER_WIDTH = 256
SGU_CHUNK = 128
N_HEADS = 4
HEAD_DIM = 64
S5_GROUPS = 16
S5_GROUP = 16
S5_STATE = 64
LRU_C = 8.0
RMS_EPS = 1e-6
D_IN_PROJ = 8 * MIXER_WIDTH + N_HEADS
D_IN_PAD = 8 * MIXER_WIDTH + 128
ADAM_LR, ADAM_B1, ADAM_B2, ADAM_EPS, ADAM_WD, ADAM_STEP = 0.001, 0.9, 0.999, 1e-08, 0.01, 10

V7X_VMEM_BYTES = 64 * 1024 * 1024
VMEM_LIMIT = V7X_VMEM_BYTES - 8 * 1024 * 1024
NEG = -1e30
MESH = pl.DeviceIdType.MESH


def _cp(sem=None, **kw):
    return pltpu.CompilerParams(dimension_semantics=sem, vmem_limit_bytes=VMEM_LIMIT, **kw)


def _full_spec(a):
    nd = a.ndim
    return pl.BlockSpec(a.shape, lambda *_: (0,) * nd)


def _tile(n, pref=512):
    return pref if n % pref == 0 else n


def _dot(a, b, ca, cb):
    return lax.dot_general(a.astype(BF16), b.astype(BF16), (((ca,), (cb,)), ((), ())),
                           preferred_element_type=F32)


def _mm_call(a, b, *, ta=False, tb=False, res=None, name):
    m, k = (a.shape[1], a.shape[0]) if ta else a.shape
    n = b.shape[0] if tb else b.shape[1]
    tm, tn, tk = _tile(m), _tile(n), _tile(k)
    if tn > 1024 or tk > 1024:
        tm = _tile(m, 256)
    nk = k // tk
    ca, cb = (0 if ta else 1), (1 if tb else 0)

    def body(*refs):
        a_ref, b_ref = refs[0], refs[1]
        o_ref, acc = refs[-2], refs[-1]
        kk = pl.program_id(2)

        @pl.when(kk == 0)
        def _():
            acc[...] = jnp.zeros_like(acc)

        acc[...] += _dot(a_ref[...], b_ref[...], ca, cb)

        @pl.when(kk == nk - 1)
        def _():
            out = acc[...]
            if res is not None:
                out = out + refs[2][...]
            o_ref[...] = out

    a_spec = pl.BlockSpec((tk, tm), lambda i, j, kk: (kk, i)) if ta else pl.BlockSpec((tm, tk), lambda i, j, kk: (i, kk))
    b_spec = pl.BlockSpec((tn, tk), lambda i, j, kk: (j, kk)) if tb else pl.BlockSpec((tk, tn), lambda i, j, kk: (kk, j))
    o_spec = pl.BlockSpec((tm, tn), lambda i, j, kk: (i, j))
    ins, specs = [a, b], [a_spec, b_spec]
    if res is not None:
        ins.append(res)
        specs.append(o_spec)
    return pl.pallas_call(
        body, name=name, grid=(m // tm, n // tn, nk), in_specs=specs, out_specs=o_spec,
        out_shape=jax.ShapeDtypeStruct((m, n), F32), scratch_shapes=[pltpu.VMEM((tm, tn), F32)],
        compiler_params=_cp(("parallel", "parallel", "arbitrary")))(*ins)


@jax.custom_vjp
def mm(a, b):
    return _mm_call(a, b, name="mm")


def _mm_fwd(a, b):
    return _mm_call(a, b, name="mm"), (a, b)


def _mm_bwd(r, g):
    a, b = r
    return _mm_call(g, b, tb=True, name="mm_da"), _mm_call(a, g, ta=True, name="mm_db")


mm.defvjp(_mm_fwd, _mm_bwd)


@jax.custom_vjp
def mm_res(a, b, r):
    return _mm_call(a, b, res=r, name="mm_res")


def _mm_res_fwd(a, b, r):
    return _mm_call(a, b, res=r, name="mm_res"), (a, b)


def _mm_res_bwd(rs, g):
    a, b = rs
    return _mm_call(g, b, tb=True, name="mm_da"), _mm_call(a, g, ta=True, name="mm_db"), g


mm_res.defvjp(_mm_res_fwd, _mm_res_bwd)


@jax.custom_vjp
def _bdot(a, b):
    return _dot(a, b, 1, 0)


def _bdot_fwd(a, b):
    return _dot(a, b, 1, 0), (a, b)


def _bdot_bwd(r, g):
    a, b = r
    return _dot(g, b, 1, 1), _dot(a, g, 0, 0)


_bdot.defvjp(_bdot_fwd, _bdot_bwd)


def _row_spec(tr, w):
    return pl.BlockSpec((tr, w), lambda i: (i, 0))


def _rowwise(fn, rows, pars, outs, *, name, tr):
    t = rows[0].shape[0]
    n_in = len(rows) + len(pars)

    def body(*refs):
        res = fn(*[r[...] for r in refs[:n_in]])
        for o_ref, v in zip(refs[n_in:], res):
            o_ref[...] = v

    return pl.pallas_call(
        body, name=name, grid=(t // tr,),
        in_specs=[_row_spec(tr, r.shape[1]) for r in rows] + [_full_spec(p) for p in pars],
        out_specs=[_row_spec(tr, w) for w in outs],
        out_shape=[jax.ShapeDtypeStruct((t, w), F32) for w in outs],
        compiler_params=_cp(("arbitrary",)))(*rows, *pars)


def _rowwise_vjp(fn, rows, pars, cots, *, name, tr):
    t = rows[0].shape[0]
    nr, npar, nc = len(rows), len(pars), len(cots)

    def body(*refs):
        vals = [r[...] for r in refs[:nr + npar]]
        cts = tuple(c[...] for c in refs[nr + npar:nr + npar + nc])
        douts = refs[nr + npar + nc:]
        _, vjp = jax.vjp(fn, *vals)
        grads = vjp(cts)
        for kk in range(nr):
            douts[kk][...] = grads[kk]

        @pl.when(pl.program_id(0) == 0)
        def _():
            for kk in range(npar):
                douts[nr + kk][...] = jnp.zeros_like(douts[nr + kk])

        for kk in range(npar):
            douts[nr + kk][...] += grads[nr + kk]

    return pl.pallas_call(
        body, name=name, grid=(t // tr,),
        in_specs=[_row_spec(tr, r.shape[1]) for r in rows] + [_full_spec(p) for p in pars]
        + [_row_spec(tr, c.shape[1]) for c in cots],
        out_specs=[_row_spec(tr, r.shape[1]) for r in rows] + [_full_spec(p) for p in pars],
        out_shape=[jax.ShapeDtypeStruct(r.shape, F32) for r in rows]
        + [jax.ShapeDtypeStruct(p.shape, F32) for p in pars],
        compiler_params=_cp(("arbitrary",)))(*rows, *pars, *cots)


def _make_rw(fn, name, tr, nr, outs):
    @jax.custom_vjp
    def f(*args):
        return tuple(_rowwise(fn, args[:nr], args[nr:], outs, name=name + "_f", tr=tr))

    def fwd(*args):
        return f(*args), args

    def bwd(args, cts):
        return tuple(_rowwise_vjp(fn, args[:nr], args[nr:], list(cts), name=name + "_b", tr=tr))

    f.defvjp(fwd, bwd)
    return f


def _rms(x, g):
    return x * lax.rsqrt(jnp.mean(jnp.square(x), axis=-1, keepdims=True) + RMS_EPS) * g


def _f_rms(x, g):
    return (_rms(x, g),)


def _f_sgu(au, av, ng, w0, w1, w2, w3, bfull):
    u = jax.nn.gelu(au)
    v = _rms(jax.nn.gelu(av), ng)
    tri = lax.broadcasted_iota(jnp.int32, (SGU_CHUNK, SGU_CHUNK), 0) >= lax.broadcasted_iota(
        jnp.int32, (SGU_CHUNK, SGU_CHUNK), 1)
    head = lax.broadcasted_iota(jnp.int32, v.shape, 1) // HEAD_DIM
    mixed = bfull
    for h, w in enumerate((w0, w1, w2, w3)):
        mixed = mixed + _bdot(jnp.where(tri, w, 0.0), jnp.where(head == h, v, 0.0))
    return (u * mixed,)


def _f_s5disc(lam_re, lam_im, log_dt, b_re, b_im):
    dt = jnp.exp(log_dt)
    mag = jnp.exp(lam_re * dt)
    abar_re = mag * jnp.cos(lam_im * dt)
    abar_im = mag * jnp.sin(lam_im * dt)
    denom = jnp.square(lam_re) + jnp.square(lam_im)
    num_re = abar_re - 1.0
    num_im = abar_im
    fac_re = (num_re * lam_re + num_im * lam_im) / denom
    fac_im = (num_im * lam_re - num_re * lam_im) / denom
    return abar_re, abar_im, fac_re * b_re - fac_im * b_im, fac_re * b_im + fac_im * b_re


def _f_s5post(s_re, s_im, u, c_re, c_im, d, gw, gb):
    y = _bdot(s_re, c_re) - _bdot(s_im, c_im) + d * u
    y = jax.nn.gelu(y)
    return (y * jax.nn.sigmoid(_bdot(y, gw) + gb),)


def _f_lrupre(xc, wa, ba, wx, bx, lam):
    r = jax.nn.sigmoid(_bdot(xc, wa) + ba)
    i = jax.nn.sigmoid(_bdot(xc, wx) + bx)
    log_a = -LRU_C * r * jax.nn.softplus(-lam)
    a = jnp.exp(log_a)
    one_minus_a2 = -jnp.tanh(log_a) * (jnp.exp(2.0 * log_a) + 1.0)
    return a, jnp.sqrt(one_minus_a2) * (i * xc)


def _f_lrupost(h, gate):
    return (h * jax.nn.gelu(gate),)


def _f_logsig(zf, bf):
    return (jax.nn.log_sigmoid(zf + bf),)


def _f_gnorm(ya, yb, yc, yd, g):
    def n(y):
        return y * lax.rsqrt(jnp.mean(jnp.square(y), axis=-1, keepdims=True) + RMS_EPS)
    return (jnp.concatenate([n(ya), n(yb), n(yc), n(yd)], axis=1) * g,)


def _f_relu2(u):
    return (jnp.square(jnp.maximum(u, 0.0)),)


rms_norm = _make_rw(_f_rms, "rms", 512, 1, [D_MODEL])
sgu_mix = _make_rw(_f_sgu, "sgu", SGU_CHUNK, 2, [MIXER_WIDTH])
s5_disc = _make_rw(_f_s5disc, "s5disc", S5_GROUPS * S5_GROUP, 5, [S5_STATE] * 4)
s5_post = _make_rw(_f_s5post, "s5post", 256, 3, [MIXER_WIDTH])
lru_pre = _make_rw(_f_lrupre, "lrupre", 512, 1, [MIXER_WIDTH, MIXER_WIDTH])
lru_post = _make_rw(_f_lrupost, "lrupost", 512, 2, [MIXER_WIDTH])
log_sig = _make_rw(_f_logsig, "logsig", 512, 1, [128])
group_norm = _make_rw(_f_gnorm, "gnorm", 512, 4, [D_MODEL])
relu2 = _make_rw(_f_relu2, "relu2", 256, 1, [4 * D_MODEL])


SCAN_TILE = 512


def _prev_spec(c, nt, rev):
    per = SCAN_TILE // 8
    if rev:
        return pl.BlockSpec((8, c), lambda i: (jnp.maximum((nt - 1 - i) * per - 1, 0), 0))
    return pl.BlockSpec((8, c), lambda i: (jnp.maximum(i * per - 1, 0), 0))


def _lti_fwd_call(a_re, a_im, b_re, b_im):
    t, c = b_re.shape
    tt = SCAN_TILE

    def body(ar_ref, ai_ref, br_ref, bi_ref, sr_ref, si_ref, cr, ci):
        @pl.when(pl.program_id(0) == 0)
        def _():
            cr[...] = jnp.zeros_like(cr)
            ci[...] = jnp.zeros_like(ci)

        ar, ai = ar_ref[...], ai_ref[...]

        def step(tt_i, carry):
            hr, hi = carry
            row = pl.ds(tt_i, 1)
            nr = ar * hr - ai * hi + br_ref[row, :]
            ni = ar * hi + ai * hr + bi_ref[row, :]
            sr_ref[row, :] = nr
            si_ref[row, :] = ni
            return nr, ni

        hr, hi = lax.fori_loop(0, tt, step, (cr[...], ci[...]), unroll=8)
        cr[...] = hr
        ci[...] = hi

    row = pl.BlockSpec((tt, c), lambda i: (i, 0))
    par = pl.BlockSpec((1, c), lambda i: (0, 0))
    return pl.pallas_call(
        body, name="lti_scan_f", grid=(t // tt,), in_specs=[par, par, row, row], out_specs=[row, row],
        out_shape=[jax.ShapeDtypeStruct((t, c), F32)] * 2,
        scratch_shapes=[pltpu.VMEM((1, c), F32)] * 2, compiler_params=_cp(("arbitrary",)))(a_re, a_im, b_re, b_im)


def _lti_bwd_call(a_re, a_im, s_re, s_im, g_re, g_im):
    t, c = g_re.shape
    tt = SCAN_TILE
    nt = t // tt

    def body(ar_ref, ai_ref, sr_ref, si_ref, pr_ref, pi_ref, gr_ref, gi_ref,
             or_ref, oi_ref, dar_ref, dai_ref, cr, ci):
        i = pl.program_id(0)

        @pl.when(i == 0)
        def _():
            cr[...] = jnp.zeros_like(cr)
            ci[...] = jnp.zeros_like(ci)
            dar_ref[...] = jnp.zeros_like(dar_ref)
            dai_ref[...] = jnp.zeros_like(dai_ref)

        ar, ai = ar_ref[...], ai_ref[...]

        def one(row, spr, spi, carry):
            gr_c, gi_c, dar, dai = carry
            nr = gr_ref[row, :] + ar * gr_c + ai * gi_c
            ni = gi_ref[row, :] + ar * gi_c - ai * gr_c
            or_ref[row, :] = nr
            oi_ref[row, :] = ni
            return nr, ni, dar + spr * nr + spi * ni, dai + spr * ni - spi * nr

        def step(kk, carry):
            tt_i = tt - 1 - kk
            prev = pl.ds(tt_i - 1, 1)
            return one(pl.ds(tt_i, 1), sr_ref[prev, :], si_ref[prev, :], carry)

        zero = jnp.zeros((1, c), F32)
        carry = lax.fori_loop(0, tt - 1, step, (cr[...], ci[...], zero, zero), unroll=8)
        first = i == nt - 1
        spr = jnp.where(first, 0.0, pr_ref[7:8, :])
        spi = jnp.where(first, 0.0, pi_ref[7:8, :])
        gr_c, gi_c, dar, dai = one(pl.ds(0, 1), spr, spi, carry)
        cr[...] = gr_c
        ci[...] = gi_c
        dar_ref[...] += dar
        dai_ref[...] += dai

    row = pl.BlockSpec((tt, c), lambda i: (nt - 1 - i, 0))
    par = pl.BlockSpec((1, c), lambda i: (0, 0))
    prev = _prev_spec(c, nt, True)
    return pl.pallas_call(
        body, name="lti_scan_b", grid=(nt,), in_specs=[par, par, row, row, prev, prev, row, row],
        out_specs=[row, row, par, par],
        out_shape=[jax.ShapeDtypeStruct((t, c), F32)] * 2 + [jax.ShapeDtypeStruct((1, c), F32)] * 2,
        scratch_shapes=[pltpu.VMEM((1, c), F32)] * 2,
        compiler_params=_cp(("arbitrary",)))(a_re, a_im, s_re, s_im, s_re, s_im, g_re, g_im)


@jax.custom_vjp
def lti_scan(a_re, a_im, b_re, b_im):
    return tuple(_lti_fwd_call(a_re, a_im, b_re, b_im))


def _lti_scan_fwd(a_re, a_im, b_re, b_im):
    s_re, s_im = _lti_fwd_call(a_re, a_im, b_re, b_im)
    return (s_re, s_im), (a_re, a_im, s_re, s_im)


def _lti_scan_bwd(r, g):
    a_re, a_im, s_re, s_im = r
    o_re, o_im, da_re, da_im = _lti_bwd_call(a_re, a_im, s_re, s_im, g[0], g[1])
    return da_re, da_im, o_re, o_im


lti_scan.defvjp(_lti_scan_fwd, _lti_scan_bwd)


def _tv_fwd_call(a, b):
    t, c = b.shape
    tt = SCAN_TILE

    def body(a_ref, b_ref, h_ref, ch):
        @pl.when(pl.program_id(0) == 0)
        def _():
            ch[...] = jnp.zeros_like(ch)

        def step(tt_i, h):
            row = pl.ds(tt_i, 1)
            h = a_ref[row, :] * h + b_ref[row, :]
            h_ref[row, :] = h
            return h

        ch[...] = lax.fori_loop(0, tt, step, ch[...], unroll=8)

    row = pl.BlockSpec((tt, c), lambda i: (i, 0))
    return pl.pallas_call(
        body, name="tv_scan_f", grid=(t // tt,), in_specs=[row, row], out_specs=row,
        out_shape=jax.ShapeDtypeStruct((t, c), F32), scratch_shapes=[pltpu.VMEM((1, c), F32)],
        compiler_params=_cp(("arbitrary",)))(a, b)


def _tv_bwd_call(a, h, g):
    t, c = g.shape
    tt = SCAN_TILE
    nt = t // tt

    def body(a_ref, h_ref, p_ref, g_ref, da_ref, db_ref, cg, ca):
        i = pl.program_id(0)

        @pl.when(i == 0)
        def _():
            cg[...] = jnp.zeros_like(cg)
            ca[...] = jnp.zeros_like(ca)

        def one(row, hp, carry):
            gc, an = carry
            gn = g_ref[row, :] + an * gc
            db_ref[row, :] = gn
            da_ref[row, :] = gn * hp
            return gn, a_ref[row, :]

        def step(kk, carry):
            tt_i = tt - 1 - kk
            return one(pl.ds(tt_i, 1), h_ref[pl.ds(tt_i - 1, 1), :], carry)

        carry = lax.fori_loop(0, tt - 1, step, (cg[...], ca[...]), unroll=8)
        hp = jnp.where(i == nt - 1, 0.0, p_ref[7:8, :])
        gc, an = one(pl.ds(0, 1), hp, carry)
        cg[...] = gc
        ca[...] = an

    row = pl.BlockSpec((tt, c), lambda i: (nt - 1 - i, 0))
    return pl.pallas_call(
        body, name="tv_scan_b", grid=(nt,), in_specs=[row, row, _prev_spec(c, nt, True), row],
        out_specs=[row, row], out_shape=[jax.ShapeDtypeStruct((t, c), F32)] * 2,
        scratch_shapes=[pltpu.VMEM((1, c), F32)] * 2, compiler_params=_cp(("arbitrary",)))(a, h, h, g)


@jax.custom_vjp
def tv_scan(a, b):
    return _tv_fwd_call(a, b)


def _tv_scan_fwd(a, b):
    h = _tv_fwd_call(a, b)
    return h, (a, h)


def _tv_scan_bwd(r, g):
    a, h = r
    return tuple(_tv_bwd_call(a, h, g))


tv_scan.defvjp(_tv_scan_fwd, _tv_scan_bwd)


CONV_K = 4
CONV_ROWS = 512


def _conv_fwd_call(x, w, b):
    t, c = x.shape

    def body(x_ref, w_ref, b_ref, o_ref, xp):
        xp[0:8, :] = jnp.zeros((8, c), F32)
        xp[8:, :] = x_ref[...]
        for blk in range(t // CONV_ROWS):
            base = blk * CONV_ROWS
            acc = jnp.broadcast_to(b_ref[...], (CONV_ROWS, c))
            for kk in range(CONV_K):
                acc = acc + w_ref[kk:kk + 1, :] * xp[base + 5 + kk:base + 5 + kk + CONV_ROWS, :]
            o_ref[base:base + CONV_ROWS, :] = acc

    return pl.pallas_call(
        body, name="conv_f", out_shape=jax.ShapeDtypeStruct((t, c), F32),
        scratch_shapes=[pltpu.VMEM((t + 8, c), F32)], compiler_params=_cp())(x, w, b)


def _conv_bwd_call(x, w, g):
    t, c = x.shape

    def body(x_ref, w_ref, g_ref, dx_ref, dw_ref, db_ref, xp, gp):
        xp[0:8, :] = jnp.zeros((8, c), F32)
        xp[8:, :] = x_ref[...]
        gp[0:t, :] = g_ref[...]
        gp[t:, :] = jnp.zeros((8, c), F32)
        dw = [jnp.zeros((1, c), F32) for _ in range(CONV_K)]
        db = jnp.zeros((1, c), F32)
        for blk in range(t // CONV_ROWS):
            base = blk * CONV_ROWS
            gb = g_ref[base:base + CONV_ROWS, :]
            acc = jnp.zeros((CONV_ROWS, c), F32)
            for kk in range(CONV_K):
                acc = acc + w_ref[kk:kk + 1, :] * gp[base + 3 - kk:base + 3 - kk + CONV_ROWS, :]
                dw[kk] = dw[kk] + jnp.sum(gb * xp[base + 5 + kk:base + 5 + kk + CONV_ROWS, :], axis=0, keepdims=True)
            db = db + jnp.sum(gb, axis=0, keepdims=True)
            dx_ref[base:base + CONV_ROWS, :] = acc
        for kk in range(CONV_K):
            dw_ref[kk:kk + 1, :] = dw[kk]
        db_ref[...] = db

    return pl.pallas_call(
        body, name="conv_b",
        out_shape=[jax.ShapeDtypeStruct((t, c), F32), jax.ShapeDtypeStruct((CONV_K, c), F32),
                   jax.ShapeDtypeStruct((1, c), F32)],
        scratch_shapes=[pltpu.VMEM((t + 8, c), F32)] * 2, compiler_params=_cp())(x, w, g)


@jax.custom_vjp
def causal_conv(x, w, b):
    return _conv_fwd_call(x, w, b)


def _causal_conv_fwd(x, w, b):
    return _conv_fwd_call(x, w, b), (x, w)


def _causal_conv_bwd(r, g):
    return tuple(_conv_bwd_call(r[0], r[1], g))


causal_conv.defvjp(_causal_conv_fwd, _causal_conv_bwd)


ATT_TILE = 512
ATT_SCALE = HEAD_DIM ** -0.5


def _attn_fwd_call(q, k, v, cc, cr):
    h, t, dh = q.shape
    tq = ATT_TILE
    nq = t // tq
    k4, v4, cr4 = k.reshape(h, nq, tq, dh), v.reshape(h, nq, tq, dh), cr.reshape(h, nq, 1, tq)

    def body(q_ref, k_ref, v_ref, cc_ref, cr_ref, o_ref, lse_ref):
        i = pl.program_id(1)
        qs = (q_ref[0] * ATT_SCALE).astype(BF16)
        cq = cc_ref[0]
        row = i * tq + lax.broadcasted_iota(jnp.int32, (tq, tq), 0)
        col = lax.broadcasted_iota(jnp.int32, (tq, tq), 1)

        def step(j, carry):
            m, l, acc = carry
            s = _dot(qs, k_ref[0, j], 1, 1) + cq - cr_ref[0, j]
            s = jnp.where(row >= j * tq + col, s, NEG)
            m_new = jnp.maximum(m, jnp.max(s, axis=1, keepdims=True))
            p = jnp.exp(s - m_new)
            alpha = jnp.exp(m - m_new)
            return m_new, alpha * l + jnp.sum(p, axis=1, keepdims=True), alpha * acc + _dot(p, v_ref[0, j], 1, 0)

        init = (jnp.full((tq, 1), NEG, F32), jnp.zeros((tq, 1), F32), jnp.zeros((tq, dh), F32))
        m, l, acc = lax.fori_loop(0, i + 1, step, init)
        o_ref[0] = acc / l
        lse_ref[0] = m + jnp.log(l)

    tile = pl.BlockSpec((1, tq, dh), lambda hh, i: (hh, i, 0))
    col1 = pl.BlockSpec((1, tq, 1), lambda hh, i: (hh, i, 0))
    whole = pl.BlockSpec((1, nq, tq, dh), lambda hh, i: (hh, 0, 0, 0))
    whole_r = pl.BlockSpec((1, nq, 1, tq), lambda hh, i: (hh, 0, 0, 0))
    return pl.pallas_call(
        body, name="attn_f", grid=(h, nq), in_specs=[tile, whole, whole, col1, whole_r], out_specs=[tile, col1],
        out_shape=[jax.ShapeDtypeStruct((h, t, dh), F32), jax.ShapeDtypeStruct((h, t, 1), F32)],
        compiler_params=_cp(("arbitrary", "arbitrary")))(q, k4, v4, cc, cr4)


def _attn_bwd_call(q, k, v, cc, cr, o, lse, do):
    h, t, dh = q.shape
    tq = ATT_TILE
    nq = t // tq
    r4 = lambda a: a.reshape(h, nq, tq, a.shape[-1])
    cr4 = cr.reshape(h, nq, 1, tq)

    def body(q_ref, k_ref, v_ref, cc_ref, cr_ref, o_ref, lse_ref, do_ref,
             dq_ref, dk_ref, dv_ref, dcc_ref, dcr_ref):
        j = pl.program_id(1)

        @pl.when(j == 0)
        def _():
            dq_ref[...] = jnp.zeros_like(dq_ref)
            dcc_ref[...] = jnp.zeros_like(dcc_ref)

        kj = k_ref[0, 0].astype(BF16)
        vj = v_ref[0, 0].astype(BF16)
        ck = cr_ref[0, 0]
        col = j * tq + lax.broadcasted_iota(jnp.int32, (tq, tq), 1)
        row = lax.broadcasted_iota(jnp.int32, (tq, tq), 0)

        def step(i, carry):
            dk, dv, dck = carry
            qi = q_ref[0, i]
            doi = do_ref[0, i]
            s = _dot(qi * ATT_SCALE, kj, 1, 1) + cc_ref[0, i] - ck
            p = jnp.where(i * tq + row >= col, jnp.exp(s - lse_ref[0, i]), 0.0)
            dv = dv + _dot(p, doi, 0, 0)
            dp = _dot(doi, vj, 1, 1)
            delta = jnp.sum(doi * o_ref[0, i], axis=1, keepdims=True)
            ds = p * (dp - delta)
            dq_ref[0, i] += _dot(ds, kj, 1, 0) * ATT_SCALE
            dk = dk + _dot(ds, qi, 0, 0) * ATT_SCALE
            dcc_ref[0, i] += jnp.sum(ds, axis=1, keepdims=True)
            return dk, dv, dck - jnp.sum(ds, axis=0, keepdims=True)

        init = (jnp.zeros((tq, dh), F32), jnp.zeros((tq, dh), F32), jnp.zeros((1, tq), F32))
        dk, dv, dck = lax.fori_loop(j, nq, step, init)
        dk_ref[0, 0] = dk
        dv_ref[0, 0] = dv
        dcr_ref[0, 0] = dck

    whole = pl.BlockSpec((1, nq, tq, dh), lambda hh, j: (hh, 0, 0, 0))
    whole_c = pl.BlockSpec((1, nq, tq, 1), lambda hh, j: (hh, 0, 0, 0))
    tile = pl.BlockSpec((1, 1, tq, dh), lambda hh, j: (hh, j, 0, 0))
    tile_r = pl.BlockSpec((1, 1, 1, tq), lambda hh, j: (hh, j, 0, 0))
    s4 = jax.ShapeDtypeStruct((h, nq, tq, dh), F32)
    dq, dk, dv, dcc, dcr = pl.pallas_call(
        body, name="attn_b", grid=(h, nq),
        in_specs=[whole, tile, tile, whole_c, tile_r, whole, whole_c, whole],
        out_specs=[whole, tile, tile, whole_c, tile_r],
        out_shape=[s4, s4, s4, jax.ShapeDtypeStruct((h, nq, tq, 1), F32), jax.ShapeDtypeStruct((h, nq, 1, tq), F32)],
        compiler_params=_cp(("arbitrary", "arbitrary")))(r4(q), r4(k), r4(v), r4(cc), cr4, r4(o), r4(lse), r4(do))
    return (dq.reshape(h, t, dh), dk.reshape(h, t, dh), dv.reshape(h, t, dh), dcc.reshape(h, t, 1),
            dcr.reshape(h, 1, t))


@jax.custom_vjp
def fox_attention(q, k, v, cc, cr):
    return _attn_fwd_call(q, k, v, cc, cr)[0]


def _fox_attention_fwd(q, k, v, cc, cr):
    o, lse = _attn_fwd_call(q, k, v, cc, cr)
    return o, (q, k, v, cc, cr, o, lse)


def _fox_attention_bwd(r, g):
    return _attn_bwd_call(*r, g)


fox_attention.defvjp(_fox_attention_fwd, _fox_attention_bwd)


def _loss_call(x, g, target):
    t, d = x.shape
    tr = 512

    def body(x_ref, g_ref, t_ref, loss_ref, dx_ref, dg_ref):
        tgt = t_ref[...]

        def f(xv, gv):
            return 0.5 * jnp.sum(jnp.mean(jnp.square(_rms(xv, gv) - tgt), axis=-1))

        val, vjp = jax.vjp(f, x_ref[...], g_ref[...])
        dx, dg = vjp(jnp.ones((), F32))
        dx_ref[...] = dx

        @pl.when(pl.program_id(0) == 0)
        def _():
            loss_ref[...] = jnp.zeros_like(loss_ref)
            dg_ref[...] = jnp.zeros_like(dg_ref)

        loss_ref[...] += jnp.full(loss_ref.shape, val, F32)
        dg_ref[...] += dg

    row = _row_spec(tr, d)
    return pl.pallas_call(
        body, name="loss_head", grid=(t // tr,), in_specs=[row, _full_spec(g), row],
        out_specs=[pl.BlockSpec((1, 128), lambda i: (0, 0)), row, _full_spec(g)],
        out_shape=[jax.ShapeDtypeStruct((1, 128), F32), jax.ShapeDtypeStruct((t, d), F32),
                   jax.ShapeDtypeStruct(g.shape, F32)],
        compiler_params=_cp(("arbitrary",)))(x, g, target)


def _blockdiag(w):
    g, a, b = w.shape
    return jnp.einsum('gab,gk->gakb', w, jnp.eye(g, dtype=w.dtype)).reshape(g * a, g * b)


def _s5_params(p):
    rep = lambda a: jnp.repeat(a, S5_GROUP, axis=0)
    rows = S5_GROUPS * S5_GROUP
    bt = lambda b: b.transpose(0, 2, 1).reshape(rows, S5_STATE)
    abar_re, abar_im, bb_re, bb_im = s5_disc(
        rep(p["s5_lambda_re"]), rep(p["s5_lambda_im"]), rep(p["s5_log_dt"][:, None]), bt(p["s5_b_re"]), bt(p["s5_b_im"]))
    first = lambda a: a.reshape(S5_GROUPS, S5_GROUP, S5_STATE)[:, 0, :].reshape(1, S5_GROUPS * S5_STATE)
    g3 = lambda a: a.reshape(S5_GROUPS, S5_GROUP, S5_STATE)
    cblk = lambda c: _blockdiag(c.transpose(0, 2, 1))
    return (first(abar_re), first(abar_im), _blockdiag(g3(bb_re)), _blockdiag(g3(bb_im)),
            cblk(p["s5_c_re"]), cblk(p["s5_c_im"]))


def _layer(x, p):
    t = x.shape[0]
    w = MIXER_WIDTH
    row = lambda a: a[None, :]
    (h,) = rms_norm(x, row(p["norm1_g"]))
    z = mm(h, jnp.pad(p["w_in"], ((0, 0), (0, D_IN_PAD - D_IN_PROJ))))
    a_u, a_v, b_in, c_x, c_gate, d_q, d_k, d_v, d_f = jnp.split(z, [w * i for i in range(1, 9)], axis=1)
    sw = p["sgu_w"]
    (y_a,) = sgu_mix(a_u, a_v, row(p["sgu_norm_g"]), sw[0], sw[1], sw[2], sw[3],
                     jnp.repeat(p["sgu_b"].T, HEAD_DIM, axis=1))
    abar_re, abar_im, bblk_re, bblk_im, cblk_re, cblk_im = _s5_params(p)
    s_re, s_im = lti_scan(abar_re, abar_im, mm(b_in, bblk_re), mm(b_in, bblk_im))
    (y_b,) = s5_post(s_re, s_im, b_in, cblk_re, cblk_im, row(p["s5_d"]), p["s5_glu_w"], row(p["s5_glu_b"]))
    xc = causal_conv(c_x, p["lru_conv_w"], row(p["lru_conv_b"]))
    a, b = lru_pre(xc, _blockdiag(p["lru_wa"]), p["lru_ba"].reshape(1, w), _blockdiag(p["lru_wx"]),
                   p["lru_bx"].reshape(1, w), row(p["lru_lambda"]))
    (y_c,) = lru_post(tv_scan(a, b), c_gate)
    (log_f,) = log_sig(d_f, jnp.pad(p["fox_fgate_b"], (0, 128 - N_HEADS))[None, :])
    c = tv_scan(jnp.ones_like(log_f), log_f)[:, :N_HEADS].T
    heads = lambda a: a.reshape(t, N_HEADS, HEAD_DIM).transpose(1, 0, 2)
    o = fox_attention(heads(d_q), heads(d_k), heads(d_v), c[:, :, None], c[:, None, :])
    y_d = o.transpose(1, 0, 2).reshape(t, w)
    (y,) = group_norm(y_a, y_b, y_c, y_d, row(p["mix_norm_g"]))
    x = mm_res(y, p["w_out"], x)
    (h,) = rms_norm(x, row(p["norm2_g"]))
    (act,) = relu2(mm(h, p["w_mlp_in"]))
    return mm_res(act, p["w_mlp_out"], x)


LAYER_WEIGHTS = ('norm1_g', 'w_in', 'sgu_norm_g', 'sgu_w', 'sgu_b', 's5_lambda_re', 's5_lambda_im', 's5_log_dt',
                 's5_b_re', 's5_b_im', 's5_c_re', 's5_c_im', 's5_d', 's5_glu_w', 's5_glu_b', 'lru_conv_w',
                 'lru_conv_b', 'lru_wa', 'lru_ba', 'lru_wx', 'lru_bx', 'lru_lambda', 'fox_fgate_b', 'mix_norm_g',
                 'w_out', 'norm2_g', 'w_mlp_in', 'w_mlp_out')
SHARDED = {'w_in': 1, 's5_glu_w': 0, 'lru_conv_w': 1, 'w_out': 0, 'w_mlp_in': 1, 'w_mlp_out': 0}


def _trunk(x, weights):
    for l in range(DEPTH):
        x = _layer(x, {n: weights[n][l] for n in LAYER_WEIGHTS})
    return x


HBM = pl.BlockSpec(memory_space=pltpu.HBM)
PACK_COLS = 1024


def _coords():
    return lax.axis_index("x"), lax.axis_index("y"), lax.axis_index("c")


def _other_chips(x, y):
    return [(1 - x, y), (x, 1 - y), (1 - x, 1 - y)]


def _allgather_shards(shard):
    r, cols = shard.shape
    rh = r // 2

    def body(in_ref, out_ref, send_sems, recv_sems, local_sem):
        x, y, c = _coords()
        me, sibling = (x, y, c), (x, y, 1 - c)
        chips = _other_chips(x, y)

        def half(px, py, pc):
            return out_ref.at[2 * px + py, pl.ds(pc * rh, rh), :]

        def copy(k, block, to, src=None):
            return pltpu.make_async_remote_copy(
                src_ref=half(*block) if src is None else src, dst_ref=half(*block),
                send_sem=send_sems.at[k], recv_sem=recv_sems.at[k], device_id=to, device_id_type=MESH)

        mine = pltpu.make_async_copy(in_ref, out_ref.at[2 * x + y], local_sem)
        mine.start()
        first = [copy(j, me, (*chip, c), src=in_ref.at[pl.ds(c * rh, rh), :]) for j, chip in enumerate(chips)]
        for cp in first:
            cp.start()
        passed = [copy(3 + j, (*chip, c), sibling) for j, chip in enumerate(chips)]
        for j, chip in enumerate(chips):
            copy(j, (*chip, c), me).wait_recv()
            passed[j].start()
        for j, chip in enumerate(chips):
            copy(3 + j, (*chip, 1 - c), me).wait_recv()
        for cp in first + passed:
            cp.wait_send()
        mine.wait()

    return pl.pallas_call(
        body, name="allgather_shards", out_shape=jax.ShapeDtypeStruct((4, r, cols), shard.dtype),
        in_specs=[HBM], out_specs=HBM,
        scratch_shapes=[pltpu.SemaphoreType.DMA((6,)), pltpu.SemaphoreType.DMA((6,)), pltpu.SemaphoreType.DMA],
        compiler_params=pltpu.CompilerParams())(shard)


def _pair_exchange(g):
    s, r, cols = g.shape
    rh = r // 2

    def body(g_ref, keep_ref, recv_ref, send_sem, recv_sem, local_sem):
        x, y, c = _coords()
        local = pltpu.make_async_copy(g_ref.at[:, pl.ds(c * rh, rh), :], keep_ref, local_sem)
        local.start()
        rc = pltpu.make_async_remote_copy(
            src_ref=g_ref.at[:, pl.ds((1 - c) * rh, rh), :], dst_ref=recv_ref, send_sem=send_sem, recv_sem=recv_sem,
            device_id=(x, y, 1 - c), device_id_type=MESH)
        rc.start()
        rc.wait()
        local.wait()

    o = jax.ShapeDtypeStruct((s, rh, cols), g.dtype)
    return pl.pallas_call(
        body, name="pair_exchange", out_shape=[o, o], in_specs=[HBM], out_specs=[HBM, HBM],
        scratch_shapes=[pltpu.SemaphoreType.DMA] * 3,
        compiler_params=pltpu.CompilerParams())(g)


def _chip_exchange(p):
    def body(p_ref, recv_ref, send_sems, recv_sems, local_sem):
        x, y, c = _coords()
        mine = 2 * x + y
        chips = _other_chips(x, y)
        local = pltpu.make_async_copy(p_ref.at[mine], recv_ref.at[mine], local_sem)
        local.start()

        def copy(j, src_slot, dst_slot, chip):
            return pltpu.make_async_remote_copy(
                src_ref=p_ref.at[src_slot], dst_ref=recv_ref.at[dst_slot], send_sem=send_sems.at[j],
                recv_sem=recv_sems.at[j], device_id=(*chip, c), device_id_type=MESH)

        sends = [copy(j, 2 * chip[0] + chip[1], mine, chip) for j, chip in enumerate(chips)]
        for cp in sends:
            cp.start()
        for j, chip in enumerate(chips):
            theirs = 2 * chip[0] + chip[1]
            copy(j, theirs, theirs, chip).wait_recv()
        for cp in sends:
            cp.wait_send()
        local.wait()

    return pl.pallas_call(
        body, name="chip_exchange", out_shape=jax.ShapeDtypeStruct(p.shape, p.dtype), in_specs=[HBM], out_specs=HBM,
        scratch_shapes=[pltpu.SemaphoreType.DMA((3,)), pltpu.SemaphoreType.DMA((3,)), pltpu.SemaphoreType.DMA],
        compiler_params=pltpu.CompilerParams())(p)


def _pair_share(f):
    rh, cols = f.shape

    def body(f_ref, out_ref, send_sem, recv_sem, local_sem):
        x, y, c = _coords()
        local = pltpu.make_async_copy(f_ref, out_ref.at[c], local_sem)
        local.start()
        rc = pltpu.make_async_remote_copy(
            src_ref=f_ref, dst_ref=out_ref.at[c], send_sem=send_sem, recv_sem=recv_sem,
            device_id=(x, y, 1 - c), device_id_type=MESH)
        rc.start()
        rc.wait_send()
        pltpu.make_async_remote_copy(
            src_ref=f_ref, dst_ref=out_ref.at[1 - c], send_sem=send_sem, recv_sem=recv_sem,
            device_id=(x, y, 1 - c), device_id_type=MESH).wait_recv()
        local.wait()

    return pl.pallas_call(
        body, name="pair_share", out_shape=jax.ShapeDtypeStruct((2, rh, cols), f.dtype), in_specs=[HBM], out_specs=HBM,
        scratch_shapes=[pltpu.SemaphoreType.DMA] * 3,
        compiler_params=pltpu.CompilerParams())(f)


def _allgather_all(blk):
    m_per, cols = blk.shape

    def body(x_ref, out_ref, send_sems, recv_sems, local_sem):
        x, y, c = _coords()
        me, sibling = (x, y, c), (x, y, 1 - c)
        chips = _other_chips(x, y)

        def rows(px, py, pc):
            return out_ref.at[4 * px + 2 * py + pc]

        def copy(k, block, to, src=None):
            return pltpu.make_async_remote_copy(
                src_ref=rows(*block) if src is None else src, dst_ref=rows(*block),
                send_sem=send_sems.at[k], recv_sem=recv_sems.at[k], device_id=to, device_id_type=MESH)

        mine = pltpu.make_async_copy(x_ref, rows(*me), local_sem)
        mine.start()
        first = [copy(0, me, sibling, src=x_ref)]
        first += [copy(1 + j, me, (*chip, c), src=x_ref) for j, chip in enumerate(chips)]
        for cp in first:
            cp.start()
        passed = [copy(4 + j, (*chip, c), sibling) for j, chip in enumerate(chips)]
        for j, chip in enumerate(chips):
            copy(1 + j, (*chip, c), me).wait_recv()
            passed[j].start()
        copy(0, sibling, me).wait_recv()
        for j, chip in enumerate(chips):
            copy(4 + j, (*chip, 1 - c), me).wait_recv()
        for cp in first + passed:
            cp.wait_send()
        mine.wait()

    return pl.pallas_call(
        body, name="allgather_all", out_shape=jax.ShapeDtypeStruct((8, m_per, cols), blk.dtype),
        in_specs=[pl.BlockSpec(memory_space=pltpu.VMEM)], out_specs=pl.BlockSpec(memory_space=pltpu.VMEM),
        scratch_shapes=[pltpu.SemaphoreType.DMA((7,)), pltpu.SemaphoreType.DMA((7,)), pltpu.SemaphoreType.DMA],
        compiler_params=pltpu.CompilerParams(vmem_limit_bytes=VMEM_LIMIT))(blk)


def _add_to_bf16(a, b, tr):
    def body(a_ref, b_ref, o_ref):
        o_ref[...] = (a_ref[...] + b_ref[...]).astype(BF16)

    spec = _row_spec(tr, a.shape[1])
    return pl.pallas_call(body, name="add_to_bf16", grid=(a.shape[0] // tr,), in_specs=[spec, spec], out_specs=spec,
                          out_shape=jax.ShapeDtypeStruct(a.shape, BF16), compiler_params=_cp(("arbitrary",)))(a, b)


def _sum_slots(p, tr, name):
    s, rows, cols = p.shape

    def body(p_ref, o_ref):
        acc = p_ref[0].astype(F32)
        for k in range(1, s):
            acc = acc + p_ref[k].astype(F32)
        o_ref[...] = acc

    return pl.pallas_call(
        body, name=name, grid=(rows // tr,), in_specs=[pl.BlockSpec((s, tr, cols), lambda i: (0, i, 0))],
        out_specs=_row_spec(tr, cols), out_shape=jax.ShapeDtypeStruct((rows, cols), F32),
        compiler_params=_cp(("arbitrary",)))(p)


def _adamw_call(w, g, m, v, name):
    rows, cols = w.shape
    tr = _tile(rows, 512) if rows % 512 == 0 else _tile(rows, 128)
    c1 = 1.0 - ADAM_B1 ** ADAM_STEP
    c2 = 1.0 - ADAM_B2 ** ADAM_STEP

    def body(w_ref, g_ref, m_ref, v_ref, d_ref, nm_ref, nv_ref):
        gv = g_ref[...]
        nm = ADAM_B1 * m_ref[...] + (1.0 - ADAM_B1) * gv
        nv = ADAM_B2 * v_ref[...] + (1.0 - ADAM_B2) * jnp.square(gv)
        d_ref[...] = -ADAM_LR * ((nm / c1) / (jnp.sqrt(nv / c2) + ADAM_EPS) + ADAM_WD * w_ref[...])
        nm_ref[...] = nm
        nv_ref[...] = nv

    spec = _row_spec(tr, cols)
    o = jax.ShapeDtypeStruct((rows, cols), F32)
    return pl.pallas_call(body, name=name, grid=(rows // tr,), in_specs=[spec] * 4, out_specs=[spec] * 3,
                          out_shape=[o, o, o], compiler_params=_cp(("arbitrary",)))(w, g, m, v)


WEIGHTS = LAYER_WEIGHTS + ('final_g',)
N_W = len(WEIGHTS)
SHARD_SHAPE = {'w_in': (1024, 513), 's5_glu_w': (64, 256), 'lru_conv_w': (4, 64), 'w_out': (256, 1024),
               'w_mlp_in': (1024, 1024), 'w_mlp_out': (1024, 1024)}
SHARD_ROWS = {n: -(-s[0] * s[1] // PACK_COLS) for n, s in SHARD_SHAPE.items()}
LAYER_ROWS = 2880
assert sum(SHARD_ROWS.values()) <= LAYER_ROWS
PACK_ROWS = DEPTH * LAYER_ROWS


def _pack_shards(shards, dtype):
    parts = []
    for n in SHARD_SHAPE:
        flat = shards[n].reshape(DEPTH, -1).astype(dtype)
        flat = jnp.pad(flat, ((0, 0), (0, SHARD_ROWS[n] * PACK_COLS - flat.shape[1])))
        parts.append(flat.reshape(DEPTH, SHARD_ROWS[n], PACK_COLS))
    used = sum(SHARD_ROWS.values())
    parts.append(jnp.zeros((DEPTH, LAYER_ROWS - used, PACK_COLS), dtype))
    return jnp.concatenate(parts, axis=1).reshape(PACK_ROWS, PACK_COLS)


def _unpack_shards(buf):
    lead = buf.shape[:-2]
    b = buf.reshape(*lead, DEPTH, LAYER_ROWS, PACK_COLS)
    out, off = {}, 0
    for n, (s0, s1) in SHARD_SHAPE.items():
        rows = SHARD_ROWS[n]
        flat = b[..., off:off + rows, :].reshape(*lead, DEPTH, rows * PACK_COLS)
        out[n] = flat[..., :s0 * s1].reshape(*lead, DEPTH, s0, s1)
        off += rows
    return out


def _join_chips(g, axis):
    _, d, s0, s1 = g.shape
    if axis == 0:
        return g.transpose(1, 0, 2, 3).reshape(d, 4 * s0, s1)
    return g.transpose(1, 2, 0, 3).reshape(d, s0, 4 * s1)


def _split_chips(w, axis):
    d = w.shape[0]
    if axis == 0:
        return w.reshape(d, 4, w.shape[1] // 4, w.shape[2]).transpose(1, 0, 2, 3)
    return w.reshape(d, w.shape[1], 4, w.shape[2] // 4).transpose(2, 0, 1, 3)


def _pack_flat(arrs, rows):
    flat = jnp.concatenate([a.reshape(-1) for a in arrs])
    return jnp.pad(flat, (0, rows * PACK_COLS - flat.shape[0])).reshape(rows, PACK_COLS)


def _unpack_flat(buf, shapes):
    flat, out, off = buf.reshape(-1), [], 0
    for s in shapes:
        n = math.prod(s)
        out.append(flat[off:off + n].reshape(s))
        off += n
    return out


def _step(*args):
    x, target = args[0], args[1 + N_W]
    w = dict(zip(WEIGHTS, args[1:1 + N_W]))
    m = dict(zip(WEIGHTS, args[2 + N_W:2 + 2 * N_W]))
    v = dict(zip(WEIGHTS, args[2 + 2 * N_W:2 + 3 * N_W]))

    gathered = _allgather_shards(_pack_shards({n: w[n] for n in SHARD_SHAPE}, BF16))
    full = {n: _join_chips(g, SHARDED[n]).astype(F32) for n, g in _unpack_shards(gathered).items()}
    trunk_w = {n: full.get(n, w[n]) for n in LAYER_WEIGHTS}

    x_out, trunk_vjp = jax.vjp(_trunk, x[0], trunk_w)
    loss_part, dx_out, d_final = _loss_call(x_out, w['final_g'][None, :], target[0])
    dx, dw = trunk_vjp(dx_out)

    send = jnp.stack([_pack_shards({n: _split_chips(dw[n], SHARDED[n])[s] for n in SHARD_SHAPE}, F32)
                      for s in range(4)])
    keep, recv = _pair_exchange(send)
    rh = PACK_ROWS // 2
    chip_sum = _add_to_bf16(keep.reshape(4 * rh, PACK_COLS), recv.reshape(4 * rh, PACK_COLS), 512)
    half = _sum_slots(_chip_exchange(chip_sum.reshape(4, rh, PACK_COLS)), 640, "sum_chips")
    g_shard = _unpack_shards(_pair_share(half).reshape(PACK_ROWS, PACK_COLS))

    small = [n for n in WEIGHTS if n not in SHARD_SHAPE]
    small_g = [d_final.reshape(-1) if n == 'final_g' else dw[n] for n in small]
    n_small = sum(math.prod(w[n].shape) for n in small) + 1
    small_rows = -(-n_small // (128 * PACK_COLS)) * 128
    small_sum = _sum_slots(_allgather_all(_pack_flat(small_g + [loss_part[0, :1]], small_rows)), 128, "sum_devices")
    *g_small, loss = _unpack_flat(small_sum, [w[n].shape for n in small] + [()])

    grads, delta, new_m, new_v = {}, {}, {}, {}
    for n in SHARD_SHAPE:
        shp = w[n].shape
        v2 = lambda a: a.reshape(-1, shp[-1])
        res = _adamw_call(v2(w[n]), v2(g_shard[n]), v2(m[n]), v2(v[n]), "adamw_" + n)
        grads[n] = g_shard[n]
        delta[n], new_m[n], new_v[n] = (r.reshape(shp) for r in res)
    pk = lambda d: _pack_flat([d[n] for n in small], small_rows)
    res = _adamw_call(pk(w), _pack_flat(g_small, small_rows), pk(m), pk(v), "adamw_small")
    shapes = [w[n].shape for n in small]
    for n, g, d_, m_, v_ in zip(small, g_small, *(_unpack_flat(r, shapes) for r in res)):
        grads[n], delta[n], new_m[n], new_v[n] = g, d_, m_, v_

    return (loss, dx[None], *[grads[n] for n in WEIGHTS], *[delta[n] for n in WEIGHTS],
            *[new_m[n] for n in WEIGHTS], *[new_v[n] for n in WEIGHTS])


def kernel(x, norm1_g, w_in, sgu_norm_g, sgu_w, sgu_b, s5_lambda_re, s5_lambda_im, s5_log_dt, s5_b_re, s5_b_im, s5_c_re, s5_c_im, s5_d, s5_glu_w, s5_glu_b, lru_conv_w, lru_conv_b, lru_wa, lru_ba, lru_wx, lru_bx, lru_lambda, fox_fgate_b, mix_norm_g, w_out, norm2_g, w_mlp_in, w_mlp_out, final_g, loss_target, m_norm1_g, m_w_in, m_sgu_norm_g, m_sgu_w, m_sgu_b, m_s5_lambda_re, m_s5_lambda_im, m_s5_log_dt, m_s5_b_re, m_s5_b_im, m_s5_c_re, m_s5_c_im, m_s5_d, m_s5_glu_w, m_s5_glu_b, m_lru_conv_w, m_lru_conv_b, m_lru_wa, m_lru_ba, m_lru_wx, m_lru_bx, m_lru_lambda, m_fox_fgate_b, m_mix_norm_g, m_w_out, m_norm2_g, m_w_mlp_in, m_w_mlp_out, m_final_g, v_norm1_g, v_w_in, v_sgu_norm_g, v_sgu_w, v_sgu_b, v_s5_lambda_re, v_s5_lambda_im, v_s5_log_dt, v_s5_b_re, v_s5_b_im, v_s5_c_re, v_s5_c_im, v_s5_d, v_s5_glu_w, v_s5_glu_b, v_lru_conv_w, v_lru_conv_b, v_lru_wa, v_lru_ba, v_lru_wx, v_lru_bx, v_lru_lambda, v_fox_fgate_b, v_mix_norm_g, v_w_out, v_norm2_g, v_w_mlp_in, v_w_mlp_out, v_final_g):
    return _step(x, norm1_g, w_in, sgu_norm_g, sgu_w, sgu_b, s5_lambda_re, s5_lambda_im, s5_log_dt, s5_b_re, s5_b_im, s5_c_re, s5_c_im, s5_d, s5_glu_w, s5_glu_b, lru_conv_w, lru_conv_b, lru_wa, lru_ba, lru_wx, lru_bx, lru_lambda, fox_fgate_b, mix_norm_g, w_out, norm2_g, w_mlp_in, w_mlp_out, final_g, loss_target, m_norm1_g, m_w_in, m_sgu_norm_g, m_sgu_w, m_sgu_b, m_s5_lambda_re, m_s5_lambda_im, m_s5_log_dt, m_s5_b_re, m_s5_b_im, m_s5_c_re, m_s5_c_im, m_s5_d, m_s5_glu_w, m_s5_glu_b, m_lru_conv_w, m_lru_conv_b, m_lru_wa, m_lru_ba, m_lru_wx, m_lru_bx, m_lru_lambda, m_fox_fgate_b, m_mix_norm_g, m_w_out, m_norm2_g, m_w_mlp_in, m_w_mlp_out, m_final_g, v_norm1_g, v_w_in, v_sgu_norm_g, v_sgu_w, v_sgu_b, v_s5_lambda_re, v_s5_lambda_im, v_s5_log_dt, v_s5_b_re, v_s5_b_im, v_s5_c_re, v_s5_c_im, v_s5_d, v_s5_glu_w, v_s5_glu_b, v_lru_conv_w, v_lru_conv_b, v_lru_wa, v_lru_ba, v_lru_wx, v_lru_bx, v_lru_lambda, v_fox_fgate_b, v_mix_norm_g, v_w_out, v_norm2_g, v_w_mlp_in, v_w_mlp_out, v_final_g)
```

```python
import functools
import math

import jax
import jax.numpy as jnp
from jax import lax
from jax.experimental import pallas as pl
from jax.experimental.pallas import tpu as pltpu

F32 = jnp.float32
BF16 = jnp.bfloat16

DEPTH = 4
D_MODEL = 1024
MIXER_WIDTH = 256
SGU_CHUNK = 128
N_HEADS = 4
HEAD_DIM = 64
S5_GROUPS = 16
S5_GROUP = 16
S5_STATE = 64
LRU_C = 8.0
RMS_EPS = 1e-6
D_IN_PROJ = 8 * MIXER_WIDTH + N_HEADS
D_IN_PAD = 8 * MIXER_WIDTH + 128
ADAM_LR, ADAM_B1, ADAM_B2, ADAM_EPS, ADAM_WD, ADAM_STEP = 0.001, 0.9, 0.999, 1e-08, 0.01, 10

V7X_VMEM_BYTES = 64 * 1024 * 1024
VMEM_LIMIT = V7X_VMEM_BYTES - 8 * 1024 * 1024
NEG = -1e30
MESH = pl.DeviceIdType.MESH


def _cp(sem=None, **kw):
    return pltpu.CompilerParams(dimension_semantics=sem, vmem_limit_bytes=VMEM_LIMIT, **kw)


def _full_spec(a):
    nd = a.ndim
    return pl.BlockSpec(a.shape, lambda *_: (0,) * nd)


def _tile(n, pref=512):
    return pref if n % pref == 0 else n


def _dot(a, b, ca, cb):
    return lax.dot_general(a.astype(BF16), b.astype(BF16), (((ca,), (cb,)), ((), ())),
                           preferred_element_type=F32)


def _mm_call(a, b, *, ta=False, tb=False, res=None, name):
    m, k = (a.shape[1], a.shape[0]) if ta else a.shape
    n = b.shape[0] if tb else b.shape[1]
    tm, tn, tk = _tile(m), _tile(n), _tile(k)
    if tn > 1024 or tk > 1024:
        tm = _tile(m, 256)
    nk = k // tk
    ca, cb = (0 if ta else 1), (1 if tb else 0)

    def body(*refs):
        a_ref, b_ref = refs[0], refs[1]
        o_ref, acc = refs[-2], refs[-1]
        kk = pl.program_id(2)

        @pl.when(kk == 0)
        def _():
            acc[...] = jnp.zeros_like(acc)

        acc[...] += _dot(a_ref[...], b_ref[...], ca, cb)

        @pl.when(kk == nk - 1)
        def _():
            out = acc[...]
            if res is not None:
                out = out + refs[2][...]
            o_ref[...] = out

    a_spec = pl.BlockSpec((tk, tm), lambda i, j, kk: (kk, i)) if ta else pl.BlockSpec((tm, tk), lambda i, j, kk: (i, kk))
    b_spec = pl.BlockSpec((tn, tk), lambda i, j, kk: (j, kk)) if tb else pl.BlockSpec((tk, tn), lambda i, j, kk: (kk, j))
    o_spec = pl.BlockSpec((tm, tn), lambda i, j, kk: (i, j))
    ins, specs = [a, b], [a_spec, b_spec]
    if res is not None:
        ins.append(res)
        specs.append(o_spec)
    return pl.pallas_call(
        body, name=name, grid=(m // tm, n // tn, nk), in_specs=specs, out_specs=o_spec,
        out_shape=jax.ShapeDtypeStruct((m, n), F32), scratch_shapes=[pltpu.VMEM((tm, tn), F32)],
        compiler_params=_cp(("parallel", "parallel", "arbitrary")))(*ins)


@jax.custom_vjp
def mm(a, b):
    return _mm_call(a, b, name="mm")


def _mm_fwd(a, b):
    return _mm_call(a, b, name="mm"), (a, b)


def _mm_bwd(r, g):
    a, b = r
    return _mm_call(g, b, tb=True, name="mm_da"), _mm_call(a, g, ta=True, name="mm_db")


mm.defvjp(_mm_fwd, _mm_bwd)


def _matmul(name, grid, a, a_spec, b, b_spec, dims, outs, *, extras=(), epilogue=None, into=None):
    nk = grid[2]
    n_ex, n_out = len(extras), len(outs)
    tm_tn = tuple(d for d in outs[0][1].block_shape if d is not None)[-2:]

    def body(*refs):
        a_ref, b_ref = refs[0], refs[1]
        ex_refs = refs[2:2 + n_ex]
        o_refs = refs[len(refs) - n_out - (nk > 1):len(refs) - (nk > 1)]

        def finish(val):
            res = epilogue(val, *[e[...] for e in ex_refs]) if epilogue else (val,)
            for o_ref, r in zip(o_refs, res):
                o_ref[...] = r.astype(o_ref.dtype)

        if nk == 1:
            finish(_dot(a_ref[...], b_ref[...], *dims))
        else:
            acc = refs[-1]
            kk = pl.program_id(2)

            @pl.when(kk == 0)
            def _():
                acc[...] = jnp.zeros_like(acc)

            acc[...] += _dot(a_ref[...], b_ref[...], *dims)

            @pl.when(kk == nk - 1)
            def _():
                finish(acc[...])

    ins = [a, b] + [e[0] for e in extras]
    specs = [a_spec, b_spec] + [e[1] for e in extras]
    aliases = {}
    if into is not None:
        aliases = {len(ins): 0}
        ins.append(into)
        specs.append(pl.BlockSpec(memory_space=pl.ANY))
    res = pl.pallas_call(
        body, name=name, grid=grid, in_specs=specs, out_specs=[o[1] for o in outs], out_shape=[o[0] for o in outs],
        scratch_shapes=[pltpu.VMEM(tm_tn, F32)] if nk > 1 else [], input_output_aliases=aliases,
        compiler_params=_cp(("arbitrary", "arbitrary", "arbitrary")))(*ins)
    return res[0] if n_out == 1 else res


@jax.custom_vjp
def _bdot(a, b):
    return _dot(a, b, 1, 0)


def _bdot_fwd(a, b):
    return _dot(a, b, 1, 0), (a, b)


def _bdot_bwd(r, g):
    a, b = r
    return _dot(g, b, 1, 1), _dot(a, g, 0, 0)


_bdot.defvjp(_bdot_fwd, _bdot_bwd)


def _row_spec(tr, w):
    return pl.BlockSpec((tr, w), lambda i: (i, 0))


def _rowwise(fn, rows, pars, outs, *, name, tr, dtype=F32):
    t = rows[0].shape[0]
    n_in = len(rows) + len(pars)

    def body(*refs):
        res = fn(*[r[...] for r in refs[:n_in]])
        for o_ref, v in zip(refs[n_in:], res):
            o_ref[...] = v.astype(o_ref.dtype)

    return pl.pallas_call(
        body, name=name, grid=(t // tr,),
        in_specs=[_row_spec(tr, r.shape[1]) for r in rows] + [_full_spec(p) for p in pars],
        out_specs=[_row_spec(tr, w) for w in outs],
        out_shape=[jax.ShapeDtypeStruct((t, w), dtype) for w in outs],
        compiler_params=_cp(("arbitrary",)))(*rows, *pars)


def _rowwise_vjp(fn, rows, pars, cots, *, name, tr, add=None):
    t = rows[0].shape[0]
    nr, npar = len(rows), len(pars)
    cots = list(cots) + ([add] if add is not None else [])
    nc = len(cots)

    def body(*refs):
        vals = [r[...] for r in refs[:nr + npar]]
        cts = [c[...] for c in refs[nr + npar:nr + npar + nc]]
        douts = refs[nr + npar + nc:]
        extra = cts.pop() if add is not None else None
        _, vjp = jax.vjp(fn, *vals)
        grads = list(vjp(tuple(cts)))
        if extra is not None:
            grads[0] = grads[0] + extra
        for kk in range(nr):
            douts[kk][...] = grads[kk]

        @pl.when(pl.program_id(0) == 0)
        def _():
            for kk in range(npar):
                douts[nr + kk][...] = jnp.zeros_like(douts[nr + kk])

        for kk in range(npar):
            douts[nr + kk][...] += grads[nr + kk]

    return pl.pallas_call(
        body, name=name, grid=(t // tr,),
        in_specs=[_row_spec(tr, r.shape[1]) for r in rows] + [_full_spec(p) for p in pars]
        + [_row_spec(tr, c.shape[1]) for c in cots],
        out_specs=[_row_spec(tr, r.shape[1]) for r in rows] + [_full_spec(p) for p in pars],
        out_shape=[jax.ShapeDtypeStruct(r.shape, F32) for r in rows]
        + [jax.ShapeDtypeStruct(p.shape, F32) for p in pars],
        compiler_params=_cp(("arbitrary",)))(*rows, *pars, *cots)


def _make_rw(fn, name, tr, nr, outs):
    @jax.custom_vjp
    def f(*args):
        return tuple(_rowwise(fn, args[:nr], args[nr:], outs, name=name + "_f", tr=tr))

    def fwd(*args):
        return f(*args), args

    def bwd(args, cts):
        return tuple(_rowwise_vjp(fn, args[:nr], args[nr:], list(cts), name=name + "_b", tr=tr))

    f.defvjp(fwd, bwd)
    return f


def _rms(x, g):
    return x * lax.rsqrt(jnp.mean(jnp.square(x), axis=-1, keepdims=True) + RMS_EPS) * g


def _f_rms(x, g):
    return (_rms(x, g),)


def _f_sgu(au, av, ng, w0, w1, w2, w3, bfull):
    u = jax.nn.gelu(au)
    v = _rms(jax.nn.gelu(av), ng)
    tri = lax.broadcasted_iota(jnp.int32, (SGU_CHUNK, SGU_CHUNK), 0) >= lax.broadcasted_iota(
        jnp.int32, (SGU_CHUNK, SGU_CHUNK), 1)
    head = lax.broadcasted_iota(jnp.int32, v.shape, 1) // HEAD_DIM
    mixed = bfull
    for h, w in enumerate((w0, w1, w2, w3)):
        mixed = mixed + _bdot(jnp.where(tri, w, 0.0), jnp.where(head == h, v, 0.0))
    return (u * mixed,)


def _f_s5disc(lam_re, lam_im, log_dt, b_re, b_im):
    dt = jnp.exp(log_dt)
    mag = jnp.exp(lam_re * dt)
    abar_re = mag * jnp.cos(lam_im * dt)
    abar_im = mag * jnp.sin(lam_im * dt)
    denom = jnp.square(lam_re) + jnp.square(lam_im)
    num_re = abar_re - 1.0
    num_im = abar_im
    fac_re = (num_re * lam_re + num_im * lam_im) / denom
    fac_im = (num_im * lam_re - num_re * lam_im) / denom
    return abar_re, abar_im, fac_re * b_re - fac_im * b_im, fac_re * b_im + fac_im * b_re


def _f_s5post(s_re, s_im, u, c_re, c_im, d, gw, gb):
    y = _bdot(s_re, c_re) - _bdot(s_im, c_im) + d * u
    y = jax.nn.gelu(y)
    return (y * jax.nn.sigmoid(_bdot(y, gw) + gb),)


def _f_lrupre(xc, wa, ba, wx, bx, lam):
    r = jax.nn.sigmoid(_bdot(xc, wa) + ba)
    i = jax.nn.sigmoid(_bdot(xc, wx) + bx)
    log_a = -LRU_C * r * jax.nn.softplus(-lam)
    a = jnp.exp(log_a)
    one_minus_a2 = -jnp.tanh(log_a) * (jnp.exp(2.0 * log_a) + 1.0)
    return a, jnp.sqrt(one_minus_a2) * (i * xc)


def _f_lrupost(h, gate):
    return (h * jax.nn.gelu(gate),)


def _f_logsig(zf, bf):
    return (jax.nn.log_sigmoid(zf + bf),)


def _f_gnorm(ya, yb, yc, yd, g):
    def n(y):
        return y * lax.rsqrt(jnp.mean(jnp.square(y), axis=-1, keepdims=True) + RMS_EPS)
    return (jnp.concatenate([n(ya), n(yb), n(yc), n(yd)], axis=1) * g,)


sgu_mix = _make_rw(_f_sgu, "sgu", SGU_CHUNK, 2, [MIXER_WIDTH])
s5_disc = _make_rw(_f_s5disc, "s5disc", S5_GROUPS * S5_GROUP, 5, [S5_STATE] * 4)
s5_post = _make_rw(_f_s5post, "s5post", 256, 3, [MIXER_WIDTH])
lru_pre = _make_rw(_f_lrupre, "lrupre", 512, 1, [MIXER_WIDTH, MIXER_WIDTH])
lru_post = _make_rw(_f_lrupost, "lrupost", 512, 2, [MIXER_WIDTH])
log_sig = _make_rw(_f_logsig, "logsig", 512, 1, [128])


SCAN_TILE = 512


def _prev_spec(c, nt, rev):
    per = SCAN_TILE // 8
    if rev:
        return pl.BlockSpec((8, c), lambda i: (jnp.maximum((nt - 1 - i) * per - 1, 0), 0))
    return pl.BlockSpec((8, c), lambda i: (jnp.maximum(i * per - 1, 0), 0))


def _lti_fwd_call(a_re, a_im, b_re, b_im):
    t, c = b_re.shape
    tt = SCAN_TILE

    def body(ar_ref, ai_ref, br_ref, bi_ref, sr_ref, si_ref, cr, ci):
        @pl.when(pl.program_id(0) == 0)
        def _():
            cr[...] = jnp.zeros_like(cr)
            ci[...] = jnp.zeros_like(ci)

        ar, ai = ar_ref[...], ai_ref[...]

        def step(tt_i, carry):
            hr, hi = carry
            row = pl.ds(tt_i, 1)
            nr = ar * hr - ai * hi + br_ref[row, :]
            ni = ar * hi + ai * hr + bi_ref[row, :]
            sr_ref[row, :] = nr
            si_ref[row, :] = ni
            return nr, ni

        hr, hi = lax.fori_loop(0, tt, step, (cr[...], ci[...]), unroll=8)
        cr[...] = hr
        ci[...] = hi

    row = pl.BlockSpec((tt, c), lambda i: (i, 0))
    par = pl.BlockSpec((1, c), lambda i: (0, 0))
    return pl.pallas_call(
        body, name="lti_scan_f", grid=(t // tt,), in_specs=[par, par, row, row], out_specs=[row, row],
        out_shape=[jax.ShapeDtypeStruct((t, c), F32)] * 2,
        scratch_shapes=[pltpu.VMEM((1, c), F32)] * 2, compiler_params=_cp(("arbitrary",)))(a_re, a_im, b_re, b_im)


def _lti_bwd_call(a_re, a_im, s_re, s_im, g_re, g_im):
    t, c = g_re.shape
    tt = SCAN_TILE
    nt = t // tt

    def body(ar_ref, ai_ref, sr_ref, si_ref, pr_ref, pi_ref, gr_ref, gi_ref,
             or_ref, oi_ref, dar_ref, dai_ref, cr, ci):
        i = pl.program_id(0)

        @pl.when(i == 0)
        def _():
            cr[...] = jnp.zeros_like(cr)
            ci[...] = jnp.zeros_like(ci)
            dar_ref[...] = jnp.zeros_like(dar_ref)
            dai_ref[...] = jnp.zeros_like(dai_ref)

        ar, ai = ar_ref[...], ai_ref[...]

        def one(row, spr, spi, carry):
            gr_c, gi_c, dar, dai = carry
            nr = gr_ref[row, :] + ar * gr_c + ai * gi_c
            ni = gi_ref[row, :] + ar * gi_c - ai * gr_c
            or_ref[row, :] = nr
            oi_ref[row, :] = ni
            return nr, ni, dar + spr * nr + spi * ni, dai + spr * ni - spi * nr

        def step(kk, carry):
            tt_i = tt - 1 - kk
            prev = pl.ds(tt_i - 1, 1)
            return one(pl.ds(tt_i, 1), sr_ref[prev, :], si_ref[prev, :], carry)

        zero = jnp.zeros((1, c), F32)
        carry = lax.fori_loop(0, tt - 1, step, (cr[...], ci[...], zero, zero), unroll=8)
        first = i == nt - 1
        spr = jnp.where(first, 0.0, pr_ref[7:8, :])
        spi = jnp.where(first, 0.0, pi_ref[7:8, :])
        gr_c, gi_c, dar, dai = one(pl.ds(0, 1), spr, spi, carry)
        cr[...] = gr_c
        ci[...] = gi_c
        dar_ref[...] += dar
        dai_ref[...] += dai

    row = pl.BlockSpec((tt, c), lambda i: (nt - 1 - i, 0))
    par = pl.BlockSpec((1, c), lambda i: (0, 0))
    prev = _prev_spec(c, nt, True)
    return pl.pallas_call(
        body, name="lti_scan_b", grid=(nt,), in_specs=[par, par, row, row, prev, prev, row, row],
        out_specs=[row, row, par, par],
        out_shape=[jax.ShapeDtypeStruct((t, c), F32)] * 2 + [jax.ShapeDtypeStruct((1, c), F32)] * 2,
        scratch_shapes=[pltpu.VMEM((1, c), F32)] * 2,
        compiler_params=_cp(("arbitrary",)))(a_re, a_im, s_re, s_im, s_re, s_im, g_re, g_im)


@jax.custom_vjp
def lti_scan(a_re, a_im, b_re, b_im):
    return tuple(_lti_fwd_call(a_re, a_im, b_re, b_im))


def _lti_scan_fwd(a_re, a_im, b_re, b_im):
    s_re, s_im = _lti_fwd_call(a_re, a_im, b_re, b_im)
    return (s_re, s_im), (a_re, a_im, s_re, s_im)


def _lti_scan_bwd(r, g):
    a_re, a_im, s_re, s_im = r
    o_re, o_im, da_re, da_im = _lti_bwd_call(a_re, a_im, s_re, s_im, g[0], g[1])
    return da_re, da_im, o_re, o_im


lti_scan.defvjp(_lti_scan_fwd, _lti_scan_bwd)


def _tv_fwd_call(a, b):
    t, c = b.shape
    tt = SCAN_TILE

    def body(a_ref, b_ref, h_ref, ch):
        @pl.when(pl.program_id(0) == 0)
        def _():
            ch[...] = jnp.zeros_like(ch)

        def step(tt_i, h):
            row = pl.ds(tt_i, 1)
            h = a_ref[row, :] * h + b_ref[row, :]
            h_ref[row, :] = h
            return h

        ch[...] = lax.fori_loop(0, tt, step, ch[...], unroll=8)

    row = pl.BlockSpec((tt, c), lambda i: (i, 0))
    return pl.pallas_call(
        body, name="tv_scan_f", grid=(t // tt,), in_specs=[row, row], out_specs=row,
        out_shape=jax.ShapeDtypeStruct((t, c), F32), scratch_shapes=[pltpu.VMEM((1, c), F32)],
        compiler_params=_cp(("arbitrary",)))(a, b)


def _tv_bwd_call(a, h, g):
    t, c = g.shape
    tt = SCAN_TILE
    nt = t // tt

    def body(a_ref, h_ref, p_ref, g_ref, da_ref, db_ref, cg, ca):
        i = pl.program_id(0)

        @pl.when(i == 0)
        def _():
            cg[...] = jnp.zeros_like(cg)
            ca[...] = jnp.zeros_like(ca)

        def one(row, hp, carry):
            gc, an = carry
            gn = g_ref[row, :] + an * gc
            db_ref[row, :] = gn
            da_ref[row, :] = gn * hp
            return gn, a_ref[row, :]

        def step(kk, carry):
            tt_i = tt - 1 - kk
            return one(pl.ds(tt_i, 1), h_ref[pl.ds(tt_i - 1, 1), :], carry)

        carry = lax.fori_loop(0, tt - 1, step, (cg[...], ca[...]), unroll=8)
        hp = jnp.where(i == nt - 1, 0.0, p_ref[7:8, :])
        gc, an = one(pl.ds(0, 1), hp, carry)
        cg[...] = gc
        ca[...] = an

    row = pl.BlockSpec((tt, c), lambda i: (nt - 1 - i, 0))
    return pl.pallas_call(
        body, name="tv_scan_b", grid=(nt,), in_specs=[row, row, _prev_spec(c, nt, True), row],
        out_specs=[row, row], out_shape=[jax.ShapeDtypeStruct((t, c), F32)] * 2,
        scratch_shapes=[pltpu.VMEM((1, c), F32)] * 2, compiler_params=_cp(("arbitrary",)))(a, h, h, g)


@jax.custom_vjp
def tv_scan(a, b):
    return _tv_fwd_call(a, b)


def _tv_scan_fwd(a, b):
    h = _tv_fwd_call(a, b)
    return h, (a, h)


def _tv_scan_bwd(r, g):
    a, h = r
    return tuple(_tv_bwd_call(a, h, g))


tv_scan.defvjp(_tv_scan_fwd, _tv_scan_bwd)


CONV_K = 4
CONV_ROWS = 512


def _conv_fwd_call(x, w, b):
    t, c = x.shape

    def body(x_ref, w_ref, b_ref, o_ref, xp):
        xp[0:8, :] = jnp.zeros((8, c), F32)
        xp[8:, :] = x_ref[...]
        for blk in range(t // CONV_ROWS):
            base = blk * CONV_ROWS
            acc = jnp.broadcast_to(b_ref[...], (CONV_ROWS, c))
            for kk in range(CONV_K):
                acc = acc + w_ref[kk:kk + 1, :] * xp[base + 5 + kk:base + 5 + kk + CONV_ROWS, :]
            o_ref[base:base + CONV_ROWS, :] = acc

    return pl.pallas_call(
        body, name="conv_f", out_shape=jax.ShapeDtypeStruct((t, c), F32),
        scratch_shapes=[pltpu.VMEM((t + 8, c), F32)], compiler_params=_cp())(x, w, b)


def _conv_bwd_call(x, w, g):
    t, c = x.shape

    def body(x_ref, w_ref, g_ref, dx_ref, dw_ref, db_ref, xp, gp):
        xp[0:8, :] = jnp.zeros((8, c), F32)
        xp[8:, :] = x_ref[...]
        gp[0:t, :] = g_ref[...]
        gp[t:, :] = jnp.zeros((8, c), F32)
        dw = [jnp.zeros((1, c), F32) for _ in range(CONV_K)]
        db = jnp.zeros((1, c), F32)
        for blk in range(t // CONV_ROWS):
            base = blk * CONV_ROWS
            gb = g_ref[base:base + CONV_ROWS, :]
            acc = jnp.zeros((CONV_ROWS, c), F32)
            for kk in range(CONV_K):
                acc = acc + w_ref[kk:kk + 1, :] * gp[base + 3 - kk:base + 3 - kk + CONV_ROWS, :]
                dw[kk] = dw[kk] + jnp.sum(gb * xp[base + 5 + kk:base + 5 + kk + CONV_ROWS, :], axis=0, keepdims=True)
            db = db + jnp.sum(gb, axis=0, keepdims=True)
            dx_ref[base:base + CONV_ROWS, :] = acc
        for kk in range(CONV_K):
            dw_ref[kk:kk + 1, :] = dw[kk]
        db_ref[...] = db

    return pl.pallas_call(
        body, name="conv_b",
        out_shape=[jax.ShapeDtypeStruct((t, c), F32), jax.ShapeDtypeStruct((CONV_K, c), F32),
                   jax.ShapeDtypeStruct((1, c), F32)],
        scratch_shapes=[pltpu.VMEM((t + 8, c), F32)] * 2, compiler_params=_cp())(x, w, g)


@jax.custom_vjp
def causal_conv(x, w, b):
    return _conv_fwd_call(x, w, b)


def _causal_conv_fwd(x, w, b):
    return _conv_fwd_call(x, w, b), (x, w)


def _causal_conv_bwd(r, g):
    return tuple(_conv_bwd_call(r[0], r[1], g))


causal_conv.defvjp(_causal_conv_fwd, _causal_conv_bwd)


ATT_TILE = 512
ATT_SCALE = HEAD_DIM ** -0.5


def _head_lane(val, lane, h):
    return jnp.sum(jnp.where(lane == h, val, 0.0), axis=1, keepdims=True)


def _attn_fwd_call(q, k, v, c128, cr):
    t, w = q.shape
    tq = ATT_TILE
    nq = t // tq
    k3, v3, cr4 = k.reshape(nq, tq, w), v.reshape(nq, tq, w), cr.reshape(N_HEADS, nq, 1, tq)

    def body(q_ref, k_ref, v_ref, c_ref, cr_ref, o_ref, lse_ref):
        i, h = pl.program_id(0), pl.program_id(1)
        hm = lax.broadcasted_iota(jnp.int32, (tq, w), 1) // HEAD_DIM == h
        lane = lax.broadcasted_iota(jnp.int32, (tq, 128), 1)
        qs = jnp.where(hm, q_ref[...] * ATT_SCALE, 0.0).astype(BF16)
        cq = _head_lane(c_ref[...], lane, h)
        row = i * tq + lax.broadcasted_iota(jnp.int32, (tq, tq), 0)
        col = lax.broadcasted_iota(jnp.int32, (tq, tq), 1)

        def step(j, carry):
            m, l, acc = carry
            s = _dot(qs, k_ref[j], 1, 1) + cq - cr_ref[0, j]
            s = jnp.where(row >= j * tq + col, s, NEG)
            m_new = jnp.maximum(m, jnp.max(s, axis=1, keepdims=True))
            p = jnp.exp(s - m_new)
            alpha = jnp.exp(m - m_new)
            return m_new, alpha * l + jnp.sum(p, axis=1, keepdims=True), alpha * acc + _dot(p, v_ref[j], 1, 0)

        init = (jnp.full((tq, 1), NEG, F32), jnp.zeros((tq, 1), F32), jnp.zeros((tq, w), F32))
        m, l, acc = lax.fori_loop(0, i + 1, step, init)
        out = jnp.where(hm, acc / l, 0.0)
        lse = jnp.where(lane == h, m + jnp.log(l), 0.0)

        @pl.when(h == 0)
        def _():
            o_ref[...] = out
            lse_ref[...] = lse

        @pl.when(h > 0)
        def _():
            o_ref[...] += out
            lse_ref[...] += lse

    tile = pl.BlockSpec((tq, w), lambda i, h: (i, 0))
    tile_c = pl.BlockSpec((tq, 128), lambda i, h: (i, 0))
    whole = pl.BlockSpec((nq, tq, w), lambda i, h: (0, 0, 0))
    rows = pl.BlockSpec((1, nq, 1, tq), lambda i, h: (h, 0, 0, 0))
    return pl.pallas_call(
        body, name="attn_f", grid=(nq, N_HEADS), in_specs=[tile, whole, whole, tile_c, rows], out_specs=[tile, tile_c],
        out_shape=[jax.ShapeDtypeStruct((t, w), F32), jax.ShapeDtypeStruct((t, 128), F32)],
        compiler_params=_cp(("arbitrary", "arbitrary")))(q, k3, v3, c128, cr4)


def _attn_bwd_call(q, k, v, c128, cr, o, lse, do):
    t, w = q.shape
    tq = ATT_TILE
    nq = t // tq
    r3 = lambda a: a.reshape(nq, tq, a.shape[-1])
    cr4 = cr.reshape(N_HEADS, nq, 1, tq)

    def body(q_ref, k_ref, v_ref, c_ref, cr_ref, o_ref, lse_ref, do_ref,
             dq_ref, dk_ref, dv_ref, dc_ref, dcr_ref):
        j, h = pl.program_id(0), pl.program_id(1)

        @pl.when((j == 0) & (h == 0))
        def _():
            dq_ref[...] = jnp.zeros_like(dq_ref)
            dc_ref[...] = jnp.zeros_like(dc_ref)

        hm = lax.broadcasted_iota(jnp.int32, (tq, w), 1) // HEAD_DIM == h
        lane = lax.broadcasted_iota(jnp.int32, (tq, 128), 1)
        kj = k_ref[...].astype(BF16)
        vj = v_ref[...].astype(BF16)
        ck = cr_ref[0, 0]
        col = j * tq + lax.broadcasted_iota(jnp.int32, (tq, tq), 1)
        row = lax.broadcasted_iota(jnp.int32, (tq, tq), 0)

        def step(i, carry):
            dk, dv, dck = carry
            qm = jnp.where(hm, q_ref[i], 0.0)
            dom = jnp.where(hm, do_ref[i], 0.0)
            s = _dot(qm * ATT_SCALE, kj, 1, 1) + _head_lane(c_ref[i], lane, h) - ck
            p = jnp.where(i * tq + row >= col, jnp.exp(s - _head_lane(lse_ref[i], lane, h)), 0.0)
            dv = dv + _dot(p, dom, 0, 0)
            dp = _dot(dom, vj, 1, 1)
            delta = jnp.sum(dom * o_ref[i], axis=1, keepdims=True)
            ds = p * (dp - delta)
            dq_ref[i] += jnp.where(hm, _dot(ds, kj, 1, 0), 0.0) * ATT_SCALE
            dk = dk + _dot(ds, qm, 0, 0) * ATT_SCALE
            dc_ref[i] += jnp.where(lane == h, jnp.sum(ds, axis=1, keepdims=True), 0.0)
            return dk, dv, dck - jnp.sum(ds, axis=0, keepdims=True)

        init = (jnp.zeros((tq, w), F32), jnp.zeros((tq, w), F32), jnp.zeros((1, tq), F32))
        dk, dv, dck = lax.fori_loop(j, nq, step, init)
        dcr_ref[0, 0] = dck

        @pl.when(h == 0)
        def _():
            dk_ref[...] = dk
            dv_ref[...] = dv

        @pl.when(h > 0)
        def _():
            dk_ref[...] += dk
            dv_ref[...] += dv

    whole = pl.BlockSpec((nq, tq, w), lambda j, h: (0, 0, 0))
    whole_c = pl.BlockSpec((nq, tq, 128), lambda j, h: (0, 0, 0))
    tile = pl.BlockSpec((None, tq, w), lambda j, h: (j, 0, 0))
    tile_r = pl.BlockSpec((1, 1, 1, tq), lambda j, h: (h, j, 0, 0))
    s3 = jax.ShapeDtypeStruct((nq, tq, w), F32)
    dq, dk, dv, dc, dcr = pl.pallas_call(
        body, name="attn_b", grid=(nq, N_HEADS),
        in_specs=[whole, tile, tile, whole_c, tile_r, whole, whole_c, whole],
        out_specs=[whole, tile, tile, whole_c, tile_r],
        out_shape=[s3, s3, s3, jax.ShapeDtypeStruct((nq, tq, 128), F32),
                   jax.ShapeDtypeStruct((N_HEADS, nq, 1, tq), F32)],
        compiler_params=_cp(("arbitrary", "arbitrary")))(r3(q), r3(k), r3(v), r3(c128), cr4, r3(o), r3(lse), r3(do))
    return dq.reshape(t, w), dk.reshape(t, w), dv.reshape(t, w), dc.reshape(t, 128), dcr.reshape(N_HEADS, 1, t)


@jax.custom_vjp
def fox_attention(q, k, v, c128, cr):
    return _attn_fwd_call(q, k, v, c128, cr)[0]


def _fox_attention_fwd(q, k, v, c128, cr):
    o, lse = _attn_fwd_call(q, k, v, c128, cr)
    return o, (q, k, v, c128, cr, o, lse)


def _fox_attention_bwd(r, g):
    return _attn_bwd_call(*r, g)


fox_attention.defvjp(_fox_attention_fwd, _fox_attention_bwd)


def _loss_call(x, g, target):
    t, d = x.shape
    tr = 512

    def body(x_ref, g_ref, t_ref, loss_ref, dx_ref, dg_ref):
        tgt = t_ref[...]

        def f(xv, gv):
            return 0.5 * jnp.sum(jnp.mean(jnp.square(_rms(xv, gv) - tgt), axis=-1))

        val, vjp = jax.vjp(f, x_ref[...], g_ref[...])
        dx, dg = vjp(jnp.ones((), F32))
        dx_ref[...] = dx

        @pl.when(pl.program_id(0) == 0)
        def _():
            loss_ref[...] = jnp.zeros_like(loss_ref)
            dg_ref[...] = jnp.zeros_like(dg_ref)

        loss_ref[...] += jnp.full(loss_ref.shape, val, F32)
        dg_ref[...] += dg

    row = _row_spec(tr, d)
    return pl.pallas_call(
        body, name="loss_head", grid=(t // tr,), in_specs=[row, _full_spec(g), row],
        out_specs=[pl.BlockSpec((1, 128), lambda i: (0, 0)), row, _full_spec(g)],
        out_shape=[jax.ShapeDtypeStruct((1, 128), F32), jax.ShapeDtypeStruct((t, d), F32),
                   jax.ShapeDtypeStruct(g.shape, F32)],
        compiler_params=_cp(("arbitrary",)))(x, g, target)


def _blockdiag(w):
    g, a, b = w.shape
    return jnp.einsum('gab,gk->gakb', w, jnp.eye(g, dtype=w.dtype)).reshape(g * a, g * b)


def _s5_params(p):
    rep = lambda a: jnp.repeat(a, S5_GROUP, axis=0)
    rows = S5_GROUPS * S5_GROUP
    bt = lambda b: b.transpose(0, 2, 1).reshape(rows, S5_STATE)
    abar_re, abar_im, bb_re, bb_im = s5_disc(
        rep(p["s5_lambda_re"]), rep(p["s5_lambda_im"]), rep(p["s5_log_dt"][:, None]), bt(p["s5_b_re"]), bt(p["s5_b_im"]))
    first = lambda a: a.reshape(S5_GROUPS, S5_GROUP, S5_STATE)[:, 0, :].reshape(1, S5_GROUPS * S5_STATE)
    g3 = lambda a: a.reshape(S5_GROUPS, S5_GROUP, S5_STATE)
    cblk = lambda c: _blockdiag(c.transpose(0, 2, 1))
    return (first(abar_re), first(abar_im), _blockdiag(g3(bb_re)), _blockdiag(g3(bb_im)),
            cblk(p["s5_c_re"]), cblk(p["s5_c_im"]))


MIXER_WEIGHTS = ('sgu_norm_g', 'sgu_w', 'sgu_b', 's5_lambda_re', 's5_lambda_im', 's5_log_dt', 's5_b_re', 's5_b_im',
                 's5_c_re', 's5_c_im', 's5_d', 's5_glu_w', 's5_glu_b', 'lru_conv_w', 'lru_conv_b', 'lru_wa', 'lru_ba',
                 'lru_wx', 'lru_bx', 'lru_lambda', 'fox_fgate_b')


def _mixers(z, p):
    w = MIXER_WIDTH
    row = lambda a: a[None, :]
    a_u, a_v, b_in, c_x, c_gate, d_q, d_k, d_v, d_f = jnp.split(z, [w * i for i in range(1, 9)], axis=1)
    sw = p["sgu_w"]
    (y_a,) = sgu_mix(a_u, a_v, row(p["sgu_norm_g"]), sw[0], sw[1], sw[2], sw[3],
                     jnp.repeat(p["sgu_b"].T, HEAD_DIM, axis=1))
    abar_re, abar_im, bblk_re, bblk_im, cblk_re, cblk_im = _s5_params(p)
    s_re, s_im = lti_scan(abar_re, abar_im, mm(b_in, bblk_re), mm(b_in, bblk_im))
    (y_b,) = s5_post(s_re, s_im, b_in, cblk_re, cblk_im, row(p["s5_d"]), p["s5_glu_w"], row(p["s5_glu_b"]))
    xc = causal_conv(c_x, p["lru_conv_w"], row(p["lru_conv_b"]))
    a, b = lru_pre(xc, _blockdiag(p["lru_wa"]), p["lru_ba"].reshape(1, w), _blockdiag(p["lru_wx"]),
                   p["lru_bx"].reshape(1, w), row(p["lru_lambda"]))
    (y_c,) = lru_post(tv_scan(a, b), c_gate)
    (log_f,) = log_sig(d_f, jnp.pad(p["fox_fgate_b"], (0, 128 - N_HEADS))[None, :])
    c128 = tv_scan(jnp.ones_like(log_f), log_f)
    y_d = fox_attention(d_q, d_k, d_v, c128, c128[:, :N_HEADS].T[:, None, :])
    return y_a, y_b, y_c, y_d


PACK_COLS = 1024
SHARD_SHAPE = {'w_mlp_in': (1024, 1024), 'w_mlp_out': (1024, 1024), 'w_out': (256, 1024), 'w_in': (1024, 513),
               's5_glu_w': (64, 256), 'lru_conv_w': (4, 64)}
SHARDED_AXIS = {'w_in': 1, 's5_glu_w': 0, 'lru_conv_w': 1, 'w_out': 0, 'w_mlp_in': 1, 'w_mlp_out': 0}
SHARD_ROWS = {n: -(-s[0] * s[1] // PACK_COLS) for n, s in SHARD_SHAPE.items()}
SHARD_OFF = {n: sum(list(SHARD_ROWS.values())[:i]) for i, n in enumerate(SHARD_SHAPE)}
LAYER_ROWS = 2880
SMALL_OFF = SHARD_OFF['w_in']
SMALL_ROWS = SHARD_OFF['lru_conv_w'] + SHARD_ROWS['lru_conv_w'] - SMALL_OFF
assert SHARD_OFF['w_mlp_out'] == 1024 and SHARD_OFF['w_out'] == 2048 and SMALL_OFF + SMALL_ROWS <= LAYER_ROWS
PACK_ROWS = DEPTH * LAYER_ROWS
TOK = 1024
FF = 4 * D_MODEL


def _w4(i_of):
    return pl.BlockSpec((None, None, 1024, PACK_COLS), i_of)


def _tile2(rows, cols, i_of):
    return pl.BlockSpec((rows, cols), i_of)


def _layer_fwd(x, l, gathered, w_in, w_out, p, mix_p):
    t = x.shape[0]
    nt = t // TOK
    f32 = lambda r, c: jax.ShapeDtypeStruct((r, c), F32)
    b16 = lambda r, c: jax.ShapeDtypeStruct((r, c), BF16)
    g1, g2, gm = p["norm1_g"][None, :], p["norm2_g"][None, :], p["mix_norm_g"][None, :]
    (h1,) = _rowwise(_f_rms, [x], [g1], [D_MODEL], name="rms_f", tr=512, dtype=BF16)
    z = _matmul("mm_in", (nt, 1, 1), h1, _tile2(TOK, D_MODEL, lambda i, j, k: (i, 0)),
                w_in, _tile2(D_MODEL, D_IN_PAD, lambda i, j, k: (0, 0)), (1, 0),
                [(f32(t, D_IN_PAD), _tile2(TOK, D_IN_PAD, lambda i, j, k: (i, 0)))])
    ys, mix_vjp = jax.vjp(_mixers, z, mix_p)
    (yn,) = _rowwise(_f_gnorm, list(ys), [gm], [D_MODEL], name="gnorm_f", tr=512, dtype=BF16)
    x_tile = _tile2(TOK, D_MODEL, lambda i, j, k: (i, 0))
    x1 = _matmul("mm_out", (nt, 1, 1), yn, x_tile, w_out, _tile2(D_MODEL, D_MODEL, lambda i, j, k: (0, 0)), (1, 0),
                 [(f32(t, D_MODEL), x_tile)], extras=[(x, x_tile)], epilogue=lambda acc, r: (acc + r,))
    (h2,) = _rowwise(_f_rms, [x1], [g2], [D_MODEL], name="rms_f", tr=512, dtype=BF16)
    ff_tile = _tile2(TOK, 1024, lambda i, j, k: (i, j))

    def up_epilogue(acc):
        r = jnp.maximum(acc, 0.0)
        return r * r, r

    act, relu = _matmul("mm_up", (nt, FF // 1024, 1), h2, x_tile, gathered, _w4(lambda i, j, k: (j, l, 0, 0)), (1, 0),
                        [(b16(t, FF), ff_tile), (b16(t, FF), ff_tile)], epilogue=up_epilogue)
    x2 = _matmul("mm_down", (nt, 1, FF // 1024), act, _tile2(TOK, 1024, lambda i, j, k: (i, k)),
                 gathered, _w4(lambda i, j, k: (k, l, 1, 0)), (1, 0),
                 [(f32(t, D_MODEL), x_tile)], extras=[(x1, x_tile)], epilogue=lambda acc, r: (acc + r,))
    return x2, (x, h1, mix_vjp, ys, yn, x1, h2, act, relu)


def _layer_bwd(g, l, res, gathered, w_in, w_out, p, send):
    x, h1, mix_vjp, ys, yn, x1, h2, act, relu = res
    t = x.shape[0]
    nt = t // TOK
    f32 = lambda r, c: jax.ShapeDtypeStruct((r, c), F32)
    g1, g2, gm = p["norm1_g"][None, :], p["norm2_g"][None, :], p["mix_norm_g"][None, :]
    x_tile = _tile2(TOK, D_MODEL, lambda i, j, k: (i, 0))
    ff_tile = _tile2(TOK, 1024, lambda i, j, k: (i, j))
    tok_k = _tile2(TOK, D_MODEL, lambda i, j, k: (k, 0))
    send_s = jax.ShapeDtypeStruct(send.shape, send.dtype)
    du = _matmul("mm_down_dx", (nt, FF // 1024, 1), g, x_tile, gathered, _w4(lambda i, j, k: (j, l, 1, 0)), (1, 1),
                 [(jax.ShapeDtypeStruct((t, FF), BF16), ff_tile)], extras=[(relu, ff_tile)],
                 epilogue=lambda acc, r: (2.0 * r.astype(F32) * acc,))
    send = _matmul("mm_down_dw", (FF // 1024, 1, nt), act, _tile2(TOK, 1024, lambda i, j, k: (k, i)), g, tok_k, (0, 0),
                   [(send_s, _w4(lambda i, j, k: (i, l, 1, 0)))], into=send)
    send = _matmul("mm_up_dw", (1, FF // 1024, nt), h2, tok_k, du, _tile2(TOK, 1024, lambda i, j, k: (k, j)), (0, 0),
                   [(send_s, _w4(lambda i, j, k: (j, l, 0, 0)))], into=send)
    dh2 = _matmul("mm_up_dx", (nt, 1, FF // 1024), du, _tile2(TOK, 1024, lambda i, j, k: (i, k)),
                  gathered, _w4(lambda i, j, k: (k, l, 0, 0)), (1, 1), [(f32(t, D_MODEL), x_tile)])
    g_mid, dg2 = _rowwise_vjp(_f_rms, [x1], [g2], [dh2], name="rms_b", tr=512, add=g)
    w_full = _tile2(D_MODEL, D_MODEL, lambda i, j, k: (0, 0))
    dyn = _matmul("mm_out_dx", (nt, 1, 1), g_mid, x_tile, w_out, w_full, (1, 1), [(f32(t, D_MODEL), x_tile)])
    send = _matmul("mm_out_dw", (4, 1, nt), yn, _tile2(TOK, 256, lambda i, j, k: (k, i)), g_mid, tok_k, (0, 0),
                   [(send_s, pl.BlockSpec((None, None, 256, PACK_COLS),
                                          lambda i, j, k: (i, l, SHARD_OFF['w_out'] // 256, 0)))], into=send)
    *dys, dgm = _rowwise_vjp(_f_gnorm, list(ys), [gm], [dyn], name="gnorm_b", tr=512)
    dz, dmix = mix_vjp(tuple(dys))
    dz = dz.astype(BF16)
    z_tile = _tile2(TOK, D_IN_PAD, lambda i, j, k: (i, 0))
    d_w_in = _matmul("mm_in_dw", (1, 1, t // 512), h1, _tile2(512, D_MODEL, lambda i, j, k: (k, 0)),
                     dz, _tile2(512, D_IN_PAD, lambda i, j, k: (k, 0)), (0, 0),
                     [(f32(D_MODEL, D_IN_PAD), _tile2(D_MODEL, D_IN_PAD, lambda i, j, k: (0, 0)))])
    dh1 = _matmul("mm_in_dx", (nt, 1, 1), dz, z_tile, w_in, _tile2(D_MODEL, D_IN_PAD, lambda i, j, k: (0, 0)), (1, 1),
                  [(f32(t, D_MODEL), x_tile)])
    dx, dg1 = _rowwise_vjp(_f_rms, [x], [g1], [dh1], name="rms_b", tr=512, add=g_mid)
    small = dict(dmix, norm1_g=dg1[0], norm2_g=dg2[0], mix_norm_g=dgm[0])
    return dx, small, d_w_in, send


HBM = pl.BlockSpec(memory_space=pltpu.HBM)
D2D_CHUNKS = 16
ICI_CHUNKS = 4


def _coords():
    return lax.axis_index("x"), lax.axis_index("y"), lax.axis_index("c")


def _other_chips(x, y):
    return [(1 - x, y), (x, 1 - y), (1 - x, 1 - y)]


def _start_chunks(make, rows, n):
    size = rows // n
    assert size * n == rows
    for k in range(n):
        make(pl.ds(k * size, size)).start()


def _allgather_shards(shard):
    r, cols = shard.shape
    rh = r // 2

    def body(in_ref, out_ref, send_sems, recv_sems, local_sem):
        x, y, c = _coords()
        me, sibling = (x, y, c), (x, y, 1 - c)
        chips = _other_chips(x, y)

        def half(px, py, pc, rows=pl.ds(0, rh)):
            return out_ref.at[2 * px + py, pl.ds(pc * rh + rows.start, rows.size), :]

        def copy(k, block, to, rows=pl.ds(0, rh), from_input=False):
            src = in_ref.at[pl.ds(block[2] * rh + rows.start, rows.size), :] if from_input else half(*block, rows)
            return pltpu.make_async_remote_copy(
                src_ref=src, dst_ref=half(*block, rows), send_sem=send_sems.at[k], recv_sem=recv_sems.at[k],
                device_id=to, device_id_type=MESH)

        def mine(rows=pl.ds(0, r)):
            return pltpu.make_async_copy(in_ref.at[rows, :], out_ref.at[2 * x + y, rows, :], local_sem)

        _start_chunks(mine, r, D2D_CHUNKS)
        for j, chip in enumerate(chips):
            _start_chunks(lambda rows: copy(j, me, (*chip, c), rows, from_input=True), rh, ICI_CHUNKS)
        for j, chip in enumerate(chips):
            copy(j, (*chip, c), me).wait_recv()
            _start_chunks(lambda rows: copy(3 + j, (*chip, c), sibling, rows), rh, D2D_CHUNKS)
        for j, chip in enumerate(chips):
            copy(3 + j, (*chip, 1 - c), me).wait_recv()
        for j, chip in enumerate(chips):
            copy(j, me, (*chip, c), from_input=True).wait_send()
            copy(3 + j, (*chip, c), sibling).wait_send()
        mine().wait()

    return pl.pallas_call(
        body, name="allgather_shards", out_shape=jax.ShapeDtypeStruct((4, r, cols), shard.dtype),
        in_specs=[HBM], out_specs=HBM,
        scratch_shapes=[pltpu.SemaphoreType.DMA((6,)), pltpu.SemaphoreType.DMA((6,)), pltpu.SemaphoreType.DMA],
        compiler_params=pltpu.CompilerParams())(shard)


def _pair_exchange(g):
    s, _, rh, cols = g.shape

    def body(g_ref, recv_ref, send_sem, recv_sem):
        x, y, c = _coords()

        def copy(slot, rows):
            return pltpu.make_async_remote_copy(
                src_ref=g_ref.at[slot, 1 - c, rows, :], dst_ref=recv_ref.at[slot, rows, :], send_sem=send_sem,
                recv_sem=recv_sem, device_id=(x, y, 1 - c), device_id_type=MESH)

        for slot in range(s):
            _start_chunks(lambda rows: copy(slot, rows), rh, D2D_CHUNKS // s)
        pltpu.make_async_remote_copy(
            src_ref=g_ref.at[:, 1 - c], dst_ref=recv_ref, send_sem=send_sem, recv_sem=recv_sem,
            device_id=(x, y, 1 - c), device_id_type=MESH).wait()

    return pl.pallas_call(
        body, name="pair_exchange", out_shape=jax.ShapeDtypeStruct((s, rh, cols), g.dtype), in_specs=[HBM],
        out_specs=HBM, scratch_shapes=[pltpu.SemaphoreType.DMA] * 2, compiler_params=pltpu.CompilerParams())(g)


def _chip_exchange(p):
    rh = p.shape[1]

    def body(p_ref, recv_ref, send_sems, recv_sems, local_sem):
        x, y, c = _coords()
        mine = 2 * x + y
        chips = _other_chips(x, y)

        def local(rows=pl.ds(0, rh)):
            return pltpu.make_async_copy(p_ref.at[mine, rows, :], recv_ref.at[mine, rows, :], local_sem)

        def copy(j, src_slot, dst_slot, chip, rows=pl.ds(0, rh)):
            return pltpu.make_async_remote_copy(
                src_ref=p_ref.at[src_slot, rows, :], dst_ref=recv_ref.at[dst_slot, rows, :], send_sem=send_sems.at[j],
                recv_sem=recv_sems.at[j], device_id=(*chip, c), device_id_type=MESH)

        _start_chunks(local, rh, D2D_CHUNKS)
        for j, chip in enumerate(chips):
            _start_chunks(lambda rows: copy(j, 2 * chip[0] + chip[1], mine, chip, rows), rh, ICI_CHUNKS)
        for j, chip in enumerate(chips):
            theirs = 2 * chip[0] + chip[1]
            copy(j, theirs, theirs, chip).wait_recv()
        for j, chip in enumerate(chips):
            copy(j, 2 * chip[0] + chip[1], mine, chip).wait_send()
        local().wait()

    return pl.pallas_call(
        body, name="chip_exchange", out_shape=jax.ShapeDtypeStruct(p.shape, p.dtype), in_specs=[HBM], out_specs=HBM,
        scratch_shapes=[pltpu.SemaphoreType.DMA((3,)), pltpu.SemaphoreType.DMA((3,)), pltpu.SemaphoreType.DMA],
        compiler_params=pltpu.CompilerParams())(p)


def _pair_share(f):
    rh, cols = f.shape

    def body(f_ref, out_ref, send_sem, recv_sem, local_sem):
        x, y, c = _coords()

        def local(rows=pl.ds(0, rh)):
            return pltpu.make_async_copy(f_ref.at[rows, :], out_ref.at[c, rows, :], local_sem)

        def copy(slot, rows=pl.ds(0, rh)):
            return pltpu.make_async_remote_copy(
                src_ref=f_ref.at[rows, :], dst_ref=out_ref.at[slot, rows, :], send_sem=send_sem, recv_sem=recv_sem,
                device_id=(x, y, 1 - c), device_id_type=MESH)

        _start_chunks(local, rh, D2D_CHUNKS)
        _start_chunks(lambda rows: copy(c, rows), rh, D2D_CHUNKS)
        copy(c).wait_send()
        copy(1 - c).wait_recv()
        local().wait()

    return pl.pallas_call(
        body, name="pair_share", out_shape=jax.ShapeDtypeStruct((2, rh, cols), f.dtype), in_specs=[HBM], out_specs=HBM,
        scratch_shapes=[pltpu.SemaphoreType.DMA] * 3,
        compiler_params=pltpu.CompilerParams())(f)


def _allgather_all(blk):
    m_per, cols = blk.shape
    whole = pl.ds(0, m_per)

    def body(x_ref, out_ref, send_sems, recv_sems, local_sem):
        x, y, c = _coords()
        me, sibling = (x, y, c), (x, y, 1 - c)
        chips = _other_chips(x, y)

        def rows_of(px, py, pc, rows):
            return out_ref.at[4 * px + 2 * py + pc, rows, :]

        def copy(k, block, to, rows=whole, from_input=False):
            return pltpu.make_async_remote_copy(
                src_ref=x_ref.at[rows, :] if from_input else rows_of(*block, rows), dst_ref=rows_of(*block, rows),
                send_sem=send_sems.at[k], recv_sem=recv_sems.at[k], device_id=to, device_id_type=MESH)

        mine = pltpu.make_async_copy(x_ref, rows_of(*me, whole), local_sem)
        mine.start()
        _start_chunks(lambda rows: copy(0, me, sibling, rows, from_input=True), m_per, ICI_CHUNKS)
        for j, chip in enumerate(chips):
            _start_chunks(lambda rows: copy(1 + j, me, (*chip, c), rows, from_input=True), m_per, ICI_CHUNKS)
        for j, chip in enumerate(chips):
            copy(1 + j, (*chip, c), me).wait_recv()
            _start_chunks(lambda rows: copy(4 + j, (*chip, c), sibling, rows), m_per, ICI_CHUNKS)
        copy(0, sibling, me).wait_recv()
        for j, chip in enumerate(chips):
            copy(4 + j, (*chip, 1 - c), me).wait_recv()
        copy(0, me, sibling, from_input=True).wait_send()
        for j, chip in enumerate(chips):
            copy(1 + j, me, (*chip, c), from_input=True).wait_send()
            copy(4 + j, (*chip, c), sibling).wait_send()
        mine.wait()

    return pl.pallas_call(
        body, name="allgather_all", out_shape=jax.ShapeDtypeStruct((8, m_per, cols), blk.dtype),
        in_specs=[pl.BlockSpec(memory_space=pltpu.VMEM)], out_specs=pl.BlockSpec(memory_space=pltpu.VMEM),
        scratch_shapes=[pltpu.SemaphoreType.DMA((7,)), pltpu.SemaphoreType.DMA((7,)), pltpu.SemaphoreType.DMA],
        compiler_params=pltpu.CompilerParams(vmem_limit_bytes=VMEM_LIMIT))(blk)


def _add_kept(g, recv, tr):
    s, _, rh, cols = g.shape

    def body(c_ref, a_ref, b_ref, o_ref):
        o_ref[...] = (a_ref[...].astype(F32) + b_ref[...].astype(F32)).astype(o_ref.dtype)

    spec = pl.BlockSpec((None, tr, cols), lambda si, i, c_ref: (si, i, 0))
    return pl.pallas_call(
        body, name="add_kept", out_shape=jax.ShapeDtypeStruct((s, rh, cols), BF16),
        grid_spec=pltpu.PrefetchScalarGridSpec(
            num_scalar_prefetch=1, grid=(s, rh // tr),
            in_specs=[pl.BlockSpec((None, None, tr, cols), lambda si, i, c_ref: (si, c_ref[0], i, 0)), spec],
            out_specs=spec),
        compiler_params=_cp(("arbitrary", "arbitrary")))(lax.axis_index("c").astype(jnp.int32).reshape(1), g, recv)


def _sum_slots(p, tr, name):
    s, rows, cols = p.shape

    def body(p_ref, o_ref):
        acc = p_ref[0].astype(F32)
        for k in range(1, s):
            acc = acc + p_ref[k].astype(F32)
        o_ref[...] = acc

    return pl.pallas_call(
        body, name=name, grid=(rows // tr,), in_specs=[pl.BlockSpec((s, tr, cols), lambda i: (0, i, 0))],
        out_specs=_row_spec(tr, cols), out_shape=jax.ShapeDtypeStruct((rows, cols), F32),
        compiler_params=_cp(("arbitrary",)))(p)


def _adamw_call(w, g, m, v, name):
    rows, cols = w.shape
    tr = _tile(rows, 512) if rows % 512 == 0 else _tile(rows, 128)
    c1 = 1.0 - ADAM_B1 ** ADAM_STEP
    c2 = 1.0 - ADAM_B2 ** ADAM_STEP

    def body(w_ref, g_ref, m_ref, v_ref, d_ref, nm_ref, nv_ref):
        gv = g_ref[...]
        nm = ADAM_B1 * m_ref[...] + (1.0 - ADAM_B1) * gv
        nv = ADAM_B2 * v_ref[...] + (1.0 - ADAM_B2) * jnp.square(gv)
        d_ref[...] = -ADAM_LR * ((nm / c1) / (jnp.sqrt(nv / c2) + ADAM_EPS) + ADAM_WD * w_ref[...])
        nm_ref[...] = nm
        nv_ref[...] = nv

    spec = _row_spec(tr, cols)
    o = jax.ShapeDtypeStruct((rows, cols), F32)
    return pl.pallas_call(body, name=name, grid=(rows // tr,), in_specs=[spec] * 4, out_specs=[spec] * 3,
                          out_shape=[o, o, o], compiler_params=_cp(("arbitrary",)))(w, g, m, v)


WEIGHTS = ('norm1_g', 'w_in', 'sgu_norm_g', 'sgu_w', 'sgu_b', 's5_lambda_re', 's5_lambda_im', 's5_log_dt',
           's5_b_re', 's5_b_im', 's5_c_re', 's5_c_im', 's5_d', 's5_glu_w', 's5_glu_b', 'lru_conv_w',
           'lru_conv_b', 'lru_wa', 'lru_ba', 'lru_wx', 'lru_bx', 'lru_lambda', 'fox_fgate_b', 'mix_norm_g',
           'w_out', 'norm2_g', 'w_mlp_in', 'w_mlp_out', 'final_g')
N_W = len(WEIGHTS)


def _pack_shards(shards, dtype, names=tuple(SHARD_SHAPE), rows=LAYER_ROWS):
    parts = []
    for n in names:
        lead = shards[n].shape[:-2]
        flat = shards[n].reshape(*lead, -1).astype(dtype)
        flat = jnp.pad(flat, [(0, 0)] * len(lead) + [(0, SHARD_ROWS[n] * PACK_COLS - flat.shape[-1])])
        parts.append(flat.reshape(*lead, SHARD_ROWS[n], PACK_COLS))
    lead = parts[0].shape[:-2]
    used = sum(SHARD_ROWS[n] for n in names)
    if rows > used:
        parts.append(jnp.zeros((*lead, rows - used, PACK_COLS), dtype))
    return jnp.concatenate(parts, axis=-2)


def _unpack_shards(buf, names=tuple(SHARD_SHAPE)):
    lead = buf.shape[:-2]
    out = {}
    for n in names:
        s0, s1 = SHARD_SHAPE[n]
        rows, off = SHARD_ROWS[n], SHARD_OFF[n]
        flat = buf[..., off:off + rows, :].reshape(*lead, rows * PACK_COLS)
        out[n] = flat[..., :s0 * s1].reshape(*lead, s0, s1)
    return out


def _join_chips(g, axis):
    _, d, s0, s1 = g.shape
    if axis == 0:
        return g.transpose(1, 0, 2, 3).reshape(d, 4 * s0, s1)
    return g.transpose(1, 2, 0, 3).reshape(d, s0, 4 * s1)


def _split_chips(w, axis):
    d = w.shape[0]
    if axis == 0:
        return w.reshape(d, 4, w.shape[1] // 4, w.shape[2]).transpose(1, 0, 2, 3)
    return w.reshape(d, w.shape[1], 4, w.shape[2] // 4).transpose(2, 0, 1, 3)


def _pack_flat(arrs, rows):
    flat = jnp.concatenate([a.reshape(-1) for a in arrs])
    return jnp.pad(flat, (0, rows * PACK_COLS - flat.shape[0])).reshape(rows, PACK_COLS)


def _unpack_flat(buf, shapes):
    flat, out, off = buf.reshape(-1), [], 0
    for s in shapes:
        n = math.prod(s)
        out.append(flat[off:off + n].reshape(s))
        off += n
    return out


GLUE_PACKED = ('w_in', 's5_glu_w', 'lru_conv_w')


def _forward_backward(x, target, final_g, gathered, rep):
    joined = {n: _join_chips(g, SHARDED_AXIS[n])
              for n, g in _unpack_shards(gathered, GLUE_PACKED + ('w_out',)).items()}
    w_in = jnp.pad(joined['w_in'], ((0, 0), (0, 0), (0, D_IN_PAD - D_IN_PROJ)))
    w_out = joined['w_out']
    norm_p = [{n: rep[n][l] for n in ('norm1_g', 'norm2_g', 'mix_norm_g')} for l in range(DEPTH)]
    mix_p = [{n: joined[n][l].astype(F32) if n in joined else rep[n][l] for n in MIXER_WEIGHTS} for l in range(DEPTH)]

    residuals = []
    for l in range(DEPTH):
        x, res = _layer_fwd(x, l, gathered, w_in[l], w_out[l], norm_p[l], mix_p[l])
        residuals.append(res)
    loss_part, g, d_final = _loss_call(x, final_g[None, :], target)

    send = jnp.zeros((4, DEPTH, LAYER_ROWS, PACK_COLS), BF16)
    small, d_w_in = [None] * DEPTH, [None] * DEPTH
    for l in reversed(range(DEPTH)):
        g, small[l], d_w_in[l], send = _layer_bwd(g, l, residuals[l], gathered, w_in[l], w_out[l], norm_p[l], send)
    stacked = {n: jnp.stack([small[l][n] for l in range(DEPTH)]) for n in small[0]}
    stacked['w_in'] = jnp.stack(d_w_in)[:, :, :D_IN_PROJ]
    glue = _pack_shards({n: _split_chips(stacked.pop(n), SHARDED_AXIS[n]) for n in GLUE_PACKED}, BF16,
                        names=GLUE_PACKED, rows=SMALL_ROWS)
    send = lax.dynamic_update_slice(send, glue, (0, 0, SMALL_OFF, 0))
    return loss_part, g, d_final, stacked, send


def _step(*args):
    x, target = args[0], args[1 + N_W]
    w = dict(zip(WEIGHTS, args[1:1 + N_W]))
    m = dict(zip(WEIGHTS, args[2 + N_W:2 + 2 * N_W]))
    v = dict(zip(WEIGHTS, args[2 + 2 * N_W:2 + 3 * N_W]))
    small = [n for n in WEIGHTS if n not in SHARD_SHAPE]

    packed = _pack_shards({n: w[n] for n in SHARD_SHAPE}, BF16).reshape(PACK_ROWS, PACK_COLS)
    gathered = _allgather_shards(packed).reshape(4, DEPTH, LAYER_ROWS, PACK_COLS)

    loss_part, dx, d_final, dw, send = _forward_backward(
        x[0], target[0], w['final_g'], gathered, {n: w[n] for n in small})

    rh = PACK_ROWS // 2
    halves = send.reshape(4, 2, rh, PACK_COLS)
    chip_sum = _add_kept(halves, _pair_exchange(halves), 1440)
    half = _sum_slots(_chip_exchange(chip_sum), 720, "sum_chips")
    g_shard = _unpack_shards(_pair_share(half).reshape(DEPTH, LAYER_ROWS, PACK_COLS))

    small_g = [d_final.reshape(-1) if n == 'final_g' else dw[n] for n in small]
    n_small = sum(math.prod(w[n].shape) for n in small) + 1
    small_rows = -(-n_small // (128 * PACK_COLS)) * 128
    small_sum = _sum_slots(_allgather_all(_pack_flat(small_g + [loss_part[0, :1]], small_rows)), 128, "sum_devices")
    *g_small, loss = _unpack_flat(small_sum, [w[n].shape for n in small] + [()])

    grads, delta, new_m, new_v = {}, {}, {}, {}
    for n in SHARD_SHAPE:
        shp = w[n].shape
        v2 = lambda a: a.reshape(-1, shp[-1])
        res = _adamw_call(v2(w[n]), v2(g_shard[n]), v2(m[n]), v2(v[n]), "adamw_" + n)
        grads[n] = g_shard[n]
        delta[n], new_m[n], new_v[n] = (r.reshape(shp) for r in res)
    pk = lambda d: _pack_flat([d[n] for n in small], small_rows)
    res = _adamw_call(pk(w), _pack_flat(g_small, small_rows), pk(m), pk(v), "adamw_small")
    shapes = [w[n].shape for n in small]
    for n, g, d_, m_, v_ in zip(small, g_small, *(_unpack_flat(r, shapes) for r in res)):
        grads[n], delta[n], new_m[n], new_v[n] = g, d_, m_, v_

    return (loss, dx[None], *[grads[n] for n in WEIGHTS], *[delta[n] for n in WEIGHTS],
            *[new_m[n] for n in WEIGHTS], *[new_v[n] for n in WEIGHTS])


def kernel(x, norm1_g, w_in, sgu_norm_g, sgu_w, sgu_b, s5_lambda_re, s5_lambda_im, s5_log_dt, s5_b_re, s5_b_im, s5_c_re, s5_c_im, s5_d, s5_glu_w, s5_glu_b, lru_conv_w, lru_conv_b, lru_wa, lru_ba, lru_wx, lru_bx, lru_lambda, fox_fgate_b, mix_norm_g, w_out, norm2_g, w_mlp_in, w_mlp_out, final_g, loss_target, m_norm1_g, m_w_in, m_sgu_norm_g, m_sgu_w, m_sgu_b, m_s5_lambda_re, m_s5_lambda_im, m_s5_log_dt, m_s5_b_re, m_s5_b_im, m_s5_c_re, m_s5_c_im, m_s5_d, m_s5_glu_w, m_s5_glu_b, m_lru_conv_w, m_lru_conv_b, m_lru_wa, m_lru_ba, m_lru_wx, m_lru_bx, m_lru_lambda, m_fox_fgate_b, m_mix_norm_g, m_w_out, m_norm2_g, m_w_mlp_in, m_w_mlp_out, m_final_g, v_norm1_g, v_w_in, v_sgu_norm_g, v_sgu_w, v_sgu_b, v_s5_lambda_re, v_s5_lambda_im, v_s5_log_dt, v_s5_b_re, v_s5_b_im, v_s5_c_re, v_s5_c_im, v_s5_d, v_s5_glu_w, v_s5_glu_b, v_lru_conv_w, v_lru_conv_b, v_lru_wa, v_lru_ba, v_lru_wx, v_lru_bx, v_lru_lambda, v_fox_fgate_b, v_mix_norm_g, v_w_out, v_norm2_g, v_w_mlp_in, v_w_mlp_out, v_final_g):
    return _step(x, norm1_g, w_in, sgu_norm_g, sgu_w, sgu_b, s5_lambda_re, s5_lambda_im, s5_log_dt, s5_b_re, s5_b_im, s5_c_re, s5_c_im, s5_d, s5_glu_w, s5_glu_b, lru_conv_w, lru_conv_b, lru_wa, lru_ba, lru_wx, lru_bx, lru_lambda, fox_fgate_b, mix_norm_g, w_out, norm2_g, w_mlp_in, w_mlp_out, final_g, loss_target, m_norm1_g, m_w_in, m_sgu_norm_g, m_sgu_w, m_sgu_b, m_s5_lambda_re, m_s5_lambda_im, m_s5_log_dt, m_s5_b_re, m_s5_b_im, m_s5_c_re, m_s5_c_im, m_s5_d, m_s5_glu_w, m_s5_glu_b, m_lru_conv_w, m_lru_conv_b, m_lru_wa, m_lru_ba, m_lru_wx, m_lru_bx, m_lru_lambda, m_fox_fgate_b, m_mix_norm_g, m_w_out, m_norm2_g, m_w_mlp_in, m_w_mlp_out, m_final_g, v_norm1_g, v_w_in, v_sgu_norm_g, v_sgu_w, v_sgu_b, v_s5_lambda_re, v_s5_lambda_im, v_s5_log_dt, v_s5_b_re, v_s5_b_im, v_s5_c_re, v_s5_c_im, v_s5_d, v_s5_glu_w, v_s5_glu_b, v_lru_conv_w, v_lru_conv_b, v_lru_wa, v_lru_ba, v_lru_wx, v_lru_bx, v_lru_lambda, v_fox_fgate_b, v_mix_norm_g, v_w_out, v_norm2_g, v_w_mlp_in, v_w_mlp_out, v_final_g)
```

```python
import functools
import math

import jax
import jax.numpy as jnp
from jax import lax
from jax.experimental import pallas as pl
from jax.experimental.pallas import tpu as pltpu

F32 = jnp.float32
BF16 = jnp.bfloat16

DEPTH = 4
D_MODEL = 1024
MIXER_WIDTH = 256
SGU_CHUNK = 128
N_HEADS = 4
HEAD_DIM = 64
S5_GROUPS = 16
S5_GROUP = 16
S5_STATE = 64
LRU_C = 8.0
RMS_EPS = 1e-6
D_IN_PROJ = 8 * MIXER_WIDTH + N_HEADS
D_IN_PAD = 8 * MIXER_WIDTH + 128
ADAM_LR, ADAM_B1, ADAM_B2, ADAM_EPS, ADAM_WD, ADAM_STEP = 0.001, 0.9, 0.999, 1e-08, 0.01, 10

V7X_VMEM_BYTES = 64 * 1024 * 1024
VMEM_LIMIT = V7X_VMEM_BYTES - 8 * 1024 * 1024
NEG = -1e30
MESH = pl.DeviceIdType.MESH


def _cp(sem=None, **kw):
    return pltpu.CompilerParams(dimension_semantics=sem, vmem_limit_bytes=VMEM_LIMIT, **kw)


def _full_spec(a):
    nd = a.ndim
    return pl.BlockSpec(a.shape, lambda *_: (0,) * nd)


def _tile(n, pref=512):
    return pref if n % pref == 0 else n


def _dot(a, b, ca, cb):
    return lax.dot_general(a.astype(BF16), b.astype(BF16), (((ca,), (cb,)), ((), ())),
                           preferred_element_type=F32)


def _mm_call(a, b, *, ta=False, tb=False, res=None, name):
    m, k = (a.shape[1], a.shape[0]) if ta else a.shape
    n = b.shape[0] if tb else b.shape[1]
    tm, tn, tk = _tile(m), _tile(n), _tile(k)
    if tn > 1024 or tk > 1024:
        tm = _tile(m, 256)
    nk = k // tk
    ca, cb = (0 if ta else 1), (1 if tb else 0)

    def body(*refs):
        a_ref, b_ref = refs[0], refs[1]
        o_ref, acc = refs[-2], refs[-1]
        kk = pl.program_id(2)

        @pl.when(kk == 0)
        def _():
            acc[...] = jnp.zeros_like(acc)

        acc[...] += _dot(a_ref[...], b_ref[...], ca, cb)

        @pl.when(kk == nk - 1)
        def _():
            out = acc[...]
            if res is not None:
                out = out + refs[2][...]
            o_ref[...] = out

    a_spec = pl.BlockSpec((tk, tm), lambda i, j, kk: (kk, i)) if ta else pl.BlockSpec((tm, tk), lambda i, j, kk: (i, kk))
    b_spec = pl.BlockSpec((tn, tk), lambda i, j, kk: (j, kk)) if tb else pl.BlockSpec((tk, tn), lambda i, j, kk: (kk, j))
    o_spec = pl.BlockSpec((tm, tn), lambda i, j, kk: (i, j))
    ins, specs = [a, b], [a_spec, b_spec]
    if res is not None:
        ins.append(res)
        specs.append(o_spec)
    return pl.pallas_call(
        body, name=name, grid=(m // tm, n // tn, nk), in_specs=specs, out_specs=o_spec,
        out_shape=jax.ShapeDtypeStruct((m, n), F32), scratch_shapes=[pltpu.VMEM((tm, tn), F32)],
        compiler_params=_cp(("parallel", "parallel", "arbitrary")))(*ins)


@jax.custom_vjp
def mm(a, b):
    return _mm_call(a, b, name="mm")


def _mm_fwd(a, b):
    return _mm_call(a, b, name="mm"), (a, b)


def _mm_bwd(r, g):
    a, b = r
    return _mm_call(g, b, tb=True, name="mm_da"), _mm_call(a, g, ta=True, name="mm_db")


mm.defvjp(_mm_fwd, _mm_bwd)


def _matmul(name, grid, a, a_spec, b, b_spec, dims, outs, *, extras=(), epilogue=None, into=None):
    nk = grid[2]
    n_ex, n_out = len(extras), len(outs)
    tm_tn = tuple(d for d in outs[0][1].block_shape if d is not None)[-2:]

    def body(*refs):
        a_ref, b_ref = refs[0], refs[1]
        ex_refs = refs[2:2 + n_ex]
        o_refs = refs[len(refs) - n_out - (nk > 1):len(refs) - (nk > 1)]

        def finish(val):
            res = epilogue(val, *[e[...] for e in ex_refs]) if epilogue else (val,)
            for o_ref, r in zip(o_refs, res):
                o_ref[...] = r.astype(o_ref.dtype)

        if nk == 1:
            finish(_dot(a_ref[...], b_ref[...], *dims))
        else:
            acc = refs[-1]
            kk = pl.program_id(2)

            @pl.when(kk == 0)
            def _():
                acc[...] = jnp.zeros_like(acc)

            acc[...] += _dot(a_ref[...], b_ref[...], *dims)

            @pl.when(kk == nk - 1)
            def _():
                finish(acc[...])

    ins = [a, b] + [e[0] for e in extras]
    specs = [a_spec, b_spec] + [e[1] for e in extras]
    aliases = {}
    if into is not None:
        aliases = {len(ins): 0}
        ins.append(into)
        specs.append(pl.BlockSpec(memory_space=pl.ANY))
    res = pl.pallas_call(
        body, name=name, grid=grid, in_specs=specs, out_specs=[o[1] for o in outs], out_shape=[o[0] for o in outs],
        scratch_shapes=[pltpu.VMEM(tm_tn, F32)] if nk > 1 else [], input_output_aliases=aliases,
        compiler_params=_cp(("arbitrary", "arbitrary", "arbitrary")))(*ins)
    return res[0] if n_out == 1 else res


@jax.custom_vjp
def _bdot(a, b):
    return _dot(a, b, 1, 0)


def _bdot_fwd(a, b):
    return _dot(a, b, 1, 0), (a, b)


def _bdot_bwd(r, g):
    a, b = r
    return _dot(g, b, 1, 1), _dot(a, g, 0, 0)


_bdot.defvjp(_bdot_fwd, _bdot_bwd)


def _row_spec(tr, w):
    return pl.BlockSpec((tr, w), lambda i: (i, 0))


def _rowwise(fn, rows, pars, outs, *, name, tr, dtype=F32):
    t = rows[0].shape[0]
    n_in = len(rows) + len(pars)

    def body(*refs):
        res = fn(*[r[...] for r in refs[:n_in]])
        for o_ref, v in zip(refs[n_in:], res):
            o_ref[...] = v.astype(o_ref.dtype)

    return pl.pallas_call(
        body, name=name, grid=(t // tr,),
        in_specs=[_row_spec(tr, r.shape[1]) for r in rows] + [_full_spec(p) for p in pars],
        out_specs=[_row_spec(tr, w) for w in outs],
        out_shape=[jax.ShapeDtypeStruct((t, w), dtype) for w in outs],
        compiler_params=_cp(("arbitrary",)))(*rows, *pars)


def _rowwise_vjp(fn, rows, pars, cots, *, name, tr, add=None):
    t = rows[0].shape[0]
    nr, npar = len(rows), len(pars)
    cots = list(cots) + ([add] if add is not None else [])
    nc = len(cots)

    def body(*refs):
        vals = [r[...] for r in refs[:nr + npar]]
        cts = [c[...] for c in refs[nr + npar:nr + npar + nc]]
        douts = refs[nr + npar + nc:]
        extra = cts.pop() if add is not None else None
        _, vjp = jax.vjp(fn, *vals)
        grads = list(vjp(tuple(cts)))
        if extra is not None:
            grads[0] = grads[0] + extra
        for kk in range(nr):
            douts[kk][...] = grads[kk]

        @pl.when(pl.program_id(0) == 0)
        def _():
            for kk in range(npar):
                douts[nr + kk][...] = jnp.zeros_like(douts[nr + kk])

        for kk in range(npar):
            douts[nr + kk][...] += grads[nr + kk]

    return pl.pallas_call(
        body, name=name, grid=(t // tr,),
        in_specs=[_row_spec(tr, r.shape[1]) for r in rows] + [_full_spec(p) for p in pars]
        + [_row_spec(tr, c.shape[1]) for c in cots],
        out_specs=[_row_spec(tr, r.shape[1]) for r in rows] + [_full_spec(p) for p in pars],
        out_shape=[jax.ShapeDtypeStruct(r.shape, F32) for r in rows]
        + [jax.ShapeDtypeStruct(p.shape, F32) for p in pars],
        compiler_params=_cp(("arbitrary",)))(*rows, *pars, *cots)


def _make_rw(fn, name, tr, nr, outs):
    @jax.custom_vjp
    def f(*args):
        return tuple(_rowwise(fn, args[:nr], args[nr:], outs, name=name + "_f", tr=tr))

    def fwd(*args):
        return f(*args), args

    def bwd(args, cts):
        return tuple(_rowwise_vjp(fn, args[:nr], args[nr:], list(cts), name=name + "_b", tr=tr))

    f.defvjp(fwd, bwd)
    return f


def _rms(x, g):
    return x * lax.rsqrt(jnp.mean(jnp.square(x), axis=-1, keepdims=True) + RMS_EPS) * g


def _f_rms(x, g):
    return (_rms(x, g),)


def _f_sgu(au, av, ng, w0, w1, w2, w3, bfull):
    u = jax.nn.gelu(au)
    v = _rms(jax.nn.gelu(av), ng)
    tri = lax.broadcasted_iota(jnp.int32, (SGU_CHUNK, SGU_CHUNK), 0) >= lax.broadcasted_iota(
        jnp.int32, (SGU_CHUNK, SGU_CHUNK), 1)
    head = lax.broadcasted_iota(jnp.int32, v.shape, 1) // HEAD_DIM
    mixed = bfull
    for h, w in enumerate((w0, w1, w2, w3)):
        mixed = mixed + _bdot(jnp.where(tri, w, 0.0), jnp.where(head == h, v, 0.0))
    return (u * mixed,)


def _f_s5disc(lam_re, lam_im, log_dt, b_re, b_im):
    dt = jnp.exp(log_dt)
    mag = jnp.exp(lam_re * dt)
    abar_re = mag * jnp.cos(lam_im * dt)
    abar_im = mag * jnp.sin(lam_im * dt)
    denom = jnp.square(lam_re) + jnp.square(lam_im)
    num_re = abar_re - 1.0
    num_im = abar_im
    fac_re = (num_re * lam_re + num_im * lam_im) / denom
    fac_im = (num_im * lam_re - num_re * lam_im) / denom
    return abar_re, abar_im, fac_re * b_re - fac_im * b_im, fac_re * b_im + fac_im * b_re


def _f_s5post(s_re, s_im, u, c_re, c_im, d, gw, gb):
    y = _bdot(s_re, c_re) - _bdot(s_im, c_im) + d * u
    y = jax.nn.gelu(y)
    return (y * jax.nn.sigmoid(_bdot(y, gw) + gb),)


def _f_lrupre(xc, wa, ba, wx, bx, lam):
    r = jax.nn.sigmoid(_bdot(xc, wa) + ba)
    i = jax.nn.sigmoid(_bdot(xc, wx) + bx)
    log_a = -LRU_C * r * jax.nn.softplus(-lam)
    a = jnp.exp(log_a)
    one_minus_a2 = -jnp.tanh(log_a) * (jnp.exp(2.0 * log_a) + 1.0)
    return a, jnp.sqrt(one_minus_a2) * (i * xc)


def _f_lrupost(h, gate):
    return (h * jax.nn.gelu(gate),)


def _f_logsig(zf, bf):
    return (jax.nn.log_sigmoid(zf + bf),)


def _f_gnorm(ya, yb, yc, yd, g):
    def n(y):
        return y * lax.rsqrt(jnp.mean(jnp.square(y), axis=-1, keepdims=True) + RMS_EPS)
    return (jnp.concatenate([n(ya), n(yb), n(yc), n(yd)], axis=1) * g,)


sgu_mix = _make_rw(_f_sgu, "sgu", SGU_CHUNK, 2, [MIXER_WIDTH])
s5_disc = _make_rw(_f_s5disc, "s5disc", S5_GROUPS * S5_GROUP, 5, [S5_STATE] * 4)
s5_post = _make_rw(_f_s5post, "s5post", 256, 3, [MIXER_WIDTH])
lru_pre = _make_rw(_f_lrupre, "lrupre", 512, 1, [MIXER_WIDTH, MIXER_WIDTH])
lru_post = _make_rw(_f_lrupost, "lrupost", 512, 2, [MIXER_WIDTH])
log_sig = _make_rw(_f_logsig, "logsig", 512, 1, [128])


SCAN_TILE = 512


def _prev_spec(c, nt, rev):
    per = SCAN_TILE // 8
    if rev:
        return pl.BlockSpec((8, c), lambda i: (jnp.maximum((nt - 1 - i) * per - 1, 0), 0))
    return pl.BlockSpec((8, c), lambda i: (jnp.maximum(i * per - 1, 0), 0))


SCAN_STEPS = (1, 2, 4)


def _cmul(ar, ai, br, bi):
    return ar * br - ai * bi, ar * bi + ai * br


def _rows_down(x, k, fill, rowid):
    return jnp.where(rowid >= k, pltpu.roll(x, k, 0), fill)


def _rows_up(x, k, fill, rowid):
    return jnp.where(rowid < 8 - k, pltpu.roll(x, 8 - k, 0), fill)


def _powers(ar, ai):
    pw = [(ar, ai)]
    for _ in range(7):
        pw.append(_cmul(*pw[-1], ar, ai))
    return pw


def _block(i):
    return pl.ds(pl.multiple_of(i * 8, 8), 8)


def _row_before(ref, i, edge):
    return jnp.where(i == 0, edge, ref[pl.ds(jnp.maximum(i * 8 - 1, 0), 1), :])


def _lti_fwd_call(a_re, a_im, b_re, b_im):
    t, c = b_re.shape
    tt = SCAN_TILE

    def body(ar_ref, ai_ref, br_ref, bi_ref, sr_ref, si_ref, cr, ci):
        @pl.when(pl.program_id(0) == 0)
        def _():
            cr[...] = jnp.zeros_like(cr)
            ci[...] = jnp.zeros_like(ci)

        pw = _powers(ar_ref[...], ai_ref[...])
        apr = jnp.concatenate([p[0] for p in pw], axis=0)
        api = jnp.concatenate([p[1] for p in pw], axis=0)
        rowid = lax.broadcasted_iota(jnp.int32, (8, c), 0)

        def block(i, carry):
            xr, xi = br_ref[_block(i), :], bi_ref[_block(i), :]
            for k in SCAN_STEPS:
                dr, di = _cmul(*pw[k - 1], _rows_down(xr, k, 0.0, rowid), _rows_down(xi, k, 0.0, rowid))
                xr, xi = xr + dr, xi + di
            dr, di = _cmul(apr, api, *carry)
            xr, xi = xr + dr, xi + di
            sr_ref[_block(i), :] = xr
            si_ref[_block(i), :] = xi
            return xr[7:8, :], xi[7:8, :]

        hr, hi = lax.fori_loop(0, tt // 8, block, (cr[...], ci[...]), unroll=2)
        cr[...] = hr
        ci[...] = hi

    row = pl.BlockSpec((tt, c), lambda i: (i, 0))
    par = pl.BlockSpec((1, c), lambda i: (0, 0))
    return pl.pallas_call(
        body, name="lti_scan_f", grid=(t // tt,), in_specs=[par, par, row, row], out_specs=[row, row],
        out_shape=[jax.ShapeDtypeStruct((t, c), F32)] * 2,
        scratch_shapes=[pltpu.VMEM((1, c), F32)] * 2, compiler_params=_cp(("arbitrary",)))(a_re, a_im, b_re, b_im)


def _lti_bwd_call(a_re, a_im, s_re, s_im, g_re, g_im):
    t, c = g_re.shape
    tt = SCAN_TILE
    nt = t // tt
    nb = tt // 8

    def body(ar_ref, ai_ref, sr_ref, si_ref, pr_ref, pi_ref, gr_ref, gi_ref,
             or_ref, oi_ref, dar_ref, dai_ref, cr, ci):
        ti = pl.program_id(0)

        @pl.when(ti == 0)
        def _():
            cr[...] = jnp.zeros_like(cr)
            ci[...] = jnp.zeros_like(ci)
            dar_ref[...] = jnp.zeros_like(dar_ref)
            dai_ref[...] = jnp.zeros_like(dai_ref)

        pw = _powers(ar_ref[...], -ai_ref[...])
        tpr = jnp.concatenate([p[0] for p in reversed(pw)], axis=0)
        tpi = jnp.concatenate([p[1] for p in reversed(pw)], axis=0)
        rowid = lax.broadcasted_iota(jnp.int32, (8, c), 0)
        first = ti == nt - 1
        edge_r = jnp.where(first, 0.0, pr_ref[7:8, :])
        edge_i = jnp.where(first, 0.0, pi_ref[7:8, :])

        def block(kk, carry):
            i = nb - 1 - kk
            gr_c, gi_c, acc_r, acc_i = carry
            xr, xi = gr_ref[_block(i), :], gi_ref[_block(i), :]
            for k in SCAN_STEPS:
                dr, di = _cmul(*pw[k - 1], _rows_up(xr, k, 0.0, rowid), _rows_up(xi, k, 0.0, rowid))
                xr, xi = xr + dr, xi + di
            dr, di = _cmul(tpr, tpi, gr_c, gi_c)
            xr, xi = xr + dr, xi + di
            or_ref[_block(i), :] = xr
            oi_ref[_block(i), :] = xi
            spr = _rows_down(sr_ref[_block(i), :], 1, _row_before(sr_ref, i, edge_r), rowid)
            spi = _rows_down(si_ref[_block(i), :], 1, _row_before(si_ref, i, edge_i), rowid)
            return xr[0:1, :], xi[0:1, :], acc_r + spr * xr + spi * xi, acc_i + spr * xi - spi * xr

        zero = jnp.zeros((8, c), F32)
        gr_c, gi_c, acc_r, acc_i = lax.fori_loop(0, nb, block, (cr[...], ci[...], zero, zero), unroll=2)
        cr[...] = gr_c
        ci[...] = gi_c
        dar_ref[...] += jnp.sum(acc_r, axis=0, keepdims=True)
        dai_ref[...] += jnp.sum(acc_i, axis=0, keepdims=True)

    row = pl.BlockSpec((tt, c), lambda i: (nt - 1 - i, 0))
    par = pl.BlockSpec((1, c), lambda i: (0, 0))
    prev = _prev_spec(c, nt, True)
    return pl.pallas_call(
        body, name="lti_scan_b", grid=(nt,), in_specs=[par, par, row, row, prev, prev, row, row],
        out_specs=[row, row, par, par],
        out_shape=[jax.ShapeDtypeStruct((t, c), F32)] * 2 + [jax.ShapeDtypeStruct((1, c), F32)] * 2,
        scratch_shapes=[pltpu.VMEM((1, c), F32)] * 2,
        compiler_params=_cp(("arbitrary",)))(a_re, a_im, s_re, s_im, s_re, s_im, g_re, g_im)


@jax.custom_vjp
def lti_scan(a_re, a_im, b_re, b_im):
    return tuple(_lti_fwd_call(a_re, a_im, b_re, b_im))


def _lti_scan_fwd(a_re, a_im, b_re, b_im):
    s_re, s_im = _lti_fwd_call(a_re, a_im, b_re, b_im)
    return (s_re, s_im), (a_re, a_im, s_re, s_im)


def _lti_scan_bwd(r, g):
    a_re, a_im, s_re, s_im = r
    o_re, o_im, da_re, da_im = _lti_bwd_call(a_re, a_im, s_re, s_im, g[0], g[1])
    return da_re, da_im, o_re, o_im


lti_scan.defvjp(_lti_scan_fwd, _lti_scan_bwd)


def _tv_fwd_call(a, b):
    t, c = b.shape
    tt = SCAN_TILE

    def body(a_ref, b_ref, h_ref, ch):
        @pl.when(pl.program_id(0) == 0)
        def _():
            ch[...] = jnp.zeros_like(ch)

        rowid = lax.broadcasted_iota(jnp.int32, (8, c), 0)

        def block(i, h):
            ab, x = a_ref[_block(i), :], b_ref[_block(i), :]
            for k in SCAN_STEPS:
                x = x + ab * _rows_down(x, k, 0.0, rowid)
                ab = ab * _rows_down(ab, k, 1.0, rowid)
            x = x + ab * h
            h_ref[_block(i), :] = x
            return x[7:8, :]

        ch[...] = lax.fori_loop(0, tt // 8, block, ch[...], unroll=2)

    row = pl.BlockSpec((tt, c), lambda i: (i, 0))
    return pl.pallas_call(
        body, name="tv_scan_f", grid=(t // tt,), in_specs=[row, row], out_specs=row,
        out_shape=jax.ShapeDtypeStruct((t, c), F32), scratch_shapes=[pltpu.VMEM((1, c), F32)],
        compiler_params=_cp(("arbitrary",)))(a, b)


def _tv_bwd_call(a, h, g):
    t, c = g.shape
    tt = SCAN_TILE
    nt = t // tt
    nb = tt // 8

    def body(a_ref, h_ref, p_ref, g_ref, da_ref, db_ref, cg, ca):
        ti = pl.program_id(0)

        @pl.when(ti == 0)
        def _():
            cg[...] = jnp.zeros_like(cg)
            ca[...] = jnp.zeros_like(ca)

        rowid = lax.broadcasted_iota(jnp.int32, (8, c), 0)
        edge = jnp.where(ti == nt - 1, 0.0, p_ref[7:8, :])

        def block(kk, carry):
            i = nb - 1 - kk
            gc, a_next = carry
            ab, x = a_ref[_block(i), :], g_ref[_block(i), :]
            cb = _rows_up(ab, 1, a_next, rowid)
            for k in SCAN_STEPS:
                x = x + cb * _rows_up(x, k, 0.0, rowid)
                cb = cb * _rows_up(cb, k, 1.0, rowid)
            x = x + cb * gc
            db_ref[_block(i), :] = x
            da_ref[_block(i), :] = x * _rows_down(h_ref[_block(i), :], 1, _row_before(h_ref, i, edge), rowid)
            return x[0:1, :], ab[0:1, :]

        gc, a_next = lax.fori_loop(0, nb, block, (cg[...], ca[...]), unroll=2)
        cg[...] = gc
        ca[...] = a_next

    row = pl.BlockSpec((tt, c), lambda i: (nt - 1 - i, 0))
    return pl.pallas_call(
        body, name="tv_scan_b", grid=(nt,), in_specs=[row, row, _prev_spec(c, nt, True), row],
        out_specs=[row, row], out_shape=[jax.ShapeDtypeStruct((t, c), F32)] * 2,
        scratch_shapes=[pltpu.VMEM((1, c), F32)] * 2, compiler_params=_cp(("arbitrary",)))(a, h, h, g)


@jax.custom_vjp
def tv_scan(a, b):
    return _tv_fwd_call(a, b)


def _tv_scan_fwd(a, b):
    h = _tv_fwd_call(a, b)
    return h, (a, h)


def _tv_scan_bwd(r, g):
    a, h = r
    return tuple(_tv_bwd_call(a, h, g))


tv_scan.defvjp(_tv_scan_fwd, _tv_scan_bwd)


CONV_K = 4
CONV_ROWS = 512


def _conv_fwd_call(x, w, b):
    t, c = x.shape

    def body(x_ref, w_ref, b_ref, o_ref, xp):
        xp[0:8, :] = jnp.zeros((8, c), F32)
        xp[8:, :] = x_ref[...]
        for blk in range(t // CONV_ROWS):
            base = blk * CONV_ROWS
            acc = jnp.broadcast_to(b_ref[...], (CONV_ROWS, c))
            for kk in range(CONV_K):
                acc = acc + w_ref[kk:kk + 1, :] * xp[base + 5 + kk:base + 5 + kk + CONV_ROWS, :]
            o_ref[base:base + CONV_ROWS, :] = acc

    return pl.pallas_call(
        body, name="conv_f", out_shape=jax.ShapeDtypeStruct((t, c), F32),
        scratch_shapes=[pltpu.VMEM((t + 8, c), F32)], compiler_params=_cp())(x, w, b)


def _conv_bwd_call(x, w, g):
    t, c = x.shape

    def body(x_ref, w_ref, g_ref, dx_ref, dw_ref, db_ref, xp, gp):
        xp[0:8, :] = jnp.zeros((8, c), F32)
        xp[8:, :] = x_ref[...]
        gp[0:t, :] = g_ref[...]
        gp[t:, :] = jnp.zeros((8, c), F32)
        dw = [jnp.zeros((1, c), F32) for _ in range(CONV_K)]
        db = jnp.zeros((1, c), F32)
        for blk in range(t // CONV_ROWS):
            base = blk * CONV_ROWS
            gb = g_ref[base:base + CONV_ROWS, :]
            acc = jnp.zeros((CONV_ROWS, c), F32)
            for kk in range(CONV_K):
                acc = acc + w_ref[kk:kk + 1, :] * gp[base + 3 - kk:base + 3 - kk + CONV_ROWS, :]
                dw[kk] = dw[kk] + jnp.sum(gb * xp[base + 5 + kk:base + 5 + kk + CONV_ROWS, :], axis=0, keepdims=True)
            db = db + jnp.sum(gb, axis=0, keepdims=True)
            dx_ref[base:base + CONV_ROWS, :] = acc
        for kk in range(CONV_K):
            dw_ref[kk:kk + 1, :] = dw[kk]
        db_ref[...] = db

    return pl.pallas_call(
        body, name="conv_b",
        out_shape=[jax.ShapeDtypeStruct((t, c), F32), jax.ShapeDtypeStruct((CONV_K, c), F32),
                   jax.ShapeDtypeStruct((1, c), F32)],
        scratch_shapes=[pltpu.VMEM((t + 8, c), F32)] * 2, compiler_params=_cp())(x, w, g)


@jax.custom_vjp
def causal_conv(x, w, b):
    return _conv_fwd_call(x, w, b)


def _causal_conv_fwd(x, w, b):
    return _conv_fwd_call(x, w, b), (x, w)


def _causal_conv_bwd(r, g):
    return tuple(_conv_bwd_call(r[0], r[1], g))


causal_conv.defvjp(_causal_conv_fwd, _causal_conv_bwd)


ATT_TILE = 512
ATT_SCALE = HEAD_DIM ** -0.5


def _head_lane(val, lane, h):
    return jnp.sum(jnp.where(lane == h, val, 0.0), axis=1, keepdims=True)


def _attn_fwd_call(q, k, v, c128, cr):
    t, w = q.shape
    tq = ATT_TILE
    nq = t // tq
    k3, v3, cr4 = k.reshape(nq, tq, w), v.reshape(nq, tq, w), cr.reshape(N_HEADS, nq, 1, tq)

    def body(q_ref, k_ref, v_ref, c_ref, cr_ref, o_ref, lse_ref):
        i, h = pl.program_id(0), pl.program_id(1)
        hm = lax.broadcasted_iota(jnp.int32, (tq, w), 1) // HEAD_DIM == h
        lane = lax.broadcasted_iota(jnp.int32, (tq, 128), 1)
        qs = jnp.where(hm, q_ref[...] * ATT_SCALE, 0.0).astype(BF16)
        cq = _head_lane(c_ref[...], lane, h)
        row = i * tq + lax.broadcasted_iota(jnp.int32, (tq, tq), 0)
        col = lax.broadcasted_iota(jnp.int32, (tq, tq), 1)

        def step(j, carry):
            m, l, acc = carry
            s = _dot(qs, k_ref[j], 1, 1) + cq - cr_ref[0, j]
            s = jnp.where(row >= j * tq + col, s, NEG)
            m_new = jnp.maximum(m, jnp.max(s, axis=1, keepdims=True))
            p = jnp.exp(s - m_new)
            alpha = jnp.exp(m - m_new)
            return m_new, alpha * l + jnp.sum(p, axis=1, keepdims=True), alpha * acc + _dot(p, v_ref[j], 1, 0)

        init = (jnp.full((tq, 1), NEG, F32), jnp.zeros((tq, 1), F32), jnp.zeros((tq, w), F32))
        m, l, acc = lax.fori_loop(0, i + 1, step, init)
        out = jnp.where(hm, acc / l, 0.0)
        lse = jnp.where(lane == h, m + jnp.log(l), 0.0)

        @pl.when(h == 0)
        def _():
            o_ref[...] = out
            lse_ref[...] = lse

        @pl.when(h > 0)
        def _():
            o_ref[...] += out
            lse_ref[...] += lse

    tile = pl.BlockSpec((tq, w), lambda i, h: (i, 0))
    tile_c = pl.BlockSpec((tq, 128), lambda i, h: (i, 0))
    whole = pl.BlockSpec((nq, tq, w), lambda i, h: (0, 0, 0))
    rows = pl.BlockSpec((1, nq, 1, tq), lambda i, h: (h, 0, 0, 0))
    return pl.pallas_call(
        body, name="attn_f", grid=(nq, N_HEADS), in_specs=[tile, whole, whole, tile_c, rows], out_specs=[tile, tile_c],
        out_shape=[jax.ShapeDtypeStruct((t, w), F32), jax.ShapeDtypeStruct((t, 128), F32)],
        compiler_params=_cp(("arbitrary", "arbitrary")))(q, k3, v3, c128, cr4)


def _attn_bwd_call(q, k, v, c128, cr, o, lse, do):
    t, w = q.shape
    tq = ATT_TILE
    nq = t // tq
    r3 = lambda a: a.reshape(nq, tq, a.shape[-1])
    cr4 = cr.reshape(N_HEADS, nq, 1, tq)

    def body(q_ref, k_ref, v_ref, c_ref, cr_ref, o_ref, lse_ref, do_ref,
             dq_ref, dk_ref, dv_ref, dc_ref, dcr_ref):
        j, h = pl.program_id(0), pl.program_id(1)

        @pl.when((j == 0) & (h == 0))
        def _():
            dq_ref[...] = jnp.zeros_like(dq_ref)
            dc_ref[...] = jnp.zeros_like(dc_ref)

        hm = lax.broadcasted_iota(jnp.int32, (tq, w), 1) // HEAD_DIM == h
        lane = lax.broadcasted_iota(jnp.int32, (tq, 128), 1)
        kj = k_ref[...].astype(BF16)
        vj = v_ref[...].astype(BF16)
        ck = cr_ref[0, 0]
        col = j * tq + lax.broadcasted_iota(jnp.int32, (tq, tq), 1)
        row = lax.broadcasted_iota(jnp.int32, (tq, tq), 0)

        def step(i, carry):
            dk, dv, dck = carry
            qm = jnp.where(hm, q_ref[i], 0.0)
            dom = jnp.where(hm, do_ref[i], 0.0)
            s = _dot(qm * ATT_SCALE, kj, 1, 1) + _head_lane(c_ref[i], lane, h) - ck
            p = jnp.where(i * tq + row >= col, jnp.exp(s - _head_lane(lse_ref[i], lane, h)), 0.0)
            dv = dv + _dot(p, dom, 0, 0)
            dp = _dot(dom, vj, 1, 1)
            delta = jnp.sum(dom * o_ref[i], axis=1, keepdims=True)
            ds = p * (dp - delta)
            dq_ref[i] += jnp.where(hm, _dot(ds, kj, 1, 0), 0.0) * ATT_SCALE
            dk = dk + _dot(ds, qm, 0, 0) * ATT_SCALE
            dc_ref[i] += jnp.where(lane == h, jnp.sum(ds, axis=1, keepdims=True), 0.0)
            return dk, dv, dck - jnp.sum(ds, axis=0, keepdims=True)

        init = (jnp.zeros((tq, w), F32), jnp.zeros((tq, w), F32), jnp.zeros((1, tq), F32))
        dk, dv, dck = lax.fori_loop(j, nq, step, init)
        dcr_ref[0, 0] = dck

        @pl.when(h == 0)
        def _():
            dk_ref[...] = dk
            dv_ref[...] = dv

        @pl.when(h > 0)
        def _():
            dk_ref[...] += dk
            dv_ref[...] += dv

    whole = pl.BlockSpec((nq, tq, w), lambda j, h: (0, 0, 0))
    whole_c = pl.BlockSpec((nq, tq, 128), lambda j, h: (0, 0, 0))
    tile = pl.BlockSpec((None, tq, w), lambda j, h: (j, 0, 0))
    tile_r = pl.BlockSpec((1, 1, 1, tq), lambda j, h: (h, j, 0, 0))
    s3 = jax.ShapeDtypeStruct((nq, tq, w), F32)
    dq, dk, dv, dc, dcr = pl.pallas_call(
        body, name="attn_b", grid=(nq, N_HEADS),
        in_specs=[whole, tile, tile, whole_c, tile_r, whole, whole_c, whole],
        out_specs=[whole, tile, tile, whole_c, tile_r],
        out_shape=[s3, s3, s3, jax.ShapeDtypeStruct((nq, tq, 128), F32),
                   jax.ShapeDtypeStruct((N_HEADS, nq, 1, tq), F32)],
        compiler_params=_cp(("arbitrary", "arbitrary")))(r3(q), r3(k), r3(v), r3(c128), cr4, r3(o), r3(lse), r3(do))
    return dq.reshape(t, w), dk.reshape(t, w), dv.reshape(t, w), dc.reshape(t, 128), dcr.reshape(N_HEADS, 1, t)


@jax.custom_vjp
def fox_attention(q, k, v, c128, cr):
    return _attn_fwd_call(q, k, v, c128, cr)[0]


def _fox_attention_fwd(q, k, v, c128, cr):
    o, lse = _attn_fwd_call(q, k, v, c128, cr)
    return o, (q, k, v, c128, cr, o, lse)


def _fox_attention_bwd(r, g):
    return _attn_bwd_call(*r, g)


fox_attention.defvjp(_fox_attention_fwd, _fox_attention_bwd)


def _loss_call(x, g, target):
    t, d = x.shape
    tr = 512

    def body(x_ref, g_ref, t_ref, loss_ref, dx_ref, dg_ref):
        tgt = t_ref[...]

        def f(xv, gv):
            return 0.5 * jnp.sum(jnp.mean(jnp.square(_rms(xv, gv) - tgt), axis=-1))

        val, vjp = jax.vjp(f, x_ref[...], g_ref[...])
        dx, dg = vjp(jnp.ones((), F32))
        dx_ref[...] = dx

        @pl.when(pl.program_id(0) == 0)
        def _():
            loss_ref[...] = jnp.zeros_like(loss_ref)
            dg_ref[...] = jnp.zeros_like(dg_ref)

        loss_ref[...] += jnp.full(loss_ref.shape, val, F32)
        dg_ref[...] += dg

    row = _row_spec(tr, d)
    return pl.pallas_call(
        body, name="loss_head", grid=(t // tr,), in_specs=[row, _full_spec(g), row],
        out_specs=[pl.BlockSpec((1, 128), lambda i: (0, 0)), row, _full_spec(g)],
        out_shape=[jax.ShapeDtypeStruct((1, 128), F32), jax.ShapeDtypeStruct((t, d), F32),
                   jax.ShapeDtypeStruct(g.shape, F32)],
        compiler_params=_cp(("arbitrary",)))(x, g, target)


def _blockdiag(w):
    g, a, b = w.shape
    return jnp.einsum('gab,gk->gakb', w, jnp.eye(g, dtype=w.dtype)).reshape(g * a, g * b)


def _s5_params(p):
    rep = lambda a: jnp.repeat(a, S5_GROUP, axis=0)
    rows = S5_GROUPS * S5_GROUP
    bt = lambda b: b.transpose(0, 2, 1).reshape(rows, S5_STATE)
    abar_re, abar_im, bb_re, bb_im = s5_disc(
        rep(p["s5_lambda_re"]), rep(p["s5_lambda_im"]), rep(p["s5_log_dt"][:, None]), bt(p["s5_b_re"]), bt(p["s5_b_im"]))
    first = lambda a: a.reshape(S5_GROUPS, S5_GROUP, S5_STATE)[:, 0, :].reshape(1, S5_GROUPS * S5_STATE)
    g3 = lambda a: a.reshape(S5_GROUPS, S5_GROUP, S5_STATE)
    cblk = lambda c: _blockdiag(c.transpose(0, 2, 1))
    return (first(abar_re), first(abar_im), _blockdiag(g3(bb_re)), _blockdiag(g3(bb_im)),
            cblk(p["s5_c_re"]), cblk(p["s5_c_im"]))


MIXER_WEIGHTS = ('sgu_norm_g', 'sgu_w', 'sgu_b', 's5_lambda_re', 's5_lambda_im', 's5_log_dt', 's5_b_re', 's5_b_im',
                 's5_c_re', 's5_c_im', 's5_d', 's5_glu_w', 's5_glu_b', 'lru_conv_w', 'lru_conv_b', 'lru_wa', 'lru_ba',
                 'lru_wx', 'lru_bx', 'lru_lambda', 'fox_fgate_b')


def _mixers(z, p):
    w = MIXER_WIDTH
    row = lambda a: a[None, :]
    a_u, a_v, b_in, c_x, c_gate, d_q, d_k, d_v, d_f = jnp.split(z, [w * i for i in range(1, 9)], axis=1)
    sw = p["sgu_w"]
    (y_a,) = sgu_mix(a_u, a_v, row(p["sgu_norm_g"]), sw[0], sw[1], sw[2], sw[3],
                     jnp.repeat(p["sgu_b"].T, HEAD_DIM, axis=1))
    abar_re, abar_im, bblk_re, bblk_im, cblk_re, cblk_im = _s5_params(p)
    s_re, s_im = lti_scan(abar_re, abar_im, mm(b_in, bblk_re), mm(b_in, bblk_im))
    (y_b,) = s5_post(s_re, s_im, b_in, cblk_re, cblk_im, row(p["s5_d"]), p["s5_glu_w"], row(p["s5_glu_b"]))
    xc = causal_conv(c_x, p["lru_conv_w"], row(p["lru_conv_b"]))
    a, b = lru_pre(xc, _blockdiag(p["lru_wa"]), p["lru_ba"].reshape(1, w), _blockdiag(p["lru_wx"]),
                   p["lru_bx"].reshape(1, w), row(p["lru_lambda"]))
    (y_c,) = lru_post(tv_scan(a, b), c_gate)
    (log_f,) = log_sig(d_f, jnp.pad(p["fox_fgate_b"], (0, 128 - N_HEADS))[None, :])
    c128 = tv_scan(jnp.ones_like(log_f), log_f)
    y_d = fox_attention(d_q, d_k, d_v, c128, c128[:, :N_HEADS].T[:, None, :])
    return y_a, y_b, y_c, y_d


PACK_COLS = 1024
SHARD_SHAPE = {'w_mlp_in': (1024, 1024), 'w_mlp_out': (1024, 1024), 'w_out': (256, 1024), 'w_in': (1024, 513),
               's5_glu_w': (64, 256), 'lru_conv_w': (4, 64)}
SHARDED_AXIS = {'w_in': 1, 's5_glu_w': 0, 'lru_conv_w': 1, 'w_out': 0, 'w_mlp_in': 1, 'w_mlp_out': 0}
SHARD_ROWS = {n: -(-s[0] * s[1] // PACK_COLS) for n, s in SHARD_SHAPE.items()}
SHARD_OFF = {n: sum(list(SHARD_ROWS.values())[:i]) for i, n in enumerate(SHARD_SHAPE)}
LAYER_ROWS = 2880
SMALL_OFF = SHARD_OFF['w_in']
GLUE_ROWS = LAYER_ROWS - SMALL_OFF
assert SHARD_OFF['w_mlp_out'] == 1024 and SHARD_OFF['w_out'] == 2048 and SMALL_OFF % GLUE_ROWS == 0
assert SHARD_OFF['lru_conv_w'] + SHARD_ROWS['lru_conv_w'] <= LAYER_ROWS
PACK_ROWS = DEPTH * LAYER_ROWS
TOK = 1024
FF = 4 * D_MODEL


def _w4(i_of):
    return pl.BlockSpec((None, None, 1024, PACK_COLS), i_of)


def _tile2(rows, cols, i_of):
    return pl.BlockSpec((rows, cols), i_of)


def _layer_fwd(x, l, gathered, w_in, w_out, p, mix_p):
    t = x.shape[0]
    nt = t // TOK
    f32 = lambda r, c: jax.ShapeDtypeStruct((r, c), F32)
    b16 = lambda r, c: jax.ShapeDtypeStruct((r, c), BF16)
    g1, g2, gm = p["norm1_g"][None, :], p["norm2_g"][None, :], p["mix_norm_g"][None, :]
    (h1,) = _rowwise(_f_rms, [x], [g1], [D_MODEL], name="rms_f", tr=512, dtype=BF16)
    z = _matmul("mm_in", (nt, 1, 1), h1, _tile2(TOK, D_MODEL, lambda i, j, k: (i, 0)),
                w_in, _tile2(D_MODEL, D_IN_PAD, lambda i, j, k: (0, 0)), (1, 0),
                [(f32(t, D_IN_PAD), _tile2(TOK, D_IN_PAD, lambda i, j, k: (i, 0)))])
    ys, mix_vjp = jax.vjp(_mixers, z, mix_p)
    (yn,) = _rowwise(_f_gnorm, list(ys), [gm], [D_MODEL], name="gnorm_f", tr=512, dtype=BF16)
    x_tile = _tile2(TOK, D_MODEL, lambda i, j, k: (i, 0))
    x1 = _matmul("mm_out", (nt, 1, 1), yn, x_tile, w_out, _tile2(D_MODEL, D_MODEL, lambda i, j, k: (0, 0)), (1, 0),
                 [(f32(t, D_MODEL), x_tile)], extras=[(x, x_tile)], epilogue=lambda acc, r: (acc + r,))
    (h2,) = _rowwise(_f_rms, [x1], [g2], [D_MODEL], name="rms_f", tr=512, dtype=BF16)
    ff_tile = _tile2(TOK, 1024, lambda i, j, k: (i, j))

    def up_epilogue(acc):
        r = jnp.maximum(acc, 0.0)
        return r * r, r

    act, relu = _matmul("mm_up", (nt, FF // 1024, 1), h2, x_tile, gathered, _w4(lambda i, j, k: (j, l, 0, 0)), (1, 0),
                        [(b16(t, FF), ff_tile), (b16(t, FF), ff_tile)], epilogue=up_epilogue)
    x2 = _matmul("mm_down", (nt, 1, FF // 1024), act, _tile2(TOK, 1024, lambda i, j, k: (i, k)),
                 gathered, _w4(lambda i, j, k: (k, l, 1, 0)), (1, 0),
                 [(f32(t, D_MODEL), x_tile)], extras=[(x1, x_tile)], epilogue=lambda acc, r: (acc + r,))
    return x2, (x, h1, mix_vjp, ys, yn, x1, h2, act, relu)


def _layer_bwd(g, l, res, gathered, w_in, w_out, p, send):
    x, h1, mix_vjp, ys, yn, x1, h2, act, relu = res
    t = x.shape[0]
    nt = t // TOK
    f32 = lambda r, c: jax.ShapeDtypeStruct((r, c), F32)
    g1, g2, gm = p["norm1_g"][None, :], p["norm2_g"][None, :], p["mix_norm_g"][None, :]
    x_tile = _tile2(TOK, D_MODEL, lambda i, j, k: (i, 0))
    ff_tile = _tile2(TOK, 1024, lambda i, j, k: (i, j))
    tok_k = _tile2(TOK, D_MODEL, lambda i, j, k: (k, 0))
    send_s = jax.ShapeDtypeStruct(send.shape, send.dtype)
    du = _matmul("mm_down_dx", (nt, FF // 1024, 1), g, x_tile, gathered, _w4(lambda i, j, k: (j, l, 1, 0)), (1, 1),
                 [(jax.ShapeDtypeStruct((t, FF), BF16), ff_tile)], extras=[(relu, ff_tile)],
                 epilogue=lambda acc, r: (2.0 * r.astype(F32) * acc,))
    send = _matmul("mm_down_dw", (FF // 1024, 1, nt), act, _tile2(TOK, 1024, lambda i, j, k: (k, i)), g, tok_k, (0, 0),
                   [(send_s, _w4(lambda i, j, k: (i, l, 1, 0)))], into=send)
    send = _matmul("mm_up_dw", (1, FF // 1024, nt), h2, tok_k, du, _tile2(TOK, 1024, lambda i, j, k: (k, j)), (0, 0),
                   [(send_s, _w4(lambda i, j, k: (j, l, 0, 0)))], into=send)
    dh2 = _matmul("mm_up_dx", (nt, 1, FF // 1024), du, _tile2(TOK, 1024, lambda i, j, k: (i, k)),
                  gathered, _w4(lambda i, j, k: (k, l, 0, 0)), (1, 1), [(f32(t, D_MODEL), x_tile)])
    g_mid, dg2 = _rowwise_vjp(_f_rms, [x1], [g2], [dh2], name="rms_b", tr=512, add=g)
    w_full = _tile2(D_MODEL, D_MODEL, lambda i, j, k: (0, 0))
    dyn = _matmul("mm_out_dx", (nt, 1, 1), g_mid, x_tile, w_out, w_full, (1, 1), [(f32(t, D_MODEL), x_tile)])
    send = _matmul("mm_out_dw", (4, 1, nt), yn, _tile2(TOK, 256, lambda i, j, k: (k, i)), g_mid, tok_k, (0, 0),
                   [(send_s, pl.BlockSpec((None, None, 256, PACK_COLS),
                                          lambda i, j, k: (i, l, SHARD_OFF['w_out'] // 256, 0)))], into=send)
    *dys, dgm = _rowwise_vjp(_f_gnorm, list(ys), [gm], [dyn], name="gnorm_b", tr=512)
    dz, dmix = mix_vjp(tuple(dys))
    dz = dz.astype(BF16)
    z_tile = _tile2(TOK, D_IN_PAD, lambda i, j, k: (i, 0))
    d_w_in = _matmul("mm_in_dw", (1, 1, t // 512), h1, _tile2(512, D_MODEL, lambda i, j, k: (k, 0)),
                     dz, _tile2(512, D_IN_PAD, lambda i, j, k: (k, 0)), (0, 0),
                     [(f32(D_MODEL, D_IN_PAD), _tile2(D_MODEL, D_IN_PAD, lambda i, j, k: (0, 0)))])
    dh1 = _matmul("mm_in_dx", (nt, 1, 1), dz, z_tile, w_in, _tile2(D_MODEL, D_IN_PAD, lambda i, j, k: (0, 0)), (1, 1),
                  [(f32(t, D_MODEL), x_tile)])
    dx, dg1 = _rowwise_vjp(_f_rms, [x], [g1], [dh1], name="rms_b", tr=512, add=g_mid)
    small = dict(dmix, norm1_g=dg1[0], norm2_g=dg2[0], mix_norm_g=dgm[0])
    return dx, small, d_w_in, send


HBM = pl.BlockSpec(memory_space=pltpu.HBM)
D2D_CHUNKS = 16
ICI_CHUNKS = 4


def _coords():
    return lax.axis_index("x"), lax.axis_index("y"), lax.axis_index("c")


def _other_chips(x, y):
    return [(1 - x, y), (x, 1 - y), (1 - x, 1 - y)]


def _start_chunks(make, rows, n):
    size = rows // n
    assert size * n == rows
    for k in range(n):
        make(pl.ds(k * size, size)).start()


def _allgather_shards(shard):
    r, cols = shard.shape
    rh = r // 2

    def body(in_ref, out_ref, send_sems, recv_sems):
        x, y, c = _coords()
        me, sibling = (x, y, c), (x, y, 1 - c)
        chips = _other_chips(x, y)

        def half(px, py, pc, rows=pl.ds(0, rh)):
            return out_ref.at[2 * px + py, pl.ds(pc * rh + rows.start, rows.size), :]

        def copy(k, block, to, rows=pl.ds(0, rh), from_input=False):
            src = in_ref.at[pl.ds(block[2] * rh + rows.start, rows.size), :] if from_input else half(*block, rows)
            return pltpu.make_async_remote_copy(
                src_ref=src, dst_ref=half(*block, rows), send_sem=send_sems.at[k], recv_sem=recv_sems.at[k],
                device_id=to, device_id_type=MESH)

        def own(rows=pl.ds(0, r)):
            return pltpu.make_async_remote_copy(
                src_ref=in_ref.at[rows, :], dst_ref=out_ref.at[2 * x + y, rows, :], send_sem=send_sems.at[6],
                recv_sem=recv_sems.at[6], device_id=sibling, device_id_type=MESH)

        for j, chip in enumerate(chips):
            _start_chunks(lambda rows: copy(j, me, (*chip, c), rows, from_input=True), rh, ICI_CHUNKS)
        _start_chunks(own, r, D2D_CHUNKS)
        for j, chip in enumerate(chips):
            copy(j, (*chip, c), me).wait_recv()
            _start_chunks(lambda rows: copy(3 + j, (*chip, c), sibling, rows), rh, D2D_CHUNKS)
        for j, chip in enumerate(chips):
            copy(3 + j, (*chip, 1 - c), me).wait_recv()
        for j, chip in enumerate(chips):
            copy(j, me, (*chip, c), from_input=True).wait_send()
            copy(3 + j, (*chip, c), sibling).wait_send()
        own().wait()

    return pl.pallas_call(
        body, name="allgather_shards", out_shape=jax.ShapeDtypeStruct((4, r, cols), shard.dtype),
        in_specs=[HBM], out_specs=HBM,
        scratch_shapes=[pltpu.SemaphoreType.DMA((7,)), pltpu.SemaphoreType.DMA((7,))],
        compiler_params=pltpu.CompilerParams())(shard)


def _pair_exchange(g):
    s, _, rh, cols = g.shape

    def body(g_ref, recv_ref, send_sem, recv_sem):
        x, y, c = _coords()

        def copy(slot, rows):
            return pltpu.make_async_remote_copy(
                src_ref=g_ref.at[slot, 1 - c, rows, :], dst_ref=recv_ref.at[slot, rows, :], send_sem=send_sem,
                recv_sem=recv_sem, device_id=(x, y, 1 - c), device_id_type=MESH)

        for slot in range(s):
            _start_chunks(lambda rows: copy(slot, rows), rh, D2D_CHUNKS // s)
        pltpu.make_async_remote_copy(
            src_ref=g_ref.at[:, 1 - c], dst_ref=recv_ref, send_sem=send_sem, recv_sem=recv_sem,
            device_id=(x, y, 1 - c), device_id_type=MESH).wait()

    return pl.pallas_call(
        body, name="pair_exchange", out_shape=jax.ShapeDtypeStruct((s, rh, cols), g.dtype), in_specs=[HBM],
        out_specs=HBM, scratch_shapes=[pltpu.SemaphoreType.DMA] * 2, compiler_params=pltpu.CompilerParams())(g)


def _chip_exchange(p):
    rh = p.shape[1]

    def body(p_ref, recv_ref, send_sems, recv_sems):
        x, y, c = _coords()
        chips = _other_chips(x, y)

        def copy(j, chip, rows=pl.ds(0, rh)):
            return pltpu.make_async_remote_copy(
                src_ref=p_ref.at[2 * chip[0] + chip[1], rows, :], dst_ref=recv_ref.at[j, rows, :],
                send_sem=send_sems.at[j], recv_sem=recv_sems.at[j], device_id=(*chip, c), device_id_type=MESH)

        for j, chip in enumerate(chips):
            _start_chunks(lambda rows: copy(j, chip, rows), rh, ICI_CHUNKS)
        for j, chip in enumerate(chips):
            copy(j, chip).wait_recv()
        for j, chip in enumerate(chips):
            copy(j, chip).wait_send()

    return pl.pallas_call(
        body, name="chip_exchange", out_shape=jax.ShapeDtypeStruct((3,) + p.shape[1:], p.dtype), in_specs=[HBM],
        out_specs=HBM, scratch_shapes=[pltpu.SemaphoreType.DMA((3,)), pltpu.SemaphoreType.DMA((3,))],
        compiler_params=pltpu.CompilerParams())(p)


def _sum_chips(p, recv, tr):
    _, rh, cols = p.shape
    x, y, c = _coords()
    where = jnp.stack([2 * x + y, c]).astype(jnp.int32)

    def body(w_ref, own_ref, r_ref, o_ref):
        acc = own_ref[...].astype(F32)
        for k in range(3):
            acc = acc + r_ref[k].astype(F32)
        o_ref[...] = acc

    return pl.pallas_call(
        body, name="sum_chips", out_shape=jax.ShapeDtypeStruct((2, rh, cols), F32),
        grid_spec=pltpu.PrefetchScalarGridSpec(
            num_scalar_prefetch=1, grid=(rh // tr,),
            in_specs=[pl.BlockSpec((None, tr, cols), lambda i, w_ref: (w_ref[0], i, 0)),
                      pl.BlockSpec((3, tr, cols), lambda i, w_ref: (0, i, 0))],
            out_specs=pl.BlockSpec((None, tr, cols), lambda i, w_ref: (w_ref[1], i, 0))),
        compiler_params=_cp(("arbitrary",)))(where, p, recv)


def _pair_share(buf):
    _, rh, cols = buf.shape

    def body(in_ref, out_ref, send_sem, recv_sem):
        x, y, c = _coords()

        def copy(slot, rows=pl.ds(0, rh)):
            return pltpu.make_async_remote_copy(
                src_ref=in_ref.at[slot, rows, :], dst_ref=out_ref.at[slot, rows, :], send_sem=send_sem,
                recv_sem=recv_sem, device_id=(x, y, 1 - c), device_id_type=MESH)

        _start_chunks(lambda rows: copy(c, rows), rh, D2D_CHUNKS)
        copy(c).wait_send()
        copy(1 - c).wait_recv()

    return pl.pallas_call(
        body, name="pair_share", out_shape=jax.ShapeDtypeStruct(buf.shape, buf.dtype), in_specs=[HBM], out_specs=HBM,
        scratch_shapes=[pltpu.SemaphoreType.DMA] * 2, input_output_aliases={0: 0},
        compiler_params=pltpu.CompilerParams())(buf)


def _allgather_all(blk):
    m_per, cols = blk.shape
    whole = pl.ds(0, m_per)

    def body(x_ref, out_ref, send_sems, recv_sems, local_sem):
        x, y, c = _coords()
        me, sibling = (x, y, c), (x, y, 1 - c)
        chips = _other_chips(x, y)

        def rows_of(px, py, pc, rows):
            return out_ref.at[4 * px + 2 * py + pc, rows, :]

        def copy(k, block, to, rows=whole, from_input=False):
            return pltpu.make_async_remote_copy(
                src_ref=x_ref.at[rows, :] if from_input else rows_of(*block, rows), dst_ref=rows_of(*block, rows),
                send_sem=send_sems.at[k], recv_sem=recv_sems.at[k], device_id=to, device_id_type=MESH)

        mine = pltpu.make_async_copy(x_ref, rows_of(*me, whole), local_sem)
        mine.start()
        _start_chunks(lambda rows: copy(0, me, sibling, rows, from_input=True), m_per, ICI_CHUNKS)
        for j, chip in enumerate(chips):
            _start_chunks(lambda rows: copy(1 + j, me, (*chip, c), rows, from_input=True), m_per, ICI_CHUNKS)
        for j, chip in enumerate(chips):
            copy(1 + j, (*chip, c), me).wait_recv()
            _start_chunks(lambda rows: copy(4 + j, (*chip, c), sibling, rows), m_per, ICI_CHUNKS)
        copy(0, sibling, me).wait_recv()
        for j, chip in enumerate(chips):
            copy(4 + j, (*chip, 1 - c), me).wait_recv()
        copy(0, me, sibling, from_input=True).wait_send()
        for j, chip in enumerate(chips):
            copy(1 + j, me, (*chip, c), from_input=True).wait_send()
            copy(4 + j, (*chip, c), sibling).wait_send()
        mine.wait()

    return pl.pallas_call(
        body, name="allgather_all", out_shape=jax.ShapeDtypeStruct((8, m_per, cols), blk.dtype),
        in_specs=[pl.BlockSpec(memory_space=pltpu.VMEM)], out_specs=pl.BlockSpec(memory_space=pltpu.VMEM),
        scratch_shapes=[pltpu.SemaphoreType.DMA((7,)), pltpu.SemaphoreType.DMA((7,)), pltpu.SemaphoreType.DMA],
        compiler_params=pltpu.CompilerParams(vmem_limit_bytes=VMEM_LIMIT))(blk)


def _add_kept(g, recv, tr):
    s, _, rh, cols = g.shape

    def body(c_ref, a_ref, b_ref, o_ref):
        o_ref[...] = (a_ref[...].astype(F32) + b_ref[...].astype(F32)).astype(o_ref.dtype)

    spec = pl.BlockSpec((None, tr, cols), lambda si, i, c_ref: (si, i, 0))
    return pl.pallas_call(
        body, name="add_kept", out_shape=jax.ShapeDtypeStruct((s, rh, cols), BF16),
        grid_spec=pltpu.PrefetchScalarGridSpec(
            num_scalar_prefetch=1, grid=(s, rh // tr),
            in_specs=[pl.BlockSpec((None, None, tr, cols), lambda si, i, c_ref: (si, c_ref[0], i, 0)), spec],
            out_specs=spec),
        compiler_params=_cp(("arbitrary", "arbitrary")))(lax.axis_index("c").astype(jnp.int32).reshape(1), g, recv)


def _sum_slots(p, tr, name):
    s, rows, cols = p.shape

    def body(p_ref, o_ref):
        acc = p_ref[0].astype(F32)
        for k in range(1, s):
            acc = acc + p_ref[k].astype(F32)
        o_ref[...] = acc

    return pl.pallas_call(
        body, name=name, grid=(rows // tr,), in_specs=[pl.BlockSpec((s, tr, cols), lambda i: (0, i, 0))],
        out_specs=_row_spec(tr, cols), out_shape=jax.ShapeDtypeStruct((rows, cols), F32),
        compiler_params=_cp(("arbitrary",)))(p)


def _adamw_call(w, g, m, v, name):
    rows, cols = w.shape
    tr = _tile(rows, 512) if rows % 512 == 0 else _tile(rows, 128)
    c1 = 1.0 - ADAM_B1 ** ADAM_STEP
    c2 = 1.0 - ADAM_B2 ** ADAM_STEP

    def body(w_ref, g_ref, m_ref, v_ref, d_ref, nm_ref, nv_ref):
        gv = g_ref[...]
        nm = ADAM_B1 * m_ref[...] + (1.0 - ADAM_B1) * gv
        nv = ADAM_B2 * v_ref[...] + (1.0 - ADAM_B2) * jnp.square(gv)
        d_ref[...] = -ADAM_LR * ((nm / c1) / (jnp.sqrt(nv / c2) + ADAM_EPS) + ADAM_WD * w_ref[...])
        nm_ref[...] = nm
        nv_ref[...] = nv

    spec = _row_spec(tr, cols)
    o = jax.ShapeDtypeStruct((rows, cols), F32)
    return pl.pallas_call(body, name=name, grid=(rows // tr,), in_specs=[spec] * 4, out_specs=[spec] * 3,
                          out_shape=[o, o, o], compiler_params=_cp(("arbitrary",)))(w, g, m, v)


WEIGHTS = ('norm1_g', 'w_in', 'sgu_norm_g', 'sgu_w', 'sgu_b', 's5_lambda_re', 's5_lambda_im', 's5_log_dt',
           's5_b_re', 's5_b_im', 's5_c_re', 's5_c_im', 's5_d', 's5_glu_w', 's5_glu_b', 'lru_conv_w',
           'lru_conv_b', 'lru_wa', 'lru_ba', 'lru_wx', 'lru_bx', 'lru_lambda', 'fox_fgate_b', 'mix_norm_g',
           'w_out', 'norm2_g', 'w_mlp_in', 'w_mlp_out', 'final_g')
N_W = len(WEIGHTS)


def _pack_shards(shards, dtype, names=tuple(SHARD_SHAPE), rows=LAYER_ROWS):
    parts = []
    for n in names:
        lead = shards[n].shape[:-2]
        flat = shards[n].reshape(*lead, -1).astype(dtype)
        flat = jnp.pad(flat, [(0, 0)] * len(lead) + [(0, SHARD_ROWS[n] * PACK_COLS - flat.shape[-1])])
        parts.append(flat.reshape(*lead, SHARD_ROWS[n], PACK_COLS))
    lead = parts[0].shape[:-2]
    used = sum(SHARD_ROWS[n] for n in names)
    if rows > used:
        parts.append(jnp.zeros((*lead, rows - used, PACK_COLS), dtype))
    return jnp.concatenate(parts, axis=-2)


def _unpack_shards(buf, names=tuple(SHARD_SHAPE)):
    lead = buf.shape[:-2]
    out = {}
    for n in names:
        s0, s1 = SHARD_SHAPE[n]
        rows, off = SHARD_ROWS[n], SHARD_OFF[n]
        flat = buf[..., off:off + rows, :].reshape(*lead, rows * PACK_COLS)
        out[n] = flat[..., :s0 * s1].reshape(*lead, s0, s1)
    return out


def _join_chips(g, axis):
    _, d, s0, s1 = g.shape
    if axis == 0:
        return g.transpose(1, 0, 2, 3).reshape(d, 4 * s0, s1)
    return g.transpose(1, 2, 0, 3).reshape(d, s0, 4 * s1)


def _split_chips(w, axis):
    d = w.shape[0]
    if axis == 0:
        return w.reshape(d, 4, w.shape[1] // 4, w.shape[2]).transpose(1, 0, 2, 3)
    return w.reshape(d, w.shape[1], 4, w.shape[2] // 4).transpose(2, 0, 1, 3)


def _flat_rows(shape):
    return -(-math.prod(shape) // PACK_COLS)


def _pack_flat(arrs, rows):
    parts = []
    for a in arrs:
        flat = a.reshape(-1)
        r = _flat_rows(a.shape)
        parts.append(jnp.pad(flat, (0, r * PACK_COLS - flat.shape[0])).reshape(r, PACK_COLS))
    used = sum(p.shape[0] for p in parts)
    parts.append(jnp.zeros((rows - used, PACK_COLS), F32))
    return jnp.concatenate(parts, axis=0)


def _unpack_flat(buf, shapes):
    out, off = [], 0
    for s in shapes:
        r = _flat_rows(s)
        out.append(buf[off:off + r].reshape(-1)[:math.prod(s)].reshape(s))
        off += r
    return out


def _write_glue(send, glue):
    def body(g_ref, s_ref, o_ref):
        o_ref[...] = g_ref[...]

    blk = (None, None, GLUE_ROWS, PACK_COLS)
    return pl.pallas_call(
        body, name="write_glue", grid=(4, DEPTH),
        in_specs=[pl.BlockSpec(blk, lambda s, l: (s, l, 0, 0)), pl.BlockSpec(memory_space=pl.ANY)],
        out_specs=pl.BlockSpec(blk, lambda s, l: (s, l, SMALL_OFF // GLUE_ROWS, 0)),
        out_shape=jax.ShapeDtypeStruct(send.shape, send.dtype), input_output_aliases={1: 0},
        compiler_params=_cp(("arbitrary", "arbitrary")))(glue, send)


GLUE_PACKED = ('w_in', 's5_glu_w', 'lru_conv_w')


def _forward_backward(x, target, final_g, gathered, rep):
    joined = {n: _join_chips(g, SHARDED_AXIS[n])
              for n, g in _unpack_shards(gathered, GLUE_PACKED + ('w_out',)).items()}
    w_in = jnp.pad(joined['w_in'], ((0, 0), (0, 0), (0, D_IN_PAD - D_IN_PROJ)))
    w_out = joined['w_out']
    norm_p = [{n: rep[n][l] for n in ('norm1_g', 'norm2_g', 'mix_norm_g')} for l in range(DEPTH)]
    mix_p = [{n: joined[n][l].astype(F32) if n in joined else rep[n][l] for n in MIXER_WEIGHTS} for l in range(DEPTH)]

    residuals = []
    for l in range(DEPTH):
        x, res = _layer_fwd(x, l, gathered, w_in[l], w_out[l], norm_p[l], mix_p[l])
        residuals.append(res)
    loss_part, g, d_final = _loss_call(x, final_g[None, :], target)

    send = lax.empty((4, DEPTH, LAYER_ROWS, PACK_COLS), BF16)
    small, d_w_in = [None] * DEPTH, [None] * DEPTH
    for l in reversed(range(DEPTH)):
        g, small[l], d_w_in[l], send = _layer_bwd(g, l, residuals[l], gathered, w_in[l], w_out[l], norm_p[l], send)
    stacked = {n: jnp.stack([small[l][n] for l in range(DEPTH)]) for n in small[0]}
    stacked['w_in'] = jnp.stack(d_w_in)[:, :, :D_IN_PROJ]
    glue = _pack_shards({n: _split_chips(stacked.pop(n), SHARDED_AXIS[n]) for n in GLUE_PACKED}, BF16,
                        names=GLUE_PACKED, rows=GLUE_ROWS)
    return loss_part, g, d_final, stacked, _write_glue(send, glue)


def _step(*args):
    x, target = args[0], args[1 + N_W]
    w = dict(zip(WEIGHTS, args[1:1 + N_W]))
    m = dict(zip(WEIGHTS, args[2 + N_W:2 + 2 * N_W]))
    v = dict(zip(WEIGHTS, args[2 + 2 * N_W:2 + 3 * N_W]))
    small = [n for n in WEIGHTS if n not in SHARD_SHAPE]

    packed = _pack_shards({n: w[n] for n in SHARD_SHAPE}, BF16).reshape(PACK_ROWS, PACK_COLS)
    gathered = _allgather_shards(packed).reshape(4, DEPTH, LAYER_ROWS, PACK_COLS)

    loss_part, dx, d_final, dw, send = _forward_backward(
        x[0], target[0], w['final_g'], gathered, {n: w[n] for n in small})

    rh = PACK_ROWS // 2
    halves = send.reshape(4, 2, rh, PACK_COLS)
    chip_sum = _add_kept(halves, _pair_exchange(halves), 1440)
    pair = _pair_share(_sum_chips(chip_sum, _chip_exchange(chip_sum), 720))
    g_shard = _unpack_shards(pair.reshape(DEPTH, LAYER_ROWS, PACK_COLS))

    small_g = [d_final.reshape(-1) if n == 'final_g' else dw[n] for n in small]
    small_rows = -(-(sum(_flat_rows(w[n].shape) for n in small) + 1) // 128) * 128
    small_sum = _sum_slots(_allgather_all(_pack_flat(small_g + [loss_part[0, :1]], small_rows)), 128, "sum_devices")
    *g_small, loss = _unpack_flat(small_sum, [w[n].shape for n in small] + [()])

    grads, delta, new_m, new_v = {}, {}, {}, {}
    for n in SHARD_SHAPE:
        shp = w[n].shape
        v2 = lambda a: a.reshape(-1, shp[-1])
        res = _adamw_call(v2(w[n]), v2(g_shard[n]), v2(m[n]), v2(v[n]), "adamw_" + n)
        grads[n] = g_shard[n]
        delta[n], new_m[n], new_v[n] = (r.reshape(shp) for r in res)
    pk = lambda d: _pack_flat([d[n] for n in small], small_rows)
    res = _adamw_call(pk(w), _pack_flat(g_small, small_rows), pk(m), pk(v), "adamw_small")
    shapes = [w[n].shape for n in small]
    for n, g, d_, m_, v_ in zip(small, g_small, *(_unpack_flat(r, shapes) for r in res)):
        grads[n], delta[n], new_m[n], new_v[n] = g, d_, m_, v_

    return (loss, dx[None], *[grads[n] for n in WEIGHTS], *[delta[n] for n in WEIGHTS],
            *[new_m[n] for n in WEIGHTS], *[new_v[n] for n in WEIGHTS])


def kernel(x, norm1_g, w_in, sgu_norm_g, sgu_w, sgu_b, s5_lambda_re, s5_lambda_im, s5_log_dt, s5_b_re, s5_b_im, s5_c_re, s5_c_im, s5_d, s5_glu_w, s5_glu_b, lru_conv_w, lru_conv_b, lru_wa, lru_ba, lru_wx, lru_bx, lru_lambda, fox_fgate_b, mix_norm_g, w_out, norm2_g, w_mlp_in, w_mlp_out, final_g, loss_target, m_norm1_g, m_w_in, m_sgu_norm_g, m_sgu_w, m_sgu_b, m_s5_lambda_re, m_s5_lambda_im, m_s5_log_dt, m_s5_b_re, m_s5_b_im, m_s5_c_re, m_s5_c_im, m_s5_d, m_s5_glu_w, m_s5_glu_b, m_lru_conv_w, m_lru_conv_b, m_lru_wa, m_lru_ba, m_lru_wx, m_lru_bx, m_lru_lambda, m_fox_fgate_b, m_mix_norm_g, m_w_out, m_norm2_g, m_w_mlp_in, m_w_mlp_out, m_final_g, v_norm1_g, v_w_in, v_sgu_norm_g, v_sgu_w, v_sgu_b, v_s5_lambda_re, v_s5_lambda_im, v_s5_log_dt, v_s5_b_re, v_s5_b_im, v_s5_c_re, v_s5_c_im, v_s5_d, v_s5_glu_w, v_s5_glu_b, v_lru_conv_w, v_lru_conv_b, v_lru_wa, v_lru_ba, v_lru_wx, v_lru_bx, v_lru_lambda, v_fox_fgate_b, v_mix_norm_g, v_w_out, v_norm2_g, v_w_mlp_in, v_w_mlp_out, v_final_g):
    return _step(x, norm1_g, w_in, sgu_norm_g, sgu_w, sgu_b, s5_lambda_re, s5_lambda_im, s5_log_dt, s5_b_re, s5_b_im, s5_c_re, s5_c_im, s5_d, s5_glu_w, s5_glu_b, lru_conv_w, lru_conv_b, lru_wa, lru_ba, lru_wx, lru_bx, lru_lambda, fox_fgate_b, mix_norm_g, w_out, norm2_g, w_mlp_in, w_mlp_out, final_g, loss_target, m_norm1_g, m_w_in, m_sgu_norm_g, m_sgu_w, m_sgu_b, m_s5_lambda_re, m_s5_lambda_im, m_s5_log_dt, m_s5_b_re, m_s5_b_im, m_s5_c_re, m_s5_c_im, m_s5_d, m_s5_glu_w, m_s5_glu_b, m_lru_conv_w, m_lru_conv_b, m_lru_wa, m_lru_ba, m_lru_wx, m_lru_bx, m_lru_lambda, m_fox_fgate_b, m_mix_norm_g, m_w_out, m_norm2_g, m_w_mlp_in, m_w_mlp_out, m_final_g, v_norm1_g, v_w_in, v_sgu_norm_g, v_sgu_w, v_sgu_b, v_s5_lambda_re, v_s5_lambda_im, v_s5_log_dt, v_s5_b_re, v_s5_b_im, v_s5_c_re, v_s5_c_im, v_s5_d, v_s5_glu_w, v_s5_glu_b, v_lru_conv_w, v_lru_conv_b, v_lru_wa, v_lru_ba, v_lru_wx, v_lru_bx, v_lru_lambda, v_fox_fgate_b, v_mix_norm_g, v_w_out, v_norm2_g, v_w_mlp_in, v_w_mlp_out, v_final_g)
```

```python
import functools
import math

import jax
import jax.numpy as jnp
from jax import lax
from jax.experimental import pallas as pl
from jax.experimental.pallas import tpu as pltpu

F32 = jnp.float32
BF16 = jnp.bfloat16

DEPTH = 4
D_MODEL = 1024
MIXER_WIDTH = 256
SGU_CHUNK = 128
N_HEADS = 4
HEAD_DIM = 64
S5_GROUPS = 16
S5_GROUP = 16
S5_STATE = 64
LRU_C = 8.0
RMS_EPS = 1e-6
D_IN_PROJ = 8 * MIXER_WIDTH + N_HEADS
D_IN_PAD = 8 * MIXER_WIDTH + 128
ADAM_LR, ADAM_B1, ADAM_B2, ADAM_EPS, ADAM_WD, ADAM_STEP = 0.001, 0.9, 0.999, 1e-08, 0.01, 10

V7X_VMEM_BYTES = 64 * 1024 * 1024
VMEM_LIMIT = V7X_VMEM_BYTES - 8 * 1024 * 1024
NEG = -1e30
MESH = pl.DeviceIdType.MESH


def _cp(sem=None, **kw):
    return pltpu.CompilerParams(dimension_semantics=sem, vmem_limit_bytes=VMEM_LIMIT, **kw)


def _full_spec(a):
    nd = a.ndim
    return pl.BlockSpec(a.shape, lambda *_: (0,) * nd)


def _tile(n, pref=512):
    return pref if n % pref == 0 else n


def _dot(a, b, ca, cb):
    return lax.dot_general(a.astype(BF16), b.astype(BF16), (((ca,), (cb,)), ((), ())),
                           preferred_element_type=F32)


def _mm_call(a, b, *, ta=False, tb=False, res=None, name):
    m, k = (a.shape[1], a.shape[0]) if ta else a.shape
    n = b.shape[0] if tb else b.shape[1]
    tm, tn, tk = _tile(m), _tile(n), _tile(k)
    if tn > 1024 or tk > 1024:
        tm = _tile(m, 256)
    nk = k // tk
    ca, cb = (0 if ta else 1), (1 if tb else 0)

    def body(*refs):
        a_ref, b_ref = refs[0], refs[1]
        o_ref, acc = refs[-2], refs[-1]
        kk = pl.program_id(2)

        @pl.when(kk == 0)
        def _():
            acc[...] = jnp.zeros_like(acc)

        acc[...] += _dot(a_ref[...], b_ref[...], ca, cb)

        @pl.when(kk == nk - 1)
        def _():
            out = acc[...]
            if res is not None:
                out = out + refs[2][...]
            o_ref[...] = out

    a_spec = pl.BlockSpec((tk, tm), lambda i, j, kk: (kk, i)) if ta else pl.BlockSpec((tm, tk), lambda i, j, kk: (i, kk))
    b_spec = pl.BlockSpec((tn, tk), lambda i, j, kk: (j, kk)) if tb else pl.BlockSpec((tk, tn), lambda i, j, kk: (kk, j))
    o_spec = pl.BlockSpec((tm, tn), lambda i, j, kk: (i, j))
    ins, specs = [a, b], [a_spec, b_spec]
    if res is not None:
        ins.append(res)
        specs.append(o_spec)
    return pl.pallas_call(
        body, name=name, grid=(m // tm, n // tn, nk), in_specs=specs, out_specs=o_spec,
        out_shape=jax.ShapeDtypeStruct((m, n), F32), scratch_shapes=[pltpu.VMEM((tm, tn), F32)],
        compiler_params=_cp(("parallel", "parallel", "arbitrary")))(*ins)


@jax.custom_vjp
def mm(a, b):
    return _mm_call(a, b, name="mm")


def _mm_fwd(a, b):
    return _mm_call(a, b, name="mm"), (a, b)


def _mm_bwd(r, g):
    a, b = r
    return _mm_call(g, b, tb=True, name="mm_da"), _mm_call(a, g, ta=True, name="mm_db")


mm.defvjp(_mm_fwd, _mm_bwd)


def _matmul(name, grid, a, a_spec, b, b_spec, dims, outs, *, extras=(), epilogue=None, into=None):
    nk = grid[2]
    n_ex, n_out = len(extras), len(outs)
    tm_tn = tuple(d for d in outs[0][1].block_shape if d is not None)[-2:]

    def body(*refs):
        a_ref, b_ref = refs[0], refs[1]
        ex_refs = refs[2:2 + n_ex]
        o_refs = refs[len(refs) - n_out - (nk > 1):len(refs) - (nk > 1)]

        def finish(val):
            res = epilogue(val, *[e[...] for e in ex_refs]) if epilogue else (val,)
            for o_ref, r in zip(o_refs, res):
                o_ref[...] = r.astype(o_ref.dtype)

        if nk == 1:
            finish(_dot(a_ref[...], b_ref[...], *dims))
        else:
            acc = refs[-1]
            kk = pl.program_id(2)

            @pl.when(kk == 0)
            def _():
                acc[...] = jnp.zeros_like(acc)

            acc[...] += _dot(a_ref[...], b_ref[...], *dims)

            @pl.when(kk == nk - 1)
            def _():
                finish(acc[...])

    ins = [a, b] + [e[0] for e in extras]
    specs = [a_spec, b_spec] + [e[1] for e in extras]
    aliases = {}
    if into is not None:
        aliases = {len(ins): 0}
        ins.append(into)
        specs.append(pl.BlockSpec(memory_space=pl.ANY))
    res = pl.pallas_call(
        body, name=name, grid=grid, in_specs=specs, out_specs=[o[1] for o in outs], out_shape=[o[0] for o in outs],
        scratch_shapes=[pltpu.VMEM(tm_tn, F32)] if nk > 1 else [], input_output_aliases=aliases,
        compiler_params=_cp(("arbitrary", "arbitrary", "arbitrary")))(*ins)
    return res[0] if n_out == 1 else res


@jax.custom_vjp
def _bdot(a, b):
    return _dot(a, b, 1, 0)


def _bdot_fwd(a, b):
    return _dot(a, b, 1, 0), (a, b)


def _bdot_bwd(r, g):
    a, b = r
    return _dot(g, b, 1, 1), _dot(a, g, 0, 0)


_bdot.defvjp(_bdot_fwd, _bdot_bwd)


def _row_spec(tr, w):
    return pl.BlockSpec((tr, w), lambda i: (i, 0))


def _rowwise(fn, rows, pars, outs, *, name, tr, dtype=F32):
    t = rows[0].shape[0]
    n_in = len(rows) + len(pars)

    def body(*refs):
        res = fn(*[r[...] for r in refs[:n_in]])
        for o_ref, v in zip(refs[n_in:], res):
            o_ref[...] = v.astype(o_ref.dtype)

    return pl.pallas_call(
        body, name=name, grid=(t // tr,),
        in_specs=[_row_spec(tr, r.shape[1]) for r in rows] + [_full_spec(p) for p in pars],
        out_specs=[_row_spec(tr, w) for w in outs],
        out_shape=[jax.ShapeDtypeStruct((t, w), dtype) for w in outs],
        compiler_params=_cp(("arbitrary",)))(*rows, *pars)


def _rowwise_vjp(fn, rows, pars, cots, *, name, tr, add=None):
    t = rows[0].shape[0]
    nr, npar = len(rows), len(pars)
    cots = list(cots) + ([add] if add is not None else [])
    nc = len(cots)

    def body(*refs):
        vals = [r[...] for r in refs[:nr + npar]]
        cts = [c[...] for c in refs[nr + npar:nr + npar + nc]]
        douts = refs[nr + npar + nc:]
        extra = cts.pop() if add is not None else None
        _, vjp = jax.vjp(fn, *vals)
        grads = list(vjp(tuple(cts)))
        if extra is not None:
            grads[0] = grads[0] + extra
        for kk in range(nr):
            douts[kk][...] = grads[kk]

        @pl.when(pl.program_id(0) == 0)
        def _():
            for kk in range(npar):
                douts[nr + kk][...] = jnp.zeros_like(douts[nr + kk])

        for kk in range(npar):
            douts[nr + kk][...] += grads[nr + kk]

    return pl.pallas_call(
        body, name=name, grid=(t // tr,),
        in_specs=[_row_spec(tr, r.shape[1]) for r in rows] + [_full_spec(p) for p in pars]
        + [_row_spec(tr, c.shape[1]) for c in cots],
        out_specs=[_row_spec(tr, r.shape[1]) for r in rows] + [_full_spec(p) for p in pars],
        out_shape=[jax.ShapeDtypeStruct(r.shape, F32) for r in rows]
        + [jax.ShapeDtypeStruct(p.shape, F32) for p in pars],
        compiler_params=_cp(("arbitrary",)))(*rows, *pars, *cots)


def _make_rw(fn, name, tr, nr, outs):
    @jax.custom_vjp
    def f(*args):
        return tuple(_rowwise(fn, args[:nr], args[nr:], outs, name=name + "_f", tr=tr))

    def fwd(*args):
        return f(*args), args

    def bwd(args, cts):
        return tuple(_rowwise_vjp(fn, args[:nr], args[nr:], list(cts), name=name + "_b", tr=tr))

    f.defvjp(fwd, bwd)
    return f


def _rms(x, g):
    return x * lax.rsqrt(jnp.mean(jnp.square(x), axis=-1, keepdims=True) + RMS_EPS) * g


def _f_rms(x, g):
    return (_rms(x, g),)


def _f_sgu(au, av, ng, w0, w1, w2, w3, bfull):
    u = jax.nn.gelu(au)
    v = _rms(jax.nn.gelu(av), ng)
    tri = lax.broadcasted_iota(jnp.int32, (SGU_CHUNK, SGU_CHUNK), 0) >= lax.broadcasted_iota(
        jnp.int32, (SGU_CHUNK, SGU_CHUNK), 1)
    head = lax.broadcasted_iota(jnp.int32, v.shape, 1) // HEAD_DIM
    mixed = bfull
    for h, w in enumerate((w0, w1, w2, w3)):
        mixed = mixed + _bdot(jnp.where(tri, w, 0.0), jnp.where(head == h, v, 0.0))
    return (u * mixed,)


def _f_s5disc(lam_re, lam_im, log_dt, b_re, b_im):
    dt = jnp.exp(log_dt)
    mag = jnp.exp(lam_re * dt)
    abar_re = mag * jnp.cos(lam_im * dt)
    abar_im = mag * jnp.sin(lam_im * dt)
    denom = jnp.square(lam_re) + jnp.square(lam_im)
    num_re = abar_re - 1.0
    num_im = abar_im
    fac_re = (num_re * lam_re + num_im * lam_im) / denom
    fac_im = (num_im * lam_re - num_re * lam_im) / denom
    return abar_re, abar_im, fac_re * b_re - fac_im * b_im, fac_re * b_im + fac_im * b_re


def _f_s5post(s_re, s_im, u, c_re, c_im, d, gw, gb):
    y = _bdot(s_re, c_re) - _bdot(s_im, c_im) + d * u
    y = jax.nn.gelu(y)
    return (y * jax.nn.sigmoid(_bdot(y, gw) + gb),)


def _f_lrupre(xc, wa, ba, wx, bx, lam):
    r = jax.nn.sigmoid(_bdot(xc, wa) + ba)
    i = jax.nn.sigmoid(_bdot(xc, wx) + bx)
    log_a = -LRU_C * r * jax.nn.softplus(-lam)
    a = jnp.exp(log_a)
    one_minus_a2 = -jnp.tanh(log_a) * (jnp.exp(2.0 * log_a) + 1.0)
    return a, jnp.sqrt(one_minus_a2) * (i * xc)


def _f_lrupost(h, gate):
    return (h * jax.nn.gelu(gate),)


def _f_logsig(zf, bf):
    return (jax.nn.log_sigmoid(zf + bf),)


def _f_gnorm(ya, yb, yc, yd, g):
    def n(y):
        return y * lax.rsqrt(jnp.mean(jnp.square(y), axis=-1, keepdims=True) + RMS_EPS)
    return (jnp.concatenate([n(ya), n(yb), n(yc), n(yd)], axis=1) * g,)


sgu_mix = _make_rw(_f_sgu, "sgu", SGU_CHUNK, 2, [MIXER_WIDTH])
s5_disc = _make_rw(_f_s5disc, "s5disc", S5_GROUPS * S5_GROUP, 5, [S5_STATE] * 4)
s5_post = _make_rw(_f_s5post, "s5post", 256, 3, [MIXER_WIDTH])
lru_pre = _make_rw(_f_lrupre, "lrupre", 512, 1, [MIXER_WIDTH, MIXER_WIDTH])
lru_post = _make_rw(_f_lrupost, "lrupost", 512, 2, [MIXER_WIDTH])
log_sig = _make_rw(_f_logsig, "logsig", 512, 1, [128])


SCAN_TILE = 512


def _prev_spec(c, nt, rev):
    per = SCAN_TILE // 8
    if rev:
        return pl.BlockSpec((8, c), lambda i: (jnp.maximum((nt - 1 - i) * per - 1, 0), 0))
    return pl.BlockSpec((8, c), lambda i: (jnp.maximum(i * per - 1, 0), 0))


SCAN_STEPS = (1, 2, 4)


def _cmul(ar, ai, br, bi):
    return ar * br - ai * bi, ar * bi + ai * br


def _rows_down(x, k, fill, rowid):
    return jnp.where(rowid >= k, pltpu.roll(x, k, 0), fill)


def _rows_up(x, k, fill, rowid):
    return jnp.where(rowid < 8 - k, pltpu.roll(x, 8 - k, 0), fill)


def _powers(ar, ai):
    pw = [(ar, ai)]
    for _ in range(7):
        pw.append(_cmul(*pw[-1], ar, ai))
    return pw


def _block(i):
    return pl.ds(pl.multiple_of(i * 8, 8), 8)


def _row_before(ref, i, edge):
    return jnp.where(i == 0, edge, ref[pl.ds(jnp.maximum(i * 8 - 1, 0), 1), :])


def _lti_fwd_call(a_re, a_im, b_re, b_im):
    t, c = b_re.shape
    tt = SCAN_TILE

    def body(ar_ref, ai_ref, br_ref, bi_ref, sr_ref, si_ref, cr, ci):
        @pl.when(pl.program_id(0) == 0)
        def _():
            cr[...] = jnp.zeros_like(cr)
            ci[...] = jnp.zeros_like(ci)

        pw = _powers(ar_ref[...], ai_ref[...])
        apr = jnp.concatenate([p[0] for p in pw], axis=0)
        api = jnp.concatenate([p[1] for p in pw], axis=0)
        rowid = lax.broadcasted_iota(jnp.int32, (8, c), 0)

        def block(i, carry):
            xr, xi = br_ref[_block(i), :], bi_ref[_block(i), :]
            for k in SCAN_STEPS:
                dr, di = _cmul(*pw[k - 1], _rows_down(xr, k, 0.0, rowid), _rows_down(xi, k, 0.0, rowid))
                xr, xi = xr + dr, xi + di
            dr, di = _cmul(apr, api, *carry)
            xr, xi = xr + dr, xi + di
            sr_ref[_block(i), :] = xr
            si_ref[_block(i), :] = xi
            return xr[7:8, :], xi[7:8, :]

        hr, hi = lax.fori_loop(0, tt // 8, block, (cr[...], ci[...]), unroll=2)
        cr[...] = hr
        ci[...] = hi

    row = pl.BlockSpec((tt, c), lambda i: (i, 0))
    par = pl.BlockSpec((1, c), lambda i: (0, 0))
    return pl.pallas_call(
        body, name="lti_scan_f", grid=(t // tt,), in_specs=[par, par, row, row], out_specs=[row, row],
        out_shape=[jax.ShapeDtypeStruct((t, c), F32)] * 2,
        scratch_shapes=[pltpu.VMEM((1, c), F32)] * 2, compiler_params=_cp(("arbitrary",)))(a_re, a_im, b_re, b_im)


def _lti_bwd_call(a_re, a_im, s_re, s_im, g_re, g_im):
    t, c = g_re.shape
    tt = SCAN_TILE
    nt = t // tt
    nb = tt // 8

    def body(ar_ref, ai_ref, sr_ref, si_ref, pr_ref, pi_ref, gr_ref, gi_ref,
             or_ref, oi_ref, dar_ref, dai_ref, cr, ci):
        ti = pl.program_id(0)

        @pl.when(ti == 0)
        def _():
            cr[...] = jnp.zeros_like(cr)
            ci[...] = jnp.zeros_like(ci)
            dar_ref[...] = jnp.zeros_like(dar_ref)
            dai_ref[...] = jnp.zeros_like(dai_ref)

        pw = _powers(ar_ref[...], -ai_ref[...])
        tpr = jnp.concatenate([p[0] for p in reversed(pw)], axis=0)
        tpi = jnp.concatenate([p[1] for p in reversed(pw)], axis=0)
        rowid = lax.broadcasted_iota(jnp.int32, (8, c), 0)
        first = ti == nt - 1
        edge_r = jnp.where(first, 0.0, pr_ref[7:8, :])
        edge_i = jnp.where(first, 0.0, pi_ref[7:8, :])

        def block(kk, carry):
            i = nb - 1 - kk
            gr_c, gi_c, acc_r, acc_i = carry
            xr, xi = gr_ref[_block(i), :], gi_ref[_block(i), :]
            for k in SCAN_STEPS:
                dr, di = _cmul(*pw[k - 1], _rows_up(xr, k, 0.0, rowid), _rows_up(xi, k, 0.0, rowid))
                xr, xi = xr + dr, xi + di
            dr, di = _cmul(tpr, tpi, gr_c, gi_c)
            xr, xi = xr + dr, xi + di
            or_ref[_block(i), :] = xr
            oi_ref[_block(i), :] = xi
            spr = _rows_down(sr_ref[_block(i), :], 1, _row_before(sr_ref, i, edge_r), rowid)
            spi = _rows_down(si_ref[_block(i), :], 1, _row_before(si_ref, i, edge_i), rowid)
            return xr[0:1, :], xi[0:1, :], acc_r + spr * xr + spi * xi, acc_i + spr * xi - spi * xr

        zero = jnp.zeros((8, c), F32)
        gr_c, gi_c, acc_r, acc_i = lax.fori_loop(0, nb, block, (cr[...], ci[...], zero, zero), unroll=2)
        cr[...] = gr_c
        ci[...] = gi_c
        dar_ref[...] += jnp.sum(acc_r, axis=0, keepdims=True)
        dai_ref[...] += jnp.sum(acc_i, axis=0, keepdims=True)

    row = pl.BlockSpec((tt, c), lambda i: (nt - 1 - i, 0))
    par = pl.BlockSpec((1, c), lambda i: (0, 0))
    prev = _prev_spec(c, nt, True)
    return pl.pallas_call(
        body, name="lti_scan_b", grid=(nt,), in_specs=[par, par, row, row, prev, prev, row, row],
        out_specs=[row, row, par, par],
        out_shape=[jax.ShapeDtypeStruct((t, c), F32)] * 2 + [jax.ShapeDtypeStruct((1, c), F32)] * 2,
        scratch_shapes=[pltpu.VMEM((1, c), F32)] * 2,
        compiler_params=_cp(("arbitrary",)))(a_re, a_im, s_re, s_im, s_re, s_im, g_re, g_im)


@jax.custom_vjp
def lti_scan(a_re, a_im, b_re, b_im):
    return tuple(_lti_fwd_call(a_re, a_im, b_re, b_im))


def _lti_scan_fwd(a_re, a_im, b_re, b_im):
    s_re, s_im = _lti_fwd_call(a_re, a_im, b_re, b_im)
    return (s_re, s_im), (a_re, a_im, s_re, s_im)


def _lti_scan_bwd(r, g):
    a_re, a_im, s_re, s_im = r
    o_re, o_im, da_re, da_im = _lti_bwd_call(a_re, a_im, s_re, s_im, g[0], g[1])
    return da_re, da_im, o_re, o_im


lti_scan.defvjp(_lti_scan_fwd, _lti_scan_bwd)


def _tv_fwd_call(a, b):
    t, c = b.shape
    tt = SCAN_TILE

    def body(a_ref, b_ref, h_ref, ch):
        @pl.when(pl.program_id(0) == 0)
        def _():
            ch[...] = jnp.zeros_like(ch)

        rowid = lax.broadcasted_iota(jnp.int32, (8, c), 0)

        def block(i, h):
            ab, x = a_ref[_block(i), :], b_ref[_block(i), :]
            for k in SCAN_STEPS:
                x = x + ab * _rows_down(x, k, 0.0, rowid)
                ab = ab * _rows_down(ab, k, 1.0, rowid)
            x = x + ab * h
            h_ref[_block(i), :] = x
            return x[7:8, :]

        ch[...] = lax.fori_loop(0, tt // 8, block, ch[...], unroll=2)

    row = pl.BlockSpec((tt, c), lambda i: (i, 0))
    return pl.pallas_call(
        body, name="tv_scan_f", grid=(t // tt,), in_specs=[row, row], out_specs=row,
        out_shape=jax.ShapeDtypeStruct((t, c), F32), scratch_shapes=[pltpu.VMEM((1, c), F32)],
        compiler_params=_cp(("arbitrary",)))(a, b)


def _tv_bwd_call(a, h, g):
    t, c = g.shape
    tt = SCAN_TILE
    nt = t // tt
    nb = tt // 8

    def body(a_ref, h_ref, p_ref, g_ref, da_ref, db_ref, cg, ca):
        ti = pl.program_id(0)

        @pl.when(ti == 0)
        def _():
            cg[...] = jnp.zeros_like(cg)
            ca[...] = jnp.zeros_like(ca)

        rowid = lax.broadcasted_iota(jnp.int32, (8, c), 0)
        edge = jnp.where(ti == nt - 1, 0.0, p_ref[7:8, :])

        def block(kk, carry):
            i = nb - 1 - kk
            gc, a_next = carry
            ab, x = a_ref[_block(i), :], g_ref[_block(i), :]
            cb = _rows_up(ab, 1, a_next, rowid)
            for k in SCAN_STEPS:
                x = x + cb * _rows_up(x, k, 0.0, rowid)
                cb = cb * _rows_up(cb, k, 1.0, rowid)
            x = x + cb * gc
            db_ref[_block(i), :] = x
            da_ref[_block(i), :] = x * _rows_down(h_ref[_block(i), :], 1, _row_before(h_ref, i, edge), rowid)
            return x[0:1, :], ab[0:1, :]

        gc, a_next = lax.fori_loop(0, nb, block, (cg[...], ca[...]), unroll=2)
        cg[...] = gc
        ca[...] = a_next

    row = pl.BlockSpec((tt, c), lambda i: (nt - 1 - i, 0))
    return pl.pallas_call(
        body, name="tv_scan_b", grid=(nt,), in_specs=[row, row, _prev_spec(c, nt, True), row],
        out_specs=[row, row], out_shape=[jax.ShapeDtypeStruct((t, c), F32)] * 2,
        scratch_shapes=[pltpu.VMEM((1, c), F32)] * 2, compiler_params=_cp(("arbitrary",)))(a, h, h, g)


@jax.custom_vjp
def tv_scan(a, b):
    return _tv_fwd_call(a, b)


def _tv_scan_fwd(a, b):
    h = _tv_fwd_call(a, b)
    return h, (a, h)


def _tv_scan_bwd(r, g):
    a, h = r
    return tuple(_tv_bwd_call(a, h, g))


tv_scan.defvjp(_tv_scan_fwd, _tv_scan_bwd)


CONV_K = 4
CONV_ROWS = 512


def _conv_fwd_call(x, w, b):
    t, c = x.shape

    def body(x_ref, w_ref, b_ref, o_ref, xp):
        xp[0:8, :] = jnp.zeros((8, c), F32)
        xp[8:, :] = x_ref[...]
        for blk in range(t // CONV_ROWS):
            base = blk * CONV_ROWS
            acc = jnp.broadcast_to(b_ref[...], (CONV_ROWS, c))
            for kk in range(CONV_K):
                acc = acc + w_ref[kk:kk + 1, :] * xp[base + 5 + kk:base + 5 + kk + CONV_ROWS, :]
            o_ref[base:base + CONV_ROWS, :] = acc

    return pl.pallas_call(
        body, name="conv_f", out_shape=jax.ShapeDtypeStruct((t, c), F32),
        scratch_shapes=[pltpu.VMEM((t + 8, c), F32)], compiler_params=_cp())(x, w, b)


def _conv_bwd_call(x, w, g):
    t, c = x.shape

    def body(x_ref, w_ref, g_ref, dx_ref, dw_ref, db_ref, xp, gp):
        xp[0:8, :] = jnp.zeros((8, c), F32)
        xp[8:, :] = x_ref[...]
        gp[0:t, :] = g_ref[...]
        gp[t:, :] = jnp.zeros((8, c), F32)
        dw = [jnp.zeros((1, c), F32) for _ in range(CONV_K)]
        db = jnp.zeros((1, c), F32)
        for blk in range(t // CONV_ROWS):
            base = blk * CONV_ROWS
            gb = g_ref[base:base + CONV_ROWS, :]
            acc = jnp.zeros((CONV_ROWS, c), F32)
            for kk in range(CONV_K):
                acc = acc + w_ref[kk:kk + 1, :] * gp[base + 3 - kk:base + 3 - kk + CONV_ROWS, :]
                dw[kk] = dw[kk] + jnp.sum(gb * xp[base + 5 + kk:base + 5 + kk + CONV_ROWS, :], axis=0, keepdims=True)
            db = db + jnp.sum(gb, axis=0, keepdims=True)
            dx_ref[base:base + CONV_ROWS, :] = acc
        for kk in range(CONV_K):
            dw_ref[kk:kk + 1, :] = dw[kk]
        db_ref[...] = db

    return pl.pallas_call(
        body, name="conv_b",
        out_shape=[jax.ShapeDtypeStruct((t, c), F32), jax.ShapeDtypeStruct((CONV_K, c), F32),
                   jax.ShapeDtypeStruct((1, c), F32)],
        scratch_shapes=[pltpu.VMEM((t + 8, c), F32)] * 2, compiler_params=_cp())(x, w, g)


@jax.custom_vjp
def causal_conv(x, w, b):
    return _conv_fwd_call(x, w, b)


def _causal_conv_fwd(x, w, b):
    return _conv_fwd_call(x, w, b), (x, w)


def _causal_conv_bwd(r, g):
    return tuple(_conv_bwd_call(r[0], r[1], g))


causal_conv.defvjp(_causal_conv_fwd, _causal_conv_bwd)


ATT_TILE = 512
ATT_ROWS = 64
ATT_SCALE = HEAD_DIM ** -0.5


def _head_lane(val, lane, h):
    return jnp.sum(jnp.where(lane == h, val, 0.0), axis=1, keepdims=True)


def _attn_fwd_call(q, k, v, c128, cr):
    t, w = q.shape
    tq = ATT_TILE
    nq = t // tq
    k3, v3, cr4 = k.reshape(nq, tq, w), v.reshape(nq, tq, w), cr.reshape(N_HEADS, nq, 1, tq)

    def body(q_ref, k_ref, v_ref, c_ref, cr_ref, o_ref, lse_ref, qs_scr, cq_scr, s_scr, p_scr, acc_scr, m_scr, l_scr):
        i, h = pl.program_id(0), pl.program_id(1)
        hm = lax.broadcasted_iota(jnp.int32, (tq, w), 1) // HEAD_DIM == h
        lane = lax.broadcasted_iota(jnp.int32, (tq, 128), 1)
        qs_scr[...] = jnp.where(hm, q_ref[...] * ATT_SCALE, 0.0).astype(BF16)
        cq_scr[...] = _head_lane(c_ref[...], lane, h)
        m_scr[...] = jnp.full((tq, 1), NEG, F32)
        l_scr[...] = jnp.zeros((tq, 1), F32)
        acc_scr[...] = jnp.zeros((tq, w), F32)

        def tile(j, diagonal):
            s_scr[...] = _dot(qs_scr[...], k_ref[j], 1, 1)
            ck = cr_ref[0, j]

            def chunk(r, _):
                rows = pl.ds(pl.multiple_of(r * ATT_ROWS, ATT_ROWS), ATT_ROWS)
                s = s_scr[rows, :] + cq_scr[rows, :] - ck
                if diagonal:
                    keep = (r * ATT_ROWS + lax.broadcasted_iota(jnp.int32, (ATT_ROWS, tq), 0)
                            >= lax.broadcasted_iota(jnp.int32, (ATT_ROWS, tq), 1))
                    s = jnp.where(keep, s, NEG)
                m_old = m_scr[rows, :]
                m_new = jnp.maximum(m_old, jnp.max(s, axis=1, keepdims=True))
                p = jnp.exp(s - m_new)
                alpha = jnp.exp(m_old - m_new)
                l_scr[rows, :] = alpha * l_scr[rows, :] + jnp.sum(p, axis=1, keepdims=True)
                m_scr[rows, :] = m_new
                p_scr[rows, :] = p.astype(BF16)
                acc_scr[rows, :] = alpha * acc_scr[rows, :]
                return 0

            lax.fori_loop(0, tq // ATT_ROWS, chunk, 0)
            acc_scr[...] += _dot(p_scr[...], v_ref[j], 1, 0)

        def before_diagonal(j, _):
            tile(j, False)
            return 0

        lax.fori_loop(0, i, before_diagonal, 0)
        tile(i, True)
        out = jnp.where(hm, acc_scr[...] / l_scr[...], 0.0)
        lse = jnp.where(lane == h, m_scr[...] + jnp.log(l_scr[...]), 0.0)

        @pl.when(h == 0)
        def _():
            o_ref[...] = out
            lse_ref[...] = lse

        @pl.when(h > 0)
        def _():
            o_ref[...] += out
            lse_ref[...] += lse

    tile = pl.BlockSpec((tq, w), lambda i, h: (i, 0))
    tile_c = pl.BlockSpec((tq, 128), lambda i, h: (i, 0))
    whole = pl.BlockSpec((nq, tq, w), lambda i, h: (0, 0, 0))
    rows = pl.BlockSpec((1, nq, 1, tq), lambda i, h: (h, 0, 0, 0))
    col = lambda width, dtype: pltpu.VMEM((tq, width), dtype)
    return pl.pallas_call(
        body, name="attn_f", grid=(nq, N_HEADS), in_specs=[tile, whole, whole, tile_c, rows], out_specs=[tile, tile_c],
        out_shape=[jax.ShapeDtypeStruct((t, w), F32), jax.ShapeDtypeStruct((t, 128), F32)],
        scratch_shapes=[col(w, BF16), col(1, F32), col(tq, F32), col(tq, BF16), col(w, F32), col(1, F32), col(1, F32)],
        compiler_params=_cp(("arbitrary", "arbitrary")))(q.astype(BF16), k3.astype(BF16), v3.astype(BF16), c128, cr4)


def _attn_bwd_call(q, k, v, c128, cr, o, lse, do):
    t, w = q.shape
    tq = ATT_TILE
    nq = t // tq
    r3 = lambda a: a.reshape(nq, tq, a.shape[-1])
    cr4 = cr.reshape(N_HEADS, nq, 1, tq)

    def body(q_ref, k_ref, v_ref, c_ref, cr_ref, o_ref, lse_ref, do_ref,
             dq_ref, dk_ref, dv_ref, dc_ref, dcr_ref,
             qm_scr, dom_scr, cq_scr, lse_scr, delta_scr, s_scr, dp_scr, p_scr, ds_scr, dk_scr, dv_scr, dck_scr):
        j, h = pl.program_id(0), pl.program_id(1)

        @pl.when((j == 0) & (h == 0))
        def _():
            dq_ref[...] = jnp.zeros_like(dq_ref)
            dc_ref[...] = jnp.zeros_like(dc_ref)

        hm = lax.broadcasted_iota(jnp.int32, (tq, w), 1) // HEAD_DIM == h
        lane = lax.broadcasted_iota(jnp.int32, (tq, 128), 1)
        lane_c = lax.broadcasted_iota(jnp.int32, (ATT_ROWS, 128), 1)
        ck = cr_ref[0, 0]
        dk_scr[...] = jnp.zeros((tq, w), F32)
        dv_scr[...] = jnp.zeros((tq, w), F32)
        dck_scr[...] = jnp.zeros((1, tq), F32)

        def tile(i, diagonal):
            qm_scr[...] = jnp.where(hm, q_ref[i], 0.0).astype(BF16)
            dom = jnp.where(hm, do_ref[i], 0.0)
            dom_scr[...] = dom.astype(BF16)
            delta_scr[...] = jnp.sum(dom * o_ref[i], axis=1, keepdims=True)
            cq_scr[...] = _head_lane(c_ref[i], lane, h)
            lse_scr[...] = _head_lane(lse_ref[i], lane, h)
            s_scr[...] = _dot(qm_scr[...] * ATT_SCALE, k_ref[...], 1, 1)
            dp_scr[...] = _dot(dom_scr[...], v_ref[...], 1, 1)

            def chunk(r, dck):
                rows = pl.ds(pl.multiple_of(r * ATT_ROWS, ATT_ROWS), ATT_ROWS)
                p = jnp.exp(s_scr[rows, :] + cq_scr[rows, :] - ck - lse_scr[rows, :])
                if diagonal:
                    keep = (r * ATT_ROWS + lax.broadcasted_iota(jnp.int32, (ATT_ROWS, tq), 0)
                            >= lax.broadcasted_iota(jnp.int32, (ATT_ROWS, tq), 1))
                    p = jnp.where(keep, p, 0.0)
                ds = p * (dp_scr[rows, :] - delta_scr[rows, :])
                p_scr[rows, :] = p.astype(BF16)
                ds_scr[rows, :] = ds.astype(BF16)
                dc_ref[i, rows, :] += jnp.where(lane_c == h, jnp.sum(ds, axis=1, keepdims=True), 0.0)
                return dck - jnp.sum(ds, axis=0, keepdims=True)

            dck_scr[...] = lax.fori_loop(0, tq // ATT_ROWS, chunk, dck_scr[...])
            dv_scr[...] += _dot(p_scr[...], dom_scr[...], 0, 0)
            dq_ref[i] += jnp.where(hm, _dot(ds_scr[...], k_ref[...], 1, 0), 0.0) * ATT_SCALE
            dk_scr[...] += _dot(ds_scr[...], qm_scr[...], 0, 0) * ATT_SCALE

        def after_diagonal(i, _):
            tile(i, False)
            return 0

        tile(j, True)
        lax.fori_loop(j + 1, nq, after_diagonal, 0)
        dcr_ref[0, 0] = dck_scr[...]

        @pl.when(h == 0)
        def _():
            dk_ref[...] = dk_scr[...]
            dv_ref[...] = dv_scr[...]

        @pl.when(h > 0)
        def _():
            dk_ref[...] += dk_scr[...]
            dv_ref[...] += dv_scr[...]

    whole = pl.BlockSpec((nq, tq, w), lambda j, h: (0, 0, 0))
    whole_c = pl.BlockSpec((nq, tq, 128), lambda j, h: (0, 0, 0))
    tile = pl.BlockSpec((None, tq, w), lambda j, h: (j, 0, 0))
    tile_r = pl.BlockSpec((1, 1, 1, tq), lambda j, h: (h, j, 0, 0))
    s3 = jax.ShapeDtypeStruct((nq, tq, w), F32)
    col = lambda width, dtype: pltpu.VMEM((tq, width), dtype)
    b16 = lambda a: r3(a).astype(BF16)
    dq, dk, dv, dc, dcr = pl.pallas_call(
        body, name="attn_b", grid=(nq, N_HEADS),
        in_specs=[whole, tile, tile, whole_c, tile_r, whole, whole_c, whole],
        out_specs=[whole, tile, tile, whole_c, tile_r],
        out_shape=[s3, s3, s3, jax.ShapeDtypeStruct((nq, tq, 128), F32),
                   jax.ShapeDtypeStruct((N_HEADS, nq, 1, tq), F32)],
        scratch_shapes=[col(w, BF16), col(w, BF16), col(1, F32), col(1, F32), col(1, F32), col(tq, F32), col(tq, F32),
                        col(tq, BF16), col(tq, BF16), col(w, F32), col(w, F32), pltpu.VMEM((1, tq), F32)],
        compiler_params=_cp(("arbitrary", "arbitrary")))(b16(q), b16(k), b16(v), r3(c128), cr4, r3(o), r3(lse), r3(do))
    return dq.reshape(t, w), dk.reshape(t, w), dv.reshape(t, w), dc.reshape(t, 128), dcr.reshape(N_HEADS, 1, t)


@jax.custom_vjp
def fox_attention(q, k, v, c128, cr):
    return _attn_fwd_call(q, k, v, c128, cr)[0]


def _fox_attention_fwd(q, k, v, c128, cr):
    o, lse = _attn_fwd_call(q, k, v, c128, cr)
    return o, (q, k, v, c128, cr, o, lse)


def _fox_attention_bwd(r, g):
    return _attn_bwd_call(*r, g)


fox_attention.defvjp(_fox_attention_fwd, _fox_attention_bwd)


def _loss_call(x, g, target):
    t, d = x.shape
    tr = 512

    def body(x_ref, g_ref, t_ref, loss_ref, dx_ref, dg_ref):
        tgt = t_ref[...]

        def f(xv, gv):
            return 0.5 * jnp.sum(jnp.mean(jnp.square(_rms(xv, gv) - tgt), axis=-1))

        val, vjp = jax.vjp(f, x_ref[...], g_ref[...])
        dx, dg = vjp(jnp.ones((), F32))
        dx_ref[...] = dx

        @pl.when(pl.program_id(0) == 0)
        def _():
            loss_ref[...] = jnp.zeros_like(loss_ref)
            dg_ref[...] = jnp.zeros_like(dg_ref)

        loss_ref[...] += jnp.full(loss_ref.shape, val, F32)
        dg_ref[...] += dg

    row = _row_spec(tr, d)
    return pl.pallas_call(
        body, name="loss_head", grid=(t // tr,), in_specs=[row, _full_spec(g), row],
        out_specs=[pl.BlockSpec((1, 128), lambda i: (0, 0)), row, _full_spec(g)],
        out_shape=[jax.ShapeDtypeStruct((1, 128), F32), jax.ShapeDtypeStruct((t, d), F32),
                   jax.ShapeDtypeStruct(g.shape, F32)],
        compiler_params=_cp(("arbitrary",)))(x, g, target)


def _blockdiag(w):
    g, a, b = w.shape
    return jnp.einsum('gab,gk->gakb', w, jnp.eye(g, dtype=w.dtype)).reshape(g * a, g * b)


def _s5_params(p):
    rep = lambda a: jnp.repeat(a, S5_GROUP, axis=0)
    rows = S5_GROUPS * S5_GROUP
    bt = lambda b: b.transpose(0, 2, 1).reshape(rows, S5_STATE)
    abar_re, abar_im, bb_re, bb_im = s5_disc(
        rep(p["s5_lambda_re"]), rep(p["s5_lambda_im"]), rep(p["s5_log_dt"][:, None]), bt(p["s5_b_re"]), bt(p["s5_b_im"]))
    first = lambda a: a.reshape(S5_GROUPS, S5_GROUP, S5_STATE)[:, 0, :].reshape(1, S5_GROUPS * S5_STATE)
    g3 = lambda a: a.reshape(S5_GROUPS, S5_GROUP, S5_STATE)
    cblk = lambda c: _blockdiag(c.transpose(0, 2, 1))
    return (first(abar_re), first(abar_im), _blockdiag(g3(bb_re)), _blockdiag(g3(bb_im)),
            cblk(p["s5_c_re"]), cblk(p["s5_c_im"]))


MIXER_WEIGHTS = ('sgu_norm_g', 'sgu_w', 'sgu_b', 's5_lambda_re', 's5_lambda_im', 's5_log_dt', 's5_b_re', 's5_b_im',
                 's5_c_re', 's5_c_im', 's5_d', 's5_glu_w', 's5_glu_b', 'lru_conv_w', 'lru_conv_b', 'lru_wa', 'lru_ba',
                 'lru_wx', 'lru_bx', 'lru_lambda', 'fox_fgate_b')


def _mixers(z, p):
    w = MIXER_WIDTH
    row = lambda a: a[None, :]
    a_u, a_v, b_in, c_x, c_gate, d_q, d_k, d_v, d_f = jnp.split(z, [w * i for i in range(1, 9)], axis=1)
    sw = p["sgu_w"]
    (y_a,) = sgu_mix(a_u, a_v, row(p["sgu_norm_g"]), sw[0], sw[1], sw[2], sw[3],
                     jnp.repeat(p["sgu_b"].T, HEAD_DIM, axis=1))
    abar_re, abar_im, bblk_re, bblk_im, cblk_re, cblk_im = _s5_params(p)
    s_re, s_im = lti_scan(abar_re, abar_im, mm(b_in, bblk_re), mm(b_in, bblk_im))
    (y_b,) = s5_post(s_re, s_im, b_in, cblk_re, cblk_im, row(p["s5_d"]), p["s5_glu_w"], row(p["s5_glu_b"]))
    xc = causal_conv(c_x, p["lru_conv_w"], row(p["lru_conv_b"]))
    a, b = lru_pre(xc, _blockdiag(p["lru_wa"]), p["lru_ba"].reshape(1, w), _blockdiag(p["lru_wx"]),
                   p["lru_bx"].reshape(1, w), row(p["lru_lambda"]))
    (y_c,) = lru_post(tv_scan(a, b), c_gate)
    (log_f,) = log_sig(d_f, jnp.pad(p["fox_fgate_b"], (0, 128 - N_HEADS))[None, :])
    c128 = tv_scan(jnp.ones_like(log_f), log_f)
    y_d = fox_attention(d_q, d_k, d_v, c128, c128[:, :N_HEADS].T[:, None, :])
    return y_a, y_b, y_c, y_d


PACK_COLS = 1024
SHARD_SHAPE = {'w_mlp_in': (1024, 1024), 'w_mlp_out': (1024, 1024), 'w_out': (256, 1024), 'w_in': (1024, 513),
               's5_glu_w': (64, 256), 'lru_conv_w': (4, 64)}
SHARDED_AXIS = {'w_in': 1, 's5_glu_w': 0, 'lru_conv_w': 1, 'w_out': 0, 'w_mlp_in': 1, 'w_mlp_out': 0}
SHARD_ROWS = {n: -(-s[0] * s[1] // PACK_COLS) for n, s in SHARD_SHAPE.items()}
SHARD_OFF = {n: sum(list(SHARD_ROWS.values())[:i]) for i, n in enumerate(SHARD_SHAPE)}
LAYER_ROWS = 2880
SMALL_OFF = SHARD_OFF['w_in']
GLUE_ROWS = LAYER_ROWS - SMALL_OFF
assert SHARD_OFF['w_mlp_out'] == 1024 and SHARD_OFF['w_out'] == 2048 and SMALL_OFF % GLUE_ROWS == 0
assert SHARD_OFF['lru_conv_w'] + SHARD_ROWS['lru_conv_w'] <= LAYER_ROWS
PACK_ROWS = DEPTH * LAYER_ROWS
TOK = 1024
FF = 4 * D_MODEL


def _w4(i_of):
    return pl.BlockSpec((None, None, 1024, PACK_COLS), i_of)


def _tile2(rows, cols, i_of):
    return pl.BlockSpec((rows, cols), i_of)


def _layer_fwd(x, l, gathered, w_in, w_out, p, mix_p):
    t = x.shape[0]
    nt = t // TOK
    f32 = lambda r, c: jax.ShapeDtypeStruct((r, c), F32)
    b16 = lambda r, c: jax.ShapeDtypeStruct((r, c), BF16)
    g1, g2, gm = p["norm1_g"][None, :], p["norm2_g"][None, :], p["mix_norm_g"][None, :]
    (h1,) = _rowwise(_f_rms, [x], [g1], [D_MODEL], name="rms_f", tr=512, dtype=BF16)
    z = _matmul("mm_in", (nt, 1, 1), h1, _tile2(TOK, D_MODEL, lambda i, j, k: (i, 0)),
                w_in, _tile2(D_MODEL, D_IN_PAD, lambda i, j, k: (0, 0)), (1, 0),
                [(f32(t, D_IN_PAD), _tile2(TOK, D_IN_PAD, lambda i, j, k: (i, 0)))])
    ys, mix_vjp = jax.vjp(_mixers, z, mix_p)
    (yn,) = _rowwise(_f_gnorm, list(ys), [gm], [D_MODEL], name="gnorm_f", tr=512, dtype=BF16)
    x_tile = _tile2(TOK, D_MODEL, lambda i, j, k: (i, 0))
    x1 = _matmul("mm_out", (nt, 1, 1), yn, x_tile, w_out, _tile2(D_MODEL, D_MODEL, lambda i, j, k: (0, 0)), (1, 0),
                 [(f32(t, D_MODEL), x_tile)], extras=[(x, x_tile)], epilogue=lambda acc, r: (acc + r,))
    (h2,) = _rowwise(_f_rms, [x1], [g2], [D_MODEL], name="rms_f", tr=512, dtype=BF16)
    ff_tile = _tile2(TOK, 1024, lambda i, j, k: (i, j))

    def up_epilogue(acc):
        r = jnp.maximum(acc, 0.0)
        return r * r, r

    act, relu = _matmul("mm_up", (nt, FF // 1024, 1), h2, x_tile, gathered, _w4(lambda i, j, k: (j, l, 0, 0)), (1, 0),
                        [(b16(t, FF), ff_tile), (b16(t, FF), ff_tile)], epilogue=up_epilogue)
    x2 = _matmul("mm_down", (nt, 1, FF // 1024), act, _tile2(TOK, 1024, lambda i, j, k: (i, k)),
                 gathered, _w4(lambda i, j, k: (k, l, 1, 0)), (1, 0),
                 [(f32(t, D_MODEL), x_tile)], extras=[(x1, x_tile)], epilogue=lambda acc, r: (acc + r,))
    return x2, (x, h1, mix_vjp, ys, yn, x1, h2, act, relu)


def _layer_bwd(g, l, res, gathered, w_in, w_out, p, send):
    x, h1, mix_vjp, ys, yn, x1, h2, act, relu = res
    t = x.shape[0]
    nt = t // TOK
    f32 = lambda r, c: jax.ShapeDtypeStruct((r, c), F32)
    g1, g2, gm = p["norm1_g"][None, :], p["norm2_g"][None, :], p["mix_norm_g"][None, :]
    x_tile = _tile2(TOK, D_MODEL, lambda i, j, k: (i, 0))
    ff_tile = _tile2(TOK, 1024, lambda i, j, k: (i, j))
    tok_k = _tile2(TOK, D_MODEL, lambda i, j, k: (k, 0))
    send_s = jax.ShapeDtypeStruct(send.shape, send.dtype)
    du = _matmul("mm_down_dx", (nt, FF // 1024, 1), g, x_tile, gathered, _w4(lambda i, j, k: (j, l, 1, 0)), (1, 1),
                 [(jax.ShapeDtypeStruct((t, FF), BF16), ff_tile)], extras=[(relu, ff_tile)],
                 epilogue=lambda acc, r: (2.0 * r.astype(F32) * acc,))
    send = _matmul("mm_down_dw", (FF // 1024, 1, nt), act, _tile2(TOK, 1024, lambda i, j, k: (k, i)), g, tok_k, (0, 0),
                   [(send_s, _w4(lambda i, j, k: (i, l, 1, 0)))], into=send)
    send = _matmul("mm_up_dw", (1, FF // 1024, nt), h2, tok_k, du, _tile2(TOK, 1024, lambda i, j, k: (k, j)), (0, 0),
                   [(send_s, _w4(lambda i, j, k: (j, l, 0, 0)))], into=send)
    dh2 = _matmul("mm_up_dx", (nt, 1, FF // 1024), du, _tile2(TOK, 1024, lambda i, j, k: (i, k)),
                  gathered, _w4(lambda i, j, k: (k, l, 0, 0)), (1, 1), [(f32(t, D_MODEL), x_tile)])
    g_mid, dg2 = _rowwise_vjp(_f_rms, [x1], [g2], [dh2], name="rms_b", tr=512, add=g)
    w_full = _tile2(D_MODEL, D_MODEL, lambda i, j, k: (0, 0))
    dyn = _matmul("mm_out_dx", (nt, 1, 1), g_mid, x_tile, w_out, w_full, (1, 1), [(f32(t, D_MODEL), x_tile)])
    send = _matmul("mm_out_dw", (4, 1, nt), yn, _tile2(TOK, 256, lambda i, j, k: (k, i)), g_mid, tok_k, (0, 0),
                   [(send_s, pl.BlockSpec((None, None, 256, PACK_COLS),
                                          lambda i, j, k: (i, l, SHARD_OFF['w_out'] // 256, 0)))], into=send)
    *dys, dgm = _rowwise_vjp(_f_gnorm, list(ys), [gm], [dyn], name="gnorm_b", tr=512)
    dz, dmix = mix_vjp(tuple(dys))
    dz = dz.astype(BF16)
    z_tile = _tile2(TOK, D_IN_PAD, lambda i, j, k: (i, 0))
    d_w_in = _matmul("mm_in_dw", (1, 1, t // 512), h1, _tile2(512, D_MODEL, lambda i, j, k: (k, 0)),
                     dz, _tile2(512, D_IN_PAD, lambda i, j, k: (k, 0)), (0, 0),
                     [(f32(D_MODEL, D_IN_PAD), _tile2(D_MODEL, D_IN_PAD, lambda i, j, k: (0, 0)))])
    dh1 = _matmul("mm_in_dx", (nt, 1, 1), dz, z_tile, w_in, _tile2(D_MODEL, D_IN_PAD, lambda i, j, k: (0, 0)), (1, 1),
                  [(f32(t, D_MODEL), x_tile)])
    dx, dg1 = _rowwise_vjp(_f_rms, [x], [g1], [dh1], name="rms_b", tr=512, add=g_mid)
    small = dict(dmix, norm1_g=dg1[0], norm2_g=dg2[0], mix_norm_g=dgm[0])
    return dx, small, d_w_in, send


HBM = pl.BlockSpec(memory_space=pltpu.HBM)
D2D_CHUNKS = 16
ICI_CHUNKS = 4


def _coords():
    return lax.axis_index("x"), lax.axis_index("y"), lax.axis_index("c")


def _other_chips(x, y):
    return [(1 - x, y), (x, 1 - y), (1 - x, 1 - y)]


def _start_chunks(make, rows, n):
    size = rows // n
    assert size * n == rows
    for k in range(n):
        make(pl.ds(k * size, size)).start()


def _allgather_shards(shard):
    r, cols = shard.shape
    rh = r // 2

    def body(in_ref, out_ref, send_sems, recv_sems):
        x, y, c = _coords()
        me, sibling = (x, y, c), (x, y, 1 - c)
        chips = _other_chips(x, y)

        def half(px, py, pc, rows=pl.ds(0, rh)):
            return out_ref.at[2 * px + py, pl.ds(pc * rh + rows.start, rows.size), :]

        def copy(k, block, to, rows=pl.ds(0, rh), from_input=False):
            src = in_ref.at[pl.ds(block[2] * rh + rows.start, rows.size), :] if from_input else half(*block, rows)
            return pltpu.make_async_remote_copy(
                src_ref=src, dst_ref=half(*block, rows), send_sem=send_sems.at[k], recv_sem=recv_sems.at[k],
                device_id=to, device_id_type=MESH)

        def own(rows=pl.ds(0, r)):
            return pltpu.make_async_remote_copy(
                src_ref=in_ref.at[rows, :], dst_ref=out_ref.at[2 * x + y, rows, :], send_sem=send_sems.at[6],
                recv_sem=recv_sems.at[6], device_id=sibling, device_id_type=MESH)

        for j, chip in enumerate(chips):
            _start_chunks(lambda rows: copy(j, me, (*chip, c), rows, from_input=True), rh, ICI_CHUNKS)
        _start_chunks(own, r, D2D_CHUNKS)
        for j, chip in enumerate(chips):
            copy(j, (*chip, c), me).wait_recv()
            _start_chunks(lambda rows: copy(3 + j, (*chip, c), sibling, rows), rh, D2D_CHUNKS)
        for j, chip in enumerate(chips):
            copy(3 + j, (*chip, 1 - c), me).wait_recv()
        for j, chip in enumerate(chips):
            copy(j, me, (*chip, c), from_input=True).wait_send()
            copy(3 + j, (*chip, c), sibling).wait_send()
        own().wait()

    return pl.pallas_call(
        body, name="allgather_shards", out_shape=jax.ShapeDtypeStruct((4, r, cols), shard.dtype),
        in_specs=[HBM], out_specs=HBM,
        scratch_shapes=[pltpu.SemaphoreType.DMA((7,)), pltpu.SemaphoreType.DMA((7,))],
        compiler_params=pltpu.CompilerParams())(shard)


def _pair_exchange(g):
    s, _, rh, cols = g.shape

    def body(g_ref, recv_ref, send_sem, recv_sem):
        x, y, c = _coords()

        def copy(slot, rows):
            return pltpu.make_async_remote_copy(
                src_ref=g_ref.at[slot, 1 - c, rows, :], dst_ref=recv_ref.at[slot, rows, :], send_sem=send_sem,
                recv_sem=recv_sem, device_id=(x, y, 1 - c), device_id_type=MESH)

        for slot in range(s):
            _start_chunks(lambda rows: copy(slot, rows), rh, D2D_CHUNKS // s)
        pltpu.make_async_remote_copy(
            src_ref=g_ref.at[:, 1 - c], dst_ref=recv_ref, send_sem=send_sem, recv_sem=recv_sem,
            device_id=(x, y, 1 - c), device_id_type=MESH).wait()

    return pl.pallas_call(
        body, name="pair_exchange", out_shape=jax.ShapeDtypeStruct((s, rh, cols), g.dtype), in_specs=[HBM],
        out_specs=HBM, scratch_shapes=[pltpu.SemaphoreType.DMA] * 2, compiler_params=pltpu.CompilerParams())(g)


def _chip_exchange(p):
    rh = p.shape[1]

    def body(p_ref, recv_ref, send_sems, recv_sems):
        x, y, c = _coords()
        chips = _other_chips(x, y)

        def copy(j, chip, rows=pl.ds(0, rh)):
            return pltpu.make_async_remote_copy(
                src_ref=p_ref.at[2 * chip[0] + chip[1], rows, :], dst_ref=recv_ref.at[j, rows, :],
                send_sem=send_sems.at[j], recv_sem=recv_sems.at[j], device_id=(*chip, c), device_id_type=MESH)

        for j, chip in enumerate(chips):
            _start_chunks(lambda rows: copy(j, chip, rows), rh, ICI_CHUNKS)
        for j, chip in enumerate(chips):
            copy(j, chip).wait_recv()
        for j, chip in enumerate(chips):
            copy(j, chip).wait_send()

    return pl.pallas_call(
        body, name="chip_exchange", out_shape=jax.ShapeDtypeStruct((3,) + p.shape[1:], p.dtype), in_specs=[HBM],
        out_specs=HBM, scratch_shapes=[pltpu.SemaphoreType.DMA((3,)), pltpu.SemaphoreType.DMA((3,))],
        compiler_params=pltpu.CompilerParams())(p)


def _sum_chips(p, recv, tr):
    _, rh, cols = p.shape
    x, y, c = _coords()
    where = jnp.stack([2 * x + y, c]).astype(jnp.int32)

    def body(w_ref, own_ref, r_ref, o_ref):
        acc = own_ref[...].astype(F32)
        for k in range(3):
            acc = acc + r_ref[k].astype(F32)
        o_ref[...] = acc

    return pl.pallas_call(
        body, name="sum_chips", out_shape=jax.ShapeDtypeStruct((2, rh, cols), F32),
        grid_spec=pltpu.PrefetchScalarGridSpec(
            num_scalar_prefetch=1, grid=(rh // tr,),
            in_specs=[pl.BlockSpec((None, tr, cols), lambda i, w_ref: (w_ref[0], i, 0)),
                      pl.BlockSpec((3, tr, cols), lambda i, w_ref: (0, i, 0))],
            out_specs=pl.BlockSpec((None, tr, cols), lambda i, w_ref: (w_ref[1], i, 0))),
        compiler_params=_cp(("arbitrary",)))(where, p, recv)


def _pair_share(buf):
    _, rh, cols = buf.shape

    def body(in_ref, out_ref, send_sem, recv_sem):
        x, y, c = _coords()

        def copy(slot, rows=pl.ds(0, rh)):
            return pltpu.make_async_remote_copy(
                src_ref=in_ref.at[slot, rows, :], dst_ref=out_ref.at[slot, rows, :], send_sem=send_sem,
                recv_sem=recv_sem, device_id=(x, y, 1 - c), device_id_type=MESH)

        _start_chunks(lambda rows: copy(c, rows), rh, D2D_CHUNKS)
        copy(c).wait_send()
        copy(1 - c).wait_recv()

    return pl.pallas_call(
        body, name="pair_share", out_shape=jax.ShapeDtypeStruct(buf.shape, buf.dtype), in_specs=[HBM], out_specs=HBM,
        scratch_shapes=[pltpu.SemaphoreType.DMA] * 2, input_output_aliases={0: 0},
        compiler_params=pltpu.CompilerParams())(buf)


def _allgather_all(blk):
    m_per, cols = blk.shape
    whole = pl.ds(0, m_per)

    def body(x_ref, out_ref, send_sems, recv_sems, local_sem):
        x, y, c = _coords()
        me, sibling = (x, y, c), (x, y, 1 - c)
        chips = _other_chips(x, y)

        def rows_of(px, py, pc, rows):
            return out_ref.at[4 * px + 2 * py + pc, rows, :]

        def copy(k, block, to, rows=whole, from_input=False):
            return pltpu.make_async_remote_copy(
                src_ref=x_ref.at[rows, :] if from_input else rows_of(*block, rows), dst_ref=rows_of(*block, rows),
                send_sem=send_sems.at[k], recv_sem=recv_sems.at[k], device_id=to, device_id_type=MESH)

        mine = pltpu.make_async_copy(x_ref, rows_of(*me, whole), local_sem)
        mine.start()
        _start_chunks(lambda rows: copy(0, me, sibling, rows, from_input=True), m_per, ICI_CHUNKS)
        for j, chip in enumerate(chips):
            _start_chunks(lambda rows: copy(1 + j, me, (*chip, c), rows, from_input=True), m_per, ICI_CHUNKS)
        for j, chip in enumerate(chips):
            copy(1 + j, (*chip, c), me).wait_recv()
            _start_chunks(lambda rows: copy(4 + j, (*chip, c), sibling, rows), m_per, ICI_CHUNKS)
        copy(0, sibling, me).wait_recv()
        for j, chip in enumerate(chips):
            copy(4 + j, (*chip, 1 - c), me).wait_recv()
        copy(0, me, sibling, from_input=True).wait_send()
        for j, chip in enumerate(chips):
            copy(1 + j, me, (*chip, c), from_input=True).wait_send()
            copy(4 + j, (*chip, c), sibling).wait_send()
        mine.wait()

    return pl.pallas_call(
        body, name="allgather_all", out_shape=jax.ShapeDtypeStruct((8, m_per, cols), blk.dtype),
        in_specs=[pl.BlockSpec(memory_space=pltpu.VMEM)], out_specs=pl.BlockSpec(memory_space=pltpu.VMEM),
        scratch_shapes=[pltpu.SemaphoreType.DMA((7,)), pltpu.SemaphoreType.DMA((7,)), pltpu.SemaphoreType.DMA],
        compiler_params=pltpu.CompilerParams(vmem_limit_bytes=VMEM_LIMIT))(blk)


def _add_kept(g, recv, tr):
    s, _, rh, cols = g.shape

    def body(c_ref, a_ref, b_ref, o_ref):
        o_ref[...] = (a_ref[...].astype(F32) + b_ref[...].astype(F32)).astype(o_ref.dtype)

    spec = pl.BlockSpec((None, tr, cols), lambda si, i, c_ref: (si, i, 0))
    return pl.pallas_call(
        body, name="add_kept", out_shape=jax.ShapeDtypeStruct((s, rh, cols), BF16),
        grid_spec=pltpu.PrefetchScalarGridSpec(
            num_scalar_prefetch=1, grid=(s, rh // tr),
            in_specs=[pl.BlockSpec((None, None, tr, cols), lambda si, i, c_ref: (si, c_ref[0], i, 0)), spec],
            out_specs=spec),
        compiler_params=_cp(("arbitrary", "arbitrary")))(lax.axis_index("c").astype(jnp.int32).reshape(1), g, recv)


def _sum_slots(p, tr, name):
    s, rows, cols = p.shape

    def body(p_ref, o_ref):
        acc = p_ref[0].astype(F32)
        for k in range(1, s):
            acc = acc + p_ref[k].astype(F32)
        o_ref[...] = acc

    return pl.pallas_call(
        body, name=name, grid=(rows // tr,), in_specs=[pl.BlockSpec((s, tr, cols), lambda i: (0, i, 0))],
        out_specs=_row_spec(tr, cols), out_shape=jax.ShapeDtypeStruct((rows, cols), F32),
        compiler_params=_cp(("arbitrary",)))(p)


def _adamw_call(w, g, m, v, name):
    rows, cols = w.shape
    tr = _tile(rows, 512) if rows % 512 == 0 else _tile(rows, 128)
    c1 = 1.0 - ADAM_B1 ** ADAM_STEP
    c2 = 1.0 - ADAM_B2 ** ADAM_STEP

    def body(w_ref, g_ref, m_ref, v_ref, d_ref, nm_ref, nv_ref):
        gv = g_ref[...]
        nm = ADAM_B1 * m_ref[...] + (1.0 - ADAM_B1) * gv
        nv = ADAM_B2 * v_ref[...] + (1.0 - ADAM_B2) * jnp.square(gv)
        d_ref[...] = -ADAM_LR * ((nm / c1) / (jnp.sqrt(nv / c2) + ADAM_EPS) + ADAM_WD * w_ref[...])
        nm_ref[...] = nm
        nv_ref[...] = nv

    spec = _row_spec(tr, cols)
    o = jax.ShapeDtypeStruct((rows, cols), F32)
    return pl.pallas_call(body, name=name, grid=(rows // tr,), in_specs=[spec] * 4, out_specs=[spec] * 3,
                          out_shape=[o, o, o], compiler_params=_cp(("arbitrary",)))(w, g, m, v)


WEIGHTS = ('norm1_g', 'w_in', 'sgu_norm_g', 'sgu_w', 'sgu_b', 's5_lambda_re', 's5_lambda_im', 's5_log_dt',
           's5_b_re', 's5_b_im', 's5_c_re', 's5_c_im', 's5_d', 's5_glu_w', 's5_glu_b', 'lru_conv_w',
           'lru_conv_b', 'lru_wa', 'lru_ba', 'lru_wx', 'lru_bx', 'lru_lambda', 'fox_fgate_b', 'mix_norm_g',
           'w_out', 'norm2_g', 'w_mlp_in', 'w_mlp_out', 'final_g')
N_W = len(WEIGHTS)


def _pack_shards(shards, dtype, names=tuple(SHARD_SHAPE), rows=LAYER_ROWS):
    parts = []
    for n in names:
        lead = shards[n].shape[:-2]
        flat = shards[n].reshape(*lead, -1).astype(dtype)
        flat = jnp.pad(flat, [(0, 0)] * len(lead) + [(0, SHARD_ROWS[n] * PACK_COLS - flat.shape[-1])])
        parts.append(flat.reshape(*lead, SHARD_ROWS[n], PACK_COLS))
    lead = parts[0].shape[:-2]
    used = sum(SHARD_ROWS[n] for n in names)
    if rows > used:
        parts.append(jnp.zeros((*lead, rows - used, PACK_COLS), dtype))
    return jnp.concatenate(parts, axis=-2)


def _unpack_shards(buf, names=tuple(SHARD_SHAPE)):
    lead = buf.shape[:-2]
    out = {}
    for n in names:
        s0, s1 = SHARD_SHAPE[n]
        rows, off = SHARD_ROWS[n], SHARD_OFF[n]
        flat = buf[..., off:off + rows, :].reshape(*lead, rows * PACK_COLS)
        out[n] = flat[..., :s0 * s1].reshape(*lead, s0, s1)
    return out


def _join_chips(g, axis):
    _, d, s0, s1 = g.shape
    if axis == 0:
        return g.transpose(1, 0, 2, 3).reshape(d, 4 * s0, s1)
    return g.transpose(1, 2, 0, 3).reshape(d, s0, 4 * s1)


def _split_chips(w, axis):
    d = w.shape[0]
    if axis == 0:
        return w.reshape(d, 4, w.shape[1] // 4, w.shape[2]).transpose(1, 0, 2, 3)
    return w.reshape(d, w.shape[1], 4, w.shape[2] // 4).transpose(2, 0, 1, 3)


def _flat_rows(shape):
    return -(-math.prod(shape) // PACK_COLS)


def _pack_flat(arrs, rows):
    parts = []
    for a in arrs:
        flat = a.reshape(-1)
        r = _flat_rows(a.shape)
        parts.append(jnp.pad(flat, (0, r * PACK_COLS - flat.shape[0])).reshape(r, PACK_COLS))
    used = sum(p.shape[0] for p in parts)
    parts.append(jnp.zeros((rows - used, PACK_COLS), F32))
    return jnp.concatenate(parts, axis=0)


def _unpack_flat(buf, shapes):
    out, off = [], 0
    for s in shapes:
        r = _flat_rows(s)
        out.append(buf[off:off + r].reshape(-1)[:math.prod(s)].reshape(s))
        off += r
    return out


def _write_glue(send, glue):
    def body(g_ref, s_ref, o_ref):
        o_ref[...] = g_ref[...]

    blk = (None, None, GLUE_ROWS, PACK_COLS)
    return pl.pallas_call(
        body, name="write_glue", grid=(4, DEPTH),
        in_specs=[pl.BlockSpec(blk, lambda s, l: (s, l, 0, 0)), pl.BlockSpec(memory_space=pl.ANY)],
        out_specs=pl.BlockSpec(blk, lambda s, l: (s, l, SMALL_OFF // GLUE_ROWS, 0)),
        out_shape=jax.ShapeDtypeStruct(send.shape, send.dtype), input_output_aliases={1: 0},
        compiler_params=_cp(("arbitrary", "arbitrary")))(glue, send)


GLUE_PACKED = ('w_in', 's5_glu_w', 'lru_conv_w')


def _forward_backward(x, target, final_g, gathered, rep):
    joined = {n: _join_chips(g, SHARDED_AXIS[n])
              for n, g in _unpack_shards(gathered, GLUE_PACKED + ('w_out',)).items()}
    w_in = jnp.pad(joined['w_in'], ((0, 0), (0, 0), (0, D_IN_PAD - D_IN_PROJ)))
    w_out = joined['w_out']
    norm_p = [{n: rep[n][l] for n in ('norm1_g', 'norm2_g', 'mix_norm_g')} for l in range(DEPTH)]
    mix_p = [{n: joined[n][l].astype(F32) if n in joined else rep[n][l] for n in MIXER_WEIGHTS} for l in range(DEPTH)]

    residuals = []
    for l in range(DEPTH):
        x, res = _layer_fwd(x, l, gathered, w_in[l], w_out[l], norm_p[l], mix_p[l])
        residuals.append(res)
    loss_part, g, d_final = _loss_call(x, final_g[None, :], target)

    send = lax.empty((4, DEPTH, LAYER_ROWS, PACK_COLS), BF16)
    small, d_w_in = [None] * DEPTH, [None] * DEPTH
    for l in reversed(range(DEPTH)):
        g, small[l], d_w_in[l], send = _layer_bwd(g, l, residuals[l], gathered, w_in[l], w_out[l], norm_p[l], send)
    stacked = {n: jnp.stack([small[l][n] for l in range(DEPTH)]) for n in small[0]}
    stacked['w_in'] = jnp.stack(d_w_in)[:, :, :D_IN_PROJ]
    glue = _pack_shards({n: _split_chips(stacked.pop(n), SHARDED_AXIS[n]) for n in GLUE_PACKED}, BF16,
                        names=GLUE_PACKED, rows=GLUE_ROWS)
    return loss_part, g, d_final, stacked, _write_glue(send, glue)


def _step(*args):
    x, target = args[0], args[1 + N_W]
    w = dict(zip(WEIGHTS, args[1:1 + N_W]))
    m = dict(zip(WEIGHTS, args[2 + N_W:2 + 2 * N_W]))
    v = dict(zip(WEIGHTS, args[2 + 2 * N_W:2 + 3 * N_W]))
    small = [n for n in WEIGHTS if n not in SHARD_SHAPE]

    packed = _pack_shards({n: w[n] for n in SHARD_SHAPE}, BF16).reshape(PACK_ROWS, PACK_COLS)
    gathered = _allgather_shards(packed).reshape(4, DEPTH, LAYER_ROWS, PACK_COLS)

    loss_part, dx, d_final, dw, send = _forward_backward(
        x[0], target[0], w['final_g'], gathered, {n: w[n] for n in small})

    rh = PACK_ROWS // 2
    halves = send.reshape(4, 2, rh, PACK_COLS)
    chip_sum = _add_kept(halves, _pair_exchange(halves), 1440)
    pair = _pair_share(_sum_chips(chip_sum, _chip_exchange(chip_sum), 720))
    g_shard = _unpack_shards(pair.reshape(DEPTH, LAYER_ROWS, PACK_COLS))

    small_g = [d_final.reshape(-1) if n == 'final_g' else dw[n] for n in small]
    small_rows = -(-(sum(_flat_rows(w[n].shape) for n in small) + 1) // 128) * 128
    small_sum = _sum_slots(_allgather_all(_pack_flat(small_g + [loss_part[0, :1]], small_rows)), 128, "sum_devices")
    *g_small, loss = _unpack_flat(small_sum, [w[n].shape for n in small] + [()])

    grads, delta, new_m, new_v = {}, {}, {}, {}
    for n in SHARD_SHAPE:
        shp = w[n].shape
        v2 = lambda a: a.reshape(-1, shp[-1])
        res = _adamw_call(v2(w[n]), v2(g_shard[n]), v2(m[n]), v2(v[n]), "adamw_" + n)
        grads[n] = g_shard[n]
        delta[n], new_m[n], new_v[n] = (r.reshape(shp) for r in res)
    pk = lambda d: _pack_flat([d[n] for n in small], small_rows)
    res = _adamw_call(pk(w), _pack_flat(g_small, small_rows), pk(m), pk(v), "adamw_small")
    shapes = [w[n].shape for n in small]
    for n, g, d_, m_, v_ in zip(small, g_small, *(_unpack_flat(r, shapes) for r in res)):
        grads[n], delta[n], new_m[n], new_v[n] = g, d_, m_, v_

    return (loss, dx[None], *[grads[n] for n in WEIGHTS], *[delta[n] for n in WEIGHTS],
            *[new_m[n] for n in WEIGHTS], *[new_v[n] for n in WEIGHTS])


def kernel(x, norm1_g, w_in, sgu_norm_g, sgu_w, sgu_b, s5_lambda_re, s5_lambda_im, s5_log_dt, s5_b_re, s5_b_im, s5_c_re, s5_c_im, s5_d, s5_glu_w, s5_glu_b, lru_conv_w, lru_conv_b, lru_wa, lru_ba, lru_wx, lru_bx, lru_lambda, fox_fgate_b, mix_norm_g, w_out, norm2_g, w_mlp_in, w_mlp_out, final_g, loss_target, m_norm1_g, m_w_in, m_sgu_norm_g, m_sgu_w, m_sgu_b, m_s5_lambda_re, m_s5_lambda_im, m_s5_log_dt, m_s5_b_re, m_s5_b_im, m_s5_c_re, m_s5_c_im, m_s5_d, m_s5_glu_w, m_s5_glu_b, m_lru_conv_w, m_lru_conv_b, m_lru_wa, m_lru_ba, m_lru_wx, m_lru_bx, m_lru_lambda, m_fox_fgate_b, m_mix_norm_g, m_w_out, m_norm2_g, m_w_mlp_in, m_w_mlp_out, m_final_g, v_norm1_g, v_w_in, v_sgu_norm_g, v_sgu_w, v_sgu_b, v_s5_lambda_re, v_s5_lambda_im, v_s5_log_dt, v_s5_b_re, v_s5_b_im, v_s5_c_re, v_s5_c_im, v_s5_d, v_s5_glu_w, v_s5_glu_b, v_lru_conv_w, v_lru_conv_b, v_lru_wa, v_lru_ba, v_lru_wx, v_lru_bx, v_lru_lambda, v_fox_fgate_b, v_mix_norm_g, v_w_out, v_norm2_g, v_w_mlp_in, v_w_mlp_out, v_final_g):
    return _step(x, norm1_g, w_in, sgu_norm_g, sgu_w, sgu_b, s5_lambda_re, s5_lambda_im, s5_log_dt, s5_b_re, s5_b_im, s5_c_re, s5_c_im, s5_d, s5_glu_w, s5_glu_b, lru_conv_w, lru_conv_b, lru_wa, lru_ba, lru_wx, lru_bx, lru_lambda, fox_fgate_b, mix_norm_g, w_out, norm2_g, w_mlp_in, w_mlp_out, final_g, loss_target, m_norm1_g, m_w_in, m_sgu_norm_g, m_sgu_w, m_sgu_b, m_s5_lambda_re, m_s5_lambda_im, m_s5_log_dt, m_s5_b_re, m_s5_b_im, m_s5_c_re, m_s5_c_im, m_s5_d, m_s5_glu_w, m_s5_glu_b, m_lru_conv_w, m_lru_conv_b, m_lru_wa, m_lru_ba, m_lru_wx, m_lru_bx, m_lru_lambda, m_fox_fgate_b, m_mix_norm_g, m_w_out, m_norm2_g, m_w_mlp_in, m_w_mlp_out, m_final_g, v_norm1_g, v_w_in, v_sgu_norm_g, v_sgu_w, v_sgu_b, v_s5_lambda_re, v_s5_lambda_im, v_s5_log_dt, v_s5_b_re, v_s5_b_im, v_s5_c_re, v_s5_c_im, v_s5_d, v_s5_glu_w, v_s5_glu_b, v_lru_conv_w, v_lru_conv_b, v_lru_wa, v_lru_ba, v_lru_wx, v_lru_bx, v_lru_lambda, v_fox_fgate_b, v_mix_norm_g, v_w_out, v_norm2_g, v_w_mlp_in, v_w_mlp_out, v_final_g)
```

```python
import functools
import math

import jax
import jax.numpy as jnp
from jax import lax
from jax.experimental import pallas as pl
from jax.experimental.pallas import tpu as pltpu

F32 = jnp.float32
BF16 = jnp.bfloat16

DEPTH = 4
D_MODEL = 1024
MIXER_WIDTH = 256
SGU_CHUNK = 128
N_HEADS = 4
HEAD_DIM = 64
S5_GROUPS = 16
S5_GROUP = 16
S5_STATE = 64
LRU_C = 8.0
RMS_EPS = 1e-6
D_IN_PROJ = 8 * MIXER_WIDTH + N_HEADS
D_IN_PAD = 8 * MIXER_WIDTH + 128
ADAM_LR, ADAM_B1, ADAM_B2, ADAM_EPS, ADAM_WD, ADAM_STEP = 0.001, 0.9, 0.999, 1e-08, 0.01, 10

V7X_VMEM_BYTES = 64 * 1024 * 1024
VMEM_LIMIT = V7X_VMEM_BYTES - 8 * 1024 * 1024
NEG = -1e30
MESH = pl.DeviceIdType.MESH


def _cp(sem=None, **kw):
    return pltpu.CompilerParams(dimension_semantics=sem, vmem_limit_bytes=VMEM_LIMIT, **kw)


def _full_spec(a):
    nd = a.ndim
    return pl.BlockSpec(a.shape, lambda *_: (0,) * nd)


def _tile(n, pref=512):
    return pref if n % pref == 0 else n


def _dot(a, b, ca, cb):
    return lax.dot_general(a.astype(BF16), b.astype(BF16), (((ca,), (cb,)), ((), ())),
                           preferred_element_type=F32)


def _mm_call(a, b, *, ta=False, tb=False, res=None, name):
    m, k = (a.shape[1], a.shape[0]) if ta else a.shape
    n = b.shape[0] if tb else b.shape[1]
    tm, tn, tk = _tile(m), _tile(n), _tile(k)
    if tn > 1024 or tk > 1024:
        tm = _tile(m, 256)
    nk = k // tk
    ca, cb = (0 if ta else 1), (1 if tb else 0)

    def body(*refs):
        a_ref, b_ref = refs[0], refs[1]
        o_ref, acc = refs[-2], refs[-1]
        kk = pl.program_id(2)

        @pl.when(kk == 0)
        def _():
            acc[...] = jnp.zeros_like(acc)

        acc[...] += _dot(a_ref[...], b_ref[...], ca, cb)

        @pl.when(kk == nk - 1)
        def _():
            out = acc[...]
            if res is not None:
                out = out + refs[2][...]
            o_ref[...] = out

    a_spec = pl.BlockSpec((tk, tm), lambda i, j, kk: (kk, i)) if ta else pl.BlockSpec((tm, tk), lambda i, j, kk: (i, kk))
    b_spec = pl.BlockSpec((tn, tk), lambda i, j, kk: (j, kk)) if tb else pl.BlockSpec((tk, tn), lambda i, j, kk: (kk, j))
    o_spec = pl.BlockSpec((tm, tn), lambda i, j, kk: (i, j))
    ins, specs = [a, b], [a_spec, b_spec]
    if res is not None:
        ins.append(res)
        specs.append(o_spec)
    return pl.pallas_call(
        body, name=name, grid=(m // tm, n // tn, nk), in_specs=specs, out_specs=o_spec,
        out_shape=jax.ShapeDtypeStruct((m, n), F32), scratch_shapes=[pltpu.VMEM((tm, tn), F32)],
        compiler_params=_cp(("parallel", "parallel", "arbitrary")))(*ins)


@jax.custom_vjp
def mm(a, b):
    return _mm_call(a, b, name="mm")


def _mm_fwd(a, b):
    return _mm_call(a, b, name="mm"), (a, b)


def _mm_bwd(r, g):
    a, b = r
    return _mm_call(g, b, tb=True, name="mm_da"), _mm_call(a, g, ta=True, name="mm_db")


mm.defvjp(_mm_fwd, _mm_bwd)


def _matmul(name, grid, a, a_spec, b, b_spec, dims, outs, *, extras=(), epilogue=None, into=None):
    nk = grid[2]
    n_ex, n_out = len(extras), len(outs)
    tm_tn = tuple(d for d in outs[0][1].block_shape if d is not None)[-2:]

    def body(*refs):
        a_ref, b_ref = refs[0], refs[1]
        ex_refs = refs[2:2 + n_ex]
        o_refs = refs[len(refs) - n_out - (nk > 1):len(refs) - (nk > 1)]

        def finish(val):
            res = epilogue(val, *[e[...] for e in ex_refs]) if epilogue else (val,)
            for o_ref, r in zip(o_refs, res):
                o_ref[...] = r.astype(o_ref.dtype)

        if nk == 1:
            finish(_dot(a_ref[...], b_ref[...], *dims))
        else:
            acc = refs[-1]
            kk = pl.program_id(2)

            @pl.when(kk == 0)
            def _():
                acc[...] = jnp.zeros_like(acc)

            acc[...] += _dot(a_ref[...], b_ref[...], *dims)

            @pl.when(kk == nk - 1)
            def _():
                finish(acc[...])

    ins = [a, b] + [e[0] for e in extras]
    specs = [a_spec, b_spec] + [e[1] for e in extras]
    aliases = {}
    if into is not None:
        aliases = {len(ins): 0}
        ins.append(into)
        specs.append(pl.BlockSpec(memory_space=pl.ANY))
    res = pl.pallas_call(
        body, name=name, grid=grid, in_specs=specs, out_specs=[o[1] for o in outs], out_shape=[o[0] for o in outs],
        scratch_shapes=[pltpu.VMEM(tm_tn, F32)] if nk > 1 else [], input_output_aliases=aliases,
        compiler_params=_cp(("arbitrary", "arbitrary", "arbitrary")))(*ins)
    return res[0] if n_out == 1 else res


@jax.custom_vjp
def _bdot(a, b):
    return _dot(a, b, 1, 0)


def _bdot_fwd(a, b):
    return _dot(a, b, 1, 0), (a, b)


def _bdot_bwd(r, g):
    a, b = r
    return _dot(g, b, 1, 1), _dot(a, g, 0, 0)


_bdot.defvjp(_bdot_fwd, _bdot_bwd)


def _row_spec(tr, w):
    return pl.BlockSpec((tr, w), lambda i: (i, 0))


def _rowwise(fn, rows, pars, outs, *, name, tr, dtype=F32):
    t = rows[0].shape[0]
    n_in = len(rows) + len(pars)

    def body(*refs):
        res = fn(*[r[...] for r in refs[:n_in]])
        for o_ref, v in zip(refs[n_in:], res):
            o_ref[...] = v.astype(o_ref.dtype)

    return pl.pallas_call(
        body, name=name, grid=(t // tr,),
        in_specs=[_row_spec(tr, r.shape[1]) for r in rows] + [_full_spec(p) for p in pars],
        out_specs=[_row_spec(tr, w) for w in outs],
        out_shape=[jax.ShapeDtypeStruct((t, w), dtype) for w in outs],
        compiler_params=_cp(("arbitrary",)))(*rows, *pars)


def _rowwise_vjp(fn, rows, pars, cots, *, name, tr, add=None):
    t = rows[0].shape[0]
    nr, npar = len(rows), len(pars)
    cots = list(cots) + ([add] if add is not None else [])
    nc = len(cots)

    def body(*refs):
        vals = [r[...] for r in refs[:nr + npar]]
        cts = [c[...] for c in refs[nr + npar:nr + npar + nc]]
        douts = refs[nr + npar + nc:]
        extra = cts.pop() if add is not None else None
        _, vjp = jax.vjp(fn, *vals)
        grads = list(vjp(tuple(cts)))
        if extra is not None:
            grads[0] = grads[0] + extra
        for kk in range(nr):
            douts[kk][...] = grads[kk]

        @pl.when(pl.program_id(0) == 0)
        def _():
            for kk in range(npar):
                douts[nr + kk][...] = jnp.zeros_like(douts[nr + kk])

        for kk in range(npar):
            douts[nr + kk][...] += grads[nr + kk]

    return pl.pallas_call(
        body, name=name, grid=(t // tr,),
        in_specs=[_row_spec(tr, r.shape[1]) for r in rows] + [_full_spec(p) for p in pars]
        + [_row_spec(tr, c.shape[1]) for c in cots],
        out_specs=[_row_spec(tr, r.shape[1]) for r in rows] + [_full_spec(p) for p in pars],
        out_shape=[jax.ShapeDtypeStruct(r.shape, F32) for r in rows]
        + [jax.ShapeDtypeStruct(p.shape, F32) for p in pars],
        compiler_params=_cp(("arbitrary",)))(*rows, *pars, *cots)


def _make_rw(fn, name, tr, nr, outs):
    @jax.custom_vjp
    def f(*args):
        return tuple(_rowwise(fn, args[:nr], args[nr:], outs, name=name + "_f", tr=tr))

    def fwd(*args):
        return f(*args), args

    def bwd(args, cts):
        return tuple(_rowwise_vjp(fn, args[:nr], args[nr:], list(cts), name=name + "_b", tr=tr))

    f.defvjp(fwd, bwd)
    return f


def _rms(x, g):
    return x * lax.rsqrt(jnp.mean(jnp.square(x), axis=-1, keepdims=True) + RMS_EPS) * g


def _f_rms(x, g):
    return (_rms(x, g),)


def _f_sgu(au, av, ng, w0, w1, w2, w3, bfull):
    u = jax.nn.gelu(au)
    v = _rms(jax.nn.gelu(av), ng)
    tri = lax.broadcasted_iota(jnp.int32, (SGU_CHUNK, SGU_CHUNK), 0) >= lax.broadcasted_iota(
        jnp.int32, (SGU_CHUNK, SGU_CHUNK), 1)
    head = lax.broadcasted_iota(jnp.int32, v.shape, 1) // HEAD_DIM
    mixed = bfull
    for h, w in enumerate((w0, w1, w2, w3)):
        mixed = mixed + _bdot(jnp.where(tri, w, 0.0), jnp.where(head == h, v, 0.0))
    return (u * mixed,)


def _f_s5disc(lam_re, lam_im, log_dt, b_re, b_im):
    dt = jnp.exp(log_dt)
    mag = jnp.exp(lam_re * dt)
    abar_re = mag * jnp.cos(lam_im * dt)
    abar_im = mag * jnp.sin(lam_im * dt)
    denom = jnp.square(lam_re) + jnp.square(lam_im)
    num_re = abar_re - 1.0
    num_im = abar_im
    fac_re = (num_re * lam_re + num_im * lam_im) / denom
    fac_im = (num_im * lam_re - num_re * lam_im) / denom
    return abar_re, abar_im, fac_re * b_re - fac_im * b_im, fac_re * b_im + fac_im * b_re


def _f_s5post(s_re, s_im, u, c_re, c_im, d, gw, gb):
    y = _bdot(s_re, c_re) - _bdot(s_im, c_im) + d * u
    y = jax.nn.gelu(y)
    return (y * jax.nn.sigmoid(_bdot(y, gw) + gb),)


def _f_lrupre(xc, wa, ba, wx, bx, lam):
    r = jax.nn.sigmoid(_bdot(xc, wa) + ba)
    i = jax.nn.sigmoid(_bdot(xc, wx) + bx)
    log_a = -LRU_C * r * jax.nn.softplus(-lam)
    a = jnp.exp(log_a)
    one_minus_a2 = -jnp.tanh(log_a) * (jnp.exp(2.0 * log_a) + 1.0)
    return a, jnp.sqrt(one_minus_a2) * (i * xc)


def _f_lrupost(h, gate):
    return (h * jax.nn.gelu(gate),)


def _f_logsig(zf, bf):
    return (jax.nn.log_sigmoid(zf + bf),)


def _f_gnorm(ya, yb, yc, yd, g):
    def n(y):
        return y * lax.rsqrt(jnp.mean(jnp.square(y), axis=-1, keepdims=True) + RMS_EPS)
    return (jnp.concatenate([n(ya), n(yb), n(yc), n(yd)], axis=1) * g,)


sgu_mix = _make_rw(_f_sgu, "sgu", SGU_CHUNK, 2, [MIXER_WIDTH])
s5_disc = _make_rw(_f_s5disc, "s5disc", S5_GROUPS * S5_GROUP, 5, [S5_STATE] * 4)
s5_post = _make_rw(_f_s5post, "s5post", 256, 3, [MIXER_WIDTH])
lru_pre = _make_rw(_f_lrupre, "lrupre", 512, 1, [MIXER_WIDTH, MIXER_WIDTH])
lru_post = _make_rw(_f_lrupost, "lrupost", 512, 2, [MIXER_WIDTH])
log_sig = _make_rw(_f_logsig, "logsig", 512, 1, [128])


SCAN_TILE = 512


def _prev_spec(c, nt, rev):
    per = SCAN_TILE // 8
    if rev:
        return pl.BlockSpec((8, c), lambda i: (jnp.maximum((nt - 1 - i) * per - 1, 0), 0))
    return pl.BlockSpec((8, c), lambda i: (jnp.maximum(i * per - 1, 0), 0))


SCAN_STEPS = (1, 2, 4)


def _cmul(ar, ai, br, bi):
    return ar * br - ai * bi, ar * bi + ai * br


def _rows_down(x, k, fill, rowid):
    return jnp.where(rowid >= k, pltpu.roll(x, k, 0), fill)


def _rows_up(x, k, fill, rowid):
    return jnp.where(rowid < 8 - k, pltpu.roll(x, 8 - k, 0), fill)


def _powers(ar, ai):
    pw = [(ar, ai)]
    for _ in range(7):
        pw.append(_cmul(*pw[-1], ar, ai))
    return pw


def _block(i):
    return pl.ds(pl.multiple_of(i * 8, 8), 8)


def _row_before(ref, i, edge):
    return jnp.where(i == 0, edge, ref[pl.ds(jnp.maximum(i * 8 - 1, 0), 1), :])


def _lti_fwd_call(a_re, a_im, b_re, b_im):
    t, c = b_re.shape
    tt = SCAN_TILE

    def body(ar_ref, ai_ref, br_ref, bi_ref, sr_ref, si_ref, cr, ci):
        @pl.when(pl.program_id(0) == 0)
        def _():
            cr[...] = jnp.zeros_like(cr)
            ci[...] = jnp.zeros_like(ci)

        pw = _powers(ar_ref[...], ai_ref[...])
        apr = jnp.concatenate([p[0] for p in pw], axis=0)
        api = jnp.concatenate([p[1] for p in pw], axis=0)
        rowid = lax.broadcasted_iota(jnp.int32, (8, c), 0)

        def block(i, carry):
            xr, xi = br_ref[_block(i), :], bi_ref[_block(i), :]
            for k in SCAN_STEPS:
                dr, di = _cmul(*pw[k - 1], _rows_down(xr, k, 0.0, rowid), _rows_down(xi, k, 0.0, rowid))
                xr, xi = xr + dr, xi + di
            dr, di = _cmul(apr, api, *carry)
            xr, xi = xr + dr, xi + di
            sr_ref[_block(i), :] = xr
            si_ref[_block(i), :] = xi
            return xr[7:8, :], xi[7:8, :]

        hr, hi = lax.fori_loop(0, tt // 8, block, (cr[...], ci[...]), unroll=2)
        cr[...] = hr
        ci[...] = hi

    row = pl.BlockSpec((tt, c), lambda i: (i, 0))
    par = pl.BlockSpec((1, c), lambda i: (0, 0))
    return pl.pallas_call(
        body, name="lti_scan_f", grid=(t // tt,), in_specs=[par, par, row, row], out_specs=[row, row],
        out_shape=[jax.ShapeDtypeStruct((t, c), F32)] * 2,
        scratch_shapes=[pltpu.VMEM((1, c), F32)] * 2, compiler_params=_cp(("arbitrary",)))(a_re, a_im, b_re, b_im)


def _lti_bwd_call(a_re, a_im, s_re, s_im, g_re, g_im):
    t, c = g_re.shape
    tt = SCAN_TILE
    nt = t // tt
    nb = tt // 8

    def body(ar_ref, ai_ref, sr_ref, si_ref, pr_ref, pi_ref, gr_ref, gi_ref,
             or_ref, oi_ref, dar_ref, dai_ref, cr, ci):
        ti = pl.program_id(0)

        @pl.when(ti == 0)
        def _():
            cr[...] = jnp.zeros_like(cr)
            ci[...] = jnp.zeros_like(ci)
            dar_ref[...] = jnp.zeros_like(dar_ref)
            dai_ref[...] = jnp.zeros_like(dai_ref)

        pw = _powers(ar_ref[...], -ai_ref[...])
        tpr = jnp.concatenate([p[0] for p in reversed(pw)], axis=0)
        tpi = jnp.concatenate([p[1] for p in reversed(pw)], axis=0)
        rowid = lax.broadcasted_iota(jnp.int32, (8, c), 0)
        first = ti == nt - 1
        edge_r = jnp.where(first, 0.0, pr_ref[7:8, :])
        edge_i = jnp.where(first, 0.0, pi_ref[7:8, :])

        def block(kk, carry):
            i = nb - 1 - kk
            gr_c, gi_c, acc_r, acc_i = carry
            xr, xi = gr_ref[_block(i), :], gi_ref[_block(i), :]
            for k in SCAN_STEPS:
                dr, di = _cmul(*pw[k - 1], _rows_up(xr, k, 0.0, rowid), _rows_up(xi, k, 0.0, rowid))
                xr, xi = xr + dr, xi + di
            dr, di = _cmul(tpr, tpi, gr_c, gi_c)
            xr, xi = xr + dr, xi + di
            or_ref[_block(i), :] = xr
            oi_ref[_block(i), :] = xi
            spr = _rows_down(sr_ref[_block(i), :], 1, _row_before(sr_ref, i, edge_r), rowid)
            spi = _rows_down(si_ref[_block(i), :], 1, _row_before(si_ref, i, edge_i), rowid)
            return xr[0:1, :], xi[0:1, :], acc_r + spr * xr + spi * xi, acc_i + spr * xi - spi * xr

        zero = jnp.zeros((8, c), F32)
        gr_c, gi_c, acc_r, acc_i = lax.fori_loop(0, nb, block, (cr[...], ci[...], zero, zero), unroll=2)
        cr[...] = gr_c
        ci[...] = gi_c
        dar_ref[...] += jnp.sum(acc_r, axis=0, keepdims=True)
        dai_ref[...] += jnp.sum(acc_i, axis=0, keepdims=True)

    row = pl.BlockSpec((tt, c), lambda i: (nt - 1 - i, 0))
    par = pl.BlockSpec((1, c), lambda i: (0, 0))
    prev = _prev_spec(c, nt, True)
    return pl.pallas_call(
        body, name="lti_scan_b", grid=(nt,), in_specs=[par, par, row, row, prev, prev, row, row],
        out_specs=[row, row, par, par],
        out_shape=[jax.ShapeDtypeStruct((t, c), F32)] * 2 + [jax.ShapeDtypeStruct((1, c), F32)] * 2,
        scratch_shapes=[pltpu.VMEM((1, c), F32)] * 2,
        compiler_params=_cp(("arbitrary",)))(a_re, a_im, s_re, s_im, s_re, s_im, g_re, g_im)


@jax.custom_vjp
def lti_scan(a_re, a_im, b_re, b_im):
    return tuple(_lti_fwd_call(a_re, a_im, b_re, b_im))


def _lti_scan_fwd(a_re, a_im, b_re, b_im):
    s_re, s_im = _lti_fwd_call(a_re, a_im, b_re, b_im)
    return (s_re, s_im), (a_re, a_im, s_re, s_im)


def _lti_scan_bwd(r, g):
    a_re, a_im, s_re, s_im = r
    o_re, o_im, da_re, da_im = _lti_bwd_call(a_re, a_im, s_re, s_im, g[0], g[1])
    return da_re, da_im, o_re, o_im


lti_scan.defvjp(_lti_scan_fwd, _lti_scan_bwd)


def _tv_fwd_call(a, b):
    t, c = b.shape
    tt = SCAN_TILE

    def body(a_ref, b_ref, h_ref, ch):
        @pl.when(pl.program_id(0) == 0)
        def _():
            ch[...] = jnp.zeros_like(ch)

        rowid = lax.broadcasted_iota(jnp.int32, (8, c), 0)

        def block(i, h):
            ab, x = a_ref[_block(i), :], b_ref[_block(i), :]
            for k in SCAN_STEPS:
                x = x + ab * _rows_down(x, k, 0.0, rowid)
                ab = ab * _rows_down(ab, k, 1.0, rowid)
            x = x + ab * h
            h_ref[_block(i), :] = x
            return x[7:8, :]

        ch[...] = lax.fori_loop(0, tt // 8, block, ch[...], unroll=2)

    row = pl.BlockSpec((tt, c), lambda i: (i, 0))
    return pl.pallas_call(
        body, name="tv_scan_f", grid=(t // tt,), in_specs=[row, row], out_specs=row,
        out_shape=jax.ShapeDtypeStruct((t, c), F32), scratch_shapes=[pltpu.VMEM((1, c), F32)],
        compiler_params=_cp(("arbitrary",)))(a, b)


def _tv_bwd_call(a, h, g):
    t, c = g.shape
    tt = SCAN_TILE
    nt = t // tt
    nb = tt // 8

    def body(a_ref, h_ref, p_ref, g_ref, da_ref, db_ref, cg, ca):
        ti = pl.program_id(0)

        @pl.when(ti == 0)
        def _():
            cg[...] = jnp.zeros_like(cg)
            ca[...] = jnp.zeros_like(ca)

        rowid = lax.broadcasted_iota(jnp.int32, (8, c), 0)
        edge = jnp.where(ti == nt - 1, 0.0, p_ref[7:8, :])

        def block(kk, carry):
            i = nb - 1 - kk
            gc, a_next = carry
            ab, x = a_ref[_block(i), :], g_ref[_block(i), :]
            cb = _rows_up(ab, 1, a_next, rowid)
            for k in SCAN_STEPS:
                x = x + cb * _rows_up(x, k, 0.0, rowid)
                cb = cb * _rows_up(cb, k, 1.0, rowid)
            x = x + cb * gc
            db_ref[_block(i), :] = x
            da_ref[_block(i), :] = x * _rows_down(h_ref[_block(i), :], 1, _row_before(h_ref, i, edge), rowid)
            return x[0:1, :], ab[0:1, :]

        gc, a_next = lax.fori_loop(0, nb, block, (cg[...], ca[...]), unroll=2)
        cg[...] = gc
        ca[...] = a_next

    row = pl.BlockSpec((tt, c), lambda i: (nt - 1 - i, 0))
    return pl.pallas_call(
        body, name="tv_scan_b", grid=(nt,), in_specs=[row, row, _prev_spec(c, nt, True), row],
        out_specs=[row, row], out_shape=[jax.ShapeDtypeStruct((t, c), F32)] * 2,
        scratch_shapes=[pltpu.VMEM((1, c), F32)] * 2, compiler_params=_cp(("arbitrary",)))(a, h, h, g)


@jax.custom_vjp
def tv_scan(a, b):
    return _tv_fwd_call(a, b)


def _tv_scan_fwd(a, b):
    h = _tv_fwd_call(a, b)
    return h, (a, h)


def _tv_scan_bwd(r, g):
    a, h = r
    return tuple(_tv_bwd_call(a, h, g))


tv_scan.defvjp(_tv_scan_fwd, _tv_scan_bwd)


CONV_K = 4
CONV_ROWS = 512


def _conv_fwd_call(x, w, b):
    t, c = x.shape

    def body(x_ref, w_ref, b_ref, o_ref, xp):
        xp[0:8, :] = jnp.zeros((8, c), F32)
        xp[8:, :] = x_ref[...]
        for blk in range(t // CONV_ROWS):
            base = blk * CONV_ROWS
            acc = jnp.broadcast_to(b_ref[...], (CONV_ROWS, c))
            for kk in range(CONV_K):
                acc = acc + w_ref[kk:kk + 1, :] * xp[base + 5 + kk:base + 5 + kk + CONV_ROWS, :]
            o_ref[base:base + CONV_ROWS, :] = acc

    return pl.pallas_call(
        body, name="conv_f", out_shape=jax.ShapeDtypeStruct((t, c), F32),
        scratch_shapes=[pltpu.VMEM((t + 8, c), F32)], compiler_params=_cp())(x, w, b)


def _conv_bwd_call(x, w, g):
    t, c = x.shape

    def body(x_ref, w_ref, g_ref, dx_ref, dw_ref, db_ref, xp, gp):
        xp[0:8, :] = jnp.zeros((8, c), F32)
        xp[8:, :] = x_ref[...]
        gp[0:t, :] = g_ref[...]
        gp[t:, :] = jnp.zeros((8, c), F32)
        dw = [jnp.zeros((1, c), F32) for _ in range(CONV_K)]
        db = jnp.zeros((1, c), F32)
        for blk in range(t // CONV_ROWS):
            base = blk * CONV_ROWS
            gb = g_ref[base:base + CONV_ROWS, :]
            acc = jnp.zeros((CONV_ROWS, c), F32)
            for kk in range(CONV_K):
                acc = acc + w_ref[kk:kk + 1, :] * gp[base + 3 - kk:base + 3 - kk + CONV_ROWS, :]
                dw[kk] = dw[kk] + jnp.sum(gb * xp[base + 5 + kk:base + 5 + kk + CONV_ROWS, :], axis=0, keepdims=True)
            db = db + jnp.sum(gb, axis=0, keepdims=True)
            dx_ref[base:base + CONV_ROWS, :] = acc
        for kk in range(CONV_K):
            dw_ref[kk:kk + 1, :] = dw[kk]
        db_ref[...] = db

    return pl.pallas_call(
        body, name="conv_b",
        out_shape=[jax.ShapeDtypeStruct((t, c), F32), jax.ShapeDtypeStruct((CONV_K, c), F32),
                   jax.ShapeDtypeStruct((1, c), F32)],
        scratch_shapes=[pltpu.VMEM((t + 8, c), F32)] * 2, compiler_params=_cp())(x, w, g)


@jax.custom_vjp
def causal_conv(x, w, b):
    return _conv_fwd_call(x, w, b)


def _causal_conv_fwd(x, w, b):
    return _conv_fwd_call(x, w, b), (x, w)


def _causal_conv_bwd(r, g):
    return tuple(_conv_bwd_call(r[0], r[1], g))


causal_conv.defvjp(_causal_conv_fwd, _causal_conv_bwd)


ATT_TILE = 512
ATT_SCALE = HEAD_DIM ** -0.5


def _head_lane(val, lane, h):
    return jnp.sum(jnp.where(lane == h, val, 0.0), axis=1, keepdims=True)


def _attn_fwd_call(q, k, v, c128, cr, next_shard=None):
    t, w = q.shape
    tq = ATT_TILE
    nq = t // tq
    k3, v3, cr4 = k.reshape(nq, tq, w), v.reshape(nq, tq, w), cr.reshape(N_HEADS, nq, 1, tq)
    fused = next_shard is not None

    def body(*refs):
        q_ref, k_ref, v_ref, c_ref, cr_ref = refs[:5]
        i, h = pl.program_id(0), pl.program_id(1)
        if fused:
            shard_ref, o_ref, lse_ref, gathered_ref, send_sems, recv_sems = refs[5:]
            start, finish = _allgather_copies(shard_ref, gathered_ref, send_sems, recv_sems)
            pl.when((i == 0) & (h == 0))(start)
        else:
            o_ref, lse_ref = refs[5:]
        hm = lax.broadcasted_iota(jnp.int32, (tq, w), 1) // HEAD_DIM == h
        lane = lax.broadcasted_iota(jnp.int32, (tq, 128), 1)
        qs = jnp.where(hm, q_ref[...] * ATT_SCALE, 0.0)
        cq = _head_lane(c_ref[...], lane, h)
        causal = lax.broadcasted_iota(jnp.int32, (tq, tq), 0) >= lax.broadcasted_iota(jnp.int32, (tq, tq), 1)

        def update(j, carry, diagonal):
            m, l, acc = carry
            s = _dot(qs, k_ref[j], 1, 1) + cq - cr_ref[0, j]
            if diagonal:
                s = jnp.where(causal, s, NEG)
            m_new = jnp.maximum(m, jnp.max(s, axis=1, keepdims=True))
            p = jnp.exp(s - m_new)
            alpha = jnp.exp(m - m_new)
            return m_new, alpha * l + jnp.sum(p, axis=1, keepdims=True), alpha * acc + _dot(p, v_ref[j], 1, 0)

        init = (jnp.full((tq, 1), NEG, F32), jnp.zeros((tq, 1), F32), jnp.zeros((tq, w), F32))
        carry = lax.fori_loop(0, i, lambda j, c: update(j, c, False), init)
        m, l, acc = update(i, carry, True)
        out = jnp.where(hm, acc / l, 0.0)
        lse = jnp.where(lane == h, m + jnp.log(l), 0.0)

        @pl.when(h == 0)
        def _():
            o_ref[...] = out
            lse_ref[...] = lse

        @pl.when(h > 0)
        def _():
            o_ref[...] += out
            lse_ref[...] += lse

        if fused:
            pl.when((i == nq - 1) & (h == N_HEADS - 1))(finish)

    tile = pl.BlockSpec((tq, w), lambda i, h: (i, 0))
    tile_c = pl.BlockSpec((tq, 128), lambda i, h: (i, 0))
    whole = pl.BlockSpec((nq, tq, w), lambda i, h: (0, 0, 0))
    rows = pl.BlockSpec((1, nq, 1, tq), lambda i, h: (h, 0, 0, 0))
    ins = [q.astype(BF16), k3.astype(BF16), v3.astype(BF16), c128, cr4]
    in_specs, out_specs = [tile, whole, whole, tile_c, rows], [tile, tile_c]
    out_shape = [jax.ShapeDtypeStruct((t, w), F32), jax.ShapeDtypeStruct((t, 128), F32)]
    scratch = []
    if fused:
        ins.append(next_shard)
        in_specs.append(HBM)
        out_specs.append(HBM)
        out_shape.append(jax.ShapeDtypeStruct((4,) + next_shard.shape, next_shard.dtype))
        scratch = [pltpu.SemaphoreType.DMA((AG_SEMS,)), pltpu.SemaphoreType.DMA((AG_SEMS,))]
    return pl.pallas_call(
        body, name="attn_f_allgather" if fused else "attn_f", grid=(nq, N_HEADS), in_specs=in_specs,
        out_specs=out_specs, out_shape=out_shape, scratch_shapes=scratch,
        compiler_params=_cp(("arbitrary", "arbitrary")))(*ins)


def _attn_bwd_call(q, k, v, c128, cr, o, lse, do, exchange=None):
    t, w = q.shape
    tq = ATT_TILE
    nq = t // tq
    r3 = lambda a: a.reshape(nq, tq, a.shape[-1])
    cr4 = cr.reshape(N_HEADS, nq, 1, tq)
    fused = exchange is not None

    def body(*refs):
        q_ref, k_ref, v_ref, c_ref, cr_ref, o_ref, lse_ref, do_ref = refs[:8]
        j, h = pl.program_id(0), pl.program_id(1)
        if fused:
            p_ref, dq_ref, dk_ref, dv_ref, dc_ref, dcr_ref, recv_ref, send_sems, recv_sems = refs[8:]
            start, finish = _chip_exchange_copies(p_ref, recv_ref, send_sems, recv_sems)
            pl.when((j == 0) & (h == 0))(start)
        else:
            dq_ref, dk_ref, dv_ref, dc_ref, dcr_ref = refs[8:]

        @pl.when((j == 0) & (h == 0))
        def _():
            dq_ref[...] = jnp.zeros_like(dq_ref)
            dc_ref[...] = jnp.zeros_like(dc_ref)

        hm = lax.broadcasted_iota(jnp.int32, (tq, w), 1) // HEAD_DIM == h
        lane = lax.broadcasted_iota(jnp.int32, (tq, 128), 1)
        kj = k_ref[...]
        vj = v_ref[...]
        ck = cr_ref[0, 0]
        causal = lax.broadcasted_iota(jnp.int32, (tq, tq), 0) >= lax.broadcasted_iota(jnp.int32, (tq, tq), 1)

        def step(i, carry, diagonal):
            dk, dv, dck = carry
            qm = jnp.where(hm, q_ref[i], 0.0)
            dom = jnp.where(hm, do_ref[i], 0.0)
            s = _dot(qm * ATT_SCALE, kj, 1, 1) + _head_lane(c_ref[i], lane, h) - ck
            if diagonal:
                s = jnp.where(causal, s, NEG)
            p = jnp.exp(s - _head_lane(lse_ref[i], lane, h))
            dv = dv + _dot(p, dom, 0, 0)
            dp = _dot(dom, vj, 1, 1)
            delta = jnp.sum(dom * o_ref[i], axis=1, keepdims=True)
            ds = p * (dp - delta)
            dq_ref[i] += jnp.where(hm, _dot(ds, kj, 1, 0), 0.0) * ATT_SCALE
            dk = dk + _dot(ds, qm, 0, 0) * ATT_SCALE
            dc_ref[i] += jnp.where(lane == h, jnp.sum(ds, axis=1, keepdims=True), 0.0)
            return dk, dv, dck - jnp.sum(ds, axis=0, keepdims=True)

        init = (jnp.zeros((tq, w), F32), jnp.zeros((tq, w), F32), jnp.zeros((1, tq), F32))
        carry = step(j, init, True)
        dk, dv, dck = lax.fori_loop(j + 1, nq, lambda i, c: step(i, c, False), carry)
        dcr_ref[0, 0] = dck

        @pl.when(h == 0)
        def _():
            dk_ref[...] = dk
            dv_ref[...] = dv

        @pl.when(h > 0)
        def _():
            dk_ref[...] += dk
            dv_ref[...] += dv

        if fused:
            pl.when((j == nq - 1) & (h == N_HEADS - 1))(finish)

    whole = pl.BlockSpec((nq, tq, w), lambda j, h: (0, 0, 0))
    whole_c = pl.BlockSpec((nq, tq, 128), lambda j, h: (0, 0, 0))
    tile = pl.BlockSpec((None, tq, w), lambda j, h: (j, 0, 0))
    tile_r = pl.BlockSpec((1, 1, 1, tq), lambda j, h: (h, j, 0, 0))
    s3 = jax.ShapeDtypeStruct((nq, tq, w), F32)
    b16 = lambda a: r3(a).astype(BF16)
    ins = [b16(q), b16(k), b16(v), r3(c128), cr4, r3(o), r3(lse), r3(do)]
    in_specs = [whole, tile, tile, whole_c, tile_r, whole, whole_c, whole]
    out_specs = [whole, tile, tile, whole_c, tile_r]
    out_shape = [s3, s3, s3, jax.ShapeDtypeStruct((nq, tq, 128), F32), jax.ShapeDtypeStruct((N_HEADS, nq, 1, tq), F32)]
    scratch = []
    if fused:
        ins.append(exchange)
        in_specs.append(HBM)
        out_specs.append(HBM)
        out_shape.append(jax.ShapeDtypeStruct((3,) + exchange.shape[1:], exchange.dtype))
        scratch = [pltpu.SemaphoreType.DMA((3,)), pltpu.SemaphoreType.DMA((3,))]
    dq, dk, dv, dc, dcr, *received = pl.pallas_call(
        body, name="attn_b_exchange" if fused else "attn_b", grid=(nq, N_HEADS), in_specs=in_specs,
        out_specs=out_specs, out_shape=out_shape, scratch_shapes=scratch,
        compiler_params=_cp(("arbitrary", "arbitrary")))(*ins)
    grads = (dq.reshape(t, w), dk.reshape(t, w), dv.reshape(t, w), dc.reshape(t, 128), dcr.reshape(N_HEADS, 1, t))
    return grads, (received[0] if fused else None)


def _loss_call(x, g, target):
    t, d = x.shape
    tr = 512

    def body(x_ref, g_ref, t_ref, loss_ref, dx_ref, dg_ref):
        tgt = t_ref[...]

        def f(xv, gv):
            return 0.5 * jnp.sum(jnp.mean(jnp.square(_rms(xv, gv) - tgt), axis=-1))

        val, vjp = jax.vjp(f, x_ref[...], g_ref[...])
        dx, dg = vjp(jnp.ones((), F32))
        dx_ref[...] = dx

        @pl.when(pl.program_id(0) == 0)
        def _():
            loss_ref[...] = jnp.zeros_like(loss_ref)
            dg_ref[...] = jnp.zeros_like(dg_ref)

        loss_ref[...] += jnp.full(loss_ref.shape, val, F32)
        dg_ref[...] += dg

    row = _row_spec(tr, d)
    return pl.pallas_call(
        body, name="loss_head", grid=(t // tr,), in_specs=[row, _full_spec(g), row],
        out_specs=[pl.BlockSpec((1, 128), lambda i: (0, 0)), row, _full_spec(g)],
        out_shape=[jax.ShapeDtypeStruct((1, 128), F32), jax.ShapeDtypeStruct((t, d), F32),
                   jax.ShapeDtypeStruct(g.shape, F32)],
        compiler_params=_cp(("arbitrary",)))(x, g, target)


def _blockdiag(w):
    g, a, b = w.shape
    return jnp.einsum('gab,gk->gakb', w, jnp.eye(g, dtype=w.dtype)).reshape(g * a, g * b)


def _s5_params(p):
    rep = lambda a: jnp.repeat(a, S5_GROUP, axis=0)
    rows = S5_GROUPS * S5_GROUP
    bt = lambda b: b.transpose(0, 2, 1).reshape(rows, S5_STATE)
    abar_re, abar_im, bb_re, bb_im = s5_disc(
        rep(p["s5_lambda_re"]), rep(p["s5_lambda_im"]), rep(p["s5_log_dt"][:, None]), bt(p["s5_b_re"]), bt(p["s5_b_im"]))
    first = lambda a: a.reshape(S5_GROUPS, S5_GROUP, S5_STATE)[:, 0, :].reshape(1, S5_GROUPS * S5_STATE)
    g3 = lambda a: a.reshape(S5_GROUPS, S5_GROUP, S5_STATE)
    cblk = lambda c: _blockdiag(c.transpose(0, 2, 1))
    return (first(abar_re), first(abar_im), _blockdiag(g3(bb_re)), _blockdiag(g3(bb_im)),
            cblk(p["s5_c_re"]), cblk(p["s5_c_im"]))


MIXER_WEIGHTS = ('sgu_norm_g', 'sgu_w', 'sgu_b', 's5_lambda_re', 's5_lambda_im', 's5_log_dt', 's5_b_re', 's5_b_im',
                 's5_c_re', 's5_c_im', 's5_d', 's5_glu_w', 's5_glu_b', 'lru_conv_w', 'lru_conv_b', 'lru_wa', 'lru_ba',
                 'lru_wx', 'lru_bx', 'lru_lambda', 'fox_fgate_b')


def _mixers_pre(z, p):
    w = MIXER_WIDTH
    row = lambda a: a[None, :]
    a_u, a_v, b_in, c_x, c_gate, d_q, d_k, d_v, d_f = jnp.split(z, [w * i for i in range(1, 9)], axis=1)
    sw = p["sgu_w"]
    (y_a,) = sgu_mix(a_u, a_v, row(p["sgu_norm_g"]), sw[0], sw[1], sw[2], sw[3],
                     jnp.repeat(p["sgu_b"].T, HEAD_DIM, axis=1))
    abar_re, abar_im, bblk_re, bblk_im, cblk_re, cblk_im = _s5_params(p)
    s_re, s_im = lti_scan(abar_re, abar_im, mm(b_in, bblk_re), mm(b_in, bblk_im))
    (y_b,) = s5_post(s_re, s_im, b_in, cblk_re, cblk_im, row(p["s5_d"]), p["s5_glu_w"], row(p["s5_glu_b"]))
    xc = causal_conv(c_x, p["lru_conv_w"], row(p["lru_conv_b"]))
    a, b = lru_pre(xc, _blockdiag(p["lru_wa"]), p["lru_ba"].reshape(1, w), _blockdiag(p["lru_wx"]),
                   p["lru_bx"].reshape(1, w), row(p["lru_lambda"]))
    (y_c,) = lru_post(tv_scan(a, b), c_gate)
    (log_f,) = log_sig(d_f, jnp.pad(p["fox_fgate_b"], (0, 128 - N_HEADS))[None, :])
    c128 = tv_scan(jnp.ones_like(log_f), log_f)
    return y_a, y_b, y_c, d_q, d_k, d_v, c128, c128[:, :N_HEADS].T[:, None, :]


PACK_COLS = 1024
SHARD_SHAPE = {'w_mlp_in': (1024, 1024), 'w_mlp_out': (1024, 1024), 'w_out': (256, 1024), 'w_in': (1024, 513),
               's5_glu_w': (64, 256), 'lru_conv_w': (4, 64)}
SHARDED_AXIS = {'w_in': 1, 's5_glu_w': 0, 'lru_conv_w': 1, 'w_out': 0, 'w_mlp_in': 1, 'w_mlp_out': 0}
SHARD_ROWS = {n: -(-s[0] * s[1] // PACK_COLS) for n, s in SHARD_SHAPE.items()}
SHARD_OFF = {n: sum(list(SHARD_ROWS.values())[:i]) for i, n in enumerate(SHARD_SHAPE)}
LAYER_ROWS = 2880
SMALL_OFF = SHARD_OFF['w_in']
GLUE_ROWS = LAYER_ROWS - SMALL_OFF
assert SHARD_OFF['w_mlp_out'] == 1024 and SHARD_OFF['w_out'] == 2048 and SMALL_OFF % GLUE_ROWS == 0
assert SHARD_OFF['lru_conv_w'] + SHARD_ROWS['lru_conv_w'] <= LAYER_ROWS
PACK_ROWS = DEPTH * LAYER_ROWS
TOK = 1024
FF = 4 * D_MODEL


def _w3(i_of):
    return pl.BlockSpec((None, 1024, PACK_COLS), i_of)


def _tile2(rows, cols, i_of):
    return pl.BlockSpec((rows, cols), i_of)


def _layer_fwd(x, gathered, w_in, w_out, p, mix_p, next_shard):
    t = x.shape[0]
    nt = t // TOK
    f32 = lambda r, c: jax.ShapeDtypeStruct((r, c), F32)
    b16 = lambda r, c: jax.ShapeDtypeStruct((r, c), BF16)
    g1, g2, gm = p["norm1_g"][None, :], p["norm2_g"][None, :], p["mix_norm_g"][None, :]
    (h1,) = _rowwise(_f_rms, [x], [g1], [D_MODEL], name="rms_f", tr=512, dtype=BF16)
    z = _matmul("mm_in", (nt, 1, 1), h1, _tile2(TOK, D_MODEL, lambda i, j, k: (i, 0)),
                w_in, _tile2(D_MODEL, D_IN_PAD, lambda i, j, k: (0, 0)), (1, 0),
                [(f32(t, D_IN_PAD), _tile2(TOK, D_IN_PAD, lambda i, j, k: (i, 0)))])
    (y_a, y_b, y_c, *attn_in), mix_vjp = jax.vjp(_mixers_pre, z, mix_p)
    y_d, lse, *next_gathered = _attn_fwd_call(*attn_in, next_shard=next_shard)
    ys = (y_a, y_b, y_c, y_d)
    (yn,) = _rowwise(_f_gnorm, list(ys), [gm], [D_MODEL], name="gnorm_f", tr=512, dtype=BF16)
    x_tile = _tile2(TOK, D_MODEL, lambda i, j, k: (i, 0))
    x1 = _matmul("mm_out", (nt, 1, 1), yn, x_tile, w_out, _tile2(D_MODEL, D_MODEL, lambda i, j, k: (0, 0)), (1, 0),
                 [(f32(t, D_MODEL), x_tile)], extras=[(x, x_tile)], epilogue=lambda acc, r: (acc + r,))
    (h2,) = _rowwise(_f_rms, [x1], [g2], [D_MODEL], name="rms_f", tr=512, dtype=BF16)
    ff_tile = _tile2(TOK, 1024, lambda i, j, k: (i, j))

    def up_epilogue(acc):
        r = jnp.maximum(acc, 0.0)
        return r * r, r

    act, relu = _matmul("mm_up", (nt, FF // 1024, 1), h2, x_tile, gathered, _w3(lambda i, j, k: (j, 0, 0)), (1, 0),
                        [(b16(t, FF), ff_tile), (b16(t, FF), ff_tile)], epilogue=up_epilogue)
    x2 = _matmul("mm_down", (nt, 1, FF // 1024), act, _tile2(TOK, 1024, lambda i, j, k: (i, k)),
                 gathered, _w3(lambda i, j, k: (k, 1, 0)), (1, 0),
                 [(f32(t, D_MODEL), x_tile)], extras=[(x1, x_tile)], epilogue=lambda acc, r: (acc + r,))
    res = (x, h1, mix_vjp, ys, attn_in, lse, yn, x1, h2, act, relu)
    return x2, res, (next_gathered[0] if next_gathered else None)


def _layer_bwd(g, res, gathered, w_in, w_out, p, exchange):
    x, h1, mix_vjp, ys, attn_in, lse, yn, x1, h2, act, relu = res
    send = lax.empty((4, LAYER_ROWS, PACK_COLS), BF16)
    t = x.shape[0]
    nt = t // TOK
    f32 = lambda r, c: jax.ShapeDtypeStruct((r, c), F32)
    g1, g2, gm = p["norm1_g"][None, :], p["norm2_g"][None, :], p["mix_norm_g"][None, :]
    x_tile = _tile2(TOK, D_MODEL, lambda i, j, k: (i, 0))
    ff_tile = _tile2(TOK, 1024, lambda i, j, k: (i, j))
    tok_k = _tile2(TOK, D_MODEL, lambda i, j, k: (k, 0))
    send_s = jax.ShapeDtypeStruct(send.shape, send.dtype)
    du = _matmul("mm_down_dx", (nt, FF // 1024, 1), g, x_tile, gathered, _w3(lambda i, j, k: (j, 1, 0)), (1, 1),
                 [(jax.ShapeDtypeStruct((t, FF), BF16), ff_tile)], extras=[(relu, ff_tile)],
                 epilogue=lambda acc, r: (2.0 * r.astype(F32) * acc,))
    send = _matmul("mm_down_dw", (FF // 1024, 1, nt), act, _tile2(TOK, 1024, lambda i, j, k: (k, i)), g, tok_k, (0, 0),
                   [(send_s, _w3(lambda i, j, k: (i, 1, 0)))], into=send)
    send = _matmul("mm_up_dw", (1, FF // 1024, nt), h2, tok_k, du, _tile2(TOK, 1024, lambda i, j, k: (k, j)), (0, 0),
                   [(send_s, _w3(lambda i, j, k: (j, 0, 0)))], into=send)
    dh2 = _matmul("mm_up_dx", (nt, 1, FF // 1024), du, _tile2(TOK, 1024, lambda i, j, k: (i, k)),
                  gathered, _w3(lambda i, j, k: (k, 0, 0)), (1, 1), [(f32(t, D_MODEL), x_tile)])
    g_mid, dg2 = _rowwise_vjp(_f_rms, [x1], [g2], [dh2], name="rms_b", tr=512, add=g)
    w_full = _tile2(D_MODEL, D_MODEL, lambda i, j, k: (0, 0))
    dyn = _matmul("mm_out_dx", (nt, 1, 1), g_mid, x_tile, w_out, w_full, (1, 1), [(f32(t, D_MODEL), x_tile)])
    send = _matmul("mm_out_dw", (4, 1, nt), yn, _tile2(TOK, 256, lambda i, j, k: (k, i)), g_mid, tok_k, (0, 0),
                   [(send_s, pl.BlockSpec((None, 256, PACK_COLS),
                                          lambda i, j, k: (i, SHARD_OFF['w_out'] // 256, 0)))], into=send)
    dy_a, dy_b, dy_c, dy_d, dgm = _rowwise_vjp(_f_gnorm, list(ys), [gm], [dyn], name="gnorm_b", tr=512)
    d_attn_in, received = _attn_bwd_call(*attn_in, ys[3], lse, dy_d, exchange=exchange)
    dz, dmix = mix_vjp((dy_a, dy_b, dy_c, *d_attn_in))
    dz = dz.astype(BF16)
    z_tile = _tile2(TOK, D_IN_PAD, lambda i, j, k: (i, 0))
    d_w_in = _matmul("mm_in_dw", (1, 1, t // 512), h1, _tile2(512, D_MODEL, lambda i, j, k: (k, 0)),
                     dz, _tile2(512, D_IN_PAD, lambda i, j, k: (k, 0)), (0, 0),
                     [(f32(D_MODEL, D_IN_PAD), _tile2(D_MODEL, D_IN_PAD, lambda i, j, k: (0, 0)))])
    dh1 = _matmul("mm_in_dx", (nt, 1, 1), dz, z_tile, w_in, _tile2(D_MODEL, D_IN_PAD, lambda i, j, k: (0, 0)), (1, 1),
                  [(f32(t, D_MODEL), x_tile)])
    dx, dg1 = _rowwise_vjp(_f_rms, [x], [g1], [dh1], name="rms_b", tr=512, add=g_mid)
    small = dict(dmix, norm1_g=dg1[0], norm2_g=dg2[0], mix_norm_g=dgm[0])
    return dx, small, d_w_in, send, received


HBM = pl.BlockSpec(memory_space=pltpu.HBM)
D2D_CHUNKS = 15
ICI_CHUNKS = 5
VMEM_CHUNKS = 4


def _coords():
    return lax.axis_index("x"), lax.axis_index("y"), lax.axis_index("c")


def _other_chips(x, y):
    return [(1 - x, y), (x, 1 - y), (1 - x, 1 - y)]


def _start_chunks(make, rows, n):
    size = rows // n
    assert size * n == rows
    for k in range(n):
        make(pl.ds(k * size, size)).start()


AG_SEMS = 7


def _allgather_copies(in_ref, out_ref, send_sems, recv_sems):
    r = in_ref.shape[0]
    rh = r // 2
    x, y, c = _coords()
    me, sibling = (x, y, c), (x, y, 1 - c)
    chips = _other_chips(x, y)

    def half(px, py, pc, rows=pl.ds(0, rh)):
        return out_ref.at[2 * px + py, pl.ds(pc * rh + rows.start, rows.size), :]

    def copy(k, block, to, rows=pl.ds(0, rh), from_input=False):
        src = in_ref.at[pl.ds(block[2] * rh + rows.start, rows.size), :] if from_input else half(*block, rows)
        return pltpu.make_async_remote_copy(
            src_ref=src, dst_ref=half(*block, rows), send_sem=send_sems.at[k], recv_sem=recv_sems.at[k],
            device_id=to, device_id_type=MESH)

    def own(rows=pl.ds(0, r)):
        return pltpu.make_async_remote_copy(
            src_ref=in_ref.at[rows, :], dst_ref=out_ref.at[2 * x + y, rows, :], send_sem=send_sems.at[6],
            recv_sem=recv_sems.at[6], device_id=sibling, device_id_type=MESH)

    def start():
        for j, chip in enumerate(chips):
            _start_chunks(lambda rows: copy(j, me, (*chip, c), rows, from_input=True), rh, ICI_CHUNKS)
        _start_chunks(own, r, D2D_CHUNKS)

    def finish():
        for j, chip in enumerate(chips):
            copy(j, (*chip, c), me).wait_recv()
            _start_chunks(lambda rows: copy(3 + j, (*chip, c), sibling, rows), rh, D2D_CHUNKS)
        for j, chip in enumerate(chips):
            copy(3 + j, (*chip, 1 - c), me).wait_recv()
        for j, chip in enumerate(chips):
            copy(j, me, (*chip, c), from_input=True).wait_send()
            copy(3 + j, (*chip, c), sibling).wait_send()
        own().wait()

    return start, finish


def _allgather_shards(shard):
    def body(in_ref, out_ref, send_sems, recv_sems):
        start, finish = _allgather_copies(in_ref, out_ref, send_sems, recv_sems)
        start()
        finish()

    return pl.pallas_call(
        body, name="allgather_shards", out_shape=jax.ShapeDtypeStruct((4,) + shard.shape, shard.dtype),
        in_specs=[HBM], out_specs=HBM,
        scratch_shapes=[pltpu.SemaphoreType.DMA((AG_SEMS,)), pltpu.SemaphoreType.DMA((AG_SEMS,))],
        compiler_params=pltpu.CompilerParams())(shard)


def _pair_exchange(g):
    s, _, rh, cols = g.shape

    def body(g_ref, recv_ref, send_sem, recv_sem):
        x, y, c = _coords()

        def copy(slot, rows):
            return pltpu.make_async_remote_copy(
                src_ref=g_ref.at[slot, 1 - c, rows, :], dst_ref=recv_ref.at[slot, rows, :], send_sem=send_sem,
                recv_sem=recv_sem, device_id=(x, y, 1 - c), device_id_type=MESH)

        for slot in range(s):
            _start_chunks(lambda rows: copy(slot, rows), rh, VMEM_CHUNKS)
        pltpu.make_async_remote_copy(
            src_ref=g_ref.at[:, 1 - c], dst_ref=recv_ref, send_sem=send_sem, recv_sem=recv_sem,
            device_id=(x, y, 1 - c), device_id_type=MESH).wait()

    return pl.pallas_call(
        body, name="pair_exchange", out_shape=jax.ShapeDtypeStruct((s, rh, cols), g.dtype), in_specs=[HBM],
        out_specs=HBM, scratch_shapes=[pltpu.SemaphoreType.DMA] * 2, compiler_params=pltpu.CompilerParams())(g)


def _chip_exchange_copies(p_ref, recv_ref, send_sems, recv_sems):
    rh = p_ref.shape[1]
    x, y, c = _coords()
    chips = _other_chips(x, y)

    def copy(j, chip, rows=pl.ds(0, rh)):
        return pltpu.make_async_remote_copy(
            src_ref=p_ref.at[2 * chip[0] + chip[1], rows, :], dst_ref=recv_ref.at[j, rows, :],
            send_sem=send_sems.at[j], recv_sem=recv_sems.at[j], device_id=(*chip, c), device_id_type=MESH)

    def start():
        for j, chip in enumerate(chips):
            _start_chunks(lambda rows: copy(j, chip, rows), rh, ICI_CHUNKS)

    def finish():
        for j, chip in enumerate(chips):
            copy(j, chip).wait_recv()
        for j, chip in enumerate(chips):
            copy(j, chip).wait_send()

    return start, finish


def _chip_exchange(p):
    def body(p_ref, recv_ref, send_sems, recv_sems):
        start, finish = _chip_exchange_copies(p_ref, recv_ref, send_sems, recv_sems)
        start()
        finish()

    return pl.pallas_call(
        body, name="chip_exchange", out_shape=jax.ShapeDtypeStruct((3,) + p.shape[1:], p.dtype), in_specs=[HBM],
        out_specs=HBM, scratch_shapes=[pltpu.SemaphoreType.DMA((3,)), pltpu.SemaphoreType.DMA((3,))],
        compiler_params=pltpu.CompilerParams())(p)


def _sum_chips(p, recv, tr):
    _, rh, cols = p.shape
    x, y, c = _coords()
    where = jnp.stack([2 * x + y, c]).astype(jnp.int32)

    def body(w_ref, own_ref, r_ref, o_ref):
        acc = own_ref[...].astype(F32)
        for k in range(3):
            acc = acc + r_ref[k].astype(F32)
        o_ref[...] = acc

    return pl.pallas_call(
        body, name="sum_chips", out_shape=jax.ShapeDtypeStruct((2, rh, cols), F32),
        grid_spec=pltpu.PrefetchScalarGridSpec(
            num_scalar_prefetch=1, grid=(rh // tr,),
            in_specs=[pl.BlockSpec((None, tr, cols), lambda i, w_ref: (w_ref[0], i, 0)),
                      pl.BlockSpec((3, tr, cols), lambda i, w_ref: (0, i, 0))],
            out_specs=pl.BlockSpec((None, tr, cols), lambda i, w_ref: (w_ref[1], i, 0))),
        compiler_params=_cp(("arbitrary",)))(where, p, recv)


def _pair_share(buf):
    _, rh, cols = buf.shape

    def body(in_ref, out_ref, send_sem, recv_sem):
        x, y, c = _coords()

        def copy(slot, rows=pl.ds(0, rh)):
            return pltpu.make_async_remote_copy(
                src_ref=in_ref.at[slot, rows, :], dst_ref=out_ref.at[slot, rows, :], send_sem=send_sem,
                recv_sem=recv_sem, device_id=(x, y, 1 - c), device_id_type=MESH)

        _start_chunks(lambda rows: copy(c, rows), rh, D2D_CHUNKS)
        copy(c).wait_send()
        copy(1 - c).wait_recv()

    return pl.pallas_call(
        body, name="pair_share", out_shape=jax.ShapeDtypeStruct(buf.shape, buf.dtype), in_specs=[HBM], out_specs=HBM,
        scratch_shapes=[pltpu.SemaphoreType.DMA] * 2, input_output_aliases={0: 0},
        compiler_params=pltpu.CompilerParams())(buf)


def _allgather_all(blk):
    m_per, cols = blk.shape
    whole = pl.ds(0, m_per)

    def body(x_ref, out_ref, send_sems, recv_sems, local_sem):
        x, y, c = _coords()
        me, sibling = (x, y, c), (x, y, 1 - c)
        chips = _other_chips(x, y)

        def rows_of(px, py, pc, rows):
            return out_ref.at[4 * px + 2 * py + pc, rows, :]

        def copy(k, block, to, rows=whole, from_input=False):
            return pltpu.make_async_remote_copy(
                src_ref=x_ref.at[rows, :] if from_input else rows_of(*block, rows), dst_ref=rows_of(*block, rows),
                send_sem=send_sems.at[k], recv_sem=recv_sems.at[k], device_id=to, device_id_type=MESH)

        mine = pltpu.make_async_copy(x_ref, rows_of(*me, whole), local_sem)
        mine.start()
        _start_chunks(lambda rows: copy(0, me, sibling, rows, from_input=True), m_per, VMEM_CHUNKS)
        for j, chip in enumerate(chips):
            _start_chunks(lambda rows: copy(1 + j, me, (*chip, c), rows, from_input=True), m_per, VMEM_CHUNKS)
        for j, chip in enumerate(chips):
            copy(1 + j, (*chip, c), me).wait_recv()
            _start_chunks(lambda rows: copy(4 + j, (*chip, c), sibling, rows), m_per, VMEM_CHUNKS)
        copy(0, sibling, me).wait_recv()
        for j, chip in enumerate(chips):
            copy(4 + j, (*chip, 1 - c), me).wait_recv()
        copy(0, me, sibling, from_input=True).wait_send()
        for j, chip in enumerate(chips):
            copy(1 + j, me, (*chip, c), from_input=True).wait_send()
            copy(4 + j, (*chip, c), sibling).wait_send()
        mine.wait()

    return pl.pallas_call(
        body, name="allgather_all", out_shape=jax.ShapeDtypeStruct((8, m_per, cols), blk.dtype),
        in_specs=[pl.BlockSpec(memory_space=pltpu.VMEM)], out_specs=pl.BlockSpec(memory_space=pltpu.VMEM),
        scratch_shapes=[pltpu.SemaphoreType.DMA((7,)), pltpu.SemaphoreType.DMA((7,)), pltpu.SemaphoreType.DMA],
        compiler_params=pltpu.CompilerParams(vmem_limit_bytes=VMEM_LIMIT))(blk)


def _add_kept(g, recv, tr):
    s, _, rh, cols = g.shape

    def body(c_ref, a_ref, b_ref, o_ref):
        o_ref[...] = (a_ref[...].astype(F32) + b_ref[...].astype(F32)).astype(o_ref.dtype)

    spec = pl.BlockSpec((None, tr, cols), lambda si, i, c_ref: (si, i, 0))
    return pl.pallas_call(
        body, name="add_kept", out_shape=jax.ShapeDtypeStruct((s, rh, cols), BF16),
        grid_spec=pltpu.PrefetchScalarGridSpec(
            num_scalar_prefetch=1, grid=(s, rh // tr),
            in_specs=[pl.BlockSpec((None, None, tr, cols), lambda si, i, c_ref: (si, c_ref[0], i, 0)), spec],
            out_specs=spec),
        compiler_params=_cp(("arbitrary", "arbitrary")))(lax.axis_index("c").astype(jnp.int32).reshape(1), g, recv)


def _sum_slots(p, tr, name):
    s, rows, cols = p.shape

    def body(p_ref, o_ref):
        acc = p_ref[0].astype(F32)
        for k in range(1, s):
            acc = acc + p_ref[k].astype(F32)
        o_ref[...] = acc

    return pl.pallas_call(
        body, name=name, grid=(rows // tr,), in_specs=[pl.BlockSpec((s, tr, cols), lambda i: (0, i, 0))],
        out_specs=_row_spec(tr, cols), out_shape=jax.ShapeDtypeStruct((rows, cols), F32),
        compiler_params=_cp(("arbitrary",)))(p)


def _adamw_call(w, g, m, v, name):
    rows, cols = w.shape
    tr = _tile(rows, 512) if rows % 512 == 0 else _tile(rows, 128)
    c1 = 1.0 - ADAM_B1 ** ADAM_STEP
    c2 = 1.0 - ADAM_B2 ** ADAM_STEP

    def body(w_ref, g_ref, m_ref, v_ref, d_ref, nm_ref, nv_ref):
        gv = g_ref[...]
        nm = ADAM_B1 * m_ref[...] + (1.0 - ADAM_B1) * gv
        nv = ADAM_B2 * v_ref[...] + (1.0 - ADAM_B2) * jnp.square(gv)
        d_ref[...] = -ADAM_LR * ((nm / c1) / (jnp.sqrt(nv / c2) + ADAM_EPS) + ADAM_WD * w_ref[...])
        nm_ref[...] = nm
        nv_ref[...] = nv

    spec = _row_spec(tr, cols)
    o = jax.ShapeDtypeStruct((rows, cols), F32)
    return pl.pallas_call(body, name=name, grid=(rows // tr,), in_specs=[spec] * 4, out_specs=[spec] * 3,
                          out_shape=[o, o, o], compiler_params=_cp(("arbitrary",)))(w, g, m, v)


WEIGHTS = ('norm1_g', 'w_in', 'sgu_norm_g', 'sgu_w', 'sgu_b', 's5_lambda_re', 's5_lambda_im', 's5_log_dt',
           's5_b_re', 's5_b_im', 's5_c_re', 's5_c_im', 's5_d', 's5_glu_w', 's5_glu_b', 'lru_conv_w',
           'lru_conv_b', 'lru_wa', 'lru_ba', 'lru_wx', 'lru_bx', 'lru_lambda', 'fox_fgate_b', 'mix_norm_g',
           'w_out', 'norm2_g', 'w_mlp_in', 'w_mlp_out', 'final_g')
N_W = len(WEIGHTS)


def _pack_shards(shards, dtype, names=tuple(SHARD_SHAPE), rows=LAYER_ROWS):
    parts = []
    for n in names:
        lead = shards[n].shape[:-2]
        flat = shards[n].reshape(*lead, -1).astype(dtype)
        flat = jnp.pad(flat, [(0, 0)] * len(lead) + [(0, SHARD_ROWS[n] * PACK_COLS - flat.shape[-1])])
        parts.append(flat.reshape(*lead, SHARD_ROWS[n], PACK_COLS))
    lead = parts[0].shape[:-2]
    used = sum(SHARD_ROWS[n] for n in names)
    if rows > used:
        parts.append(jnp.zeros((*lead, rows - used, PACK_COLS), dtype))
    return jnp.concatenate(parts, axis=-2)


def _unpack_shards(buf, names=tuple(SHARD_SHAPE)):
    lead = buf.shape[:-2]
    out = {}
    for n in names:
        s0, s1 = SHARD_SHAPE[n]
        rows, off = SHARD_ROWS[n], SHARD_OFF[n]
        flat = buf[..., off:off + rows, :].reshape(*lead, rows * PACK_COLS)
        out[n] = flat[..., :s0 * s1].reshape(*lead, s0, s1)
    return out


def _join_chips(g, axis):
    _, d, s0, s1 = g.shape
    if axis == 0:
        return g.transpose(1, 0, 2, 3).reshape(d, 4 * s0, s1)
    return g.transpose(1, 2, 0, 3).reshape(d, s0, 4 * s1)


def _split_chips(w, axis):
    d = w.shape[0]
    if axis == 0:
        return w.reshape(d, 4, w.shape[1] // 4, w.shape[2]).transpose(1, 0, 2, 3)
    return w.reshape(d, w.shape[1], 4, w.shape[2] // 4).transpose(2, 0, 1, 3)


def _flat_rows(shape):
    return -(-math.prod(shape) // PACK_COLS)


def _pack_flat(arrs, rows):
    parts = []
    for a in arrs:
        flat = a.reshape(-1)
        r = _flat_rows(a.shape)
        parts.append(jnp.pad(flat, (0, r * PACK_COLS - flat.shape[0])).reshape(r, PACK_COLS))
    used = sum(p.shape[0] for p in parts)
    parts.append(jnp.zeros((rows - used, PACK_COLS), F32))
    return jnp.concatenate(parts, axis=0)


def _unpack_flat(buf, shapes):
    out, off = [], 0
    for s in shapes:
        r = _flat_rows(s)
        out.append(buf[off:off + r].reshape(-1)[:math.prod(s)].reshape(s))
        off += r
    return out


def _write_glue(send, glue):
    def body(g_ref, s_ref, o_ref):
        o_ref[...] = g_ref[...]

    blk = (None, GLUE_ROWS, PACK_COLS)
    return pl.pallas_call(
        body, name="write_glue", grid=(4,),
        in_specs=[pl.BlockSpec(blk, lambda s: (s, 0, 0)), pl.BlockSpec(memory_space=pl.ANY)],
        out_specs=pl.BlockSpec(blk, lambda s: (s, SMALL_OFF // GLUE_ROWS, 0)),
        out_shape=jax.ShapeDtypeStruct(send.shape, send.dtype), input_output_aliases={1: 0},
        compiler_params=_cp(("arbitrary",)))(glue, send)


GLUE_PACKED = ('w_in', 's5_glu_w', 'lru_conv_w')
REDUCE_ROWS = LAYER_ROWS // 4


def _layer_weights(gathered):
    parts = _unpack_shards(gathered[:, None], GLUE_PACKED + ('w_out',))
    joined = {n: _join_chips(g, SHARDED_AXIS[n])[0] for n, g in parts.items()}
    joined['w_in'] = jnp.pad(joined['w_in'], ((0, 0), (0, D_IN_PAD - D_IN_PROJ)))
    return joined


def _reduce_start(send):
    halves = send.reshape(4, 2, LAYER_ROWS // 2, PACK_COLS)
    return _add_kept(halves, _pair_exchange(halves), REDUCE_ROWS)


def _reduce_finish(chip_sum, received):
    return _pair_share(_sum_chips(chip_sum, received, REDUCE_ROWS)).reshape(LAYER_ROWS, PACK_COLS)


def _forward_backward(x, target, final_g, shards, rep):
    norm_p = [{n: rep[n][l] for n in ('norm1_g', 'norm2_g', 'mix_norm_g')} for l in range(DEPTH)]
    gathered = _allgather_shards(shards[0])
    layers = []
    for l in range(DEPTH):
        lw = _layer_weights(gathered)
        mix_p = {n: lw[n].astype(F32) if n in lw else rep[n][l] for n in MIXER_WEIGHTS}
        x, res, following = _layer_fwd(x, gathered, lw['w_in'], lw['w_out'], norm_p[l], mix_p,
                                       shards[l + 1] if l + 1 < DEPTH else None)
        layers.append((res, gathered, lw))
        gathered = following
    loss_part, g, d_final = _loss_call(x, final_g[None, :], target)

    small, reduced, pending = [None] * DEPTH, [None] * DEPTH, None
    for l in reversed(range(DEPTH)):
        res, gathered, lw = layers[l]
        g, small[l], d_w_in, send, received = _layer_bwd(g, res, gathered, lw['w_in'], lw['w_out'], norm_p[l], pending)
        if pending is not None:
            reduced[l + 1] = _reduce_finish(pending, received)
        mine = {'w_in': d_w_in[:, :D_IN_PROJ], 's5_glu_w': small[l].pop('s5_glu_w'),
                'lru_conv_w': small[l].pop('lru_conv_w')}
        glue = _pack_shards({n: _split_chips(a[None], SHARDED_AXIS[n]) for n, a in mine.items()}, BF16,
                            names=GLUE_PACKED, rows=GLUE_ROWS)[:, 0]
        pending = _reduce_start(_write_glue(send, glue))
    reduced[0] = _reduce_finish(pending, _chip_exchange(pending))
    stacked = {n: jnp.stack([small[l][n] for l in range(DEPTH)]) for n in small[0]}
    return loss_part, g, d_final, stacked, _unpack_shards(jnp.stack(reduced))


def _step(*args):
    x, target = args[0], args[1 + N_W]
    w = dict(zip(WEIGHTS, args[1:1 + N_W]))
    m = dict(zip(WEIGHTS, args[2 + N_W:2 + 2 * N_W]))
    v = dict(zip(WEIGHTS, args[2 + 2 * N_W:2 + 3 * N_W]))
    small = [n for n in WEIGHTS if n not in SHARD_SHAPE]

    shards = _pack_shards({n: w[n] for n in SHARD_SHAPE}, BF16)
    loss_part, dx, d_final, dw, g_shard = _forward_backward(
        x[0], target[0], w['final_g'], shards, {n: w[n] for n in small})

    small_g = [d_final.reshape(-1) if n == 'final_g' else dw[n] for n in small]
    small_rows = -(-(sum(_flat_rows(w[n].shape) for n in small) + 1) // 128) * 128
    small_sum = _sum_slots(_allgather_all(_pack_flat(small_g + [loss_part[0, :1]], small_rows)), 128, "sum_devices")
    *g_small, loss = _unpack_flat(small_sum, [w[n].shape for n in small] + [()])

    grads, delta, new_m, new_v = {}, {}, {}, {}
    for n in SHARD_SHAPE:
        shp = w[n].shape
        v2 = lambda a: a.reshape(-1, shp[-1])
        res = _adamw_call(v2(w[n]), v2(g_shard[n]), v2(m[n]), v2(v[n]), "adamw_" + n)
        grads[n] = g_shard[n]
        delta[n], new_m[n], new_v[n] = (r.reshape(shp) for r in res)
    pk = lambda d: _pack_flat([d[n] for n in small], small_rows)
    res = _adamw_call(pk(w), _pack_flat(g_small, small_rows), pk(m), pk(v), "adamw_small")
    shapes = [w[n].shape for n in small]
    for n, g, d_, m_, v_ in zip(small, g_small, *(_unpack_flat(r, shapes) for r in res)):
        grads[n], delta[n], new_m[n], new_v[n] = g, d_, m_, v_

    return (loss, dx[None], *[grads[n] for n in WEIGHTS], *[delta[n] for n in WEIGHTS],
            *[new_m[n] for n in WEIGHTS], *[new_v[n] for n in WEIGHTS])


def kernel(x, norm1_g, w_in, sgu_norm_g, sgu_w, sgu_b, s5_lambda_re, s5_lambda_im, s5_log_dt, s5_b_re, s5_b_im, s5_c_re, s5_c_im, s5_d, s5_glu_w, s5_glu_b, lru_conv_w, lru_conv_b, lru_wa, lru_ba, lru_wx, lru_bx, lru_lambda, fox_fgate_b, mix_norm_g, w_out, norm2_g, w_mlp_in, w_mlp_out, final_g, loss_target, m_norm1_g, m_w_in, m_sgu_norm_g, m_sgu_w, m_sgu_b, m_s5_lambda_re, m_s5_lambda_im, m_s5_log_dt, m_s5_b_re, m_s5_b_im, m_s5_c_re, m_s5_c_im, m_s5_d, m_s5_glu_w, m_s5_glu_b, m_lru_conv_w, m_lru_conv_b, m_lru_wa, m_lru_ba, m_lru_wx, m_lru_bx, m_lru_lambda, m_fox_fgate_b, m_mix_norm_g, m_w_out, m_norm2_g, m_w_mlp_in, m_w_mlp_out, m_final_g, v_norm1_g, v_w_in, v_sgu_norm_g, v_sgu_w, v_sgu_b, v_s5_lambda_re, v_s5_lambda_im, v_s5_log_dt, v_s5_b_re, v_s5_b_im, v_s5_c_re, v_s5_c_im, v_s5_d, v_s5_glu_w, v_s5_glu_b, v_lru_conv_w, v_lru_conv_b, v_lru_wa, v_lru_ba, v_lru_wx, v_lru_bx, v_lru_lambda, v_fox_fgate_b, v_mix_norm_g, v_w_out, v_norm2_g, v_w_mlp_in, v_w_mlp_out, v_final_g):
    return _step(x, norm1_g, w_in, sgu_norm_g, sgu_w, sgu_b, s5_lambda_re, s5_lambda_im, s5_log_dt, s5_b_re, s5_b_im, s5_c_re, s5_c_im, s5_d, s5_glu_w, s5_glu_b, lru_conv_w, lru_conv_b, lru_wa, lru_ba, lru_wx, lru_bx, lru_lambda, fox_fgate_b, mix_norm_g, w_out, norm2_g, w_mlp_in, w_mlp_out, final_g, loss_target, m_norm1_g, m_w_in, m_sgu_norm_g, m_sgu_w, m_sgu_b, m_s5_lambda_re, m_s5_lambda_im, m_s5_log_dt, m_s5_b_re, m_s5_b_im, m_s5_c_re, m_s5_c_im, m_s5_d, m_s5_glu_w, m_s5_glu_b, m_lru_conv_w, m_lru_conv_b, m_lru_wa, m_lru_ba, m_lru_wx, m_lru_bx, m_lru_lambda, m_fox_fgate_b, m_mix_norm_g, m_w_out, m_norm2_g, m_w_mlp_in, m_w_mlp_out, m_final_g, v_norm1_g, v_w_in, v_sgu_norm_g, v_sgu_w, v_sgu_b, v_s5_lambda_re, v_s5_lambda_im, v_s5_log_dt, v_s5_b_re, v_s5_b_im, v_s5_c_re, v_s5_c_im, v_s5_d, v_s5_glu_w, v_s5_glu_b, v_lru_conv_w, v_lru_conv_b, v_lru_wa, v_lru_ba, v_lru_wx, v_lru_bx, v_lru_lambda, v_fox_fgate_b, v_mix_norm_g, v_w_out, v_norm2_g, v_w_mlp_in, v_w_mlp_out, v_final_g)
```

```python
import functools
import math

import jax
import jax.numpy as jnp
from jax import lax
from jax.experimental import pallas as pl
from jax.experimental.pallas import tpu as pltpu

F32 = jnp.float32
BF16 = jnp.bfloat16

DEPTH = 4
D_MODEL = 1024
MIXER_WIDTH = 256
SGU_CHUNK = 128
N_HEADS = 4
HEAD_DIM = 64
S5_GROUPS = 16
S5_GROUP = 16
S5_STATE = 64
LRU_C = 8.0
RMS_EPS = 1e-6
D_IN_PROJ = 8 * MIXER_WIDTH + N_HEADS
D_IN_PAD = 8 * MIXER_WIDTH + 128
ADAM_LR, ADAM_B1, ADAM_B2, ADAM_EPS, ADAM_WD, ADAM_STEP = 0.001, 0.9, 0.999, 1e-08, 0.01, 10

V7X_VMEM_BYTES = 64 * 1024 * 1024
VMEM_LIMIT = V7X_VMEM_BYTES - 8 * 1024 * 1024
NEG = -1e30
MESH = pl.DeviceIdType.MESH


def _cp(sem=None, **kw):
    return pltpu.CompilerParams(dimension_semantics=sem, vmem_limit_bytes=VMEM_LIMIT, **kw)


def _full_spec(a):
    nd = a.ndim
    return pl.BlockSpec(a.shape, lambda *_: (0,) * nd)


def _tile(n, pref=512):
    return pref if n % pref == 0 else n


def _dot(a, b, ca, cb):
    return lax.dot_general(a.astype(BF16), b.astype(BF16), (((ca,), (cb,)), ((), ())),
                           preferred_element_type=F32)


def _matmul(name, grid, a, a_spec, b, b_spec, dims, outs, *, extras=(), epilogue=None, into=None, summed=0):
    nk = grid[2]
    n_ex, n_out = len(extras), len(outs)
    tm_tn = tuple(d for d in outs[0][1].block_shape if d is not None)[-2:]

    def body(*refs):
        a_ref, b_ref = refs[0], refs[1]
        ex_refs = refs[2:2 + n_ex]
        o_refs = refs[len(refs) - n_out - (nk > 1):len(refs) - (nk > 1)]
        if summed:
            @pl.when((pl.program_id(0) == 0) & (pl.program_id(1) == 0) & (pl.program_id(2) == 0))
            def _():
                for o_ref in o_refs[n_out - summed:]:
                    o_ref[...] = jnp.zeros_like(o_ref)

        def finish(val):
            res = epilogue(val, *[e[...] for e in ex_refs]) if epilogue else (val,)
            for idx, (o_ref, r) in enumerate(zip(o_refs, res)):
                if idx >= n_out - summed:
                    o_ref[...] += r
                else:
                    o_ref[...] = r.astype(o_ref.dtype)

        if nk == 1:
            finish(_dot(a_ref[...], b_ref[...], *dims))
        else:
            acc = refs[-1]
            kk = pl.program_id(2)

            @pl.when(kk == 0)
            def _():
                acc[...] = jnp.zeros_like(acc)

            acc[...] += _dot(a_ref[...], b_ref[...], *dims)

            @pl.when(kk == nk - 1)
            def _():
                finish(acc[...])

    ins = [a, b] + [e[0] for e in extras]
    specs = [a_spec, b_spec] + [e[1] for e in extras]
    aliases = {}
    if into is not None:
        aliases = {len(ins): 0}
        ins.append(into)
        specs.append(pl.BlockSpec(memory_space=pl.ANY))
    res = pl.pallas_call(
        body, name=name, grid=grid, in_specs=specs, out_specs=[o[1] for o in outs], out_shape=[o[0] for o in outs],
        scratch_shapes=[pltpu.VMEM(tm_tn, F32)] if nk > 1 else [], input_output_aliases=aliases,
        compiler_params=_cp(("arbitrary", "arbitrary", "arbitrary")))(*ins)
    return res[0] if n_out == 1 else res


@jax.custom_vjp
def _bdot(a, b):
    return _dot(a, b, 1, 0)


def _bdot_fwd(a, b):
    return _dot(a, b, 1, 0), (a, b)


def _bdot_bwd(r, g):
    a, b = r
    return _dot(g, b, 1, 1), _dot(a, g, 0, 0)


_bdot.defvjp(_bdot_fwd, _bdot_bwd)


def _row_spec(tr, w):
    return pl.BlockSpec((tr, w), lambda i: (i, 0))


def _rowwise(fn, rows, pars, outs, *, name, tr, dtype=F32):
    t = rows[0].shape[0]
    n_in = len(rows) + len(pars)

    def body(*refs):
        res = fn(*[r[...] for r in refs[:n_in]])
        for o_ref, v in zip(refs[n_in:], res):
            o_ref[...] = v.astype(o_ref.dtype)

    return pl.pallas_call(
        body, name=name, grid=(t // tr,),
        in_specs=[_row_spec(tr, r.shape[1]) for r in rows] + [_full_spec(p) for p in pars],
        out_specs=[_row_spec(tr, w) for w in outs],
        out_shape=[jax.ShapeDtypeStruct((t, w), dtype) for w in outs],
        compiler_params=_cp(("arbitrary",)))(*rows, *pars)


def _rowwise_vjp(fn, rows, pars, cots, *, name, tr, add=None):
    t = rows[0].shape[0]
    nr, npar = len(rows), len(pars)
    cots = list(cots) + ([add] if add is not None else [])
    nc = len(cots)

    def body(*refs):
        vals = [r[...] for r in refs[:nr + npar]]
        cts = [c[...] for c in refs[nr + npar:nr + npar + nc]]
        douts = refs[nr + npar + nc:]
        extra = cts.pop() if add is not None else None
        _, vjp = jax.vjp(fn, *vals)
        grads = list(vjp(tuple(cts)))
        if extra is not None:
            grads[0] = grads[0] + extra
        for kk in range(nr):
            douts[kk][...] = grads[kk]

        @pl.when(pl.program_id(0) == 0)
        def _():
            for kk in range(npar):
                douts[nr + kk][...] = jnp.zeros_like(douts[nr + kk])

        for kk in range(npar):
            douts[nr + kk][...] += grads[nr + kk]

    return pl.pallas_call(
        body, name=name, grid=(t // tr,),
        in_specs=[_row_spec(tr, r.shape[1]) for r in rows] + [_full_spec(p) for p in pars]
        + [_row_spec(tr, c.shape[1]) for c in cots],
        out_specs=[_row_spec(tr, r.shape[1]) for r in rows] + [_full_spec(p) for p in pars],
        out_shape=[jax.ShapeDtypeStruct(r.shape, F32) for r in rows]
        + [jax.ShapeDtypeStruct(p.shape, F32) for p in pars],
        compiler_params=_cp(("arbitrary",)))(*rows, *pars, *cots)


def _make_rw(fn, name, tr, nr, outs):
    @jax.custom_vjp
    def f(*args):
        return tuple(_rowwise(fn, args[:nr], args[nr:], outs, name=name + "_f", tr=tr))

    def fwd(*args):
        return f(*args), args

    def bwd(args, cts):
        return tuple(_rowwise_vjp(fn, args[:nr], args[nr:], list(cts), name=name + "_b", tr=tr))

    f.defvjp(fwd, bwd)
    return f


def _rms(x, g):
    return x * lax.rsqrt(jnp.mean(jnp.square(x), axis=-1, keepdims=True) + RMS_EPS) * g


def _f_rms(x, g):
    return (_rms(x, g),)


def _f_sgu(au, av, ng, w0, w1, w2, w3, bfull):
    u = jax.nn.gelu(au)
    v = _rms(jax.nn.gelu(av), ng)
    tri = lax.broadcasted_iota(jnp.int32, (SGU_CHUNK, SGU_CHUNK), 0) >= lax.broadcasted_iota(
        jnp.int32, (SGU_CHUNK, SGU_CHUNK), 1)
    head = lax.broadcasted_iota(jnp.int32, v.shape, 1) // HEAD_DIM
    mixed = bfull
    for h, w in enumerate((w0, w1, w2, w3)):
        mixed = mixed + _bdot(jnp.where(tri, w, 0.0), jnp.where(head == h, v, 0.0))
    return (u * mixed,)


def _f_s5disc(lam_re, lam_im, log_dt, b_re, b_im):
    dt = jnp.exp(log_dt)
    mag = jnp.exp(lam_re * dt)
    abar_re = mag * jnp.cos(lam_im * dt)
    abar_im = mag * jnp.sin(lam_im * dt)
    denom = jnp.square(lam_re) + jnp.square(lam_im)
    num_re = abar_re - 1.0
    num_im = abar_im
    fac_re = (num_re * lam_re + num_im * lam_im) / denom
    fac_im = (num_im * lam_re - num_re * lam_im) / denom
    return abar_re, abar_im, fac_re * b_re - fac_im * b_im, fac_re * b_im + fac_im * b_re


def _f_s5post(s_re, s_im, u, c_re, c_im, d, gw, gb):
    y = _bdot(s_re, c_re) - _bdot(s_im, c_im) + d * u
    y = jax.nn.gelu(y)
    return (y * jax.nn.sigmoid(_bdot(y, gw) + gb),)


def _f_lrupre(xc, wa, ba, wx, bx, lam):
    r = jax.nn.sigmoid(_bdot(xc, wa) + ba)
    i = jax.nn.sigmoid(_bdot(xc, wx) + bx)
    log_a = -LRU_C * r * jax.nn.softplus(-lam)
    a = jnp.exp(log_a)
    one_minus_a2 = -jnp.tanh(log_a) * (jnp.exp(2.0 * log_a) + 1.0)
    return a, jnp.sqrt(one_minus_a2) * (i * xc)


def _f_lrupost(h, gate):
    return (h * jax.nn.gelu(gate),)


def _f_logsig(zf, bf):
    return (jax.nn.log_sigmoid(zf + bf),)


def _f_gnorm(ya, yb, yc, yd, g):
    def n(y):
        return y * lax.rsqrt(jnp.mean(jnp.square(y), axis=-1, keepdims=True) + RMS_EPS)
    return (jnp.concatenate([n(ya), n(yb), n(yc), n(yd)], axis=1) * g,)


sgu_mix = _make_rw(_f_sgu, "sgu", SGU_CHUNK, 2, [MIXER_WIDTH])
s5_disc = _make_rw(_f_s5disc, "s5disc", S5_GROUPS * S5_GROUP, 5, [S5_STATE] * 4)
s5_post = _make_rw(_f_s5post, "s5post", 256, 3, [MIXER_WIDTH])
lru_pre = _make_rw(_f_lrupre, "lrupre", 512, 1, [MIXER_WIDTH, MIXER_WIDTH])
lru_post = _make_rw(_f_lrupost, "lrupost", 512, 2, [MIXER_WIDTH])
log_sig = _make_rw(_f_logsig, "logsig", 512, 1, [128])


SCAN_TILE = 512


def _prev_spec(c, nt, rev):
    per = SCAN_TILE // 8
    if rev:
        return pl.BlockSpec((8, c), lambda i: (jnp.maximum((nt - 1 - i) * per - 1, 0), 0))
    return pl.BlockSpec((8, c), lambda i: (jnp.maximum(i * per - 1, 0), 0))


SCAN_STEPS = (1, 2, 4)


def _cmul(ar, ai, br, bi):
    return ar * br - ai * bi, ar * bi + ai * br


def _rows_down(x, k, fill, rowid):
    return jnp.where(rowid >= k, pltpu.roll(x, k, 0), fill)


def _rows_up(x, k, fill, rowid):
    return jnp.where(rowid < 8 - k, pltpu.roll(x, 8 - k, 0), fill)


def _powers(ar, ai):
    pw = [(ar, ai)]
    for _ in range(7):
        pw.append(_cmul(*pw[-1], ar, ai))
    return pw


def _block(i):
    return pl.ds(pl.multiple_of(i * 8, 8), 8)


def _row_before(ref, i, edge):
    return jnp.where(i == 0, edge, ref[pl.ds(jnp.maximum(i * 8 - 1, 0), 1), :])


def _lti_fwd_call(u, w_re, w_im, a_re, a_im):
    t, kdim = u.shape
    c = w_re.shape[1]
    tt = SCAN_TILE

    def body(u_ref, wr_ref, wi_ref, ar_ref, ai_ref, sr_ref, si_ref, br_ref, bi_ref, cr, ci):
        @pl.when(pl.program_id(0) == 0)
        def _():
            cr[...] = jnp.zeros_like(cr)
            ci[...] = jnp.zeros_like(ci)

        br_ref[...] = _dot(u_ref[...], wr_ref[...], 1, 0)
        bi_ref[...] = _dot(u_ref[...], wi_ref[...], 1, 0)
        pw = _powers(ar_ref[...], ai_ref[...])
        apr = jnp.concatenate([p[0] for p in pw], axis=0)
        api = jnp.concatenate([p[1] for p in pw], axis=0)
        rowid = lax.broadcasted_iota(jnp.int32, (8, c), 0)

        def block(i, carry):
            xr, xi = br_ref[_block(i), :], bi_ref[_block(i), :]
            for k in SCAN_STEPS:
                dr, di = _cmul(*pw[k - 1], _rows_down(xr, k, 0.0, rowid), _rows_down(xi, k, 0.0, rowid))
                xr, xi = xr + dr, xi + di
            dr, di = _cmul(apr, api, *carry)
            xr, xi = xr + dr, xi + di
            sr_ref[_block(i), :] = xr
            si_ref[_block(i), :] = xi
            return xr[7:8, :], xi[7:8, :]

        hr, hi = lax.fori_loop(0, tt // 8, block, (cr[...], ci[...]), unroll=2)
        cr[...] = hr
        ci[...] = hi

    row = pl.BlockSpec((tt, c), lambda i: (i, 0))
    par = pl.BlockSpec((1, c), lambda i: (0, 0))
    return pl.pallas_call(
        body, name="lti_scan_f", grid=(t // tt,),
        in_specs=[pl.BlockSpec((tt, kdim), lambda i: (i, 0)), _full_spec(w_re), _full_spec(w_im), par, par],
        out_specs=[row, row], out_shape=[jax.ShapeDtypeStruct((t, c), F32)] * 2,
        scratch_shapes=[pltpu.VMEM((tt, c), F32)] * 2 + [pltpu.VMEM((1, c), F32)] * 2,
        compiler_params=_cp(("arbitrary",)))(u, w_re, w_im, a_re, a_im)


def _lti_bwd_call(u, w_re, w_im, a_re, a_im, s_re, s_im, g_re, g_im):
    t, c = g_re.shape
    kdim = u.shape[1]
    tt = SCAN_TILE
    nt = t // tt
    nb = tt // 8

    def body(u_ref, wr_ref, wi_ref, ar_ref, ai_ref, sr_ref, si_ref, pr_ref, pi_ref, gr_ref, gi_ref,
             du_ref, dwr_ref, dwi_ref, dar_ref, dai_ref, or_ref, oi_ref, cr, ci):
        ti = pl.program_id(0)

        @pl.when(ti == 0)
        def _():
            cr[...] = jnp.zeros_like(cr)
            ci[...] = jnp.zeros_like(ci)
            dar_ref[...] = jnp.zeros_like(dar_ref)
            dai_ref[...] = jnp.zeros_like(dai_ref)
            dwr_ref[...] = jnp.zeros_like(dwr_ref)
            dwi_ref[...] = jnp.zeros_like(dwi_ref)

        pw = _powers(ar_ref[...], -ai_ref[...])
        tpr = jnp.concatenate([p[0] for p in reversed(pw)], axis=0)
        tpi = jnp.concatenate([p[1] for p in reversed(pw)], axis=0)
        rowid = lax.broadcasted_iota(jnp.int32, (8, c), 0)
        first = ti == nt - 1
        edge_r = jnp.where(first, 0.0, pr_ref[7:8, :])
        edge_i = jnp.where(first, 0.0, pi_ref[7:8, :])

        def block(kk, carry):
            i = nb - 1 - kk
            gr_c, gi_c, acc_r, acc_i = carry
            xr, xi = gr_ref[_block(i), :], gi_ref[_block(i), :]
            for k in SCAN_STEPS:
                dr, di = _cmul(*pw[k - 1], _rows_up(xr, k, 0.0, rowid), _rows_up(xi, k, 0.0, rowid))
                xr, xi = xr + dr, xi + di
            dr, di = _cmul(tpr, tpi, gr_c, gi_c)
            xr, xi = xr + dr, xi + di
            or_ref[_block(i), :] = xr
            oi_ref[_block(i), :] = xi
            spr = _rows_down(sr_ref[_block(i), :], 1, _row_before(sr_ref, i, edge_r), rowid)
            spi = _rows_down(si_ref[_block(i), :], 1, _row_before(si_ref, i, edge_i), rowid)
            return xr[0:1, :], xi[0:1, :], acc_r + spr * xr + spi * xi, acc_i + spr * xi - spi * xr

        zero = jnp.zeros((8, c), F32)
        gr_c, gi_c, acc_r, acc_i = lax.fori_loop(0, nb, block, (cr[...], ci[...], zero, zero), unroll=2)
        cr[...] = gr_c
        ci[...] = gi_c
        dar_ref[...] += jnp.sum(acc_r, axis=0, keepdims=True)
        dai_ref[...] += jnp.sum(acc_i, axis=0, keepdims=True)
        du_ref[...] = _dot(or_ref[...], wr_ref[...], 1, 1) + _dot(oi_ref[...], wi_ref[...], 1, 1)
        dwr_ref[...] += _dot(u_ref[...], or_ref[...], 0, 0)
        dwi_ref[...] += _dot(u_ref[...], oi_ref[...], 0, 0)

    row = pl.BlockSpec((tt, c), lambda i: (nt - 1 - i, 0))
    row_u = pl.BlockSpec((tt, kdim), lambda i: (nt - 1 - i, 0))
    par = pl.BlockSpec((1, c), lambda i: (0, 0))
    prev = _prev_spec(c, nt, True)
    return pl.pallas_call(
        body, name="lti_scan_b", grid=(nt,),
        in_specs=[row_u, _full_spec(w_re), _full_spec(w_im), par, par, row, row, prev, prev, row, row],
        out_specs=[row_u, _full_spec(w_re), _full_spec(w_im), par, par],
        out_shape=[jax.ShapeDtypeStruct((t, kdim), F32), jax.ShapeDtypeStruct(w_re.shape, F32),
                   jax.ShapeDtypeStruct(w_im.shape, F32)] + [jax.ShapeDtypeStruct((1, c), F32)] * 2,
        scratch_shapes=[pltpu.VMEM((tt, c), F32)] * 2 + [pltpu.VMEM((1, c), F32)] * 2,
        compiler_params=_cp(("arbitrary",)))(u, w_re, w_im, a_re, a_im, s_re, s_im, s_re, s_im, g_re, g_im)


@jax.custom_vjp
def lti_scan(u, w_re, w_im, a_re, a_im):
    return tuple(_lti_fwd_call(u, w_re, w_im, a_re, a_im))


def _lti_scan_fwd(u, w_re, w_im, a_re, a_im):
    s_re, s_im = _lti_fwd_call(u, w_re, w_im, a_re, a_im)
    return (s_re, s_im), (u, w_re, w_im, a_re, a_im, s_re, s_im)


def _lti_scan_bwd(r, g):
    return tuple(_lti_bwd_call(*r, g[0], g[1]))


lti_scan.defvjp(_lti_scan_fwd, _lti_scan_bwd)


def _tv_fwd_call(a, b):
    t, c = b.shape
    tt = SCAN_TILE

    def body(a_ref, b_ref, h_ref, ch):
        @pl.when(pl.program_id(0) == 0)
        def _():
            ch[...] = jnp.zeros_like(ch)

        rowid = lax.broadcasted_iota(jnp.int32, (8, c), 0)

        def block(i, h):
            ab, x = a_ref[_block(i), :], b_ref[_block(i), :]
            for k in SCAN_STEPS:
                x = x + ab * _rows_down(x, k, 0.0, rowid)
                ab = ab * _rows_down(ab, k, 1.0, rowid)
            x = x + ab * h
            h_ref[_block(i), :] = x
            return x[7:8, :]

        ch[...] = lax.fori_loop(0, tt // 8, block, ch[...], unroll=2)

    row = pl.BlockSpec((tt, c), lambda i: (i, 0))
    return pl.pallas_call(
        body, name="tv_scan_f", grid=(t // tt,), in_specs=[row, row], out_specs=row,
        out_shape=jax.ShapeDtypeStruct((t, c), F32), scratch_shapes=[pltpu.VMEM((1, c), F32)],
        compiler_params=_cp(("arbitrary",)))(a, b)


def _tv_bwd_call(a, h, g):
    t, c = g.shape
    tt = SCAN_TILE
    nt = t // tt
    nb = tt // 8

    def body(a_ref, h_ref, p_ref, g_ref, da_ref, db_ref, cg, ca):
        ti = pl.program_id(0)

        @pl.when(ti == 0)
        def _():
            cg[...] = jnp.zeros_like(cg)
            ca[...] = jnp.zeros_like(ca)

        rowid = lax.broadcasted_iota(jnp.int32, (8, c), 0)
        edge = jnp.where(ti == nt - 1, 0.0, p_ref[7:8, :])

        def block(kk, carry):
            i = nb - 1 - kk
            gc, a_next = carry
            ab, x = a_ref[_block(i), :], g_ref[_block(i), :]
            cb = _rows_up(ab, 1, a_next, rowid)
            for k in SCAN_STEPS:
                x = x + cb * _rows_up(x, k, 0.0, rowid)
                cb = cb * _rows_up(cb, k, 1.0, rowid)
            x = x + cb * gc
            db_ref[_block(i), :] = x
            da_ref[_block(i), :] = x * _rows_down(h_ref[_block(i), :], 1, _row_before(h_ref, i, edge), rowid)
            return x[0:1, :], ab[0:1, :]

        gc, a_next = lax.fori_loop(0, nb, block, (cg[...], ca[...]), unroll=2)
        cg[...] = gc
        ca[...] = a_next

    row = pl.BlockSpec((tt, c), lambda i: (nt - 1 - i, 0))
    return pl.pallas_call(
        body, name="tv_scan_b", grid=(nt,), in_specs=[row, row, _prev_spec(c, nt, True), row],
        out_specs=[row, row], out_shape=[jax.ShapeDtypeStruct((t, c), F32)] * 2,
        scratch_shapes=[pltpu.VMEM((1, c), F32)] * 2, compiler_params=_cp(("arbitrary",)))(a, h, h, g)


@jax.custom_vjp
def tv_scan(a, b):
    return _tv_fwd_call(a, b)


def _tv_scan_fwd(a, b):
    h = _tv_fwd_call(a, b)
    return h, (a, h)


def _tv_scan_bwd(r, g):
    a, h = r
    return tuple(_tv_bwd_call(a, h, g))


tv_scan.defvjp(_tv_scan_fwd, _tv_scan_bwd)


CONV_K = 4
CONV_ROWS = 512


def _conv_fwd_call(x, w, b):
    t, c = x.shape

    def body(x_ref, w_ref, b_ref, o_ref, xp):
        xp[0:8, :] = jnp.zeros((8, c), F32)
        xp[8:, :] = x_ref[...]
        for blk in range(t // CONV_ROWS):
            base = blk * CONV_ROWS
            acc = jnp.broadcast_to(b_ref[...], (CONV_ROWS, c))
            for kk in range(CONV_K):
                acc = acc + w_ref[kk:kk + 1, :] * xp[base + 5 + kk:base + 5 + kk + CONV_ROWS, :]
            o_ref[base:base + CONV_ROWS, :] = acc

    return pl.pallas_call(
        body, name="conv_f", out_shape=jax.ShapeDtypeStruct((t, c), F32),
        scratch_shapes=[pltpu.VMEM((t + 8, c), F32)], compiler_params=_cp())(x, w, b)


def _conv_bwd_call(x, w, g):
    t, c = x.shape

    def body(x_ref, w_ref, g_ref, dx_ref, dw_ref, db_ref, xp, gp):
        xp[0:8, :] = jnp.zeros((8, c), F32)
        xp[8:, :] = x_ref[...]
        gp[0:t, :] = g_ref[...]
        gp[t:, :] = jnp.zeros((8, c), F32)
        dw = [jnp.zeros((1, c), F32) for _ in range(CONV_K)]
        db = jnp.zeros((1, c), F32)
        for blk in range(t // CONV_ROWS):
            base = blk * CONV_ROWS
            gb = g_ref[base:base + CONV_ROWS, :]
            acc = jnp.zeros((CONV_ROWS, c), F32)
            for kk in range(CONV_K):
                acc = acc + w_ref[kk:kk + 1, :] * gp[base + 3 - kk:base + 3 - kk + CONV_ROWS, :]
                dw[kk] = dw[kk] + jnp.sum(gb * xp[base + 5 + kk:base + 5 + kk + CONV_ROWS, :], axis=0, keepdims=True)
            db = db + jnp.sum(gb, axis=0, keepdims=True)
            dx_ref[base:base + CONV_ROWS, :] = acc
        for kk in range(CONV_K):
            dw_ref[kk:kk + 1, :] = dw[kk]
        db_ref[...] = db

    return pl.pallas_call(
        body, name="conv_b",
        out_shape=[jax.ShapeDtypeStruct((t, c), F32), jax.ShapeDtypeStruct((CONV_K, c), F32),
                   jax.ShapeDtypeStruct((1, c), F32)],
        scratch_shapes=[pltpu.VMEM((t + 8, c), F32)] * 2, compiler_params=_cp())(x, w, g)


@jax.custom_vjp
def causal_conv(x, w, b):
    return _conv_fwd_call(x, w, b)


def _causal_conv_fwd(x, w, b):
    return _conv_fwd_call(x, w, b), (x, w)


def _causal_conv_bwd(r, g):
    return tuple(_conv_bwd_call(r[0], r[1], g))


causal_conv.defvjp(_causal_conv_fwd, _causal_conv_bwd)


ATT_TILE = 512
ATT_SCALE = HEAD_DIM ** -0.5


def _head_lane(val, lane, h):
    return jnp.sum(jnp.where(lane == h, val, 0.0), axis=1, keepdims=True)


def _attn_fwd_call(q, k, v, c128, cr, next_shard=None):
    t, w = q.shape
    tq = ATT_TILE
    nq = t // tq
    k3, v3, cr4 = k.reshape(nq, tq, w), v.reshape(nq, tq, w), cr.reshape(N_HEADS, nq, 1, tq)
    fused = next_shard is not None

    def body(*refs):
        q_ref, k_ref, v_ref, c_ref, cr_ref = refs[:5]
        i, h = pl.program_id(0), pl.program_id(1)
        if fused:
            shard_ref, o_ref, lse_ref, gathered_ref, send_sems, recv_sems = refs[5:]
            start, finish = _allgather_copies(shard_ref, gathered_ref, send_sems, recv_sems)
            pl.when((i == 0) & (h == 0))(start)
        else:
            o_ref, lse_ref = refs[5:]
        hm = lax.broadcasted_iota(jnp.int32, (tq, w), 1) // HEAD_DIM == h
        lane = lax.broadcasted_iota(jnp.int32, (tq, 128), 1)
        qs = jnp.where(hm, q_ref[...] * ATT_SCALE, 0.0)
        cq = _head_lane(c_ref[...], lane, h)
        causal = lax.broadcasted_iota(jnp.int32, (tq, tq), 0) >= lax.broadcasted_iota(jnp.int32, (tq, tq), 1)

        def update(j, carry, diagonal):
            m, l, acc = carry
            s = _dot(qs, k_ref[j], 1, 1) + cq - cr_ref[0, j]
            if diagonal:
                s = jnp.where(causal, s, NEG)
            m_new = jnp.maximum(m, jnp.max(s, axis=1, keepdims=True))
            p = jnp.exp(s - m_new)
            alpha = jnp.exp(m - m_new)
            return m_new, alpha * l + jnp.sum(p, axis=1, keepdims=True), alpha * acc + _dot(p, v_ref[j], 1, 0)

        init = (jnp.full((tq, 1), NEG, F32), jnp.zeros((tq, 1), F32), jnp.zeros((tq, w), F32))
        carry = lax.fori_loop(0, i, lambda j, c: update(j, c, False), init)
        m, l, acc = update(i, carry, True)
        out = jnp.where(hm, acc / l, 0.0)
        lse = jnp.where(lane == h, m + jnp.log(l), 0.0)

        @pl.when(h == 0)
        def _():
            o_ref[...] = out
            lse_ref[...] = lse

        @pl.when(h > 0)
        def _():
            o_ref[...] += out
            lse_ref[...] += lse

        if fused:
            pl.when((i == nq - 1) & (h == N_HEADS - 1))(finish)

    tile = pl.BlockSpec((tq, w), lambda i, h: (i, 0))
    tile_c = pl.BlockSpec((tq, 128), lambda i, h: (i, 0))
    whole = pl.BlockSpec((nq, tq, w), lambda i, h: (0, 0, 0))
    rows = pl.BlockSpec((1, nq, 1, tq), lambda i, h: (h, 0, 0, 0))
    ins = [q.astype(BF16), k3.astype(BF16), v3.astype(BF16), c128, cr4]
    in_specs, out_specs = [tile, whole, whole, tile_c, rows], [tile, tile_c]
    out_shape = [jax.ShapeDtypeStruct((t, w), F32), jax.ShapeDtypeStruct((t, 128), F32)]
    scratch = []
    if fused:
        ins.append(next_shard)
        in_specs.append(HBM)
        out_specs.append(HBM)
        out_shape.append(jax.ShapeDtypeStruct((4,) + next_shard.shape, next_shard.dtype))
        scratch = [pltpu.SemaphoreType.DMA((AG_SEMS,)), pltpu.SemaphoreType.DMA((AG_SEMS,))]
    return pl.pallas_call(
        body, name="attn_f_allgather" if fused else "attn_f", grid=(nq, N_HEADS), in_specs=in_specs,
        out_specs=out_specs, out_shape=out_shape, scratch_shapes=scratch,
        compiler_params=_cp(("arbitrary", "arbitrary")))(*ins)


def _attn_bwd_call(q, k, v, c128, cr, o, lse, do, exchange=None):
    t, w = q.shape
    tq = ATT_TILE
    nq = t // tq
    r3 = lambda a: a.reshape(nq, tq, a.shape[-1])
    cr4 = cr.reshape(N_HEADS, nq, 1, tq)
    fused = exchange is not None

    def body(*refs):
        q_ref, k_ref, v_ref, c_ref, cr_ref, o_ref, lse_ref, do_ref = refs[:8]
        j, h = pl.program_id(0), pl.program_id(1)
        if fused:
            p_ref, dq_ref, dk_ref, dv_ref, dc_ref, dcr_ref, recv_ref, send_sems, recv_sems = refs[8:]
            start, finish = _chip_exchange_copies(p_ref, recv_ref, send_sems, recv_sems)
            pl.when((j == 0) & (h == 0))(start)
        else:
            dq_ref, dk_ref, dv_ref, dc_ref, dcr_ref = refs[8:]

        @pl.when((j == 0) & (h == 0))
        def _():
            dq_ref[...] = jnp.zeros_like(dq_ref)
            dc_ref[...] = jnp.zeros_like(dc_ref)

        hm = lax.broadcasted_iota(jnp.int32, (tq, w), 1) // HEAD_DIM == h
        lane = lax.broadcasted_iota(jnp.int32, (tq, 128), 1)
        kj = k_ref[...]
        vj = v_ref[...]
        ck = cr_ref[0, 0]
        causal = lax.broadcasted_iota(jnp.int32, (tq, tq), 0) >= lax.broadcasted_iota(jnp.int32, (tq, tq), 1)

        def step(i, carry, diagonal):
            dk, dv, dck = carry
            qm = jnp.where(hm, q_ref[i], 0.0)
            dom = jnp.where(hm, do_ref[i], 0.0)
            s = _dot(qm * ATT_SCALE, kj, 1, 1) + _head_lane(c_ref[i], lane, h) - ck
            if diagonal:
                s = jnp.where(causal, s, NEG)
            p = jnp.exp(s - _head_lane(lse_ref[i], lane, h))
            dv = dv + _dot(p, dom, 0, 0)
            dp = _dot(dom, vj, 1, 1)
            delta = jnp.sum(dom * o_ref[i], axis=1, keepdims=True)
            ds = p * (dp - delta)
            dq_ref[i] += jnp.where(hm, _dot(ds, kj, 1, 0), 0.0) * ATT_SCALE
            dk = dk + _dot(ds, qm, 0, 0) * ATT_SCALE
            dc_ref[i] += jnp.where(lane == h, jnp.sum(ds, axis=1, keepdims=True), 0.0)
            return dk, dv, dck - jnp.sum(ds, axis=0, keepdims=True)

        init = (jnp.zeros((tq, w), F32), jnp.zeros((tq, w), F32), jnp.zeros((1, tq), F32))
        carry = step(j, init, True)
        dk, dv, dck = lax.fori_loop(j + 1, nq, lambda i, c: step(i, c, False), carry)
        dcr_ref[0, 0] = dck

        @pl.when(h == 0)
        def _():
            dk_ref[...] = dk
            dv_ref[...] = dv

        @pl.when(h > 0)
        def _():
            dk_ref[...] += dk
            dv_ref[...] += dv

        if fused:
            pl.when((j == nq - 1) & (h == N_HEADS - 1))(finish)

    whole = pl.BlockSpec((nq, tq, w), lambda j, h: (0, 0, 0))
    whole_c = pl.BlockSpec((nq, tq, 128), lambda j, h: (0, 0, 0))
    tile = pl.BlockSpec((None, tq, w), lambda j, h: (j, 0, 0))
    tile_r = pl.BlockSpec((1, 1, 1, tq), lambda j, h: (h, j, 0, 0))
    s3 = jax.ShapeDtypeStruct((nq, tq, w), F32)
    b16 = lambda a: r3(a).astype(BF16)
    ins = [b16(q), b16(k), b16(v), r3(c128), cr4, r3(o), r3(lse), r3(do)]
    in_specs = [whole, tile, tile, whole_c, tile_r, whole, whole_c, whole]
    out_specs = [whole, tile, tile, whole_c, tile_r]
    out_shape = [s3, s3, s3, jax.ShapeDtypeStruct((nq, tq, 128), F32), jax.ShapeDtypeStruct((N_HEADS, nq, 1, tq), F32)]
    scratch = []
    if fused:
        ins.append(exchange)
        in_specs.append(HBM)
        out_specs.append(HBM)
        out_shape.append(jax.ShapeDtypeStruct((3,) + exchange.shape[1:], exchange.dtype))
        scratch = [pltpu.SemaphoreType.DMA((3,)), pltpu.SemaphoreType.DMA((3,))]
    dq, dk, dv, dc, dcr, *received = pl.pallas_call(
        body, name="attn_b_exchange" if fused else "attn_b", grid=(nq, N_HEADS), in_specs=in_specs,
        out_specs=out_specs, out_shape=out_shape, scratch_shapes=scratch,
        compiler_params=_cp(("arbitrary", "arbitrary")))(*ins)
    grads = (dq.reshape(t, w), dk.reshape(t, w), dv.reshape(t, w), dc.reshape(t, 128), dcr.reshape(N_HEADS, 1, t))
    return grads, (received[0] if fused else None)


def _loss_call(x, g, target):
    t, d = x.shape
    tr = 512

    def body(x_ref, g_ref, t_ref, loss_ref, dx_ref, dg_ref):
        tgt = t_ref[...]

        def f(xv, gv):
            return 0.5 * jnp.sum(jnp.mean(jnp.square(_rms(xv, gv) - tgt), axis=-1))

        val, vjp = jax.vjp(f, x_ref[...], g_ref[...])
        dx, dg = vjp(jnp.ones((), F32))
        dx_ref[...] = dx

        @pl.when(pl.program_id(0) == 0)
        def _():
            loss_ref[...] = jnp.zeros_like(loss_ref)
            dg_ref[...] = jnp.zeros_like(dg_ref)

        loss_ref[...] += jnp.full(loss_ref.shape, val, F32)
        dg_ref[...] += dg

    row = _row_spec(tr, d)
    return pl.pallas_call(
        body, name="loss_head", grid=(t // tr,), in_specs=[row, _full_spec(g), row],
        out_specs=[pl.BlockSpec((1, 128), lambda i: (0, 0)), row, _full_spec(g)],
        out_shape=[jax.ShapeDtypeStruct((1, 128), F32), jax.ShapeDtypeStruct((t, d), F32),
                   jax.ShapeDtypeStruct(g.shape, F32)],
        compiler_params=_cp(("arbitrary",)))(x, g, target)


def _blockdiag(w):
    g, a, b = w.shape
    return jnp.einsum('gab,gk->gakb', w, jnp.eye(g, dtype=w.dtype)).reshape(g * a, g * b)


def _s5_params(p):
    rep = lambda a: jnp.repeat(a, S5_GROUP, axis=0)
    rows = S5_GROUPS * S5_GROUP
    bt = lambda b: b.transpose(0, 2, 1).reshape(rows, S5_STATE)
    abar_re, abar_im, bb_re, bb_im = s5_disc(
        rep(p["s5_lambda_re"]), rep(p["s5_lambda_im"]), rep(p["s5_log_dt"][:, None]), bt(p["s5_b_re"]), bt(p["s5_b_im"]))
    first = lambda a: a.reshape(S5_GROUPS, S5_GROUP, S5_STATE)[:, 0, :].reshape(1, S5_GROUPS * S5_STATE)
    g3 = lambda a: a.reshape(S5_GROUPS, S5_GROUP, S5_STATE)
    cblk = lambda c: _blockdiag(c.transpose(0, 2, 1))
    return (first(abar_re), first(abar_im), _blockdiag(g3(bb_re)), _blockdiag(g3(bb_im)),
            cblk(p["s5_c_re"]), cblk(p["s5_c_im"]))


MIXER_WEIGHTS = ('sgu_norm_g', 'sgu_w', 'sgu_b', 's5_lambda_re', 's5_lambda_im', 's5_log_dt', 's5_b_re', 's5_b_im',
                 's5_c_re', 's5_c_im', 's5_d', 's5_glu_w', 's5_glu_b', 'lru_conv_w', 'lru_conv_b', 'lru_wa', 'lru_ba',
                 'lru_wx', 'lru_bx', 'lru_lambda', 'fox_fgate_b')


def _mixers_pre(z, p):
    w = MIXER_WIDTH
    row = lambda a: a[None, :]
    a_u, a_v, b_in, c_x, c_gate, d_q, d_k, d_v, d_f = jnp.split(z, [w * i for i in range(1, 9)], axis=1)
    sw = p["sgu_w"]
    (y_a,) = sgu_mix(a_u, a_v, row(p["sgu_norm_g"]), sw[0], sw[1], sw[2], sw[3],
                     jnp.repeat(p["sgu_b"].T, HEAD_DIM, axis=1))
    abar_re, abar_im, bblk_re, bblk_im, cblk_re, cblk_im = _s5_params(p)
    s_re, s_im = lti_scan(b_in, bblk_re, bblk_im, abar_re, abar_im)
    (y_b,) = s5_post(s_re, s_im, b_in, cblk_re, cblk_im, row(p["s5_d"]), p["s5_glu_w"], row(p["s5_glu_b"]))
    xc = causal_conv(c_x, p["lru_conv_w"], row(p["lru_conv_b"]))
    a, b = lru_pre(xc, _blockdiag(p["lru_wa"]), p["lru_ba"].reshape(1, w), _blockdiag(p["lru_wx"]),
                   p["lru_bx"].reshape(1, w), row(p["lru_lambda"]))
    (y_c,) = lru_post(tv_scan(a, b), c_gate)
    (log_f,) = log_sig(d_f, jnp.pad(p["fox_fgate_b"], (0, 128 - N_HEADS))[None, :])
    c128 = tv_scan(jnp.ones_like(log_f), log_f)
    return y_a, y_b, y_c, d_q, d_k, d_v, c128, c128[:, :N_HEADS].T[:, None, :]


PACK_COLS = 1024
SHARD_SHAPE = {'w_mlp_in': (1024, 1024), 'w_mlp_out': (1024, 1024), 'w_out': (256, 1024), 'w_in': (1024, 513),
               's5_glu_w': (64, 256), 'lru_conv_w': (4, 64)}
SHARDED_AXIS = {'w_in': 1, 's5_glu_w': 0, 'lru_conv_w': 1, 'w_out': 0, 'w_mlp_in': 1, 'w_mlp_out': 0}
SHARD_ROWS = {n: -(-s[0] * s[1] // PACK_COLS) for n, s in SHARD_SHAPE.items()}
SHARD_OFF = {n: sum(list(SHARD_ROWS.values())[:i]) for i, n in enumerate(SHARD_SHAPE)}
LAYER_ROWS = 2880
SMALL_OFF = SHARD_OFF['w_in']
GLUE_ROWS = LAYER_ROWS - SMALL_OFF
assert SHARD_OFF['w_mlp_out'] == 1024 and SHARD_OFF['w_out'] == 2048 and SMALL_OFF % GLUE_ROWS == 0
assert SHARD_OFF['lru_conv_w'] + SHARD_ROWS['lru_conv_w'] <= LAYER_ROWS
PACK_ROWS = DEPTH * LAYER_ROWS
TOK = 1024
FF = 4 * D_MODEL


def _w3(i_of):
    return pl.BlockSpec((None, 1024, PACK_COLS), i_of)


def _tile2(rows, cols, i_of):
    return pl.BlockSpec((rows, cols), i_of)


def _layer_fwd(x, gathered, w_in, w_out, p, mix_p, next_shard):
    t = x.shape[0]
    nt = t // TOK
    f32 = lambda r, c: jax.ShapeDtypeStruct((r, c), F32)
    b16 = lambda r, c: jax.ShapeDtypeStruct((r, c), BF16)
    g1, g2, gm = p["norm1_g"][None, :], p["norm2_g"][None, :], p["mix_norm_g"][None, :]
    (h1,) = _rowwise(_f_rms, [x], [g1], [D_MODEL], name="rms_f", tr=512, dtype=BF16)
    z = _matmul("mm_in", (nt, 1, 1), h1, _tile2(TOK, D_MODEL, lambda i, j, k: (i, 0)),
                w_in, _tile2(D_MODEL, D_IN_PAD, lambda i, j, k: (0, 0)), (1, 0),
                [(f32(t, D_IN_PAD), _tile2(TOK, D_IN_PAD, lambda i, j, k: (i, 0)))])
    (y_a, y_b, y_c, *attn_in), mix_vjp = jax.vjp(_mixers_pre, z, mix_p)
    y_d, lse, *next_gathered = _attn_fwd_call(*attn_in, next_shard=next_shard)
    ys = (y_a, y_b, y_c, y_d)
    (yn,) = _rowwise(_f_gnorm, list(ys), [gm], [D_MODEL], name="gnorm_f", tr=512, dtype=BF16)
    x_tile = _tile2(TOK, D_MODEL, lambda i, j, k: (i, 0))
    x1 = _matmul("mm_out", (nt, 1, 1), yn, x_tile, w_out, _tile2(D_MODEL, D_MODEL, lambda i, j, k: (0, 0)), (1, 0),
                 [(f32(t, D_MODEL), x_tile)], extras=[(x, x_tile)], epilogue=lambda acc, r: (acc + r,))
    (h2,) = _rowwise(_f_rms, [x1], [g2], [D_MODEL], name="rms_f", tr=512, dtype=BF16)
    ff_tile = _tile2(TOK, 1024, lambda i, j, k: (i, j))

    act = _matmul("mm_up", (nt, FF // 1024, 1), h2, x_tile, gathered, _w3(lambda i, j, k: (j, 0, 0)), (1, 0),
                  [(b16(t, FF), ff_tile)], epilogue=lambda acc: (jnp.square(jnp.maximum(acc, 0.0)),))
    x2 = _matmul("mm_down", (nt, 1, FF // 1024), act, _tile2(TOK, 1024, lambda i, j, k: (i, k)),
                 gathered, _w3(lambda i, j, k: (k, 1, 0)), (1, 0),
                 [(f32(t, D_MODEL), x_tile)], extras=[(x1, x_tile)], epilogue=lambda acc, r: (acc + r,))
    res = (x, h1, mix_vjp, ys, attn_in, lse, yn, x1, h2, act)
    return x2, res, (next_gathered[0] if next_gathered else None)


def _layer_bwd(g, res, gathered, w_in, w_out, p, exchange):
    x, h1, mix_vjp, ys, attn_in, lse, yn, x1, h2, act = res
    send = lax.empty((4, LAYER_ROWS, PACK_COLS), BF16)
    t = x.shape[0]
    nt = t // TOK
    f32 = lambda r, c: jax.ShapeDtypeStruct((r, c), F32)
    g1, g2, gm = p["norm1_g"][None, :], p["norm2_g"][None, :], p["mix_norm_g"][None, :]
    x_tile = _tile2(TOK, D_MODEL, lambda i, j, k: (i, 0))
    ff_tile = _tile2(TOK, 1024, lambda i, j, k: (i, j))
    tok_k = _tile2(TOK, D_MODEL, lambda i, j, k: (k, 0))
    send_s = jax.ShapeDtypeStruct(send.shape, send.dtype)
    du = _matmul("mm_down_dx", (nt, FF // 1024, 1), g, x_tile, gathered, _w3(lambda i, j, k: (j, 1, 0)), (1, 1),
                 [(jax.ShapeDtypeStruct((t, FF), BF16), ff_tile)], extras=[(act, ff_tile)],
                 epilogue=lambda acc, a: (2.0 * jnp.sqrt(a.astype(F32)) * acc,))
    send = _matmul("mm_down_dw", (FF // 1024, 1, nt), act, _tile2(TOK, 1024, lambda i, j, k: (k, i)), g, tok_k, (0, 0),
                   [(send_s, _w3(lambda i, j, k: (i, 1, 0)))], into=send)
    send = _matmul("mm_up_dw", (1, FF // 1024, nt), h2, tok_k, du, _tile2(TOK, 1024, lambda i, j, k: (k, j)), (0, 0),
                   [(send_s, _w3(lambda i, j, k: (j, 0, 0)))], into=send)
    gain = _tile2(1, D_MODEL, lambda i, j, k: (0, 0))

    def norm_bwd(dh, xv, gv, through):
        _, vjp = jax.vjp(_rms, xv, gv)
        dxv, dgv = vjp(dh)
        return dxv + through, dgv

    g_mid, dg2 = _matmul("mm_up_dx", (nt, 1, FF // 1024), du, _tile2(TOK, 1024, lambda i, j, k: (i, k)),
                         gathered, _w3(lambda i, j, k: (k, 0, 0)), (1, 1),
                         [(f32(t, D_MODEL), x_tile), (f32(1, D_MODEL), gain)],
                         extras=[(x1, x_tile), (g2, gain), (g, x_tile)], epilogue=norm_bwd, summed=1)
    w_full = _tile2(D_MODEL, D_MODEL, lambda i, j, k: (0, 0))
    dyn = _matmul("mm_out_dx", (nt, 1, 1), g_mid, x_tile, w_out, w_full, (1, 1), [(f32(t, D_MODEL), x_tile)])
    send = _matmul("mm_out_dw", (4, 1, nt), yn, _tile2(TOK, 256, lambda i, j, k: (k, i)), g_mid, tok_k, (0, 0),
                   [(send_s, pl.BlockSpec((None, 256, PACK_COLS),
                                          lambda i, j, k: (i, SHARD_OFF['w_out'] // 256, 0)))], into=send)
    dy_a, dy_b, dy_c, dy_d, dgm = _rowwise_vjp(_f_gnorm, list(ys), [gm], [dyn], name="gnorm_b", tr=512)
    d_attn_in, received = _attn_bwd_call(*attn_in, ys[3], lse, dy_d, exchange=exchange)
    dz, dmix = mix_vjp((dy_a, dy_b, dy_c, *d_attn_in))
    dz = dz.astype(BF16)
    z_tile = _tile2(TOK, D_IN_PAD, lambda i, j, k: (i, 0))
    d_w_in = _matmul("mm_in_dw", (1, 1, t // 512), h1, _tile2(512, D_MODEL, lambda i, j, k: (k, 0)),
                     dz, _tile2(512, D_IN_PAD, lambda i, j, k: (k, 0)), (0, 0),
                     [(f32(D_MODEL, D_IN_PAD), _tile2(D_MODEL, D_IN_PAD, lambda i, j, k: (0, 0)))])
    dx, dg1 = _matmul("mm_in_dx", (nt, 1, 1), dz, z_tile, w_in, _tile2(D_MODEL, D_IN_PAD, lambda i, j, k: (0, 0)), (1, 1),
                      [(f32(t, D_MODEL), x_tile), (f32(1, D_MODEL), gain)],
                      extras=[(x, x_tile), (g1, gain), (g_mid, x_tile)], epilogue=norm_bwd, summed=1)
    small = dict(dmix, norm1_g=dg1[0], norm2_g=dg2[0], mix_norm_g=dgm[0])
    return dx, small, d_w_in, send, received


HBM = pl.BlockSpec(memory_space=pltpu.HBM)
D2D_CHUNKS = 15
ICI_CHUNKS = 5
VMEM_CHUNKS = 4


def _coords():
    return lax.axis_index("x"), lax.axis_index("y"), lax.axis_index("c")


def _other_chips(x, y):
    return [(1 - x, y), (x, 1 - y), (1 - x, 1 - y)]


def _start_chunks(make, rows, n):
    size = rows // n
    assert size * n == rows
    for k in range(n):
        make(pl.ds(k * size, size)).start()


AG_SEMS = 7


def _allgather_copies(in_ref, out_ref, send_sems, recv_sems):
    r = in_ref.shape[0]
    rh = r // 2
    x, y, c = _coords()
    me, sibling = (x, y, c), (x, y, 1 - c)
    chips = _other_chips(x, y)

    def half(px, py, pc, rows=pl.ds(0, rh)):
        return out_ref.at[2 * px + py, pl.ds(pc * rh + rows.start, rows.size), :]

    def copy(k, block, to, rows=pl.ds(0, rh), from_input=False):
        src = in_ref.at[pl.ds(block[2] * rh + rows.start, rows.size), :] if from_input else half(*block, rows)
        return pltpu.make_async_remote_copy(
            src_ref=src, dst_ref=half(*block, rows), send_sem=send_sems.at[k], recv_sem=recv_sems.at[k],
            device_id=to, device_id_type=MESH)

    def own(rows=pl.ds(0, r)):
        return pltpu.make_async_remote_copy(
            src_ref=in_ref.at[rows, :], dst_ref=out_ref.at[2 * x + y, rows, :], send_sem=send_sems.at[6],
            recv_sem=recv_sems.at[6], device_id=sibling, device_id_type=MESH)

    def start():
        for j, chip in enumerate(chips):
            _start_chunks(lambda rows: copy(j, me, (*chip, c), rows, from_input=True), rh, ICI_CHUNKS)
        _start_chunks(own, r, D2D_CHUNKS)

    def finish():
        for j, chip in enumerate(chips):
            copy(j, (*chip, c), me).wait_recv()
            _start_chunks(lambda rows: copy(3 + j, (*chip, c), sibling, rows), rh, D2D_CHUNKS)
        for j, chip in enumerate(chips):
            copy(3 + j, (*chip, 1 - c), me).wait_recv()
        for j, chip in enumerate(chips):
            copy(j, me, (*chip, c), from_input=True).wait_send()
            copy(3 + j, (*chip, c), sibling).wait_send()
        own().wait()

    return start, finish


def _allgather_shards(shard):
    def body(in_ref, out_ref, send_sems, recv_sems):
        start, finish = _allgather_copies(in_ref, out_ref, send_sems, recv_sems)
        start()
        finish()

    return pl.pallas_call(
        body, name="allgather_shards", out_shape=jax.ShapeDtypeStruct((4,) + shard.shape, shard.dtype),
        in_specs=[HBM], out_specs=HBM,
        scratch_shapes=[pltpu.SemaphoreType.DMA((AG_SEMS,)), pltpu.SemaphoreType.DMA((AG_SEMS,))],
        compiler_params=pltpu.CompilerParams())(shard)


def _pair_exchange(g):
    s, _, rh, cols = g.shape

    def body(g_ref, recv_ref, send_sem, recv_sem):
        x, y, c = _coords()

        def copy(slot, rows):
            return pltpu.make_async_remote_copy(
                src_ref=g_ref.at[slot, 1 - c, rows, :], dst_ref=recv_ref.at[slot, rows, :], send_sem=send_sem,
                recv_sem=recv_sem, device_id=(x, y, 1 - c), device_id_type=MESH)

        for slot in range(s):
            _start_chunks(lambda rows: copy(slot, rows), rh, VMEM_CHUNKS)
        pltpu.make_async_remote_copy(
            src_ref=g_ref.at[:, 1 - c], dst_ref=recv_ref, send_sem=send_sem, recv_sem=recv_sem,
            device_id=(x, y, 1 - c), device_id_type=MESH).wait()

    return pl.pallas_call(
        body, name="pair_exchange", out_shape=jax.ShapeDtypeStruct((s, rh, cols), g.dtype), in_specs=[HBM],
        out_specs=HBM, scratch_shapes=[pltpu.SemaphoreType.DMA] * 2, compiler_params=pltpu.CompilerParams())(g)


def _chip_exchange_copies(p_ref, recv_ref, send_sems, recv_sems):
    rh = p_ref.shape[1]
    x, y, c = _coords()
    chips = _other_chips(x, y)

    def copy(j, chip, rows=pl.ds(0, rh)):
        return pltpu.make_async_remote_copy(
            src_ref=p_ref.at[2 * chip[0] + chip[1], rows, :], dst_ref=recv_ref.at[j, rows, :],
            send_sem=send_sems.at[j], recv_sem=recv_sems.at[j], device_id=(*chip, c), device_id_type=MESH)

    def start():
        for j, chip in enumerate(chips):
            _start_chunks(lambda rows: copy(j, chip, rows), rh, ICI_CHUNKS)

    def finish():
        for j, chip in enumerate(chips):
            copy(j, chip).wait_recv()
        for j, chip in enumerate(chips):
            copy(j, chip).wait_send()

    return start, finish


def _chip_exchange(p):
    def body(p_ref, recv_ref, send_sems, recv_sems):
        start, finish = _chip_exchange_copies(p_ref, recv_ref, send_sems, recv_sems)
        start()
        finish()

    return pl.pallas_call(
        body, name="chip_exchange", out_shape=jax.ShapeDtypeStruct((3,) + p.shape[1:], p.dtype), in_specs=[HBM],
        out_specs=HBM, scratch_shapes=[pltpu.SemaphoreType.DMA((3,)), pltpu.SemaphoreType.DMA((3,))],
        compiler_params=pltpu.CompilerParams())(p)


def _sum_chips(p, recv, tr):
    _, rh, cols = p.shape
    x, y, c = _coords()
    where = jnp.stack([2 * x + y, c]).astype(jnp.int32)

    def body(w_ref, own_ref, r_ref, o_ref):
        acc = own_ref[...].astype(F32)
        for k in range(3):
            acc = acc + r_ref[k].astype(F32)
        o_ref[...] = acc

    return pl.pallas_call(
        body, name="sum_chips", out_shape=jax.ShapeDtypeStruct((2, rh, cols), F32),
        grid_spec=pltpu.PrefetchScalarGridSpec(
            num_scalar_prefetch=1, grid=(rh // tr,),
            in_specs=[pl.BlockSpec((None, tr, cols), lambda i, w_ref: (w_ref[0], i, 0)),
                      pl.BlockSpec((3, tr, cols), lambda i, w_ref: (0, i, 0))],
            out_specs=pl.BlockSpec((None, tr, cols), lambda i, w_ref: (w_ref[1], i, 0))),
        compiler_params=_cp(("arbitrary",)))(where, p, recv)


def _pair_share(buf):
    _, rh, cols = buf.shape

    def body(in_ref, out_ref, send_sem, recv_sem):
        x, y, c = _coords()

        def copy(slot, rows=pl.ds(0, rh)):
            return pltpu.make_async_remote_copy(
                src_ref=in_ref.at[slot, rows, :], dst_ref=out_ref.at[slot, rows, :], send_sem=send_sem,
                recv_sem=recv_sem, device_id=(x, y, 1 - c), device_id_type=MESH)

        _start_chunks(lambda rows: copy(c, rows), rh, D2D_CHUNKS)
        copy(c).wait_send()
        copy(1 - c).wait_recv()

    return pl.pallas_call(
        body, name="pair_share", out_shape=jax.ShapeDtypeStruct(buf.shape, buf.dtype), in_specs=[HBM], out_specs=HBM,
        scratch_shapes=[pltpu.SemaphoreType.DMA] * 2, input_output_aliases={0: 0},
        compiler_params=pltpu.CompilerParams())(buf)


def _allgather_all(blk):
    m_per, cols = blk.shape
    whole = pl.ds(0, m_per)

    def body(x_ref, out_ref, send_sems, recv_sems, local_sem):
        x, y, c = _coords()
        me, sibling = (x, y, c), (x, y, 1 - c)
        chips = _other_chips(x, y)

        def rows_of(px, py, pc, rows):
            return out_ref.at[4 * px + 2 * py + pc, rows, :]

        def copy(k, block, to, rows=whole, from_input=False):
            return pltpu.make_async_remote_copy(
                src_ref=x_ref.at[rows, :] if from_input else rows_of(*block, rows), dst_ref=rows_of(*block, rows),
                send_sem=send_sems.at[k], recv_sem=recv_sems.at[k], device_id=to, device_id_type=MESH)

        mine = pltpu.make_async_copy(x_ref, rows_of(*me, whole), local_sem)
        mine.start()
        _start_chunks(lambda rows: copy(0, me, sibling, rows, from_input=True), m_per, VMEM_CHUNKS)
        for j, chip in enumerate(chips):
            _start_chunks(lambda rows: copy(1 + j, me, (*chip, c), rows, from_input=True), m_per, VMEM_CHUNKS)
        for j, chip in enumerate(chips):
            copy(1 + j, (*chip, c), me).wait_recv()
            _start_chunks(lambda rows: copy(4 + j, (*chip, c), sibling, rows), m_per, VMEM_CHUNKS)
        copy(0, sibling, me).wait_recv()
        for j, chip in enumerate(chips):
            copy(4 + j, (*chip, 1 - c), me).wait_recv()
        copy(0, me, sibling, from_input=True).wait_send()
        for j, chip in enumerate(chips):
            copy(1 + j, me, (*chip, c), from_input=True).wait_send()
            copy(4 + j, (*chip, c), sibling).wait_send()
        mine.wait()

    return pl.pallas_call(
        body, name="allgather_all", out_shape=jax.ShapeDtypeStruct((8, m_per, cols), blk.dtype),
        in_specs=[pl.BlockSpec(memory_space=pltpu.VMEM)], out_specs=pl.BlockSpec(memory_space=pltpu.VMEM),
        scratch_shapes=[pltpu.SemaphoreType.DMA((7,)), pltpu.SemaphoreType.DMA((7,)), pltpu.SemaphoreType.DMA],
        compiler_params=pltpu.CompilerParams(vmem_limit_bytes=VMEM_LIMIT))(blk)


def _add_kept(g, recv, tr):
    s, _, rh, cols = g.shape

    def body(c_ref, a_ref, b_ref, o_ref):
        o_ref[...] = (a_ref[...].astype(F32) + b_ref[...].astype(F32)).astype(o_ref.dtype)

    spec = pl.BlockSpec((None, tr, cols), lambda si, i, c_ref: (si, i, 0))
    return pl.pallas_call(
        body, name="add_kept", out_shape=jax.ShapeDtypeStruct((s, rh, cols), BF16),
        grid_spec=pltpu.PrefetchScalarGridSpec(
            num_scalar_prefetch=1, grid=(s, rh // tr),
            in_specs=[pl.BlockSpec((None, None, tr, cols), lambda si, i, c_ref: (si, c_ref[0], i, 0)), spec],
            out_specs=spec),
        compiler_params=_cp(("arbitrary", "arbitrary")))(lax.axis_index("c").astype(jnp.int32).reshape(1), g, recv)


def _sum_slots(p, tr, name):
    s, rows, cols = p.shape

    def body(p_ref, o_ref):
        acc = p_ref[0].astype(F32)
        for k in range(1, s):
            acc = acc + p_ref[k].astype(F32)
        o_ref[...] = acc

    return pl.pallas_call(
        body, name=name, grid=(rows // tr,), in_specs=[pl.BlockSpec((s, tr, cols), lambda i: (0, i, 0))],
        out_specs=_row_spec(tr, cols), out_shape=jax.ShapeDtypeStruct((rows, cols), F32),
        compiler_params=_cp(("arbitrary",)))(p)


def _adamw_call(w, g, m, v, name):
    rows, cols = w.shape
    tr = _tile(rows, 512) if rows % 512 == 0 else _tile(rows, 128)
    c1 = 1.0 - ADAM_B1 ** ADAM_STEP
    c2 = 1.0 - ADAM_B2 ** ADAM_STEP

    def body(w_ref, g_ref, m_ref, v_ref, d_ref, nm_ref, nv_ref):
        gv = g_ref[...]
        nm = ADAM_B1 * m_ref[...] + (1.0 - ADAM_B1) * gv
        nv = ADAM_B2 * v_ref[...] + (1.0 - ADAM_B2) * jnp.square(gv)
        d_ref[...] = -ADAM_LR * ((nm / c1) / (jnp.sqrt(nv / c2) + ADAM_EPS) + ADAM_WD * w_ref[...])
        nm_ref[...] = nm
        nv_ref[...] = nv

    spec = _row_spec(tr, cols)
    o = jax.ShapeDtypeStruct((rows, cols), F32)
    return pl.pallas_call(body, name=name, grid=(rows // tr,), in_specs=[spec] * 4, out_specs=[spec] * 3,
                          out_shape=[o, o, o], compiler_params=_cp(("arbitrary",)))(w, g, m, v)


WEIGHTS = ('norm1_g', 'w_in', 'sgu_norm_g', 'sgu_w', 'sgu_b', 's5_lambda_re', 's5_lambda_im', 's5_log_dt',
           's5_b_re', 's5_b_im', 's5_c_re', 's5_c_im', 's5_d', 's5_glu_w', 's5_glu_b', 'lru_conv_w',
           'lru_conv_b', 'lru_wa', 'lru_ba', 'lru_wx', 'lru_bx', 'lru_lambda', 'fox_fgate_b', 'mix_norm_g',
           'w_out', 'norm2_g', 'w_mlp_in', 'w_mlp_out', 'final_g')
N_W = len(WEIGHTS)


def _pack_shards(shards, dtype, names=tuple(SHARD_SHAPE), rows=LAYER_ROWS):
    parts = []
    for n in names:
        lead = shards[n].shape[:-2]
        flat = shards[n].reshape(*lead, -1).astype(dtype)
        flat = jnp.pad(flat, [(0, 0)] * len(lead) + [(0, SHARD_ROWS[n] * PACK_COLS - flat.shape[-1])])
        parts.append(flat.reshape(*lead, SHARD_ROWS[n], PACK_COLS))
    lead = parts[0].shape[:-2]
    used = sum(SHARD_ROWS[n] for n in names)
    if rows > used:
        parts.append(jnp.zeros((*lead, rows - used, PACK_COLS), dtype))
    return jnp.concatenate(parts, axis=-2)


def _unpack_shards(buf, names=tuple(SHARD_SHAPE)):
    lead = buf.shape[:-2]
    out = {}
    for n in names:
        s0, s1 = SHARD_SHAPE[n]
        rows, off = SHARD_ROWS[n], SHARD_OFF[n]
        flat = buf[..., off:off + rows, :].reshape(*lead, rows * PACK_COLS)
        out[n] = flat[..., :s0 * s1].reshape(*lead, s0, s1)
    return out


def _join_chips(g, axis):
    _, d, s0, s1 = g.shape
    if axis == 0:
        return g.transpose(1, 0, 2, 3).reshape(d, 4 * s0, s1)
    return g.transpose(1, 2, 0, 3).reshape(d, s0, 4 * s1)


def _split_chips(w, axis):
    d = w.shape[0]
    if axis == 0:
        return w.reshape(d, 4, w.shape[1] // 4, w.shape[2]).transpose(1, 0, 2, 3)
    return w.reshape(d, w.shape[1], 4, w.shape[2] // 4).transpose(2, 0, 1, 3)


def _flat_rows(shape):
    return -(-math.prod(shape) // PACK_COLS)


def _pack_flat(arrs, rows, dtype=F32):
    parts = []
    for a in arrs:
        flat = a.reshape(-1).astype(dtype)
        r = _flat_rows(a.shape)
        parts.append(jnp.pad(flat, (0, r * PACK_COLS - flat.shape[0])).reshape(r, PACK_COLS))
    used = sum(p.shape[0] for p in parts)
    parts.append(jnp.zeros((rows - used, PACK_COLS), dtype))
    return jnp.concatenate(parts, axis=0)


def _split3(s):
    hi = s.astype(BF16).astype(F32)
    mid = (s - hi).astype(BF16).astype(F32)
    return jnp.stack([hi, mid, s - hi - mid])


def _unpack_flat(buf, shapes):
    out, off = [], 0
    for s in shapes:
        r = _flat_rows(s)
        out.append(buf[off:off + r].reshape(-1)[:math.prod(s)].reshape(s))
        off += r
    return out


def _write_glue(send, glue):
    def body(g_ref, s_ref, o_ref):
        o_ref[...] = g_ref[...]

    blk = (None, GLUE_ROWS, PACK_COLS)
    return pl.pallas_call(
        body, name="write_glue", grid=(4,),
        in_specs=[pl.BlockSpec(blk, lambda s: (s, 0, 0)), pl.BlockSpec(memory_space=pl.ANY)],
        out_specs=pl.BlockSpec(blk, lambda s: (s, SMALL_OFF // GLUE_ROWS, 0)),
        out_shape=jax.ShapeDtypeStruct(send.shape, send.dtype), input_output_aliases={1: 0},
        compiler_params=_cp(("arbitrary",)))(glue, send)


GLUE_PACKED = ('w_in', 's5_glu_w', 'lru_conv_w')
REDUCE_ROWS = LAYER_ROWS // 4


def _layer_weights(gathered):
    parts = _unpack_shards(gathered[:, None], GLUE_PACKED + ('w_out',))
    joined = {n: _join_chips(g, SHARDED_AXIS[n])[0] for n, g in parts.items()}
    joined['w_in'] = jnp.pad(joined['w_in'], ((0, 0), (0, D_IN_PAD - D_IN_PROJ)))
    return joined


def _reduce_start(send):
    halves = send.reshape(4, 2, LAYER_ROWS // 2, PACK_COLS)
    return _add_kept(halves, _pair_exchange(halves), REDUCE_ROWS)


def _reduce_finish(chip_sum, received):
    return _pair_share(_sum_chips(chip_sum, received, REDUCE_ROWS)).reshape(LAYER_ROWS, PACK_COLS)


def _forward_backward(x, target, final_g, shards, rep):
    norm_p = [{n: rep[n][l] for n in ('norm1_g', 'norm2_g', 'mix_norm_g')} for l in range(DEPTH)]
    gathered = _allgather_shards(shards[0])
    layers = []
    for l in range(DEPTH):
        lw = _layer_weights(gathered)
        mix_p = {n: lw[n].astype(F32) if n in lw else rep[n][l] for n in MIXER_WEIGHTS}
        x, res, following = _layer_fwd(x, gathered, lw['w_in'], lw['w_out'], norm_p[l], mix_p,
                                       shards[l + 1] if l + 1 < DEPTH else None)
        layers.append((res, gathered, lw))
        gathered = following
    loss_part, g, d_final = _loss_call(x, final_g[None, :], target)

    small, reduced, pending = [None] * DEPTH, [None] * DEPTH, None
    for l in reversed(range(DEPTH)):
        res, gathered, lw = layers[l]
        g, small[l], d_w_in, send, received = _layer_bwd(g, res, gathered, lw['w_in'], lw['w_out'], norm_p[l], pending)
        if pending is not None:
            reduced[l + 1] = _reduce_finish(pending, received)
        mine = {'w_in': d_w_in[:, :D_IN_PROJ], 's5_glu_w': small[l].pop('s5_glu_w'),
                'lru_conv_w': small[l].pop('lru_conv_w')}
        glue = _pack_shards({n: _split_chips(a[None], SHARDED_AXIS[n]) for n, a in mine.items()}, BF16,
                            names=GLUE_PACKED, rows=GLUE_ROWS)[:, 0]
        pending = _reduce_start(_write_glue(send, glue))
    reduced[0] = _reduce_finish(pending, _chip_exchange(pending))
    stacked = {n: jnp.stack([small[l][n] for l in range(DEPTH)]) for n in small[0]}
    return loss_part, g, d_final, stacked, _unpack_shards(jnp.stack(reduced))


def _step(*args):
    x, target = args[0], args[1 + N_W]
    w = dict(zip(WEIGHTS, args[1:1 + N_W]))
    m = dict(zip(WEIGHTS, args[2 + N_W:2 + 2 * N_W]))
    v = dict(zip(WEIGHTS, args[2 + 2 * N_W:2 + 3 * N_W]))
    small = [n for n in WEIGHTS if n not in SHARD_SHAPE]

    shards = _pack_shards({n: w[n] for n in SHARD_SHAPE}, BF16)
    loss_part, dx, d_final, dw, g_shard = _forward_backward(
        x[0], target[0], w['final_g'], shards, {n: w[n] for n in small})

    small_g = [d_final.reshape(-1) if n == 'final_g' else dw[n] for n in small]
    small_rows = -(-(sum(_flat_rows(w[n].shape) for n in small) + 1) // 128) * 128
    mine = _pack_flat(small_g + [_split3(loss_part[0, 0])], small_rows, BF16)
    small_sum = _sum_slots(_allgather_all(mine), 128, "sum_devices")
    *g_small, loss = _unpack_flat(small_sum, [w[n].shape for n in small] + [(3,)])
    loss = jnp.sum(loss)

    grads, delta, new_m, new_v = {}, {}, {}, {}
    for n in SHARD_SHAPE:
        shp = w[n].shape
        v2 = lambda a: a.reshape(-1, shp[-1])
        res = _adamw_call(v2(w[n]), v2(g_shard[n]), v2(m[n]), v2(v[n]), "adamw_" + n)
        grads[n] = g_shard[n]
        delta[n], new_m[n], new_v[n] = (r.reshape(shp) for r in res)
    pk = lambda d: _pack_flat([d[n] for n in small], small_rows)
    res = _adamw_call(pk(w), _pack_flat(g_small, small_rows), pk(m), pk(v), "adamw_small")
    shapes = [w[n].shape for n in small]
    for n, g, d_, m_, v_ in zip(small, g_small, *(_unpack_flat(r, shapes) for r in res)):
        grads[n], delta[n], new_m[n], new_v[n] = g, d_, m_, v_

    return (loss, dx[None], *[grads[n] for n in WEIGHTS], *[delta[n] for n in WEIGHTS],
            *[new_m[n] for n in WEIGHTS], *[new_v[n] for n in WEIGHTS])


def kernel(x, norm1_g, w_in, sgu_norm_g, sgu_w, sgu_b, s5_lambda_re, s5_lambda_im, s5_log_dt, s5_b_re, s5_b_im, s5_c_re, s5_c_im, s5_d, s5_glu_w, s5_glu_b, lru_conv_w, lru_conv_b, lru_wa, lru_ba, lru_wx, lru_bx, lru_lambda, fox_fgate_b, mix_norm_g, w_out, norm2_g, w_mlp_in, w_mlp_out, final_g, loss_target, m_norm1_g, m_w_in, m_sgu_norm_g, m_sgu_w, m_sgu_b, m_s5_lambda_re, m_s5_lambda_im, m_s5_log_dt, m_s5_b_re, m_s5_b_im, m_s5_c_re, m_s5_c_im, m_s5_d, m_s5_glu_w, m_s5_glu_b, m_lru_conv_w, m_lru_conv_b, m_lru_wa, m_lru_ba, m_lru_wx, m_lru_bx, m_lru_lambda, m_fox_fgate_b, m_mix_norm_g, m_w_out, m_norm2_g, m_w_mlp_in, m_w_mlp_out, m_final_g, v_norm1_g, v_w_in, v_sgu_norm_g, v_sgu_w, v_sgu_b, v_s5_lambda_re, v_s5_lambda_im, v_s5_log_dt, v_s5_b_re, v_s5_b_im, v_s5_c_re, v_s5_c_im, v_s5_d, v_s5_glu_w, v_s5_glu_b, v_lru_conv_w, v_lru_conv_b, v_lru_wa, v_lru_ba, v_lru_wx, v_lru_bx, v_lru_lambda, v_fox_fgate_b, v_mix_norm_g, v_w_out, v_norm2_g, v_w_mlp_in, v_w_mlp_out, v_final_g):
    return _step(x, norm1_g, w_in, sgu_norm_g, sgu_w, sgu_b, s5_lambda_re, s5_lambda_im, s5_log_dt, s5_b_re, s5_b_im, s5_c_re, s5_c_im, s5_d, s5_glu_w, s5_glu_b, lru_conv_w, lru_conv_b, lru_wa, lru_ba, lru_wx, lru_bx, lru_lambda, fox_fgate_b, mix_norm_g, w_out, norm2_g, w_mlp_in, w_mlp_out, final_g, loss_target, m_norm1_g, m_w_in, m_sgu_norm_g, m_sgu_w, m_sgu_b, m_s5_lambda_re, m_s5_lambda_im, m_s5_log_dt, m_s5_b_re, m_s5_b_im, m_s5_c_re, m_s5_c_im, m_s5_d, m_s5_glu_w, m_s5_glu_b, m_lru_conv_w, m_lru_conv_b, m_lru_wa, m_lru_ba, m_lru_wx, m_lru_bx, m_lru_lambda, m_fox_fgate_b, m_mix_norm_g, m_w_out, m_norm2_g, m_w_mlp_in, m_w_mlp_out, m_final_g, v_norm1_g, v_w_in, v_sgu_norm_g, v_sgu_w, v_sgu_b, v_s5_lambda_re, v_s5_lambda_im, v_s5_log_dt, v_s5_b_re, v_s5_b_im, v_s5_c_re, v_s5_c_im, v_s5_d, v_s5_glu_w, v_s5_glu_b, v_lru_conv_w, v_lru_conv_b, v_lru_wa, v_lru_ba, v_lru_wx, v_lru_bx, v_lru_lambda, v_fox_fgate_b, v_mix_norm_g, v_w_out, v_norm2_g, v_w_mlp_in, v_w_mlp_out, v_final_g)
```

```python
import functools
import math

import jax
import jax.numpy as jnp
from jax import lax
from jax.experimental import pallas as pl
from jax.experimental.pallas import tpu as pltpu

F32 = jnp.float32
BF16 = jnp.bfloat16

DEPTH = 4
D_MODEL = 1024
MIXER_WIDTH = 256
SGU_CHUNK = 128
N_HEADS = 4
HEAD_DIM = 64
S5_GROUPS = 16
S5_GROUP = 16
S5_STATE = 64
LRU_C = 8.0
RMS_EPS = 1e-6
D_IN_PROJ = 8 * MIXER_WIDTH + N_HEADS
D_IN_PAD = 8 * MIXER_WIDTH + 128
ADAM_LR, ADAM_B1, ADAM_B2, ADAM_EPS, ADAM_WD, ADAM_STEP = 0.001, 0.9, 0.999, 1e-08, 0.01, 10

V7X_VMEM_BYTES = 64 * 1024 * 1024
VMEM_LIMIT = V7X_VMEM_BYTES - 8 * 1024 * 1024
NEG = -1e30
MESH = pl.DeviceIdType.MESH


def _cp(sem=None, **kw):
    return pltpu.CompilerParams(dimension_semantics=sem, vmem_limit_bytes=VMEM_LIMIT, **kw)


def _full_spec(a):
    nd = a.ndim
    return pl.BlockSpec(a.shape, lambda *_: (0,) * nd)


def _tile(n, pref=512):
    return pref if n % pref == 0 else n


def _dot(a, b, ca, cb):
    return lax.dot_general(a.astype(BF16), b.astype(BF16), (((ca,), (cb,)), ((), ())),
                           preferred_element_type=F32)


def _matmul(name, grid, a, a_spec, b, b_spec, dims, outs, *, extras=(), epilogue=None, into=None, summed=0,
            acc_shape=None):
    nk = grid[2]
    n_ex, n_out = len(extras), len(outs)
    tm_tn = acc_shape or tuple(d for d in outs[0][1].block_shape if d is not None)[-2:]

    def body(*refs):
        a_ref, b_ref = refs[0], refs[1]
        ex_refs = refs[2:2 + n_ex]
        o_refs = refs[len(refs) - n_out - (nk > 1):len(refs) - (nk > 1)]
        if summed:
            @pl.when((pl.program_id(0) == 0) & (pl.program_id(1) == 0) & (pl.program_id(2) == 0))
            def _():
                for o_ref in o_refs[n_out - summed:]:
                    o_ref[...] = jnp.zeros_like(o_ref)

        def finish(val):
            res = epilogue(val, *[e[...] for e in ex_refs]) if epilogue else (val,)
            for idx, (o_ref, r) in enumerate(zip(o_refs, res)):
                if idx >= n_out - summed:
                    o_ref[...] += r
                else:
                    o_ref[...] = r.astype(o_ref.dtype)

        if nk == 1:
            finish(_dot(a_ref[...], b_ref[...], *dims))
        else:
            acc = refs[-1]
            kk = pl.program_id(2)

            @pl.when(kk == 0)
            def _():
                acc[...] = jnp.zeros_like(acc)

            acc[...] += _dot(a_ref[...], b_ref[...], *dims)

            @pl.when(kk == nk - 1)
            def _():
                finish(acc[...])

    ins = [a, b] + [e[0] for e in extras]
    specs = [a_spec, b_spec] + [e[1] for e in extras]
    aliases = {}
    if into is not None:
        aliases = {len(ins): 0}
        ins.append(into)
        specs.append(pl.BlockSpec(memory_space=pl.ANY))
    res = pl.pallas_call(
        body, name=name, grid=grid, in_specs=specs, out_specs=[o[1] for o in outs], out_shape=[o[0] for o in outs],
        scratch_shapes=[pltpu.VMEM(tm_tn, F32)] if nk > 1 else [], input_output_aliases=aliases,
        compiler_params=_cp(("arbitrary", "arbitrary", "arbitrary")))(*ins)
    return res[0] if n_out == 1 else res


@jax.custom_vjp
def _bdot(a, b):
    return _dot(a, b, 1, 0)


def _bdot_fwd(a, b):
    return _dot(a, b, 1, 0), (a, b)


def _bdot_bwd(r, g):
    a, b = r
    return _dot(g, b, 1, 1), _dot(a, g, 0, 0)


_bdot.defvjp(_bdot_fwd, _bdot_bwd)


def _row_spec(tr, w):
    return pl.BlockSpec((tr, w), lambda i: (i, 0))


def _rowwise(fn, rows, pars, outs, *, name, tr, dtype=F32):
    t = rows[0].shape[0]
    n_in = len(rows) + len(pars)

    def body(*refs):
        res = fn(*[r[...] for r in refs[:n_in]])
        for o_ref, v in zip(refs[n_in:], res):
            o_ref[...] = v.astype(o_ref.dtype)

    return pl.pallas_call(
        body, name=name, grid=(t // tr,),
        in_specs=[_row_spec(tr, r.shape[1]) for r in rows] + [_full_spec(p) for p in pars],
        out_specs=[_row_spec(tr, w) for w in outs],
        out_shape=[jax.ShapeDtypeStruct((t, w), dtype) for w in outs],
        compiler_params=_cp(("arbitrary",)))(*rows, *pars)


def _rowwise_vjp(fn, rows, pars, cots, *, name, tr, add=None):
    t = rows[0].shape[0]
    nr, npar = len(rows), len(pars)
    cots = list(cots) + ([add] if add is not None else [])
    nc = len(cots)

    def body(*refs):
        vals = [r[...] for r in refs[:nr + npar]]
        cts = [c[...] for c in refs[nr + npar:nr + npar + nc]]
        douts = refs[nr + npar + nc:]
        extra = cts.pop() if add is not None else None
        _, vjp = jax.vjp(fn, *vals)
        grads = list(vjp(tuple(cts)))
        if extra is not None:
            grads[0] = grads[0] + extra
        for kk in range(nr):
            douts[kk][...] = grads[kk]

        @pl.when(pl.program_id(0) == 0)
        def _():
            for kk in range(npar):
                douts[nr + kk][...] = jnp.zeros_like(douts[nr + kk])

        for kk in range(npar):
            douts[nr + kk][...] += grads[nr + kk]

    return pl.pallas_call(
        body, name=name, grid=(t // tr,),
        in_specs=[_row_spec(tr, r.shape[1]) for r in rows] + [_full_spec(p) for p in pars]
        + [_row_spec(tr, c.shape[1]) for c in cots],
        out_specs=[_row_spec(tr, r.shape[1]) for r in rows] + [_full_spec(p) for p in pars],
        out_shape=[jax.ShapeDtypeStruct(r.shape, F32) for r in rows]
        + [jax.ShapeDtypeStruct(p.shape, F32) for p in pars],
        compiler_params=_cp(("arbitrary",)))(*rows, *pars, *cots)


def _make_rw(fn, name, tr, nr, outs):
    @jax.custom_vjp
    def f(*args):
        return tuple(_rowwise(fn, args[:nr], args[nr:], outs, name=name + "_f", tr=tr))

    def fwd(*args):
        return f(*args), args

    def bwd(args, cts):
        return tuple(_rowwise_vjp(fn, args[:nr], args[nr:], list(cts), name=name + "_b", tr=tr))

    f.defvjp(fwd, bwd)
    return f


def _rms(x, g):
    return x * lax.rsqrt(jnp.mean(jnp.square(x), axis=-1, keepdims=True) + RMS_EPS) * g


def _f_rms(x, g):
    return (_rms(x, g),)


def _f_sgu(au, av, ng, w0, w1, w2, w3, bfull):
    u = jax.nn.gelu(au)
    v = _rms(jax.nn.gelu(av), ng)
    tri = lax.broadcasted_iota(jnp.int32, (SGU_CHUNK, SGU_CHUNK), 0) >= lax.broadcasted_iota(
        jnp.int32, (SGU_CHUNK, SGU_CHUNK), 1)
    head = lax.broadcasted_iota(jnp.int32, v.shape, 1) // HEAD_DIM
    mixed = bfull
    for h, w in enumerate((w0, w1, w2, w3)):
        mixed = mixed + _bdot(jnp.where(tri, w, 0.0), jnp.where(head == h, v, 0.0))
    return (u * mixed,)


def _f_s5disc(lam_re, lam_im, log_dt, b_re, b_im):
    dt = jnp.exp(log_dt)
    mag = jnp.exp(lam_re * dt)
    abar_re = mag * jnp.cos(lam_im * dt)
    abar_im = mag * jnp.sin(lam_im * dt)
    denom = jnp.square(lam_re) + jnp.square(lam_im)
    num_re = abar_re - 1.0
    num_im = abar_im
    fac_re = (num_re * lam_re + num_im * lam_im) / denom
    fac_im = (num_im * lam_re - num_re * lam_im) / denom
    return abar_re, abar_im, fac_re * b_re - fac_im * b_im, fac_re * b_im + fac_im * b_re


def _f_s5post(s_re, s_im, u, c_re, c_im, d, gw, gb):
    y = _bdot(s_re, c_re) - _bdot(s_im, c_im) + d * u
    y = jax.nn.gelu(y)
    return (y * jax.nn.sigmoid(_bdot(y, gw) + gb),)


def _f_lrupre(xc, wa, ba, wx, bx, lam):
    r = jax.nn.sigmoid(_bdot(xc, wa) + ba)
    i = jax.nn.sigmoid(_bdot(xc, wx) + bx)
    log_a = -LRU_C * r * jax.nn.softplus(-lam)
    a = jnp.exp(log_a)
    one_minus_a2 = -jnp.tanh(log_a) * (jnp.exp(2.0 * log_a) + 1.0)
    return a, jnp.sqrt(one_minus_a2) * (i * xc)


def _f_lrupost(h, gate):
    return (h * jax.nn.gelu(gate),)


def _f_logsig(zf, bf):
    return (jax.nn.log_sigmoid(zf + bf),)


def _f_gnorm(ya, yb, yc, yd, g):
    def n(y):
        return y * lax.rsqrt(jnp.mean(jnp.square(y), axis=-1, keepdims=True) + RMS_EPS)
    return (jnp.concatenate([n(ya), n(yb), n(yc), n(yd)], axis=1) * g,)


sgu_mix = _make_rw(_f_sgu, "sgu", SGU_CHUNK, 2, [MIXER_WIDTH])
s5_disc = _make_rw(_f_s5disc, "s5disc", S5_GROUPS * S5_GROUP, 5, [S5_STATE] * 4)
s5_post = _make_rw(_f_s5post, "s5post", 256, 3, [MIXER_WIDTH])
lru_pre = _make_rw(_f_lrupre, "lrupre", 512, 1, [MIXER_WIDTH, MIXER_WIDTH])
lru_post = _make_rw(_f_lrupost, "lrupost", 512, 2, [MIXER_WIDTH])
log_sig = _make_rw(_f_logsig, "logsig", 512, 1, [128])


SCAN_TILE = 512


def _prev_spec(c, nt, rev):
    per = SCAN_TILE // 8
    if rev:
        return pl.BlockSpec((8, c), lambda i: (jnp.maximum((nt - 1 - i) * per - 1, 0), 0))
    return pl.BlockSpec((8, c), lambda i: (jnp.maximum(i * per - 1, 0), 0))


SCAN_STEPS = (1, 2, 4)


def _cmul(ar, ai, br, bi):
    return ar * br - ai * bi, ar * bi + ai * br


def _rows_down(x, k, fill, rowid):
    return jnp.where(rowid >= k, pltpu.roll(x, k, 0), fill)


def _rows_up(x, k, fill, rowid):
    return jnp.where(rowid < 8 - k, pltpu.roll(x, 8 - k, 0), fill)


def _powers(ar, ai):
    pw = [(ar, ai)]
    for _ in range(7):
        pw.append(_cmul(*pw[-1], ar, ai))
    return pw


def _block(i):
    return pl.ds(pl.multiple_of(i * 8, 8), 8)


def _row_before(ref, i, edge):
    return jnp.where(i == 0, edge, ref[pl.ds(jnp.maximum(i * 8 - 1, 0), 1), :])


def _lti_fwd_call(u, w_re, w_im, a_re, a_im):
    t, kdim = u.shape
    c = w_re.shape[1]
    tt = SCAN_TILE

    def body(u_ref, wr_ref, wi_ref, ar_ref, ai_ref, sr_ref, si_ref, br_ref, bi_ref, cr, ci):
        @pl.when(pl.program_id(0) == 0)
        def _():
            cr[...] = jnp.zeros_like(cr)
            ci[...] = jnp.zeros_like(ci)

        br_ref[...] = _dot(u_ref[...], wr_ref[...], 1, 0)
        bi_ref[...] = _dot(u_ref[...], wi_ref[...], 1, 0)
        pw = _powers(ar_ref[...], ai_ref[...])
        apr = jnp.concatenate([p[0] for p in pw], axis=0)
        api = jnp.concatenate([p[1] for p in pw], axis=0)
        rowid = lax.broadcasted_iota(jnp.int32, (8, c), 0)

        def block(i, carry):
            xr, xi = br_ref[_block(i), :], bi_ref[_block(i), :]
            for k in SCAN_STEPS:
                dr, di = _cmul(*pw[k - 1], _rows_down(xr, k, 0.0, rowid), _rows_down(xi, k, 0.0, rowid))
                xr, xi = xr + dr, xi + di
            dr, di = _cmul(apr, api, *carry)
            xr, xi = xr + dr, xi + di
            sr_ref[_block(i), :] = xr
            si_ref[_block(i), :] = xi
            return xr[7:8, :], xi[7:8, :]

        hr, hi = lax.fori_loop(0, tt // 8, block, (cr[...], ci[...]), unroll=2)
        cr[...] = hr
        ci[...] = hi

    row = pl.BlockSpec((tt, c), lambda i: (i, 0))
    par = pl.BlockSpec((1, c), lambda i: (0, 0))
    return pl.pallas_call(
        body, name="lti_scan_f", grid=(t // tt,),
        in_specs=[pl.BlockSpec((tt, kdim), lambda i: (i, 0)), _full_spec(w_re), _full_spec(w_im), par, par],
        out_specs=[row, row], out_shape=[jax.ShapeDtypeStruct((t, c), F32)] * 2,
        scratch_shapes=[pltpu.VMEM((tt, c), F32)] * 2 + [pltpu.VMEM((1, c), F32)] * 2,
        compiler_params=_cp(("arbitrary",)))(u, w_re, w_im, a_re, a_im)


def _lti_bwd_call(u, w_re, w_im, a_re, a_im, s_re, s_im, g_re, g_im):
    t, c = g_re.shape
    kdim = u.shape[1]
    tt = SCAN_TILE
    nt = t // tt
    nb = tt // 8

    def body(u_ref, wr_ref, wi_ref, ar_ref, ai_ref, sr_ref, si_ref, pr_ref, pi_ref, gr_ref, gi_ref,
             du_ref, dwr_ref, dwi_ref, dar_ref, dai_ref, or_ref, oi_ref, cr, ci):
        ti = pl.program_id(0)

        @pl.when(ti == 0)
        def _():
            cr[...] = jnp.zeros_like(cr)
            ci[...] = jnp.zeros_like(ci)
            dar_ref[...] = jnp.zeros_like(dar_ref)
            dai_ref[...] = jnp.zeros_like(dai_ref)
            dwr_ref[...] = jnp.zeros_like(dwr_ref)
            dwi_ref[...] = jnp.zeros_like(dwi_ref)

        pw = _powers(ar_ref[...], -ai_ref[...])
        tpr = jnp.concatenate([p[0] for p in reversed(pw)], axis=0)
        tpi = jnp.concatenate([p[1] for p in reversed(pw)], axis=0)
        rowid = lax.broadcasted_iota(jnp.int32, (8, c), 0)
        first = ti == nt - 1
        edge_r = jnp.where(first, 0.0, pr_ref[7:8, :])
        edge_i = jnp.where(first, 0.0, pi_ref[7:8, :])

        def block(kk, carry):
            i = nb - 1 - kk
            gr_c, gi_c, acc_r, acc_i = carry
            xr, xi = gr_ref[_block(i), :], gi_ref[_block(i), :]
            for k in SCAN_STEPS:
                dr, di = _cmul(*pw[k - 1], _rows_up(xr, k, 0.0, rowid), _rows_up(xi, k, 0.0, rowid))
                xr, xi = xr + dr, xi + di
            dr, di = _cmul(tpr, tpi, gr_c, gi_c)
            xr, xi = xr + dr, xi + di
            or_ref[_block(i), :] = xr
            oi_ref[_block(i), :] = xi
            spr = _rows_down(sr_ref[_block(i), :], 1, _row_before(sr_ref, i, edge_r), rowid)
            spi = _rows_down(si_ref[_block(i), :], 1, _row_before(si_ref, i, edge_i), rowid)
            return xr[0:1, :], xi[0:1, :], acc_r + spr * xr + spi * xi, acc_i + spr * xi - spi * xr

        zero = jnp.zeros((8, c), F32)
        gr_c, gi_c, acc_r, acc_i = lax.fori_loop(0, nb, block, (cr[...], ci[...], zero, zero), unroll=2)
        cr[...] = gr_c
        ci[...] = gi_c
        dar_ref[...] += jnp.sum(acc_r, axis=0, keepdims=True)
        dai_ref[...] += jnp.sum(acc_i, axis=0, keepdims=True)
        du_ref[...] = _dot(or_ref[...], wr_ref[...], 1, 1) + _dot(oi_ref[...], wi_ref[...], 1, 1)
        dwr_ref[...] += _dot(u_ref[...], or_ref[...], 0, 0)
        dwi_ref[...] += _dot(u_ref[...], oi_ref[...], 0, 0)

    row = pl.BlockSpec((tt, c), lambda i: (nt - 1 - i, 0))
    row_u = pl.BlockSpec((tt, kdim), lambda i: (nt - 1 - i, 0))
    par = pl.BlockSpec((1, c), lambda i: (0, 0))
    prev = _prev_spec(c, nt, True)
    return pl.pallas_call(
        body, name="lti_scan_b", grid=(nt,),
        in_specs=[row_u, _full_spec(w_re), _full_spec(w_im), par, par, row, row, prev, prev, row, row],
        out_specs=[row_u, _full_spec(w_re), _full_spec(w_im), par, par],
        out_shape=[jax.ShapeDtypeStruct((t, kdim), F32), jax.ShapeDtypeStruct(w_re.shape, F32),
                   jax.ShapeDtypeStruct(w_im.shape, F32)] + [jax.ShapeDtypeStruct((1, c), F32)] * 2,
        scratch_shapes=[pltpu.VMEM((tt, c), F32)] * 2 + [pltpu.VMEM((1, c), F32)] * 2,
        compiler_params=_cp(("arbitrary",)))(u, w_re, w_im, a_re, a_im, s_re, s_im, s_re, s_im, g_re, g_im)


@jax.custom_vjp
def lti_scan(u, w_re, w_im, a_re, a_im):
    return tuple(_lti_fwd_call(u, w_re, w_im, a_re, a_im))


def _lti_scan_fwd(u, w_re, w_im, a_re, a_im):
    s_re, s_im = _lti_fwd_call(u, w_re, w_im, a_re, a_im)
    return (s_re, s_im), (u, w_re, w_im, a_re, a_im, s_re, s_im)


def _lti_scan_bwd(r, g):
    return tuple(_lti_bwd_call(*r, g[0], g[1]))


lti_scan.defvjp(_lti_scan_fwd, _lti_scan_bwd)


def _tv_fwd_call(a, b):
    t, c = b.shape
    tt = SCAN_TILE

    def body(a_ref, b_ref, h_ref, ch):
        @pl.when(pl.program_id(0) == 0)
        def _():
            ch[...] = jnp.zeros_like(ch)

        rowid = lax.broadcasted_iota(jnp.int32, (8, c), 0)

        def block(i, h):
            ab, x = a_ref[_block(i), :], b_ref[_block(i), :]
            for k in SCAN_STEPS:
                x = x + ab * _rows_down(x, k, 0.0, rowid)
                ab = ab * _rows_down(ab, k, 1.0, rowid)
            x = x + ab * h
            h_ref[_block(i), :] = x
            return x[7:8, :]

        ch[...] = lax.fori_loop(0, tt // 8, block, ch[...], unroll=2)

    row = pl.BlockSpec((tt, c), lambda i: (i, 0))
    return pl.pallas_call(
        body, name="tv_scan_f", grid=(t // tt,), in_specs=[row, row], out_specs=row,
        out_shape=jax.ShapeDtypeStruct((t, c), F32), scratch_shapes=[pltpu.VMEM((1, c), F32)],
        compiler_params=_cp(("arbitrary",)))(a, b)


def _tv_bwd_call(a, h, g):
    t, c = g.shape
    tt = SCAN_TILE
    nt = t // tt
    nb = tt // 8

    def body(a_ref, h_ref, p_ref, g_ref, da_ref, db_ref, cg, ca):
        ti = pl.program_id(0)

        @pl.when(ti == 0)
        def _():
            cg[...] = jnp.zeros_like(cg)
            ca[...] = jnp.zeros_like(ca)

        rowid = lax.broadcasted_iota(jnp.int32, (8, c), 0)
        edge = jnp.where(ti == nt - 1, 0.0, p_ref[7:8, :])

        def block(kk, carry):
            i = nb - 1 - kk
            gc, a_next = carry
            ab, x = a_ref[_block(i), :], g_ref[_block(i), :]
            cb = _rows_up(ab, 1, a_next, rowid)
            for k in SCAN_STEPS:
                x = x + cb * _rows_up(x, k, 0.0, rowid)
                cb = cb * _rows_up(cb, k, 1.0, rowid)
            x = x + cb * gc
            db_ref[_block(i), :] = x
            da_ref[_block(i), :] = x * _rows_down(h_ref[_block(i), :], 1, _row_before(h_ref, i, edge), rowid)
            return x[0:1, :], ab[0:1, :]

        gc, a_next = lax.fori_loop(0, nb, block, (cg[...], ca[...]), unroll=2)
        cg[...] = gc
        ca[...] = a_next

    row = pl.BlockSpec((tt, c), lambda i: (nt - 1 - i, 0))
    return pl.pallas_call(
        body, name="tv_scan_b", grid=(nt,), in_specs=[row, row, _prev_spec(c, nt, True), row],
        out_specs=[row, row], out_shape=[jax.ShapeDtypeStruct((t, c), F32)] * 2,
        scratch_shapes=[pltpu.VMEM((1, c), F32)] * 2, compiler_params=_cp(("arbitrary",)))(a, h, h, g)


@jax.custom_vjp
def tv_scan(a, b):
    return _tv_fwd_call(a, b)


def _tv_scan_fwd(a, b):
    h = _tv_fwd_call(a, b)
    return h, (a, h)


def _tv_scan_bwd(r, g):
    a, h = r
    return tuple(_tv_bwd_call(a, h, g))


tv_scan.defvjp(_tv_scan_fwd, _tv_scan_bwd)


CONV_K = 4
CONV_ROWS = 512


def _conv_fwd_call(x, w, b):
    t, c = x.shape

    def body(x_ref, w_ref, b_ref, o_ref, xp):
        xp[0:8, :] = jnp.zeros((8, c), F32)
        xp[8:, :] = x_ref[...]
        for blk in range(t // CONV_ROWS):
            base = blk * CONV_ROWS
            acc = jnp.broadcast_to(b_ref[...], (CONV_ROWS, c))
            for kk in range(CONV_K):
                acc = acc + w_ref[kk:kk + 1, :] * xp[base + 5 + kk:base + 5 + kk + CONV_ROWS, :]
            o_ref[base:base + CONV_ROWS, :] = acc

    return pl.pallas_call(
        body, name="conv_f", out_shape=jax.ShapeDtypeStruct((t, c), F32),
        scratch_shapes=[pltpu.VMEM((t + 8, c), F32)], compiler_params=_cp())(x, w, b)


def _conv_bwd_call(x, w, g):
    t, c = x.shape

    def body(x_ref, w_ref, g_ref, dx_ref, dw_ref, db_ref, xp, gp):
        xp[0:8, :] = jnp.zeros((8, c), F32)
        xp[8:, :] = x_ref[...]
        gp[0:t, :] = g_ref[...]
        gp[t:, :] = jnp.zeros((8, c), F32)
        dw = [jnp.zeros((1, c), F32) for _ in range(CONV_K)]
        db = jnp.zeros((1, c), F32)
        for blk in range(t // CONV_ROWS):
            base = blk * CONV_ROWS
            gb = g_ref[base:base + CONV_ROWS, :]
            acc = jnp.zeros((CONV_ROWS, c), F32)
            for kk in range(CONV_K):
                acc = acc + w_ref[kk:kk + 1, :] * gp[base + 3 - kk:base + 3 - kk + CONV_ROWS, :]
                dw[kk] = dw[kk] + jnp.sum(gb * xp[base + 5 + kk:base + 5 + kk + CONV_ROWS, :], axis=0, keepdims=True)
            db = db + jnp.sum(gb, axis=0, keepdims=True)
            dx_ref[base:base + CONV_ROWS, :] = acc
        for kk in range(CONV_K):
            dw_ref[kk:kk + 1, :] = dw[kk]
        db_ref[...] = db

    return pl.pallas_call(
        body, name="conv_b",
        out_shape=[jax.ShapeDtypeStruct((t, c), F32), jax.ShapeDtypeStruct((CONV_K, c), F32),
                   jax.ShapeDtypeStruct((1, c), F32)],
        scratch_shapes=[pltpu.VMEM((t + 8, c), F32)] * 2, compiler_params=_cp())(x, w, g)


@jax.custom_vjp
def causal_conv(x, w, b):
    return _conv_fwd_call(x, w, b)


def _causal_conv_fwd(x, w, b):
    return _conv_fwd_call(x, w, b), (x, w)


def _causal_conv_bwd(r, g):
    return tuple(_conv_bwd_call(r[0], r[1], g))


causal_conv.defvjp(_causal_conv_fwd, _causal_conv_bwd)


ATT_TILE = 512
ATT_SCALE = HEAD_DIM ** -0.5


def _head_lane(val, lane, h):
    return jnp.sum(jnp.where(lane == h, val, 0.0), axis=1, keepdims=True)


def _attn_fwd_call(q, k, v, c128, cr, next_shard=None):
    t, w = q.shape
    tq = ATT_TILE
    nq = t // tq
    k3, v3, cr4 = k.reshape(nq, tq, w), v.reshape(nq, tq, w), cr.reshape(N_HEADS, nq, 1, tq)
    fused = next_shard is not None

    def body(*refs):
        q_ref, k_ref, v_ref, c_ref, cr_ref = refs[:5]
        i, h = pl.program_id(0), pl.program_id(1)
        if fused:
            shard_ref, o_ref, lse_ref, gathered_ref, send_sems, recv_sems = refs[5:]
            start, finish = _allgather_copies(shard_ref, gathered_ref, send_sems, recv_sems)
            pl.when((i == 0) & (h == 0))(start)
        else:
            o_ref, lse_ref = refs[5:]
        hm = lax.broadcasted_iota(jnp.int32, (tq, w), 1) // HEAD_DIM == h
        lane = lax.broadcasted_iota(jnp.int32, (tq, 128), 1)
        qs = jnp.where(hm, q_ref[...] * ATT_SCALE, 0.0)
        cq = _head_lane(c_ref[...], lane, h)
        causal = lax.broadcasted_iota(jnp.int32, (tq, tq), 0) >= lax.broadcasted_iota(jnp.int32, (tq, tq), 1)

        def update(j, carry, diagonal):
            m, l, acc = carry
            s = _dot(qs, k_ref[j], 1, 1) + cq - cr_ref[0, j]
            if diagonal:
                s = jnp.where(causal, s, NEG)
            m_new = jnp.maximum(m, jnp.max(s, axis=1, keepdims=True))
            p = jnp.exp(s - m_new)
            alpha = jnp.exp(m - m_new)
            return m_new, alpha * l + jnp.sum(p, axis=1, keepdims=True), alpha * acc + _dot(p, v_ref[j], 1, 0)

        init = (jnp.full((tq, 1), NEG, F32), jnp.zeros((tq, 1), F32), jnp.zeros((tq, w), F32))
        carry = lax.fori_loop(0, i, lambda j, c: update(j, c, False), init)
        m, l, acc = update(i, carry, True)
        out = jnp.where(hm, acc / l, 0.0)
        lse = jnp.where(lane == h, m + jnp.log(l), 0.0)

        @pl.when(h == 0)
        def _():
            o_ref[...] = out
            lse_ref[...] = lse

        @pl.when(h > 0)
        def _():
            o_ref[...] += out
            lse_ref[...] += lse

        if fused:
            pl.when((i == nq - 1) & (h == N_HEADS - 1))(finish)

    tile = pl.BlockSpec((tq, w), lambda i, h: (i, 0))
    tile_c = pl.BlockSpec((tq, 128), lambda i, h: (i, 0))
    whole = pl.BlockSpec((nq, tq, w), lambda i, h: (0, 0, 0))
    rows = pl.BlockSpec((1, nq, 1, tq), lambda i, h: (h, 0, 0, 0))
    ins = [q.astype(BF16), k3.astype(BF16), v3.astype(BF16), c128, cr4]
    in_specs, out_specs = [tile, whole, whole, tile_c, rows], [tile, tile_c]
    out_shape = [jax.ShapeDtypeStruct((t, w), F32), jax.ShapeDtypeStruct((t, 128), F32)]
    scratch = []
    if fused:
        ins.append(next_shard)
        in_specs.append(HBM)
        out_specs.append(HBM)
        out_shape.append(jax.ShapeDtypeStruct((4,) + next_shard.shape, next_shard.dtype))
        scratch = [pltpu.SemaphoreType.DMA((AG_SEMS,)), pltpu.SemaphoreType.DMA((AG_SEMS,))]
    return pl.pallas_call(
        body, name="attn_f_allgather" if fused else "attn_f", grid=(nq, N_HEADS), in_specs=in_specs,
        out_specs=out_specs, out_shape=out_shape, scratch_shapes=scratch,
        compiler_params=_cp(("arbitrary", "arbitrary")))(*ins)


def _attn_bwd_call(q, k, v, c128, cr, o, lse, do, exchange=None):
    t, w = q.shape
    tq = ATT_TILE
    nq = t // tq
    r3 = lambda a: a.reshape(nq, tq, a.shape[-1])
    cr4 = cr.reshape(N_HEADS, nq, 1, tq)
    fused = exchange is not None

    def body(*refs):
        q_ref, k_ref, v_ref, c_ref, cr_ref, o_ref, lse_ref, do_ref = refs[:8]
        j, h = pl.program_id(0), pl.program_id(1)
        if fused:
            p_ref, dq_ref, dk_ref, dv_ref, dc_ref, dcr_ref, recv_ref, send_sems, recv_sems = refs[8:]
            start, finish = _chip_exchange_copies(p_ref, recv_ref, send_sems, recv_sems)
            pl.when((j == 0) & (h == 0))(start)
        else:
            dq_ref, dk_ref, dv_ref, dc_ref, dcr_ref = refs[8:]

        @pl.when((j == 0) & (h == 0))
        def _():
            dq_ref[...] = jnp.zeros_like(dq_ref)
            dc_ref[...] = jnp.zeros_like(dc_ref)

        hm = lax.broadcasted_iota(jnp.int32, (tq, w), 1) // HEAD_DIM == h
        lane = lax.broadcasted_iota(jnp.int32, (tq, 128), 1)
        kj = k_ref[...]
        vj = v_ref[...]
        ck = cr_ref[0, 0]
        causal = lax.broadcasted_iota(jnp.int32, (tq, tq), 0) >= lax.broadcasted_iota(jnp.int32, (tq, tq), 1)

        def step(i, carry, diagonal):
            dk, dv, dck = carry
            qm = jnp.where(hm, q_ref[i], 0.0)
            dom = jnp.where(hm, do_ref[i], 0.0)
            s = _dot(qm * ATT_SCALE, kj, 1, 1) + _head_lane(c_ref[i], lane, h) - ck
            if diagonal:
                s = jnp.where(causal, s, NEG)
            p = jnp.exp(s - _head_lane(lse_ref[i], lane, h))
            dv = dv + _dot(p, dom, 0, 0)
            dp = _dot(dom, vj, 1, 1)
            delta = jnp.sum(dom * o_ref[i], axis=1, keepdims=True)
            ds = p * (dp - delta)
            dq_ref[i] += jnp.where(hm, _dot(ds, kj, 1, 0), 0.0) * ATT_SCALE
            dk = dk + _dot(ds, qm, 0, 0) * ATT_SCALE
            dc_ref[i] += jnp.where(lane == h, jnp.sum(ds, axis=1, keepdims=True), 0.0)
            return dk, dv, dck - jnp.sum(ds, axis=0, keepdims=True)

        init = (jnp.zeros((tq, w), F32), jnp.zeros((tq, w), F32), jnp.zeros((1, tq), F32))
        carry = step(j, init, True)
        dk, dv, dck = lax.fori_loop(j + 1, nq, lambda i, c: step(i, c, False), carry)
        dcr_ref[0, 0] = dck

        @pl.when(h == 0)
        def _():
            dk_ref[...] = dk
            dv_ref[...] = dv

        @pl.when(h > 0)
        def _():
            dk_ref[...] += dk
            dv_ref[...] += dv

        if fused:
            pl.when((j == nq - 1) & (h == N_HEADS - 1))(finish)

    whole = pl.BlockSpec((nq, tq, w), lambda j, h: (0, 0, 0))
    whole_c = pl.BlockSpec((nq, tq, 128), lambda j, h: (0, 0, 0))
    tile = pl.BlockSpec((None, tq, w), lambda j, h: (j, 0, 0))
    tile_r = pl.BlockSpec((1, 1, 1, tq), lambda j, h: (h, j, 0, 0))
    s3 = jax.ShapeDtypeStruct((nq, tq, w), F32)
    b16 = lambda a: r3(a).astype(BF16)
    ins = [b16(q), b16(k), b16(v), r3(c128), cr4, r3(o), r3(lse), r3(do)]
    in_specs = [whole, tile, tile, whole_c, tile_r, whole, whole_c, whole]
    out_specs = [whole, tile, tile, whole_c, tile_r]
    out_shape = [s3, s3, s3, jax.ShapeDtypeStruct((nq, tq, 128), F32), jax.ShapeDtypeStruct((N_HEADS, nq, 1, tq), F32)]
    scratch = []
    if fused:
        ins.append(exchange)
        in_specs.append(HBM)
        out_specs.append(HBM)
        out_shape.append(jax.ShapeDtypeStruct((3,) + exchange.shape[1:], exchange.dtype))
        scratch = [pltpu.SemaphoreType.DMA((3,)), pltpu.SemaphoreType.DMA((3,))]
    dq, dk, dv, dc, dcr, *received = pl.pallas_call(
        body, name="attn_b_exchange" if fused else "attn_b", grid=(nq, N_HEADS), in_specs=in_specs,
        out_specs=out_specs, out_shape=out_shape, scratch_shapes=scratch,
        compiler_params=_cp(("arbitrary", "arbitrary")))(*ins)
    grads = (dq.reshape(t, w), dk.reshape(t, w), dv.reshape(t, w), dc.reshape(t, 128), dcr.reshape(N_HEADS, 1, t))
    return grads, (received[0] if fused else None)


def _loss_call(x, g, target):
    t, d = x.shape
    tr = 512

    def body(x_ref, g_ref, t_ref, loss_ref, dx_ref, dg_ref):
        tgt = t_ref[...]

        def f(xv, gv):
            return 0.5 * jnp.sum(jnp.mean(jnp.square(_rms(xv, gv) - tgt), axis=-1))

        val, vjp = jax.vjp(f, x_ref[...], g_ref[...])
        dx, dg = vjp(jnp.ones((), F32))
        dx_ref[...] = dx

        @pl.when(pl.program_id(0) == 0)
        def _():
            loss_ref[...] = jnp.zeros_like(loss_ref)
            dg_ref[...] = jnp.zeros_like(dg_ref)

        loss_ref[...] += jnp.full(loss_ref.shape, val, F32)
        dg_ref[...] += dg

    row = _row_spec(tr, d)
    return pl.pallas_call(
        body, name="loss_head", grid=(t // tr,), in_specs=[row, _full_spec(g), row],
        out_specs=[pl.BlockSpec((1, 128), lambda i: (0, 0)), row, _full_spec(g)],
        out_shape=[jax.ShapeDtypeStruct((1, 128), F32), jax.ShapeDtypeStruct((t, d), F32),
                   jax.ShapeDtypeStruct(g.shape, F32)],
        compiler_params=_cp(("arbitrary",)))(x, g, target)


def _blockdiag(w):
    l, g, a, b = w.shape
    return jnp.einsum('lgab,gk->lgakb', w, jnp.eye(g, dtype=w.dtype)).reshape(l, g * a, g * b)


def _prepare(rep):
    d = DEPTH
    w = MIXER_WIDTH
    rows = S5_GROUPS * S5_GROUP
    rep16 = lambda a: jnp.repeat(a, S5_GROUP, axis=1).reshape(d * rows, -1)
    bt = lambda b: b.transpose(0, 1, 3, 2).reshape(d * rows, S5_STATE)
    abar_re, abar_im, bb_re, bb_im = s5_disc(
        rep16(rep["s5_lambda_re"]), rep16(rep["s5_lambda_im"]), rep16(rep["s5_log_dt"][:, :, None]),
        bt(rep["s5_b_re"]), bt(rep["s5_b_im"]))
    g4 = lambda a: a.reshape(d, S5_GROUPS, S5_GROUP, S5_STATE)
    first = lambda a: g4(a)[:, :, 0, :].reshape(d, 1, S5_GROUPS * S5_STATE)
    cblk = lambda c: _blockdiag(c.transpose(0, 1, 3, 2))
    row = lambda a: a.reshape(d, 1, -1)
    return dict(
        sgu_norm_g=row(rep["sgu_norm_g"]), sgu_w=rep["sgu_w"],
        sgu_bias=jnp.repeat(rep["sgu_b"].transpose(0, 2, 1), HEAD_DIM, axis=2),
        abar_re=first(abar_re), abar_im=first(abar_im), bblk_re=_blockdiag(g4(bb_re)), bblk_im=_blockdiag(g4(bb_im)),
        cblk_re=cblk(rep["s5_c_re"]), cblk_im=cblk(rep["s5_c_im"]), s5_d=row(rep["s5_d"]), s5_glu_b=row(rep["s5_glu_b"]),
        lru_conv_b=row(rep["lru_conv_b"]), lru_wa=_blockdiag(rep["lru_wa"]), lru_ba=row(rep["lru_ba"]),
        lru_wx=_blockdiag(rep["lru_wx"]), lru_bx=row(rep["lru_bx"]), lru_lambda=row(rep["lru_lambda"]),
        fgate_b=jnp.pad(rep["fox_fgate_b"], ((0, 0), (0, 128 - N_HEADS)))[:, None, :])


PREPARED_FROM = ('sgu_norm_g', 'sgu_w', 'sgu_b', 's5_lambda_re', 's5_lambda_im', 's5_log_dt', 's5_b_re', 's5_b_im',
                 's5_c_re', 's5_c_im', 's5_d', 's5_glu_b', 'lru_conv_b', 'lru_wa', 'lru_ba', 'lru_wx', 'lru_bx',
                 'lru_lambda', 'fox_fgate_b')


def _mixers_pre(z, p, glu_w, conv_w):
    w = MIXER_WIDTH
    a_u, a_v, b_in, c_x, c_gate, d_q, d_k, d_v, d_f = jnp.split(z, [w * i for i in range(1, 9)], axis=1)
    sw = p["sgu_w"]
    (y_a,) = sgu_mix(a_u, a_v, p["sgu_norm_g"], sw[0], sw[1], sw[2], sw[3], p["sgu_bias"])
    s_re, s_im = lti_scan(b_in, p["bblk_re"], p["bblk_im"], p["abar_re"], p["abar_im"])
    (y_b,) = s5_post(s_re, s_im, b_in, p["cblk_re"], p["cblk_im"], p["s5_d"], glu_w, p["s5_glu_b"])
    xc = causal_conv(c_x, conv_w, p["lru_conv_b"])
    a, b = lru_pre(xc, p["lru_wa"], p["lru_ba"], p["lru_wx"], p["lru_bx"], p["lru_lambda"])
    (y_c,) = lru_post(tv_scan(a, b), c_gate)
    (log_f,) = log_sig(d_f, p["fgate_b"])
    c128 = tv_scan(jnp.ones_like(log_f), log_f)
    return y_a, y_b, y_c, d_q, d_k, d_v, c128, c128[:, :N_HEADS].T[:, None, :]


PACK_COLS = 1024
SHARD_SHAPE = {'w_mlp_in': (1024, 1024), 'w_mlp_out': (1024, 1024), 'w_out': (256, 1024), 'w_in': (1024, 513),
               's5_glu_w': (64, 256), 'lru_conv_w': (4, 64)}
SHARDED_AXIS = {'w_in': 1, 's5_glu_w': 0, 'lru_conv_w': 1, 'w_out': 0, 'w_mlp_in': 1, 'w_mlp_out': 0}
SHARD_ROWS = {n: -(-s[0] * s[1] // PACK_COLS) for n, s in SHARD_SHAPE.items()}
SHARD_OFF = {n: sum(list(SHARD_ROWS.values())[:i]) for i, n in enumerate(SHARD_SHAPE)}
LAYER_ROWS = 2880
SMALL_OFF = SHARD_OFF['w_in']
GLUE_ROWS = LAYER_ROWS - SMALL_OFF
assert SHARD_OFF['w_mlp_out'] == 1024 and SHARD_OFF['w_out'] == 2048 and SMALL_OFF % GLUE_ROWS == 0
assert SHARD_OFF['lru_conv_w'] + SHARD_ROWS['lru_conv_w'] <= LAYER_ROWS
PACK_ROWS = DEPTH * LAYER_ROWS
TOK = 1024
FF = 4 * D_MODEL


def _w3(i_of):
    return pl.BlockSpec((None, 1024, PACK_COLS), i_of)


def _tile2(rows, cols, i_of):
    return pl.BlockSpec((rows, cols), i_of)


def _layer_fwd(x, gathered, w_in, w_out, p, mix_p, next_shard):
    t = x.shape[0]
    nt = t // TOK
    f32 = lambda r, c: jax.ShapeDtypeStruct((r, c), F32)
    b16 = lambda r, c: jax.ShapeDtypeStruct((r, c), BF16)
    g1, g2, gm = p["norm1_g"][None, :], p["norm2_g"][None, :], p["mix_norm_g"][None, :]
    (h1,) = _rowwise(_f_rms, [x], [g1], [D_MODEL], name="rms_f", tr=512, dtype=BF16)
    z = _matmul("mm_in", (nt, 1, 1), h1, _tile2(TOK, D_MODEL, lambda i, j, k: (i, 0)),
                w_in, _tile2(D_MODEL, D_IN_PAD, lambda i, j, k: (0, 0)), (1, 0),
                [(f32(t, D_IN_PAD), _tile2(TOK, D_IN_PAD, lambda i, j, k: (i, 0)))])
    (y_a, y_b, y_c, d_q, d_k, d_v, c128, cr), mix_vjp = jax.vjp(_mixers_pre, z, *mix_p)
    attn_in = (d_q.astype(BF16), d_k.astype(BF16), d_v.astype(BF16), c128, cr)
    y_d, lse, *next_gathered = _attn_fwd_call(*attn_in, next_shard=next_shard)
    ys = (y_a, y_b, y_c, y_d)
    (yn,) = _rowwise(_f_gnorm, list(ys), [gm], [D_MODEL], name="gnorm_f", tr=512, dtype=BF16)
    x_tile = _tile2(TOK, D_MODEL, lambda i, j, k: (i, 0))
    x1 = _matmul("mm_out", (nt, 1, 1), yn, x_tile, w_out, _tile2(D_MODEL, D_MODEL, lambda i, j, k: (0, 0)), (1, 0),
                 [(f32(t, D_MODEL), x_tile)], extras=[(x, x_tile)], epilogue=lambda acc, r: (acc + r,))
    (h2,) = _rowwise(_f_rms, [x1], [g2], [D_MODEL], name="rms_f", tr=512, dtype=BF16)
    ff_tile = _tile2(TOK, 1024, lambda i, j, k: (i, j))

    act = _matmul("mm_up", (nt, FF // 1024, 1), h2, x_tile, gathered, _w3(lambda i, j, k: (j, 0, 0)), (1, 0),
                  [(b16(t, FF), ff_tile)], epilogue=lambda acc: (jnp.square(jnp.maximum(acc, 0.0)),))
    x2 = _matmul("mm_down", (nt, 1, FF // 1024), act, _tile2(TOK, 1024, lambda i, j, k: (i, k)),
                 gathered, _w3(lambda i, j, k: (k, 1, 0)), (1, 0),
                 [(f32(t, D_MODEL), x_tile)], extras=[(x1, x_tile)], epilogue=lambda acc, r: (acc + r,))
    res = (x, h1, mix_vjp, ys, attn_in, lse, yn, x1, h2, act)
    return x2, res, (next_gathered[0] if next_gathered else None)


def _layer_bwd(g, res, gathered, w_in, w_out, p, exchange):
    x, h1, mix_vjp, ys, attn_in, lse, yn, x1, h2, act = res
    send = lax.empty((4, LAYER_ROWS, PACK_COLS), BF16)
    t = x.shape[0]
    nt = t // TOK
    f32 = lambda r, c: jax.ShapeDtypeStruct((r, c), F32)
    g1, g2, gm = p["norm1_g"][None, :], p["norm2_g"][None, :], p["mix_norm_g"][None, :]
    x_tile = _tile2(TOK, D_MODEL, lambda i, j, k: (i, 0))
    ff_tile = _tile2(TOK, 1024, lambda i, j, k: (i, j))
    tok_k = _tile2(TOK, D_MODEL, lambda i, j, k: (k, 0))
    send_s = jax.ShapeDtypeStruct(send.shape, send.dtype)
    du = _matmul("mm_down_dx", (nt, FF // 1024, 1), g, x_tile, gathered, _w3(lambda i, j, k: (j, 1, 0)), (1, 1),
                 [(jax.ShapeDtypeStruct((t, FF), BF16), ff_tile)], extras=[(act, ff_tile)],
                 epilogue=lambda acc, a: (2.0 * jnp.sqrt(a.astype(F32)) * acc,))
    send = _matmul("mm_down_dw", (FF // 1024, 1, nt), act, _tile2(TOK, 1024, lambda i, j, k: (k, i)), g, tok_k, (0, 0),
                   [(send_s, _w3(lambda i, j, k: (i, 1, 0)))], into=send)
    send = _matmul("mm_up_dw", (1, FF // 1024, nt), h2, tok_k, du, _tile2(TOK, 1024, lambda i, j, k: (k, j)), (0, 0),
                   [(send_s, _w3(lambda i, j, k: (j, 0, 0)))], into=send)
    gain = _tile2(1, D_MODEL, lambda i, j, k: (0, 0))

    def norm_bwd(dh, xv, gv, through):
        _, vjp = jax.vjp(_rms, xv, gv)
        dxv, dgv = vjp(dh)
        return dxv + through, dgv

    g_mid, dg2 = _matmul("mm_up_dx", (nt, 1, FF // 1024), du, _tile2(TOK, 1024, lambda i, j, k: (i, k)),
                         gathered, _w3(lambda i, j, k: (k, 0, 0)), (1, 1),
                         [(f32(t, D_MODEL), x_tile), (f32(1, D_MODEL), gain)],
                         extras=[(x1, x_tile), (g2, gain), (g, x_tile)], epilogue=norm_bwd, summed=1)
    w_full = _tile2(D_MODEL, D_MODEL, lambda i, j, k: (0, 0))
    dyn = _matmul("mm_out_dx", (nt, 1, 1), g_mid, x_tile, w_out, w_full, (1, 1), [(f32(t, D_MODEL), x_tile)])
    quarter = D_MODEL // 4
    send = _matmul("mm_out_dw", (1, 1, nt), yn, tok_k, g_mid, tok_k, (0, 0),
                   [(send_s, pl.BlockSpec((4, quarter, PACK_COLS), lambda i, j, k: (0, SHARD_OFF['w_out'] // quarter, 0)))],
                   epilogue=lambda acc: (acc.reshape(4, quarter, PACK_COLS),), into=send, acc_shape=(D_MODEL, D_MODEL))
    dy_a, dy_b, dy_c, dy_d, dgm = _rowwise_vjp(_f_gnorm, list(ys), [gm], [dyn], name="gnorm_b", tr=512)
    d_attn_in, received = _attn_bwd_call(*attn_in, ys[3], lse, dy_d, exchange=exchange)
    dz, *d_mix = mix_vjp((dy_a, dy_b, dy_c, *d_attn_in))
    dz = dz.astype(BF16)
    z_tile = _tile2(TOK, D_IN_PAD, lambda i, j, k: (i, 0))
    d_w_in = _matmul("mm_in_dw", (1, 1, t // 512), h1, _tile2(512, D_MODEL, lambda i, j, k: (k, 0)),
                     dz, _tile2(512, D_IN_PAD, lambda i, j, k: (k, 0)), (0, 0),
                     [(f32(D_MODEL, D_IN_PAD), _tile2(D_MODEL, D_IN_PAD, lambda i, j, k: (0, 0)))])
    dx, dg1 = _matmul("mm_in_dx", (nt, 1, 1), dz, z_tile, w_in, _tile2(D_MODEL, D_IN_PAD, lambda i, j, k: (0, 0)), (1, 1),
                      [(f32(t, D_MODEL), x_tile), (f32(1, D_MODEL), gain)],
                      extras=[(x, x_tile), (g1, gain), (g_mid, x_tile)], epilogue=norm_bwd, summed=1)
    norms = dict(norm1_g=dg1[0], norm2_g=dg2[0], mix_norm_g=dgm[0])
    return dx, norms, d_mix, d_w_in, send, received


HBM = pl.BlockSpec(memory_space=pltpu.HBM)
D2D_CHUNKS = 15
ICI_CHUNKS = 5
VMEM_CHUNKS = 4


def _coords():
    return lax.axis_index("x"), lax.axis_index("y"), lax.axis_index("c")


def _other_chips(x, y):
    return [(1 - x, y), (x, 1 - y), (1 - x, 1 - y)]


def _start_chunks(make, rows, n):
    size = rows // n
    assert size * n == rows
    for k in range(n):
        make(pl.ds(k * size, size)).start()


AG_SEMS = 7


def _allgather_copies(in_ref, out_ref, send_sems, recv_sems):
    r = in_ref.shape[0]
    rh = r // 2
    x, y, c = _coords()
    me, sibling = (x, y, c), (x, y, 1 - c)
    chips = _other_chips(x, y)

    def half(px, py, pc, rows=pl.ds(0, rh)):
        return out_ref.at[2 * px + py, pl.ds(pc * rh + rows.start, rows.size), :]

    def copy(k, block, to, rows=pl.ds(0, rh), from_input=False):
        src = in_ref.at[pl.ds(block[2] * rh + rows.start, rows.size), :] if from_input else half(*block, rows)
        return pltpu.make_async_remote_copy(
            src_ref=src, dst_ref=half(*block, rows), send_sem=send_sems.at[k], recv_sem=recv_sems.at[k],
            device_id=to, device_id_type=MESH)

    def own(rows=pl.ds(0, r)):
        return pltpu.make_async_remote_copy(
            src_ref=in_ref.at[rows, :], dst_ref=out_ref.at[2 * x + y, rows, :], send_sem=send_sems.at[6],
            recv_sem=recv_sems.at[6], device_id=sibling, device_id_type=MESH)

    def start():
        for j, chip in enumerate(chips):
            _start_chunks(lambda rows: copy(j, me, (*chip, c), rows, from_input=True), rh, ICI_CHUNKS)
        _start_chunks(own, r, D2D_CHUNKS)

    def finish():
        for j, chip in enumerate(chips):
            copy(j, (*chip, c), me).wait_recv()
            _start_chunks(lambda rows: copy(3 + j, (*chip, c), sibling, rows), rh, D2D_CHUNKS)
        for j, chip in enumerate(chips):
            copy(3 + j, (*chip, 1 - c), me).wait_recv()
        for j, chip in enumerate(chips):
            copy(j, me, (*chip, c), from_input=True).wait_send()
            copy(3 + j, (*chip, c), sibling).wait_send()
        own().wait()

    return start, finish


def _allgather_shards(shard):
    def body(in_ref, out_ref, send_sems, recv_sems):
        start, finish = _allgather_copies(in_ref, out_ref, send_sems, recv_sems)
        start()
        finish()

    return pl.pallas_call(
        body, name="allgather_shards", out_shape=jax.ShapeDtypeStruct((4,) + shard.shape, shard.dtype),
        in_specs=[HBM], out_specs=HBM,
        scratch_shapes=[pltpu.SemaphoreType.DMA((AG_SEMS,)), pltpu.SemaphoreType.DMA((AG_SEMS,))],
        compiler_params=pltpu.CompilerParams())(shard)


def _pair_exchange(g):
    s, _, rh, cols = g.shape

    def body(g_ref, recv_ref, send_sem, recv_sem):
        x, y, c = _coords()

        def copy(slot, rows):
            return pltpu.make_async_remote_copy(
                src_ref=g_ref.at[slot, 1 - c, rows, :], dst_ref=recv_ref.at[slot, rows, :], send_sem=send_sem,
                recv_sem=recv_sem, device_id=(x, y, 1 - c), device_id_type=MESH)

        for slot in range(s):
            _start_chunks(lambda rows: copy(slot, rows), rh, VMEM_CHUNKS)
        pltpu.make_async_remote_copy(
            src_ref=g_ref.at[:, 1 - c], dst_ref=recv_ref, send_sem=send_sem, recv_sem=recv_sem,
            device_id=(x, y, 1 - c), device_id_type=MESH).wait()

    return pl.pallas_call(
        body, name="pair_exchange", out_shape=jax.ShapeDtypeStruct((s, rh, cols), g.dtype), in_specs=[HBM],
        out_specs=HBM, scratch_shapes=[pltpu.SemaphoreType.DMA] * 2, compiler_params=pltpu.CompilerParams())(g)


def _chip_exchange_copies(p_ref, recv_ref, send_sems, recv_sems):
    rh = p_ref.shape[1]
    x, y, c = _coords()
    chips = _other_chips(x, y)

    def copy(j, chip, rows=pl.ds(0, rh)):
        return pltpu.make_async_remote_copy(
            src_ref=p_ref.at[2 * chip[0] + chip[1], rows, :], dst_ref=recv_ref.at[j, rows, :],
            send_sem=send_sems.at[j], recv_sem=recv_sems.at[j], device_id=(*chip, c), device_id_type=MESH)

    def start():
        for j, chip in enumerate(chips):
            _start_chunks(lambda rows: copy(j, chip, rows), rh, ICI_CHUNKS)

    def finish():
        for j, chip in enumerate(chips):
            copy(j, chip).wait_recv()
        for j, chip in enumerate(chips):
            copy(j, chip).wait_send()

    return start, finish


def _chip_exchange(p):
    def body(p_ref, recv_ref, send_sems, recv_sems):
        start, finish = _chip_exchange_copies(p_ref, recv_ref, send_sems, recv_sems)
        start()
        finish()

    return pl.pallas_call(
        body, name="chip_exchange", out_shape=jax.ShapeDtypeStruct((3,) + p.shape[1:], p.dtype), in_specs=[HBM],
        out_specs=HBM, scratch_shapes=[pltpu.SemaphoreType.DMA((3,)), pltpu.SemaphoreType.DMA((3,))],
        compiler_params=pltpu.CompilerParams())(p)


def _sum_chips(p, recv, tr):
    _, rh, cols = p.shape
    x, y, c = _coords()
    where = jnp.stack([2 * x + y, c]).astype(jnp.int32)

    def body(w_ref, own_ref, r_ref, o_ref):
        acc = own_ref[...].astype(F32)
        for k in range(3):
            acc = acc + r_ref[k].astype(F32)
        o_ref[...] = acc

    return pl.pallas_call(
        body, name="sum_chips", out_shape=jax.ShapeDtypeStruct((2, rh, cols), F32),
        grid_spec=pltpu.PrefetchScalarGridSpec(
            num_scalar_prefetch=1, grid=(rh // tr,),
            in_specs=[pl.BlockSpec((None, tr, cols), lambda i, w_ref: (w_ref[0], i, 0)),
                      pl.BlockSpec((3, tr, cols), lambda i, w_ref: (0, i, 0))],
            out_specs=pl.BlockSpec((None, tr, cols), lambda i, w_ref: (w_ref[1], i, 0))),
        compiler_params=_cp(("arbitrary",)))(where, p, recv)


def _pair_share(buf):
    _, rh, cols = buf.shape

    def body(in_ref, out_ref, send_sem, recv_sem):
        x, y, c = _coords()

        def copy(slot, rows=pl.ds(0, rh)):
            return pltpu.make_async_remote_copy(
                src_ref=in_ref.at[slot, rows, :], dst_ref=out_ref.at[slot, rows, :], send_sem=send_sem,
                recv_sem=recv_sem, device_id=(x, y, 1 - c), device_id_type=MESH)

        _start_chunks(lambda rows: copy(c, rows), rh, D2D_CHUNKS)
        copy(c).wait_send()
        copy(1 - c).wait_recv()

    return pl.pallas_call(
        body, name="pair_share", out_shape=jax.ShapeDtypeStruct(buf.shape, buf.dtype), in_specs=[HBM], out_specs=HBM,
        scratch_shapes=[pltpu.SemaphoreType.DMA] * 2, input_output_aliases={0: 0},
        compiler_params=pltpu.CompilerParams())(buf)


def _allgather_all(blk):
    m_per, cols = blk.shape
    whole = pl.ds(0, m_per)

    def body(x_ref, out_ref, send_sems, recv_sems, local_sem):
        x, y, c = _coords()
        me, sibling = (x, y, c), (x, y, 1 - c)
        chips = _other_chips(x, y)

        def rows_of(px, py, pc, rows):
            return out_ref.at[4 * px + 2 * py + pc, rows, :]

        def copy(k, block, to, rows=whole, from_input=False):
            return pltpu.make_async_remote_copy(
                src_ref=x_ref.at[rows, :] if from_input else rows_of(*block, rows), dst_ref=rows_of(*block, rows),
                send_sem=send_sems.at[k], recv_sem=recv_sems.at[k], device_id=to, device_id_type=MESH)

        mine = pltpu.make_async_copy(x_ref, rows_of(*me, whole), local_sem)
        mine.start()
        _start_chunks(lambda rows: copy(0, me, sibling, rows, from_input=True), m_per, VMEM_CHUNKS)
        for j, chip in enumerate(chips):
            _start_chunks(lambda rows: copy(1 + j, me, (*chip, c), rows, from_input=True), m_per, VMEM_CHUNKS)
        for j, chip in enumerate(chips):
            copy(1 + j, (*chip, c), me).wait_recv()
            _start_chunks(lambda rows: copy(4 + j, (*chip, c), sibling, rows), m_per, VMEM_CHUNKS)
        copy(0, sibling, me).wait_recv()
        for j, chip in enumerate(chips):
            copy(4 + j, (*chip, 1 - c), me).wait_recv()
        copy(0, me, sibling, from_input=True).wait_send()
        for j, chip in enumerate(chips):
            copy(1 + j, me, (*chip, c), from_input=True).wait_send()
            copy(4 + j, (*chip, c), sibling).wait_send()
        mine.wait()

    return pl.pallas_call(
        body, name="allgather_all", out_shape=jax.ShapeDtypeStruct((8, m_per, cols), blk.dtype),
        in_specs=[pl.BlockSpec(memory_space=pltpu.VMEM)], out_specs=pl.BlockSpec(memory_space=pltpu.VMEM),
        scratch_shapes=[pltpu.SemaphoreType.DMA((7,)), pltpu.SemaphoreType.DMA((7,)), pltpu.SemaphoreType.DMA],
        compiler_params=pltpu.CompilerParams(vmem_limit_bytes=VMEM_LIMIT))(blk)


def _add_kept(g, recv, tr):
    s, _, rh, cols = g.shape

    def body(c_ref, a_ref, b_ref, o_ref):
        o_ref[...] = (a_ref[...].astype(F32) + b_ref[...].astype(F32)).astype(o_ref.dtype)

    spec = pl.BlockSpec((None, tr, cols), lambda si, i, c_ref: (si, i, 0))
    return pl.pallas_call(
        body, name="add_kept", out_shape=jax.ShapeDtypeStruct((s, rh, cols), BF16),
        grid_spec=pltpu.PrefetchScalarGridSpec(
            num_scalar_prefetch=1, grid=(s, rh // tr),
            in_specs=[pl.BlockSpec((None, None, tr, cols), lambda si, i, c_ref: (si, c_ref[0], i, 0)), spec],
            out_specs=spec),
        compiler_params=_cp(("arbitrary", "arbitrary")))(lax.axis_index("c").astype(jnp.int32).reshape(1), g, recv)


def _sum_slots(p, tr, name):
    s, rows, cols = p.shape

    def body(p_ref, o_ref):
        acc = p_ref[0].astype(F32)
        for k in range(1, s):
            acc = acc + p_ref[k].astype(F32)
        o_ref[...] = acc

    return pl.pallas_call(
        body, name=name, grid=(rows // tr,), in_specs=[pl.BlockSpec((s, tr, cols), lambda i: (0, i, 0))],
        out_specs=_row_spec(tr, cols), out_shape=jax.ShapeDtypeStruct((rows, cols), F32),
        compiler_params=_cp(("arbitrary",)))(p)


def _adamw_call(w, g, m, v, name):
    rows, cols = w.shape
    tr = _tile(rows, 512) if rows % 512 == 0 else _tile(rows, 128)
    c1 = 1.0 - ADAM_B1 ** ADAM_STEP
    c2 = 1.0 - ADAM_B2 ** ADAM_STEP

    def body(w_ref, g_ref, m_ref, v_ref, d_ref, nm_ref, nv_ref):
        gv = g_ref[...]
        nm = ADAM_B1 * m_ref[...] + (1.0 - ADAM_B1) * gv
        nv = ADAM_B2 * v_ref[...] + (1.0 - ADAM_B2) * jnp.square(gv)
        d_ref[...] = -ADAM_LR * ((nm / c1) / (jnp.sqrt(nv / c2) + ADAM_EPS) + ADAM_WD * w_ref[...])
        nm_ref[...] = nm
        nv_ref[...] = nv

    spec = _row_spec(tr, cols)
    o = jax.ShapeDtypeStruct((rows, cols), F32)
    return pl.pallas_call(body, name=name, grid=(rows // tr,), in_specs=[spec] * 4, out_specs=[spec] * 3,
                          out_shape=[o, o, o], compiler_params=_cp(("arbitrary",)))(w, g, m, v)


WEIGHTS = ('norm1_g', 'w_in', 'sgu_norm_g', 'sgu_w', 'sgu_b', 's5_lambda_re', 's5_lambda_im', 's5_log_dt',
           's5_b_re', 's5_b_im', 's5_c_re', 's5_c_im', 's5_d', 's5_glu_w', 's5_glu_b', 'lru_conv_w',
           'lru_conv_b', 'lru_wa', 'lru_ba', 'lru_wx', 'lru_bx', 'lru_lambda', 'fox_fgate_b', 'mix_norm_g',
           'w_out', 'norm2_g', 'w_mlp_in', 'w_mlp_out', 'final_g')
N_W = len(WEIGHTS)


def _pack_shards(shards, dtype, names=tuple(SHARD_SHAPE), rows=LAYER_ROWS):
    parts = []
    for n in names:
        lead = shards[n].shape[:-2]
        flat = shards[n].reshape(*lead, -1).astype(dtype)
        flat = jnp.pad(flat, [(0, 0)] * len(lead) + [(0, SHARD_ROWS[n] * PACK_COLS - flat.shape[-1])])
        parts.append(flat.reshape(*lead, SHARD_ROWS[n], PACK_COLS))
    lead = parts[0].shape[:-2]
    used = sum(SHARD_ROWS[n] for n in names)
    if rows > used:
        parts.append(jnp.zeros((*lead, rows - used, PACK_COLS), dtype))
    return jnp.concatenate(parts, axis=-2)


def _unpack_shards(buf, names=tuple(SHARD_SHAPE)):
    lead = buf.shape[:-2]
    out = {}
    for n in names:
        s0, s1 = SHARD_SHAPE[n]
        rows, off = SHARD_ROWS[n], SHARD_OFF[n]
        flat = buf[..., off:off + rows, :].reshape(*lead, rows * PACK_COLS)
        out[n] = flat[..., :s0 * s1].reshape(*lead, s0, s1)
    return out


def _join_chips(g, axis):
    _, d, s0, s1 = g.shape
    if axis == 0:
        return g.transpose(1, 0, 2, 3).reshape(d, 4 * s0, s1)
    return g.transpose(1, 2, 0, 3).reshape(d, s0, 4 * s1)


def _split_chips(w, axis):
    d = w.shape[0]
    if axis == 0:
        return w.reshape(d, 4, w.shape[1] // 4, w.shape[2]).transpose(1, 0, 2, 3)
    return w.reshape(d, w.shape[1], 4, w.shape[2] // 4).transpose(2, 0, 1, 3)


def _flat_rows(shape):
    return -(-math.prod(shape) // PACK_COLS)


def _pack_flat(arrs, rows, dtype=F32):
    parts = []
    for a in arrs:
        flat = a.reshape(-1).astype(dtype)
        r = _flat_rows(a.shape)
        parts.append(jnp.pad(flat, (0, r * PACK_COLS - flat.shape[0])).reshape(r, PACK_COLS))
    used = sum(p.shape[0] for p in parts)
    parts.append(jnp.zeros((rows - used, PACK_COLS), dtype))
    return jnp.concatenate(parts, axis=0)


def _split3(s):
    hi = s.astype(BF16).astype(F32)
    mid = (s - hi).astype(BF16).astype(F32)
    return jnp.stack([hi, mid, s - hi - mid])


def _unpack_flat(buf, shapes):
    out, off = [], 0
    for s in shapes:
        r = _flat_rows(s)
        out.append(buf[off:off + r].reshape(-1)[:math.prod(s)].reshape(s))
        off += r
    return out


def _write_glue(send, glue):
    def body(g_ref, s_ref, o_ref):
        o_ref[...] = g_ref[...]

    blk = (None, GLUE_ROWS, PACK_COLS)
    return pl.pallas_call(
        body, name="write_glue", grid=(4,),
        in_specs=[pl.BlockSpec(blk, lambda s: (s, 0, 0)), pl.BlockSpec(memory_space=pl.ANY)],
        out_specs=pl.BlockSpec(blk, lambda s: (s, SMALL_OFF // GLUE_ROWS, 0)),
        out_shape=jax.ShapeDtypeStruct(send.shape, send.dtype), input_output_aliases={1: 0},
        compiler_params=_cp(("arbitrary",)))(glue, send)


GLUE_PACKED = ('w_in', 's5_glu_w', 'lru_conv_w')
REDUCE_ROWS = LAYER_ROWS // 4


def _layer_weights(gathered):
    parts = _unpack_shards(gathered[:, None], GLUE_PACKED + ('w_out',))
    joined = {n: _join_chips(g, SHARDED_AXIS[n])[0] for n, g in parts.items()}
    joined['w_in'] = jnp.pad(joined['w_in'], ((0, 0), (0, D_IN_PAD - D_IN_PROJ)))
    return joined


def _reduce_start(send):
    halves = send.reshape(4, 2, LAYER_ROWS // 2, PACK_COLS)
    return _add_kept(halves, _pair_exchange(halves), REDUCE_ROWS)


def _reduce_finish(chip_sum, received):
    return _pair_share(_sum_chips(chip_sum, received, REDUCE_ROWS)).reshape(LAYER_ROWS, PACK_COLS)


def _forward_backward(x, target, final_g, shards, rep):
    norm_p = [{n: rep[n][l] for n in ('norm1_g', 'norm2_g', 'mix_norm_g')} for l in range(DEPTH)]
    prepared, prepare_vjp = jax.vjp(_prepare, {n: rep[n] for n in PREPARED_FROM})
    gathered = _allgather_shards(shards[0])
    layers = []
    for l in range(DEPTH):
        lw = _layer_weights(gathered)
        mix_p = ({n: a[l] for n, a in prepared.items()}, lw['s5_glu_w'].astype(F32), lw['lru_conv_w'].astype(F32))
        x, res, following = _layer_fwd(x, gathered, lw['w_in'], lw['w_out'], norm_p[l], mix_p,
                                       shards[l + 1] if l + 1 < DEPTH else None)
        layers.append((res, gathered, lw))
        gathered = following
    loss_part, g, d_final = _loss_call(x, final_g[None, :], target)

    norms, d_prepared, reduced, pending = [None] * DEPTH, [None] * DEPTH, [None] * DEPTH, None
    for l in reversed(range(DEPTH)):
        res, gathered, lw = layers[l]
        g, norms[l], (d_prepared[l], d_glu_w, d_conv_w), d_w_in, send, received = _layer_bwd(
            g, res, gathered, lw['w_in'], lw['w_out'], norm_p[l], pending)
        if pending is not None:
            reduced[l + 1] = _reduce_finish(pending, received)
        mine = {'w_in': d_w_in[:, :D_IN_PROJ], 's5_glu_w': d_glu_w, 'lru_conv_w': d_conv_w}
        glue = _pack_shards({n: _split_chips(a[None], SHARDED_AXIS[n]) for n, a in mine.items()}, BF16,
                            names=GLUE_PACKED, rows=GLUE_ROWS)[:, 0]
        pending = _reduce_start(_write_glue(send, glue))
    reduced[0] = _reduce_finish(pending, _chip_exchange(pending))
    stack = lambda per_layer: {n: jnp.stack([per_layer[l][n] for l in range(DEPTH)]) for n in per_layer[0]}
    (d_rep,) = prepare_vjp(stack(d_prepared))
    return loss_part, g, d_final, dict(d_rep, **stack(norms)), _unpack_shards(jnp.stack(reduced))


def _step(*args):
    x, target = args[0], args[1 + N_W]
    w = dict(zip(WEIGHTS, args[1:1 + N_W]))
    m = dict(zip(WEIGHTS, args[2 + N_W:2 + 2 * N_W]))
    v = dict(zip(WEIGHTS, args[2 + 2 * N_W:2 + 3 * N_W]))
    small = [n for n in WEIGHTS if n not in SHARD_SHAPE]

    shards = _pack_shards({n: w[n] for n in SHARD_SHAPE}, BF16)
    loss_part, dx, d_final, dw, g_shard = _forward_backward(
        x[0], target[0], w['final_g'], shards, {n: w[n] for n in small})

    small_g = [d_final.reshape(-1) if n == 'final_g' else dw[n] for n in small]
    small_rows = -(-(sum(_flat_rows(w[n].shape) for n in small) + 1) // 128) * 128
    mine = _pack_flat(small_g + [_split3(loss_part[0, 0])], small_rows, BF16)
    small_sum = _sum_slots(_allgather_all(mine), 128, "sum_devices")
    *g_small, loss = _unpack_flat(small_sum, [w[n].shape for n in small] + [(3,)])
    loss = jnp.sum(loss)

    grads, delta, new_m, new_v = {}, {}, {}, {}
    for n in SHARD_SHAPE:
        shp = w[n].shape
        v2 = lambda a: a.reshape(-1, shp[-1])
        res = _adamw_call(v2(w[n]), v2(g_shard[n]), v2(m[n]), v2(v[n]), "adamw_" + n)
        grads[n] = g_shard[n]
        delta[n], new_m[n], new_v[n] = (r.reshape(shp) for r in res)
    pk = lambda d: _pack_flat([d[n] for n in small], small_rows)
    res = _adamw_call(pk(w), _pack_flat(g_small, small_rows), pk(m), pk(v), "adamw_small")
    shapes = [w[n].shape for n in small]
    for n, g, d_, m_, v_ in zip(small, g_small, *(_unpack_flat(r, shapes) for r in res)):
        grads[n], delta[n], new_m[n], new_v[n] = g, d_, m_, v_

    return (loss, dx[None], *[grads[n] for n in WEIGHTS], *[delta[n] for n in WEIGHTS],
            *[new_m[n] for n in WEIGHTS], *[new_v[n] for n in WEIGHTS])


def kernel(x, norm1_g, w_in, sgu_norm_g, sgu_w, sgu_b, s5_lambda_re, s5_lambda_im, s5_log_dt, s5_b_re, s5_b_im, s5_c_re, s5_c_im, s5_d, s5_glu_w, s5_glu_b, lru_conv_w, lru_conv_b, lru_wa, lru_ba, lru_wx, lru_bx, lru_lambda, fox_fgate_b, mix_norm_g, w_out, norm2_g, w_mlp_in, w_mlp_out, final_g, loss_target, m_norm1_g, m_w_in, m_sgu_norm_g, m_sgu_w, m_sgu_b, m_s5_lambda_re, m_s5_lambda_im, m_s5_log_dt, m_s5_b_re, m_s5_b_im, m_s5_c_re, m_s5_c_im, m_s5_d, m_s5_glu_w, m_s5_glu_b, m_lru_conv_w, m_lru_conv_b, m_lru_wa, m_lru_ba, m_lru_wx, m_lru_bx, m_lru_lambda, m_fox_fgate_b, m_mix_norm_g, m_w_out, m_norm2_g, m_w_mlp_in, m_w_mlp_out, m_final_g, v_norm1_g, v_w_in, v_sgu_norm_g, v_sgu_w, v_sgu_b, v_s5_lambda_re, v_s5_lambda_im, v_s5_log_dt, v_s5_b_re, v_s5_b_im, v_s5_c_re, v_s5_c_im, v_s5_d, v_s5_glu_w, v_s5_glu_b, v_lru_conv_w, v_lru_conv_b, v_lru_wa, v_lru_ba, v_lru_wx, v_lru_bx, v_lru_lambda, v_fox_fgate_b, v_mix_norm_g, v_w_out, v_norm2_g, v_w_mlp_in, v_w_mlp_out, v_final_g):
    return _step(x, norm1_g, w_in, sgu_norm_g, sgu_w, sgu_b, s5_lambda_re, s5_lambda_im, s5_log_dt, s5_b_re, s5_b_im, s5_c_re, s5_c_im, s5_d, s5_glu_w, s5_glu_b, lru_conv_w, lru_conv_b, lru_wa, lru_ba, lru_wx, lru_bx, lru_lambda, fox_fgate_b, mix_norm_g, w_out, norm2_g, w_mlp_in, w_mlp_out, final_g, loss_target, m_norm1_g, m_w_in, m_sgu_norm_g, m_sgu_w, m_sgu_b, m_s5_lambda_re, m_s5_lambda_im, m_s5_log_dt, m_s5_b_re, m_s5_b_im, m_s5_c_re, m_s5_c_im, m_s5_d, m_s5_glu_w, m_s5_glu_b, m_lru_conv_w, m_lru_conv_b, m_lru_wa, m_lru_ba, m_lru_wx, m_lru_bx, m_lru_lambda, m_fox_fgate_b, m_mix_norm_g, m_w_out, m_norm2_g, m_w_mlp_in, m_w_mlp_out, m_final_g, v_norm1_g, v_w_in, v_sgu_norm_g, v_sgu_w, v_sgu_b, v_s5_lambda_re, v_s5_lambda_im, v_s5_log_dt, v_s5_b_re, v_s5_b_im, v_s5_c_re, v_s5_c_im, v_s5_d, v_s5_glu_w, v_s5_glu_b, v_lru_conv_w, v_lru_conv_b, v_lru_wa, v_lru_ba, v_lru_wx, v_lru_bx, v_lru_lambda, v_fox_fgate_b, v_mix_norm_g, v_w_out, v_norm2_g, v_w_mlp_in, v_w_mlp_out, v_final_g)
```

```python
import functools
import math

import jax
import jax.numpy as jnp
from jax import lax
from jax.experimental import pallas as pl
from jax.experimental.pallas import tpu as pltpu

F32 = jnp.float32
BF16 = jnp.bfloat16

DEPTH = 4
D_MODEL = 1024
MIXER_WIDTH = 256
SGU_CHUNK = 128
N_HEADS = 4
HEAD_DIM = 64
S5_GROUPS = 16
S5_GROUP = 16
S5_STATE = 64
LRU_C = 8.0
RMS_EPS = 1e-6
D_IN_PROJ = 8 * MIXER_WIDTH + N_HEADS
D_IN_PAD = 8 * MIXER_WIDTH + 128
ADAM_LR, ADAM_B1, ADAM_B2, ADAM_EPS, ADAM_WD, ADAM_STEP = 0.001, 0.9, 0.999, 1e-08, 0.01, 10

V7X_VMEM_BYTES = 64 * 1024 * 1024
VMEM_LIMIT = V7X_VMEM_BYTES - 8 * 1024 * 1024
NEG = -1e30
MESH = pl.DeviceIdType.MESH


def _cp(sem=None, **kw):
    return pltpu.CompilerParams(dimension_semantics=sem, vmem_limit_bytes=VMEM_LIMIT, **kw)


def _full_spec(a):
    nd = a.ndim
    return pl.BlockSpec(a.shape, lambda *_: (0,) * nd)


def _tile(n, pref=512):
    return pref if n % pref == 0 else n


def _dot(a, b, ca, cb):
    return lax.dot_general(a.astype(BF16), b.astype(BF16), (((ca,), (cb,)), ((), ())),
                           preferred_element_type=F32)


def _matmul(name, grid, a, a_spec, b, b_spec, dims, outs, *, extras=(), epilogue=None, into=None, summed=0,
            acc_shape=None, rider=None):
    nk = grid[2]
    n_ex, n_out = len(extras), len(outs)
    tm_tn = acc_shape or tuple(d for d in outs[0][1].block_shape if d is not None)[-2:]
    n_in = 2 + n_ex + (into is not None)

    def body(*refs):
        a_ref, b_ref = refs[0], refs[1]
        ex_refs = refs[2:2 + n_ex]
        o_refs = refs[n_in + (rider is not None):n_in + (rider is not None) + n_out]
        if rider is not None:
            start, finish_rider = rider[3](refs[n_in], refs[n_in + 1 + n_out], refs[-2], refs[-1])
            step = [pl.program_id(d) for d in range(3)]
            pl.when((step[0] == 0) & (step[1] == 0) & (step[2] == 0))(start)
        if summed:
            @pl.when((pl.program_id(0) == 0) & (pl.program_id(1) == 0) & (pl.program_id(2) == 0))
            def _():
                for o_ref in o_refs[n_out - summed:]:
                    o_ref[...] = jnp.zeros_like(o_ref)

        def finish(val):
            res = epilogue(val, *[e[...] for e in ex_refs]) if epilogue else (val,)
            for idx, (o_ref, r) in enumerate(zip(o_refs, res)):
                if idx >= n_out - summed:
                    o_ref[...] += r
                else:
                    o_ref[...] = r.astype(o_ref.dtype)

        if nk == 1:
            finish(_dot(a_ref[...], b_ref[...], *dims))
        else:
            acc = refs[n_in + (rider is not None) + n_out + (rider is not None)]
            kk = pl.program_id(2)

            @pl.when(kk == 0)
            def _():
                acc[...] = jnp.zeros_like(acc)

            acc[...] += _dot(a_ref[...], b_ref[...], *dims)

            @pl.when(kk == nk - 1)
            def _():
                finish(acc[...])

        if rider is not None:
            pl.when((step[0] == grid[0] - 1) & (step[1] == grid[1] - 1) & (step[2] == grid[2] - 1))(finish_rider)

    ins = [a, b] + [e[0] for e in extras]
    specs = [a_spec, b_spec] + [e[1] for e in extras]
    aliases = {}
    if into is not None:
        aliases = {len(ins): 0}
        ins.append(into)
        specs.append(pl.BlockSpec(memory_space=pl.ANY))
    out_specs, out_shape = [o[1] for o in outs], [o[0] for o in outs]
    scratch = [pltpu.VMEM(tm_tn, F32)] if nk > 1 else []
    if rider is not None:
        if rider[2]:
            aliases[len(ins)] = n_out
        ins.append(rider[0])
        specs.append(HBM)
        out_specs.append(HBM)
        out_shape.append(rider[1])
        scratch += [pltpu.SemaphoreType.DMA] * 2
        name += "_rider"
    res = pl.pallas_call(
        body, name=name, grid=grid, in_specs=specs, out_specs=out_specs, out_shape=out_shape,
        scratch_shapes=scratch, input_output_aliases=aliases,
        compiler_params=_cp(("arbitrary", "arbitrary", "arbitrary")))(*ins)
    return res[0] if len(res) == 1 else res


@jax.custom_vjp
def _bdot(a, b):
    return _dot(a, b, 1, 0)


def _bdot_fwd(a, b):
    return _dot(a, b, 1, 0), (a, b)


def _bdot_bwd(r, g):
    a, b = r
    return _dot(g, b, 1, 1), _dot(a, g, 0, 0)


_bdot.defvjp(_bdot_fwd, _bdot_bwd)


def _row_spec(tr, w):
    return pl.BlockSpec((tr, w), lambda i: (i, 0))


def _rowwise(fn, rows, pars, outs, *, name, tr, dtype=F32):
    t = rows[0].shape[0]
    n_in = len(rows) + len(pars)

    def body(*refs):
        res = fn(*[r[...] for r in refs[:n_in]])
        for o_ref, v in zip(refs[n_in:], res):
            o_ref[...] = v.astype(o_ref.dtype)

    return pl.pallas_call(
        body, name=name, grid=(t // tr,),
        in_specs=[_row_spec(tr, r.shape[1]) for r in rows] + [_full_spec(p) for p in pars],
        out_specs=[_row_spec(tr, w) for w in outs],
        out_shape=[jax.ShapeDtypeStruct((t, w), dtype) for w in outs],
        compiler_params=_cp(("arbitrary",)))(*rows, *pars)


def _rowwise_vjp(fn, rows, pars, cots, *, name, tr, add=None):
    t = rows[0].shape[0]
    nr, npar = len(rows), len(pars)
    cots = list(cots) + ([add] if add is not None else [])
    nc = len(cots)

    def body(*refs):
        vals = [r[...] for r in refs[:nr + npar]]
        cts = [c[...] for c in refs[nr + npar:nr + npar + nc]]
        douts = refs[nr + npar + nc:]
        extra = cts.pop() if add is not None else None
        _, vjp = jax.vjp(fn, *vals)
        grads = list(vjp(tuple(cts)))
        if extra is not None:
            grads[0] = grads[0] + extra
        for kk in range(nr):
            douts[kk][...] = grads[kk]

        @pl.when(pl.program_id(0) == 0)
        def _():
            for kk in range(npar):
                douts[nr + kk][...] = jnp.zeros_like(douts[nr + kk])

        for kk in range(npar):
            douts[nr + kk][...] += grads[nr + kk]

    return pl.pallas_call(
        body, name=name, grid=(t // tr,),
        in_specs=[_row_spec(tr, r.shape[1]) for r in rows] + [_full_spec(p) for p in pars]
        + [_row_spec(tr, c.shape[1]) for c in cots],
        out_specs=[_row_spec(tr, r.shape[1]) for r in rows] + [_full_spec(p) for p in pars],
        out_shape=[jax.ShapeDtypeStruct(r.shape, F32) for r in rows]
        + [jax.ShapeDtypeStruct(p.shape, F32) for p in pars],
        compiler_params=_cp(("arbitrary",)))(*rows, *pars, *cots)


def _make_rw(fn, name, tr, nr, outs):
    @jax.custom_vjp
    def f(*args):
        return tuple(_rowwise(fn, args[:nr], args[nr:], outs, name=name + "_f", tr=tr))

    def fwd(*args):
        return f(*args), args

    def bwd(args, cts):
        return tuple(_rowwise_vjp(fn, args[:nr], args[nr:], list(cts), name=name + "_b", tr=tr))

    f.defvjp(fwd, bwd)
    return f


def _rms(x, g):
    return x * lax.rsqrt(jnp.mean(jnp.square(x), axis=-1, keepdims=True) + RMS_EPS) * g


def _f_rms(x, g):
    return (_rms(x, g),)


def _f_sgu(au, av, ng, w0, w1, w2, w3, bfull):
    u = jax.nn.gelu(au)
    v = _rms(jax.nn.gelu(av), ng)
    tri = lax.broadcasted_iota(jnp.int32, (SGU_CHUNK, SGU_CHUNK), 0) >= lax.broadcasted_iota(
        jnp.int32, (SGU_CHUNK, SGU_CHUNK), 1)
    head = lax.broadcasted_iota(jnp.int32, v.shape, 1) // HEAD_DIM
    mixed = bfull
    for h, w in enumerate((w0, w1, w2, w3)):
        mixed = mixed + _bdot(jnp.where(tri, w, 0.0), jnp.where(head == h, v, 0.0))
    return (u * mixed,)


def _f_s5disc(lam_re, lam_im, log_dt, b_re, b_im):
    dt = jnp.exp(log_dt)
    mag = jnp.exp(lam_re * dt)
    abar_re = mag * jnp.cos(lam_im * dt)
    abar_im = mag * jnp.sin(lam_im * dt)
    denom = jnp.square(lam_re) + jnp.square(lam_im)
    num_re = abar_re - 1.0
    num_im = abar_im
    fac_re = (num_re * lam_re + num_im * lam_im) / denom
    fac_im = (num_im * lam_re - num_re * lam_im) / denom
    return abar_re, abar_im, fac_re * b_re - fac_im * b_im, fac_re * b_im + fac_im * b_re


def _f_s5post(s_re, s_im, u, c_re, c_im, d, gw, gb):
    y = _bdot(s_re, c_re) - _bdot(s_im, c_im) + d * u
    y = jax.nn.gelu(y)
    return (y * jax.nn.sigmoid(_bdot(y, gw) + gb),)


def _f_lrupre(xc, wa, ba, wx, bx, lam):
    r = jax.nn.sigmoid(_bdot(xc, wa) + ba)
    i = jax.nn.sigmoid(_bdot(xc, wx) + bx)
    log_a = -LRU_C * r * jax.nn.softplus(-lam)
    a = jnp.exp(log_a)
    one_minus_a2 = -jnp.tanh(log_a) * (jnp.exp(2.0 * log_a) + 1.0)
    return a, jnp.sqrt(one_minus_a2) * (i * xc)


def _f_lrupost(h, gate):
    return (h * jax.nn.gelu(gate),)


def _f_logsig(zf, bf):
    return (jax.nn.log_sigmoid(zf + bf),)


def _f_gnorm(ya, yb, yc, yd, g):
    def n(y):
        return y * lax.rsqrt(jnp.mean(jnp.square(y), axis=-1, keepdims=True) + RMS_EPS)
    return (jnp.concatenate([n(ya), n(yb), n(yc), n(yd)], axis=1) * g,)


sgu_mix = _make_rw(_f_sgu, "sgu", SGU_CHUNK, 2, [MIXER_WIDTH])
s5_disc = _make_rw(_f_s5disc, "s5disc", S5_GROUPS * S5_GROUP, 5, [S5_STATE] * 4)
s5_post = _make_rw(_f_s5post, "s5post", 256, 3, [MIXER_WIDTH])
lru_pre = _make_rw(_f_lrupre, "lrupre", 512, 1, [MIXER_WIDTH, MIXER_WIDTH])
lru_post = _make_rw(_f_lrupost, "lrupost", 512, 2, [MIXER_WIDTH])
log_sig = _make_rw(_f_logsig, "logsig", 512, 1, [128])


SCAN_TILE = 512


def _prev_spec(c, nt, rev):
    per = SCAN_TILE // 8
    if rev:
        return pl.BlockSpec((8, c), lambda i: (jnp.maximum((nt - 1 - i) * per - 1, 0), 0))
    return pl.BlockSpec((8, c), lambda i: (jnp.maximum(i * per - 1, 0), 0))


SCAN_STEPS = (1, 2, 4)


def _cmul(ar, ai, br, bi):
    return ar * br - ai * bi, ar * bi + ai * br


def _rows_down(x, k, fill, rowid):
    return jnp.where(rowid >= k, pltpu.roll(x, k, 0), fill)


def _rows_up(x, k, fill, rowid):
    return jnp.where(rowid < 8 - k, pltpu.roll(x, 8 - k, 0), fill)


def _powers(ar, ai):
    pw = [(ar, ai)]
    for _ in range(7):
        pw.append(_cmul(*pw[-1], ar, ai))
    return pw


def _block(i):
    return pl.ds(pl.multiple_of(i * 8, 8), 8)


def _row_before(ref, i, edge):
    return jnp.where(i == 0, edge, ref[pl.ds(jnp.maximum(i * 8 - 1, 0), 1), :])


def _lti_fwd_call(u, w_re, w_im, a_re, a_im):
    t, kdim = u.shape
    c = w_re.shape[1]
    tt = SCAN_TILE

    def body(u_ref, wr_ref, wi_ref, ar_ref, ai_ref, sr_ref, si_ref, br_ref, bi_ref, cr, ci):
        @pl.when(pl.program_id(0) == 0)
        def _():
            cr[...] = jnp.zeros_like(cr)
            ci[...] = jnp.zeros_like(ci)

        br_ref[...] = _dot(u_ref[...], wr_ref[...], 1, 0)
        bi_ref[...] = _dot(u_ref[...], wi_ref[...], 1, 0)
        pw = _powers(ar_ref[...], ai_ref[...])
        apr = jnp.concatenate([p[0] for p in pw], axis=0)
        api = jnp.concatenate([p[1] for p in pw], axis=0)
        rowid = lax.broadcasted_iota(jnp.int32, (8, c), 0)

        def block(i, carry):
            xr, xi = br_ref[_block(i), :], bi_ref[_block(i), :]
            for k in SCAN_STEPS:
                dr, di = _cmul(*pw[k - 1], _rows_down(xr, k, 0.0, rowid), _rows_down(xi, k, 0.0, rowid))
                xr, xi = xr + dr, xi + di
            dr, di = _cmul(apr, api, *carry)
            xr, xi = xr + dr, xi + di
            sr_ref[_block(i), :] = xr
            si_ref[_block(i), :] = xi
            return xr[7:8, :], xi[7:8, :]

        hr, hi = lax.fori_loop(0, tt // 8, block, (cr[...], ci[...]), unroll=2)
        cr[...] = hr
        ci[...] = hi

    row = pl.BlockSpec((tt, c), lambda i: (i, 0))
    par = pl.BlockSpec((1, c), lambda i: (0, 0))
    return pl.pallas_call(
        body, name="lti_scan_f", grid=(t // tt,),
        in_specs=[pl.BlockSpec((tt, kdim), lambda i: (i, 0)), _full_spec(w_re), _full_spec(w_im), par, par],
        out_specs=[row, row], out_shape=[jax.ShapeDtypeStruct((t, c), F32)] * 2,
        scratch_shapes=[pltpu.VMEM((tt, c), F32)] * 2 + [pltpu.VMEM((1, c), F32)] * 2,
        compiler_params=_cp(("arbitrary",)))(u, w_re, w_im, a_re, a_im)


def _lti_bwd_call(u, w_re, w_im, a_re, a_im, s_re, s_im, g_re, g_im):
    t, c = g_re.shape
    kdim = u.shape[1]
    tt = SCAN_TILE
    nt = t // tt
    nb = tt // 8

    def body(u_ref, wr_ref, wi_ref, ar_ref, ai_ref, sr_ref, si_ref, pr_ref, pi_ref, gr_ref, gi_ref,
             du_ref, dwr_ref, dwi_ref, dar_ref, dai_ref, or_ref, oi_ref, cr, ci):
        ti = pl.program_id(0)

        @pl.when(ti == 0)
        def _():
            cr[...] = jnp.zeros_like(cr)
            ci[...] = jnp.zeros_like(ci)
            dar_ref[...] = jnp.zeros_like(dar_ref)
            dai_ref[...] = jnp.zeros_like(dai_ref)
            dwr_ref[...] = jnp.zeros_like(dwr_ref)
            dwi_ref[...] = jnp.zeros_like(dwi_ref)

        pw = _powers(ar_ref[...], -ai_ref[...])
        tpr = jnp.concatenate([p[0] for p in reversed(pw)], axis=0)
        tpi = jnp.concatenate([p[1] for p in reversed(pw)], axis=0)
        rowid = lax.broadcasted_iota(jnp.int32, (8, c), 0)
        first = ti == nt - 1
        edge_r = jnp.where(first, 0.0, pr_ref[7:8, :])
        edge_i = jnp.where(first, 0.0, pi_ref[7:8, :])

        def block(kk, carry):
            i = nb - 1 - kk
            gr_c, gi_c, acc_r, acc_i = carry
            xr, xi = gr_ref[_block(i), :], gi_ref[_block(i), :]
            for k in SCAN_STEPS:
                dr, di = _cmul(*pw[k - 1], _rows_up(xr, k, 0.0, rowid), _rows_up(xi, k, 0.0, rowid))
                xr, xi = xr + dr, xi + di
            dr, di = _cmul(tpr, tpi, gr_c, gi_c)
            xr, xi = xr + dr, xi + di
            or_ref[_block(i), :] = xr
            oi_ref[_block(i), :] = xi
            spr = _rows_down(sr_ref[_block(i), :], 1, _row_before(sr_ref, i, edge_r), rowid)
            spi = _rows_down(si_ref[_block(i), :], 1, _row_before(si_ref, i, edge_i), rowid)
            return xr[0:1, :], xi[0:1, :], acc_r + spr * xr + spi * xi, acc_i + spr * xi - spi * xr

        zero = jnp.zeros((8, c), F32)
        gr_c, gi_c, acc_r, acc_i = lax.fori_loop(0, nb, block, (cr[...], ci[...], zero, zero), unroll=2)
        cr[...] = gr_c
        ci[...] = gi_c
        dar_ref[...] += jnp.sum(acc_r, axis=0, keepdims=True)
        dai_ref[...] += jnp.sum(acc_i, axis=0, keepdims=True)
        du_ref[...] = _dot(or_ref[...], wr_ref[...], 1, 1) + _dot(oi_ref[...], wi_ref[...], 1, 1)
        dwr_ref[...] += _dot(u_ref[...], or_ref[...], 0, 0)
        dwi_ref[...] += _dot(u_ref[...], oi_ref[...], 0, 0)

    row = pl.BlockSpec((tt, c), lambda i: (nt - 1 - i, 0))
    row_u = pl.BlockSpec((tt, kdim), lambda i: (nt - 1 - i, 0))
    par = pl.BlockSpec((1, c), lambda i: (0, 0))
    prev = _prev_spec(c, nt, True)
    return pl.pallas_call(
        body, name="lti_scan_b", grid=(nt,),
        in_specs=[row_u, _full_spec(w_re), _full_spec(w_im), par, par, row, row, prev, prev, row, row],
        out_specs=[row_u, _full_spec(w_re), _full_spec(w_im), par, par],
        out_shape=[jax.ShapeDtypeStruct((t, kdim), F32), jax.ShapeDtypeStruct(w_re.shape, F32),
                   jax.ShapeDtypeStruct(w_im.shape, F32)] + [jax.ShapeDtypeStruct((1, c), F32)] * 2,
        scratch_shapes=[pltpu.VMEM((tt, c), F32)] * 2 + [pltpu.VMEM((1, c), F32)] * 2,
        compiler_params=_cp(("arbitrary",)))(u, w_re, w_im, a_re, a_im, s_re, s_im, s_re, s_im, g_re, g_im)


@jax.custom_vjp
def lti_scan(u, w_re, w_im, a_re, a_im):
    return tuple(_lti_fwd_call(u, w_re, w_im, a_re, a_im))


def _lti_scan_fwd(u, w_re, w_im, a_re, a_im):
    s_re, s_im = _lti_fwd_call(u, w_re, w_im, a_re, a_im)
    return (s_re, s_im), (u, w_re, w_im, a_re, a_im, s_re, s_im)


def _lti_scan_bwd(r, g):
    return tuple(_lti_bwd_call(*r, g[0], g[1]))


lti_scan.defvjp(_lti_scan_fwd, _lti_scan_bwd)


def _tv_fwd_call(a, b):
    t, c = b.shape
    tt = SCAN_TILE

    def body(a_ref, b_ref, h_ref, ch):
        @pl.when(pl.program_id(0) == 0)
        def _():
            ch[...] = jnp.zeros_like(ch)

        rowid = lax.broadcasted_iota(jnp.int32, (8, c), 0)

        def block(i, h):
            ab, x = a_ref[_block(i), :], b_ref[_block(i), :]
            for k in SCAN_STEPS:
                x = x + ab * _rows_down(x, k, 0.0, rowid)
                ab = ab * _rows_down(ab, k, 1.0, rowid)
            x = x + ab * h
            h_ref[_block(i), :] = x
            return x[7:8, :]

        ch[...] = lax.fori_loop(0, tt // 8, block, ch[...], unroll=2)

    row = pl.BlockSpec((tt, c), lambda i: (i, 0))
    return pl.pallas_call(
        body, name="tv_scan_f", grid=(t // tt,), in_specs=[row, row], out_specs=row,
        out_shape=jax.ShapeDtypeStruct((t, c), F32), scratch_shapes=[pltpu.VMEM((1, c), F32)],
        compiler_params=_cp(("arbitrary",)))(a, b)


def _tv_bwd_call(a, h, g):
    t, c = g.shape
    tt = SCAN_TILE
    nt = t // tt
    nb = tt // 8

    def body(a_ref, h_ref, p_ref, g_ref, da_ref, db_ref, cg, ca):
        ti = pl.program_id(0)

        @pl.when(ti == 0)
        def _():
            cg[...] = jnp.zeros_like(cg)
            ca[...] = jnp.zeros_like(ca)

        rowid = lax.broadcasted_iota(jnp.int32, (8, c), 0)
        edge = jnp.where(ti == nt - 1, 0.0, p_ref[7:8, :])

        def block(kk, carry):
            i = nb - 1 - kk
            gc, a_next = carry
            ab, x = a_ref[_block(i), :], g_ref[_block(i), :]
            cb = _rows_up(ab, 1, a_next, rowid)
            for k in SCAN_STEPS:
                x = x + cb * _rows_up(x, k, 0.0, rowid)
                cb = cb * _rows_up(cb, k, 1.0, rowid)
            x = x + cb * gc
            db_ref[_block(i), :] = x
            da_ref[_block(i), :] = x * _rows_down(h_ref[_block(i), :], 1, _row_before(h_ref, i, edge), rowid)
            return x[0:1, :], ab[0:1, :]

        gc, a_next = lax.fori_loop(0, nb, block, (cg[...], ca[...]), unroll=2)
        cg[...] = gc
        ca[...] = a_next

    row = pl.BlockSpec((tt, c), lambda i: (nt - 1 - i, 0))
    return pl.pallas_call(
        body, name="tv_scan_b", grid=(nt,), in_specs=[row, row, _prev_spec(c, nt, True), row],
        out_specs=[row, row], out_shape=[jax.ShapeDtypeStruct((t, c), F32)] * 2,
        scratch_shapes=[pltpu.VMEM((1, c), F32)] * 2, compiler_params=_cp(("arbitrary",)))(a, h, h, g)


@jax.custom_vjp
def tv_scan(a, b):
    return _tv_fwd_call(a, b)


def _tv_scan_fwd(a, b):
    h = _tv_fwd_call(a, b)
    return h, (a, h)


def _tv_scan_bwd(r, g):
    a, h = r
    return tuple(_tv_bwd_call(a, h, g))


tv_scan.defvjp(_tv_scan_fwd, _tv_scan_bwd)


CONV_K = 4
CONV_ROWS = 512


def _conv_fwd_call(x, w, b):
    t, c = x.shape

    def body(x_ref, w_ref, b_ref, o_ref, xp):
        xp[0:8, :] = jnp.zeros((8, c), F32)
        xp[8:, :] = x_ref[...]
        for blk in range(t // CONV_ROWS):
            base = blk * CONV_ROWS
            acc = jnp.broadcast_to(b_ref[...], (CONV_ROWS, c))
            for kk in range(CONV_K):
                acc = acc + w_ref[kk:kk + 1, :] * xp[base + 5 + kk:base + 5 + kk + CONV_ROWS, :]
            o_ref[base:base + CONV_ROWS, :] = acc

    return pl.pallas_call(
        body, name="conv_f", out_shape=jax.ShapeDtypeStruct((t, c), F32),
        scratch_shapes=[pltpu.VMEM((t + 8, c), F32)], compiler_params=_cp())(x, w, b)


def _conv_bwd_call(x, w, g):
    t, c = x.shape

    def body(x_ref, w_ref, g_ref, dx_ref, dw_ref, db_ref, xp, gp):
        xp[0:8, :] = jnp.zeros((8, c), F32)
        xp[8:, :] = x_ref[...]
        gp[0:t, :] = g_ref[...]
        gp[t:, :] = jnp.zeros((8, c), F32)
        dw = [jnp.zeros((1, c), F32) for _ in range(CONV_K)]
        db = jnp.zeros((1, c), F32)
        for blk in range(t // CONV_ROWS):
            base = blk * CONV_ROWS
            gb = g_ref[base:base + CONV_ROWS, :]
            acc = jnp.zeros((CONV_ROWS, c), F32)
            for kk in range(CONV_K):
                acc = acc + w_ref[kk:kk + 1, :] * gp[base + 3 - kk:base + 3 - kk + CONV_ROWS, :]
                dw[kk] = dw[kk] + jnp.sum(gb * xp[base + 5 + kk:base + 5 + kk + CONV_ROWS, :], axis=0, keepdims=True)
            db = db + jnp.sum(gb, axis=0, keepdims=True)
            dx_ref[base:base + CONV_ROWS, :] = acc
        for kk in range(CONV_K):
            dw_ref[kk:kk + 1, :] = dw[kk]
        db_ref[...] = db

    return pl.pallas_call(
        body, name="conv_b",
        out_shape=[jax.ShapeDtypeStruct((t, c), F32), jax.ShapeDtypeStruct((CONV_K, c), F32),
                   jax.ShapeDtypeStruct((1, c), F32)],
        scratch_shapes=[pltpu.VMEM((t + 8, c), F32)] * 2, compiler_params=_cp())(x, w, g)


@jax.custom_vjp
def causal_conv(x, w, b):
    return _conv_fwd_call(x, w, b)


def _causal_conv_fwd(x, w, b):
    return _conv_fwd_call(x, w, b), (x, w)


def _causal_conv_bwd(r, g):
    return tuple(_conv_bwd_call(r[0], r[1], g))


causal_conv.defvjp(_causal_conv_fwd, _causal_conv_bwd)


ATT_TILE = 512
ATT_SCALE = HEAD_DIM ** -0.5


def _head_lane(val, lane, h):
    return jnp.sum(jnp.where(lane == h, val, 0.0), axis=1, keepdims=True)


def _attn_fwd_call(q, k, v, c128, cr, next_shard=None):
    t, w = q.shape
    tq = ATT_TILE
    nq = t // tq
    k3, v3, cr4 = k.reshape(nq, tq, w), v.reshape(nq, tq, w), cr.reshape(N_HEADS, nq, 1, tq)
    fused = next_shard is not None

    def body(*refs):
        q_ref, k_ref, v_ref, c_ref, cr_ref = refs[:5]
        i, h = pl.program_id(0), pl.program_id(1)
        if fused:
            shard_ref, o_ref, lse_ref, gathered_ref, send_sems, recv_sems = refs[5:]
            start, finish = _allgather_copies(shard_ref, gathered_ref, send_sems, recv_sems)
            pl.when((i == 0) & (h == 0))(start)
        else:
            o_ref, lse_ref = refs[5:]
        hm = lax.broadcasted_iota(jnp.int32, (tq, w), 1) // HEAD_DIM == h
        lane = lax.broadcasted_iota(jnp.int32, (tq, 128), 1)
        qs = jnp.where(hm, q_ref[...] * ATT_SCALE, 0.0)
        cq = _head_lane(c_ref[...], lane, h)
        causal = lax.broadcasted_iota(jnp.int32, (tq, tq), 0) >= lax.broadcasted_iota(jnp.int32, (tq, tq), 1)

        def update(j, carry, diagonal):
            m, l, acc = carry
            s = _dot(qs, k_ref[j], 1, 1) - cr_ref[0, j]
            if diagonal:
                s = jnp.where(causal, s, NEG)
            m_new = jnp.maximum(m, cq + jnp.max(s, axis=1, keepdims=True))
            p = jnp.exp(s + (cq - m_new))
            alpha = jnp.exp(m - m_new)
            return m_new, alpha * l + jnp.sum(p, axis=1, keepdims=True), alpha * acc + _dot(p, v_ref[j], 1, 0)

        init = (jnp.full((tq, 1), NEG, F32), jnp.zeros((tq, 1), F32), jnp.zeros((tq, w), F32))
        carry = lax.fori_loop(0, i, lambda j, c: update(j, c, False), init)
        m, l, acc = update(i, carry, True)
        out = jnp.where(hm, acc / l, 0.0)
        lse = jnp.where(lane == h, m + jnp.log(l), 0.0)

        @pl.when(h == 0)
        def _():
            o_ref[...] = out
            lse_ref[...] = lse

        @pl.when(h > 0)
        def _():
            o_ref[...] += out
            lse_ref[...] += lse

        if fused:
            pl.when((i == nq - 1) & (h == N_HEADS - 1))(finish)

    tile = pl.BlockSpec((tq, w), lambda i, h: (i, 0))
    tile_c = pl.BlockSpec((tq, 128), lambda i, h: (i, 0))
    whole = pl.BlockSpec((nq, tq, w), lambda i, h: (0, 0, 0))
    rows = pl.BlockSpec((1, nq, 1, tq), lambda i, h: (h, 0, 0, 0))
    ins = [q.astype(BF16), k3.astype(BF16), v3.astype(BF16), c128, cr4]
    in_specs, out_specs = [tile, whole, whole, tile_c, rows], [tile, tile_c]
    out_shape = [jax.ShapeDtypeStruct((t, w), F32), jax.ShapeDtypeStruct((t, 128), F32)]
    scratch = []
    if fused:
        ins.append(next_shard)
        in_specs.append(HBM)
        out_specs.append(HBM)
        out_shape.append(jax.ShapeDtypeStruct((4,) + next_shard.shape, next_shard.dtype))
        scratch = [pltpu.SemaphoreType.DMA((AG_SEMS,)), pltpu.SemaphoreType.DMA((AG_SEMS,))]
    return pl.pallas_call(
        body, name="attn_f_allgather" if fused else "attn_f", grid=(nq, N_HEADS), in_specs=in_specs,
        out_specs=out_specs, out_shape=out_shape, scratch_shapes=scratch,
        compiler_params=_cp(("arbitrary", "arbitrary")))(*ins)


def _attn_bwd_call(q, k, v, c128, cr, o, lse, do, exchange=None):
    t, w = q.shape
    tq = ATT_TILE
    nq = t // tq
    r3 = lambda a: a.reshape(nq, tq, a.shape[-1])
    cr4 = cr.reshape(N_HEADS, nq, 1, tq)
    fused = exchange is not None

    def body(*refs):
        q_ref, k_ref, v_ref, c_ref, cr_ref, o_ref, lse_ref, do_ref = refs[:8]
        j, h = pl.program_id(0), pl.program_id(1)
        if fused:
            p_ref, dq_ref, dk_ref, dv_ref, dc_ref, dcr_ref, recv_ref, send_sems, recv_sems = refs[8:]
            start, finish = _chip_exchange_copies(p_ref, recv_ref, send_sems, recv_sems)
            pl.when((j == 0) & (h == 0))(start)
        else:
            dq_ref, dk_ref, dv_ref, dc_ref, dcr_ref = refs[8:]

        @pl.when((j == 0) & (h == 0))
        def _():
            dq_ref[...] = jnp.zeros_like(dq_ref)
            dc_ref[...] = jnp.zeros_like(dc_ref)

        hm = lax.broadcasted_iota(jnp.int32, (tq, w), 1) // HEAD_DIM == h
        lane = lax.broadcasted_iota(jnp.int32, (tq, 128), 1)
        kj = k_ref[...]
        vj = v_ref[...]
        ck = cr_ref[0, 0]
        causal = lax.broadcasted_iota(jnp.int32, (tq, tq), 0) >= lax.broadcasted_iota(jnp.int32, (tq, tq), 1)

        def step(i, carry, diagonal):
            dk, dv, dck = carry
            qm = jnp.where(hm, q_ref[i], 0.0)
            dom = jnp.where(hm, do_ref[i], 0.0)
            s = _dot(qm * ATT_SCALE, kj, 1, 1) - ck
            if diagonal:
                s = jnp.where(causal, s, NEG)
            p = jnp.exp(s + (_head_lane(c_ref[i], lane, h) - _head_lane(lse_ref[i], lane, h)))
            dv = dv + _dot(p, dom, 0, 0)
            dp = _dot(dom, vj, 1, 1)
            delta = jnp.sum(dom * o_ref[i], axis=1, keepdims=True)
            ds = p * (dp - delta)
            dq_ref[i] += jnp.where(hm, _dot(ds, kj, 1, 0), 0.0) * ATT_SCALE
            dk = dk + _dot(ds, qm, 0, 0) * ATT_SCALE
            dc_ref[i] += jnp.where(lane == h, jnp.sum(ds, axis=1, keepdims=True), 0.0)
            return dk, dv, dck - jnp.sum(ds, axis=0, keepdims=True)

        init = (jnp.zeros((tq, w), F32), jnp.zeros((tq, w), F32), jnp.zeros((1, tq), F32))
        carry = step(j, init, True)
        dk, dv, dck = lax.fori_loop(j + 1, nq, lambda i, c: step(i, c, False), carry)
        dcr_ref[0, 0] = dck

        @pl.when(h == 0)
        def _():
            dk_ref[...] = dk
            dv_ref[...] = dv

        @pl.when(h > 0)
        def _():
            dk_ref[...] += dk
            dv_ref[...] += dv

        if fused:
            pl.when((j == nq - 1) & (h == N_HEADS - 1))(finish)

    whole = pl.BlockSpec((nq, tq, w), lambda j, h: (0, 0, 0))
    whole_c = pl.BlockSpec((nq, tq, 128), lambda j, h: (0, 0, 0))
    tile = pl.BlockSpec((None, tq, w), lambda j, h: (j, 0, 0))
    tile_r = pl.BlockSpec((1, 1, 1, tq), lambda j, h: (h, j, 0, 0))
    s3 = jax.ShapeDtypeStruct((nq, tq, w), F32)
    b16 = lambda a: r3(a).astype(BF16)
    ins = [b16(q), b16(k), b16(v), r3(c128), cr4, r3(o), r3(lse), r3(do)]
    in_specs = [whole, tile, tile, whole_c, tile_r, whole, whole_c, whole]
    out_specs = [whole, tile, tile, whole_c, tile_r]
    out_shape = [s3, s3, s3, jax.ShapeDtypeStruct((nq, tq, 128), F32), jax.ShapeDtypeStruct((N_HEADS, nq, 1, tq), F32)]
    scratch = []
    if fused:
        ins.append(exchange)
        in_specs.append(HBM)
        out_specs.append(HBM)
        out_shape.append(jax.ShapeDtypeStruct((3,) + exchange.shape[1:], exchange.dtype))
        scratch = [pltpu.SemaphoreType.DMA((3,)), pltpu.SemaphoreType.DMA((3,))]
    dq, dk, dv, dc, dcr, *received = pl.pallas_call(
        body, name="attn_b_exchange" if fused else "attn_b", grid=(nq, N_HEADS), in_specs=in_specs,
        out_specs=out_specs, out_shape=out_shape, scratch_shapes=scratch,
        compiler_params=_cp(("arbitrary", "arbitrary")))(*ins)
    grads = (dq.reshape(t, w), dk.reshape(t, w), dv.reshape(t, w), dc.reshape(t, 128), dcr.reshape(N_HEADS, 1, t))
    return grads, (received[0] if fused else None)


def _loss_call(x, g, target):
    t, d = x.shape
    tr = 512

    def body(x_ref, g_ref, t_ref, loss_ref, dx_ref, dg_ref):
        tgt = t_ref[...]

        def f(xv, gv):
            return 0.5 * jnp.sum(jnp.mean(jnp.square(_rms(xv, gv) - tgt), axis=-1))

        val, vjp = jax.vjp(f, x_ref[...], g_ref[...])
        dx, dg = vjp(jnp.ones((), F32))
        dx_ref[...] = dx

        @pl.when(pl.program_id(0) == 0)
        def _():
            loss_ref[...] = jnp.zeros_like(loss_ref)
            dg_ref[...] = jnp.zeros_like(dg_ref)

        loss_ref[...] += jnp.full(loss_ref.shape, val, F32)
        dg_ref[...] += dg

    row = _row_spec(tr, d)
    return pl.pallas_call(
        body, name="loss_head", grid=(t // tr,), in_specs=[row, _full_spec(g), row],
        out_specs=[pl.BlockSpec((1, 128), lambda i: (0, 0)), row, _full_spec(g)],
        out_shape=[jax.ShapeDtypeStruct((1, 128), F32), jax.ShapeDtypeStruct((t, d), F32),
                   jax.ShapeDtypeStruct(g.shape, F32)],
        compiler_params=_cp(("arbitrary",)))(x, g, target)


def _blockdiag(w):
    l, g, a, b = w.shape
    return jnp.einsum('lgab,gk->lgakb', w, jnp.eye(g, dtype=w.dtype)).reshape(l, g * a, g * b)


def _prepare(rep):
    d = DEPTH
    w = MIXER_WIDTH
    rows = S5_GROUPS * S5_GROUP
    rep16 = lambda a: jnp.repeat(a, S5_GROUP, axis=1).reshape(d * rows, -1)
    bt = lambda b: b.transpose(0, 1, 3, 2).reshape(d * rows, S5_STATE)
    abar_re, abar_im, bb_re, bb_im = s5_disc(
        rep16(rep["s5_lambda_re"]), rep16(rep["s5_lambda_im"]), rep16(rep["s5_log_dt"][:, :, None]),
        bt(rep["s5_b_re"]), bt(rep["s5_b_im"]))
    g4 = lambda a: a.reshape(d, S5_GROUPS, S5_GROUP, S5_STATE)
    first = lambda a: g4(a)[:, :, 0, :].reshape(d, 1, S5_GROUPS * S5_STATE)
    cblk = lambda c: _blockdiag(c.transpose(0, 1, 3, 2))
    row = lambda a: a.reshape(d, 1, -1)
    return dict(
        sgu_norm_g=row(rep["sgu_norm_g"]), sgu_w=rep["sgu_w"],
        sgu_bias=jnp.repeat(rep["sgu_b"].transpose(0, 2, 1), HEAD_DIM, axis=2),
        abar_re=first(abar_re), abar_im=first(abar_im), bblk_re=_blockdiag(g4(bb_re)), bblk_im=_blockdiag(g4(bb_im)),
        cblk_re=cblk(rep["s5_c_re"]), cblk_im=cblk(rep["s5_c_im"]), s5_d=row(rep["s5_d"]), s5_glu_b=row(rep["s5_glu_b"]),
        lru_conv_b=row(rep["lru_conv_b"]), lru_wa=_blockdiag(rep["lru_wa"]), lru_ba=row(rep["lru_ba"]),
        lru_wx=_blockdiag(rep["lru_wx"]), lru_bx=row(rep["lru_bx"]), lru_lambda=row(rep["lru_lambda"]),
        fgate_b=jnp.pad(rep["fox_fgate_b"], ((0, 0), (0, 128 - N_HEADS)))[:, None, :])


PREPARED_FROM = ('sgu_norm_g', 'sgu_w', 'sgu_b', 's5_lambda_re', 's5_lambda_im', 's5_log_dt', 's5_b_re', 's5_b_im',
                 's5_c_re', 's5_c_im', 's5_d', 's5_glu_b', 'lru_conv_b', 'lru_wa', 'lru_ba', 'lru_wx', 'lru_bx',
                 'lru_lambda', 'fox_fgate_b')


def _mixers_pre(pieces, p, glu_w, conv_w):
    a_u, a_v, b_in, c_x, c_gate, d_q, d_k, d_v, d_f = pieces
    sw = p["sgu_w"]
    (y_a,) = sgu_mix(a_u, a_v, p["sgu_norm_g"], sw[0], sw[1], sw[2], sw[3], p["sgu_bias"])
    s_re, s_im = lti_scan(b_in, p["bblk_re"], p["bblk_im"], p["abar_re"], p["abar_im"])
    (y_b,) = s5_post(s_re, s_im, b_in, p["cblk_re"], p["cblk_im"], p["s5_d"], glu_w, p["s5_glu_b"])
    xc = causal_conv(c_x, conv_w, p["lru_conv_b"])
    a, b = lru_pre(xc, p["lru_wa"], p["lru_ba"], p["lru_wx"], p["lru_bx"], p["lru_lambda"])
    (y_c,) = lru_post(tv_scan(a, b), c_gate)
    (log_f,) = log_sig(d_f, p["fgate_b"])
    c128 = tv_scan(jnp.ones_like(log_f), log_f)
    return y_a, y_b, y_c, d_q, d_k, d_v, c128, c128[:, :N_HEADS].T[:, None, :]


PACK_COLS = 1024
SHARD_SHAPE = {'w_mlp_in': (1024, 1024), 'w_mlp_out': (1024, 1024), 'w_out': (256, 1024), 'w_in': (1024, 513),
               's5_glu_w': (64, 256), 'lru_conv_w': (4, 64)}
SHARDED_AXIS = {'w_in': 1, 's5_glu_w': 0, 'lru_conv_w': 1, 'w_out': 0, 'w_mlp_in': 1, 'w_mlp_out': 0}
SHARD_ROWS = {n: -(-s[0] * s[1] // PACK_COLS) for n, s in SHARD_SHAPE.items()}
SHARD_OFF = {n: sum(list(SHARD_ROWS.values())[:i]) for i, n in enumerate(SHARD_SHAPE)}
LAYER_ROWS = 2880
SMALL_OFF = SHARD_OFF['w_in']
GLUE_ROWS = LAYER_ROWS - SMALL_OFF
assert SHARD_OFF['w_mlp_out'] == 1024 and SHARD_OFF['w_out'] == 2048 and SMALL_OFF % GLUE_ROWS == 0
assert SHARD_OFF['lru_conv_w'] + SHARD_ROWS['lru_conv_w'] <= LAYER_ROWS
PACK_ROWS = DEPTH * LAYER_ROWS
TOK = 1024
FF = 4 * D_MODEL


def _w3(i_of):
    return pl.BlockSpec((None, 1024, PACK_COLS), i_of)


def _tile2(rows, cols, i_of):
    return pl.BlockSpec((rows, cols), i_of)


def _layer_fwd(x, gathered, w_in, w_out, p, mix_p, next_shard):
    t = x.shape[0]
    nt = t // TOK
    f32 = lambda r, c: jax.ShapeDtypeStruct((r, c), F32)
    b16 = lambda r, c: jax.ShapeDtypeStruct((r, c), BF16)
    g1, g2, gm = p["norm1_g"][None, :], p["norm2_g"][None, :], p["mix_norm_g"][None, :]
    (h1,) = _rowwise(_f_rms, [x], [g1], [D_MODEL], name="rms_f", tr=512, dtype=BF16)
    z = _matmul("mm_in", (nt, 1, 1), h1, _tile2(TOK, D_MODEL, lambda i, j, k: (i, 0)),
                w_in, _tile2(D_MODEL, D_IN_PAD, lambda i, j, k: (0, 0)), (1, 0),
                [(f32(t, D_IN_PAD), _tile2(TOK, D_IN_PAD, lambda i, j, k: (i, 0)))])
    pieces = tuple(jnp.split(z, [MIXER_WIDTH * i for i in range(1, 9)], axis=1))
    (y_a, y_b, y_c, d_q, d_k, d_v, c128, cr), mix_vjp = jax.vjp(_mixers_pre, pieces, *mix_p)
    attn_in = (d_q.astype(BF16), d_k.astype(BF16), d_v.astype(BF16), c128, cr)
    y_d, lse, *next_gathered = _attn_fwd_call(*attn_in, next_shard=next_shard)
    ys = (y_a, y_b, y_c, y_d)
    (yn,) = _rowwise(_f_gnorm, list(ys), [gm], [D_MODEL], name="gnorm_f", tr=512, dtype=BF16)
    x_tile = _tile2(TOK, D_MODEL, lambda i, j, k: (i, 0))
    x1 = _matmul("mm_out", (nt, 1, 1), yn, x_tile, w_out, _tile2(D_MODEL, D_MODEL, lambda i, j, k: (0, 0)), (1, 0),
                 [(f32(t, D_MODEL), x_tile)], extras=[(x, x_tile)], epilogue=lambda acc, r: (acc + r,))
    (h2,) = _rowwise(_f_rms, [x1], [g2], [D_MODEL], name="rms_f", tr=512, dtype=BF16)
    ff_tile = _tile2(TOK, 1024, lambda i, j, k: (i, j))

    act = _matmul("mm_up", (nt, FF // 1024, 1), h2, x_tile, gathered, _w3(lambda i, j, k: (j, 0, 0)), (1, 0),
                  [(b16(t, FF), ff_tile)], epilogue=lambda acc: (jnp.square(jnp.maximum(acc, 0.0)),))
    x2 = _matmul("mm_down", (nt, 1, FF // 1024), act, _tile2(TOK, 1024, lambda i, j, k: (i, k)),
                 gathered, _w3(lambda i, j, k: (k, 1, 0)), (1, 0),
                 [(f32(t, D_MODEL), x_tile)], extras=[(x1, x_tile)], epilogue=lambda acc, r: (acc + r,))
    res = (x, h1, mix_vjp, ys, attn_in, lse, yn, x1, h2, act)
    return x2, res, (next_gathered[0] if next_gathered else None)


def _layer_bwd(g, res, gathered, w_in, w_out, p, later_send):
    x, h1, mix_vjp, ys, attn_in, lse, yn, x1, h2, act = res
    half_rows = LAYER_ROWS // 2
    riding = later_send is not None
    halves = later_send.reshape(4, 2, half_rows, PACK_COLS) if riding else None
    send = lax.empty((4, LAYER_ROWS, PACK_COLS), BF16)
    t = x.shape[0]
    nt = t // TOK
    f32 = lambda r, c: jax.ShapeDtypeStruct((r, c), F32)
    g1, g2, gm = p["norm1_g"][None, :], p["norm2_g"][None, :], p["mix_norm_g"][None, :]
    x_tile = _tile2(TOK, D_MODEL, lambda i, j, k: (i, 0))
    ff_tile = _tile2(TOK, 1024, lambda i, j, k: (i, j))
    tok_k = _tile2(TOK, D_MODEL, lambda i, j, k: (k, 0))
    send_s = jax.ShapeDtypeStruct(send.shape, send.dtype)
    pair_rider = (halves, jax.ShapeDtypeStruct((4, half_rows, PACK_COLS), BF16), False, _pair_exchange_copies)
    du = _matmul("mm_down_dx", (nt, FF // 1024, 1), g, x_tile, gathered, _w3(lambda i, j, k: (j, 1, 0)), (1, 1),
                 [(jax.ShapeDtypeStruct((t, FF), BF16), ff_tile)], extras=[(act, ff_tile)],
                 epilogue=lambda acc, a: (2.0 * jnp.sqrt(a.astype(F32)) * acc,),
                 rider=pair_rider if riding else None)
    exchange = None
    if riding:
        du, from_sibling = du
        exchange = _add_kept(halves, from_sibling, REDUCE_ROWS)
    send = _matmul("mm_down_dw", (FF // 1024, 1, nt), act, _tile2(TOK, 1024, lambda i, j, k: (k, i)), g, tok_k, (0, 0),
                   [(send_s, _w3(lambda i, j, k: (i, 1, 0)))], into=send)
    send = _matmul("mm_up_dw", (1, FF // 1024, nt), h2, tok_k, du, _tile2(TOK, 1024, lambda i, j, k: (k, j)), (0, 0),
                   [(send_s, _w3(lambda i, j, k: (j, 0, 0)))], into=send)
    gain = _tile2(1, D_MODEL, lambda i, j, k: (0, 0))

    def norm_bwd(dh, xv, gv, through):
        _, vjp = jax.vjp(_rms, xv, gv)
        dxv, dgv = vjp(dh)
        return dxv + through, dgv

    g_mid, dg2 = _matmul("mm_up_dx", (nt, 1, FF // 1024), du, _tile2(TOK, 1024, lambda i, j, k: (i, k)),
                         gathered, _w3(lambda i, j, k: (k, 0, 0)), (1, 1),
                         [(f32(t, D_MODEL), x_tile), (f32(1, D_MODEL), gain)],
                         extras=[(x1, x_tile), (g2, gain), (g, x_tile)], epilogue=norm_bwd, summed=1)
    w_full = _tile2(D_MODEL, D_MODEL, lambda i, j, k: (0, 0))
    dyn = _matmul("mm_out_dx", (nt, 1, 1), g_mid, x_tile, w_out, w_full, (1, 1), [(f32(t, D_MODEL), x_tile)])
    quarter = D_MODEL // 4
    send = _matmul("mm_out_dw", (1, 1, nt), yn, tok_k, g_mid, tok_k, (0, 0),
                   [(send_s, pl.BlockSpec((4, quarter, PACK_COLS), lambda i, j, k: (0, SHARD_OFF['w_out'] // quarter, 0)))],
                   epilogue=lambda acc: (acc.reshape(4, quarter, PACK_COLS),), into=send, acc_shape=(D_MODEL, D_MODEL))
    dy_a, dy_b, dy_c, dy_d, dgm = _rowwise_vjp(_f_gnorm, list(ys), [gm], [dyn], name="gnorm_b", tr=512)
    d_attn_in, received = _attn_bwd_call(*attn_in, ys[3], lse, dy_d, exchange=exchange)
    d_pieces, *d_mix = mix_vjp((dy_a, dy_b, dy_c, *d_attn_in))
    dz = jnp.concatenate([d.astype(BF16) for d in d_pieces], axis=1)
    z_tile = _tile2(TOK, D_IN_PAD, lambda i, j, k: (i, 0))
    share_rider = None
    if riding:
        pair = _sum_chips(exchange, received, REDUCE_ROWS)
        share_rider = (pair, jax.ShapeDtypeStruct(pair.shape, pair.dtype), True, _pair_share_copies)
    d_w_in = _matmul("mm_in_dw", (1, 1, t // 512), h1, _tile2(512, D_MODEL, lambda i, j, k: (k, 0)),
                     dz, _tile2(512, D_IN_PAD, lambda i, j, k: (k, 0)), (0, 0),
                     [(f32(D_MODEL, D_IN_PAD), _tile2(D_MODEL, D_IN_PAD, lambda i, j, k: (0, 0)))], rider=share_rider)
    reduced = None
    if riding:
        d_w_in, pair = d_w_in
        reduced = pair.reshape(LAYER_ROWS, PACK_COLS)
    dx, dg1 = _matmul("mm_in_dx", (nt, 1, 1), dz, z_tile, w_in, _tile2(D_MODEL, D_IN_PAD, lambda i, j, k: (0, 0)), (1, 1),
                      [(f32(t, D_MODEL), x_tile), (f32(1, D_MODEL), gain)],
                      extras=[(x, x_tile), (g1, gain), (g_mid, x_tile)], epilogue=norm_bwd, summed=1)
    norms = dict(norm1_g=dg1[0], norm2_g=dg2[0], mix_norm_g=dgm[0])
    return dx, norms, d_mix, d_w_in, send, reduced


HBM = pl.BlockSpec(memory_space=pltpu.HBM)
D2D_CHUNKS = 15
ICI_CHUNKS = 5
VMEM_CHUNKS = 4


def _coords():
    return lax.axis_index("x"), lax.axis_index("y"), lax.axis_index("c")


def _other_chips(x, y):
    return [(1 - x, y), (x, 1 - y), (1 - x, 1 - y)]


def _start_chunks(make, rows, n):
    size = rows // n
    assert size * n == rows
    for k in range(n):
        make(pl.ds(k * size, size)).start()


AG_SEMS = 7


def _allgather_copies(in_ref, out_ref, send_sems, recv_sems):
    r = in_ref.shape[0]
    rh = r // 2
    x, y, c = _coords()
    me, sibling = (x, y, c), (x, y, 1 - c)
    chips = _other_chips(x, y)

    def half(px, py, pc, rows=pl.ds(0, rh)):
        return out_ref.at[2 * px + py, pl.ds(pc * rh + rows.start, rows.size), :]

    def copy(k, block, to, rows=pl.ds(0, rh), from_input=False):
        src = in_ref.at[pl.ds(block[2] * rh + rows.start, rows.size), :] if from_input else half(*block, rows)
        return pltpu.make_async_remote_copy(
            src_ref=src, dst_ref=half(*block, rows), send_sem=send_sems.at[k], recv_sem=recv_sems.at[k],
            device_id=to, device_id_type=MESH)

    def own(rows=pl.ds(0, r)):
        return pltpu.make_async_remote_copy(
            src_ref=in_ref.at[rows, :], dst_ref=out_ref.at[2 * x + y, rows, :], send_sem=send_sems.at[6],
            recv_sem=recv_sems.at[6], device_id=sibling, device_id_type=MESH)

    def start():
        for j, chip in enumerate(chips):
            _start_chunks(lambda rows: copy(j, me, (*chip, c), rows, from_input=True), rh, ICI_CHUNKS)
        _start_chunks(own, r, D2D_CHUNKS)

    def finish():
        for j, chip in enumerate(chips):
            copy(j, (*chip, c), me).wait_recv()
            _start_chunks(lambda rows: copy(3 + j, (*chip, c), sibling, rows), rh, D2D_CHUNKS)
        for j, chip in enumerate(chips):
            copy(3 + j, (*chip, 1 - c), me).wait_recv()
        for j, chip in enumerate(chips):
            copy(j, me, (*chip, c), from_input=True).wait_send()
            copy(3 + j, (*chip, c), sibling).wait_send()
        own().wait()

    return start, finish


def _allgather_shards(shard):
    def body(in_ref, out_ref, send_sems, recv_sems):
        start, finish = _allgather_copies(in_ref, out_ref, send_sems, recv_sems)
        start()
        finish()

    return pl.pallas_call(
        body, name="allgather_shards", out_shape=jax.ShapeDtypeStruct((4,) + shard.shape, shard.dtype),
        in_specs=[HBM], out_specs=HBM,
        scratch_shapes=[pltpu.SemaphoreType.DMA((AG_SEMS,)), pltpu.SemaphoreType.DMA((AG_SEMS,))],
        compiler_params=pltpu.CompilerParams())(shard)


def _pair_exchange_copies(g_ref, recv_ref, send_sem, recv_sem):
    s, _, rh, _ = g_ref.shape
    x, y, c = _coords()

    def copy(slot, rows):
        return pltpu.make_async_remote_copy(
            src_ref=g_ref.at[slot, 1 - c, rows, :], dst_ref=recv_ref.at[slot, rows, :], send_sem=send_sem,
            recv_sem=recv_sem, device_id=(x, y, 1 - c), device_id_type=MESH)

    def start():
        for slot in range(s):
            _start_chunks(lambda rows: copy(slot, rows), rh, VMEM_CHUNKS)

    def finish():
        pltpu.make_async_remote_copy(
            src_ref=g_ref.at[:, 1 - c], dst_ref=recv_ref, send_sem=send_sem, recv_sem=recv_sem,
            device_id=(x, y, 1 - c), device_id_type=MESH).wait()

    return start, finish


def _pair_share_copies(in_ref, out_ref, send_sem, recv_sem):
    rh = in_ref.shape[1]
    x, y, c = _coords()

    def copy(slot, rows=pl.ds(0, rh)):
        return pltpu.make_async_remote_copy(
            src_ref=in_ref.at[slot, rows, :], dst_ref=out_ref.at[slot, rows, :], send_sem=send_sem,
            recv_sem=recv_sem, device_id=(x, y, 1 - c), device_id_type=MESH)

    def start():
        _start_chunks(lambda rows: copy(c, rows), rh, D2D_CHUNKS)

    def finish():
        copy(c).wait_send()
        copy(1 - c).wait_recv()

    return start, finish


def _pair_exchange(g):
    s, _, rh, cols = g.shape

    def body(g_ref, recv_ref, send_sem, recv_sem):
        start, finish = _pair_exchange_copies(g_ref, recv_ref, send_sem, recv_sem)
        start()
        finish()

    return pl.pallas_call(
        body, name="pair_exchange", out_shape=jax.ShapeDtypeStruct((s, rh, cols), g.dtype), in_specs=[HBM],
        out_specs=HBM, scratch_shapes=[pltpu.SemaphoreType.DMA] * 2, compiler_params=pltpu.CompilerParams())(g)


def _chip_exchange_copies(p_ref, recv_ref, send_sems, recv_sems):
    rh = p_ref.shape[1]
    x, y, c = _coords()
    chips = _other_chips(x, y)

    def copy(j, chip, rows=pl.ds(0, rh)):
        return pltpu.make_async_remote_copy(
            src_ref=p_ref.at[2 * chip[0] + chip[1], rows, :], dst_ref=recv_ref.at[j, rows, :],
            send_sem=send_sems.at[j], recv_sem=recv_sems.at[j], device_id=(*chip, c), device_id_type=MESH)

    def start():
        for j, chip in enumerate(chips):
            _start_chunks(lambda rows: copy(j, chip, rows), rh, ICI_CHUNKS)

    def finish():
        for j, chip in enumerate(chips):
            copy(j, chip).wait_recv()
        for j, chip in enumerate(chips):
            copy(j, chip).wait_send()

    return start, finish


def _chip_exchange(p):
    def body(p_ref, recv_ref, send_sems, recv_sems):
        start, finish = _chip_exchange_copies(p_ref, recv_ref, send_sems, recv_sems)
        start()
        finish()

    return pl.pallas_call(
        body, name="chip_exchange", out_shape=jax.ShapeDtypeStruct((3,) + p.shape[1:], p.dtype), in_specs=[HBM],
        out_specs=HBM, scratch_shapes=[pltpu.SemaphoreType.DMA((3,)), pltpu.SemaphoreType.DMA((3,))],
        compiler_params=pltpu.CompilerParams())(p)


def _sum_chips(p, recv, tr):
    _, rh, cols = p.shape
    x, y, c = _coords()
    where = jnp.stack([2 * x + y, c]).astype(jnp.int32)

    def body(w_ref, own_ref, r_ref, o_ref):
        acc = own_ref[...].astype(F32)
        for k in range(3):
            acc = acc + r_ref[k].astype(F32)
        o_ref[...] = acc

    return pl.pallas_call(
        body, name="sum_chips", out_shape=jax.ShapeDtypeStruct((2, rh, cols), F32),
        grid_spec=pltpu.PrefetchScalarGridSpec(
            num_scalar_prefetch=1, grid=(rh // tr,),
            in_specs=[pl.BlockSpec((None, tr, cols), lambda i, w_ref: (w_ref[0], i, 0)),
                      pl.BlockSpec((3, tr, cols), lambda i, w_ref: (0, i, 0))],
            out_specs=pl.BlockSpec((None, tr, cols), lambda i, w_ref: (w_ref[1], i, 0))),
        compiler_params=_cp(("arbitrary",)))(where, p, recv)


def _pair_share(buf):
    _, rh, cols = buf.shape

    def body(in_ref, out_ref, send_sem, recv_sem):
        start, finish = _pair_share_copies(in_ref, out_ref, send_sem, recv_sem)
        start()
        finish()

    return pl.pallas_call(
        body, name="pair_share", out_shape=jax.ShapeDtypeStruct(buf.shape, buf.dtype), in_specs=[HBM], out_specs=HBM,
        scratch_shapes=[pltpu.SemaphoreType.DMA] * 2, input_output_aliases={0: 0},
        compiler_params=pltpu.CompilerParams())(buf)


def _allgather_all(blk):
    m_per, cols = blk.shape
    whole = pl.ds(0, m_per)

    def body(x_ref, out_ref, send_sems, recv_sems, local_sem):
        x, y, c = _coords()
        me, sibling = (x, y, c), (x, y, 1 - c)
        chips = _other_chips(x, y)

        def rows_of(px, py, pc, rows):
            return out_ref.at[4 * px + 2 * py + pc, rows, :]

        def copy(k, block, to, rows=whole, from_input=False):
            return pltpu.make_async_remote_copy(
                src_ref=x_ref.at[rows, :] if from_input else rows_of(*block, rows), dst_ref=rows_of(*block, rows),
                send_sem=send_sems.at[k], recv_sem=recv_sems.at[k], device_id=to, device_id_type=MESH)

        mine = pltpu.make_async_copy(x_ref, rows_of(*me, whole), local_sem)
        mine.start()
        _start_chunks(lambda rows: copy(0, me, sibling, rows, from_input=True), m_per, VMEM_CHUNKS)
        for j, chip in enumerate(chips):
            _start_chunks(lambda rows: copy(1 + j, me, (*chip, c), rows, from_input=True), m_per, VMEM_CHUNKS)
        for j, chip in enumerate(chips):
            copy(1 + j, (*chip, c), me).wait_recv()
            _start_chunks(lambda rows: copy(4 + j, (*chip, c), sibling, rows), m_per, VMEM_CHUNKS)
        copy(0, sibling, me).wait_recv()
        for j, chip in enumerate(chips):
            copy(4 + j, (*chip, 1 - c), me).wait_recv()
        copy(0, me, sibling, from_input=True).wait_send()
        for j, chip in enumerate(chips):
            copy(1 + j, me, (*chip, c), from_input=True).wait_send()
            copy(4 + j, (*chip, c), sibling).wait_send()
        mine.wait()

    return pl.pallas_call(
        body, name="allgather_all", out_shape=jax.ShapeDtypeStruct((8, m_per, cols), blk.dtype),
        in_specs=[pl.BlockSpec(memory_space=pltpu.VMEM)], out_specs=pl.BlockSpec(memory_space=pltpu.VMEM),
        scratch_shapes=[pltpu.SemaphoreType.DMA((7,)), pltpu.SemaphoreType.DMA((7,)), pltpu.SemaphoreType.DMA],
        compiler_params=pltpu.CompilerParams(vmem_limit_bytes=VMEM_LIMIT))(blk)


def _add_kept(g, recv, tr):
    s, _, rh, cols = g.shape

    def body(c_ref, a_ref, b_ref, o_ref):
        o_ref[...] = (a_ref[...].astype(F32) + b_ref[...].astype(F32)).astype(o_ref.dtype)

    spec = pl.BlockSpec((None, tr, cols), lambda si, i, c_ref: (si, i, 0))
    return pl.pallas_call(
        body, name="add_kept", out_shape=jax.ShapeDtypeStruct((s, rh, cols), BF16),
        grid_spec=pltpu.PrefetchScalarGridSpec(
            num_scalar_prefetch=1, grid=(s, rh // tr),
            in_specs=[pl.BlockSpec((None, None, tr, cols), lambda si, i, c_ref: (si, c_ref[0], i, 0)), spec],
            out_specs=spec),
        compiler_params=_cp(("arbitrary", "arbitrary")))(lax.axis_index("c").astype(jnp.int32).reshape(1), g, recv)


def _sum_slots(p, tr, name):
    s, rows, cols = p.shape

    def body(p_ref, o_ref):
        acc = p_ref[0].astype(F32)
        for k in range(1, s):
            acc = acc + p_ref[k].astype(F32)
        o_ref[...] = acc

    return pl.pallas_call(
        body, name=name, grid=(rows // tr,), in_specs=[pl.BlockSpec((s, tr, cols), lambda i: (0, i, 0))],
        out_specs=_row_spec(tr, cols), out_shape=jax.ShapeDtypeStruct((rows, cols), F32),
        compiler_params=_cp(("arbitrary",)))(p)


def _adamw_call(w, g, m, v, name):
    rows, cols = w.shape
    tr = _tile(rows, 512) if rows % 512 == 0 else _tile(rows, 128)
    c1 = 1.0 - ADAM_B1 ** ADAM_STEP
    c2 = 1.0 - ADAM_B2 ** ADAM_STEP

    def body(w_ref, g_ref, m_ref, v_ref, d_ref, nm_ref, nv_ref):
        gv = g_ref[...]
        nm = ADAM_B1 * m_ref[...] + (1.0 - ADAM_B1) * gv
        nv = ADAM_B2 * v_ref[...] + (1.0 - ADAM_B2) * jnp.square(gv)
        d_ref[...] = -ADAM_LR * ((nm / c1) / (jnp.sqrt(nv / c2) + ADAM_EPS) + ADAM_WD * w_ref[...])
        nm_ref[...] = nm
        nv_ref[...] = nv

    spec = _row_spec(tr, cols)
    o = jax.ShapeDtypeStruct((rows, cols), F32)
    return pl.pallas_call(body, name=name, grid=(rows // tr,), in_specs=[spec] * 4, out_specs=[spec] * 3,
                          out_shape=[o, o, o], compiler_params=_cp(("arbitrary",)))(w, g, m, v)


WEIGHTS = ('norm1_g', 'w_in', 'sgu_norm_g', 'sgu_w', 'sgu_b', 's5_lambda_re', 's5_lambda_im', 's5_log_dt',
           's5_b_re', 's5_b_im', 's5_c_re', 's5_c_im', 's5_d', 's5_glu_w', 's5_glu_b', 'lru_conv_w',
           'lru_conv_b', 'lru_wa', 'lru_ba', 'lru_wx', 'lru_bx', 'lru_lambda', 'fox_fgate_b', 'mix_norm_g',
           'w_out', 'norm2_g', 'w_mlp_in', 'w_mlp_out', 'final_g')
N_W = len(WEIGHTS)


def _pack_shards(shards, dtype, names=tuple(SHARD_SHAPE), rows=LAYER_ROWS):
    parts = []
    for n in names:
        lead = shards[n].shape[:-2]
        flat = shards[n].reshape(*lead, -1).astype(dtype)
        flat = jnp.pad(flat, [(0, 0)] * len(lead) + [(0, SHARD_ROWS[n] * PACK_COLS - flat.shape[-1])])
        parts.append(flat.reshape(*lead, SHARD_ROWS[n], PACK_COLS))
    lead = parts[0].shape[:-2]
    used = sum(SHARD_ROWS[n] for n in names)
    if rows > used:
        parts.append(jnp.zeros((*lead, rows - used, PACK_COLS), dtype))
    return jnp.concatenate(parts, axis=-2)


def _unpack_shards(buf, names=tuple(SHARD_SHAPE)):
    lead = buf.shape[:-2]
    out = {}
    for n in names:
        s0, s1 = SHARD_SHAPE[n]
        rows, off = SHARD_ROWS[n], SHARD_OFF[n]
        flat = buf[..., off:off + rows, :].reshape(*lead, rows * PACK_COLS)
        out[n] = flat[..., :s0 * s1].reshape(*lead, s0, s1)
    return out


def _join_chips(g, axis):
    _, d, s0, s1 = g.shape
    if axis == 0:
        return g.transpose(1, 0, 2, 3).reshape(d, 4 * s0, s1)
    return g.transpose(1, 2, 0, 3).reshape(d, s0, 4 * s1)


def _split_chips(w, axis):
    d = w.shape[0]
    if axis == 0:
        return w.reshape(d, 4, w.shape[1] // 4, w.shape[2]).transpose(1, 0, 2, 3)
    return w.reshape(d, w.shape[1], 4, w.shape[2] // 4).transpose(2, 0, 1, 3)


def _flat_rows(shape):
    return -(-math.prod(shape) // PACK_COLS)


def _pack_flat(arrs, rows, dtype=F32):
    parts = []
    for a in arrs:
        flat = a.reshape(-1).astype(dtype)
        r = _flat_rows(a.shape)
        parts.append(jnp.pad(flat, (0, r * PACK_COLS - flat.shape[0])).reshape(r, PACK_COLS))
    used = sum(p.shape[0] for p in parts)
    parts.append(jnp.zeros((rows - used, PACK_COLS), dtype))
    return jnp.concatenate(parts, axis=0)


def _split3(s):
    hi = s.astype(BF16).astype(F32)
    mid = (s - hi).astype(BF16).astype(F32)
    return jnp.stack([hi, mid, s - hi - mid])


def _unpack_flat(buf, shapes):
    out, off = [], 0
    for s in shapes:
        r = _flat_rows(s)
        out.append(buf[off:off + r].reshape(-1)[:math.prod(s)].reshape(s))
        off += r
    return out


def _write_glue(send, glue):
    def body(g_ref, s_ref, o_ref):
        o_ref[...] = g_ref[...]

    blk = (None, GLUE_ROWS, PACK_COLS)
    return pl.pallas_call(
        body, name="write_glue", grid=(4,),
        in_specs=[pl.BlockSpec(blk, lambda s: (s, 0, 0)), pl.BlockSpec(memory_space=pl.ANY)],
        out_specs=pl.BlockSpec(blk, lambda s: (s, SMALL_OFF // GLUE_ROWS, 0)),
        out_shape=jax.ShapeDtypeStruct(send.shape, send.dtype), input_output_aliases={1: 0},
        compiler_params=_cp(("arbitrary",)))(glue, send)


GLUE_PACKED = ('w_in', 's5_glu_w', 'lru_conv_w')
REDUCE_ROWS = LAYER_ROWS // 4


def _layer_weights(gathered):
    parts = _unpack_shards(gathered[:, None], GLUE_PACKED + ('w_out',))
    joined = {n: _join_chips(g, SHARDED_AXIS[n])[0] for n, g in parts.items()}
    joined['w_in'] = jnp.pad(joined['w_in'], ((0, 0), (0, D_IN_PAD - D_IN_PROJ)))
    return joined


def _reduce_start(send):
    halves = send.reshape(4, 2, LAYER_ROWS // 2, PACK_COLS)
    return _add_kept(halves, _pair_exchange(halves), REDUCE_ROWS)


def _reduce_finish(chip_sum, received):
    return _pair_share(_sum_chips(chip_sum, received, REDUCE_ROWS)).reshape(LAYER_ROWS, PACK_COLS)


def _forward_backward(x, target, final_g, shards, rep):
    norm_p = [{n: rep[n][l] for n in ('norm1_g', 'norm2_g', 'mix_norm_g')} for l in range(DEPTH)]
    prepared, prepare_vjp = jax.vjp(_prepare, {n: rep[n] for n in PREPARED_FROM})
    gathered = _allgather_shards(shards[0])
    layers = []
    for l in range(DEPTH):
        lw = _layer_weights(gathered)
        mix_p = ({n: a[l] for n, a in prepared.items()}, lw['s5_glu_w'].astype(F32), lw['lru_conv_w'].astype(F32))
        x, res, following = _layer_fwd(x, gathered, lw['w_in'], lw['w_out'], norm_p[l], mix_p,
                                       shards[l + 1] if l + 1 < DEPTH else None)
        layers.append((res, gathered, lw))
        gathered = following
    loss_part, g, d_final = _loss_call(x, final_g[None, :], target)

    norms, d_prepared, reduced, later_send = [None] * DEPTH, [None] * DEPTH, [None] * DEPTH, None
    for l in reversed(range(DEPTH)):
        res, gathered, lw = layers[l]
        g, norms[l], (d_prepared[l], d_glu_w, d_conv_w), d_w_in, send, later_reduced = _layer_bwd(
            g, res, gathered, lw['w_in'], lw['w_out'], norm_p[l], later_send)
        if later_send is not None:
            reduced[l + 1] = later_reduced
        mine = {'w_in': d_w_in[:, :D_IN_PROJ], 's5_glu_w': d_glu_w, 'lru_conv_w': d_conv_w}
        glue = _pack_shards({n: _split_chips(a[None], SHARDED_AXIS[n]) for n, a in mine.items()}, BF16,
                            names=GLUE_PACKED, rows=GLUE_ROWS)[:, 0]
        later_send = _write_glue(send, glue)
    chip_sum = _reduce_start(later_send)
    reduced[0] = _reduce_finish(chip_sum, _chip_exchange(chip_sum))
    stack = lambda per_layer: {n: jnp.stack([per_layer[l][n] for l in range(DEPTH)]) for n in per_layer[0]}
    (d_rep,) = prepare_vjp(stack(d_prepared))
    return loss_part, g, d_final, dict(d_rep, **stack(norms)), _unpack_shards(jnp.stack(reduced))


def _step(*args):
    x, target = args[0], args[1 + N_W]
    w = dict(zip(WEIGHTS, args[1:1 + N_W]))
    m = dict(zip(WEIGHTS, args[2 + N_W:2 + 2 * N_W]))
    v = dict(zip(WEIGHTS, args[2 + 2 * N_W:2 + 3 * N_W]))
    small = [n for n in WEIGHTS if n not in SHARD_SHAPE]

    shards = _pack_shards({n: w[n] for n in SHARD_SHAPE}, BF16)
    loss_part, dx, d_final, dw, g_shard = _forward_backward(
        x[0], target[0], w['final_g'], shards, {n: w[n] for n in small})

    small_g = [d_final.reshape(-1) if n == 'final_g' else dw[n] for n in small]
    small_rows = -(-(sum(_flat_rows(w[n].shape) for n in small) + 1) // 128) * 128
    mine = _pack_flat(small_g + [_split3(loss_part[0, 0])], small_rows, BF16)
    small_sum = _sum_slots(_allgather_all(mine), 128, "sum_devices")
    *g_small, loss = _unpack_flat(small_sum, [w[n].shape for n in small] + [(3,)])
    loss = jnp.sum(loss)

    grads, delta, new_m, new_v = {}, {}, {}, {}
    for n in SHARD_SHAPE:
        shp = w[n].shape
        v2 = lambda a: a.reshape(-1, shp[-1])
        res = _adamw_call(v2(w[n]), v2(g_shard[n]), v2(m[n]), v2(v[n]), "adamw_" + n)
        grads[n] = g_shard[n]
        delta[n], new_m[n], new_v[n] = (r.reshape(shp) for r in res)
    pk = lambda d: _pack_flat([d[n] for n in small], small_rows)
    res = _adamw_call(pk(w), _pack_flat(g_small, small_rows), pk(m), pk(v), "adamw_small")
    shapes = [w[n].shape for n in small]
    for n, g, d_, m_, v_ in zip(small, g_small, *(_unpack_flat(r, shapes) for r in res)):
        grads[n], delta[n], new_m[n], new_v[n] = g, d_, m_, v_

    return (loss, dx[None], *[grads[n] for n in WEIGHTS], *[delta[n] for n in WEIGHTS],
            *[new_m[n] for n in WEIGHTS], *[new_v[n] for n in WEIGHTS])


def kernel(x, norm1_g, w_in, sgu_norm_g, sgu_w, sgu_b, s5_lambda_re, s5_lambda_im, s5_log_dt, s5_b_re, s5_b_im, s5_c_re, s5_c_im, s5_d, s5_glu_w, s5_glu_b, lru_conv_w, lru_conv_b, lru_wa, lru_ba, lru_wx, lru_bx, lru_lambda, fox_fgate_b, mix_norm_g, w_out, norm2_g, w_mlp_in, w_mlp_out, final_g, loss_target, m_norm1_g, m_w_in, m_sgu_norm_g, m_sgu_w, m_sgu_b, m_s5_lambda_re, m_s5_lambda_im, m_s5_log_dt, m_s5_b_re, m_s5_b_im, m_s5_c_re, m_s5_c_im, m_s5_d, m_s5_glu_w, m_s5_glu_b, m_lru_conv_w, m_lru_conv_b, m_lru_wa, m_lru_ba, m_lru_wx, m_lru_bx, m_lru_lambda, m_fox_fgate_b, m_mix_norm_g, m_w_out, m_norm2_g, m_w_mlp_in, m_w_mlp_out, m_final_g, v_norm1_g, v_w_in, v_sgu_norm_g, v_sgu_w, v_sgu_b, v_s5_lambda_re, v_s5_lambda_im, v_s5_log_dt, v_s5_b_re, v_s5_b_im, v_s5_c_re, v_s5_c_im, v_s5_d, v_s5_glu_w, v_s5_glu_b, v_lru_conv_w, v_lru_conv_b, v_lru_wa, v_lru_ba, v_lru_wx, v_lru_bx, v_lru_lambda, v_fox_fgate_b, v_mix_norm_g, v_w_out, v_norm2_g, v_w_mlp_in, v_w_mlp_out, v_final_g):
    return _step(x, norm1_g, w_in, sgu_norm_g, sgu_w, sgu_b, s5_lambda_re, s5_lambda_im, s5_log_dt, s5_b_re, s5_b_im, s5_c_re, s5_c_im, s5_d, s5_glu_w, s5_glu_b, lru_conv_w, lru_conv_b, lru_wa, lru_ba, lru_wx, lru_bx, lru_lambda, fox_fgate_b, mix_norm_g, w_out, norm2_g, w_mlp_in, w_mlp_out, final_g, loss_target, m_norm1_g, m_w_in, m_sgu_norm_g, m_sgu_w, m_sgu_b, m_s5_lambda_re, m_s5_lambda_im, m_s5_log_dt, m_s5_b_re, m_s5_b_im, m_s5_c_re, m_s5_c_im, m_s5_d, m_s5_glu_w, m_s5_glu_b, m_lru_conv_w, m_lru_conv_b, m_lru_wa, m_lru_ba, m_lru_wx, m_lru_bx, m_lru_lambda, m_fox_fgate_b, m_mix_norm_g, m_w_out, m_norm2_g, m_w_mlp_in, m_w_mlp_out, m_final_g, v_norm1_g, v_w_in, v_sgu_norm_g, v_sgu_w, v_sgu_b, v_s5_lambda_re, v_s5_lambda_im, v_s5_log_dt, v_s5_b_re, v_s5_b_im, v_s5_c_re, v_s5_c_im, v_s5_d, v_s5_glu_w, v_s5_glu_b, v_lru_conv_w, v_lru_conv_b, v_lru_wa, v_lru_ba, v_lru_wx, v_lru_bx, v_lru_lambda, v_fox_fgate_b, v_mix_norm_g, v_w_out, v_norm2_g, v_w_mlp_in, v_w_mlp_out, v_final_g)
```

```python
import functools
import math

import jax
import jax.numpy as jnp
from jax import lax
from jax.experimental import pallas as pl
from jax.experimental.pallas import tpu as pltpu

F32 = jnp.float32
BF16 = jnp.bfloat16

DEPTH = 4
D_MODEL = 1024
MIXER_WIDTH = 256
SGU_CHUNK = 128
N_HEADS = 4
HEAD_DIM = 64
S5_GROUPS = 16
S5_GROUP = 16
S5_STATE = 64
LRU_C = 8.0
RMS_EPS = 1e-6
D_IN_PROJ = 8 * MIXER_WIDTH + N_HEADS
D_IN_PAD = 8 * MIXER_WIDTH + 128
ADAM_LR, ADAM_B1, ADAM_B2, ADAM_EPS, ADAM_WD, ADAM_STEP = 0.001, 0.9, 0.999, 1e-08, 0.01, 10

V7X_VMEM_BYTES = 64 * 1024 * 1024
VMEM_LIMIT = V7X_VMEM_BYTES - 8 * 1024 * 1024
NEG = -1e30
MESH = pl.DeviceIdType.MESH


def _cp(sem=None, **kw):
    return pltpu.CompilerParams(dimension_semantics=sem, vmem_limit_bytes=VMEM_LIMIT, **kw)


def _full_spec(a):
    nd = a.ndim
    return pl.BlockSpec(a.shape, lambda *_: (0,) * nd)


def _tile(n, pref=512):
    return pref if n % pref == 0 else n


def _dot(a, b, ca, cb):
    return lax.dot_general(a.astype(BF16), b.astype(BF16), (((ca,), (cb,)), ((), ())),
                           preferred_element_type=F32)


def _matmul(name, grid, a, a_spec, b, b_spec, dims, outs, *, extras=(), epilogue=None, into=None, summed=0,
            acc_shape=None, rider=None):
    nk = grid[2]
    n_ex, n_out = len(extras), len(outs)
    tm_tn = acc_shape or tuple(d for d in outs[0][1].block_shape if d is not None)[-2:]
    n_in = 2 + n_ex + (into is not None)

    def body(*refs):
        a_ref, b_ref = refs[0], refs[1]
        ex_refs = refs[2:2 + n_ex]
        o_refs = refs[n_in + (rider is not None):n_in + (rider is not None) + n_out]
        if rider is not None:
            start, finish_rider = rider[3](refs[n_in], refs[n_in + 1 + n_out], refs[-2], refs[-1])
            step = [pl.program_id(d) for d in range(3)]
            pl.when((step[0] == 0) & (step[1] == 0) & (step[2] == 0))(start)
        if summed:
            @pl.when((pl.program_id(0) == 0) & (pl.program_id(1) == 0) & (pl.program_id(2) == 0))
            def _():
                for o_ref in o_refs[n_out - summed:]:
                    o_ref[...] = jnp.zeros_like(o_ref)

        def finish(val):
            res = epilogue(val, *[e[...] for e in ex_refs]) if epilogue else (val,)
            for idx, (o_ref, r) in enumerate(zip(o_refs, res)):
                if idx >= n_out - summed:
                    o_ref[...] += r
                else:
                    o_ref[...] = r.astype(o_ref.dtype)

        if nk == 1:
            finish(_dot(a_ref[...], b_ref[...], *dims))
        else:
            acc = refs[n_in + (rider is not None) + n_out + (rider is not None)]
            kk = pl.program_id(2)

            @pl.when(kk == 0)
            def _():
                acc[...] = jnp.zeros_like(acc)

            acc[...] += _dot(a_ref[...], b_ref[...], *dims)

            @pl.when(kk == nk - 1)
            def _():
                finish(acc[...])

        if rider is not None:
            pl.when((step[0] == grid[0] - 1) & (step[1] == grid[1] - 1) & (step[2] == grid[2] - 1))(finish_rider)

    ins = [a, b] + [e[0] for e in extras]
    specs = [a_spec, b_spec] + [e[1] for e in extras]
    aliases = {}
    if into is not None:
        aliases = {len(ins): 0}
        ins.append(into)
        specs.append(pl.BlockSpec(memory_space=pl.ANY))
    out_specs, out_shape = [o[1] for o in outs], [o[0] for o in outs]
    scratch = [pltpu.VMEM(tm_tn, F32)] if nk > 1 else []
    if rider is not None:
        if rider[2]:
            aliases[len(ins)] = n_out
        ins.append(rider[0])
        specs.append(HBM)
        out_specs.append(HBM)
        out_shape.append(rider[1])
        scratch += [pltpu.SemaphoreType.DMA] * 2
        name += "_rider"
    res = pl.pallas_call(
        body, name=name, grid=grid, in_specs=specs, out_specs=out_specs, out_shape=out_shape,
        scratch_shapes=scratch, input_output_aliases=aliases,
        compiler_params=_cp(("arbitrary", "arbitrary", "arbitrary")))(*ins)
    return res[0] if len(res) == 1 else res


@jax.custom_vjp
def _bdot(a, b):
    return _dot(a, b, 1, 0)


def _bdot_fwd(a, b):
    return _dot(a, b, 1, 0), (a, b)


def _bdot_bwd(r, g):
    a, b = r
    return _dot(g, b, 1, 1), _dot(a, g, 0, 0)


_bdot.defvjp(_bdot_fwd, _bdot_bwd)


def _row_spec(tr, w):
    return pl.BlockSpec((tr, w), lambda i: (i, 0))


def _rowwise(fn, rows, pars, outs, *, name, tr, dtype=F32):
    t = rows[0].shape[0]
    n_in = len(rows) + len(pars)

    def body(*refs):
        res = fn(*[r[...] for r in refs[:n_in]])
        for o_ref, v in zip(refs[n_in:], res):
            o_ref[...] = v.astype(o_ref.dtype)

    return pl.pallas_call(
        body, name=name, grid=(t // tr,),
        in_specs=[_row_spec(tr, r.shape[1]) for r in rows] + [_full_spec(p) for p in pars],
        out_specs=[_row_spec(tr, w) for w in outs],
        out_shape=[jax.ShapeDtypeStruct((t, w), dtype) for w in outs],
        compiler_params=_cp(("arbitrary",)))(*rows, *pars)


def _rowwise_vjp(fn, rows, pars, cots, *, name, tr, add=None):
    t = rows[0].shape[0]
    nr, npar = len(rows), len(pars)
    cots = list(cots) + ([add] if add is not None else [])
    nc = len(cots)

    def body(*refs):
        vals = [r[...] for r in refs[:nr + npar]]
        cts = [c[...] for c in refs[nr + npar:nr + npar + nc]]
        douts = refs[nr + npar + nc:]
        extra = cts.pop() if add is not None else None
        _, vjp = jax.vjp(fn, *vals)
        grads = list(vjp(tuple(cts)))
        if extra is not None:
            grads[0] = grads[0] + extra
        for kk in range(nr):
            douts[kk][...] = grads[kk]

        @pl.when(pl.program_id(0) == 0)
        def _():
            for kk in range(npar):
                douts[nr + kk][...] = jnp.zeros_like(douts[nr + kk])

        for kk in range(npar):
            douts[nr + kk][...] += grads[nr + kk]

    return pl.pallas_call(
        body, name=name, grid=(t // tr,),
        in_specs=[_row_spec(tr, r.shape[1]) for r in rows] + [_full_spec(p) for p in pars]
        + [_row_spec(tr, c.shape[1]) for c in cots],
        out_specs=[_row_spec(tr, r.shape[1]) for r in rows] + [_full_spec(p) for p in pars],
        out_shape=[jax.ShapeDtypeStruct(r.shape, F32) for r in rows]
        + [jax.ShapeDtypeStruct(p.shape, F32) for p in pars],
        compiler_params=_cp(("arbitrary",)))(*rows, *pars, *cots)


def _make_rw(fn, name, tr, nr, outs):
    @jax.custom_vjp
    def f(*args):
        return tuple(_rowwise(fn, args[:nr], args[nr:], outs, name=name + "_f", tr=tr))

    def fwd(*args):
        return f(*args), args

    def bwd(args, cts):
        return tuple(_rowwise_vjp(fn, args[:nr], args[nr:], list(cts), name=name + "_b", tr=tr))

    f.defvjp(fwd, bwd)
    return f


def _rms(x, g):
    return x * lax.rsqrt(jnp.mean(jnp.square(x), axis=-1, keepdims=True) + RMS_EPS) * g


def _f_rms(x, g):
    return (_rms(x, g),)


def _f_sgu(au, av, ng, w0, w1, w2, w3, bfull):
    u = jax.nn.gelu(au)
    v = _rms(jax.nn.gelu(av), ng)
    tri = lax.broadcasted_iota(jnp.int32, (SGU_CHUNK, SGU_CHUNK), 0) >= lax.broadcasted_iota(
        jnp.int32, (SGU_CHUNK, SGU_CHUNK), 1)
    head = lax.broadcasted_iota(jnp.int32, v.shape, 1) // HEAD_DIM
    mixed = bfull
    for h, w in enumerate((w0, w1, w2, w3)):
        mixed = mixed + _bdot(jnp.where(tri, w, 0.0), jnp.where(head == h, v, 0.0))
    return (u * mixed,)


def _f_s5disc(lam_re, lam_im, log_dt, b_re, b_im):
    dt = jnp.exp(log_dt)
    mag = jnp.exp(lam_re * dt)
    abar_re = mag * jnp.cos(lam_im * dt)
    abar_im = mag * jnp.sin(lam_im * dt)
    denom = jnp.square(lam_re) + jnp.square(lam_im)
    num_re = abar_re - 1.0
    num_im = abar_im
    fac_re = (num_re * lam_re + num_im * lam_im) / denom
    fac_im = (num_im * lam_re - num_re * lam_im) / denom
    return abar_re, abar_im, fac_re * b_re - fac_im * b_im, fac_re * b_im + fac_im * b_re


def _f_s5post(s_re, s_im, u, c_re, c_im, d, gw, gb):
    y = _bdot(s_re, c_re) - _bdot(s_im, c_im) + d * u
    y = jax.nn.gelu(y)
    return (y * jax.nn.sigmoid(_bdot(y, gw) + gb),)


def _f_lrupre(xc, wa, ba, wx, bx, lam):
    r = jax.nn.sigmoid(_bdot(xc, wa) + ba)
    i = jax.nn.sigmoid(_bdot(xc, wx) + bx)
    log_a = -LRU_C * r * jax.nn.softplus(-lam)
    a = jnp.exp(log_a)
    one_minus_a2 = -jnp.tanh(log_a) * (jnp.exp(2.0 * log_a) + 1.0)
    return a, jnp.sqrt(one_minus_a2) * (i * xc)


def _f_lrupost(h, gate):
    return (h * jax.nn.gelu(gate),)


def _f_logsig(zf, bf):
    return (jax.nn.log_sigmoid(zf + bf),)


def _f_gnorm(ya, yb, yc, yd, g):
    def n(y):
        return y * lax.rsqrt(jnp.mean(jnp.square(y), axis=-1, keepdims=True) + RMS_EPS)
    return (jnp.concatenate([n(ya), n(yb), n(yc), n(yd)], axis=1) * g,)


sgu_mix = _make_rw(_f_sgu, "sgu", SGU_CHUNK, 2, [MIXER_WIDTH])
s5_disc = _make_rw(_f_s5disc, "s5disc", S5_GROUPS * S5_GROUP, 5, [S5_STATE] * 4)
s5_post = _make_rw(_f_s5post, "s5post", 256, 3, [MIXER_WIDTH])
lru_pre = _make_rw(_f_lrupre, "lrupre", 512, 1, [MIXER_WIDTH, MIXER_WIDTH])
lru_post = _make_rw(_f_lrupost, "lrupost", 512, 2, [MIXER_WIDTH])
log_sig = _make_rw(_f_logsig, "logsig", 512, 1, [128])


SCAN_TILE = 512


def _prev_spec(c, nt, rev):
    per = SCAN_TILE // 8
    if rev:
        return pl.BlockSpec((8, c), lambda i: (jnp.maximum((nt - 1 - i) * per - 1, 0), 0))
    return pl.BlockSpec((8, c), lambda i: (jnp.maximum(i * per - 1, 0), 0))


SCAN_STEPS = (1, 2, 4)


def _cmul(ar, ai, br, bi):
    return ar * br - ai * bi, ar * bi + ai * br


def _rows_down(x, k, fill, rowid):
    return jnp.where(rowid >= k, pltpu.roll(x, k, 0), fill)


def _rows_up(x, k, fill, rowid):
    return jnp.where(rowid < 8 - k, pltpu.roll(x, 8 - k, 0), fill)


def _powers(ar, ai):
    pw = [(ar, ai)]
    for _ in range(7):
        pw.append(_cmul(*pw[-1], ar, ai))
    return pw


def _block(i):
    return pl.ds(pl.multiple_of(i * 8, 8), 8)


def _row_before(ref, i, edge):
    return jnp.where(i == 0, edge, ref[pl.ds(jnp.maximum(i * 8 - 1, 0), 1), :])


def _lti_fwd_call(u, w_re, w_im, a_re, a_im):
    t, kdim = u.shape
    c = w_re.shape[1]
    tt = SCAN_TILE

    def body(u_ref, wr_ref, wi_ref, ar_ref, ai_ref, sr_ref, si_ref, br_ref, bi_ref, cr, ci):
        @pl.when(pl.program_id(0) == 0)
        def _():
            cr[...] = jnp.zeros_like(cr)
            ci[...] = jnp.zeros_like(ci)

        br_ref[...] = _dot(u_ref[...], wr_ref[...], 1, 0)
        bi_ref[...] = _dot(u_ref[...], wi_ref[...], 1, 0)
        pw = _powers(ar_ref[...], ai_ref[...])
        apr = jnp.concatenate([p[0] for p in pw], axis=0)
        api = jnp.concatenate([p[1] for p in pw], axis=0)
        rowid = lax.broadcasted_iota(jnp.int32, (8, c), 0)

        def block(i, carry):
            xr, xi = br_ref[_block(i), :], bi_ref[_block(i), :]
            for k in SCAN_STEPS:
                dr, di = _cmul(*pw[k - 1], _rows_down(xr, k, 0.0, rowid), _rows_down(xi, k, 0.0, rowid))
                xr, xi = xr + dr, xi + di
            dr, di = _cmul(apr, api, *carry)
            xr, xi = xr + dr, xi + di
            sr_ref[_block(i), :] = xr
            si_ref[_block(i), :] = xi
            return xr[7:8, :], xi[7:8, :]

        hr, hi = lax.fori_loop(0, tt // 8, block, (cr[...], ci[...]), unroll=2)
        cr[...] = hr
        ci[...] = hi

    row = pl.BlockSpec((tt, c), lambda i: (i, 0))
    par = pl.BlockSpec((1, c), lambda i: (0, 0))
    return pl.pallas_call(
        body, name="lti_scan_f", grid=(t // tt,),
        in_specs=[pl.BlockSpec((tt, kdim), lambda i: (i, 0)), _full_spec(w_re), _full_spec(w_im), par, par],
        out_specs=[row, row], out_shape=[jax.ShapeDtypeStruct((t, c), F32)] * 2,
        scratch_shapes=[pltpu.VMEM((tt, c), F32)] * 2 + [pltpu.VMEM((1, c), F32)] * 2,
        compiler_params=_cp(("arbitrary",)))(u, w_re, w_im, a_re, a_im)


def _lti_bwd_call(u, w_re, w_im, a_re, a_im, s_re, s_im, g_re, g_im):
    t, c = g_re.shape
    kdim = u.shape[1]
    tt = SCAN_TILE
    nt = t // tt
    nb = tt // 8

    def body(u_ref, wr_ref, wi_ref, ar_ref, ai_ref, sr_ref, si_ref, pr_ref, pi_ref, gr_ref, gi_ref,
             du_ref, dwr_ref, dwi_ref, dar_ref, dai_ref, or_ref, oi_ref, cr, ci):
        ti = pl.program_id(0)

        @pl.when(ti == 0)
        def _():
            cr[...] = jnp.zeros_like(cr)
            ci[...] = jnp.zeros_like(ci)
            dar_ref[...] = jnp.zeros_like(dar_ref)
            dai_ref[...] = jnp.zeros_like(dai_ref)
            dwr_ref[...] = jnp.zeros_like(dwr_ref)
            dwi_ref[...] = jnp.zeros_like(dwi_ref)

        pw = _powers(ar_ref[...], -ai_ref[...])
        tpr = jnp.concatenate([p[0] for p in reversed(pw)], axis=0)
        tpi = jnp.concatenate([p[1] for p in reversed(pw)], axis=0)
        rowid = lax.broadcasted_iota(jnp.int32, (8, c), 0)
        first = ti == nt - 1
        edge_r = jnp.where(first, 0.0, pr_ref[7:8, :])
        edge_i = jnp.where(first, 0.0, pi_ref[7:8, :])

        def block(kk, carry):
            i = nb - 1 - kk
            gr_c, gi_c, acc_r, acc_i = carry
            xr, xi = gr_ref[_block(i), :], gi_ref[_block(i), :]
            for k in SCAN_STEPS:
                dr, di = _cmul(*pw[k - 1], _rows_up(xr, k, 0.0, rowid), _rows_up(xi, k, 0.0, rowid))
                xr, xi = xr + dr, xi + di
            dr, di = _cmul(tpr, tpi, gr_c, gi_c)
            xr, xi = xr + dr, xi + di
            or_ref[_block(i), :] = xr
            oi_ref[_block(i), :] = xi
            spr = _rows_down(sr_ref[_block(i), :], 1, _row_before(sr_ref, i, edge_r), rowid)
            spi = _rows_down(si_ref[_block(i), :], 1, _row_before(si_ref, i, edge_i), rowid)
            return xr[0:1, :], xi[0:1, :], acc_r + spr * xr + spi * xi, acc_i + spr * xi - spi * xr

        zero = jnp.zeros((8, c), F32)
        gr_c, gi_c, acc_r, acc_i = lax.fori_loop(0, nb, block, (cr[...], ci[...], zero, zero), unroll=2)
        cr[...] = gr_c
        ci[...] = gi_c
        dar_ref[...] += jnp.sum(acc_r, axis=0, keepdims=True)
        dai_ref[...] += jnp.sum(acc_i, axis=0, keepdims=True)
        du_ref[...] = _dot(or_ref[...], wr_ref[...], 1, 1) + _dot(oi_ref[...], wi_ref[...], 1, 1)
        dwr_ref[...] += _dot(u_ref[...], or_ref[...], 0, 0)
        dwi_ref[...] += _dot(u_ref[...], oi_ref[...], 0, 0)

    row = pl.BlockSpec((tt, c), lambda i: (nt - 1 - i, 0))
    row_u = pl.BlockSpec((tt, kdim), lambda i: (nt - 1 - i, 0))
    par = pl.BlockSpec((1, c), lambda i: (0, 0))
    prev = _prev_spec(c, nt, True)
    return pl.pallas_call(
        body, name="lti_scan_b", grid=(nt,),
        in_specs=[row_u, _full_spec(w_re), _full_spec(w_im), par, par, row, row, prev, prev, row, row],
        out_specs=[row_u, _full_spec(w_re), _full_spec(w_im), par, par],
        out_shape=[jax.ShapeDtypeStruct((t, kdim), F32), jax.ShapeDtypeStruct(w_re.shape, F32),
                   jax.ShapeDtypeStruct(w_im.shape, F32)] + [jax.ShapeDtypeStruct((1, c), F32)] * 2,
        scratch_shapes=[pltpu.VMEM((tt, c), F32)] * 2 + [pltpu.VMEM((1, c), F32)] * 2,
        compiler_params=_cp(("arbitrary",)))(u, w_re, w_im, a_re, a_im, s_re, s_im, s_re, s_im, g_re, g_im)


@jax.custom_vjp
def lti_scan(u, w_re, w_im, a_re, a_im):
    return tuple(_lti_fwd_call(u, w_re, w_im, a_re, a_im))


def _lti_scan_fwd(u, w_re, w_im, a_re, a_im):
    s_re, s_im = _lti_fwd_call(u, w_re, w_im, a_re, a_im)
    return (s_re, s_im), (u, w_re, w_im, a_re, a_im, s_re, s_im)


def _lti_scan_bwd(r, g):
    return tuple(_lti_bwd_call(*r, g[0], g[1]))


lti_scan.defvjp(_lti_scan_fwd, _lti_scan_bwd)


def _tv_fwd_call(a, b):
    t, c = b.shape
    tt = SCAN_TILE

    def body(a_ref, b_ref, h_ref, ch):
        @pl.when(pl.program_id(0) == 0)
        def _():
            ch[...] = jnp.zeros_like(ch)

        rowid = lax.broadcasted_iota(jnp.int32, (8, c), 0)

        def block(i, h):
            ab, x = a_ref[_block(i), :], b_ref[_block(i), :]
            for k in SCAN_STEPS:
                x = x + ab * _rows_down(x, k, 0.0, rowid)
                ab = ab * _rows_down(ab, k, 1.0, rowid)
            x = x + ab * h
            h_ref[_block(i), :] = x
            return x[7:8, :]

        ch[...] = lax.fori_loop(0, tt // 8, block, ch[...], unroll=2)

    row = pl.BlockSpec((tt, c), lambda i: (i, 0))
    return pl.pallas_call(
        body, name="tv_scan_f", grid=(t // tt,), in_specs=[row, row], out_specs=row,
        out_shape=jax.ShapeDtypeStruct((t, c), F32), scratch_shapes=[pltpu.VMEM((1, c), F32)],
        compiler_params=_cp(("arbitrary",)))(a, b)


def _tv_bwd_call(a, h, g):
    t, c = g.shape
    tt = SCAN_TILE
    nt = t // tt
    nb = tt // 8

    def body(a_ref, h_ref, p_ref, g_ref, da_ref, db_ref, cg, ca):
        ti = pl.program_id(0)

        @pl.when(ti == 0)
        def _():
            cg[...] = jnp.zeros_like(cg)
            ca[...] = jnp.zeros_like(ca)

        rowid = lax.broadcasted_iota(jnp.int32, (8, c), 0)
        edge = jnp.where(ti == nt - 1, 0.0, p_ref[7:8, :])

        def block(kk, carry):
            i = nb - 1 - kk
            gc, a_next = carry
            ab, x = a_ref[_block(i), :], g_ref[_block(i), :]
            cb = _rows_up(ab, 1, a_next, rowid)
            for k in SCAN_STEPS:
                x = x + cb * _rows_up(x, k, 0.0, rowid)
                cb = cb * _rows_up(cb, k, 1.0, rowid)
            x = x + cb * gc
            db_ref[_block(i), :] = x
            da_ref[_block(i), :] = x * _rows_down(h_ref[_block(i), :], 1, _row_before(h_ref, i, edge), rowid)
            return x[0:1, :], ab[0:1, :]

        gc, a_next = lax.fori_loop(0, nb, block, (cg[...], ca[...]), unroll=2)
        cg[...] = gc
        ca[...] = a_next

    row = pl.BlockSpec((tt, c), lambda i: (nt - 1 - i, 0))
    return pl.pallas_call(
        body, name="tv_scan_b", grid=(nt,), in_specs=[row, row, _prev_spec(c, nt, True), row],
        out_specs=[row, row], out_shape=[jax.ShapeDtypeStruct((t, c), F32)] * 2,
        scratch_shapes=[pltpu.VMEM((1, c), F32)] * 2, compiler_params=_cp(("arbitrary",)))(a, h, h, g)


@jax.custom_vjp
def tv_scan(a, b):
    return _tv_fwd_call(a, b)


def _tv_scan_fwd(a, b):
    h = _tv_fwd_call(a, b)
    return h, (a, h)


def _tv_scan_bwd(r, g):
    a, h = r
    return tuple(_tv_bwd_call(a, h, g))


tv_scan.defvjp(_tv_scan_fwd, _tv_scan_bwd)


CONV_K = 4
CONV_ROWS = 512


def _conv_fwd_call(x, w, b):
    t, c = x.shape

    def body(x_ref, w_ref, b_ref, o_ref, xp):
        xp[0:8, :] = jnp.zeros((8, c), F32)
        xp[8:, :] = x_ref[...]
        for blk in range(t // CONV_ROWS):
            base = blk * CONV_ROWS
            acc = jnp.broadcast_to(b_ref[...], (CONV_ROWS, c))
            for kk in range(CONV_K):
                acc = acc + w_ref[kk:kk + 1, :] * xp[base + 5 + kk:base + 5 + kk + CONV_ROWS, :]
            o_ref[base:base + CONV_ROWS, :] = acc

    return pl.pallas_call(
        body, name="conv_f", out_shape=jax.ShapeDtypeStruct((t, c), F32),
        scratch_shapes=[pltpu.VMEM((t + 8, c), F32)], compiler_params=_cp())(x, w, b)


def _conv_bwd_call(x, w, g):
    t, c = x.shape

    def body(x_ref, w_ref, g_ref, dx_ref, dw_ref, db_ref, xp, gp):
        xp[0:8, :] = jnp.zeros((8, c), F32)
        xp[8:, :] = x_ref[...]
        gp[0:t, :] = g_ref[...]
        gp[t:, :] = jnp.zeros((8, c), F32)
        dw = [jnp.zeros((1, c), F32) for _ in range(CONV_K)]
        db = jnp.zeros((1, c), F32)
        for blk in range(t // CONV_ROWS):
            base = blk * CONV_ROWS
            gb = g_ref[base:base + CONV_ROWS, :]
            acc = jnp.zeros((CONV_ROWS, c), F32)
            for kk in range(CONV_K):
                acc = acc + w_ref[kk:kk + 1, :] * gp[base + 3 - kk:base + 3 - kk + CONV_ROWS, :]
                dw[kk] = dw[kk] + jnp.sum(gb * xp[base + 5 + kk:base + 5 + kk + CONV_ROWS, :], axis=0, keepdims=True)
            db = db + jnp.sum(gb, axis=0, keepdims=True)
            dx_ref[base:base + CONV_ROWS, :] = acc
        for kk in range(CONV_K):
            dw_ref[kk:kk + 1, :] = dw[kk]
        db_ref[...] = db

    return pl.pallas_call(
        body, name="conv_b",
        out_shape=[jax.ShapeDtypeStruct((t, c), F32), jax.ShapeDtypeStruct((CONV_K, c), F32),
                   jax.ShapeDtypeStruct((1, c), F32)],
        scratch_shapes=[pltpu.VMEM((t + 8, c), F32)] * 2, compiler_params=_cp())(x, w, g)


@jax.custom_vjp
def causal_conv(x, w, b):
    return _conv_fwd_call(x, w, b)


def _causal_conv_fwd(x, w, b):
    return _conv_fwd_call(x, w, b), (x, w)


def _causal_conv_bwd(r, g):
    return tuple(_conv_bwd_call(r[0], r[1], g))


causal_conv.defvjp(_causal_conv_fwd, _causal_conv_bwd)


ATT_TILE = 512
ATT_SCALE = HEAD_DIM ** -0.5


def _head_lane(val, lane, h):
    return jnp.sum(jnp.where(lane == h, val, 0.0), axis=1, keepdims=True)


def _attn_fwd_call(q, k, v, c128, cr, next_shard=None, late=None):
    t, w = q.shape
    tq = ATT_TILE
    nq = t // tq
    k3, v3, cr4 = k.reshape(nq, tq, w), v.reshape(nq, tq, w), cr.reshape(N_HEADS, nq, 1, tq)
    fused = next_shard is not None
    both = late is not None

    def body(*refs):
        q_ref, k_ref, v_ref, c_ref, cr_ref = refs[:5]
        i, h = pl.program_id(0), pl.program_id(1)
        gathers = []
        if both:
            shard_ref, own_ref, _, o_ref, lse_ref, gathered_ref, own_out_ref, *sems = refs[5:]
            gathers.append(_allgather_copies(own_ref, own_out_ref, sems[2], sems[3], LATE_ROWS))
        elif fused:
            shard_ref, o_ref, lse_ref, gathered_ref, *sems = refs[5:]
        else:
            o_ref, lse_ref = refs[5:]
        if fused:
            gathers.append(_allgather_copies(shard_ref, gathered_ref, sems[0], sems[1]))
        for start, _ in gathers:
            pl.when((i == 0) & (h == 0))(start)
        hm = lax.broadcasted_iota(jnp.int32, (tq, w), 1) // HEAD_DIM == h
        lane = lax.broadcasted_iota(jnp.int32, (tq, 128), 1)
        qs = jnp.where(hm, q_ref[...] * ATT_SCALE, 0.0)
        cq = _head_lane(c_ref[...], lane, h)
        causal = lax.broadcasted_iota(jnp.int32, (tq, tq), 0) >= lax.broadcasted_iota(jnp.int32, (tq, tq), 1)

        def update(j, carry, diagonal):
            m, l, acc = carry
            s = _dot(qs, k_ref[j], 1, 1) - cr_ref[0, j]
            if diagonal:
                s = jnp.where(causal, s, NEG)
            m_new = jnp.maximum(m, cq + jnp.max(s, axis=1, keepdims=True))
            p = jnp.exp(s + (cq - m_new))
            alpha = jnp.exp(m - m_new)
            return m_new, alpha * l + jnp.sum(p, axis=1, keepdims=True), alpha * acc + _dot(p, v_ref[j], 1, 0)

        init = (jnp.full((tq, 1), NEG, F32), jnp.zeros((tq, 1), F32), jnp.zeros((tq, w), F32))
        carry = lax.fori_loop(0, i, lambda j, c: update(j, c, False), init)
        m, l, acc = update(i, carry, True)
        out = jnp.where(hm, acc / l, 0.0)
        lse = jnp.where(lane == h, m + jnp.log(l), 0.0)

        @pl.when(h == 0)
        def _():
            o_ref[...] = out
            lse_ref[...] = lse

        @pl.when(h > 0)
        def _():
            o_ref[...] += out
            lse_ref[...] += lse

        for _, finish in gathers:
            pl.when((i == nq - 1) & (h == N_HEADS - 1))(finish)

    tile = pl.BlockSpec((tq, w), lambda i, h: (i, 0))
    tile_c = pl.BlockSpec((tq, 128), lambda i, h: (i, 0))
    whole = pl.BlockSpec((nq, tq, w), lambda i, h: (0, 0, 0))
    rows = pl.BlockSpec((1, nq, 1, tq), lambda i, h: (h, 0, 0, 0))
    ins = [q.astype(BF16), k3.astype(BF16), v3.astype(BF16), c128, cr4]
    in_specs, out_specs = [tile, whole, whole, tile_c, rows], [tile, tile_c]
    out_shape = [jax.ShapeDtypeStruct((t, w), F32), jax.ShapeDtypeStruct((t, 128), F32)]
    scratch, aliases, name = [], {}, "attn_f"
    sem_pair = [pltpu.SemaphoreType.DMA((AG_SEMS,)), pltpu.SemaphoreType.DMA((AG_SEMS,))]
    if fused:
        ins.append(next_shard)
        in_specs.append(HBM)
        out_specs.append(HBM)
        out_shape.append(jax.ShapeDtypeStruct((4,) + next_shard.shape, next_shard.dtype))
        scratch, name = scratch + sem_pair, "attn_f_allgather"
    if both:
        own_shard, own_gathered = late
        aliases = {len(ins) + 1: len(out_shape)}
        ins += [own_shard, own_gathered]
        in_specs += [HBM, HBM]
        out_specs.append(HBM)
        out_shape.append(jax.ShapeDtypeStruct(own_gathered.shape, own_gathered.dtype))
        scratch, name = scratch + sem_pair, "attn_f_allgather2"
    return pl.pallas_call(
        body, name=name, grid=(nq, N_HEADS), in_specs=in_specs, out_specs=out_specs, out_shape=out_shape,
        scratch_shapes=scratch, input_output_aliases=aliases,
        compiler_params=_cp(("arbitrary", "arbitrary")))(*ins)


def _attn_bwd_call(q, k, v, c128, cr, o, lse, do, exchange=None):
    t, w = q.shape
    tq = ATT_TILE
    nq = t // tq
    r3 = lambda a: a.reshape(nq, tq, a.shape[-1])
    cr4 = cr.reshape(N_HEADS, nq, 1, tq)
    fused = exchange is not None

    def body(*refs):
        q_ref, k_ref, v_ref, c_ref, cr_ref, o_ref, lse_ref, do_ref = refs[:8]
        j, h = pl.program_id(0), pl.program_id(1)
        if fused:
            p_ref, dq_ref, dk_ref, dv_ref, dc_ref, dcr_ref, recv_ref, send_sems, recv_sems = refs[8:]
            start, finish = _chip_exchange_copies(p_ref, recv_ref, send_sems, recv_sems)
            pl.when((j == 0) & (h == 0))(start)
        else:
            dq_ref, dk_ref, dv_ref, dc_ref, dcr_ref = refs[8:]

        @pl.when((j == 0) & (h == 0))
        def _():
            dq_ref[...] = jnp.zeros_like(dq_ref)
            dc_ref[...] = jnp.zeros_like(dc_ref)

        hm = lax.broadcasted_iota(jnp.int32, (tq, w), 1) // HEAD_DIM == h
        lane = lax.broadcasted_iota(jnp.int32, (tq, 128), 1)
        kj = k_ref[...]
        vj = v_ref[...]
        ck = cr_ref[0, 0]
        causal = lax.broadcasted_iota(jnp.int32, (tq, tq), 0) >= lax.broadcasted_iota(jnp.int32, (tq, tq), 1)

        def step(i, carry, diagonal):
            dk, dv, dck = carry
            qm = jnp.where(hm, q_ref[i], 0.0)
            dom = jnp.where(hm, do_ref[i], 0.0)
            s = _dot(qm * ATT_SCALE, kj, 1, 1) - ck
            if diagonal:
                s = jnp.where(causal, s, NEG)
            p = jnp.exp(s + (_head_lane(c_ref[i], lane, h) - _head_lane(lse_ref[i], lane, h)))
            dv = dv + _dot(p, dom, 0, 0)
            dp = _dot(dom, vj, 1, 1)
            delta = jnp.sum(dom * o_ref[i], axis=1, keepdims=True)
            ds = p * (dp - delta)
            dq_ref[i] += jnp.where(hm, _dot(ds, kj, 1, 0), 0.0) * ATT_SCALE
            dk = dk + _dot(ds, qm, 0, 0) * ATT_SCALE
            dc_ref[i] += jnp.where(lane == h, jnp.sum(ds, axis=1, keepdims=True), 0.0)
            return dk, dv, dck - jnp.sum(ds, axis=0, keepdims=True)

        init = (jnp.zeros((tq, w), F32), jnp.zeros((tq, w), F32), jnp.zeros((1, tq), F32))
        carry = step(j, init, True)
        dk, dv, dck = lax.fori_loop(j + 1, nq, lambda i, c: step(i, c, False), carry)
        dcr_ref[0, 0] = dck

        @pl.when(h == 0)
        def _():
            dk_ref[...] = dk
            dv_ref[...] = dv

        @pl.when(h > 0)
        def _():
            dk_ref[...] += dk
            dv_ref[...] += dv

        if fused:
            pl.when((j == nq - 1) & (h == N_HEADS - 1))(finish)

    whole = pl.BlockSpec((nq, tq, w), lambda j, h: (0, 0, 0))
    whole_c = pl.BlockSpec((nq, tq, 128), lambda j, h: (0, 0, 0))
    tile = pl.BlockSpec((None, tq, w), lambda j, h: (j, 0, 0))
    tile_r = pl.BlockSpec((1, 1, 1, tq), lambda j, h: (h, j, 0, 0))
    s3 = jax.ShapeDtypeStruct((nq, tq, w), F32)
    b16 = lambda a: r3(a).astype(BF16)
    ins = [b16(q), b16(k), b16(v), r3(c128), cr4, r3(o), r3(lse), r3(do)]
    in_specs = [whole, tile, tile, whole_c, tile_r, whole, whole_c, whole]
    out_specs = [whole, tile, tile, whole_c, tile_r]
    out_shape = [s3, s3, s3, jax.ShapeDtypeStruct((nq, tq, 128), F32), jax.ShapeDtypeStruct((N_HEADS, nq, 1, tq), F32)]
    scratch = []
    if fused:
        ins.append(exchange)
        in_specs.append(HBM)
        out_specs.append(HBM)
        out_shape.append(jax.ShapeDtypeStruct((3,) + exchange.shape[1:], exchange.dtype))
        scratch = [pltpu.SemaphoreType.DMA((3,)), pltpu.SemaphoreType.DMA((3,))]
    dq, dk, dv, dc, dcr, *received = pl.pallas_call(
        body, name="attn_b_exchange" if fused else "attn_b", grid=(nq, N_HEADS), in_specs=in_specs,
        out_specs=out_specs, out_shape=out_shape, scratch_shapes=scratch,
        compiler_params=_cp(("arbitrary", "arbitrary")))(*ins)
    grads = (dq.reshape(t, w), dk.reshape(t, w), dv.reshape(t, w), dc.reshape(t, 128), dcr.reshape(N_HEADS, 1, t))
    return grads, (received[0] if fused else None)


def _loss_call(x, g, target):
    t, d = x.shape
    tr = 512

    def body(x_ref, g_ref, t_ref, loss_ref, dx_ref, dg_ref):
        tgt = t_ref[...]

        def f(xv, gv):
            return 0.5 * jnp.sum(jnp.mean(jnp.square(_rms(xv, gv) - tgt), axis=-1))

        val, vjp = jax.vjp(f, x_ref[...], g_ref[...])
        dx, dg = vjp(jnp.ones((), F32))
        dx_ref[...] = dx

        @pl.when(pl.program_id(0) == 0)
        def _():
            loss_ref[...] = jnp.zeros_like(loss_ref)
            dg_ref[...] = jnp.zeros_like(dg_ref)

        loss_ref[...] += jnp.full(loss_ref.shape, val, F32)
        dg_ref[...] += dg

    row = _row_spec(tr, d)
    return pl.pallas_call(
        body, name="loss_head", grid=(t // tr,), in_specs=[row, _full_spec(g), row],
        out_specs=[pl.BlockSpec((1, 128), lambda i: (0, 0)), row, _full_spec(g)],
        out_shape=[jax.ShapeDtypeStruct((1, 128), F32), jax.ShapeDtypeStruct((t, d), F32),
                   jax.ShapeDtypeStruct(g.shape, F32)],
        compiler_params=_cp(("arbitrary",)))(x, g, target)


def _blockdiag(w):
    l, g, a, b = w.shape
    return jnp.einsum('lgab,gk->lgakb', w, jnp.eye(g, dtype=w.dtype)).reshape(l, g * a, g * b)


def _prepare(rep):
    d = DEPTH
    w = MIXER_WIDTH
    rows = S5_GROUPS * S5_GROUP
    rep16 = lambda a: jnp.repeat(a, S5_GROUP, axis=1).reshape(d * rows, -1)
    bt = lambda b: b.transpose(0, 1, 3, 2).reshape(d * rows, S5_STATE)
    abar_re, abar_im, bb_re, bb_im = s5_disc(
        rep16(rep["s5_lambda_re"]), rep16(rep["s5_lambda_im"]), rep16(rep["s5_log_dt"][:, :, None]),
        bt(rep["s5_b_re"]), bt(rep["s5_b_im"]))
    g4 = lambda a: a.reshape(d, S5_GROUPS, S5_GROUP, S5_STATE)
    first = lambda a: g4(a)[:, :, 0, :].reshape(d, 1, S5_GROUPS * S5_STATE)
    cblk = lambda c: _blockdiag(c.transpose(0, 1, 3, 2))
    row = lambda a: a.reshape(d, 1, -1)
    return dict(
        sgu_norm_g=row(rep["sgu_norm_g"]), sgu_w=rep["sgu_w"],
        sgu_bias=jnp.repeat(rep["sgu_b"].transpose(0, 2, 1), HEAD_DIM, axis=2),
        abar_re=first(abar_re), abar_im=first(abar_im), bblk_re=_blockdiag(g4(bb_re)), bblk_im=_blockdiag(g4(bb_im)),
        cblk_re=cblk(rep["s5_c_re"]), cblk_im=cblk(rep["s5_c_im"]), s5_d=row(rep["s5_d"]), s5_glu_b=row(rep["s5_glu_b"]),
        lru_conv_b=row(rep["lru_conv_b"]), lru_wa=_blockdiag(rep["lru_wa"]), lru_ba=row(rep["lru_ba"]),
        lru_wx=_blockdiag(rep["lru_wx"]), lru_bx=row(rep["lru_bx"]), lru_lambda=row(rep["lru_lambda"]),
        fgate_b=jnp.pad(rep["fox_fgate_b"], ((0, 0), (0, 128 - N_HEADS)))[:, None, :])


PREPARED_FROM = ('sgu_norm_g', 'sgu_w', 'sgu_b', 's5_lambda_re', 's5_lambda_im', 's5_log_dt', 's5_b_re', 's5_b_im',
                 's5_c_re', 's5_c_im', 's5_d', 's5_glu_b', 'lru_conv_b', 'lru_wa', 'lru_ba', 'lru_wx', 'lru_bx',
                 'lru_lambda', 'fox_fgate_b')


def _mixers_pre(pieces, p, glu_w, conv_w):
    a_u, a_v, b_in, c_x, c_gate, d_q, d_k, d_v, d_f = pieces
    sw = p["sgu_w"]
    (y_a,) = sgu_mix(a_u, a_v, p["sgu_norm_g"], sw[0], sw[1], sw[2], sw[3], p["sgu_bias"])
    s_re, s_im = lti_scan(b_in, p["bblk_re"], p["bblk_im"], p["abar_re"], p["abar_im"])
    (y_b,) = s5_post(s_re, s_im, b_in, p["cblk_re"], p["cblk_im"], p["s5_d"], glu_w, p["s5_glu_b"])
    xc = causal_conv(c_x, conv_w, p["lru_conv_b"])
    a, b = lru_pre(xc, p["lru_wa"], p["lru_ba"], p["lru_wx"], p["lru_bx"], p["lru_lambda"])
    (y_c,) = lru_post(tv_scan(a, b), c_gate)
    (log_f,) = log_sig(d_f, p["fgate_b"])
    c128 = tv_scan(jnp.ones_like(log_f), log_f)
    return y_a, y_b, y_c, d_q, d_k, d_v, c128, c128[:, :N_HEADS].T[:, None, :]


PACK_COLS = 1024
SHARD_SHAPE = {'w_mlp_in': (1024, 1024), 'w_mlp_out': (1024, 1024), 'w_out': (256, 1024), 'w_in': (1024, 513),
               's5_glu_w': (64, 256), 'lru_conv_w': (4, 64)}
SHARDED_AXIS = {'w_in': 1, 's5_glu_w': 0, 'lru_conv_w': 1, 'w_out': 0, 'w_mlp_in': 1, 'w_mlp_out': 0}
SHARD_ROWS = {n: -(-s[0] * s[1] // PACK_COLS) for n, s in SHARD_SHAPE.items()}
SHARD_OFF = {n: sum(list(SHARD_ROWS.values())[:i]) for i, n in enumerate(SHARD_SHAPE)}
LAYER_ROWS = 2880
SMALL_OFF = SHARD_OFF['w_in']
GLUE_ROWS = LAYER_ROWS - SMALL_OFF
assert SHARD_OFF['w_mlp_out'] == 1024 and SHARD_OFF['w_out'] == 2048 and SMALL_OFF % GLUE_ROWS == 0
assert SHARD_OFF['lru_conv_w'] + SHARD_ROWS['lru_conv_w'] <= LAYER_ROWS
PACK_ROWS = DEPTH * LAYER_ROWS
TOK = 1024
FF = 4 * D_MODEL


def _w3(i_of):
    return pl.BlockSpec((None, 1024, PACK_COLS), i_of)


def _tile2(rows, cols, i_of):
    return pl.BlockSpec((rows, cols), i_of)


def _layer_fwd(x, h1, gathered, w_in, p, mix_p, next_shard, own_shard, next_gain):
    t = x.shape[0]
    nt = t // TOK
    f32 = lambda r, c: jax.ShapeDtypeStruct((r, c), F32)
    b16 = lambda r, c: jax.ShapeDtypeStruct((r, c), BF16)
    g2, gm = p["norm2_g"][None, :], p["mix_norm_g"][None, :]
    z = _matmul("mm_in", (nt, 1, 1), h1, _tile2(TOK, D_MODEL, lambda i, j, k: (i, 0)),
                w_in, _tile2(D_MODEL, D_IN_PAD, lambda i, j, k: (0, 0)), (1, 0),
                [(f32(t, D_IN_PAD), _tile2(TOK, D_IN_PAD, lambda i, j, k: (i, 0)))])
    pieces = tuple(jnp.split(z, [MIXER_WIDTH * i for i in range(1, 9)], axis=1))
    (y_a, y_b, y_c, d_q, d_k, d_v, c128, cr), mix_vjp = jax.vjp(_mixers_pre, pieces, *mix_p)
    attn_in = (d_q.astype(BF16), d_k.astype(BF16), d_v.astype(BF16), c128, cr)
    y_d, lse, *more = _attn_fwd_call(*attn_in, next_shard=next_shard,
                                     late=None if own_shard is None else (own_shard, gathered))
    if own_shard is not None:
        gathered = more.pop()
    w_out = _join_chips(_unpack_shards(gathered[:, None], ('w_out',))['w_out'], SHARDED_AXIS['w_out'])[0]
    ys = (y_a, y_b, y_c, y_d)
    (yn,) = _rowwise(_f_gnorm, list(ys), [gm], [D_MODEL], name="gnorm_f", tr=512, dtype=BF16)
    x_tile = _tile2(TOK, D_MODEL, lambda i, j, k: (i, 0))
    gain = _tile2(1, D_MODEL, lambda i, j, k: (0, 0))

    def add_and_norm(acc, r, gv):
        s = acc + r
        return s, _rms(s, gv)

    x1, h2 = _matmul("mm_out", (nt, 1, 1), yn, x_tile, w_out, _tile2(D_MODEL, D_MODEL, lambda i, j, k: (0, 0)), (1, 0),
                     [(f32(t, D_MODEL), x_tile), (b16(t, D_MODEL), x_tile)], extras=[(x, x_tile), (g2, gain)],
                     epilogue=add_and_norm)
    ff_tile = _tile2(TOK, 1024, lambda i, j, k: (i, j))
    act = _matmul("mm_up", (nt, FF // 1024, 1), h2, x_tile, gathered, _w3(lambda i, j, k: (j, 0, 0)), (1, 0),
                  [(b16(t, FF), ff_tile)], epilogue=lambda acc: (jnp.square(jnp.maximum(acc, 0.0)),))
    down = ("mm_down", (nt, 1, FF // 1024), act, _tile2(TOK, 1024, lambda i, j, k: (i, k)),
            gathered, _w3(lambda i, j, k: (k, 1, 0)), (1, 0))
    if next_gain is None:
        x2, h_next = _matmul(*down, [(f32(t, D_MODEL), x_tile)], extras=[(x1, x_tile)],
                             epilogue=lambda acc, r: (acc + r,)), None
    else:
        x2, h_next = _matmul(*down, [(f32(t, D_MODEL), x_tile), (b16(t, D_MODEL), x_tile)],
                             extras=[(x1, x_tile), (next_gain[None, :], gain)], epilogue=add_and_norm)
    res = (x, h1, mix_vjp, ys, attn_in, lse, yn, x1, h2, act, gathered, w_out)
    return x2, h_next, res, (more[0] if more else None)


def _layer_bwd(g, res, w_in, p, later_send):
    x, h1, mix_vjp, ys, attn_in, lse, yn, x1, h2, act, gathered, w_out = res
    half_rows = LAYER_ROWS // 2
    riding = later_send is not None
    halves = later_send.reshape(4, 2, half_rows, PACK_COLS) if riding else None
    send = lax.empty((4, LAYER_ROWS, PACK_COLS), BF16)
    t = x.shape[0]
    nt = t // TOK
    f32 = lambda r, c: jax.ShapeDtypeStruct((r, c), F32)
    g1, g2, gm = p["norm1_g"][None, :], p["norm2_g"][None, :], p["mix_norm_g"][None, :]
    x_tile = _tile2(TOK, D_MODEL, lambda i, j, k: (i, 0))
    ff_tile = _tile2(TOK, 1024, lambda i, j, k: (i, j))
    tok_k = _tile2(TOK, D_MODEL, lambda i, j, k: (k, 0))
    send_s = jax.ShapeDtypeStruct(send.shape, send.dtype)
    pair_rider = (halves, jax.ShapeDtypeStruct((4, half_rows, PACK_COLS), BF16), False, _pair_exchange_copies)
    du = _matmul("mm_down_dx", (nt, FF // 1024, 1), g, x_tile, gathered, _w3(lambda i, j, k: (j, 1, 0)), (1, 1),
                 [(jax.ShapeDtypeStruct((t, FF), BF16), ff_tile)], extras=[(act, ff_tile)],
                 epilogue=lambda acc, a: (2.0 * jnp.sqrt(a.astype(F32)) * acc,),
                 rider=pair_rider if riding else None)
    exchange = None
    if riding:
        du, from_sibling = du
        exchange = _add_kept(halves, from_sibling, REDUCE_ROWS)
    send = _matmul("mm_down_dw", (FF // 1024, 1, nt), act, _tile2(TOK, 1024, lambda i, j, k: (k, i)), g, tok_k, (0, 0),
                   [(send_s, _w3(lambda i, j, k: (i, 1, 0)))], into=send)
    send = _matmul("mm_up_dw", (1, FF // 1024, nt), h2, tok_k, du, _tile2(TOK, 1024, lambda i, j, k: (k, j)), (0, 0),
                   [(send_s, _w3(lambda i, j, k: (j, 0, 0)))], into=send)
    gain = _tile2(1, D_MODEL, lambda i, j, k: (0, 0))

    def norm_bwd(dh, xv, gv, through):
        _, vjp = jax.vjp(_rms, xv, gv)
        dxv, dgv = vjp(dh)
        return dxv + through, dgv

    g_mid, dg2 = _matmul("mm_up_dx", (nt, 1, FF // 1024), du, _tile2(TOK, 1024, lambda i, j, k: (i, k)),
                         gathered, _w3(lambda i, j, k: (k, 0, 0)), (1, 1),
                         [(f32(t, D_MODEL), x_tile), (f32(1, D_MODEL), gain)],
                         extras=[(x1, x_tile), (g2, gain), (g, x_tile)], epilogue=norm_bwd, summed=1)
    w_full = _tile2(D_MODEL, D_MODEL, lambda i, j, k: (0, 0))
    dyn = _matmul("mm_out_dx", (nt, 1, 1), g_mid, x_tile, w_out, w_full, (1, 1), [(f32(t, D_MODEL), x_tile)])
    quarter = D_MODEL // 4
    send = _matmul("mm_out_dw", (1, 1, nt), yn, tok_k, g_mid, tok_k, (0, 0),
                   [(send_s, pl.BlockSpec((4, quarter, PACK_COLS), lambda i, j, k: (0, SHARD_OFF['w_out'] // quarter, 0)))],
                   epilogue=lambda acc: (acc.reshape(4, quarter, PACK_COLS),), into=send, acc_shape=(D_MODEL, D_MODEL))
    dy_a, dy_b, dy_c, dy_d, dgm = _rowwise_vjp(_f_gnorm, list(ys), [gm], [dyn], name="gnorm_b", tr=512)
    d_attn_in, received = _attn_bwd_call(*attn_in, ys[3], lse, dy_d, exchange=exchange)
    d_pieces, *d_mix = mix_vjp((dy_a, dy_b, dy_c, *d_attn_in))
    dz = jnp.concatenate([d.astype(BF16) for d in d_pieces], axis=1)
    z_tile = _tile2(TOK, D_IN_PAD, lambda i, j, k: (i, 0))
    share_rider = None
    if riding:
        pair = _sum_chips(exchange, received, REDUCE_ROWS)
        share_rider = (pair, jax.ShapeDtypeStruct(pair.shape, pair.dtype), True, _pair_share_copies)
    d_w_in = _matmul("mm_in_dw", (1, 1, t // 512), h1, _tile2(512, D_MODEL, lambda i, j, k: (k, 0)),
                     dz, _tile2(512, D_IN_PAD, lambda i, j, k: (k, 0)), (0, 0),
                     [(f32(D_MODEL, D_IN_PAD), _tile2(D_MODEL, D_IN_PAD, lambda i, j, k: (0, 0)))], rider=share_rider)
    reduced = None
    if riding:
        d_w_in, pair = d_w_in
        reduced = pair.reshape(LAYER_ROWS, PACK_COLS)
    dx, dg1 = _matmul("mm_in_dx", (nt, 1, 1), dz, z_tile, w_in, _tile2(D_MODEL, D_IN_PAD, lambda i, j, k: (0, 0)), (1, 1),
                      [(f32(t, D_MODEL), x_tile), (f32(1, D_MODEL), gain)],
                      extras=[(x, x_tile), (g1, gain), (g_mid, x_tile)], epilogue=norm_bwd, summed=1)
    norms = dict(norm1_g=dg1[0], norm2_g=dg2[0], mix_norm_g=dgm[0])
    return dx, norms, d_mix, d_w_in, send, reduced


HBM = pl.BlockSpec(memory_space=pltpu.HBM)
D2D_CHUNKS = 15
ICI_CHUNKS = 5
VMEM_CHUNKS = 4


def _coords():
    return lax.axis_index("x"), lax.axis_index("y"), lax.axis_index("c")


def _other_chips(x, y):
    return [(1 - x, y), (x, 1 - y), (1 - x, 1 - y)]


def _start_chunks(make, rows, n):
    size = rows // n
    assert size * n == rows
    for k in range(n):
        make(pl.ds(k * size, size)).start()


AG_SEMS = 7


ALL_ROWS = (0, LAYER_ROWS, ICI_CHUNKS, D2D_CHUNKS)
EARLY_ROWS = (SMALL_OFF, GLUE_ROWS, 4, 4)
LATE_ROWS = (0, SMALL_OFF, 4, 12)


def _allgather_copies(in_ref, out_ref, send_sems, recv_sems, part=ALL_ROWS):
    r0, r, ici_chunks, d2d_chunks = part
    rh = r // 2
    x, y, c = _coords()
    me, sibling = (x, y, c), (x, y, 1 - c)
    chips = _other_chips(x, y)

    def half(px, py, pc, rows=pl.ds(0, rh)):
        return out_ref.at[2 * px + py, pl.ds(r0 + pc * rh + rows.start, rows.size), :]

    def copy(k, block, to, rows=pl.ds(0, rh), from_input=False):
        src = in_ref.at[pl.ds(r0 + block[2] * rh + rows.start, rows.size), :] if from_input else half(*block, rows)
        return pltpu.make_async_remote_copy(
            src_ref=src, dst_ref=half(*block, rows), send_sem=send_sems.at[k], recv_sem=recv_sems.at[k],
            device_id=to, device_id_type=MESH)

    def own(rows=pl.ds(0, r)):
        mine = pl.ds(r0 + rows.start, rows.size)
        return pltpu.make_async_remote_copy(
            src_ref=in_ref.at[mine, :], dst_ref=out_ref.at[2 * x + y, mine, :], send_sem=send_sems.at[6],
            recv_sem=recv_sems.at[6], device_id=sibling, device_id_type=MESH)

    def start():
        for j, chip in enumerate(chips):
            _start_chunks(lambda rows: copy(j, me, (*chip, c), rows, from_input=True), rh, ici_chunks)
        _start_chunks(own, r, d2d_chunks)

    def finish():
        for j, chip in enumerate(chips):
            copy(j, (*chip, c), me).wait_recv()
            _start_chunks(lambda rows: copy(3 + j, (*chip, c), sibling, rows), rh, d2d_chunks)
        for j, chip in enumerate(chips):
            copy(3 + j, (*chip, 1 - c), me).wait_recv()
        for j, chip in enumerate(chips):
            copy(j, me, (*chip, c), from_input=True).wait_send()
            copy(3 + j, (*chip, c), sibling).wait_send()
        own().wait()

    return start, finish


def _allgather_shards(shard, part):
    def body(in_ref, out_ref, send_sems, recv_sems):
        start, finish = _allgather_copies(in_ref, out_ref, send_sems, recv_sems, part)
        start()
        finish()

    return pl.pallas_call(
        body, name="allgather_shards", out_shape=jax.ShapeDtypeStruct((4,) + shard.shape, shard.dtype),
        in_specs=[HBM], out_specs=HBM,
        scratch_shapes=[pltpu.SemaphoreType.DMA((AG_SEMS,)), pltpu.SemaphoreType.DMA((AG_SEMS,))],
        compiler_params=pltpu.CompilerParams())(shard)


def _pair_exchange_copies(g_ref, recv_ref, send_sem, recv_sem):
    s, _, rh, _ = g_ref.shape
    x, y, c = _coords()

    def copy(slot, rows):
        return pltpu.make_async_remote_copy(
            src_ref=g_ref.at[slot, 1 - c, rows, :], dst_ref=recv_ref.at[slot, rows, :], send_sem=send_sem,
            recv_sem=recv_sem, device_id=(x, y, 1 - c), device_id_type=MESH)

    def start():
        for slot in range(s):
            _start_chunks(lambda rows: copy(slot, rows), rh, VMEM_CHUNKS)

    def finish():
        pltpu.make_async_remote_copy(
            src_ref=g_ref.at[:, 1 - c], dst_ref=recv_ref, send_sem=send_sem, recv_sem=recv_sem,
            device_id=(x, y, 1 - c), device_id_type=MESH).wait()

    return start, finish


def _pair_share_copies(in_ref, out_ref, send_sem, recv_sem):
    rh = in_ref.shape[1]
    x, y, c = _coords()

    def copy(slot, rows=pl.ds(0, rh)):
        return pltpu.make_async_remote_copy(
            src_ref=in_ref.at[slot, rows, :], dst_ref=out_ref.at[slot, rows, :], send_sem=send_sem,
            recv_sem=recv_sem, device_id=(x, y, 1 - c), device_id_type=MESH)

    def start():
        _start_chunks(lambda rows: copy(c, rows), rh, D2D_CHUNKS)

    def finish():
        copy(c).wait_send()
        copy(1 - c).wait_recv()

    return start, finish


def _pair_exchange(g):
    s, _, rh, cols = g.shape

    def body(g_ref, recv_ref, send_sem, recv_sem):
        start, finish = _pair_exchange_copies(g_ref, recv_ref, send_sem, recv_sem)
        start()
        finish()

    return pl.pallas_call(
        body, name="pair_exchange", out_shape=jax.ShapeDtypeStruct((s, rh, cols), g.dtype), in_specs=[HBM],
        out_specs=HBM, scratch_shapes=[pltpu.SemaphoreType.DMA] * 2, compiler_params=pltpu.CompilerParams())(g)


def _chip_exchange_copies(p_ref, recv_ref, send_sems, recv_sems):
    rh = p_ref.shape[1]
    x, y, c = _coords()
    chips = _other_chips(x, y)

    def copy(j, chip, rows=pl.ds(0, rh)):
        return pltpu.make_async_remote_copy(
            src_ref=p_ref.at[2 * chip[0] + chip[1], rows, :], dst_ref=recv_ref.at[j, rows, :],
            send_sem=send_sems.at[j], recv_sem=recv_sems.at[j], device_id=(*chip, c), device_id_type=MESH)

    def start():
        for j, chip in enumerate(chips):
            _start_chunks(lambda rows: copy(j, chip, rows), rh, ICI_CHUNKS)

    def finish():
        for j, chip in enumerate(chips):
            copy(j, chip).wait_recv()
        for j, chip in enumerate(chips):
            copy(j, chip).wait_send()

    return start, finish


def _chip_exchange(p):
    def body(p_ref, recv_ref, send_sems, recv_sems):
        start, finish = _chip_exchange_copies(p_ref, recv_ref, send_sems, recv_sems)
        start()
        finish()

    return pl.pallas_call(
        body, name="chip_exchange", out_shape=jax.ShapeDtypeStruct((3,) + p.shape[1:], p.dtype), in_specs=[HBM],
        out_specs=HBM, scratch_shapes=[pltpu.SemaphoreType.DMA((3,)), pltpu.SemaphoreType.DMA((3,))],
        compiler_params=pltpu.CompilerParams())(p)


def _sum_chips(p, recv, tr):
    _, rh, cols = p.shape
    x, y, c = _coords()
    where = jnp.stack([2 * x + y, c]).astype(jnp.int32)

    def body(w_ref, own_ref, r_ref, o_ref):
        acc = own_ref[...].astype(F32)
        for k in range(3):
            acc = acc + r_ref[k].astype(F32)
        o_ref[...] = acc

    return pl.pallas_call(
        body, name="sum_chips", out_shape=jax.ShapeDtypeStruct((2, rh, cols), F32),
        grid_spec=pltpu.PrefetchScalarGridSpec(
            num_scalar_prefetch=1, grid=(rh // tr,),
            in_specs=[pl.BlockSpec((None, tr, cols), lambda i, w_ref: (w_ref[0], i, 0)),
                      pl.BlockSpec((3, tr, cols), lambda i, w_ref: (0, i, 0))],
            out_specs=pl.BlockSpec((None, tr, cols), lambda i, w_ref: (w_ref[1], i, 0))),
        compiler_params=_cp(("arbitrary",)))(where, p, recv)


def _pair_share(buf):
    _, rh, cols = buf.shape

    def body(in_ref, out_ref, send_sem, recv_sem):
        start, finish = _pair_share_copies(in_ref, out_ref, send_sem, recv_sem)
        start()
        finish()

    return pl.pallas_call(
        body, name="pair_share", out_shape=jax.ShapeDtypeStruct(buf.shape, buf.dtype), in_specs=[HBM], out_specs=HBM,
        scratch_shapes=[pltpu.SemaphoreType.DMA] * 2, input_output_aliases={0: 0},
        compiler_params=pltpu.CompilerParams())(buf)


def _allgather_all(blk):
    m_per, cols = blk.shape
    whole = pl.ds(0, m_per)

    def body(x_ref, out_ref, send_sems, recv_sems, local_sem):
        x, y, c = _coords()
        me, sibling = (x, y, c), (x, y, 1 - c)
        chips = _other_chips(x, y)

        def rows_of(px, py, pc, rows):
            return out_ref.at[4 * px + 2 * py + pc, rows, :]

        def copy(k, block, to, rows=whole, from_input=False):
            return pltpu.make_async_remote_copy(
                src_ref=x_ref.at[rows, :] if from_input else rows_of(*block, rows), dst_ref=rows_of(*block, rows),
                send_sem=send_sems.at[k], recv_sem=recv_sems.at[k], device_id=to, device_id_type=MESH)

        mine = pltpu.make_async_copy(x_ref, rows_of(*me, whole), local_sem)
        mine.start()
        _start_chunks(lambda rows: copy(0, me, sibling, rows, from_input=True), m_per, VMEM_CHUNKS)
        for j, chip in enumerate(chips):
            _start_chunks(lambda rows: copy(1 + j, me, (*chip, c), rows, from_input=True), m_per, VMEM_CHUNKS)
        for j, chip in enumerate(chips):
            copy(1 + j, (*chip, c), me).wait_recv()
            _start_chunks(lambda rows: copy(4 + j, (*chip, c), sibling, rows), m_per, VMEM_CHUNKS)
        copy(0, sibling, me).wait_recv()
        for j, chip in enumerate(chips):
            copy(4 + j, (*chip, 1 - c), me).wait_recv()
        copy(0, me, sibling, from_input=True).wait_send()
        for j, chip in enumerate(chips):
            copy(1 + j, me, (*chip, c), from_input=True).wait_send()
            copy(4 + j, (*chip, c), sibling).wait_send()
        mine.wait()

    return pl.pallas_call(
        body, name="allgather_all", out_shape=jax.ShapeDtypeStruct((8, m_per, cols), blk.dtype),
        in_specs=[pl.BlockSpec(memory_space=pltpu.VMEM)], out_specs=pl.BlockSpec(memory_space=pltpu.VMEM),
        scratch_shapes=[pltpu.SemaphoreType.DMA((7,)), pltpu.SemaphoreType.DMA((7,)), pltpu.SemaphoreType.DMA],
        compiler_params=pltpu.CompilerParams(vmem_limit_bytes=VMEM_LIMIT))(blk)


def _add_kept(g, recv, tr):
    s, _, rh, cols = g.shape

    def body(c_ref, a_ref, b_ref, o_ref):
        o_ref[...] = (a_ref[...].astype(F32) + b_ref[...].astype(F32)).astype(o_ref.dtype)

    spec = pl.BlockSpec((None, tr, cols), lambda si, i, c_ref: (si, i, 0))
    return pl.pallas_call(
        body, name="add_kept", out_shape=jax.ShapeDtypeStruct((s, rh, cols), BF16),
        grid_spec=pltpu.PrefetchScalarGridSpec(
            num_scalar_prefetch=1, grid=(s, rh // tr),
            in_specs=[pl.BlockSpec((None, None, tr, cols), lambda si, i, c_ref: (si, c_ref[0], i, 0)), spec],
            out_specs=spec),
        compiler_params=_cp(("arbitrary", "arbitrary")))(lax.axis_index("c").astype(jnp.int32).reshape(1), g, recv)


def _sum_slots(p, tr, name):
    s, rows, cols = p.shape

    def body(p_ref, o_ref):
        acc = p_ref[0].astype(F32)
        for k in range(1, s):
            acc = acc + p_ref[k].astype(F32)
        o_ref[...] = acc

    return pl.pallas_call(
        body, name=name, grid=(rows // tr,), in_specs=[pl.BlockSpec((s, tr, cols), lambda i: (0, i, 0))],
        out_specs=_row_spec(tr, cols), out_shape=jax.ShapeDtypeStruct((rows, cols), F32),
        compiler_params=_cp(("arbitrary",)))(p)


def _adamw_call(w, g, m, v, name):
    rows, cols = w.shape
    tr = _tile(rows, 512) if rows % 512 == 0 else _tile(rows, 128)
    c1 = 1.0 - ADAM_B1 ** ADAM_STEP
    c2 = 1.0 - ADAM_B2 ** ADAM_STEP

    def body(w_ref, g_ref, m_ref, v_ref, d_ref, nm_ref, nv_ref):
        gv = g_ref[...]
        nm = ADAM_B1 * m_ref[...] + (1.0 - ADAM_B1) * gv
        nv = ADAM_B2 * v_ref[...] + (1.0 - ADAM_B2) * jnp.square(gv)
        d_ref[...] = -ADAM_LR * ((nm / c1) / (jnp.sqrt(nv / c2) + ADAM_EPS) + ADAM_WD * w_ref[...])
        nm_ref[...] = nm
        nv_ref[...] = nv

    spec = _row_spec(tr, cols)
    o = jax.ShapeDtypeStruct((rows, cols), F32)
    return pl.pallas_call(body, name=name, grid=(rows // tr,), in_specs=[spec] * 4, out_specs=[spec] * 3,
                          out_shape=[o, o, o], compiler_params=_cp(("arbitrary",)))(w, g, m, v)


WEIGHTS = ('norm1_g', 'w_in', 'sgu_norm_g', 'sgu_w', 'sgu_b', 's5_lambda_re', 's5_lambda_im', 's5_log_dt',
           's5_b_re', 's5_b_im', 's5_c_re', 's5_c_im', 's5_d', 's5_glu_w', 's5_glu_b', 'lru_conv_w',
           'lru_conv_b', 'lru_wa', 'lru_ba', 'lru_wx', 'lru_bx', 'lru_lambda', 'fox_fgate_b', 'mix_norm_g',
           'w_out', 'norm2_g', 'w_mlp_in', 'w_mlp_out', 'final_g')
N_W = len(WEIGHTS)


def _pack_shards(shards, dtype, names=tuple(SHARD_SHAPE), rows=LAYER_ROWS):
    parts = []
    for n in names:
        lead = shards[n].shape[:-2]
        flat = shards[n].reshape(*lead, -1).astype(dtype)
        flat = jnp.pad(flat, [(0, 0)] * len(lead) + [(0, SHARD_ROWS[n] * PACK_COLS - flat.shape[-1])])
        parts.append(flat.reshape(*lead, SHARD_ROWS[n], PACK_COLS))
    lead = parts[0].shape[:-2]
    used = sum(SHARD_ROWS[n] for n in names)
    if rows > used:
        parts.append(jnp.zeros((*lead, rows - used, PACK_COLS), dtype))
    return jnp.concatenate(parts, axis=-2)


def _unpack_shards(buf, names=tuple(SHARD_SHAPE)):
    lead = buf.shape[:-2]
    out = {}
    for n in names:
        s0, s1 = SHARD_SHAPE[n]
        rows, off = SHARD_ROWS[n], SHARD_OFF[n]
        flat = buf[..., off:off + rows, :].reshape(*lead, rows * PACK_COLS)
        out[n] = flat[..., :s0 * s1].reshape(*lead, s0, s1)
    return out


def _join_chips(g, axis):
    _, d, s0, s1 = g.shape
    if axis == 0:
        return g.transpose(1, 0, 2, 3).reshape(d, 4 * s0, s1)
    return g.transpose(1, 2, 0, 3).reshape(d, s0, 4 * s1)


def _split_chips(w, axis):
    d = w.shape[0]
    if axis == 0:
        return w.reshape(d, 4, w.shape[1] // 4, w.shape[2]).transpose(1, 0, 2, 3)
    return w.reshape(d, w.shape[1], 4, w.shape[2] // 4).transpose(2, 0, 1, 3)


def _flat_rows(shape):
    return -(-math.prod(shape) // PACK_COLS)


def _pack_flat(arrs, rows, dtype=F32):
    parts = []
    for a in arrs:
        flat = a.reshape(-1).astype(dtype)
        r = _flat_rows(a.shape)
        parts.append(jnp.pad(flat, (0, r * PACK_COLS - flat.shape[0])).reshape(r, PACK_COLS))
    used = sum(p.shape[0] for p in parts)
    parts.append(jnp.zeros((rows - used, PACK_COLS), dtype))
    return jnp.concatenate(parts, axis=0)


def _split3(s):
    hi = s.astype(BF16).astype(F32)
    mid = (s - hi).astype(BF16).astype(F32)
    return jnp.stack([hi, mid, s - hi - mid])


def _unpack_flat(buf, shapes):
    out, off = [], 0
    for s in shapes:
        r = _flat_rows(s)
        out.append(buf[off:off + r].reshape(-1)[:math.prod(s)].reshape(s))
        off += r
    return out


def _write_rows(buf, src, row_offset):
    n, r, cols = src.shape

    def body(s_ref, b_ref, o_ref):
        o_ref[...] = s_ref[...].astype(o_ref.dtype)

    blk = (None, r, cols)
    return pl.pallas_call(
        body, name="write_rows", grid=(n,),
        in_specs=[pl.BlockSpec(blk, lambda s: (s, 0, 0)), pl.BlockSpec(memory_space=pl.ANY)],
        out_specs=pl.BlockSpec(blk, lambda s: (s, row_offset // r, 0)),
        out_shape=jax.ShapeDtypeStruct(buf.shape, buf.dtype), input_output_aliases={1: 0},
        compiler_params=_cp(("arbitrary",)))(src, buf)


def _write_glue(send, glue):
    return _write_rows(send, glue, SMALL_OFF)


GLUE_PACKED = ('w_in', 's5_glu_w', 'lru_conv_w')
REDUCE_ROWS = LAYER_ROWS // 4


def _pack_weights(w):
    buf = lax.empty((DEPTH, LAYER_ROWS, PACK_COLS), BF16)
    for n in ('w_mlp_in', 'w_mlp_out', 'w_out'):
        buf = _write_rows(buf, w[n], SHARD_OFF[n])
    return _write_rows(buf, _pack_shards(w, BF16, names=GLUE_PACKED, rows=GLUE_ROWS), SMALL_OFF)


def _layer_weights(gathered):
    parts = _unpack_shards(gathered[:, None], GLUE_PACKED)
    joined = {n: _join_chips(g, SHARDED_AXIS[n])[0] for n, g in parts.items()}
    joined['w_in'] = jnp.pad(joined['w_in'], ((0, 0), (0, D_IN_PAD - D_IN_PROJ)))
    return joined


def _reduce_start(send):
    halves = send.reshape(4, 2, LAYER_ROWS // 2, PACK_COLS)
    return _add_kept(halves, _pair_exchange(halves), REDUCE_ROWS)


def _reduce_finish(chip_sum, received):
    return _pair_share(_sum_chips(chip_sum, received, REDUCE_ROWS)).reshape(LAYER_ROWS, PACK_COLS)


def _forward_backward(x, target, final_g, shards, rep):
    norm_p = [{n: rep[n][l] for n in ('norm1_g', 'norm2_g', 'mix_norm_g')} for l in range(DEPTH)]
    prepared, prepare_vjp = jax.vjp(_prepare, {n: rep[n] for n in PREPARED_FROM})
    gathered = _allgather_shards(shards[0], EARLY_ROWS)
    (h,) = _rowwise(_f_rms, [x], [norm_p[0]["norm1_g"][None, :]], [D_MODEL], name="rms_f", tr=512, dtype=BF16)
    layers = []
    for l in range(DEPTH):
        lw = _layer_weights(gathered)
        mix_p = ({n: a[l] for n, a in prepared.items()}, lw['s5_glu_w'].astype(F32), lw['lru_conv_w'].astype(F32))
        last = l + 1 == DEPTH
        x, h, res, gathered = _layer_fwd(
            x, h, gathered, lw['w_in'], norm_p[l], mix_p, None if last else shards[l + 1],
            shards[0] if l == 0 else None, None if last else norm_p[l + 1]["norm1_g"])
        layers.append((res, lw))
    loss_part, g, d_final = _loss_call(x, final_g[None, :], target)

    norms, d_prepared, reduced, later_send = [None] * DEPTH, [None] * DEPTH, [None] * DEPTH, None
    for l in reversed(range(DEPTH)):
        res, lw = layers[l]
        g, norms[l], (d_prepared[l], d_glu_w, d_conv_w), d_w_in, send, later_reduced = _layer_bwd(
            g, res, lw['w_in'], norm_p[l], later_send)
        if later_send is not None:
            reduced[l + 1] = later_reduced
        mine = {'w_in': d_w_in[:, :D_IN_PROJ], 's5_glu_w': d_glu_w, 'lru_conv_w': d_conv_w}
        glue = _pack_shards({n: _split_chips(a[None], SHARDED_AXIS[n]) for n, a in mine.items()}, BF16,
                            names=GLUE_PACKED, rows=GLUE_ROWS)[:, 0]
        later_send = _write_glue(send, glue)
    chip_sum = _reduce_start(later_send)
    reduced[0] = _reduce_finish(chip_sum, _chip_exchange(chip_sum))
    stack = lambda per_layer: {n: jnp.stack([per_layer[l][n] for l in range(DEPTH)]) for n in per_layer[0]}
    (d_rep,) = prepare_vjp(stack(d_prepared))
    return loss_part, g, d_final, dict(d_rep, **stack(norms)), _unpack_shards(jnp.stack(reduced))


def _step(*args):
    x, target = args[0], args[1 + N_W]
    w = dict(zip(WEIGHTS, args[1:1 + N_W]))
    m = dict(zip(WEIGHTS, args[2 + N_W:2 + 2 * N_W]))
    v = dict(zip(WEIGHTS, args[2 + 2 * N_W:2 + 3 * N_W]))
    small = [n for n in WEIGHTS if n not in SHARD_SHAPE]

    shards = _pack_weights({n: w[n] for n in SHARD_SHAPE})
    loss_part, dx, d_final, dw, g_shard = _forward_backward(
        x[0], target[0], w['final_g'], shards, {n: w[n] for n in small})

    small_g = [d_final.reshape(-1) if n == 'final_g' else dw[n] for n in small]
    small_rows = -(-(sum(_flat_rows(w[n].shape) for n in small) + 1) // 128) * 128
    mine = _pack_flat(small_g + [_split3(loss_part[0, 0])], small_rows, BF16)
    small_sum = _sum_slots(_allgather_all(mine), 128, "sum_devices")
    *g_small, loss = _unpack_flat(small_sum, [w[n].shape for n in small] + [(3,)])
    loss = jnp.sum(loss)

    grads, delta, new_m, new_v = {}, {}, {}, {}
    for n in SHARD_SHAPE:
        shp = w[n].shape
        v2 = lambda a: a.reshape(-1, shp[-1])
        res = _adamw_call(v2(w[n]), v2(g_shard[n]), v2(m[n]), v2(v[n]), "adamw_" + n)
        grads[n] = g_shard[n]
        delta[n], new_m[n], new_v[n] = (r.reshape(shp) for r in res)
    pk = lambda d: _pack_flat([d[n] for n in small], small_rows)
    res = _adamw_call(pk(w), _pack_flat(g_small, small_rows), pk(m), pk(v), "adamw_small")
    shapes = [w[n].shape for n in small]
    for n, g, d_, m_, v_ in zip(small, g_small, *(_unpack_flat(r, shapes) for r in res)):
        grads[n], delta[n], new_m[n], new_v[n] = g, d_, m_, v_

    return (loss, dx[None], *[grads[n] for n in WEIGHTS], *[delta[n] for n in WEIGHTS],
            *[new_m[n] for n in WEIGHTS], *[new_v[n] for n in WEIGHTS])


def kernel(x, norm1_g, w_in, sgu_norm_g, sgu_w, sgu_b, s5_lambda_re, s5_lambda_im, s5_log_dt, s5_b_re, s5_b_im, s5_c_re, s5_c_im, s5_d, s5_glu_w, s5_glu_b, lru_conv_w, lru_conv_b, lru_wa, lru_ba, lru_wx, lru_bx, lru_lambda, fox_fgate_b, mix_norm_g, w_out, norm2_g, w_mlp_in, w_mlp_out, final_g, loss_target, m_norm1_g, m_w_in, m_sgu_norm_g, m_sgu_w, m_sgu_b, m_s5_lambda_re, m_s5_lambda_im, m_s5_log_dt, m_s5_b_re, m_s5_b_im, m_s5_c_re, m_s5_c_im, m_s5_d, m_s5_glu_w, m_s5_glu_b, m_lru_conv_w, m_lru_conv_b, m_lru_wa, m_lru_ba, m_lru_wx, m_lru_bx, m_lru_lambda, m_fox_fgate_b, m_mix_norm_g, m_w_out, m_norm2_g, m_w_mlp_in, m_w_mlp_out, m_final_g, v_norm1_g, v_w_in, v_sgu_norm_g, v_sgu_w, v_sgu_b, v_s5_lambda_re, v_s5_lambda_im, v_s5_log_dt, v_s5_b_re, v_s5_b_im, v_s5_c_re, v_s5_c_im, v_s5_d, v_s5_glu_w, v_s5_glu_b, v_lru_conv_w, v_lru_conv_b, v_lru_wa, v_lru_ba, v_lru_wx, v_lru_bx, v_lru_lambda, v_fox_fgate_b, v_mix_norm_g, v_w_out, v_norm2_g, v_w_mlp_in, v_w_mlp_out, v_final_g):
    return _step(x, norm1_g, w_in, sgu_norm_g, sgu_w, sgu_b, s5_lambda_re, s5_lambda_im, s5_log_dt, s5_b_re, s5_b_im, s5_c_re, s5_c_im, s5_d, s5_glu_w, s5_glu_b, lru_conv_w, lru_conv_b, lru_wa, lru_ba, lru_wx, lru_bx, lru_lambda, fox_fgate_b, mix_norm_g, w_out, norm2_g, w_mlp_in, w_mlp_out, final_g, loss_target, m_norm1_g, m_w_in, m_sgu_norm_g, m_sgu_w, m_sgu_b, m_s5_lambda_re, m_s5_lambda_im, m_s5_log_dt, m_s5_b_re, m_s5_b_im, m_s5_c_re, m_s5_c_im, m_s5_d, m_s5_glu_w, m_s5_glu_b, m_lru_conv_w, m_lru_conv_b, m_lru_wa, m_lru_ba, m_lru_wx, m_lru_bx, m_lru_lambda, m_fox_fgate_b, m_mix_norm_g, m_w_out, m_norm2_g, m_w_mlp_in, m_w_mlp_out, m_final_g, v_norm1_g, v_w_in, v_sgu_norm_g, v_sgu_w, v_sgu_b, v_s5_lambda_re, v_s5_lambda_im, v_s5_log_dt, v_s5_b_re, v_s5_b_im, v_s5_c_re, v_s5_c_im, v_s5_d, v_s5_glu_w, v_s5_glu_b, v_lru_conv_w, v_lru_conv_b, v_lru_wa, v_lru_ba, v_lru_wx, v_lru_bx, v_lru_lambda, v_fox_fgate_b, v_mix_norm_g, v_w_out, v_norm2_g, v_w_mlp_in, v_w_mlp_out, v_final_g)
```

```python
import functools
import math

import jax
import jax.numpy as jnp
from jax import lax
from jax.experimental import pallas as pl
from jax.experimental.pallas import tpu as pltpu

F32 = jnp.float32
BF16 = jnp.bfloat16

DEPTH = 4
D_MODEL = 1024
MIXER_WIDTH = 256
SGU_CHUNK = 128
N_HEADS = 4
HEAD_DIM = 64
S5_GROUPS = 16
S5_GROUP = 16
S5_STATE = 64
LRU_C = 8.0
RMS_EPS = 1e-6
D_IN_PROJ = 8 * MIXER_WIDTH + N_HEADS
D_IN_PAD = 8 * MIXER_WIDTH + 128
ADAM_LR, ADAM_B1, ADAM_B2, ADAM_EPS, ADAM_WD, ADAM_STEP = 0.001, 0.9, 0.999, 1e-08, 0.01, 10

V7X_VMEM_BYTES = 64 * 1024 * 1024
VMEM_LIMIT = V7X_VMEM_BYTES - 8 * 1024 * 1024
NEG = -1e30
MESH = pl.DeviceIdType.MESH


def _cp(sem=None, **kw):
    return pltpu.CompilerParams(dimension_semantics=sem, vmem_limit_bytes=VMEM_LIMIT, **kw)


def _full_spec(a):
    nd = a.ndim
    return pl.BlockSpec(a.shape, lambda *_: (0,) * nd)


def _tile(n, pref=512):
    return pref if n % pref == 0 else n


def _dot(a, b, ca, cb):
    return lax.dot_general(a.astype(BF16), b.astype(BF16), (((ca,), (cb,)), ((), ())),
                           preferred_element_type=F32)


def _matmul(name, grid, a, a_spec, b, b_spec, dims, outs, *, extras=(), epilogue=None, into=None, summed=0,
            acc_shape=None, rider=None):
    nk = grid[2]
    n_ex, n_out = len(extras), len(outs)
    tm_tn = acc_shape or tuple(d for d in outs[0][1].block_shape if d is not None)[-2:]
    n_in = 2 + n_ex + (into is not None)

    def body(*refs):
        a_ref, b_ref = refs[0], refs[1]
        ex_refs = refs[2:2 + n_ex]
        o_refs = refs[n_in + (rider is not None):n_in + (rider is not None) + n_out]
        if rider is not None:
            start, finish_rider = rider[3](refs[n_in], refs[n_in + 1 + n_out], refs[-2], refs[-1])
            step = [pl.program_id(d) for d in range(3)]
            pl.when((step[0] == 0) & (step[1] == 0) & (step[2] == 0))(start)
        if summed:
            @pl.when((pl.program_id(0) == 0) & (pl.program_id(1) == 0) & (pl.program_id(2) == 0))
            def _():
                for o_ref in o_refs[n_out - summed:]:
                    o_ref[...] = jnp.zeros_like(o_ref)

        def finish(val):
            res = epilogue(val, *[e[...] for e in ex_refs]) if epilogue else (val,)
            for idx, (o_ref, r) in enumerate(zip(o_refs, res)):
                if idx >= n_out - summed:
                    o_ref[...] += r
                else:
                    o_ref[...] = r.astype(o_ref.dtype)

        if nk == 1:
            finish(_dot(a_ref[...], b_ref[...], *dims))
        else:
            acc = refs[n_in + (rider is not None) + n_out + (rider is not None)]
            kk = pl.program_id(2)

            @pl.when(kk == 0)
            def _():
                acc[...] = jnp.zeros_like(acc)

            acc[...] += _dot(a_ref[...], b_ref[...], *dims)

            @pl.when(kk == nk - 1)
            def _():
                finish(acc[...])

        if rider is not None:
            pl.when((step[0] == grid[0] - 1) & (step[1] == grid[1] - 1) & (step[2] == grid[2] - 1))(finish_rider)

    ins = [a, b] + [e[0] for e in extras]
    specs = [a_spec, b_spec] + [e[1] for e in extras]
    aliases = {}
    if into is not None:
        aliases = {len(ins): 0}
        ins.append(into)
        specs.append(pl.BlockSpec(memory_space=pl.ANY))
    out_specs, out_shape = [o[1] for o in outs], [o[0] for o in outs]
    scratch = [pltpu.VMEM(tm_tn, F32)] if nk > 1 else []
    if rider is not None:
        if rider[2]:
            aliases[len(ins)] = n_out
        ins.append(rider[0])
        specs.append(HBM)
        out_specs.append(HBM)
        out_shape.append(rider[1])
        scratch += [pltpu.SemaphoreType.DMA] * 2
        name += "_rider"
    res = pl.pallas_call(
        body, name=name, grid=grid, in_specs=specs, out_specs=out_specs, out_shape=out_shape,
        scratch_shapes=scratch, input_output_aliases=aliases,
        compiler_params=_cp(("arbitrary", "arbitrary", "arbitrary")))(*ins)
    return res[0] if len(res) == 1 else res


@jax.custom_vjp
def _bdot(a, b):
    return _dot(a, b, 1, 0)


def _bdot_fwd(a, b):
    return _dot(a, b, 1, 0), (a, b)


def _bdot_bwd(r, g):
    a, b = r
    return _dot(g, b, 1, 1), _dot(a, g, 0, 0)


_bdot.defvjp(_bdot_fwd, _bdot_bwd)


def _row_spec(tr, w):
    return pl.BlockSpec((tr, w), lambda i: (i, 0))


def _rowwise(fn, rows, pars, outs, *, name, tr, dtype=F32):
    t = rows[0].shape[0]
    n_in = len(rows) + len(pars)

    def body(*refs):
        res = fn(*[r[...] for r in refs[:n_in]])
        for o_ref, v in zip(refs[n_in:], res):
            o_ref[...] = v.astype(o_ref.dtype)

    return pl.pallas_call(
        body, name=name, grid=(t // tr,),
        in_specs=[_row_spec(tr, r.shape[1]) for r in rows] + [_full_spec(p) for p in pars],
        out_specs=[_row_spec(tr, w) for w in outs],
        out_shape=[jax.ShapeDtypeStruct((t, w), dtype) for w in outs],
        compiler_params=_cp(("arbitrary",)))(*rows, *pars)


def _rowwise_vjp(fn, rows, pars, cots, *, name, tr, add=None):
    t = rows[0].shape[0]
    nr, npar = len(rows), len(pars)
    cots = list(cots) + ([add] if add is not None else [])
    nc = len(cots)

    def body(*refs):
        vals = [r[...] for r in refs[:nr + npar]]
        cts = [c[...] for c in refs[nr + npar:nr + npar + nc]]
        douts = refs[nr + npar + nc:]
        extra = cts.pop() if add is not None else None
        _, vjp = jax.vjp(fn, *vals)
        grads = list(vjp(tuple(cts)))
        if extra is not None:
            grads[0] = grads[0] + extra
        for kk in range(nr):
            douts[kk][...] = grads[kk]

        @pl.when(pl.program_id(0) == 0)
        def _():
            for kk in range(npar):
                douts[nr + kk][...] = jnp.zeros_like(douts[nr + kk])

        for kk in range(npar):
            douts[nr + kk][...] += grads[nr + kk]

    return pl.pallas_call(
        body, name=name, grid=(t // tr,),
        in_specs=[_row_spec(tr, r.shape[1]) for r in rows] + [_full_spec(p) for p in pars]
        + [_row_spec(tr, c.shape[1]) for c in cots],
        out_specs=[_row_spec(tr, r.shape[1]) for r in rows] + [_full_spec(p) for p in pars],
        out_shape=[jax.ShapeDtypeStruct(r.shape, F32) for r in rows]
        + [jax.ShapeDtypeStruct(p.shape, F32) for p in pars],
        compiler_params=_cp(("arbitrary",)))(*rows, *pars, *cots)


def _make_rw(fn, name, tr, nr, outs):
    @jax.custom_vjp
    def f(*args):
        return tuple(_rowwise(fn, args[:nr], args[nr:], outs, name=name + "_f", tr=tr))

    def fwd(*args):
        return f(*args), args

    def bwd(args, cts):
        return tuple(_rowwise_vjp(fn, args[:nr], args[nr:], list(cts), name=name + "_b", tr=tr))

    f.defvjp(fwd, bwd)
    return f


def _rms(x, g):
    return x * lax.rsqrt(jnp.mean(jnp.square(x), axis=-1, keepdims=True) + RMS_EPS) * g


def _f_rms(x, g):
    return (_rms(x, g),)


def _f_sgu(au, av, ng, w0, w1, w2, w3, bfull):
    u = jax.nn.gelu(au)
    v = _rms(jax.nn.gelu(av), ng)
    tri = lax.broadcasted_iota(jnp.int32, (SGU_CHUNK, SGU_CHUNK), 0) >= lax.broadcasted_iota(
        jnp.int32, (SGU_CHUNK, SGU_CHUNK), 1)
    head = lax.broadcasted_iota(jnp.int32, v.shape, 1) // HEAD_DIM
    mixed = bfull
    for h, w in enumerate((w0, w1, w2, w3)):
        mixed = mixed + _bdot(jnp.where(tri, w, 0.0), jnp.where(head == h, v, 0.0))
    return (u * mixed,)


def _f_s5disc(lam_re, lam_im, log_dt, b_re, b_im):
    dt = jnp.exp(log_dt)
    mag = jnp.exp(lam_re * dt)
    abar_re = mag * jnp.cos(lam_im * dt)
    abar_im = mag * jnp.sin(lam_im * dt)
    denom = jnp.square(lam_re) + jnp.square(lam_im)
    num_re = abar_re - 1.0
    num_im = abar_im
    fac_re = (num_re * lam_re + num_im * lam_im) / denom
    fac_im = (num_im * lam_re - num_re * lam_im) / denom
    return abar_re, abar_im, fac_re * b_re - fac_im * b_im, fac_re * b_im + fac_im * b_re


def _f_s5post(s_re, s_im, u, c_re, c_im, d, gw, gb):
    y = _bdot(s_re, c_re) - _bdot(s_im, c_im) + d * u
    y = jax.nn.gelu(y)
    return (y * jax.nn.sigmoid(_bdot(y, gw) + gb),)


def _f_lrupre(xc, wa, ba, wx, bx, lam):
    r = jax.nn.sigmoid(_bdot(xc, wa) + ba)
    i = jax.nn.sigmoid(_bdot(xc, wx) + bx)
    log_a = -LRU_C * r * jax.nn.softplus(-lam)
    a = jnp.exp(log_a)
    one_minus_a2 = -jnp.tanh(log_a) * (jnp.exp(2.0 * log_a) + 1.0)
    return a, jnp.sqrt(one_minus_a2) * (i * xc)


def _f_lrupost(h, gate):
    return (h * jax.nn.gelu(gate),)


def _f_logsig(zf, bf):
    return (jax.nn.log_sigmoid(zf + bf),)


def _f_gnorm(ya, yb, yc, yd, g):
    def n(y):
        return y * lax.rsqrt(jnp.mean(jnp.square(y), axis=-1, keepdims=True) + RMS_EPS)
    return (jnp.concatenate([n(ya), n(yb), n(yc), n(yd)], axis=1) * g,)


sgu_mix = _make_rw(_f_sgu, "sgu", SGU_CHUNK, 2, [MIXER_WIDTH])
s5_disc = _make_rw(_f_s5disc, "s5disc", S5_GROUPS * S5_GROUP, 5, [S5_STATE] * 4)
s5_post = _make_rw(_f_s5post, "s5post", 256, 3, [MIXER_WIDTH])
lru_pre = _make_rw(_f_lrupre, "lrupre", 512, 1, [MIXER_WIDTH, MIXER_WIDTH])
lru_post = _make_rw(_f_lrupost, "lrupost", 512, 2, [MIXER_WIDTH])
log_sig = _make_rw(_f_logsig, "logsig", 512, 1, [128])


SCAN_TILE = 512


def _prev_spec(c, nt, rev):
    per = SCAN_TILE // 8
    if rev:
        return pl.BlockSpec((8, c), lambda i: (jnp.maximum((nt - 1 - i) * per - 1, 0), 0))
    return pl.BlockSpec((8, c), lambda i: (jnp.maximum(i * per - 1, 0), 0))


SCAN_STEPS = (1, 2, 4)


def _cmul(ar, ai, br, bi):
    return ar * br - ai * bi, ar * bi + ai * br


def _rows_down(x, k, fill, rowid):
    return jnp.where(rowid >= k, pltpu.roll(x, k, 0), fill)


def _rows_up(x, k, fill, rowid):
    return jnp.where(rowid < 8 - k, pltpu.roll(x, 8 - k, 0), fill)


def _powers(ar, ai):
    pw = [(ar, ai)]
    for _ in range(7):
        pw.append(_cmul(*pw[-1], ar, ai))
    return pw


def _block(i):
    return pl.ds(pl.multiple_of(i * 8, 8), 8)


def _row_before(ref, i, edge):
    return jnp.where(i == 0, edge, ref[pl.ds(jnp.maximum(i * 8 - 1, 0), 1), :])


def _lti_fwd_call(u, w_re, w_im, a_re, a_im):
    t, kdim = u.shape
    c = w_re.shape[1]
    tt = SCAN_TILE

    def body(u_ref, wr_ref, wi_ref, ar_ref, ai_ref, sr_ref, si_ref, br_ref, bi_ref, cr, ci):
        @pl.when(pl.program_id(0) == 0)
        def _():
            cr[...] = jnp.zeros_like(cr)
            ci[...] = jnp.zeros_like(ci)

        br_ref[...] = _dot(u_ref[...], wr_ref[...], 1, 0)
        bi_ref[...] = _dot(u_ref[...], wi_ref[...], 1, 0)
        pw = _powers(ar_ref[...], ai_ref[...])
        apr = jnp.concatenate([p[0] for p in pw], axis=0)
        api = jnp.concatenate([p[1] for p in pw], axis=0)
        rowid = lax.broadcasted_iota(jnp.int32, (8, c), 0)

        def block(i, carry):
            xr, xi = br_ref[_block(i), :], bi_ref[_block(i), :]
            for k in SCAN_STEPS:
                dr, di = _cmul(*pw[k - 1], _rows_down(xr, k, 0.0, rowid), _rows_down(xi, k, 0.0, rowid))
                xr, xi = xr + dr, xi + di
            dr, di = _cmul(apr, api, *carry)
            xr, xi = xr + dr, xi + di
            sr_ref[_block(i), :] = xr
            si_ref[_block(i), :] = xi
            return xr[7:8, :], xi[7:8, :]

        hr, hi = lax.fori_loop(0, tt // 8, block, (cr[...], ci[...]), unroll=2)
        cr[...] = hr
        ci[...] = hi

    row = pl.BlockSpec((tt, c), lambda i: (i, 0))
    par = pl.BlockSpec((1, c), lambda i: (0, 0))
    return pl.pallas_call(
        body, name="lti_scan_f", grid=(t // tt,),
        in_specs=[pl.BlockSpec((tt, kdim), lambda i: (i, 0)), _full_spec(w_re), _full_spec(w_im), par, par],
        out_specs=[row, row], out_shape=[jax.ShapeDtypeStruct((t, c), F32)] * 2,
        scratch_shapes=[pltpu.VMEM((tt, c), F32)] * 2 + [pltpu.VMEM((1, c), F32)] * 2,
        compiler_params=_cp(("arbitrary",)))(u, w_re, w_im, a_re, a_im)


def _lti_bwd_call(u, w_re, w_im, a_re, a_im, s_re, s_im, g_re, g_im):
    t, c = g_re.shape
    kdim = u.shape[1]
    tt = SCAN_TILE
    nt = t // tt
    nb = tt // 8

    def body(u_ref, wr_ref, wi_ref, ar_ref, ai_ref, sr_ref, si_ref, pr_ref, pi_ref, gr_ref, gi_ref,
             du_ref, dwr_ref, dwi_ref, dar_ref, dai_ref, or_ref, oi_ref, cr, ci):
        ti = pl.program_id(0)

        @pl.when(ti == 0)
        def _():
            cr[...] = jnp.zeros_like(cr)
            ci[...] = jnp.zeros_like(ci)
            dar_ref[...] = jnp.zeros_like(dar_ref)
            dai_ref[...] = jnp.zeros_like(dai_ref)
            dwr_ref[...] = jnp.zeros_like(dwr_ref)
            dwi_ref[...] = jnp.zeros_like(dwi_ref)

        pw = _powers(ar_ref[...], -ai_ref[...])
        tpr = jnp.concatenate([p[0] for p in reversed(pw)], axis=0)
        tpi = jnp.concatenate([p[1] for p in reversed(pw)], axis=0)
        rowid = lax.broadcasted_iota(jnp.int32, (8, c), 0)
        first = ti == nt - 1
        edge_r = jnp.where(first, 0.0, pr_ref[7:8, :])
        edge_i = jnp.where(first, 0.0, pi_ref[7:8, :])

        def block(kk, carry):
            i = nb - 1 - kk
            gr_c, gi_c, acc_r, acc_i = carry
            xr, xi = gr_ref[_block(i), :], gi_ref[_block(i), :]
            for k in SCAN_STEPS:
                dr, di = _cmul(*pw[k - 1], _rows_up(xr, k, 0.0, rowid), _rows_up(xi, k, 0.0, rowid))
                xr, xi = xr + dr, xi + di
            dr, di = _cmul(tpr, tpi, gr_c, gi_c)
            xr, xi = xr + dr, xi + di
            or_ref[_block(i), :] = xr
            oi_ref[_block(i), :] = xi
            spr = _rows_down(sr_ref[_block(i), :], 1, _row_before(sr_ref, i, edge_r), rowid)
            spi = _rows_down(si_ref[_block(i), :], 1, _row_before(si_ref, i, edge_i), rowid)
            return xr[0:1, :], xi[0:1, :], acc_r + spr * xr + spi * xi, acc_i + spr * xi - spi * xr

        zero = jnp.zeros((8, c), F32)
        gr_c, gi_c, acc_r, acc_i = lax.fori_loop(0, nb, block, (cr[...], ci[...], zero, zero), unroll=2)
        cr[...] = gr_c
        ci[...] = gi_c
        dar_ref[...] += jnp.sum(acc_r, axis=0, keepdims=True)
        dai_ref[...] += jnp.sum(acc_i, axis=0, keepdims=True)
        du_ref[...] = _dot(or_ref[...], wr_ref[...], 1, 1) + _dot(oi_ref[...], wi_ref[...], 1, 1)
        dwr_ref[...] += _dot(u_ref[...], or_ref[...], 0, 0)
        dwi_ref[...] += _dot(u_ref[...], oi_ref[...], 0, 0)

    row = pl.BlockSpec((tt, c), lambda i: (nt - 1 - i, 0))
    row_u = pl.BlockSpec((tt, kdim), lambda i: (nt - 1 - i, 0))
    par = pl.BlockSpec((1, c), lambda i: (0, 0))
    prev = _prev_spec(c, nt, True)
    return pl.pallas_call(
        body, name="lti_scan_b", grid=(nt,),
        in_specs=[row_u, _full_spec(w_re), _full_spec(w_im), par, par, row, row, prev, prev, row, row],
        out_specs=[row_u, _full_spec(w_re), _full_spec(w_im), par, par],
        out_shape=[jax.ShapeDtypeStruct((t, kdim), F32), jax.ShapeDtypeStruct(w_re.shape, F32),
                   jax.ShapeDtypeStruct(w_im.shape, F32)] + [jax.ShapeDtypeStruct((1, c), F32)] * 2,
        scratch_shapes=[pltpu.VMEM((tt, c), F32)] * 2 + [pltpu.VMEM((1, c), F32)] * 2,
        compiler_params=_cp(("arbitrary",)))(u, w_re, w_im, a_re, a_im, s_re, s_im, s_re, s_im, g_re, g_im)


@jax.custom_vjp
def lti_scan(u, w_re, w_im, a_re, a_im):
    return tuple(_lti_fwd_call(u, w_re, w_im, a_re, a_im))


def _lti_scan_fwd(u, w_re, w_im, a_re, a_im):
    s_re, s_im = _lti_fwd_call(u, w_re, w_im, a_re, a_im)
    return (s_re, s_im), (u, w_re, w_im, a_re, a_im, s_re, s_im)


def _lti_scan_bwd(r, g):
    return tuple(_lti_bwd_call(*r, g[0], g[1]))


lti_scan.defvjp(_lti_scan_fwd, _lti_scan_bwd)


def _tv_fwd_call(a, b):
    t, c = b.shape
    tt = SCAN_TILE

    def body(a_ref, b_ref, h_ref, ch):
        @pl.when(pl.program_id(0) == 0)
        def _():
            ch[...] = jnp.zeros_like(ch)

        rowid = lax.broadcasted_iota(jnp.int32, (8, c), 0)

        def block(i, h):
            ab, x = a_ref[_block(i), :], b_ref[_block(i), :]
            for k in SCAN_STEPS:
                x = x + ab * _rows_down(x, k, 0.0, rowid)
                ab = ab * _rows_down(ab, k, 1.0, rowid)
            x = x + ab * h
            h_ref[_block(i), :] = x
            return x[7:8, :]

        ch[...] = lax.fori_loop(0, tt // 8, block, ch[...], unroll=2)

    row = pl.BlockSpec((tt, c), lambda i: (i, 0))
    return pl.pallas_call(
        body, name="tv_scan_f", grid=(t // tt,), in_specs=[row, row], out_specs=row,
        out_shape=jax.ShapeDtypeStruct((t, c), F32), scratch_shapes=[pltpu.VMEM((1, c), F32)],
        compiler_params=_cp(("arbitrary",)))(a, b)


def _tv_bwd_call(a, h, g):
    t, c = g.shape
    tt = SCAN_TILE
    nt = t // tt
    nb = tt // 8

    def body(a_ref, h_ref, p_ref, g_ref, da_ref, db_ref, cg, ca):
        ti = pl.program_id(0)

        @pl.when(ti == 0)
        def _():
            cg[...] = jnp.zeros_like(cg)
            ca[...] = jnp.zeros_like(ca)

        rowid = lax.broadcasted_iota(jnp.int32, (8, c), 0)
        edge = jnp.where(ti == nt - 1, 0.0, p_ref[7:8, :])

        def block(kk, carry):
            i = nb - 1 - kk
            gc, a_next = carry
            ab, x = a_ref[_block(i), :], g_ref[_block(i), :]
            cb = _rows_up(ab, 1, a_next, rowid)
            for k in SCAN_STEPS:
                x = x + cb * _rows_up(x, k, 0.0, rowid)
                cb = cb * _rows_up(cb, k, 1.0, rowid)
            x = x + cb * gc
            db_ref[_block(i), :] = x
            da_ref[_block(i), :] = x * _rows_down(h_ref[_block(i), :], 1, _row_before(h_ref, i, edge), rowid)
            return x[0:1, :], ab[0:1, :]

        gc, a_next = lax.fori_loop(0, nb, block, (cg[...], ca[...]), unroll=2)
        cg[...] = gc
        ca[...] = a_next

    row = pl.BlockSpec((tt, c), lambda i: (nt - 1 - i, 0))
    return pl.pallas_call(
        body, name="tv_scan_b", grid=(nt,), in_specs=[row, row, _prev_spec(c, nt, True), row],
        out_specs=[row, row], out_shape=[jax.ShapeDtypeStruct((t, c), F32)] * 2,
        scratch_shapes=[pltpu.VMEM((1, c), F32)] * 2, compiler_params=_cp(("arbitrary",)))(a, h, h, g)


@jax.custom_vjp
def tv_scan(a, b):
    return _tv_fwd_call(a, b)


def _tv_scan_fwd(a, b):
    h = _tv_fwd_call(a, b)
    return h, (a, h)


def _tv_scan_bwd(r, g):
    a, h = r
    return tuple(_tv_bwd_call(a, h, g))


tv_scan.defvjp(_tv_scan_fwd, _tv_scan_bwd)


CONV_K = 4
CONV_ROWS = 512


def _conv_fwd_call(x, w, b):
    t, c = x.shape

    def body(x_ref, w_ref, b_ref, o_ref, xp):
        xp[0:8, :] = jnp.zeros((8, c), F32)
        xp[8:, :] = x_ref[...]
        for blk in range(t // CONV_ROWS):
            base = blk * CONV_ROWS
            acc = jnp.broadcast_to(b_ref[...], (CONV_ROWS, c))
            for kk in range(CONV_K):
                acc = acc + w_ref[kk:kk + 1, :] * xp[base + 5 + kk:base + 5 + kk + CONV_ROWS, :]
            o_ref[base:base + CONV_ROWS, :] = acc

    return pl.pallas_call(
        body, name="conv_f", out_shape=jax.ShapeDtypeStruct((t, c), F32),
        scratch_shapes=[pltpu.VMEM((t + 8, c), F32)], compiler_params=_cp())(x, w, b)


def _conv_bwd_call(x, w, g):
    t, c = x.shape

    def body(x_ref, w_ref, g_ref, dx_ref, dw_ref, db_ref, xp, gp):
        xp[0:8, :] = jnp.zeros((8, c), F32)
        xp[8:, :] = x_ref[...]
        gp[0:t, :] = g_ref[...]
        gp[t:, :] = jnp.zeros((8, c), F32)
        dw = [jnp.zeros((1, c), F32) for _ in range(CONV_K)]
        db = jnp.zeros((1, c), F32)
        for blk in range(t // CONV_ROWS):
            base = blk * CONV_ROWS
            gb = g_ref[base:base + CONV_ROWS, :]
            acc = jnp.zeros((CONV_ROWS, c), F32)
            for kk in range(CONV_K):
                acc = acc + w_ref[kk:kk + 1, :] * gp[base + 3 - kk:base + 3 - kk + CONV_ROWS, :]
                dw[kk] = dw[kk] + jnp.sum(gb * xp[base + 5 + kk:base + 5 + kk + CONV_ROWS, :], axis=0, keepdims=True)
            db = db + jnp.sum(gb, axis=0, keepdims=True)
            dx_ref[base:base + CONV_ROWS, :] = acc
        for kk in range(CONV_K):
            dw_ref[kk:kk + 1, :] = dw[kk]
        db_ref[...] = db

    return pl.pallas_call(
        body, name="conv_b",
        out_shape=[jax.ShapeDtypeStruct((t, c), F32), jax.ShapeDtypeStruct((CONV_K, c), F32),
                   jax.ShapeDtypeStruct((1, c), F32)],
        scratch_shapes=[pltpu.VMEM((t + 8, c), F32)] * 2, compiler_params=_cp())(x, w, g)


@jax.custom_vjp
def causal_conv(x, w, b):
    return _conv_fwd_call(x, w, b)


def _causal_conv_fwd(x, w, b):
    return _conv_fwd_call(x, w, b), (x, w)


def _causal_conv_bwd(r, g):
    return tuple(_conv_bwd_call(r[0], r[1], g))


causal_conv.defvjp(_causal_conv_fwd, _causal_conv_bwd)


ATT_TILE = 512
ATT_SCALE = HEAD_DIM ** -0.5


def _head_lane(val, lane, h):
    return jnp.sum(jnp.where(lane == h, val, 0.0), axis=1, keepdims=True)


def _attn_fwd_call(qkv, c128, cr, next_shard=None, late=None):
    t, w = qkv.shape[0], MIXER_WIDTH
    tq = ATT_TILE
    nq = t // tq
    qkv3, cr4 = qkv.reshape(nq, tq, 3 * w), cr.reshape(N_HEADS, nq, 1, tq)
    fused = next_shard is not None
    both = late is not None

    def body(*refs):
        q_ref, k_ref, v_ref, c_ref, cr_ref = refs[:5]
        i, h = pl.program_id(0), pl.program_id(1)
        rest = list(refs[5:])
        shard_ref = rest.pop(0) if fused else None
        own_ref = rest.pop(0) if both else None
        if both:
            rest.pop(0)
        o_ref, lse_ref = rest.pop(0), rest.pop(0)
        gathers = []
        if fused:
            gathers.append(_allgather_copies(shard_ref, rest.pop(0), rest[-2 - 2 * both], rest[-1 - 2 * both], EARLY_ROWS))
        if both:
            gathers.append(_allgather_copies(own_ref, rest.pop(0), rest[-2], rest[-1], LATE_ROWS))
        for start, _ in gathers:
            pl.when((i == 0) & (h == 0))(start)
        hm = lax.broadcasted_iota(jnp.int32, (tq, w), 1) // HEAD_DIM == h
        lane = lax.broadcasted_iota(jnp.int32, (tq, 128), 1)
        qs = jnp.where(hm, q_ref[...] * ATT_SCALE, 0.0)
        cq = _head_lane(c_ref[...], lane, h)
        causal = lax.broadcasted_iota(jnp.int32, (tq, tq), 0) >= lax.broadcasted_iota(jnp.int32, (tq, tq), 1)

        def update(j, carry, diagonal):
            m, l, acc = carry
            s = _dot(qs, k_ref[j], 1, 1) - cr_ref[0, j]
            if diagonal:
                s = jnp.where(causal, s, NEG)
            m_new = jnp.maximum(m, cq + jnp.max(s, axis=1, keepdims=True))
            p = jnp.exp(s + (cq - m_new))
            alpha = jnp.exp(m - m_new)
            return m_new, alpha * l + jnp.sum(p, axis=1, keepdims=True), alpha * acc + _dot(p, v_ref[j], 1, 0)

        init = (jnp.full((tq, 1), NEG, F32), jnp.zeros((tq, 1), F32), jnp.zeros((tq, w), F32))
        carry = lax.fori_loop(0, i, lambda j, c: update(j, c, False), init)
        m, l, acc = update(i, carry, True)
        out = jnp.where(hm, acc / l, 0.0)
        lse = jnp.where(lane == h, m + jnp.log(l), 0.0)

        @pl.when(h == 0)
        def _():
            o_ref[...] = out
            lse_ref[...] = lse

        @pl.when(h > 0)
        def _():
            o_ref[...] += out
            lse_ref[...] += lse

        for _, finish in gathers:
            pl.when((i == nq - 1) & (h == N_HEADS - 1))(finish)

    tile = pl.BlockSpec((tq, w), lambda i, h: (i, 0))
    tile_c = pl.BlockSpec((tq, 128), lambda i, h: (i, 0))
    rows = pl.BlockSpec((1, nq, 1, tq), lambda i, h: (h, 0, 0, 0))
    q_spec = pl.BlockSpec((None, tq, w), lambda i, h: (i, 0, 0))
    k_spec = pl.BlockSpec((nq, tq, w), lambda i, h: (0, 0, 1))
    v_spec = pl.BlockSpec((nq, tq, w), lambda i, h: (0, 0, 2))
    ins = [qkv3, qkv3, qkv3, c128, cr4]
    in_specs, out_specs = [q_spec, k_spec, v_spec, tile_c, rows], [tile, tile_c]
    out_shape = [jax.ShapeDtypeStruct((t, w), F32), jax.ShapeDtypeStruct((t, 128), F32)]
    scratch, aliases, name = [], {}, "attn_f"
    sem_pair = [pltpu.SemaphoreType.DMA((AG_SEMS,)), pltpu.SemaphoreType.DMA((AG_SEMS,))]
    if fused:
        ins.append(next_shard)
        in_specs.append(HBM)
        name += "_next"
    if both:
        aliases = {len(ins) + 1: 2 + fused}
        ins += list(late)
        in_specs += [HBM, HBM]
        name += "_late"
    if fused:
        out_specs.append(HBM)
        out_shape.append(jax.ShapeDtypeStruct((4,) + next_shard.shape, next_shard.dtype))
        scratch += sem_pair
    if both:
        out_specs.append(HBM)
        out_shape.append(jax.ShapeDtypeStruct(late[1].shape, late[1].dtype))
        scratch += sem_pair
    return pl.pallas_call(
        body, name=name, grid=(nq, N_HEADS), in_specs=in_specs, out_specs=out_specs, out_shape=out_shape,
        scratch_shapes=scratch, input_output_aliases=aliases,
        compiler_params=_cp(("arbitrary", "arbitrary")))(*ins)


def _attn_bwd_call(qkv, c128, cr, o, lse, do, exchange=None):
    t, w = qkv.shape[0], MIXER_WIDTH
    tq = ATT_TILE
    nq = t // tq
    r3 = lambda a: a.reshape(nq, tq, a.shape[-1])
    cr4 = cr.reshape(N_HEADS, nq, 1, tq)
    fused = exchange is not None

    def body(*refs):
        q_ref, k_ref, v_ref, c_ref, cr_ref, o_ref, lse_ref, do_ref = refs[:8]
        j, h = pl.program_id(0), pl.program_id(1)
        if fused:
            p_ref, dq_ref, dk_ref, dv_ref, dc_ref, dcr_ref, recv_ref, send_sems, recv_sems = refs[8:]
            start, finish = _chip_exchange_copies(p_ref, recv_ref, send_sems, recv_sems)
            pl.when((j == 0) & (h == 0))(start)
        else:
            dq_ref, dk_ref, dv_ref, dc_ref, dcr_ref = refs[8:]

        @pl.when((j == 0) & (h == 0))
        def _():
            dq_ref[...] = jnp.zeros_like(dq_ref)
            dc_ref[...] = jnp.zeros_like(dc_ref)

        hm = lax.broadcasted_iota(jnp.int32, (tq, w), 1) // HEAD_DIM == h
        lane = lax.broadcasted_iota(jnp.int32, (tq, 128), 1)
        kj = k_ref[...]
        vj = v_ref[...]
        ck = cr_ref[0, 0]
        causal = lax.broadcasted_iota(jnp.int32, (tq, tq), 0) >= lax.broadcasted_iota(jnp.int32, (tq, tq), 1)

        def step(i, carry, diagonal):
            dk, dv, dck = carry
            qm = jnp.where(hm, q_ref[i], 0.0)
            dom = jnp.where(hm, do_ref[i], 0.0)
            s = _dot(qm * ATT_SCALE, kj, 1, 1) - ck
            if diagonal:
                s = jnp.where(causal, s, NEG)
            p = jnp.exp(s + (_head_lane(c_ref[i], lane, h) - _head_lane(lse_ref[i], lane, h)))
            dv = dv + _dot(p, dom, 0, 0)
            dp = _dot(dom, vj, 1, 1)
            delta = jnp.sum(dom * o_ref[i], axis=1, keepdims=True)
            ds = p * (dp - delta)
            dq_ref[i] += jnp.where(hm, _dot(ds, kj, 1, 0), 0.0) * ATT_SCALE
            dk = dk + _dot(ds, qm, 0, 0) * ATT_SCALE
            dc_ref[i] += jnp.where(lane == h, jnp.sum(ds, axis=1, keepdims=True), 0.0)
            return dk, dv, dck - jnp.sum(ds, axis=0, keepdims=True)

        init = (jnp.zeros((tq, w), F32), jnp.zeros((tq, w), F32), jnp.zeros((1, tq), F32))
        carry = step(j, init, True)
        dk, dv, dck = lax.fori_loop(j + 1, nq, lambda i, c: step(i, c, False), carry)
        dcr_ref[0, 0] = dck

        @pl.when(h == 0)
        def _():
            dk_ref[...] = dk
            dv_ref[...] = dv

        @pl.when(h > 0)
        def _():
            dk_ref[...] += dk
            dv_ref[...] += dv

        if fused:
            pl.when((j == nq - 1) & (h == N_HEADS - 1))(finish)

    whole = pl.BlockSpec((nq, tq, w), lambda j, h: (0, 0, 0))
    whole_c = pl.BlockSpec((nq, tq, 128), lambda j, h: (0, 0, 0))
    tile = pl.BlockSpec((None, tq, w), lambda j, h: (j, 0, 0))
    tile_r = pl.BlockSpec((1, 1, 1, tq), lambda j, h: (h, j, 0, 0))
    s3 = jax.ShapeDtypeStruct((nq, tq, w), F32)
    qkv3 = r3(qkv)
    k_tile = pl.BlockSpec((None, tq, w), lambda j, h: (j, 0, 1))
    v_tile = pl.BlockSpec((None, tq, w), lambda j, h: (j, 0, 2))
    ins = [qkv3, qkv3, qkv3, r3(c128), cr4, r3(o), r3(lse), r3(do)]
    in_specs = [whole, k_tile, v_tile, whole_c, tile_r, whole, whole_c, whole]
    out_specs = [whole, tile, tile, whole_c, tile_r]
    out_shape = [s3, s3, s3, jax.ShapeDtypeStruct((nq, tq, 128), F32), jax.ShapeDtypeStruct((N_HEADS, nq, 1, tq), F32)]
    scratch = []
    if fused:
        ins.append(exchange)
        in_specs.append(HBM)
        out_specs.append(HBM)
        out_shape.append(jax.ShapeDtypeStruct((3,) + exchange.shape[1:], exchange.dtype))
        scratch = [pltpu.SemaphoreType.DMA((3,)), pltpu.SemaphoreType.DMA((3,))]
    dq, dk, dv, dc, dcr, *received = pl.pallas_call(
        body, name="attn_b_exchange" if fused else "attn_b", grid=(nq, N_HEADS), in_specs=in_specs,
        out_specs=out_specs, out_shape=out_shape, scratch_shapes=scratch,
        compiler_params=_cp(("arbitrary", "arbitrary")))(*ins)
    grads = (dq.reshape(t, w), dk.reshape(t, w), dv.reshape(t, w), dc.reshape(t, 128), dcr.reshape(N_HEADS, 1, t))
    return grads, (received[0] if fused else None)


def _loss_call(x, g, target):
    t, d = x.shape
    tr = 512

    def body(x_ref, g_ref, t_ref, loss_ref, dx_ref, dg_ref):
        tgt = t_ref[...]

        def f(xv, gv):
            return 0.5 * jnp.sum(jnp.mean(jnp.square(_rms(xv, gv) - tgt), axis=-1))

        val, vjp = jax.vjp(f, x_ref[...], g_ref[...])
        dx, dg = vjp(jnp.ones((), F32))
        dx_ref[...] = dx

        @pl.when(pl.program_id(0) == 0)
        def _():
            loss_ref[...] = jnp.zeros_like(loss_ref)
            dg_ref[...] = jnp.zeros_like(dg_ref)

        loss_ref[...] += jnp.full(loss_ref.shape, val, F32)
        dg_ref[...] += dg

    row = _row_spec(tr, d)
    return pl.pallas_call(
        body, name="loss_head", grid=(t // tr,), in_specs=[row, _full_spec(g), row],
        out_specs=[pl.BlockSpec((1, 128), lambda i: (0, 0)), row, _full_spec(g)],
        out_shape=[jax.ShapeDtypeStruct((1, 128), F32), jax.ShapeDtypeStruct((t, d), F32),
                   jax.ShapeDtypeStruct(g.shape, F32)],
        compiler_params=_cp(("arbitrary",)))(x, g, target)


def _blockdiag(w):
    l, g, a, b = w.shape
    return jnp.einsum('lgab,gk->lgakb', w, jnp.eye(g, dtype=w.dtype)).reshape(l, g * a, g * b)


def _prepare(rep):
    d = DEPTH
    w = MIXER_WIDTH
    rows = S5_GROUPS * S5_GROUP
    rep16 = lambda a: jnp.repeat(a, S5_GROUP, axis=1).reshape(d * rows, -1)
    bt = lambda b: b.transpose(0, 1, 3, 2).reshape(d * rows, S5_STATE)
    abar_re, abar_im, bb_re, bb_im = s5_disc(
        rep16(rep["s5_lambda_re"]), rep16(rep["s5_lambda_im"]), rep16(rep["s5_log_dt"][:, :, None]),
        bt(rep["s5_b_re"]), bt(rep["s5_b_im"]))
    g4 = lambda a: a.reshape(d, S5_GROUPS, S5_GROUP, S5_STATE)
    first = lambda a: g4(a)[:, :, 0, :].reshape(d, 1, S5_GROUPS * S5_STATE)
    cblk = lambda c: _blockdiag(c.transpose(0, 1, 3, 2))
    row = lambda a: a.reshape(d, 1, -1)
    return dict(
        sgu_norm_g=row(rep["sgu_norm_g"]), sgu_w=rep["sgu_w"],
        sgu_bias=jnp.repeat(rep["sgu_b"].transpose(0, 2, 1), HEAD_DIM, axis=2),
        abar_re=first(abar_re), abar_im=first(abar_im), bblk_re=_blockdiag(g4(bb_re)), bblk_im=_blockdiag(g4(bb_im)),
        cblk_re=cblk(rep["s5_c_re"]), cblk_im=cblk(rep["s5_c_im"]), s5_d=row(rep["s5_d"]), s5_glu_b=row(rep["s5_glu_b"]),
        lru_conv_b=row(rep["lru_conv_b"]), lru_wa=_blockdiag(rep["lru_wa"]), lru_ba=row(rep["lru_ba"]),
        lru_wx=_blockdiag(rep["lru_wx"]), lru_bx=row(rep["lru_bx"]), lru_lambda=row(rep["lru_lambda"]),
        fgate_b=jnp.pad(rep["fox_fgate_b"], ((0, 0), (0, 128 - N_HEADS)))[:, None, :])


PREPARED_FROM = ('sgu_norm_g', 'sgu_w', 'sgu_b', 's5_lambda_re', 's5_lambda_im', 's5_log_dt', 's5_b_re', 's5_b_im',
                 's5_c_re', 's5_c_im', 's5_d', 's5_glu_b', 'lru_conv_b', 'lru_wa', 'lru_ba', 'lru_wx', 'lru_bx',
                 'lru_lambda', 'fox_fgate_b')


def _mixers_pre(pieces, p, glu_w, conv_w):
    a_u, a_v, b_in, c_x, c_gate, d_q, d_k, d_v, d_f = pieces
    sw = p["sgu_w"]
    (y_a,) = sgu_mix(a_u, a_v, p["sgu_norm_g"], sw[0], sw[1], sw[2], sw[3], p["sgu_bias"])
    s_re, s_im = lti_scan(b_in, p["bblk_re"], p["bblk_im"], p["abar_re"], p["abar_im"])
    (y_b,) = s5_post(s_re, s_im, b_in, p["cblk_re"], p["cblk_im"], p["s5_d"], glu_w, p["s5_glu_b"])
    xc = causal_conv(c_x, conv_w, p["lru_conv_b"])
    a, b = lru_pre(xc, p["lru_wa"], p["lru_ba"], p["lru_wx"], p["lru_bx"], p["lru_lambda"])
    (y_c,) = lru_post(tv_scan(a, b), c_gate)
    (log_f,) = log_sig(d_f, p["fgate_b"])
    c128 = tv_scan(jnp.ones_like(log_f), log_f)
    return y_a, y_b, y_c, d_q, d_k, d_v, c128, c128[:, :N_HEADS].T[:, None, :]


PACK_COLS = 1024
SHARD_SHAPE = {'w_mlp_in': (1024, 1024), 'w_mlp_out': (1024, 1024), 'w_out': (256, 1024), 'w_in': (1024, 513),
               's5_glu_w': (64, 256), 'lru_conv_w': (4, 64)}
SHARDED_AXIS = {'w_in': 1, 's5_glu_w': 0, 'lru_conv_w': 1, 'w_out': 0, 'w_mlp_in': 1, 'w_mlp_out': 0}
SHARD_ROWS = {n: -(-s[0] * s[1] // PACK_COLS) for n, s in SHARD_SHAPE.items()}
SHARD_OFF = {n: sum(list(SHARD_ROWS.values())[:i]) for i, n in enumerate(SHARD_SHAPE)}
LAYER_ROWS = 2880
SMALL_OFF = SHARD_OFF['w_in']
GLUE_ROWS = LAYER_ROWS - SMALL_OFF
assert SHARD_OFF['w_mlp_out'] == 1024 and SHARD_OFF['w_out'] == 2048 and SMALL_OFF % GLUE_ROWS == 0
assert SHARD_OFF['lru_conv_w'] + SHARD_ROWS['lru_conv_w'] <= LAYER_ROWS
PACK_ROWS = DEPTH * LAYER_ROWS
TOK = 1024
FF = 4 * D_MODEL


def _w3(i_of):
    return pl.BlockSpec((None, 1024, PACK_COLS), i_of)


def _tile2(rows, cols, i_of):
    return pl.BlockSpec((rows, cols), i_of)


def _layer_fwd(x, h1, gathered, w_in, p, mix_p, next_shard, own_shard, next_gain):
    t = x.shape[0]
    nt = t // TOK
    f32 = lambda r, c: jax.ShapeDtypeStruct((r, c), F32)
    b16 = lambda r, c: jax.ShapeDtypeStruct((r, c), BF16)
    g2, gm = p["norm2_g"][None, :], p["mix_norm_g"][None, :]
    qkv_cols = (5 * MIXER_WIDTH, 8 * MIXER_WIDTH)
    z, qkv = _matmul("mm_in", (nt, 1, 1), h1, _tile2(TOK, D_MODEL, lambda i, j, k: (i, 0)),
                     w_in, _tile2(D_MODEL, D_IN_PAD, lambda i, j, k: (0, 0)), (1, 0),
                     [(f32(t, D_IN_PAD), _tile2(TOK, D_IN_PAD, lambda i, j, k: (i, 0))),
                      (b16(t, 3 * MIXER_WIDTH), _tile2(TOK, 3 * MIXER_WIDTH, lambda i, j, k: (i, 0)))],
                     epilogue=lambda acc: (acc, acc[:, qkv_cols[0]:qkv_cols[1]]))
    pieces = tuple(jnp.split(z, [MIXER_WIDTH * i for i in range(1, 9)], axis=1))
    (y_a, y_b, y_c, _, _, _, c128, cr), mix_vjp = jax.vjp(_mixers_pre, pieces, *mix_p)
    attn_in = (qkv, c128, cr)
    y_d, lse, *more = _attn_fwd_call(*attn_in, next_shard=next_shard,
                                     late=None if own_shard is None else (own_shard, gathered))
    if own_shard is not None:
        gathered = more.pop()
    w_out = _join_chips(_unpack_shards(gathered[:, None], ('w_out',))['w_out'], SHARDED_AXIS['w_out'])[0]
    ys = (y_a, y_b, y_c, y_d)
    (yn,) = _rowwise(_f_gnorm, list(ys), [gm], [D_MODEL], name="gnorm_f", tr=512, dtype=BF16)
    x_tile = _tile2(TOK, D_MODEL, lambda i, j, k: (i, 0))
    gain = _tile2(1, D_MODEL, lambda i, j, k: (0, 0))

    def add_and_norm(acc, r, gv):
        s = acc + r
        return s, _rms(s, gv)

    x1, h2 = _matmul("mm_out", (nt, 1, 1), yn, x_tile, w_out, _tile2(D_MODEL, D_MODEL, lambda i, j, k: (0, 0)), (1, 0),
                     [(f32(t, D_MODEL), x_tile), (b16(t, D_MODEL), x_tile)], extras=[(x, x_tile), (g2, gain)],
                     epilogue=add_and_norm)
    ff_tile = _tile2(TOK, 1024, lambda i, j, k: (i, j))
    act = _matmul("mm_up", (nt, FF // 1024, 1), h2, x_tile, gathered, _w3(lambda i, j, k: (j, 0, 0)), (1, 0),
                  [(b16(t, FF), ff_tile)], epilogue=lambda acc: (jnp.square(jnp.maximum(acc, 0.0)),))
    down = ("mm_down", (nt, 1, FF // 1024), act, _tile2(TOK, 1024, lambda i, j, k: (i, k)),
            gathered, _w3(lambda i, j, k: (k, 1, 0)), (1, 0))
    if next_gain is None:
        x2, h_next = _matmul(*down, [(f32(t, D_MODEL), x_tile)], extras=[(x1, x_tile)],
                             epilogue=lambda acc, r: (acc + r,)), None
    else:
        x2, h_next = _matmul(*down, [(f32(t, D_MODEL), x_tile), (b16(t, D_MODEL), x_tile)],
                             extras=[(x1, x_tile), (next_gain[None, :], gain)], epilogue=add_and_norm)
    res = (x, h1, mix_vjp, ys, attn_in, lse, yn, x1, h2, act, gathered, w_out)
    return x2, h_next, res, (more[0] if more else None)


def _layer_bwd(g, res, w_in, p, later_send):
    x, h1, mix_vjp, ys, attn_in, lse, yn, x1, h2, act, gathered, w_out = res
    half_rows = LAYER_ROWS // 2
    riding = later_send is not None
    halves = later_send.reshape(4, 2, half_rows, PACK_COLS) if riding else None
    send = lax.empty((4, LAYER_ROWS, PACK_COLS), BF16)
    t = x.shape[0]
    nt = t // TOK
    f32 = lambda r, c: jax.ShapeDtypeStruct((r, c), F32)
    g1, g2, gm = p["norm1_g"][None, :], p["norm2_g"][None, :], p["mix_norm_g"][None, :]
    x_tile = _tile2(TOK, D_MODEL, lambda i, j, k: (i, 0))
    ff_tile = _tile2(TOK, 1024, lambda i, j, k: (i, j))
    tok_k = _tile2(TOK, D_MODEL, lambda i, j, k: (k, 0))
    send_s = jax.ShapeDtypeStruct(send.shape, send.dtype)
    pair_rider = (halves, jax.ShapeDtypeStruct((4, half_rows, PACK_COLS), BF16), False, _pair_exchange_copies)
    du = _matmul("mm_down_dx", (nt, FF // 1024, 1), g, x_tile, gathered, _w3(lambda i, j, k: (j, 1, 0)), (1, 1),
                 [(jax.ShapeDtypeStruct((t, FF), BF16), ff_tile)], extras=[(act, ff_tile)],
                 epilogue=lambda acc, a: (2.0 * jnp.sqrt(a.astype(F32)) * acc,),
                 rider=pair_rider if riding else None)
    exchange = None
    if riding:
        du, from_sibling = du
        exchange = _add_kept(halves, from_sibling, REDUCE_ROWS)
    send = _matmul("mm_down_dw", (FF // 1024, 1, nt), act, _tile2(TOK, 1024, lambda i, j, k: (k, i)), g, tok_k, (0, 0),
                   [(send_s, _w3(lambda i, j, k: (i, 1, 0)))], into=send)
    send = _matmul("mm_up_dw", (1, FF // 1024, nt), h2, tok_k, du, _tile2(TOK, 1024, lambda i, j, k: (k, j)), (0, 0),
                   [(send_s, _w3(lambda i, j, k: (j, 0, 0)))], into=send)
    gain = _tile2(1, D_MODEL, lambda i, j, k: (0, 0))

    def norm_bwd(dh, xv, gv, through):
        _, vjp = jax.vjp(_rms, xv, gv)
        dxv, dgv = vjp(dh)
        return dxv + through, dgv

    g_mid, dg2 = _matmul("mm_up_dx", (nt, 1, FF // 1024), du, _tile2(TOK, 1024, lambda i, j, k: (i, k)),
                         gathered, _w3(lambda i, j, k: (k, 0, 0)), (1, 1),
                         [(f32(t, D_MODEL), x_tile), (f32(1, D_MODEL), gain)],
                         extras=[(x1, x_tile), (g2, gain), (g, x_tile)], epilogue=norm_bwd, summed=1)
    w_full = _tile2(D_MODEL, D_MODEL, lambda i, j, k: (0, 0))
    dyn = _matmul("mm_out_dx", (nt, 1, 1), g_mid, x_tile, w_out, w_full, (1, 1), [(f32(t, D_MODEL), x_tile)])
    quarter = D_MODEL // 4
    send = _matmul("mm_out_dw", (1, 1, nt), yn, tok_k, g_mid, tok_k, (0, 0),
                   [(send_s, pl.BlockSpec((4, quarter, PACK_COLS), lambda i, j, k: (0, SHARD_OFF['w_out'] // quarter, 0)))],
                   epilogue=lambda acc: (acc.reshape(4, quarter, PACK_COLS),), into=send, acc_shape=(D_MODEL, D_MODEL))
    dy_a, dy_b, dy_c, dy_d, dgm = _rowwise_vjp(_f_gnorm, list(ys), [gm], [dyn], name="gnorm_b", tr=512)
    d_attn_in, received = _attn_bwd_call(*attn_in, ys[3], lse, dy_d, exchange=exchange)
    d_pieces, *d_mix = mix_vjp((dy_a, dy_b, dy_c, *d_attn_in))
    dz = jnp.concatenate([d.astype(BF16) for d in d_pieces], axis=1)
    z_tile = _tile2(TOK, D_IN_PAD, lambda i, j, k: (i, 0))
    share_rider = None
    if riding:
        pair = _sum_chips(exchange, received, REDUCE_ROWS)
        share_rider = (pair, jax.ShapeDtypeStruct(pair.shape, pair.dtype), True, _pair_share_copies)
    d_w_in = _matmul("mm_in_dw", (1, 1, t // 512), h1, _tile2(512, D_MODEL, lambda i, j, k: (k, 0)),
                     dz, _tile2(512, D_IN_PAD, lambda i, j, k: (k, 0)), (0, 0),
                     [(f32(D_MODEL, D_IN_PAD), _tile2(D_MODEL, D_IN_PAD, lambda i, j, k: (0, 0)))], rider=share_rider)
    reduced = None
    if riding:
        d_w_in, pair = d_w_in
        reduced = pair.reshape(LAYER_ROWS, PACK_COLS)
    dx, dg1 = _matmul("mm_in_dx", (nt, 1, 1), dz, z_tile, w_in, _tile2(D_MODEL, D_IN_PAD, lambda i, j, k: (0, 0)), (1, 1),
                      [(f32(t, D_MODEL), x_tile), (f32(1, D_MODEL), gain)],
                      extras=[(x, x_tile), (g1, gain), (g_mid, x_tile)], epilogue=norm_bwd, summed=1)
    norms = dict(norm1_g=dg1[0], norm2_g=dg2[0], mix_norm_g=dgm[0])
    return dx, norms, d_mix, d_w_in, send, reduced


HBM = pl.BlockSpec(memory_space=pltpu.HBM)
D2D_CHUNKS = 15
ICI_CHUNKS = 5
VMEM_CHUNKS = 4


def _coords():
    return lax.axis_index("x"), lax.axis_index("y"), lax.axis_index("c")


def _other_chips(x, y):
    return [(1 - x, y), (x, 1 - y), (1 - x, 1 - y)]


def _start_chunks(make, rows, n):
    size = rows // n
    assert size * n == rows
    for k in range(n):
        make(pl.ds(k * size, size)).start()


AG_SEMS = 7


ALL_ROWS = (0, LAYER_ROWS, ICI_CHUNKS, D2D_CHUNKS)
EARLY_ROWS = (SMALL_OFF, GLUE_ROWS, 4, 4)
LATE_ROWS = (0, SMALL_OFF, 4, 12)


def _allgather_copies(in_ref, out_ref, send_sems, recv_sems, part=ALL_ROWS):
    r0, r, ici_chunks, d2d_chunks = part
    rh = r // 2
    x, y, c = _coords()
    me, sibling = (x, y, c), (x, y, 1 - c)
    chips = _other_chips(x, y)

    def half(px, py, pc, rows=pl.ds(0, rh)):
        return out_ref.at[2 * px + py, pl.ds(r0 + pc * rh + rows.start, rows.size), :]

    def copy(k, block, to, rows=pl.ds(0, rh), from_input=False):
        src = in_ref.at[pl.ds(r0 + block[2] * rh + rows.start, rows.size), :] if from_input else half(*block, rows)
        return pltpu.make_async_remote_copy(
            src_ref=src, dst_ref=half(*block, rows), send_sem=send_sems.at[k], recv_sem=recv_sems.at[k],
            device_id=to, device_id_type=MESH)

    def own(rows=pl.ds(0, r)):
        mine = pl.ds(r0 + rows.start, rows.size)
        return pltpu.make_async_remote_copy(
            src_ref=in_ref.at[mine, :], dst_ref=out_ref.at[2 * x + y, mine, :], send_sem=send_sems.at[6],
            recv_sem=recv_sems.at[6], device_id=sibling, device_id_type=MESH)

    def start():
        for j, chip in enumerate(chips):
            _start_chunks(lambda rows: copy(j, me, (*chip, c), rows, from_input=True), rh, ici_chunks)
        _start_chunks(own, r, d2d_chunks)

    def finish():
        for j, chip in enumerate(chips):
            copy(j, (*chip, c), me).wait_recv()
            _start_chunks(lambda rows: copy(3 + j, (*chip, c), sibling, rows), rh, d2d_chunks)
        for j, chip in enumerate(chips):
            copy(3 + j, (*chip, 1 - c), me).wait_recv()
        for j, chip in enumerate(chips):
            copy(j, me, (*chip, c), from_input=True).wait_send()
            copy(3 + j, (*chip, c), sibling).wait_send()
        own().wait()

    return start, finish


def _allgather_shards(shard, part):
    def body(in_ref, out_ref, send_sems, recv_sems):
        start, finish = _allgather_copies(in_ref, out_ref, send_sems, recv_sems, part)
        start()
        finish()

    return pl.pallas_call(
        body, name="allgather_shards", out_shape=jax.ShapeDtypeStruct((4,) + shard.shape, shard.dtype),
        in_specs=[HBM], out_specs=HBM,
        scratch_shapes=[pltpu.SemaphoreType.DMA((AG_SEMS,)), pltpu.SemaphoreType.DMA((AG_SEMS,))],
        compiler_params=pltpu.CompilerParams())(shard)


def _pair_exchange_copies(g_ref, recv_ref, send_sem, recv_sem):
    s, _, rh, _ = g_ref.shape
    x, y, c = _coords()

    def copy(slot, rows):
        return pltpu.make_async_remote_copy(
            src_ref=g_ref.at[slot, 1 - c, rows, :], dst_ref=recv_ref.at[slot, rows, :], send_sem=send_sem,
            recv_sem=recv_sem, device_id=(x, y, 1 - c), device_id_type=MESH)

    def start():
        for slot in range(s):
            _start_chunks(lambda rows: copy(slot, rows), rh, VMEM_CHUNKS)

    def finish():
        pltpu.make_async_remote_copy(
            src_ref=g_ref.at[:, 1 - c], dst_ref=recv_ref, send_sem=send_sem, recv_sem=recv_sem,
            device_id=(x, y, 1 - c), device_id_type=MESH).wait()

    return start, finish


def _pair_share_copies(in_ref, out_ref, send_sem, recv_sem):
    rh = in_ref.shape[1]
    x, y, c = _coords()

    def copy(slot, rows=pl.ds(0, rh)):
        return pltpu.make_async_remote_copy(
            src_ref=in_ref.at[slot, rows, :], dst_ref=out_ref.at[slot, rows, :], send_sem=send_sem,
            recv_sem=recv_sem, device_id=(x, y, 1 - c), device_id_type=MESH)

    def start():
        _start_chunks(lambda rows: copy(c, rows), rh, D2D_CHUNKS)

    def finish():
        copy(c).wait_send()
        copy(1 - c).wait_recv()

    return start, finish


def _pair_exchange(g):
    s, _, rh, cols = g.shape

    def body(g_ref, recv_ref, send_sem, recv_sem):
        start, finish = _pair_exchange_copies(g_ref, recv_ref, send_sem, recv_sem)
        start()
        finish()

    return pl.pallas_call(
        body, name="pair_exchange", out_shape=jax.ShapeDtypeStruct((s, rh, cols), g.dtype), in_specs=[HBM],
        out_specs=HBM, scratch_shapes=[pltpu.SemaphoreType.DMA] * 2, compiler_params=pltpu.CompilerParams())(g)


def _chip_exchange_copies(p_ref, recv_ref, send_sems, recv_sems):
    rh = p_ref.shape[1]
    x, y, c = _coords()
    chips = _other_chips(x, y)

    def copy(j, chip, rows=pl.ds(0, rh)):
        return pltpu.make_async_remote_copy(
            src_ref=p_ref.at[2 * chip[0] + chip[1], rows, :], dst_ref=recv_ref.at[j, rows, :],
            send_sem=send_sems.at[j], recv_sem=recv_sems.at[j], device_id=(*chip, c), device_id_type=MESH)

    def start():
        for j, chip in enumerate(chips):
            _start_chunks(lambda rows: copy(j, chip, rows), rh, ICI_CHUNKS)

    def finish():
        for j, chip in enumerate(chips):
            copy(j, chip).wait_recv()
        for j, chip in enumerate(chips):
            copy(j, chip).wait_send()

    return start, finish


def _chip_exchange(p):
    def body(p_ref, recv_ref, send_sems, recv_sems):
        start, finish = _chip_exchange_copies(p_ref, recv_ref, send_sems, recv_sems)
        start()
        finish()

    return pl.pallas_call(
        body, name="chip_exchange", out_shape=jax.ShapeDtypeStruct((3,) + p.shape[1:], p.dtype), in_specs=[HBM],
        out_specs=HBM, scratch_shapes=[pltpu.SemaphoreType.DMA((3,)), pltpu.SemaphoreType.DMA((3,))],
        compiler_params=pltpu.CompilerParams())(p)


def _sum_chips(p, recv, tr):
    _, rh, cols = p.shape
    x, y, c = _coords()
    one = lambda v: v.astype(jnp.int32).reshape(1)

    def body(chip_ref, c_ref, own_ref, r_ref, o_ref):
        acc = own_ref[...].astype(F32)
        for k in range(3):
            acc = acc + r_ref[k].astype(F32)
        o_ref[...] = acc

    return pl.pallas_call(
        body, name="sum_chips", out_shape=jax.ShapeDtypeStruct((2, rh, cols), F32),
        grid_spec=pltpu.PrefetchScalarGridSpec(
            num_scalar_prefetch=2, grid=(rh // tr,),
            in_specs=[pl.BlockSpec((None, tr, cols), lambda i, chip_ref, c_ref: (chip_ref[0], i, 0)),
                      pl.BlockSpec((3, tr, cols), lambda i, chip_ref, c_ref: (0, i, 0))],
            out_specs=pl.BlockSpec((None, tr, cols), lambda i, chip_ref, c_ref: (c_ref[0], i, 0))),
        compiler_params=_cp(("arbitrary",)))(one(2 * x + y), one(c), p, recv)


def _pair_share(buf):
    _, rh, cols = buf.shape

    def body(in_ref, out_ref, send_sem, recv_sem):
        start, finish = _pair_share_copies(in_ref, out_ref, send_sem, recv_sem)
        start()
        finish()

    return pl.pallas_call(
        body, name="pair_share", out_shape=jax.ShapeDtypeStruct(buf.shape, buf.dtype), in_specs=[HBM], out_specs=HBM,
        scratch_shapes=[pltpu.SemaphoreType.DMA] * 2, input_output_aliases={0: 0},
        compiler_params=pltpu.CompilerParams())(buf)


def _allgather_all(blk):
    m_per, cols = blk.shape
    whole = pl.ds(0, m_per)

    def body(x_ref, out_ref, send_sems, recv_sems, local_sem):
        x, y, c = _coords()
        me, sibling = (x, y, c), (x, y, 1 - c)
        chips = _other_chips(x, y)

        def rows_of(px, py, pc, rows):
            return out_ref.at[4 * px + 2 * py + pc, rows, :]

        def copy(k, block, to, rows=whole, from_input=False):
            return pltpu.make_async_remote_copy(
                src_ref=x_ref.at[rows, :] if from_input else rows_of(*block, rows), dst_ref=rows_of(*block, rows),
                send_sem=send_sems.at[k], recv_sem=recv_sems.at[k], device_id=to, device_id_type=MESH)

        mine = pltpu.make_async_copy(x_ref, rows_of(*me, whole), local_sem)
        mine.start()
        _start_chunks(lambda rows: copy(0, me, sibling, rows, from_input=True), m_per, VMEM_CHUNKS)
        for j, chip in enumerate(chips):
            _start_chunks(lambda rows: copy(1 + j, me, (*chip, c), rows, from_input=True), m_per, VMEM_CHUNKS)
        for j, chip in enumerate(chips):
            copy(1 + j, (*chip, c), me).wait_recv()
            _start_chunks(lambda rows: copy(4 + j, (*chip, c), sibling, rows), m_per, VMEM_CHUNKS)
        copy(0, sibling, me).wait_recv()
        for j, chip in enumerate(chips):
            copy(4 + j, (*chip, 1 - c), me).wait_recv()
        copy(0, me, sibling, from_input=True).wait_send()
        for j, chip in enumerate(chips):
            copy(1 + j, me, (*chip, c), from_input=True).wait_send()
            copy(4 + j, (*chip, c), sibling).wait_send()
        mine.wait()

    return pl.pallas_call(
        body, name="allgather_all", out_shape=jax.ShapeDtypeStruct((8, m_per, cols), blk.dtype),
        in_specs=[pl.BlockSpec(memory_space=pltpu.VMEM)], out_specs=pl.BlockSpec(memory_space=pltpu.VMEM),
        scratch_shapes=[pltpu.SemaphoreType.DMA((7,)), pltpu.SemaphoreType.DMA((7,)), pltpu.SemaphoreType.DMA],
        compiler_params=pltpu.CompilerParams(vmem_limit_bytes=VMEM_LIMIT))(blk)


def _add_kept(g, recv, tr):
    s, _, rh, cols = g.shape

    def body(c_ref, a_ref, b_ref, o_ref):
        o_ref[...] = (a_ref[...].astype(F32) + b_ref[...].astype(F32)).astype(o_ref.dtype)

    spec = pl.BlockSpec((None, tr, cols), lambda si, i, c_ref: (si, i, 0))
    return pl.pallas_call(
        body, name="add_kept", out_shape=jax.ShapeDtypeStruct((s, rh, cols), BF16),
        grid_spec=pltpu.PrefetchScalarGridSpec(
            num_scalar_prefetch=1, grid=(s, rh // tr),
            in_specs=[pl.BlockSpec((None, None, tr, cols), lambda si, i, c_ref: (si, c_ref[0], i, 0)), spec],
            out_specs=spec),
        compiler_params=_cp(("arbitrary", "arbitrary")))(lax.axis_index("c").astype(jnp.int32).reshape(1), g, recv)


def _sum_slots(p, tr, name):
    s, rows, cols = p.shape

    def body(p_ref, o_ref):
        acc = p_ref[0].astype(F32)
        for k in range(1, s):
            acc = acc + p_ref[k].astype(F32)
        o_ref[...] = acc

    return pl.pallas_call(
        body, name=name, grid=(rows // tr,), in_specs=[pl.BlockSpec((s, tr, cols), lambda i: (0, i, 0))],
        out_specs=_row_spec(tr, cols), out_shape=jax.ShapeDtypeStruct((rows, cols), F32),
        compiler_params=_cp(("arbitrary",)))(p)


def _adamw_call(w, g, m, v, name):
    rows, cols = w.shape
    tr = _tile(rows, 512) if rows % 512 == 0 else _tile(rows, 128)
    c1 = 1.0 - ADAM_B1 ** ADAM_STEP
    c2 = 1.0 - ADAM_B2 ** ADAM_STEP

    def body(w_ref, g_ref, m_ref, v_ref, d_ref, nm_ref, nv_ref):
        gv = g_ref[...]
        nm = ADAM_B1 * m_ref[...] + (1.0 - ADAM_B1) * gv
        nv = ADAM_B2 * v_ref[...] + (1.0 - ADAM_B2) * jnp.square(gv)
        d_ref[...] = -ADAM_LR * ((nm / c1) / (jnp.sqrt(nv / c2) + ADAM_EPS) + ADAM_WD * w_ref[...])
        nm_ref[...] = nm
        nv_ref[...] = nv

    spec = _row_spec(tr, cols)
    o = jax.ShapeDtypeStruct((rows, cols), F32)
    return pl.pallas_call(body, name=name, grid=(rows // tr,), in_specs=[spec] * 4, out_specs=[spec] * 3,
                          out_shape=[o, o, o], compiler_params=_cp(("arbitrary",)))(w, g, m, v)


WEIGHTS = ('norm1_g', 'w_in', 'sgu_norm_g', 'sgu_w', 'sgu_b', 's5_lambda_re', 's5_lambda_im', 's5_log_dt',
           's5_b_re', 's5_b_im', 's5_c_re', 's5_c_im', 's5_d', 's5_glu_w', 's5_glu_b', 'lru_conv_w',
           'lru_conv_b', 'lru_wa', 'lru_ba', 'lru_wx', 'lru_bx', 'lru_lambda', 'fox_fgate_b', 'mix_norm_g',
           'w_out', 'norm2_g', 'w_mlp_in', 'w_mlp_out', 'final_g')
N_W = len(WEIGHTS)


def _pack_shards(shards, dtype, names=tuple(SHARD_SHAPE), rows=LAYER_ROWS):
    parts = []
    for n in names:
        lead = shards[n].shape[:-2]
        flat = shards[n].reshape(*lead, -1).astype(dtype)
        flat = jnp.pad(flat, [(0, 0)] * len(lead) + [(0, SHARD_ROWS[n] * PACK_COLS - flat.shape[-1])])
        parts.append(flat.reshape(*lead, SHARD_ROWS[n], PACK_COLS))
    lead = parts[0].shape[:-2]
    used = sum(SHARD_ROWS[n] for n in names)
    if rows > used:
        parts.append(jnp.zeros((*lead, rows - used, PACK_COLS), dtype))
    return jnp.concatenate(parts, axis=-2)


def _unpack_shards(buf, names=tuple(SHARD_SHAPE)):
    lead = buf.shape[:-2]
    out = {}
    for n in names:
        s0, s1 = SHARD_SHAPE[n]
        rows, off = SHARD_ROWS[n], SHARD_OFF[n]
        flat = buf[..., off:off + rows, :].reshape(*lead, rows * PACK_COLS)
        out[n] = flat[..., :s0 * s1].reshape(*lead, s0, s1)
    return out


def _join_chips(g, axis):
    _, d, s0, s1 = g.shape
    if axis == 0:
        return g.transpose(1, 0, 2, 3).reshape(d, 4 * s0, s1)
    return g.transpose(1, 2, 0, 3).reshape(d, s0, 4 * s1)


def _split_chips(w, axis):
    d = w.shape[0]
    if axis == 0:
        return w.reshape(d, 4, w.shape[1] // 4, w.shape[2]).transpose(1, 0, 2, 3)
    return w.reshape(d, w.shape[1], 4, w.shape[2] // 4).transpose(2, 0, 1, 3)


def _flat_rows(shape):
    return -(-math.prod(shape) // PACK_COLS)


def _pack_flat(arrs, rows, dtype=F32):
    parts = []
    for a in arrs:
        flat = a.reshape(-1).astype(dtype)
        r = _flat_rows(a.shape)
        parts.append(jnp.pad(flat, (0, r * PACK_COLS - flat.shape[0])).reshape(r, PACK_COLS))
    used = sum(p.shape[0] for p in parts)
    parts.append(jnp.zeros((rows - used, PACK_COLS), dtype))
    return jnp.concatenate(parts, axis=0)


def _split3(s):
    hi = s.astype(BF16).astype(F32)
    mid = (s - hi).astype(BF16).astype(F32)
    return jnp.stack([hi, mid, s - hi - mid])


def _unpack_flat(buf, shapes):
    out, off = [], 0
    for s in shapes:
        r = _flat_rows(s)
        out.append(buf[off:off + r].reshape(-1)[:math.prod(s)].reshape(s))
        off += r
    return out


def _write_rows(buf, src, row_offset):
    n, r, cols = src.shape

    def body(s_ref, b_ref, o_ref):
        o_ref[...] = s_ref[...].astype(o_ref.dtype)

    blk = (None, r, cols)
    return pl.pallas_call(
        body, name="write_rows", grid=(n,),
        in_specs=[pl.BlockSpec(blk, lambda s: (s, 0, 0)), pl.BlockSpec(memory_space=pl.ANY)],
        out_specs=pl.BlockSpec(blk, lambda s: (s, row_offset // r, 0)),
        out_shape=jax.ShapeDtypeStruct(buf.shape, buf.dtype), input_output_aliases={1: 0},
        compiler_params=_cp(("arbitrary",)))(src, buf)


def _write_glue(send, glue):
    return _write_rows(send, glue, SMALL_OFF)


GLUE_PACKED = ('w_in', 's5_glu_w', 'lru_conv_w')
REDUCE_ROWS = LAYER_ROWS // 4


def _pack_weights(w):
    buf = lax.empty((DEPTH, LAYER_ROWS, PACK_COLS), BF16)
    for n in ('w_mlp_in', 'w_mlp_out', 'w_out'):
        buf = _write_rows(buf, w[n], SHARD_OFF[n])
    return _write_rows(buf, _pack_shards(w, BF16, names=GLUE_PACKED, rows=GLUE_ROWS), SMALL_OFF)


def _layer_weights(gathered):
    parts = _unpack_shards(gathered[:, None], GLUE_PACKED)
    joined = {n: _join_chips(g, SHARDED_AXIS[n])[0] for n, g in parts.items()}
    joined['w_in'] = jnp.pad(joined['w_in'], ((0, 0), (0, D_IN_PAD - D_IN_PROJ)))
    return joined


def _reduce_start(send):
    halves = send.reshape(4, 2, LAYER_ROWS // 2, PACK_COLS)
    return _add_kept(halves, _pair_exchange(halves), REDUCE_ROWS)


def _reduce_finish(chip_sum, received):
    return _pair_share(_sum_chips(chip_sum, received, REDUCE_ROWS)).reshape(LAYER_ROWS, PACK_COLS)


def _forward_backward(x, target, final_g, shards, rep):
    norm_p = [{n: rep[n][l] for n in ('norm1_g', 'norm2_g', 'mix_norm_g')} for l in range(DEPTH)]
    prepared, prepare_vjp = jax.vjp(_prepare, {n: rep[n] for n in PREPARED_FROM})
    gathered = _allgather_shards(shards[0], EARLY_ROWS)
    (h,) = _rowwise(_f_rms, [x], [norm_p[0]["norm1_g"][None, :]], [D_MODEL], name="rms_f", tr=512, dtype=BF16)
    layers = []
    for l in range(DEPTH):
        lw = _layer_weights(gathered)
        mix_p = ({n: a[l] for n, a in prepared.items()}, lw['s5_glu_w'].astype(F32), lw['lru_conv_w'].astype(F32))
        last = l + 1 == DEPTH
        x, h, res, gathered = _layer_fwd(
            x, h, gathered, lw['w_in'], norm_p[l], mix_p, None if last else shards[l + 1],
            shards[l], None if last else norm_p[l + 1]["norm1_g"])
        layers.append((res, lw))
    loss_part, g, d_final = _loss_call(x, final_g[None, :], target)

    norms, d_prepared, reduced, later_send = [None] * DEPTH, [None] * DEPTH, [None] * DEPTH, None
    for l in reversed(range(DEPTH)):
        res, lw = layers[l]
        g, norms[l], (d_prepared[l], d_glu_w, d_conv_w), d_w_in, send, later_reduced = _layer_bwd(
            g, res, lw['w_in'], norm_p[l], later_send)
        if later_send is not None:
            reduced[l + 1] = later_reduced
        mine = {'w_in': d_w_in[:, :D_IN_PROJ], 's5_glu_w': d_glu_w, 'lru_conv_w': d_conv_w}
        glue = _pack_shards({n: _split_chips(a[None], SHARDED_AXIS[n]) for n, a in mine.items()}, BF16,
                            names=GLUE_PACKED, rows=GLUE_ROWS)[:, 0]
        later_send = _write_glue(send, glue)
    stack = lambda per_layer: {n: jnp.stack([per_layer[l][n] for l in range(DEPTH)]) for n in per_layer[0]}
    (d_rep,) = prepare_vjp(stack(d_prepared))
    return loss_part, g, d_final, dict(d_rep, **stack(norms)), reduced[1:], _reduce_start(later_send)


def _step(*args):
    x, target = args[0], args[1 + N_W]
    w = dict(zip(WEIGHTS, args[1:1 + N_W]))
    m = dict(zip(WEIGHTS, args[2 + N_W:2 + 2 * N_W]))
    v = dict(zip(WEIGHTS, args[2 + 2 * N_W:2 + 3 * N_W]))
    small = [n for n in WEIGHTS if n not in SHARD_SHAPE]

    shards = _pack_weights({n: w[n] for n in SHARD_SHAPE})
    loss_part, dx, d_final, dw, reduced_above, chip_sum0 = _forward_backward(
        x[0], target[0], w['final_g'], shards, {n: w[n] for n in small})

    small_g = [d_final.reshape(-1) if n == 'final_g' else dw[n] for n in small]
    small_rows = -(-(sum(_flat_rows(w[n].shape) for n in small) + 1) // 128) * 128
    mine = _pack_flat(small_g + [_split3(loss_part[0, 0])], small_rows, BF16)
    g_shard = _unpack_shards(jnp.stack([_reduce_finish(chip_sum0, _chip_exchange(chip_sum0))] + reduced_above))
    small_sum = _sum_slots(_allgather_all(mine), 128, "sum_devices")
    *g_small, loss = _unpack_flat(small_sum, [w[n].shape for n in small] + [(3,)])
    loss = jnp.sum(loss)

    grads, delta, new_m, new_v = {}, {}, {}, {}
    for n in SHARD_SHAPE:
        shp = w[n].shape
        v2 = lambda a: a.reshape(-1, shp[-1])
        res = _adamw_call(v2(w[n]), v2(g_shard[n]), v2(m[n]), v2(v[n]), "adamw_" + n)
        grads[n] = g_shard[n]
        delta[n], new_m[n], new_v[n] = (r.reshape(shp) for r in res)
    pk = lambda d: _pack_flat([d[n] for n in small], small_rows)
    res = _adamw_call(pk(w), _pack_flat(g_small, small_rows), pk(m), pk(v), "adamw_small")
    shapes = [w[n].shape for n in small]
    for n, g, d_, m_, v_ in zip(small, g_small, *(_unpack_flat(r, shapes) for r in res)):
        grads[n], delta[n], new_m[n], new_v[n] = g, d_, m_, v_

    return (loss, dx[None], *[grads[n] for n in WEIGHTS], *[delta[n] for n in WEIGHTS],
            *[new_m[n] for n in WEIGHTS], *[new_v[n] for n in WEIGHTS])


def kernel(x, norm1_g, w_in, sgu_norm_g, sgu_w, sgu_b, s5_lambda_re, s5_lambda_im, s5_log_dt, s5_b_re, s5_b_im, s5_c_re, s5_c_im, s5_d, s5_glu_w, s5_glu_b, lru_conv_w, lru_conv_b, lru_wa, lru_ba, lru_wx, lru_bx, lru_lambda, fox_fgate_b, mix_norm_g, w_out, norm2_g, w_mlp_in, w_mlp_out, final_g, loss_target, m_norm1_g, m_w_in, m_sgu_norm_g, m_sgu_w, m_sgu_b, m_s5_lambda_re, m_s5_lambda_im, m_s5_log_dt, m_s5_b_re, m_s5_b_im, m_s5_c_re, m_s5_c_im, m_s5_d, m_s5_glu_w, m_s5_glu_b, m_lru_conv_w, m_lru_conv_b, m_lru_wa, m_lru_ba, m_lru_wx, m_lru_bx, m_lru_lambda, m_fox_fgate_b, m_mix_norm_g, m_w_out, m_norm2_g, m_w_mlp_in, m_w_mlp_out, m_final_g, v_norm1_g, v_w_in, v_sgu_norm_g, v_sgu_w, v_sgu_b, v_s5_lambda_re, v_s5_lambda_im, v_s5_log_dt, v_s5_b_re, v_s5_b_im, v_s5_c_re, v_s5_c_im, v_s5_d, v_s5_glu_w, v_s5_glu_b, v_lru_conv_w, v_lru_conv_b, v_lru_wa, v_lru_ba, v_lru_wx, v_lru_bx, v_lru_lambda, v_fox_fgate_b, v_mix_norm_g, v_w_out, v_norm2_g, v_w_mlp_in, v_w_mlp_out, v_final_g):
    return _step(x, norm1_g, w_in, sgu_norm_g, sgu_w, sgu_b, s5_lambda_re, s5_lambda_im, s5_log_dt, s5_b_re, s5_b_im, s5_c_re, s5_c_im, s5_d, s5_glu_w, s5_glu_b, lru_conv_w, lru_conv_b, lru_wa, lru_ba, lru_wx, lru_bx, lru_lambda, fox_fgate_b, mix_norm_g, w_out, norm2_g, w_mlp_in, w_mlp_out, final_g, loss_target, m_norm1_g, m_w_in, m_sgu_norm_g, m_sgu_w, m_sgu_b, m_s5_lambda_re, m_s5_lambda_im, m_s5_log_dt, m_s5_b_re, m_s5_b_im, m_s5_c_re, m_s5_c_im, m_s5_d, m_s5_glu_w, m_s5_glu_b, m_lru_conv_w, m_lru_conv_b, m_lru_wa, m_lru_ba, m_lru_wx, m_lru_bx, m_lru_lambda, m_fox_fgate_b, m_mix_norm_g, m_w_out, m_norm2_g, m_w_mlp_in, m_w_mlp_out, m_final_g, v_norm1_g, v_w_in, v_sgu_norm_g, v_sgu_w, v_sgu_b, v_s5_lambda_re, v_s5_lambda_im, v_s5_log_dt, v_s5_b_re, v_s5_b_im, v_s5_c_re, v_s5_c_im, v_s5_d, v_s5_glu_w, v_s5_glu_b, v_lru_conv_w, v_lru_conv_b, v_lru_wa, v_lru_ba, v_lru_wx, v_lru_bx, v_lru_lambda, v_fox_fgate_b, v_mix_norm_g, v_w_out, v_norm2_g, v_w_mlp_in, v_w_mlp_out, v_final_g)
```

```python
import math

import jax
import jax.numpy as jnp
from jax import lax
from jax.experimental import pallas as pl
from jax.experimental.pallas import tpu as pltpu

F32 = jnp.float32
BF16 = jnp.bfloat16

DEPTH = 4
D_MODEL = 1024
MIXER_WIDTH = 256
SGU_CHUNK = 128
N_HEADS = 4
HEAD_DIM = 64
S5_GROUPS = 16
S5_GROUP = 16
S5_STATE = 64
LRU_C = 8.0
RMS_EPS = 1e-6
D_IN_PROJ = 8 * MIXER_WIDTH + N_HEADS
D_IN_PAD = 8 * MIXER_WIDTH + 128
ADAM_LR, ADAM_B1, ADAM_B2, ADAM_EPS, ADAM_WD, ADAM_STEP = 0.001, 0.9, 0.999, 1e-08, 0.01, 10

V7X_VMEM_BYTES = 64 * 1024 * 1024
VMEM_LIMIT = V7X_VMEM_BYTES - 8 * 1024 * 1024
NEG = -1e30
MESH = pl.DeviceIdType.MESH


def _cp(sem=None, **kw):
    return pltpu.CompilerParams(dimension_semantics=sem, vmem_limit_bytes=VMEM_LIMIT, **kw)


def _full_spec(a):
    nd = a.ndim
    return pl.BlockSpec(a.shape, lambda *_: (0,) * nd)


def _tile(n, pref=512):
    return pref if n % pref == 0 else n


def _dot(a, b, ca, cb):
    return lax.dot_general(a.astype(BF16), b.astype(BF16), (((ca,), (cb,)), ((), ())),
                           preferred_element_type=F32)


def _matmul(name, grid, a, a_spec, b, b_spec, dims, outs, *, extras=(), epilogue=None, into=None, summed=0,
            acc_shape=None, rider=None):
    nk = grid[2]
    n_ex, n_out = len(extras), len(outs)
    tm_tn = acc_shape or tuple(d for d in outs[0][1].block_shape if d is not None)[-2:]
    n_in = 2 + n_ex + (into is not None)

    def body(*refs):
        a_ref, b_ref = refs[0], refs[1]
        ex_refs = refs[2:2 + n_ex]
        o_refs = refs[n_in + (rider is not None):n_in + (rider is not None) + n_out]
        if rider is not None:
            start, finish_rider = rider[3](refs[n_in], refs[n_in + 1 + n_out], refs[-2], refs[-1])
            step = [pl.program_id(d) for d in range(3)]
            pl.when((step[0] == 0) & (step[1] == 0) & (step[2] == 0))(start)
        if summed:
            @pl.when((pl.program_id(0) == 0) & (pl.program_id(1) == 0) & (pl.program_id(2) == 0))
            def _():
                for o_ref in o_refs[n_out - summed:]:
                    o_ref[...] = jnp.zeros_like(o_ref)

        def finish(val):
            res = epilogue(val, *[e[...] for e in ex_refs]) if epilogue else (val,)
            for idx, (o_ref, r) in enumerate(zip(o_refs, res)):
                if idx >= n_out - summed:
                    o_ref[...] += r
                else:
                    o_ref[...] = r.astype(o_ref.dtype)

        if nk == 1:
            finish(_dot(a_ref[...], b_ref[...], *dims))
        else:
            acc = refs[n_in + (rider is not None) + n_out + (rider is not None)]
            kk = pl.program_id(2)

            @pl.when(kk == 0)
            def _():
                acc[...] = jnp.zeros_like(acc)

            acc[...] += _dot(a_ref[...], b_ref[...], *dims)

            @pl.when(kk == nk - 1)
            def _():
                finish(acc[...])

        if rider is not None:
            pl.when((step[0] == grid[0] - 1) & (step[1] == grid[1] - 1) & (step[2] == grid[2] - 1))(finish_rider)

    ins = [a, b] + [e[0] for e in extras]
    specs = [a_spec, b_spec] + [e[1] for e in extras]
    aliases = {}
    if into is not None:
        aliases = {len(ins): 0}
        ins.append(into)
        specs.append(pl.BlockSpec(memory_space=pl.ANY))
    out_specs, out_shape = [o[1] for o in outs], [o[0] for o in outs]
    scratch = [pltpu.VMEM(tm_tn, F32)] if nk > 1 else []
    if rider is not None:
        if rider[2]:
            aliases[len(ins)] = n_out
        ins.append(rider[0])
        specs.append(HBM)
        out_specs.append(HBM)
        out_shape.append(rider[1])
        scratch += [pltpu.SemaphoreType.DMA] * 2
        name += "_rider"
    res = pl.pallas_call(
        body, name=name, grid=grid, in_specs=specs, out_specs=out_specs, out_shape=out_shape,
        scratch_shapes=scratch, input_output_aliases=aliases,
        compiler_params=_cp(("arbitrary", "arbitrary", "arbitrary")))(*ins)
    return res[0] if len(res) == 1 else res


@jax.custom_vjp
def _bdot(a, b):
    return _dot(a, b, 1, 0)


def _bdot_fwd(a, b):
    return _dot(a, b, 1, 0), (a, b)


def _bdot_bwd(r, g):
    a, b = r
    return _dot(g, b, 1, 1), _dot(a, g, 0, 0)


_bdot.defvjp(_bdot_fwd, _bdot_bwd)


def _row_spec(tr, w):
    return pl.BlockSpec((tr, w), lambda i: (i, 0))


def _rowwise(fn, rows, pars, outs, *, name, tr, dtype=F32):
    t = rows[0].shape[0]
    n_in = len(rows) + len(pars)

    def body(*refs):
        res = fn(*[r[...] for r in refs[:n_in]])
        for o_ref, v in zip(refs[n_in:], res):
            o_ref[...] = v.astype(o_ref.dtype)

    return pl.pallas_call(
        body, name=name, grid=(t // tr,),
        in_specs=[_row_spec(tr, r.shape[1]) for r in rows] + [_full_spec(p) for p in pars],
        out_specs=[_row_spec(tr, w) for w in outs],
        out_shape=[jax.ShapeDtypeStruct((t, w), dtype) for w in outs],
        compiler_params=_cp(("arbitrary",)))(*rows, *pars)


def _rowwise_vjp(fn, rows, pars, cots, *, name, tr, add=None):
    t = rows[0].shape[0]
    nr, npar = len(rows), len(pars)
    cots = list(cots) + ([add] if add is not None else [])
    nc = len(cots)

    def body(*refs):
        vals = [r[...] for r in refs[:nr + npar]]
        cts = [c[...] for c in refs[nr + npar:nr + npar + nc]]
        douts = refs[nr + npar + nc:]
        extra = cts.pop() if add is not None else None
        _, vjp = jax.vjp(fn, *vals)
        grads = list(vjp(tuple(cts)))
        if extra is not None:
            grads[0] = grads[0] + extra
        for kk in range(nr):
            douts[kk][...] = grads[kk]

        @pl.when(pl.program_id(0) == 0)
        def _():
            for kk in range(npar):
                douts[nr + kk][...] = jnp.zeros_like(douts[nr + kk])

        for kk in range(npar):
            douts[nr + kk][...] += grads[nr + kk]

    return pl.pallas_call(
        body, name=name, grid=(t // tr,),
        in_specs=[_row_spec(tr, r.shape[1]) for r in rows] + [_full_spec(p) for p in pars]
        + [_row_spec(tr, c.shape[1]) for c in cots],
        out_specs=[_row_spec(tr, r.shape[1]) for r in rows] + [_full_spec(p) for p in pars],
        out_shape=[jax.ShapeDtypeStruct(r.shape, F32) for r in rows]
        + [jax.ShapeDtypeStruct(p.shape, F32) for p in pars],
        compiler_params=_cp(("arbitrary",)))(*rows, *pars, *cots)


def _make_rw(fn, name, tr, nr, outs):
    @jax.custom_vjp
    def f(*args):
        return tuple(_rowwise(fn, args[:nr], args[nr:], outs, name=name + "_f", tr=tr))

    def fwd(*args):
        return f(*args), args

    def bwd(args, cts):
        return tuple(_rowwise_vjp(fn, args[:nr], args[nr:], list(cts), name=name + "_b", tr=tr))

    f.defvjp(fwd, bwd)
    return f


def _rms(x, g):
    return x * lax.rsqrt(jnp.mean(jnp.square(x), axis=-1, keepdims=True) + RMS_EPS) * g


def _f_rms(x, g):
    return (_rms(x, g),)


def _f_sgu(au, av, ng, w0, w1, w2, w3, bfull):
    u = jax.nn.gelu(au)
    v = _rms(jax.nn.gelu(av), ng)
    tri = lax.broadcasted_iota(jnp.int32, (SGU_CHUNK, SGU_CHUNK), 0) >= lax.broadcasted_iota(
        jnp.int32, (SGU_CHUNK, SGU_CHUNK), 1)
    head = lax.broadcasted_iota(jnp.int32, v.shape, 1) // HEAD_DIM
    mixed = bfull
    for h, w in enumerate((w0, w1, w2, w3)):
        mixed = mixed + _bdot(jnp.where(tri, w, 0.0), jnp.where(head == h, v, 0.0))
    return (u * mixed,)


def _f_s5disc(lam_re, lam_im, log_dt, b_re, b_im):
    dt = jnp.exp(log_dt)
    mag = jnp.exp(lam_re * dt)
    abar_re = mag * jnp.cos(lam_im * dt)
    abar_im = mag * jnp.sin(lam_im * dt)
    denom = jnp.square(lam_re) + jnp.square(lam_im)
    num_re = abar_re - 1.0
    num_im = abar_im
    fac_re = (num_re * lam_re + num_im * lam_im) / denom
    fac_im = (num_im * lam_re - num_re * lam_im) / denom
    return abar_re, abar_im, fac_re * b_re - fac_im * b_im, fac_re * b_im + fac_im * b_re


def _f_s5post(s_re, s_im, u, c_re, c_im, d, gw, gb):
    y = _bdot(s_re, c_re) - _bdot(s_im, c_im) + d * u
    y = jax.nn.gelu(y)
    return (y * jax.nn.sigmoid(_bdot(y, gw) + gb),)


def _f_lrupre(xc, wa, ba, wx, bx, lam):
    r = jax.nn.sigmoid(_bdot(xc, wa) + ba)
    i = jax.nn.sigmoid(_bdot(xc, wx) + bx)
    log_a = -LRU_C * r * jax.nn.softplus(-lam)
    a = jnp.exp(log_a)
    one_minus_a2 = -jnp.tanh(log_a) * (jnp.exp(2.0 * log_a) + 1.0)
    return a, jnp.sqrt(one_minus_a2) * (i * xc)


def _f_lrupost(h, gate):
    return (h * jax.nn.gelu(gate),)


def _f_logsig(zf, bf):
    return (jax.nn.log_sigmoid(zf + bf),)


def _f_gnorm(ya, yb, yc, yd, g):
    def n(y):
        return y * lax.rsqrt(jnp.mean(jnp.square(y), axis=-1, keepdims=True) + RMS_EPS)
    return (jnp.concatenate([n(ya), n(yb), n(yc), n(yd)], axis=1) * g,)


sgu_mix = _make_rw(_f_sgu, "sgu", SGU_CHUNK, 2, [MIXER_WIDTH])
s5_disc = _make_rw(_f_s5disc, "s5disc", S5_GROUPS * S5_GROUP, 5, [S5_STATE] * 4)
s5_post = _make_rw(_f_s5post, "s5post", 256, 3, [MIXER_WIDTH])
lru_pre = _make_rw(_f_lrupre, "lrupre", 512, 1, [MIXER_WIDTH, MIXER_WIDTH])
lru_post = _make_rw(_f_lrupost, "lrupost", 512, 2, [MIXER_WIDTH])
log_sig = _make_rw(_f_logsig, "logsig", 512, 1, [128])


SCAN_TILE = 512


def _prev_spec(c, nt, rev):
    per = SCAN_TILE // 8
    if rev:
        return pl.BlockSpec((8, c), lambda i: (jnp.maximum((nt - 1 - i) * per - 1, 0), 0))
    return pl.BlockSpec((8, c), lambda i: (jnp.maximum(i * per - 1, 0), 0))


SCAN_STEPS = (1, 2, 4)


def _cmul(ar, ai, br, bi):
    return ar * br - ai * bi, ar * bi + ai * br


def _rows_down(x, k, fill, rowid):
    return jnp.where(rowid >= k, pltpu.roll(x, k, 0), fill)


def _rows_up(x, k, fill, rowid):
    return jnp.where(rowid < 8 - k, pltpu.roll(x, 8 - k, 0), fill)


def _powers(ar, ai):
    pw = [(ar, ai)]
    for _ in range(7):
        pw.append(_cmul(*pw[-1], ar, ai))
    return pw


def _block(i):
    return pl.ds(pl.multiple_of(i * 8, 8), 8)


def _row_before(ref, i, edge):
    return jnp.where(i == 0, edge, ref[pl.ds(jnp.maximum(i * 8 - 1, 0), 1), :])


def _lti_fwd_call(u, w_re, w_im, a_re, a_im):
    t, kdim = u.shape
    c = w_re.shape[1]
    tt = SCAN_TILE

    def body(u_ref, wr_ref, wi_ref, ar_ref, ai_ref, sr_ref, si_ref, br_ref, bi_ref, cr, ci):
        @pl.when(pl.program_id(0) == 0)
        def _():
            cr[...] = jnp.zeros_like(cr)
            ci[...] = jnp.zeros_like(ci)

        br_ref[...] = _dot(u_ref[...], wr_ref[...], 1, 0)
        bi_ref[...] = _dot(u_ref[...], wi_ref[...], 1, 0)
        pw = _powers(ar_ref[...], ai_ref[...])
        apr = jnp.concatenate([p[0] for p in pw], axis=0)
        api = jnp.concatenate([p[1] for p in pw], axis=0)
        rowid = lax.broadcasted_iota(jnp.int32, (8, c), 0)

        def block(i, carry):
            xr, xi = br_ref[_block(i), :], bi_ref[_block(i), :]
            for k in SCAN_STEPS:
                dr, di = _cmul(*pw[k - 1], _rows_down(xr, k, 0.0, rowid), _rows_down(xi, k, 0.0, rowid))
                xr, xi = xr + dr, xi + di
            dr, di = _cmul(apr, api, *carry)
            xr, xi = xr + dr, xi + di
            sr_ref[_block(i), :] = xr
            si_ref[_block(i), :] = xi
            return xr[7:8, :], xi[7:8, :]

        hr, hi = lax.fori_loop(0, tt // 8, block, (cr[...], ci[...]), unroll=2)
        cr[...] = hr
        ci[...] = hi

    row = pl.BlockSpec((tt, c), lambda i: (i, 0))
    par = pl.BlockSpec((1, c), lambda i: (0, 0))
    return pl.pallas_call(
        body, name="lti_scan_f", grid=(t // tt,),
        in_specs=[pl.BlockSpec((tt, kdim), lambda i: (i, 0)), _full_spec(w_re), _full_spec(w_im), par, par],
        out_specs=[row, row], out_shape=[jax.ShapeDtypeStruct((t, c), F32)] * 2,
        scratch_shapes=[pltpu.VMEM((tt, c), F32)] * 2 + [pltpu.VMEM((1, c), F32)] * 2,
        compiler_params=_cp(("arbitrary",)))(u, w_re, w_im, a_re, a_im)


def _lti_bwd_call(u, w_re, w_im, a_re, a_im, s_re, s_im, g_re, g_im):
    t, c = g_re.shape
    kdim = u.shape[1]
    tt = SCAN_TILE
    nt = t // tt
    nb = tt // 8

    def body(u_ref, wr_ref, wi_ref, ar_ref, ai_ref, sr_ref, si_ref, pr_ref, pi_ref, gr_ref, gi_ref,
             du_ref, dwr_ref, dwi_ref, dar_ref, dai_ref, or_ref, oi_ref, cr, ci):
        ti = pl.program_id(0)

        @pl.when(ti == 0)
        def _():
            cr[...] = jnp.zeros_like(cr)
            ci[...] = jnp.zeros_like(ci)
            dar_ref[...] = jnp.zeros_like(dar_ref)
            dai_ref[...] = jnp.zeros_like(dai_ref)
            dwr_ref[...] = jnp.zeros_like(dwr_ref)
            dwi_ref[...] = jnp.zeros_like(dwi_ref)

        pw = _powers(ar_ref[...], -ai_ref[...])
        tpr = jnp.concatenate([p[0] for p in reversed(pw)], axis=0)
        tpi = jnp.concatenate([p[1] for p in reversed(pw)], axis=0)
        rowid = lax.broadcasted_iota(jnp.int32, (8, c), 0)
        first = ti == nt - 1
        edge_r = jnp.where(first, 0.0, pr_ref[7:8, :])
        edge_i = jnp.where(first, 0.0, pi_ref[7:8, :])

        def block(kk, carry):
            i = nb - 1 - kk
            gr_c, gi_c, acc_r, acc_i = carry
            xr, xi = gr_ref[_block(i), :], gi_ref[_block(i), :]
            for k in SCAN_STEPS:
                dr, di = _cmul(*pw[k - 1], _rows_up(xr, k, 0.0, rowid), _rows_up(xi, k, 0.0, rowid))
                xr, xi = xr + dr, xi + di
            dr, di = _cmul(tpr, tpi, gr_c, gi_c)
            xr, xi = xr + dr, xi + di
            or_ref[_block(i), :] = xr
            oi_ref[_block(i), :] = xi
            spr = _rows_down(sr_ref[_block(i), :], 1, _row_before(sr_ref, i, edge_r), rowid)
            spi = _rows_down(si_ref[_block(i), :], 1, _row_before(si_ref, i, edge_i), rowid)
            return xr[0:1, :], xi[0:1, :], acc_r + spr * xr + spi * xi, acc_i + spr * xi - spi * xr

        zero = jnp.zeros((8, c), F32)
        gr_c, gi_c, acc_r, acc_i = lax.fori_loop(0, nb, block, (cr[...], ci[...], zero, zero), unroll=2)
        cr[...] = gr_c
        ci[...] = gi_c
        dar_ref[...] += jnp.sum(acc_r, axis=0, keepdims=True)
        dai_ref[...] += jnp.sum(acc_i, axis=0, keepdims=True)
        du_ref[...] = _dot(or_ref[...], wr_ref[...], 1, 1) + _dot(oi_ref[...], wi_ref[...], 1, 1)
        dwr_ref[...] += _dot(u_ref[...], or_ref[...], 0, 0)
        dwi_ref[...] += _dot(u_ref[...], oi_ref[...], 0, 0)

    row = pl.BlockSpec((tt, c), lambda i: (nt - 1 - i, 0))
    row_u = pl.BlockSpec((tt, kdim), lambda i: (nt - 1 - i, 0))
    par = pl.BlockSpec((1, c), lambda i: (0, 0))
    prev = _prev_spec(c, nt, True)
    return pl.pallas_call(
        body, name="lti_scan_b", grid=(nt,),
        in_specs=[row_u, _full_spec(w_re), _full_spec(w_im), par, par, row, row, prev, prev, row, row],
        out_specs=[row_u, _full_spec(w_re), _full_spec(w_im), par, par],
        out_shape=[jax.ShapeDtypeStruct((t, kdim), F32), jax.ShapeDtypeStruct(w_re.shape, F32),
                   jax.ShapeDtypeStruct(w_im.shape, F32)] + [jax.ShapeDtypeStruct((1, c), F32)] * 2,
        scratch_shapes=[pltpu.VMEM((tt, c), F32)] * 2 + [pltpu.VMEM((1, c), F32)] * 2,
        compiler_params=_cp(("arbitrary",)))(u, w_re, w_im, a_re, a_im, s_re, s_im, s_re, s_im, g_re, g_im)


@jax.custom_vjp
def lti_scan(u, w_re, w_im, a_re, a_im):
    return tuple(_lti_fwd_call(u, w_re, w_im, a_re, a_im))


def _lti_scan_fwd(u, w_re, w_im, a_re, a_im):
    s_re, s_im = _lti_fwd_call(u, w_re, w_im, a_re, a_im)
    return (s_re, s_im), (u, w_re, w_im, a_re, a_im, s_re, s_im)


def _lti_scan_bwd(r, g):
    return tuple(_lti_bwd_call(*r, g[0], g[1]))


lti_scan.defvjp(_lti_scan_fwd, _lti_scan_bwd)


def _tv_fwd_call(a, b):
    t, c = b.shape
    tt = SCAN_TILE

    def body(a_ref, b_ref, h_ref, ch):
        @pl.when(pl.program_id(0) == 0)
        def _():
            ch[...] = jnp.zeros_like(ch)

        rowid = lax.broadcasted_iota(jnp.int32, (8, c), 0)

        def block(i, h):
            ab, x = a_ref[_block(i), :], b_ref[_block(i), :]
            for k in SCAN_STEPS:
                x = x + ab * _rows_down(x, k, 0.0, rowid)
                ab = ab * _rows_down(ab, k, 1.0, rowid)
            x = x + ab * h
            h_ref[_block(i), :] = x
            return x[7:8, :]

        ch[...] = lax.fori_loop(0, tt // 8, block, ch[...], unroll=2)

    row = pl.BlockSpec((tt, c), lambda i: (i, 0))
    return pl.pallas_call(
        body, name="tv_scan_f", grid=(t // tt,), in_specs=[row, row], out_specs=row,
        out_shape=jax.ShapeDtypeStruct((t, c), F32), scratch_shapes=[pltpu.VMEM((1, c), F32)],
        compiler_params=_cp(("arbitrary",)))(a, b)


def _tv_bwd_call(a, h, g):
    t, c = g.shape
    tt = SCAN_TILE
    nt = t // tt
    nb = tt // 8

    def body(a_ref, h_ref, p_ref, g_ref, da_ref, db_ref, cg, ca):
        ti = pl.program_id(0)

        @pl.when(ti == 0)
        def _():
            cg[...] = jnp.zeros_like(cg)
            ca[...] = jnp.zeros_like(ca)

        rowid = lax.broadcasted_iota(jnp.int32, (8, c), 0)
        edge = jnp.where(ti == nt - 1, 0.0, p_ref[7:8, :])

        def block(kk, carry):
            i = nb - 1 - kk
            gc, a_next = carry
            ab, x = a_ref[_block(i), :], g_ref[_block(i), :]
            cb = _rows_up(ab, 1, a_next, rowid)
            for k in SCAN_STEPS:
                x = x + cb * _rows_up(x, k, 0.0, rowid)
                cb = cb * _rows_up(cb, k, 1.0, rowid)
            x = x + cb * gc
            db_ref[_block(i), :] = x
            da_ref[_block(i), :] = x * _rows_down(h_ref[_block(i), :], 1, _row_before(h_ref, i, edge), rowid)
            return x[0:1, :], ab[0:1, :]

        gc, a_next = lax.fori_loop(0, nb, block, (cg[...], ca[...]), unroll=2)
        cg[...] = gc
        ca[...] = a_next

    row = pl.BlockSpec((tt, c), lambda i: (nt - 1 - i, 0))
    return pl.pallas_call(
        body, name="tv_scan_b", grid=(nt,), in_specs=[row, row, _prev_spec(c, nt, True), row],
        out_specs=[row, row], out_shape=[jax.ShapeDtypeStruct((t, c), F32)] * 2,
        scratch_shapes=[pltpu.VMEM((1, c), F32)] * 2, compiler_params=_cp(("arbitrary",)))(a, h, h, g)


@jax.custom_vjp
def tv_scan(a, b):
    return _tv_fwd_call(a, b)


def _tv_scan_fwd(a, b):
    h = _tv_fwd_call(a, b)
    return h, (a, h)


def _tv_scan_bwd(r, g):
    a, h = r
    return tuple(_tv_bwd_call(a, h, g))


tv_scan.defvjp(_tv_scan_fwd, _tv_scan_bwd)


CONV_K = 4
CONV_ROWS = 512


def _conv_fwd_call(x, w, b):
    t, c = x.shape

    def body(x_ref, w_ref, b_ref, o_ref, xp):
        xp[0:8, :] = jnp.zeros((8, c), F32)
        xp[8:, :] = x_ref[...]
        for blk in range(t // CONV_ROWS):
            base = blk * CONV_ROWS
            acc = jnp.broadcast_to(b_ref[...], (CONV_ROWS, c))
            for kk in range(CONV_K):
                acc = acc + w_ref[kk:kk + 1, :] * xp[base + 5 + kk:base + 5 + kk + CONV_ROWS, :]
            o_ref[base:base + CONV_ROWS, :] = acc

    return pl.pallas_call(
        body, name="conv_f", out_shape=jax.ShapeDtypeStruct((t, c), F32),
        scratch_shapes=[pltpu.VMEM((t + 8, c), F32)], compiler_params=_cp())(x, w, b)


def _conv_bwd_call(x, w, g):
    t, c = x.shape

    def body(x_ref, w_ref, g_ref, dx_ref, dw_ref, db_ref, xp, gp):
        xp[0:8, :] = jnp.zeros((8, c), F32)
        xp[8:, :] = x_ref[...]
        gp[0:t, :] = g_ref[...]
        gp[t:, :] = jnp.zeros((8, c), F32)
        dw = [jnp.zeros((1, c), F32) for _ in range(CONV_K)]
        db = jnp.zeros((1, c), F32)
        for blk in range(t // CONV_ROWS):
            base = blk * CONV_ROWS
            gb = g_ref[base:base + CONV_ROWS, :]
            acc = jnp.zeros((CONV_ROWS, c), F32)
            for kk in range(CONV_K):
                acc = acc + w_ref[kk:kk + 1, :] * gp[base + 3 - kk:base + 3 - kk + CONV_ROWS, :]
                dw[kk] = dw[kk] + jnp.sum(gb * xp[base + 5 + kk:base + 5 + kk + CONV_ROWS, :], axis=0, keepdims=True)
            db = db + jnp.sum(gb, axis=0, keepdims=True)
            dx_ref[base:base + CONV_ROWS, :] = acc
        for kk in range(CONV_K):
            dw_ref[kk:kk + 1, :] = dw[kk]
        db_ref[...] = db

    return pl.pallas_call(
        body, name="conv_b",
        out_shape=[jax.ShapeDtypeStruct((t, c), F32), jax.ShapeDtypeStruct((CONV_K, c), F32),
                   jax.ShapeDtypeStruct((1, c), F32)],
        scratch_shapes=[pltpu.VMEM((t + 8, c), F32)] * 2, compiler_params=_cp())(x, w, g)


@jax.custom_vjp
def causal_conv(x, w, b):
    return _conv_fwd_call(x, w, b)


def _causal_conv_fwd(x, w, b):
    return _conv_fwd_call(x, w, b), (x, w)


def _causal_conv_bwd(r, g):
    return tuple(_conv_bwd_call(r[0], r[1], g))


causal_conv.defvjp(_causal_conv_fwd, _causal_conv_bwd)


ATT_TILE = 512
ATT_SCALE = HEAD_DIM ** -0.5


def _head_lane(val, lane, h):
    return jnp.sum(jnp.where(lane == h, val, 0.0), axis=1, keepdims=True)


def _attn_fwd_call(qkv, c128, cr, next_shard=None, late=None):
    t, w = qkv.shape[0], MIXER_WIDTH
    tq = ATT_TILE
    nq = t // tq
    qkv3, cr4 = qkv.reshape(nq, tq, 3 * w), cr.reshape(N_HEADS, nq, 1, tq)
    fused = next_shard is not None
    both = late is not None

    def body(*refs):
        q_ref, k_ref, v_ref, c_ref, cr_ref = refs[:5]
        i, h = pl.program_id(0), pl.program_id(1)
        rest = list(refs[5:])
        shard_ref = rest.pop(0) if fused else None
        own_ref = rest.pop(0) if both else None
        if both:
            rest.pop(0)
        o_ref, lse_ref = rest.pop(0), rest.pop(0)
        gathers = []
        if fused:
            gathers.append(_allgather_copies(shard_ref, rest.pop(0), rest[-2 - 2 * both], rest[-1 - 2 * both], EARLY_ROWS))
        if both:
            gathers.append(_allgather_copies(own_ref, rest.pop(0), rest[-2], rest[-1], LATE_ROWS))
        for start, _ in gathers:
            pl.when((i == 0) & (h == 0))(start)
        hm = lax.broadcasted_iota(jnp.int32, (tq, w), 1) // HEAD_DIM == h
        lane = lax.broadcasted_iota(jnp.int32, (tq, 128), 1)
        qs = jnp.where(hm, q_ref[...] * ATT_SCALE, 0.0)
        cq = _head_lane(c_ref[...], lane, h)
        causal = lax.broadcasted_iota(jnp.int32, (tq, tq), 0) >= lax.broadcasted_iota(jnp.int32, (tq, tq), 1)

        def update(j, carry, diagonal):
            m, l, acc = carry
            s = _dot(qs, k_ref[j], 1, 1) - cr_ref[0, j]
            if diagonal:
                s = jnp.where(causal, s, NEG)
            m_new = jnp.maximum(m, cq + jnp.max(s, axis=1, keepdims=True))
            p = jnp.exp(s + (cq - m_new))
            alpha = jnp.exp(m - m_new)
            return m_new, alpha * l + jnp.sum(p, axis=1, keepdims=True), alpha * acc + _dot(p, v_ref[j], 1, 0)

        init = (jnp.full((tq, 1), NEG, F32), jnp.zeros((tq, 1), F32), jnp.zeros((tq, w), F32))
        carry = lax.fori_loop(0, i, lambda j, c: update(j, c, False), init)
        m, l, acc = update(i, carry, True)
        out = jnp.where(hm, acc / l, 0.0)
        lse = jnp.where(lane == h, m + jnp.log(l), 0.0)

        @pl.when(h == 0)
        def _():
            o_ref[...] = out
            lse_ref[...] = lse

        @pl.when(h > 0)
        def _():
            o_ref[...] += out
            lse_ref[...] += lse

        for _, finish in gathers:
            pl.when((i == nq - 1) & (h == N_HEADS - 1))(finish)

    tile = pl.BlockSpec((tq, w), lambda i, h: (i, 0))
    tile_c = pl.BlockSpec((tq, 128), lambda i, h: (i, 0))
    rows = pl.BlockSpec((1, nq, 1, tq), lambda i, h: (h, 0, 0, 0))
    q_spec = pl.BlockSpec((None, tq, w), lambda i, h: (i, 0, 0))
    k_spec = pl.BlockSpec((nq, tq, w), lambda i, h: (0, 0, 1))
    v_spec = pl.BlockSpec((nq, tq, w), lambda i, h: (0, 0, 2))
    ins = [qkv3, qkv3, qkv3, c128, cr4]
    in_specs, out_specs = [q_spec, k_spec, v_spec, tile_c, rows], [tile, tile_c]
    out_shape = [jax.ShapeDtypeStruct((t, w), F32), jax.ShapeDtypeStruct((t, 128), F32)]
    scratch, aliases, name = [], {}, "attn_f"
    sem_pair = [pltpu.SemaphoreType.DMA((AG_SEMS,)), pltpu.SemaphoreType.DMA((AG_SEMS,))]
    if fused:
        ins.append(next_shard)
        in_specs.append(HBM)
        name += "_next"
    if both:
        aliases = {len(ins) + 1: 2 + fused}
        ins += list(late)
        in_specs += [HBM, HBM]
        name += "_late"
    if fused:
        out_specs.append(HBM)
        out_shape.append(jax.ShapeDtypeStruct((4,) + next_shard.shape, next_shard.dtype))
        scratch += sem_pair
    if both:
        out_specs.append(HBM)
        out_shape.append(jax.ShapeDtypeStruct(late[1].shape, late[1].dtype))
        scratch += sem_pair
    return pl.pallas_call(
        body, name=name, grid=(nq, N_HEADS), in_specs=in_specs, out_specs=out_specs, out_shape=out_shape,
        scratch_shapes=scratch, input_output_aliases=aliases,
        compiler_params=_cp(("arbitrary", "arbitrary")))(*ins)


def _attn_bwd_call(qkv, c128, cr, o, lse, do, exchange=None):
    t, w = qkv.shape[0], MIXER_WIDTH
    tq = ATT_TILE
    nq = t // tq
    r3 = lambda a: a.reshape(nq, tq, a.shape[-1])
    cr4 = cr.reshape(N_HEADS, nq, 1, tq)
    fused = exchange is not None

    def body(*refs):
        q_ref, k_ref, v_ref, c_ref, cr_ref, o_ref, lse_ref, do_ref = refs[:8]
        j, h = pl.program_id(0), pl.program_id(1)
        if fused:
            p_ref, dq_ref, dk_ref, dv_ref, dc_ref, dcr_ref, recv_ref, send_sems, recv_sems = refs[8:]
            start, finish = _chip_exchange_copies(p_ref, recv_ref, send_sems, recv_sems)
            pl.when((j == 0) & (h == 0))(start)
        else:
            dq_ref, dk_ref, dv_ref, dc_ref, dcr_ref = refs[8:]

        @pl.when((j == 0) & (h == 0))
        def _():
            dq_ref[...] = jnp.zeros_like(dq_ref)
            dc_ref[...] = jnp.zeros_like(dc_ref)

        hm = lax.broadcasted_iota(jnp.int32, (tq, w), 1) // HEAD_DIM == h
        lane = lax.broadcasted_iota(jnp.int32, (tq, 128), 1)
        kj = k_ref[...]
        vj = v_ref[...]
        ck = cr_ref[0, 0]
        causal = lax.broadcasted_iota(jnp.int32, (tq, tq), 0) >= lax.broadcasted_iota(jnp.int32, (tq, tq), 1)

        def step(i, carry, diagonal):
            dk, dv, dck = carry
            qm = jnp.where(hm, q_ref[i], 0.0)
            dom = jnp.where(hm, do_ref[i], 0.0)
            s = _dot(qm * ATT_SCALE, kj, 1, 1) - ck
            if diagonal:
                s = jnp.where(causal, s, NEG)
            p = jnp.exp(s + (_head_lane(c_ref[i], lane, h) - _head_lane(lse_ref[i], lane, h)))
            dv = dv + _dot(p, dom, 0, 0)
            dp = _dot(dom, vj, 1, 1)
            delta = jnp.sum(dom * o_ref[i], axis=1, keepdims=True)
            ds = p * (dp - delta)
            dq_ref[i] += jnp.where(hm, _dot(ds, kj, 1, 0), 0.0) * ATT_SCALE
            dk = dk + _dot(ds, qm, 0, 0) * ATT_SCALE
            dc_ref[i] += jnp.where(lane == h, jnp.sum(ds, axis=1, keepdims=True), 0.0)
            return dk, dv, dck - jnp.sum(ds, axis=0, keepdims=True)

        init = (jnp.zeros((tq, w), F32), jnp.zeros((tq, w), F32), jnp.zeros((1, tq), F32))
        carry = step(j, init, True)
        dk, dv, dck = lax.fori_loop(j + 1, nq, lambda i, c: step(i, c, False), carry)
        dcr_ref[0, 0] = dck

        @pl.when(h == 0)
        def _():
            dk_ref[...] = dk
            dv_ref[...] = dv

        @pl.when(h > 0)
        def _():
            dk_ref[...] += dk
            dv_ref[...] += dv

        if fused:
            pl.when((j == nq - 1) & (h == N_HEADS - 1))(finish)

    whole = pl.BlockSpec((nq, tq, w), lambda j, h: (0, 0, 0))
    whole_c = pl.BlockSpec((nq, tq, 128), lambda j, h: (0, 0, 0))
    tile = pl.BlockSpec((None, tq, w), lambda j, h: (j, 0, 0))
    tile_r = pl.BlockSpec((1, 1, 1, tq), lambda j, h: (h, j, 0, 0))
    s3 = jax.ShapeDtypeStruct((nq, tq, w), F32)
    qkv3 = r3(qkv)
    k_tile = pl.BlockSpec((None, tq, w), lambda j, h: (j, 0, 1))
    v_tile = pl.BlockSpec((None, tq, w), lambda j, h: (j, 0, 2))
    ins = [qkv3, qkv3, qkv3, r3(c128), cr4, r3(o), r3(lse), r3(do)]
    in_specs = [whole, k_tile, v_tile, whole_c, tile_r, whole, whole_c, whole]
    out_specs = [whole, tile, tile, whole_c, tile_r]
    out_shape = [s3, s3, s3, jax.ShapeDtypeStruct((nq, tq, 128), F32), jax.ShapeDtypeStruct((N_HEADS, nq, 1, tq), F32)]
    scratch = []
    if fused:
        ins.append(exchange)
        in_specs.append(HBM)
        out_specs.append(HBM)
        out_shape.append(jax.ShapeDtypeStruct((3,) + exchange.shape[1:], exchange.dtype))
        scratch = [pltpu.SemaphoreType.DMA((3,)), pltpu.SemaphoreType.DMA((3,))]
    dq, dk, dv, dc, dcr, *received = pl.pallas_call(
        body, name="attn_b_exchange" if fused else "attn_b", grid=(nq, N_HEADS), in_specs=in_specs,
        out_specs=out_specs, out_shape=out_shape, scratch_shapes=scratch,
        compiler_params=_cp(("arbitrary", "arbitrary")))(*ins)
    grads = (dq.reshape(t, w), dk.reshape(t, w), dv.reshape(t, w), dc.reshape(t, 128), dcr.reshape(N_HEADS, 1, t))
    return grads, (received[0] if fused else None)


def _loss_call(x, g, target):
    t, d = x.shape
    tr = 512

    def body(x_ref, g_ref, t_ref, loss_ref, dx_ref, dg_ref):
        tgt = t_ref[...]

        def f(xv, gv):
            return 0.5 * jnp.sum(jnp.mean(jnp.square(_rms(xv, gv) - tgt), axis=-1))

        val, vjp = jax.vjp(f, x_ref[...], g_ref[...])
        dx, dg = vjp(jnp.ones((), F32))
        dx_ref[...] = dx

        @pl.when(pl.program_id(0) == 0)
        def _():
            loss_ref[...] = jnp.zeros_like(loss_ref)
            dg_ref[...] = jnp.zeros_like(dg_ref)

        loss_ref[...] += jnp.full(loss_ref.shape, val, F32)
        dg_ref[...] += dg

    row = _row_spec(tr, d)
    return pl.pallas_call(
        body, name="loss_head", grid=(t // tr,), in_specs=[row, _full_spec(g), row],
        out_specs=[pl.BlockSpec((1, 128), lambda i: (0, 0)), row, _full_spec(g)],
        out_shape=[jax.ShapeDtypeStruct((1, 128), F32), jax.ShapeDtypeStruct((t, d), F32),
                   jax.ShapeDtypeStruct(g.shape, F32)],
        compiler_params=_cp(("arbitrary",)))(x, g, target)


def _blockdiag(w):
    l, g, a, b = w.shape
    return jnp.einsum('lgab,gk->lgakb', w, jnp.eye(g, dtype=w.dtype)).reshape(l, g * a, g * b)


def _prepare(rep):
    d = DEPTH
    w = MIXER_WIDTH
    rows = S5_GROUPS * S5_GROUP
    rep16 = lambda a: jnp.repeat(a, S5_GROUP, axis=1).reshape(d * rows, -1)
    bt = lambda b: b.transpose(0, 1, 3, 2).reshape(d * rows, S5_STATE)
    abar_re, abar_im, bb_re, bb_im = s5_disc(
        rep16(rep["s5_lambda_re"]), rep16(rep["s5_lambda_im"]), rep16(rep["s5_log_dt"][:, :, None]),
        bt(rep["s5_b_re"]), bt(rep["s5_b_im"]))
    g4 = lambda a: a.reshape(d, S5_GROUPS, S5_GROUP, S5_STATE)
    first = lambda a: g4(a)[:, :, 0, :].reshape(d, 1, S5_GROUPS * S5_STATE)
    cblk = lambda c: _blockdiag(c.transpose(0, 1, 3, 2))
    row = lambda a: a.reshape(d, 1, -1)
    return dict(
        sgu_norm_g=row(rep["sgu_norm_g"]), sgu_w=rep["sgu_w"],
        sgu_bias=jnp.repeat(rep["sgu_b"].transpose(0, 2, 1), HEAD_DIM, axis=2),
        abar_re=first(abar_re), abar_im=first(abar_im), bblk_re=_blockdiag(g4(bb_re)), bblk_im=_blockdiag(g4(bb_im)),
        cblk_re=cblk(rep["s5_c_re"]), cblk_im=cblk(rep["s5_c_im"]), s5_d=row(rep["s5_d"]), s5_glu_b=row(rep["s5_glu_b"]),
        lru_conv_b=row(rep["lru_conv_b"]), lru_wa=_blockdiag(rep["lru_wa"]), lru_ba=row(rep["lru_ba"]),
        lru_wx=_blockdiag(rep["lru_wx"]), lru_bx=row(rep["lru_bx"]), lru_lambda=row(rep["lru_lambda"]),
        fgate_b=jnp.pad(rep["fox_fgate_b"], ((0, 0), (0, 128 - N_HEADS)))[:, None, :])


PREPARED_FROM = ('sgu_norm_g', 'sgu_w', 'sgu_b', 's5_lambda_re', 's5_lambda_im', 's5_log_dt', 's5_b_re', 's5_b_im',
                 's5_c_re', 's5_c_im', 's5_d', 's5_glu_b', 'lru_conv_b', 'lru_wa', 'lru_ba', 'lru_wx', 'lru_bx',
                 'lru_lambda', 'fox_fgate_b')


def _mixers_pre(pieces, p, glu_w, conv_w):
    a_u, a_v, b_in, c_x, c_gate, d_q, d_k, d_v, d_f = pieces
    sw = p["sgu_w"]
    (y_a,) = sgu_mix(a_u, a_v, p["sgu_norm_g"], sw[0], sw[1], sw[2], sw[3], p["sgu_bias"])
    s_re, s_im = lti_scan(b_in, p["bblk_re"], p["bblk_im"], p["abar_re"], p["abar_im"])
    (y_b,) = s5_post(s_re, s_im, b_in, p["cblk_re"], p["cblk_im"], p["s5_d"], glu_w, p["s5_glu_b"])
    xc = causal_conv(c_x, conv_w, p["lru_conv_b"])
    a, b = lru_pre(xc, p["lru_wa"], p["lru_ba"], p["lru_wx"], p["lru_bx"], p["lru_lambda"])
    (y_c,) = lru_post(tv_scan(a, b), c_gate)
    (log_f,) = log_sig(d_f, p["fgate_b"])
    c128 = tv_scan(jnp.ones_like(log_f), log_f)
    return y_a, y_b, y_c, d_q, d_k, d_v, c128, c128[:, :N_HEADS].T[:, None, :]


PACK_COLS = 1024
SHARD_SHAPE = {'w_mlp_in': (1024, 1024), 'w_mlp_out': (1024, 1024), 'w_out': (256, 1024), 'w_in': (1024, 513),
               's5_glu_w': (64, 256), 'lru_conv_w': (4, 64)}
SHARDED_AXIS = {'w_in': 1, 's5_glu_w': 0, 'lru_conv_w': 1, 'w_out': 0, 'w_mlp_in': 1, 'w_mlp_out': 0}
SHARD_ROWS = {n: -(-s[0] * s[1] // PACK_COLS) for n, s in SHARD_SHAPE.items()}
SHARD_OFF = {n: sum(list(SHARD_ROWS.values())[:i]) for i, n in enumerate(SHARD_SHAPE)}
LAYER_ROWS = 2880
SMALL_OFF = SHARD_OFF['w_in']
GLUE_ROWS = LAYER_ROWS - SMALL_OFF
assert SHARD_OFF['w_mlp_out'] == 1024 and SHARD_OFF['w_out'] == 2048 and SMALL_OFF % GLUE_ROWS == 0
assert SHARD_OFF['lru_conv_w'] + SHARD_ROWS['lru_conv_w'] <= LAYER_ROWS
PACK_ROWS = DEPTH * LAYER_ROWS
TOK = 1024
FF = 4 * D_MODEL


def _w3(i_of):
    return pl.BlockSpec((None, 1024, PACK_COLS), i_of)


def _tile2(rows, cols, i_of):
    return pl.BlockSpec((rows, cols), i_of)


def _layer_fwd(x, h1, gathered, w_in, p, mix_p, next_shard, own_shard, next_gain):
    t = x.shape[0]
    nt = t // TOK
    f32 = lambda r, c: jax.ShapeDtypeStruct((r, c), F32)
    b16 = lambda r, c: jax.ShapeDtypeStruct((r, c), BF16)
    g2, gm = p["norm2_g"][None, :], p["mix_norm_g"][None, :]
    qkv_cols = (5 * MIXER_WIDTH, 8 * MIXER_WIDTH)
    z, qkv = _matmul("mm_in", (nt, 1, 1), h1, _tile2(TOK, D_MODEL, lambda i, j, k: (i, 0)),
                     w_in, _tile2(D_MODEL, D_IN_PAD, lambda i, j, k: (0, 0)), (1, 0),
                     [(f32(t, D_IN_PAD), _tile2(TOK, D_IN_PAD, lambda i, j, k: (i, 0))),
                      (b16(t, 3 * MIXER_WIDTH), _tile2(TOK, 3 * MIXER_WIDTH, lambda i, j, k: (i, 0)))],
                     epilogue=lambda acc: (acc, acc[:, qkv_cols[0]:qkv_cols[1]]))
    pieces = tuple(jnp.split(z, [MIXER_WIDTH * i for i in range(1, 9)], axis=1))
    (y_a, y_b, y_c, _, _, _, c128, cr), mix_vjp = jax.vjp(_mixers_pre, pieces, *mix_p)
    attn_in = (qkv, c128, cr)
    y_d, lse, *more = _attn_fwd_call(*attn_in, next_shard=next_shard,
                                     late=None if own_shard is None else (own_shard, gathered))
    if own_shard is not None:
        gathered = more.pop()
    w_out = _join_chips(_unpack_shards(gathered[:, None], ('w_out',))['w_out'], SHARDED_AXIS['w_out'])[0]
    ys = (y_a, y_b, y_c, y_d)
    (yn,) = _rowwise(_f_gnorm, list(ys), [gm], [D_MODEL], name="gnorm_f", tr=512, dtype=BF16)
    x_tile = _tile2(TOK, D_MODEL, lambda i, j, k: (i, 0))
    gain = _tile2(1, D_MODEL, lambda i, j, k: (0, 0))

    def add_and_norm(acc, r, gv):
        s = acc + r
        return s, _rms(s, gv)

    x1, h2 = _matmul("mm_out", (nt, 1, 1), yn, x_tile, w_out, _tile2(D_MODEL, D_MODEL, lambda i, j, k: (0, 0)), (1, 0),
                     [(f32(t, D_MODEL), x_tile), (b16(t, D_MODEL), x_tile)], extras=[(x, x_tile), (g2, gain)],
                     epilogue=add_and_norm)
    ff_tile = _tile2(TOK, 1024, lambda i, j, k: (i, j))
    act = _matmul("mm_up", (nt, FF // 1024, 1), h2, x_tile, gathered, _w3(lambda i, j, k: (j, 0, 0)), (1, 0),
                  [(b16(t, FF), ff_tile)], epilogue=lambda acc: (jnp.square(jnp.maximum(acc, 0.0)),))
    down = ("mm_down", (nt, 1, FF // 1024), act, _tile2(TOK, 1024, lambda i, j, k: (i, k)),
            gathered, _w3(lambda i, j, k: (k, 1, 0)), (1, 0))
    if next_gain is None:
        x2, h_next = _matmul(*down, [(f32(t, D_MODEL), x_tile)], extras=[(x1, x_tile)],
                             epilogue=lambda acc, r: (acc + r,)), None
    else:
        x2, h_next = _matmul(*down, [(f32(t, D_MODEL), x_tile), (b16(t, D_MODEL), x_tile)],
                             extras=[(x1, x_tile), (next_gain[None, :], gain)], epilogue=add_and_norm)
    res = (x, h1, mix_vjp, ys, attn_in, lse, yn, x1, h2, act, gathered, w_out)
    return x2, h_next, res, (more[0] if more else None)


def _layer_bwd(g, res, w_in, p, later_send):
    x, h1, mix_vjp, ys, attn_in, lse, yn, x1, h2, act, gathered, w_out = res
    half_rows = LAYER_ROWS // 2
    riding = later_send is not None
    halves = later_send.reshape(4, 2, half_rows, PACK_COLS) if riding else None
    send = lax.empty((4, LAYER_ROWS, PACK_COLS), BF16)
    t = x.shape[0]
    nt = t // TOK
    f32 = lambda r, c: jax.ShapeDtypeStruct((r, c), F32)
    g1, g2, gm = p["norm1_g"][None, :], p["norm2_g"][None, :], p["mix_norm_g"][None, :]
    x_tile = _tile2(TOK, D_MODEL, lambda i, j, k: (i, 0))
    ff_tile = _tile2(TOK, 1024, lambda i, j, k: (i, j))
    tok_k = _tile2(TOK, D_MODEL, lambda i, j, k: (k, 0))
    send_s = jax.ShapeDtypeStruct(send.shape, send.dtype)
    pair_rider = (halves, jax.ShapeDtypeStruct((4, half_rows, PACK_COLS), BF16), False, _pair_exchange_copies)
    du = _matmul("mm_down_dx", (nt, FF // 1024, 1), g, x_tile, gathered, _w3(lambda i, j, k: (j, 1, 0)), (1, 1),
                 [(jax.ShapeDtypeStruct((t, FF), BF16), ff_tile)], extras=[(act, ff_tile)],
                 epilogue=lambda acc, a: (2.0 * jnp.sqrt(a.astype(F32)) * acc,),
                 rider=pair_rider if riding else None)
    exchange = None
    if riding:
        du, from_sibling = du
        exchange = _add_kept(halves, from_sibling, REDUCE_ROWS)
    send = _matmul("mm_down_dw", (FF // 1024, 1, nt), act, _tile2(TOK, 1024, lambda i, j, k: (k, i)), g, tok_k, (0, 0),
                   [(send_s, _w3(lambda i, j, k: (i, 1, 0)))], into=send)
    send = _matmul("mm_up_dw", (1, FF // 1024, nt), h2, tok_k, du, _tile2(TOK, 1024, lambda i, j, k: (k, j)), (0, 0),
                   [(send_s, _w3(lambda i, j, k: (j, 0, 0)))], into=send)
    gain = _tile2(1, D_MODEL, lambda i, j, k: (0, 0))

    def norm_bwd(dh, xv, gv, through):
        _, vjp = jax.vjp(_rms, xv, gv)
        dxv, dgv = vjp(dh)
        return dxv + through, dgv

    g_mid, dg2 = _matmul("mm_up_dx", (nt, 1, FF // 1024), du, _tile2(TOK, 1024, lambda i, j, k: (i, k)),
                         gathered, _w3(lambda i, j, k: (k, 0, 0)), (1, 1),
                         [(f32(t, D_MODEL), x_tile), (f32(1, D_MODEL), gain)],
                         extras=[(x1, x_tile), (g2, gain), (g, x_tile)], epilogue=norm_bwd, summed=1)
    w_full = _tile2(D_MODEL, D_MODEL, lambda i, j, k: (0, 0))
    dyn = _matmul("mm_out_dx", (nt, 1, 1), g_mid, x_tile, w_out, w_full, (1, 1), [(f32(t, D_MODEL), x_tile)])
    quarter = D_MODEL // 4
    send = _matmul("mm_out_dw", (1, 1, nt), yn, tok_k, g_mid, tok_k, (0, 0),
                   [(send_s, pl.BlockSpec((4, quarter, PACK_COLS), lambda i, j, k: (0, SHARD_OFF['w_out'] // quarter, 0)))],
                   epilogue=lambda acc: (acc.reshape(4, quarter, PACK_COLS),), into=send, acc_shape=(D_MODEL, D_MODEL))
    dy_a, dy_b, dy_c, dy_d, dgm = _rowwise_vjp(_f_gnorm, list(ys), [gm], [dyn], name="gnorm_b", tr=512)
    d_attn_in, received = _attn_bwd_call(*attn_in, ys[3], lse, dy_d, exchange=exchange)
    d_pieces, *d_mix = mix_vjp((dy_a, dy_b, dy_c, *d_attn_in))
    dz = jnp.concatenate([d.astype(BF16) for d in d_pieces], axis=1)
    z_tile = _tile2(TOK, D_IN_PAD, lambda i, j, k: (i, 0))
    share_rider = None
    if riding:
        pair = _sum_chips(exchange, received, REDUCE_ROWS)
        share_rider = (pair, jax.ShapeDtypeStruct(pair.shape, pair.dtype), True, _pair_share_copies)
    d_w_in = _matmul("mm_in_dw", (1, 1, t // 512), h1, _tile2(512, D_MODEL, lambda i, j, k: (k, 0)),
                     dz, _tile2(512, D_IN_PAD, lambda i, j, k: (k, 0)), (0, 0),
                     [(f32(D_MODEL, D_IN_PAD), _tile2(D_MODEL, D_IN_PAD, lambda i, j, k: (0, 0)))], rider=share_rider)
    reduced = None
    if riding:
        d_w_in, pair = d_w_in
        reduced = pair.reshape(LAYER_ROWS, PACK_COLS)
    dx, dg1 = _matmul("mm_in_dx", (nt, 1, 1), dz, z_tile, w_in, _tile2(D_MODEL, D_IN_PAD, lambda i, j, k: (0, 0)), (1, 1),
                      [(f32(t, D_MODEL), x_tile), (f32(1, D_MODEL), gain)],
                      extras=[(x, x_tile), (g1, gain), (g_mid, x_tile)], epilogue=norm_bwd, summed=1)
    norms = dict(norm1_g=dg1[0], norm2_g=dg2[0], mix_norm_g=dgm[0])
    return dx, norms, d_mix, d_w_in, send, reduced


HBM = pl.BlockSpec(memory_space=pltpu.HBM)
D2D_CHUNKS = 15
ICI_CHUNKS = 5
VMEM_CHUNKS = 4


def _coords():
    return lax.axis_index("x"), lax.axis_index("y"), lax.axis_index("c")


def _other_chips(x, y):
    return [(1 - x, y), (x, 1 - y), (1 - x, 1 - y)]


def _start_chunks(make, rows, n):
    size = rows // n
    assert size * n == rows
    for k in range(n):
        make(pl.ds(k * size, size)).start()


AG_SEMS = 7


ALL_ROWS = (0, LAYER_ROWS, ICI_CHUNKS, D2D_CHUNKS)
EARLY_ROWS = (SMALL_OFF, GLUE_ROWS, 4, 4)
LATE_ROWS = (0, SMALL_OFF, 4, 12)


def _allgather_copies(in_ref, out_ref, send_sems, recv_sems, part=ALL_ROWS):
    r0, r, ici_chunks, d2d_chunks = part
    rh = r // 2
    x, y, c = _coords()
    me, sibling = (x, y, c), (x, y, 1 - c)
    chips = _other_chips(x, y)

    def half(px, py, pc, rows=pl.ds(0, rh)):
        return out_ref.at[2 * px + py, pl.ds(r0 + pc * rh + rows.start, rows.size), :]

    def copy(k, block, to, rows=pl.ds(0, rh), from_input=False):
        src = in_ref.at[pl.ds(r0 + block[2] * rh + rows.start, rows.size), :] if from_input else half(*block, rows)
        return pltpu.make_async_remote_copy(
            src_ref=src, dst_ref=half(*block, rows), send_sem=send_sems.at[k], recv_sem=recv_sems.at[k],
            device_id=to, device_id_type=MESH)

    def own(rows=pl.ds(0, r)):
        mine = pl.ds(r0 + rows.start, rows.size)
        return pltpu.make_async_remote_copy(
            src_ref=in_ref.at[mine, :], dst_ref=out_ref.at[2 * x + y, mine, :], send_sem=send_sems.at[6],
            recv_sem=recv_sems.at[6], device_id=sibling, device_id_type=MESH)

    def start():
        for j, chip in enumerate(chips):
            _start_chunks(lambda rows: copy(j, me, (*chip, c), rows, from_input=True), rh, ici_chunks)
        _start_chunks(own, r, d2d_chunks)

    def finish():
        for j, chip in enumerate(chips):
            copy(j, (*chip, c), me).wait_recv()
            _start_chunks(lambda rows: copy(3 + j, (*chip, c), sibling, rows), rh, d2d_chunks)
        for j, chip in enumerate(chips):
            copy(3 + j, (*chip, 1 - c), me).wait_recv()
        for j, chip in enumerate(chips):
            copy(j, me, (*chip, c), from_input=True).wait_send()
            copy(3 + j, (*chip, c), sibling).wait_send()
        own().wait()

    return start, finish


def _allgather_shards(shard, part):
    def body(in_ref, out_ref, send_sems, recv_sems):
        start, finish = _allgather_copies(in_ref, out_ref, send_sems, recv_sems, part)
        start()
        finish()

    return pl.pallas_call(
        body, name="allgather_shards", out_shape=jax.ShapeDtypeStruct((4,) + shard.shape, shard.dtype),
        in_specs=[HBM], out_specs=HBM,
        scratch_shapes=[pltpu.SemaphoreType.DMA((AG_SEMS,)), pltpu.SemaphoreType.DMA((AG_SEMS,))],
        compiler_params=pltpu.CompilerParams())(shard)


def _pair_exchange_copies(g_ref, recv_ref, send_sem, recv_sem):
    s, _, rh, _ = g_ref.shape
    x, y, c = _coords()

    def copy(slot, rows):
        return pltpu.make_async_remote_copy(
            src_ref=g_ref.at[slot, 1 - c, rows, :], dst_ref=recv_ref.at[slot, rows, :], send_sem=send_sem,
            recv_sem=recv_sem, device_id=(x, y, 1 - c), device_id_type=MESH)

    def start():
        for slot in range(s):
            _start_chunks(lambda rows: copy(slot, rows), rh, VMEM_CHUNKS)

    def finish():
        pltpu.make_async_remote_copy(
            src_ref=g_ref.at[:, 1 - c], dst_ref=recv_ref, send_sem=send_sem, recv_sem=recv_sem,
            device_id=(x, y, 1 - c), device_id_type=MESH).wait()

    return start, finish


def _pair_share_copies(in_ref, out_ref, send_sem, recv_sem):
    rh = in_ref.shape[1]
    x, y, c = _coords()

    def copy(slot, rows=pl.ds(0, rh)):
        return pltpu.make_async_remote_copy(
            src_ref=in_ref.at[slot, rows, :], dst_ref=out_ref.at[slot, rows, :], send_sem=send_sem,
            recv_sem=recv_sem, device_id=(x, y, 1 - c), device_id_type=MESH)

    def start():
        _start_chunks(lambda rows: copy(c, rows), rh, D2D_CHUNKS)

    def finish():
        copy(c).wait_send()
        copy(1 - c).wait_recv()

    return start, finish


def _pair_exchange(g):
    s, _, rh, cols = g.shape

    def body(g_ref, recv_ref, send_sem, recv_sem):
        start, finish = _pair_exchange_copies(g_ref, recv_ref, send_sem, recv_sem)
        start()
        finish()

    return pl.pallas_call(
        body, name="pair_exchange", out_shape=jax.ShapeDtypeStruct((s, rh, cols), g.dtype), in_specs=[HBM],
        out_specs=HBM, scratch_shapes=[pltpu.SemaphoreType.DMA] * 2, compiler_params=pltpu.CompilerParams())(g)


def _chip_exchange_copies(p_ref, recv_ref, send_sems, recv_sems):
    rh = p_ref.shape[1]
    x, y, c = _coords()
    chips = _other_chips(x, y)

    def copy(j, chip, rows=pl.ds(0, rh)):
        return pltpu.make_async_remote_copy(
            src_ref=p_ref.at[2 * chip[0] + chip[1], rows, :], dst_ref=recv_ref.at[j, rows, :],
            send_sem=send_sems.at[j], recv_sem=recv_sems.at[j], device_id=(*chip, c), device_id_type=MESH)

    def start():
        for j, chip in enumerate(chips):
            _start_chunks(lambda rows: copy(j, chip, rows), rh, ICI_CHUNKS)

    def finish():
        for j, chip in enumerate(chips):
            copy(j, chip).wait_recv()
        for j, chip in enumerate(chips):
            copy(j, chip).wait_send()

    return start, finish


def _sum_chips(p, recv, tr):
    _, rh, cols = p.shape
    x, y, c = _coords()
    one = lambda v: v.astype(jnp.int32).reshape(1)

    def body(chip_ref, c_ref, own_ref, r_ref, o_ref):
        acc = own_ref[...].astype(F32)
        for k in range(3):
            acc = acc + r_ref[k].astype(F32)
        o_ref[...] = acc

    return pl.pallas_call(
        body, name="sum_chips", out_shape=jax.ShapeDtypeStruct((2, rh, cols), F32),
        grid_spec=pltpu.PrefetchScalarGridSpec(
            num_scalar_prefetch=2, grid=(rh // tr,),
            in_specs=[pl.BlockSpec((None, tr, cols), lambda i, chip_ref, c_ref: (chip_ref[0], i, 0)),
                      pl.BlockSpec((3, tr, cols), lambda i, chip_ref, c_ref: (0, i, 0))],
            out_specs=pl.BlockSpec((None, tr, cols), lambda i, chip_ref, c_ref: (c_ref[0], i, 0))),
        compiler_params=_cp(("arbitrary",)))(one(2 * x + y), one(c), p, recv)


def _pair_share(buf):
    _, rh, cols = buf.shape

    def body(in_ref, out_ref, send_sem, recv_sem):
        start, finish = _pair_share_copies(in_ref, out_ref, send_sem, recv_sem)
        start()
        finish()

    return pl.pallas_call(
        body, name="pair_share", out_shape=jax.ShapeDtypeStruct(buf.shape, buf.dtype), in_specs=[HBM], out_specs=HBM,
        scratch_shapes=[pltpu.SemaphoreType.DMA] * 2, input_output_aliases={0: 0},
        compiler_params=pltpu.CompilerParams())(buf)


def _allgather_all(blk, exchange):
    m_per, cols = blk.shape
    whole = pl.ds(0, m_per)

    def body(x_ref, p_ref, out_ref, got_ref, send_sems, recv_sems, local_sem, p_send_sems, p_recv_sems):
        x, y, c = _coords()
        me, sibling = (x, y, c), (x, y, 1 - c)
        chips = _other_chips(x, y)
        start_exchange, finish_exchange = _chip_exchange_copies(p_ref, got_ref, p_send_sems, p_recv_sems)
        start_exchange()

        def rows_of(px, py, pc, rows):
            return out_ref.at[4 * px + 2 * py + pc, rows, :]

        def copy(k, block, to, rows=whole, from_input=False):
            return pltpu.make_async_remote_copy(
                src_ref=x_ref.at[rows, :] if from_input else rows_of(*block, rows), dst_ref=rows_of(*block, rows),
                send_sem=send_sems.at[k], recv_sem=recv_sems.at[k], device_id=to, device_id_type=MESH)

        mine = pltpu.make_async_copy(x_ref, rows_of(*me, whole), local_sem)
        mine.start()
        _start_chunks(lambda rows: copy(0, me, sibling, rows, from_input=True), m_per, VMEM_CHUNKS)
        for j, chip in enumerate(chips):
            _start_chunks(lambda rows: copy(1 + j, me, (*chip, c), rows, from_input=True), m_per, VMEM_CHUNKS)
        for j, chip in enumerate(chips):
            copy(1 + j, (*chip, c), me).wait_recv()
            _start_chunks(lambda rows: copy(4 + j, (*chip, c), sibling, rows), m_per, VMEM_CHUNKS)
        copy(0, sibling, me).wait_recv()
        for j, chip in enumerate(chips):
            copy(4 + j, (*chip, 1 - c), me).wait_recv()
        copy(0, me, sibling, from_input=True).wait_send()
        for j, chip in enumerate(chips):
            copy(1 + j, me, (*chip, c), from_input=True).wait_send()
            copy(4 + j, (*chip, c), sibling).wait_send()
        mine.wait()
        finish_exchange()

    vmem = pl.BlockSpec(memory_space=pltpu.VMEM)
    return pl.pallas_call(
        body, name="allgather_all",
        out_shape=[jax.ShapeDtypeStruct((8, m_per, cols), blk.dtype),
                   jax.ShapeDtypeStruct((3,) + exchange.shape[1:], exchange.dtype)],
        in_specs=[vmem, HBM], out_specs=[vmem, HBM],
        scratch_shapes=[pltpu.SemaphoreType.DMA((7,)), pltpu.SemaphoreType.DMA((7,)), pltpu.SemaphoreType.DMA,
                        pltpu.SemaphoreType.DMA((3,)), pltpu.SemaphoreType.DMA((3,))],
        compiler_params=pltpu.CompilerParams(vmem_limit_bytes=VMEM_LIMIT))(blk, exchange)


def _add_kept(g, recv, tr):
    s, _, rh, cols = g.shape

    def body(c_ref, a_ref, b_ref, o_ref):
        o_ref[...] = (a_ref[...].astype(F32) + b_ref[...].astype(F32)).astype(o_ref.dtype)

    spec = pl.BlockSpec((None, tr, cols), lambda si, i, c_ref: (si, i, 0))
    return pl.pallas_call(
        body, name="add_kept", out_shape=jax.ShapeDtypeStruct((s, rh, cols), BF16),
        grid_spec=pltpu.PrefetchScalarGridSpec(
            num_scalar_prefetch=1, grid=(s, rh // tr),
            in_specs=[pl.BlockSpec((None, None, tr, cols), lambda si, i, c_ref: (si, c_ref[0], i, 0)), spec],
            out_specs=spec),
        compiler_params=_cp(("arbitrary", "arbitrary")))(lax.axis_index("c").astype(jnp.int32).reshape(1), g, recv)


def _sum_slots(p, tr, name):
    s, rows, cols = p.shape

    def body(p_ref, o_ref):
        acc = p_ref[0].astype(F32)
        for k in range(1, s):
            acc = acc + p_ref[k].astype(F32)
        o_ref[...] = acc

    return pl.pallas_call(
        body, name=name, grid=(rows // tr,), in_specs=[pl.BlockSpec((s, tr, cols), lambda i: (0, i, 0))],
        out_specs=_row_spec(tr, cols), out_shape=jax.ShapeDtypeStruct((rows, cols), F32),
        compiler_params=_cp(("arbitrary",)))(p)


def _adamw_call(w, g, m, v, name):
    rows, cols = w.shape
    tr = _tile(rows, 512) if rows % 512 == 0 else _tile(rows, 128)
    c1 = 1.0 - ADAM_B1 ** ADAM_STEP
    c2 = 1.0 - ADAM_B2 ** ADAM_STEP

    def body(w_ref, g_ref, m_ref, v_ref, d_ref, nm_ref, nv_ref):
        gv = g_ref[...]
        nm = ADAM_B1 * m_ref[...] + (1.0 - ADAM_B1) * gv
        nv = ADAM_B2 * v_ref[...] + (1.0 - ADAM_B2) * jnp.square(gv)
        d_ref[...] = -ADAM_LR * ((nm / c1) / (jnp.sqrt(nv / c2) + ADAM_EPS) + ADAM_WD * w_ref[...])
        nm_ref[...] = nm
        nv_ref[...] = nv

    spec = _row_spec(tr, cols)
    o = jax.ShapeDtypeStruct((rows, cols), F32)
    return pl.pallas_call(body, name=name, grid=(rows // tr,), in_specs=[spec] * 4, out_specs=[spec] * 3,
                          out_shape=[o, o, o], compiler_params=_cp(("arbitrary",)))(w, g, m, v)


WEIGHTS = ('norm1_g', 'w_in', 'sgu_norm_g', 'sgu_w', 'sgu_b', 's5_lambda_re', 's5_lambda_im', 's5_log_dt',
           's5_b_re', 's5_b_im', 's5_c_re', 's5_c_im', 's5_d', 's5_glu_w', 's5_glu_b', 'lru_conv_w',
           'lru_conv_b', 'lru_wa', 'lru_ba', 'lru_wx', 'lru_bx', 'lru_lambda', 'fox_fgate_b', 'mix_norm_g',
           'w_out', 'norm2_g', 'w_mlp_in', 'w_mlp_out', 'final_g')
N_W = len(WEIGHTS)


def _pack_shards(shards, dtype, names=tuple(SHARD_SHAPE), rows=LAYER_ROWS):
    parts = []
    for n in names:
        lead = shards[n].shape[:-2]
        flat = shards[n].reshape(*lead, -1).astype(dtype)
        flat = jnp.pad(flat, [(0, 0)] * len(lead) + [(0, SHARD_ROWS[n] * PACK_COLS - flat.shape[-1])])
        parts.append(flat.reshape(*lead, SHARD_ROWS[n], PACK_COLS))
    lead = parts[0].shape[:-2]
    used = sum(SHARD_ROWS[n] for n in names)
    if rows > used:
        parts.append(jnp.zeros((*lead, rows - used, PACK_COLS), dtype))
    return jnp.concatenate(parts, axis=-2)


def _unpack_shards(buf, names=tuple(SHARD_SHAPE)):
    lead = buf.shape[:-2]
    out = {}
    for n in names:
        s0, s1 = SHARD_SHAPE[n]
        rows, off = SHARD_ROWS[n], SHARD_OFF[n]
        flat = buf[..., off:off + rows, :].reshape(*lead, rows * PACK_COLS)
        out[n] = flat[..., :s0 * s1].reshape(*lead, s0, s1)
    return out


def _join_chips(g, axis):
    _, d, s0, s1 = g.shape
    if axis == 0:
        return g.transpose(1, 0, 2, 3).reshape(d, 4 * s0, s1)
    return g.transpose(1, 2, 0, 3).reshape(d, s0, 4 * s1)


def _split_chips(w, axis):
    d = w.shape[0]
    if axis == 0:
        return w.reshape(d, 4, w.shape[1] // 4, w.shape[2]).transpose(1, 0, 2, 3)
    return w.reshape(d, w.shape[1], 4, w.shape[2] // 4).transpose(2, 0, 1, 3)


def _flat_rows(shape):
    return -(-math.prod(shape) // PACK_COLS)


def _pack_flat(arrs, rows, dtype=F32):
    parts = []
    for a in arrs:
        flat = a.reshape(-1).astype(dtype)
        r = _flat_rows(a.shape)
        parts.append(jnp.pad(flat, (0, r * PACK_COLS - flat.shape[0])).reshape(r, PACK_COLS))
    used = sum(p.shape[0] for p in parts)
    parts.append(jnp.zeros((rows - used, PACK_COLS), dtype))
    return jnp.concatenate(parts, axis=0)


def _split3(s):
    hi = s.astype(BF16).astype(F32)
    mid = (s - hi).astype(BF16).astype(F32)
    return jnp.stack([hi, mid, s - hi - mid])


def _unpack_flat(buf, shapes):
    out, off = [], 0
    for s in shapes:
        r = _flat_rows(s)
        out.append(buf[off:off + r].reshape(-1)[:math.prod(s)].reshape(s))
        off += r
    return out


def _write_rows(buf, src, row_offset):
    n, r, cols = src.shape

    def body(s_ref, b_ref, o_ref):
        o_ref[...] = s_ref[...].astype(o_ref.dtype)

    blk = (None, r, cols)
    return pl.pallas_call(
        body, name="write_rows", grid=(n,),
        in_specs=[pl.BlockSpec(blk, lambda s: (s, 0, 0)), pl.BlockSpec(memory_space=pl.ANY)],
        out_specs=pl.BlockSpec(blk, lambda s: (s, row_offset // r, 0)),
        out_shape=jax.ShapeDtypeStruct(buf.shape, buf.dtype), input_output_aliases={1: 0},
        compiler_params=_cp(("arbitrary",)))(src, buf)


def _write_glue(send, glue):
    return _write_rows(send, glue, SMALL_OFF)


GLUE_PACKED = ('w_in', 's5_glu_w', 'lru_conv_w')
REDUCE_ROWS = LAYER_ROWS // 4


def _pack_weights(w):
    buf = lax.empty((DEPTH, LAYER_ROWS, PACK_COLS), BF16)
    for n in ('w_mlp_in', 'w_mlp_out', 'w_out'):
        buf = _write_rows(buf, w[n], SHARD_OFF[n])
    return _write_rows(buf, _pack_shards(w, BF16, names=GLUE_PACKED, rows=GLUE_ROWS), SMALL_OFF)


def _layer_weights(gathered):
    parts = _unpack_shards(gathered[:, None], GLUE_PACKED)
    joined = {n: _join_chips(g, SHARDED_AXIS[n])[0] for n, g in parts.items()}
    joined['w_in'] = jnp.pad(joined['w_in'], ((0, 0), (0, D_IN_PAD - D_IN_PROJ)))
    return joined


def _reduce_start(send):
    halves = send.reshape(4, 2, LAYER_ROWS // 2, PACK_COLS)
    return _add_kept(halves, _pair_exchange(halves), REDUCE_ROWS)


def _reduce_finish(chip_sum, received):
    return _pair_share(_sum_chips(chip_sum, received, REDUCE_ROWS)).reshape(LAYER_ROWS, PACK_COLS)


def _forward_backward(x, target, final_g, shards, rep):
    norm_p = [{n: rep[n][l] for n in ('norm1_g', 'norm2_g', 'mix_norm_g')} for l in range(DEPTH)]
    prepared, prepare_vjp = jax.vjp(_prepare, {n: rep[n] for n in PREPARED_FROM})
    gathered = _allgather_shards(shards[0], EARLY_ROWS)
    (h,) = _rowwise(_f_rms, [x], [norm_p[0]["norm1_g"][None, :]], [D_MODEL], name="rms_f", tr=512, dtype=BF16)
    layers = []
    for l in range(DEPTH):
        lw = _layer_weights(gathered)
        mix_p = ({n: a[l] for n, a in prepared.items()}, lw['s5_glu_w'].astype(F32), lw['lru_conv_w'].astype(F32))
        last = l + 1 == DEPTH
        x, h, res, gathered = _layer_fwd(
            x, h, gathered, lw['w_in'], norm_p[l], mix_p, None if last else shards[l + 1],
            shards[l], None if last else norm_p[l + 1]["norm1_g"])
        layers.append((res, lw))
    loss_part, g, d_final = _loss_call(x, final_g[None, :], target)

    norms, d_prepared, reduced, later_send = [None] * DEPTH, [None] * DEPTH, [None] * DEPTH, None
    for l in reversed(range(DEPTH)):
        res, lw = layers[l]
        g, norms[l], (d_prepared[l], d_glu_w, d_conv_w), d_w_in, send, later_reduced = _layer_bwd(
            g, res, lw['w_in'], norm_p[l], later_send)
        if later_send is not None:
            reduced[l + 1] = later_reduced
        mine = {'w_in': d_w_in[:, :D_IN_PROJ], 's5_glu_w': d_glu_w, 'lru_conv_w': d_conv_w}
        glue = _pack_shards({n: _split_chips(a[None], SHARDED_AXIS[n]) for n, a in mine.items()}, BF16,
                            names=GLUE_PACKED, rows=GLUE_ROWS)[:, 0]
        later_send = _write_glue(send, glue)
    stack = lambda per_layer: {n: jnp.stack([per_layer[l][n] for l in range(DEPTH)]) for n in per_layer[0]}
    (d_rep,) = prepare_vjp(stack(d_prepared))
    return loss_part, g, d_final, dict(d_rep, **stack(norms)), reduced[1:], _reduce_start(later_send)


def _step(*args):
    x, target = args[0], args[1 + N_W]
    w = dict(zip(WEIGHTS, args[1:1 + N_W]))
    m = dict(zip(WEIGHTS, args[2 + N_W:2 + 2 * N_W]))
    v = dict(zip(WEIGHTS, args[2 + 2 * N_W:2 + 3 * N_W]))
    small = [n for n in WEIGHTS if n not in SHARD_SHAPE]

    shards = _pack_weights({n: w[n] for n in SHARD_SHAPE})
    loss_part, dx, d_final, dw, reduced_above, chip_sum0 = _forward_backward(
        x[0], target[0], w['final_g'], shards, {n: w[n] for n in small})

    small_g = [d_final.reshape(-1) if n == 'final_g' else dw[n] for n in small]
    small_rows = -(-(sum(_flat_rows(w[n].shape) for n in small) + 1) // 128) * 128
    mine = _pack_flat(small_g + [_split3(loss_part[0, 0])], small_rows, BF16)
    everyone, received0 = _allgather_all(mine, chip_sum0)
    g_shard = _unpack_shards(jnp.stack([_reduce_finish(chip_sum0, received0)] + reduced_above))
    small_sum = _sum_slots(everyone, 128, "sum_devices")
    *g_small, loss = _unpack_flat(small_sum, [w[n].shape for n in small] + [(3,)])
    loss = jnp.sum(loss)

    grads, delta, new_m, new_v = {}, {}, {}, {}
    for n in SHARD_SHAPE:
        shp = w[n].shape
        v2 = lambda a: a.reshape(-1, shp[-1])
        res = _adamw_call(v2(w[n]), v2(g_shard[n]), v2(m[n]), v2(v[n]), "adamw_" + n)
        grads[n] = g_shard[n]
        delta[n], new_m[n], new_v[n] = (r.reshape(shp) for r in res)
    for n, g in zip(small, g_small):
        shp = w[n].shape
        v2 = lambda a: a.reshape(-1, shp[-1])
        res = _adamw_call(v2(w[n]), v2(g), v2(m[n]), v2(v[n]), "adamw_" + n)
        grads[n] = g
        delta[n], new_m[n], new_v[n] = (r.reshape(shp) for r in res)

    return (loss, dx[None], *[grads[n] for n in WEIGHTS], *[delta[n] for n in WEIGHTS],
            *[new_m[n] for n in WEIGHTS], *[new_v[n] for n in WEIGHTS])


def kernel(x, norm1_g, w_in, sgu_norm_g, sgu_w, sgu_b, s5_lambda_re, s5_lambda_im, s5_log_dt, s5_b_re, s5_b_im, s5_c_re, s5_c_im, s5_d, s5_glu_w, s5_glu_b, lru_conv_w, lru_conv_b, lru_wa, lru_ba, lru_wx, lru_bx, lru_lambda, fox_fgate_b, mix_norm_g, w_out, norm2_g, w_mlp_in, w_mlp_out, final_g, loss_target, m_norm1_g, m_w_in, m_sgu_norm_g, m_sgu_w, m_sgu_b, m_s5_lambda_re, m_s5_lambda_im, m_s5_log_dt, m_s5_b_re, m_s5_b_im, m_s5_c_re, m_s5_c_im, m_s5_d, m_s5_glu_w, m_s5_glu_b, m_lru_conv_w, m_lru_conv_b, m_lru_wa, m_lru_ba, m_lru_wx, m_lru_bx, m_lru_lambda, m_fox_fgate_b, m_mix_norm_g, m_w_out, m_norm2_g, m_w_mlp_in, m_w_mlp_out, m_final_g, v_norm1_g, v_w_in, v_sgu_norm_g, v_sgu_w, v_sgu_b, v_s5_lambda_re, v_s5_lambda_im, v_s5_log_dt, v_s5_b_re, v_s5_b_im, v_s5_c_re, v_s5_c_im, v_s5_d, v_s5_glu_w, v_s5_glu_b, v_lru_conv_w, v_lru_conv_b, v_lru_wa, v_lru_ba, v_lru_wx, v_lru_bx, v_lru_lambda, v_fox_fgate_b, v_mix_norm_g, v_w_out, v_norm2_g, v_w_mlp_in, v_w_mlp_out, v_final_g):
    return _step(x, norm1_g, w_in, sgu_norm_g, sgu_w, sgu_b, s5_lambda_re, s5_lambda_im, s5_log_dt, s5_b_re, s5_b_im, s5_c_re, s5_c_im, s5_d, s5_glu_w, s5_glu_b, lru_conv_w, lru_conv_b, lru_wa, lru_ba, lru_wx, lru_bx, lru_lambda, fox_fgate_b, mix_norm_g, w_out, norm2_g, w_mlp_in, w_mlp_out, final_g, loss_target, m_norm1_g, m_w_in, m_sgu_norm_g, m_sgu_w, m_sgu_b, m_s5_lambda_re, m_s5_lambda_im, m_s5_log_dt, m_s5_b_re, m_s5_b_im, m_s5_c_re, m_s5_c_im, m_s5_d, m_s5_glu_w, m_s5_glu_b, m_lru_conv_w, m_lru_conv_b, m_lru_wa, m_lru_ba, m_lru_wx, m_lru_bx, m_lru_lambda, m_fox_fgate_b, m_mix_norm_g, m_w_out, m_norm2_g, m_w_mlp_in, m_w_mlp_out, m_final_g, v_norm1_g, v_w_in, v_sgu_norm_g, v_sgu_w, v_sgu_b, v_s5_lambda_re, v_s5_lambda_im, v_s5_log_dt, v_s5_b_re, v_s5_b_im, v_s5_c_re, v_s5_c_im, v_s5_d, v_s5_glu_w, v_s5_glu_b, v_lru_conv_w, v_lru_conv_b, v_lru_wa, v_lru_ba, v_lru_wx, v_lru_bx, v_lru_lambda, v_fox_fgate_b, v_mix_norm_g, v_w_out, v_norm2_g, v_w_mlp_in, v_w_mlp_out, v_final_g)
```

```python
import math

import jax
import jax.numpy as jnp
from jax import lax
from jax.experimental import pallas as pl
from jax.experimental.pallas import tpu as pltpu

F32 = jnp.float32
BF16 = jnp.bfloat16

DEPTH = 4
D_MODEL = 1024
MIXER_WIDTH = 256
SGU_CHUNK = 128
N_HEADS = 4
HEAD_DIM = 64
S5_GROUPS = 16
S5_GROUP = 16
S5_STATE = 64
LRU_C = 8.0
RMS_EPS = 1e-6
D_IN_PROJ = 8 * MIXER_WIDTH + N_HEADS
D_IN_PAD = 8 * MIXER_WIDTH + 128
ADAM_LR, ADAM_B1, ADAM_B2, ADAM_EPS, ADAM_WD, ADAM_STEP = 0.001, 0.9, 0.999, 1e-08, 0.01, 10

V7X_VMEM_BYTES = 64 * 1024 * 1024
VMEM_LIMIT = V7X_VMEM_BYTES - 8 * 1024 * 1024
NEG = -1e30
MESH = pl.DeviceIdType.MESH


def _cp(sem=None, **kw):
    return pltpu.CompilerParams(dimension_semantics=sem, vmem_limit_bytes=VMEM_LIMIT, **kw)


def _full_spec(a):
    nd = a.ndim
    return pl.BlockSpec(a.shape, lambda *_: (0,) * nd)


def _tile(n, pref=512):
    return pref if n % pref == 0 else n


def _dot(a, b, ca, cb):
    return lax.dot_general(a.astype(BF16), b.astype(BF16), (((ca,), (cb,)), ((), ())),
                           preferred_element_type=F32)


def _matmul(name, grid, a, a_spec, b, b_spec, dims, outs, *, extras=(), epilogue=None, into=None, summed=0,
            acc_shape=None, rider=None):
    nk = grid[2]
    n_ex, n_out = len(extras), len(outs)
    tm_tn = acc_shape or tuple(d for d in outs[0][1].block_shape if d is not None)[-2:]
    n_in = 2 + n_ex + (into is not None)

    def body(*refs):
        a_ref, b_ref = refs[0], refs[1]
        ex_refs = refs[2:2 + n_ex]
        o_refs = refs[n_in + (rider is not None):n_in + (rider is not None) + n_out]
        if rider is not None:
            start, finish_rider = rider[3](refs[n_in], refs[n_in + 1 + n_out], refs[-2], refs[-1])
            step = [pl.program_id(d) for d in range(3)]
            pl.when((step[0] == 0) & (step[1] == 0) & (step[2] == 0))(start)
        if summed:
            @pl.when((pl.program_id(0) == 0) & (pl.program_id(1) == 0) & (pl.program_id(2) == 0))
            def _():
                for o_ref in o_refs[n_out - summed:]:
                    o_ref[...] = jnp.zeros_like(o_ref)

        def finish(val):
            res = epilogue(val, *[e[...] for e in ex_refs]) if epilogue else (val,)
            for idx, (o_ref, r) in enumerate(zip(o_refs, res)):
                if idx >= n_out - summed:
                    o_ref[...] += r
                else:
                    o_ref[...] = r.astype(o_ref.dtype)

        if nk == 1:
            finish(_dot(a_ref[...], b_ref[...], *dims))
        else:
            acc = refs[n_in + (rider is not None) + n_out + (rider is not None)]
            kk = pl.program_id(2)

            @pl.when(kk == 0)
            def _():
                acc[...] = jnp.zeros_like(acc)

            acc[...] += _dot(a_ref[...], b_ref[...], *dims)

            @pl.when(kk == nk - 1)
            def _():
                finish(acc[...])

        if rider is not None:
            pl.when((step[0] == grid[0] - 1) & (step[1] == grid[1] - 1) & (step[2] == grid[2] - 1))(finish_rider)

    ins = [a, b] + [e[0] for e in extras]
    specs = [a_spec, b_spec] + [e[1] for e in extras]
    aliases = {}
    if into is not None:
        aliases = {len(ins): 0}
        ins.append(into)
        specs.append(pl.BlockSpec(memory_space=pl.ANY))
    out_specs, out_shape = [o[1] for o in outs], [o[0] for o in outs]
    scratch = [pltpu.VMEM(tm_tn, F32)] if nk > 1 else []
    if rider is not None:
        if rider[2]:
            aliases[len(ins)] = n_out
        ins.append(rider[0])
        specs.append(HBM)
        out_specs.append(HBM)
        out_shape.append(rider[1])
        scratch += [pltpu.SemaphoreType.DMA] * 2
        name += "_rider"
    res = pl.pallas_call(
        body, name=name, grid=grid, in_specs=specs, out_specs=out_specs, out_shape=out_shape,
        scratch_shapes=scratch, input_output_aliases=aliases,
        compiler_params=_cp(("arbitrary", "arbitrary", "arbitrary")))(*ins)
    return res[0] if len(res) == 1 else res


@jax.custom_vjp
def _bdot(a, b):
    return _dot(a, b, 1, 0)


def _bdot_fwd(a, b):
    return _dot(a, b, 1, 0), (a, b)


def _bdot_bwd(r, g):
    a, b = r
    return _dot(g, b, 1, 1), _dot(a, g, 0, 0)


_bdot.defvjp(_bdot_fwd, _bdot_bwd)


def _row_spec(tr, w):
    return pl.BlockSpec((tr, w), lambda i: (i, 0))


def _rowwise(fn, rows, pars, outs, *, name, tr, dtype=F32):
    t = rows[0].shape[0]
    n_in = len(rows) + len(pars)

    def body(*refs):
        res = fn(*[r[...] for r in refs[:n_in]])
        for o_ref, v in zip(refs[n_in:], res):
            o_ref[...] = v.astype(o_ref.dtype)

    return pl.pallas_call(
        body, name=name, grid=(t // tr,),
        in_specs=[_row_spec(tr, r.shape[1]) for r in rows] + [_full_spec(p) for p in pars],
        out_specs=[_row_spec(tr, w) for w in outs],
        out_shape=[jax.ShapeDtypeStruct((t, w), dtype) for w in outs],
        compiler_params=_cp(("arbitrary",)))(*rows, *pars)


def _rowwise_vjp(fn, rows, pars, cots, *, name, tr, add=None):
    t = rows[0].shape[0]
    nr, npar = len(rows), len(pars)
    cots = list(cots) + ([add] if add is not None else [])
    nc = len(cots)

    def body(*refs):
        vals = [r[...] for r in refs[:nr + npar]]
        cts = [c[...] for c in refs[nr + npar:nr + npar + nc]]
        douts = refs[nr + npar + nc:]
        extra = cts.pop() if add is not None else None
        _, vjp = jax.vjp(fn, *vals)
        grads = list(vjp(tuple(cts)))
        if extra is not None:
            grads[0] = grads[0] + extra
        for kk in range(nr):
            douts[kk][...] = grads[kk]

        @pl.when(pl.program_id(0) == 0)
        def _():
            for kk in range(npar):
                douts[nr + kk][...] = jnp.zeros_like(douts[nr + kk])

        for kk in range(npar):
            douts[nr + kk][...] += grads[nr + kk]

    return pl.pallas_call(
        body, name=name, grid=(t // tr,),
        in_specs=[_row_spec(tr, r.shape[1]) for r in rows] + [_full_spec(p) for p in pars]
        + [_row_spec(tr, c.shape[1]) for c in cots],
        out_specs=[_row_spec(tr, r.shape[1]) for r in rows] + [_full_spec(p) for p in pars],
        out_shape=[jax.ShapeDtypeStruct(r.shape, F32) for r in rows]
        + [jax.ShapeDtypeStruct(p.shape, F32) for p in pars],
        compiler_params=_cp(("arbitrary",)))(*rows, *pars, *cots)


def _make_rw(fn, name, tr, nr, outs):
    @jax.custom_vjp
    def f(*args):
        return tuple(_rowwise(fn, args[:nr], args[nr:], outs, name=name + "_f", tr=tr))

    def fwd(*args):
        return f(*args), args

    def bwd(args, cts):
        return tuple(_rowwise_vjp(fn, args[:nr], args[nr:], list(cts), name=name + "_b", tr=tr))

    f.defvjp(fwd, bwd)
    return f


def _rms(x, g):
    return x * lax.rsqrt(jnp.mean(jnp.square(x), axis=-1, keepdims=True) + RMS_EPS) * g


def _f_rms(x, g):
    return (_rms(x, g),)


def _f_sgu(au, av, ng, w0, w1, w2, w3, bfull):
    u = jax.nn.gelu(au)
    v = _rms(jax.nn.gelu(av), ng)
    tri = lax.broadcasted_iota(jnp.int32, (SGU_CHUNK, SGU_CHUNK), 0) >= lax.broadcasted_iota(
        jnp.int32, (SGU_CHUNK, SGU_CHUNK), 1)
    head = lax.broadcasted_iota(jnp.int32, v.shape, 1) // HEAD_DIM
    mixed = bfull
    for h, w in enumerate((w0, w1, w2, w3)):
        mixed = mixed + _bdot(jnp.where(tri, w, 0.0), jnp.where(head == h, v, 0.0))
    return (u * mixed,)


def _f_s5disc(lam_re, lam_im, log_dt, b_re, b_im):
    dt = jnp.exp(log_dt)
    mag = jnp.exp(lam_re * dt)
    abar_re = mag * jnp.cos(lam_im * dt)
    abar_im = mag * jnp.sin(lam_im * dt)
    denom = jnp.square(lam_re) + jnp.square(lam_im)
    num_re = abar_re - 1.0
    num_im = abar_im
    fac_re = (num_re * lam_re + num_im * lam_im) / denom
    fac_im = (num_im * lam_re - num_re * lam_im) / denom
    return abar_re, abar_im, fac_re * b_re - fac_im * b_im, fac_re * b_im + fac_im * b_re


def _f_s5post(s_re, s_im, u, c_re, c_im, d, gw, gb):
    y = _bdot(s_re, c_re) - _bdot(s_im, c_im) + d * u
    y = jax.nn.gelu(y)
    return (y * jax.nn.sigmoid(_bdot(y, gw) + gb),)


def _f_lrupre(xc, wa, ba, wx, bx, lam):
    r = jax.nn.sigmoid(_bdot(xc, wa) + ba)
    i = jax.nn.sigmoid(_bdot(xc, wx) + bx)
    log_a = -LRU_C * r * jax.nn.softplus(-lam)
    a = jnp.exp(log_a)
    one_minus_a2 = -jnp.tanh(log_a) * (jnp.exp(2.0 * log_a) + 1.0)
    return a, jnp.sqrt(one_minus_a2) * (i * xc)


def _f_lrupost(h, gate):
    return (h * jax.nn.gelu(gate),)


def _f_logsig(zf, bf):
    return (jax.nn.log_sigmoid(zf + bf),)


def _f_gnorm(ya, yb, yc, yd, g):
    def n(y):
        return y * lax.rsqrt(jnp.mean(jnp.square(y), axis=-1, keepdims=True) + RMS_EPS)
    return (jnp.concatenate([n(ya), n(yb), n(yc), n(yd)], axis=1) * g,)


sgu_mix = _make_rw(_f_sgu, "sgu", SGU_CHUNK, 2, [MIXER_WIDTH])
s5_disc = _make_rw(_f_s5disc, "s5disc", S5_GROUPS * S5_GROUP, 5, [S5_STATE] * 4)
s5_post = _make_rw(_f_s5post, "s5post", 256, 3, [MIXER_WIDTH])
lru_pre = _make_rw(_f_lrupre, "lrupre", 512, 1, [MIXER_WIDTH, MIXER_WIDTH])
lru_post = _make_rw(_f_lrupost, "lrupost", 512, 2, [MIXER_WIDTH])
log_sig = _make_rw(_f_logsig, "logsig", 512, 1, [128])


SCAN_TILE = 512
TV_TILE = 1024


def _prev_spec(c, nt, rev, tile=SCAN_TILE):
    per = tile // 8
    if rev:
        return pl.BlockSpec((8, c), lambda i: (jnp.maximum((nt - 1 - i) * per - 1, 0), 0))
    return pl.BlockSpec((8, c), lambda i: (jnp.maximum(i * per - 1, 0), 0))


SCAN_STEPS = (1, 2, 4)


def _cmul(ar, ai, br, bi):
    return ar * br - ai * bi, ar * bi + ai * br


def _rows_down(x, k, fill, rowid):
    return jnp.where(rowid >= k, pltpu.roll(x, k, 0), fill)


def _rows_up(x, k, fill, rowid):
    return jnp.where(rowid < 8 - k, pltpu.roll(x, 8 - k, 0), fill)


def _powers(ar, ai):
    pw = [(ar, ai)]
    for _ in range(7):
        pw.append(_cmul(*pw[-1], ar, ai))
    return pw


def _block(i):
    return pl.ds(pl.multiple_of(i * 8, 8), 8)


def _row_before(ref, i, edge):
    return jnp.where(i == 0, edge, ref[pl.ds(jnp.maximum(i * 8 - 1, 0), 1), :])


def _lti_fwd_call(u, w_re, w_im, a_re, a_im):
    t, kdim = u.shape
    c = w_re.shape[1]
    tt = SCAN_TILE

    def body(u_ref, wr_ref, wi_ref, ar_ref, ai_ref, sr_ref, si_ref, br_ref, bi_ref, cr, ci):
        @pl.when(pl.program_id(0) == 0)
        def _():
            cr[...] = jnp.zeros_like(cr)
            ci[...] = jnp.zeros_like(ci)

        br_ref[...] = _dot(u_ref[...], wr_ref[...], 1, 0)
        bi_ref[...] = _dot(u_ref[...], wi_ref[...], 1, 0)
        pw = _powers(ar_ref[...], ai_ref[...])
        apr = jnp.concatenate([p[0] for p in pw], axis=0)
        api = jnp.concatenate([p[1] for p in pw], axis=0)
        rowid = lax.broadcasted_iota(jnp.int32, (8, c), 0)

        def block(i, carry):
            xr, xi = br_ref[_block(i), :], bi_ref[_block(i), :]
            for k in SCAN_STEPS:
                dr, di = _cmul(*pw[k - 1], _rows_down(xr, k, 0.0, rowid), _rows_down(xi, k, 0.0, rowid))
                xr, xi = xr + dr, xi + di
            dr, di = _cmul(apr, api, *carry)
            xr, xi = xr + dr, xi + di
            sr_ref[_block(i), :] = xr
            si_ref[_block(i), :] = xi
            return xr[7:8, :], xi[7:8, :]

        hr, hi = lax.fori_loop(0, tt // 8, block, (cr[...], ci[...]), unroll=2)
        cr[...] = hr
        ci[...] = hi

    row = pl.BlockSpec((tt, c), lambda i: (i, 0))
    par = pl.BlockSpec((1, c), lambda i: (0, 0))
    return pl.pallas_call(
        body, name="lti_scan_f", grid=(t // tt,),
        in_specs=[pl.BlockSpec((tt, kdim), lambda i: (i, 0)), _full_spec(w_re), _full_spec(w_im), par, par],
        out_specs=[row, row], out_shape=[jax.ShapeDtypeStruct((t, c), F32)] * 2,
        scratch_shapes=[pltpu.VMEM((tt, c), F32)] * 2 + [pltpu.VMEM((1, c), F32)] * 2,
        compiler_params=_cp(("arbitrary",)))(u, w_re, w_im, a_re, a_im)


def _lti_bwd_call(u, w_re, w_im, a_re, a_im, s_re, s_im, g_re, g_im):
    t, c = g_re.shape
    kdim = u.shape[1]
    tt = SCAN_TILE
    nt = t // tt
    nb = tt // 8

    def body(u_ref, wr_ref, wi_ref, ar_ref, ai_ref, sr_ref, si_ref, pr_ref, pi_ref, gr_ref, gi_ref,
             du_ref, dwr_ref, dwi_ref, dar_ref, dai_ref, or_ref, oi_ref, cr, ci):
        ti = pl.program_id(0)

        @pl.when(ti == 0)
        def _():
            cr[...] = jnp.zeros_like(cr)
            ci[...] = jnp.zeros_like(ci)
            dar_ref[...] = jnp.zeros_like(dar_ref)
            dai_ref[...] = jnp.zeros_like(dai_ref)
            dwr_ref[...] = jnp.zeros_like(dwr_ref)
            dwi_ref[...] = jnp.zeros_like(dwi_ref)

        pw = _powers(ar_ref[...], -ai_ref[...])
        tpr = jnp.concatenate([p[0] for p in reversed(pw)], axis=0)
        tpi = jnp.concatenate([p[1] for p in reversed(pw)], axis=0)
        rowid = lax.broadcasted_iota(jnp.int32, (8, c), 0)
        first = ti == nt - 1
        edge_r = jnp.where(first, 0.0, pr_ref[7:8, :])
        edge_i = jnp.where(first, 0.0, pi_ref[7:8, :])

        def block(kk, carry):
            i = nb - 1 - kk
            gr_c, gi_c, acc_r, acc_i = carry
            xr, xi = gr_ref[_block(i), :], gi_ref[_block(i), :]
            for k in SCAN_STEPS:
                dr, di = _cmul(*pw[k - 1], _rows_up(xr, k, 0.0, rowid), _rows_up(xi, k, 0.0, rowid))
                xr, xi = xr + dr, xi + di
            dr, di = _cmul(tpr, tpi, gr_c, gi_c)
            xr, xi = xr + dr, xi + di
            or_ref[_block(i), :] = xr
            oi_ref[_block(i), :] = xi
            spr = _rows_down(sr_ref[_block(i), :], 1, _row_before(sr_ref, i, edge_r), rowid)
            spi = _rows_down(si_ref[_block(i), :], 1, _row_before(si_ref, i, edge_i), rowid)
            return xr[0:1, :], xi[0:1, :], acc_r + spr * xr + spi * xi, acc_i + spr * xi - spi * xr

        zero = jnp.zeros((8, c), F32)
        gr_c, gi_c, acc_r, acc_i = lax.fori_loop(0, nb, block, (cr[...], ci[...], zero, zero), unroll=2)
        cr[...] = gr_c
        ci[...] = gi_c
        dar_ref[...] += jnp.sum(acc_r, axis=0, keepdims=True)
        dai_ref[...] += jnp.sum(acc_i, axis=0, keepdims=True)
        du_ref[...] = _dot(or_ref[...], wr_ref[...], 1, 1) + _dot(oi_ref[...], wi_ref[...], 1, 1)
        dwr_ref[...] += _dot(u_ref[...], or_ref[...], 0, 0)
        dwi_ref[...] += _dot(u_ref[...], oi_ref[...], 0, 0)

    row = pl.BlockSpec((tt, c), lambda i: (nt - 1 - i, 0))
    row_u = pl.BlockSpec((tt, kdim), lambda i: (nt - 1 - i, 0))
    par = pl.BlockSpec((1, c), lambda i: (0, 0))
    prev = _prev_spec(c, nt, True)
    return pl.pallas_call(
        body, name="lti_scan_b", grid=(nt,),
        in_specs=[row_u, _full_spec(w_re), _full_spec(w_im), par, par, row, row, prev, prev, row, row],
        out_specs=[row_u, _full_spec(w_re), _full_spec(w_im), par, par],
        out_shape=[jax.ShapeDtypeStruct((t, kdim), F32), jax.ShapeDtypeStruct(w_re.shape, F32),
                   jax.ShapeDtypeStruct(w_im.shape, F32)] + [jax.ShapeDtypeStruct((1, c), F32)] * 2,
        scratch_shapes=[pltpu.VMEM((tt, c), F32)] * 2 + [pltpu.VMEM((1, c), F32)] * 2,
        compiler_params=_cp(("arbitrary",)))(u, w_re, w_im, a_re, a_im, s_re, s_im, s_re, s_im, g_re, g_im)


@jax.custom_vjp
def lti_scan(u, w_re, w_im, a_re, a_im):
    return tuple(_lti_fwd_call(u, w_re, w_im, a_re, a_im))


def _lti_scan_fwd(u, w_re, w_im, a_re, a_im):
    s_re, s_im = _lti_fwd_call(u, w_re, w_im, a_re, a_im)
    return (s_re, s_im), (u, w_re, w_im, a_re, a_im, s_re, s_im)


def _lti_scan_bwd(r, g):
    return tuple(_lti_bwd_call(*r, g[0], g[1]))


lti_scan.defvjp(_lti_scan_fwd, _lti_scan_bwd)


def _tv_fwd_call(a, b):
    t, c = b.shape
    tt = min(TV_TILE, t)

    def body(a_ref, b_ref, h_ref, ch):
        @pl.when(pl.program_id(0) == 0)
        def _():
            ch[...] = jnp.zeros_like(ch)

        rowid = lax.broadcasted_iota(jnp.int32, (8, c), 0)

        def block(i, h):
            ab, x = a_ref[_block(i), :], b_ref[_block(i), :]
            for k in SCAN_STEPS:
                x = x + ab * _rows_down(x, k, 0.0, rowid)
                ab = ab * _rows_down(ab, k, 1.0, rowid)
            x = x + ab * h
            h_ref[_block(i), :] = x
            return x[7:8, :]

        ch[...] = lax.fori_loop(0, tt // 8, block, ch[...], unroll=2)

    row = pl.BlockSpec((tt, c), lambda i: (i, 0))
    return pl.pallas_call(
        body, name="tv_scan_f", grid=(t // tt,), in_specs=[row, row], out_specs=row,
        out_shape=jax.ShapeDtypeStruct((t, c), F32), scratch_shapes=[pltpu.VMEM((1, c), F32)],
        compiler_params=_cp(("arbitrary",)))(a, b)


def _tv_bwd_call(a, h, g):
    t, c = g.shape
    tt = min(TV_TILE, t)
    nt = t // tt
    nb = tt // 8

    def body(a_ref, h_ref, p_ref, g_ref, da_ref, db_ref, cg, ca):
        ti = pl.program_id(0)

        @pl.when(ti == 0)
        def _():
            cg[...] = jnp.zeros_like(cg)
            ca[...] = jnp.zeros_like(ca)

        rowid = lax.broadcasted_iota(jnp.int32, (8, c), 0)
        edge = jnp.where(ti == nt - 1, 0.0, p_ref[7:8, :])

        def block(kk, carry):
            i = nb - 1 - kk
            gc, a_next = carry
            ab, x = a_ref[_block(i), :], g_ref[_block(i), :]
            cb = _rows_up(ab, 1, a_next, rowid)
            for k in SCAN_STEPS:
                x = x + cb * _rows_up(x, k, 0.0, rowid)
                cb = cb * _rows_up(cb, k, 1.0, rowid)
            x = x + cb * gc
            db_ref[_block(i), :] = x
            da_ref[_block(i), :] = x * _rows_down(h_ref[_block(i), :], 1, _row_before(h_ref, i, edge), rowid)
            return x[0:1, :], ab[0:1, :]

        gc, a_next = lax.fori_loop(0, nb, block, (cg[...], ca[...]), unroll=2)
        cg[...] = gc
        ca[...] = a_next

    row = pl.BlockSpec((tt, c), lambda i: (nt - 1 - i, 0))
    return pl.pallas_call(
        body, name="tv_scan_b", grid=(nt,), in_specs=[row, row, _prev_spec(c, nt, True, tt), row],
        out_specs=[row, row], out_shape=[jax.ShapeDtypeStruct((t, c), F32)] * 2,
        scratch_shapes=[pltpu.VMEM((1, c), F32)] * 2, compiler_params=_cp(("arbitrary",)))(a, h, h, g)


@jax.custom_vjp
def tv_scan(a, b):
    return _tv_fwd_call(a, b)


def _tv_scan_fwd(a, b):
    h = _tv_fwd_call(a, b)
    return h, (a, h)


def _tv_scan_bwd(r, g):
    a, h = r
    return tuple(_tv_bwd_call(a, h, g))


tv_scan.defvjp(_tv_scan_fwd, _tv_scan_bwd)


CONV_K = 4
CONV_ROWS = 512


def _conv_fwd_call(x, w, b):
    t, c = x.shape

    def body(x_ref, w_ref, b_ref, o_ref, xp):
        xp[0:8, :] = jnp.zeros((8, c), F32)
        xp[8:, :] = x_ref[...]
        for blk in range(t // CONV_ROWS):
            base = blk * CONV_ROWS
            acc = jnp.broadcast_to(b_ref[...], (CONV_ROWS, c))
            for kk in range(CONV_K):
                acc = acc + w_ref[kk:kk + 1, :] * xp[base + 5 + kk:base + 5 + kk + CONV_ROWS, :]
            o_ref[base:base + CONV_ROWS, :] = acc

    return pl.pallas_call(
        body, name="conv_f", out_shape=jax.ShapeDtypeStruct((t, c), F32),
        scratch_shapes=[pltpu.VMEM((t + 8, c), F32)], compiler_params=_cp())(x, w, b)


def _conv_bwd_call(x, w, g):
    t, c = x.shape

    def body(x_ref, w_ref, g_ref, dx_ref, dw_ref, db_ref, xp, gp):
        xp[0:8, :] = jnp.zeros((8, c), F32)
        xp[8:, :] = x_ref[...]
        gp[0:t, :] = g_ref[...]
        gp[t:, :] = jnp.zeros((8, c), F32)
        dw = [jnp.zeros((1, c), F32) for _ in range(CONV_K)]
        db = jnp.zeros((1, c), F32)
        for blk in range(t // CONV_ROWS):
            base = blk * CONV_ROWS
            gb = g_ref[base:base + CONV_ROWS, :]
            acc = jnp.zeros((CONV_ROWS, c), F32)
            for kk in range(CONV_K):
                acc = acc + w_ref[kk:kk + 1, :] * gp[base + 3 - kk:base + 3 - kk + CONV_ROWS, :]
                dw[kk] = dw[kk] + jnp.sum(gb * xp[base + 5 + kk:base + 5 + kk + CONV_ROWS, :], axis=0, keepdims=True)
            db = db + jnp.sum(gb, axis=0, keepdims=True)
            dx_ref[base:base + CONV_ROWS, :] = acc
        for kk in range(CONV_K):
            dw_ref[kk:kk + 1, :] = dw[kk]
        db_ref[...] = db

    return pl.pallas_call(
        body, name="conv_b",
        out_shape=[jax.ShapeDtypeStruct((t, c), F32), jax.ShapeDtypeStruct((CONV_K, c), F32),
                   jax.ShapeDtypeStruct((1, c), F32)],
        scratch_shapes=[pltpu.VMEM((t + 8, c), F32)] * 2, compiler_params=_cp())(x, w, g)


@jax.custom_vjp
def causal_conv(x, w, b):
    return _conv_fwd_call(x, w, b)


def _causal_conv_fwd(x, w, b):
    return _conv_fwd_call(x, w, b), (x, w)


def _causal_conv_bwd(r, g):
    return tuple(_conv_bwd_call(r[0], r[1], g))


causal_conv.defvjp(_causal_conv_fwd, _causal_conv_bwd)


ATT_TILE = 512
ATT_SCALE = HEAD_DIM ** -0.5


def _head_lane(val, lane, h):
    return jnp.sum(jnp.where(lane == h, val, 0.0), axis=1, keepdims=True)


def _attn_fwd_call(qkv, c128, cr, next_shard=None, late=None):
    t, w = qkv.shape[0], MIXER_WIDTH
    tq = ATT_TILE
    nq = t // tq
    qkv3, cr4 = qkv.reshape(nq, tq, 3 * w), cr.reshape(N_HEADS, nq, 1, tq)
    fused = next_shard is not None
    both = late is not None

    def body(*refs):
        q_ref, k_ref, v_ref, c_ref, cr_ref = refs[:5]
        i, h = pl.program_id(0), pl.program_id(1)
        rest = list(refs[5:])
        shard_ref = rest.pop(0) if fused else None
        own_ref = rest.pop(0) if both else None
        if both:
            rest.pop(0)
        o_ref, lse_ref = rest.pop(0), rest.pop(0)
        gathers = []
        if fused:
            gathers.append(_allgather_copies(shard_ref, rest.pop(0), rest[-2 - 2 * both], rest[-1 - 2 * both], EARLY_ROWS))
        if both:
            gathers.append(_allgather_copies(own_ref, rest.pop(0), rest[-2], rest[-1], LATE_ROWS))
        for start, _ in gathers:
            pl.when((i == 0) & (h == 0))(start)
        hm = lax.broadcasted_iota(jnp.int32, (tq, w), 1) // HEAD_DIM == h
        lane = lax.broadcasted_iota(jnp.int32, (tq, 128), 1)
        qs = jnp.where(hm, q_ref[...] * ATT_SCALE, 0.0)
        cq = _head_lane(c_ref[...], lane, h)
        causal = lax.broadcasted_iota(jnp.int32, (tq, tq), 0) >= lax.broadcasted_iota(jnp.int32, (tq, tq), 1)

        def update(j, carry, diagonal):
            m, l, acc = carry
            s = _dot(qs, k_ref[j], 1, 1) - cr_ref[0, j]
            if diagonal:
                s = jnp.where(causal, s, NEG)
            m_new = jnp.maximum(m, cq + jnp.max(s, axis=1, keepdims=True))
            p = jnp.exp(s + (cq - m_new))
            alpha = jnp.exp(m - m_new)
            return m_new, alpha * l + jnp.sum(p, axis=1, keepdims=True), alpha * acc + _dot(p, v_ref[j], 1, 0)

        init = (jnp.full((tq, 1), NEG, F32), jnp.zeros((tq, 1), F32), jnp.zeros((tq, w), F32))
        carry = lax.fori_loop(0, i, lambda j, c: update(j, c, False), init)
        m, l, acc = update(i, carry, True)
        out = jnp.where(hm, acc / l, 0.0)
        lse = jnp.where(lane == h, m + jnp.log(l), 0.0)

        @pl.when(h == 0)
        def _():
            o_ref[...] = out
            lse_ref[...] = lse

        @pl.when(h > 0)
        def _():
            o_ref[...] += out
            lse_ref[...] += lse

        for _, finish in gathers:
            pl.when((i == nq - 1) & (h == N_HEADS - 1))(finish)

    tile = pl.BlockSpec((tq, w), lambda i, h: (i, 0))
    tile_c = pl.BlockSpec((tq, 128), lambda i, h: (i, 0))
    rows = pl.BlockSpec((1, nq, 1, tq), lambda i, h: (h, 0, 0, 0))
    q_spec = pl.BlockSpec((None, tq, w), lambda i, h: (i, 0, 0))
    k_spec = pl.BlockSpec((nq, tq, w), lambda i, h: (0, 0, 1))
    v_spec = pl.BlockSpec((nq, tq, w), lambda i, h: (0, 0, 2))
    ins = [qkv3, qkv3, qkv3, c128, cr4]
    in_specs, out_specs = [q_spec, k_spec, v_spec, tile_c, rows], [tile, tile_c]
    out_shape = [jax.ShapeDtypeStruct((t, w), F32), jax.ShapeDtypeStruct((t, 128), F32)]
    scratch, aliases, name = [], {}, "attn_f"
    sem_pair = [pltpu.SemaphoreType.DMA((AG_SEMS,)), pltpu.SemaphoreType.DMA((AG_SEMS,))]
    if fused:
        ins.append(next_shard)
        in_specs.append(HBM)
        name += "_next"
    if both:
        aliases = {len(ins) + 1: 2 + fused}
        ins += list(late)
        in_specs += [HBM, HBM]
        name += "_late"
    if fused:
        out_specs.append(HBM)
        out_shape.append(jax.ShapeDtypeStruct((4,) + next_shard.shape, next_shard.dtype))
        scratch += sem_pair
    if both:
        out_specs.append(HBM)
        out_shape.append(jax.ShapeDtypeStruct(late[1].shape, late[1].dtype))
        scratch += sem_pair
    return pl.pallas_call(
        body, name=name, grid=(nq, N_HEADS), in_specs=in_specs, out_specs=out_specs, out_shape=out_shape,
        scratch_shapes=scratch, input_output_aliases=aliases,
        compiler_params=_cp(("arbitrary", "arbitrary")))(*ins)


def _attn_bwd_call(qkv, c128, cr, o, lse, do, exchange=None):
    t, w = qkv.shape[0], MIXER_WIDTH
    tq = ATT_TILE
    nq = t // tq
    r3 = lambda a: a.reshape(nq, tq, a.shape[-1])
    cr4 = cr.reshape(N_HEADS, nq, 1, tq)
    fused = exchange is not None

    def body(*refs):
        q_ref, k_ref, v_ref, c_ref, cr_ref, o_ref, lse_ref, do_ref = refs[:8]
        j, h = pl.program_id(0), pl.program_id(1)
        if fused:
            p_ref, dq_ref, dk_ref, dv_ref, dc_ref, dcr_ref, recv_ref, send_sems, recv_sems = refs[8:]
            start, finish = _chip_exchange_copies(p_ref, recv_ref, send_sems, recv_sems)
            pl.when((j == 0) & (h == 0))(start)
        else:
            dq_ref, dk_ref, dv_ref, dc_ref, dcr_ref = refs[8:]

        @pl.when((j == 0) & (h == 0))
        def _():
            dq_ref[...] = jnp.zeros_like(dq_ref)
            dc_ref[...] = jnp.zeros_like(dc_ref)

        hm = lax.broadcasted_iota(jnp.int32, (tq, w), 1) // HEAD_DIM == h
        lane = lax.broadcasted_iota(jnp.int32, (tq, 128), 1)
        kj = k_ref[...]
        vj = v_ref[...]
        ck = cr_ref[0, 0]
        causal = lax.broadcasted_iota(jnp.int32, (tq, tq), 0) >= lax.broadcasted_iota(jnp.int32, (tq, tq), 1)

        def step(i, carry, diagonal):
            dk, dv, dck = carry
            qm = jnp.where(hm, q_ref[i], 0.0)
            dom = jnp.where(hm, do_ref[i], 0.0)
            s = _dot(qm * ATT_SCALE, kj, 1, 1) - ck
            if diagonal:
                s = jnp.where(causal, s, NEG)
            p = jnp.exp(s + (_head_lane(c_ref[i], lane, h) - _head_lane(lse_ref[i], lane, h)))
            dv = dv + _dot(p, dom, 0, 0)
            dp = _dot(dom, vj, 1, 1)
            delta = jnp.sum(dom * o_ref[i], axis=1, keepdims=True)
            ds = p * (dp - delta)
            dq_ref[i] += jnp.where(hm, _dot(ds, kj, 1, 0), 0.0) * ATT_SCALE
            dk = dk + _dot(ds, qm, 0, 0) * ATT_SCALE
            dc_ref[i] += jnp.where(lane == h, jnp.sum(ds, axis=1, keepdims=True), 0.0)
            return dk, dv, dck - jnp.sum(ds, axis=0, keepdims=True)

        init = (jnp.zeros((tq, w), F32), jnp.zeros((tq, w), F32), jnp.zeros((1, tq), F32))
        carry = step(j, init, True)
        dk, dv, dck = lax.fori_loop(j + 1, nq, lambda i, c: step(i, c, False), carry)
        dcr_ref[0, 0] = dck

        @pl.when(h == 0)
        def _():
            dk_ref[...] = dk
            dv_ref[...] = dv

        @pl.when(h > 0)
        def _():
            dk_ref[...] += dk
            dv_ref[...] += dv

        if fused:
            pl.when((j == nq - 1) & (h == N_HEADS - 1))(finish)

    whole = pl.BlockSpec((nq, tq, w), lambda j, h: (0, 0, 0))
    whole_c = pl.BlockSpec((nq, tq, 128), lambda j, h: (0, 0, 0))
    tile = pl.BlockSpec((None, tq, w), lambda j, h: (j, 0, 0))
    tile_r = pl.BlockSpec((1, 1, 1, tq), lambda j, h: (h, j, 0, 0))
    s3 = jax.ShapeDtypeStruct((nq, tq, w), F32)
    qkv3 = r3(qkv)
    k_tile = pl.BlockSpec((None, tq, w), lambda j, h: (j, 0, 1))
    v_tile = pl.BlockSpec((None, tq, w), lambda j, h: (j, 0, 2))
    ins = [qkv3, qkv3, qkv3, r3(c128), cr4, r3(o), r3(lse), r3(do)]
    in_specs = [whole, k_tile, v_tile, whole_c, tile_r, whole, whole_c, whole]
    out_specs = [whole, tile, tile, whole_c, tile_r]
    out_shape = [s3, s3, s3, jax.ShapeDtypeStruct((nq, tq, 128), F32), jax.ShapeDtypeStruct((N_HEADS, nq, 1, tq), F32)]
    scratch = []
    if fused:
        ins.append(exchange)
        in_specs.append(HBM)
        out_specs.append(HBM)
        out_shape.append(jax.ShapeDtypeStruct((3,) + exchange.shape[1:], exchange.dtype))
        scratch = [pltpu.SemaphoreType.DMA((3,)), pltpu.SemaphoreType.DMA((3,))]
    dq, dk, dv, dc, dcr, *received = pl.pallas_call(
        body, name="attn_b_exchange" if fused else "attn_b", grid=(nq, N_HEADS), in_specs=in_specs,
        out_specs=out_specs, out_shape=out_shape, scratch_shapes=scratch,
        compiler_params=_cp(("arbitrary", "arbitrary")))(*ins)
    grads = (dq.reshape(t, w), dk.reshape(t, w), dv.reshape(t, w), dc.reshape(t, 128), dcr.reshape(N_HEADS, 1, t))
    return grads, (received[0] if fused else None)


def _loss_call(x, g, target):
    t, d = x.shape
    tr = 512

    def body(x_ref, g_ref, t_ref, loss_ref, dx_ref, dg_ref):
        tgt = t_ref[...]

        def f(xv, gv):
            return 0.5 * jnp.sum(jnp.mean(jnp.square(_rms(xv, gv) - tgt), axis=-1))

        val, vjp = jax.vjp(f, x_ref[...], g_ref[...])
        dx, dg = vjp(jnp.ones((), F32))
        dx_ref[...] = dx

        @pl.when(pl.program_id(0) == 0)
        def _():
            loss_ref[...] = jnp.zeros_like(loss_ref)
            dg_ref[...] = jnp.zeros_like(dg_ref)

        loss_ref[...] += jnp.full(loss_ref.shape, val, F32)
        dg_ref[...] += dg

    row = _row_spec(tr, d)
    return pl.pallas_call(
        body, name="loss_head", grid=(t // tr,), in_specs=[row, _full_spec(g), row],
        out_specs=[pl.BlockSpec((1, 128), lambda i: (0, 0)), row, _full_spec(g)],
        out_shape=[jax.ShapeDtypeStruct((1, 128), F32), jax.ShapeDtypeStruct((t, d), F32),
                   jax.ShapeDtypeStruct(g.shape, F32)],
        compiler_params=_cp(("arbitrary",)))(x, g, target)


def _blockdiag(w):
    l, g, a, b = w.shape
    return jnp.einsum('lgab,gk->lgakb', w, jnp.eye(g, dtype=w.dtype)).reshape(l, g * a, g * b)


def _prepare(rep):
    d = DEPTH
    w = MIXER_WIDTH
    rows = S5_GROUPS * S5_GROUP
    rep16 = lambda a: jnp.repeat(a, S5_GROUP, axis=1).reshape(d * rows, -1)
    bt = lambda b: b.transpose(0, 1, 3, 2).reshape(d * rows, S5_STATE)
    abar_re, abar_im, bb_re, bb_im = s5_disc(
        rep16(rep["s5_lambda_re"]), rep16(rep["s5_lambda_im"]), rep16(rep["s5_log_dt"][:, :, None]),
        bt(rep["s5_b_re"]), bt(rep["s5_b_im"]))
    g4 = lambda a: a.reshape(d, S5_GROUPS, S5_GROUP, S5_STATE)
    first = lambda a: g4(a)[:, :, 0, :].reshape(d, 1, S5_GROUPS * S5_STATE)
    cblk = lambda c: _blockdiag(c.transpose(0, 1, 3, 2))
    row = lambda a: a.reshape(d, 1, -1)
    return dict(
        sgu_norm_g=row(rep["sgu_norm_g"]), sgu_w=rep["sgu_w"],
        sgu_bias=jnp.repeat(rep["sgu_b"].transpose(0, 2, 1), HEAD_DIM, axis=2),
        abar_re=first(abar_re), abar_im=first(abar_im), bblk_re=_blockdiag(g4(bb_re)), bblk_im=_blockdiag(g4(bb_im)),
        cblk_re=cblk(rep["s5_c_re"]), cblk_im=cblk(rep["s5_c_im"]), s5_d=row(rep["s5_d"]), s5_glu_b=row(rep["s5_glu_b"]),
        lru_conv_b=row(rep["lru_conv_b"]), lru_wa=_blockdiag(rep["lru_wa"]), lru_ba=row(rep["lru_ba"]),
        lru_wx=_blockdiag(rep["lru_wx"]), lru_bx=row(rep["lru_bx"]), lru_lambda=row(rep["lru_lambda"]),
        fgate_b=jnp.pad(rep["fox_fgate_b"], ((0, 0), (0, 128 - N_HEADS)))[:, None, :])


PREPARED_FROM = ('sgu_norm_g', 'sgu_w', 'sgu_b', 's5_lambda_re', 's5_lambda_im', 's5_log_dt', 's5_b_re', 's5_b_im',
                 's5_c_re', 's5_c_im', 's5_d', 's5_glu_b', 'lru_conv_b', 'lru_wa', 'lru_ba', 'lru_wx', 'lru_bx',
                 'lru_lambda', 'fox_fgate_b')


def _mixers_pre(pieces, p, glu_w, conv_w):
    a_u, a_v, b_in, c_x, c_gate, d_q, d_k, d_v, d_f = pieces
    sw = p["sgu_w"]
    (y_a,) = sgu_mix(a_u, a_v, p["sgu_norm_g"], sw[0], sw[1], sw[2], sw[3], p["sgu_bias"])
    s_re, s_im = lti_scan(b_in, p["bblk_re"], p["bblk_im"], p["abar_re"], p["abar_im"])
    (y_b,) = s5_post(s_re, s_im, b_in, p["cblk_re"], p["cblk_im"], p["s5_d"], glu_w, p["s5_glu_b"])
    xc = causal_conv(c_x, conv_w, p["lru_conv_b"])
    a, b = lru_pre(xc, p["lru_wa"], p["lru_ba"], p["lru_wx"], p["lru_bx"], p["lru_lambda"])
    (y_c,) = lru_post(tv_scan(a, b), c_gate)
    (log_f,) = log_sig(d_f, p["fgate_b"])
    c128 = tv_scan(jnp.ones_like(log_f), log_f)
    return y_a, y_b, y_c, d_q, d_k, d_v, c128, c128[:, :N_HEADS].T[:, None, :]


PACK_COLS = 1024
SHARD_SHAPE = {'w_mlp_in': (1024, 1024), 'w_mlp_out': (1024, 1024), 'w_out': (256, 1024), 'w_in': (1024, 513),
               's5_glu_w': (64, 256), 'lru_conv_w': (4, 64)}
SHARDED_AXIS = {'w_in': 1, 's5_glu_w': 0, 'lru_conv_w': 1, 'w_out': 0, 'w_mlp_in': 1, 'w_mlp_out': 0}
SHARD_ROWS = {n: -(-s[0] * s[1] // PACK_COLS) for n, s in SHARD_SHAPE.items()}
SHARD_OFF = {n: sum(list(SHARD_ROWS.values())[:i]) for i, n in enumerate(SHARD_SHAPE)}
LAYER_ROWS = 2880
SMALL_OFF = SHARD_OFF['w_in']
GLUE_ROWS = LAYER_ROWS - SMALL_OFF
assert SHARD_OFF['w_mlp_out'] == 1024 and SHARD_OFF['w_out'] == 2048 and SMALL_OFF % GLUE_ROWS == 0
assert SHARD_OFF['lru_conv_w'] + SHARD_ROWS['lru_conv_w'] <= LAYER_ROWS
PACK_ROWS = DEPTH * LAYER_ROWS
TOK = 1024
FF = 4 * D_MODEL


def _w3(i_of):
    return pl.BlockSpec((None, 1024, PACK_COLS), i_of)


def _tile2(rows, cols, i_of):
    return pl.BlockSpec((rows, cols), i_of)


def _layer_fwd(x, h1, gathered, w_in, p, mix_p, next_shard, own_shard, next_gain):
    t = x.shape[0]
    nt = t // TOK
    f32 = lambda r, c: jax.ShapeDtypeStruct((r, c), F32)
    b16 = lambda r, c: jax.ShapeDtypeStruct((r, c), BF16)
    g2, gm = p["norm2_g"][None, :], p["mix_norm_g"][None, :]
    qkv_cols = (5 * MIXER_WIDTH, 8 * MIXER_WIDTH)
    z, qkv = _matmul("mm_in", (nt, 1, 1), h1, _tile2(TOK, D_MODEL, lambda i, j, k: (i, 0)),
                     w_in, _tile2(D_MODEL, D_IN_PAD, lambda i, j, k: (0, 0)), (1, 0),
                     [(f32(t, D_IN_PAD), _tile2(TOK, D_IN_PAD, lambda i, j, k: (i, 0))),
                      (b16(t, 3 * MIXER_WIDTH), _tile2(TOK, 3 * MIXER_WIDTH, lambda i, j, k: (i, 0)))],
                     epilogue=lambda acc: (acc, acc[:, qkv_cols[0]:qkv_cols[1]]))
    pieces = tuple(jnp.split(z, [MIXER_WIDTH * i for i in range(1, 9)], axis=1))
    (y_a, y_b, y_c, _, _, _, c128, cr), mix_vjp = jax.vjp(_mixers_pre, pieces, *mix_p)
    attn_in = (qkv, c128, cr)
    y_d, lse, *more = _attn_fwd_call(*attn_in, next_shard=next_shard,
                                     late=None if own_shard is None else (own_shard, gathered))
    if own_shard is not None:
        gathered = more.pop()
    w_out = _join_chips(_unpack_shards(gathered[:, None], ('w_out',))['w_out'], SHARDED_AXIS['w_out'])[0]
    ys = (y_a, y_b, y_c, y_d)
    (yn,) = _rowwise(_f_gnorm, list(ys), [gm], [D_MODEL], name="gnorm_f", tr=512, dtype=BF16)
    x_tile = _tile2(TOK, D_MODEL, lambda i, j, k: (i, 0))
    gain = _tile2(1, D_MODEL, lambda i, j, k: (0, 0))

    def add_and_norm(acc, r, gv):
        s = acc + r
        return s, _rms(s, gv)

    x1, h2 = _matmul("mm_out", (nt, 1, 1), yn, x_tile, w_out, _tile2(D_MODEL, D_MODEL, lambda i, j, k: (0, 0)), (1, 0),
                     [(f32(t, D_MODEL), x_tile), (b16(t, D_MODEL), x_tile)], extras=[(x, x_tile), (g2, gain)],
                     epilogue=add_and_norm)
    ff_tile = _tile2(TOK, 1024, lambda i, j, k: (i, j))
    act = _matmul("mm_up", (nt, FF // 1024, 1), h2, x_tile, gathered, _w3(lambda i, j, k: (j, 0, 0)), (1, 0),
                  [(b16(t, FF), ff_tile)], epilogue=lambda acc: (jnp.square(jnp.maximum(acc, 0.0)),))
    down = ("mm_down", (nt, 1, FF // 1024), act, _tile2(TOK, 1024, lambda i, j, k: (i, k)),
            gathered, _w3(lambda i, j, k: (k, 1, 0)), (1, 0))
    if next_gain is None:
        x2, h_next = _matmul(*down, [(f32(t, D_MODEL), x_tile)], extras=[(x1, x_tile)],
                             epilogue=lambda acc, r: (acc + r,)), None
    else:
        x2, h_next = _matmul(*down, [(f32(t, D_MODEL), x_tile), (b16(t, D_MODEL), x_tile)],
                             extras=[(x1, x_tile), (next_gain[None, :], gain)], epilogue=add_and_norm)
    res = (x, h1, mix_vjp, ys, attn_in, lse, yn, x1, h2, act, gathered, w_out)
    return x2, h_next, res, (more[0] if more else None)


def _layer_bwd(g, res, w_in, p, later_send):
    x, h1, mix_vjp, ys, attn_in, lse, yn, x1, h2, act, gathered, w_out = res
    half_rows = LAYER_ROWS // 2
    riding = later_send is not None
    halves = later_send.reshape(4, 2, half_rows, PACK_COLS) if riding else None
    send = lax.empty((4, LAYER_ROWS, PACK_COLS), BF16)
    t = x.shape[0]
    nt = t // TOK
    f32 = lambda r, c: jax.ShapeDtypeStruct((r, c), F32)
    g1, g2, gm = p["norm1_g"][None, :], p["norm2_g"][None, :], p["mix_norm_g"][None, :]
    x_tile = _tile2(TOK, D_MODEL, lambda i, j, k: (i, 0))
    ff_tile = _tile2(TOK, 1024, lambda i, j, k: (i, j))
    tok_k = _tile2(TOK, D_MODEL, lambda i, j, k: (k, 0))
    send_s = jax.ShapeDtypeStruct(send.shape, send.dtype)
    pair_rider = (halves, jax.ShapeDtypeStruct((4, half_rows, PACK_COLS), BF16), False, _pair_exchange_copies)
    du = _matmul("mm_down_dx", (nt, FF // 1024, 1), g, x_tile, gathered, _w3(lambda i, j, k: (j, 1, 0)), (1, 1),
                 [(jax.ShapeDtypeStruct((t, FF), BF16), ff_tile)], extras=[(act, ff_tile)],
                 epilogue=lambda acc, a: (2.0 * jnp.sqrt(a.astype(F32)) * acc,),
                 rider=pair_rider if riding else None)
    exchange = None
    if riding:
        du, from_sibling = du
        exchange = _add_kept(halves, from_sibling, REDUCE_ROWS)
    send = _matmul("mm_down_dw", (FF // 1024, 1, nt), act, _tile2(TOK, 1024, lambda i, j, k: (k, i)), g, tok_k, (0, 0),
                   [(send_s, _w3(lambda i, j, k: (i, 1, 0)))], into=send)
    send = _matmul("mm_up_dw", (1, FF // 1024, nt), h2, tok_k, du, _tile2(TOK, 1024, lambda i, j, k: (k, j)), (0, 0),
                   [(send_s, _w3(lambda i, j, k: (j, 0, 0)))], into=send)
    gain = _tile2(1, D_MODEL, lambda i, j, k: (0, 0))

    def norm_bwd(dh, xv, gv, through):
        _, vjp = jax.vjp(_rms, xv, gv)
        dxv, dgv = vjp(dh)
        return dxv + through, dgv

    g_mid, dg2 = _matmul("mm_up_dx", (nt, 1, FF // 1024), du, _tile2(TOK, 1024, lambda i, j, k: (i, k)),
                         gathered, _w3(lambda i, j, k: (k, 0, 0)), (1, 1),
                         [(f32(t, D_MODEL), x_tile), (f32(1, D_MODEL), gain)],
                         extras=[(x1, x_tile), (g2, gain), (g, x_tile)], epilogue=norm_bwd, summed=1)
    w_full = _tile2(D_MODEL, D_MODEL, lambda i, j, k: (0, 0))
    dyn = _matmul("mm_out_dx", (nt, 1, 1), g_mid, x_tile, w_out, w_full, (1, 1), [(f32(t, D_MODEL), x_tile)])
    quarter = D_MODEL // 4
    send = _matmul("mm_out_dw", (1, 1, nt), yn, tok_k, g_mid, tok_k, (0, 0),
                   [(send_s, pl.BlockSpec((4, quarter, PACK_COLS), lambda i, j, k: (0, SHARD_OFF['w_out'] // quarter, 0)))],
                   epilogue=lambda acc: (acc.reshape(4, quarter, PACK_COLS),), into=send, acc_shape=(D_MODEL, D_MODEL))
    dy_a, dy_b, dy_c, dy_d, dgm = _rowwise_vjp(_f_gnorm, list(ys), [gm], [dyn], name="gnorm_b", tr=512)
    d_attn_in, received = _attn_bwd_call(*attn_in, ys[3], lse, dy_d, exchange=exchange)
    d_pieces, *d_mix = mix_vjp((dy_a, dy_b, dy_c, *d_attn_in))
    dz = jnp.concatenate([d.astype(BF16) for d in d_pieces], axis=1)
    z_tile = _tile2(TOK, D_IN_PAD, lambda i, j, k: (i, 0))
    share_rider = None
    if riding:
        pair = _sum_chips(exchange, received, REDUCE_ROWS)
        share_rider = (pair, jax.ShapeDtypeStruct(pair.shape, pair.dtype), True, _pair_share_copies)
    d_w_in = _matmul("mm_in_dw", (1, 1, t // 512), h1, _tile2(512, D_MODEL, lambda i, j, k: (k, 0)),
                     dz, _tile2(512, D_IN_PAD, lambda i, j, k: (k, 0)), (0, 0),
                     [(f32(D_MODEL, D_IN_PAD), _tile2(D_MODEL, D_IN_PAD, lambda i, j, k: (0, 0)))], rider=share_rider)
    reduced = None
    if riding:
        d_w_in, pair = d_w_in
        reduced = pair.reshape(LAYER_ROWS, PACK_COLS)
    dx, dg1 = _matmul("mm_in_dx", (nt, 1, 1), dz, z_tile, w_in, _tile2(D_MODEL, D_IN_PAD, lambda i, j, k: (0, 0)), (1, 1),
                      [(f32(t, D_MODEL), x_tile), (f32(1, D_MODEL), gain)],
                      extras=[(x, x_tile), (g1, gain), (g_mid, x_tile)], epilogue=norm_bwd, summed=1)
    norms = dict(norm1_g=dg1[0], norm2_g=dg2[0], mix_norm_g=dgm[0])
    return dx, norms, d_mix, d_w_in, send, reduced


HBM = pl.BlockSpec(memory_space=pltpu.HBM)
D2D_CHUNKS = 15
ICI_CHUNKS = 5
VMEM_CHUNKS = 4


def _coords():
    return lax.axis_index("x"), lax.axis_index("y"), lax.axis_index("c")


def _other_chips(x, y):
    return [(1 - x, y), (x, 1 - y), (1 - x, 1 - y)]


def _start_chunks(make, rows, n):
    size = rows // n
    assert size * n == rows
    for k in range(n):
        make(pl.ds(k * size, size)).start()


AG_SEMS = 7


ALL_ROWS = (0, LAYER_ROWS, ICI_CHUNKS, D2D_CHUNKS)
EARLY_ROWS = (SMALL_OFF, GLUE_ROWS, 4, 4)
LATE_ROWS = (0, SMALL_OFF, 4, 12)


def _allgather_copies(in_ref, out_ref, send_sems, recv_sems, part=ALL_ROWS):
    r0, r, ici_chunks, d2d_chunks = part
    rh = r // 2
    x, y, c = _coords()
    me, sibling = (x, y, c), (x, y, 1 - c)
    chips = _other_chips(x, y)

    def half(px, py, pc, rows=pl.ds(0, rh)):
        return out_ref.at[2 * px + py, pl.ds(r0 + pc * rh + rows.start, rows.size), :]

    def copy(k, block, to, rows=pl.ds(0, rh), from_input=False):
        src = in_ref.at[pl.ds(r0 + block[2] * rh + rows.start, rows.size), :] if from_input else half(*block, rows)
        return pltpu.make_async_remote_copy(
            src_ref=src, dst_ref=half(*block, rows), send_sem=send_sems.at[k], recv_sem=recv_sems.at[k],
            device_id=to, device_id_type=MESH)

    def own(rows=pl.ds(0, r)):
        mine = pl.ds(r0 + rows.start, rows.size)
        return pltpu.make_async_remote_copy(
            src_ref=in_ref.at[mine, :], dst_ref=out_ref.at[2 * x + y, mine, :], send_sem=send_sems.at[6],
            recv_sem=recv_sems.at[6], device_id=sibling, device_id_type=MESH)

    def start():
        for j, chip in enumerate(chips):
            _start_chunks(lambda rows: copy(j, me, (*chip, c), rows, from_input=True), rh, ici_chunks)
        _start_chunks(own, r, d2d_chunks)

    def finish():
        for j, chip in enumerate(chips):
            copy(j, (*chip, c), me).wait_recv()
            _start_chunks(lambda rows: copy(3 + j, (*chip, c), sibling, rows), rh, d2d_chunks)
        for j, chip in enumerate(chips):
            copy(3 + j, (*chip, 1 - c), me).wait_recv()
        for j, chip in enumerate(chips):
            copy(j, me, (*chip, c), from_input=True).wait_send()
            copy(3 + j, (*chip, c), sibling).wait_send()
        own().wait()

    return start, finish


def _allgather_shards(shard, part):
    def body(in_ref, out_ref, send_sems, recv_sems):
        start, finish = _allgather_copies(in_ref, out_ref, send_sems, recv_sems, part)
        start()
        finish()

    return pl.pallas_call(
        body, name="allgather_shards", out_shape=jax.ShapeDtypeStruct((4,) + shard.shape, shard.dtype),
        in_specs=[HBM], out_specs=HBM,
        scratch_shapes=[pltpu.SemaphoreType.DMA((AG_SEMS,)), pltpu.SemaphoreType.DMA((AG_SEMS,))],
        compiler_params=pltpu.CompilerParams())(shard)


def _pair_exchange_copies(g_ref, recv_ref, send_sem, recv_sem):
    s, _, rh, _ = g_ref.shape
    x, y, c = _coords()

    def copy(slot, rows):
        return pltpu.make_async_remote_copy(
            src_ref=g_ref.at[slot, 1 - c, rows, :], dst_ref=recv_ref.at[slot, rows, :], send_sem=send_sem,
            recv_sem=recv_sem, device_id=(x, y, 1 - c), device_id_type=MESH)

    def start():
        for slot in range(s):
            _start_chunks(lambda rows: copy(slot, rows), rh, VMEM_CHUNKS)

    def finish():
        pltpu.make_async_remote_copy(
            src_ref=g_ref.at[:, 1 - c], dst_ref=recv_ref, send_sem=send_sem, recv_sem=recv_sem,
            device_id=(x, y, 1 - c), device_id_type=MESH).wait()

    return start, finish


def _pair_share_copies(in_ref, out_ref, send_sem, recv_sem):
    rh = in_ref.shape[1]
    x, y, c = _coords()

    def copy(slot, rows=pl.ds(0, rh)):
        return pltpu.make_async_remote_copy(
            src_ref=in_ref.at[slot, rows, :], dst_ref=out_ref.at[slot, rows, :], send_sem=send_sem,
            recv_sem=recv_sem, device_id=(x, y, 1 - c), device_id_type=MESH)

    def start():
        _start_chunks(lambda rows: copy(c, rows), rh, D2D_CHUNKS)

    def finish():
        copy(c).wait_send()
        copy(1 - c).wait_recv()

    return start, finish


def _pair_exchange(g):
    s, _, rh, cols = g.shape

    def body(g_ref, recv_ref, send_sem, recv_sem):
        start, finish = _pair_exchange_copies(g_ref, recv_ref, send_sem, recv_sem)
        start()
        finish()

    return pl.pallas_call(
        body, name="pair_exchange", out_shape=jax.ShapeDtypeStruct((s, rh, cols), g.dtype), in_specs=[HBM],
        out_specs=HBM, scratch_shapes=[pltpu.SemaphoreType.DMA] * 2, compiler_params=pltpu.CompilerParams())(g)


def _chip_exchange_copies(p_ref, recv_ref, send_sems, recv_sems):
    rh = p_ref.shape[1]
    x, y, c = _coords()
    chips = _other_chips(x, y)

    def copy(j, chip, rows=pl.ds(0, rh)):
        return pltpu.make_async_remote_copy(
            src_ref=p_ref.at[2 * chip[0] + chip[1], rows, :], dst_ref=recv_ref.at[j, rows, :],
            send_sem=send_sems.at[j], recv_sem=recv_sems.at[j], device_id=(*chip, c), device_id_type=MESH)

    def start():
        for j, chip in enumerate(chips):
            _start_chunks(lambda rows: copy(j, chip, rows), rh, ICI_CHUNKS)

    def finish():
        for j, chip in enumerate(chips):
            copy(j, chip).wait_recv()
        for j, chip in enumerate(chips):
            copy(j, chip).wait_send()

    return start, finish


def _sum_chips(p, recv, tr):
    _, rh, cols = p.shape
    x, y, c = _coords()
    one = lambda v: v.astype(jnp.int32).reshape(1)

    def body(chip_ref, c_ref, own_ref, r_ref, o_ref):
        acc = own_ref[...].astype(F32)
        for k in range(3):
            acc = acc + r_ref[k].astype(F32)
        o_ref[...] = acc

    return pl.pallas_call(
        body, name="sum_chips", out_shape=jax.ShapeDtypeStruct((2, rh, cols), F32),
        grid_spec=pltpu.PrefetchScalarGridSpec(
            num_scalar_prefetch=2, grid=(rh // tr,),
            in_specs=[pl.BlockSpec((None, tr, cols), lambda i, chip_ref, c_ref: (chip_ref[0], i, 0)),
                      pl.BlockSpec((3, tr, cols), lambda i, chip_ref, c_ref: (0, i, 0))],
            out_specs=pl.BlockSpec((None, tr, cols), lambda i, chip_ref, c_ref: (c_ref[0], i, 0))),
        compiler_params=_cp(("arbitrary",)))(one(2 * x + y), one(c), p, recv)


def _pair_share(buf):
    _, rh, cols = buf.shape

    def body(in_ref, out_ref, send_sem, recv_sem):
        start, finish = _pair_share_copies(in_ref, out_ref, send_sem, recv_sem)
        start()
        finish()

    return pl.pallas_call(
        body, name="pair_share", out_shape=jax.ShapeDtypeStruct(buf.shape, buf.dtype), in_specs=[HBM], out_specs=HBM,
        scratch_shapes=[pltpu.SemaphoreType.DMA] * 2, input_output_aliases={0: 0},
        compiler_params=pltpu.CompilerParams())(buf)


def _allgather_all(blk, exchange):
    m_per, cols = blk.shape
    whole = pl.ds(0, m_per)

    def body(x_ref, p_ref, out_ref, got_ref, send_sems, recv_sems, local_sem, p_send_sems, p_recv_sems):
        x, y, c = _coords()
        me, sibling = (x, y, c), (x, y, 1 - c)
        chips = _other_chips(x, y)
        start_exchange, finish_exchange = _chip_exchange_copies(p_ref, got_ref, p_send_sems, p_recv_sems)
        start_exchange()

        def rows_of(px, py, pc, rows):
            return out_ref.at[4 * px + 2 * py + pc, rows, :]

        def copy(k, block, to, rows=whole, from_input=False):
            return pltpu.make_async_remote_copy(
                src_ref=x_ref.at[rows, :] if from_input else rows_of(*block, rows), dst_ref=rows_of(*block, rows),
                send_sem=send_sems.at[k], recv_sem=recv_sems.at[k], device_id=to, device_id_type=MESH)

        mine = pltpu.make_async_copy(x_ref, rows_of(*me, whole), local_sem)
        mine.start()
        _start_chunks(lambda rows: copy(0, me, sibling, rows, from_input=True), m_per, VMEM_CHUNKS)
        for j, chip in enumerate(chips):
            _start_chunks(lambda rows: copy(1 + j, me, (*chip, c), rows, from_input=True), m_per, VMEM_CHUNKS)
        for j, chip in enumerate(chips):
            copy(1 + j, (*chip, c), me).wait_recv()
            _start_chunks(lambda rows: copy(4 + j, (*chip, c), sibling, rows), m_per, VMEM_CHUNKS)
        copy(0, sibling, me).wait_recv()
        for j, chip in enumerate(chips):
            copy(4 + j, (*chip, 1 - c), me).wait_recv()
        copy(0, me, sibling, from_input=True).wait_send()
        for j, chip in enumerate(chips):
            copy(1 + j, me, (*chip, c), from_input=True).wait_send()
            copy(4 + j, (*chip, c), sibling).wait_send()
        mine.wait()
        finish_exchange()

    vmem = pl.BlockSpec(memory_space=pltpu.VMEM)
    return pl.pallas_call(
        body, name="allgather_all",
        out_shape=[jax.ShapeDtypeStruct((8, m_per, cols), blk.dtype),
                   jax.ShapeDtypeStruct((3,) + exchange.shape[1:], exchange.dtype)],
        in_specs=[vmem, HBM], out_specs=[vmem, HBM],
        scratch_shapes=[pltpu.SemaphoreType.DMA((7,)), pltpu.SemaphoreType.DMA((7,)), pltpu.SemaphoreType.DMA,
                        pltpu.SemaphoreType.DMA((3,)), pltpu.SemaphoreType.DMA((3,))],
        compiler_params=pltpu.CompilerParams(vmem_limit_bytes=VMEM_LIMIT))(blk, exchange)


def _add_kept(g, recv, tr):
    s, _, rh, cols = g.shape

    def body(c_ref, a_ref, b_ref, o_ref):
        o_ref[...] = (a_ref[...].astype(F32) + b_ref[...].astype(F32)).astype(o_ref.dtype)

    spec = pl.BlockSpec((None, tr, cols), lambda si, i, c_ref: (si, i, 0))
    return pl.pallas_call(
        body, name="add_kept", out_shape=jax.ShapeDtypeStruct((s, rh, cols), BF16),
        grid_spec=pltpu.PrefetchScalarGridSpec(
            num_scalar_prefetch=1, grid=(s, rh // tr),
            in_specs=[pl.BlockSpec((None, None, tr, cols), lambda si, i, c_ref: (si, c_ref[0], i, 0)), spec],
            out_specs=spec),
        compiler_params=_cp(("arbitrary", "arbitrary")))(lax.axis_index("c").astype(jnp.int32).reshape(1), g, recv)


def _sum_slots(p, tr, name):
    s, rows, cols = p.shape

    def body(p_ref, o_ref):
        acc = p_ref[0].astype(F32)
        for k in range(1, s):
            acc = acc + p_ref[k].astype(F32)
        o_ref[...] = acc

    return pl.pallas_call(
        body, name=name, grid=(rows // tr,), in_specs=[pl.BlockSpec((s, tr, cols), lambda i: (0, i, 0))],
        out_specs=_row_spec(tr, cols), out_shape=jax.ShapeDtypeStruct((rows, cols), F32),
        compiler_params=_cp(("arbitrary",)))(p)


def _adamw_call(w, g, m, v, name):
    rows, cols = w.shape
    tr = _tile(rows, 512) if rows % 512 == 0 else _tile(rows, 128)
    c1 = 1.0 - ADAM_B1 ** ADAM_STEP
    c2 = 1.0 - ADAM_B2 ** ADAM_STEP

    def body(w_ref, g_ref, m_ref, v_ref, d_ref, nm_ref, nv_ref):
        gv = g_ref[...]
        nm = ADAM_B1 * m_ref[...] + (1.0 - ADAM_B1) * gv
        nv = ADAM_B2 * v_ref[...] + (1.0 - ADAM_B2) * jnp.square(gv)
        d_ref[...] = -ADAM_LR * ((nm / c1) / (jnp.sqrt(nv / c2) + ADAM_EPS) + ADAM_WD * w_ref[...])
        nm_ref[...] = nm
        nv_ref[...] = nv

    spec = _row_spec(tr, cols)
    o = jax.ShapeDtypeStruct((rows, cols), F32)
    return pl.pallas_call(body, name=name, grid=(rows // tr,), in_specs=[spec] * 4, out_specs=[spec] * 3,
                          out_shape=[o, o, o], compiler_params=_cp(("arbitrary",)))(w, g, m, v)


WEIGHTS = ('norm1_g', 'w_in', 'sgu_norm_g', 'sgu_w', 'sgu_b', 's5_lambda_re', 's5_lambda_im', 's5_log_dt',
           's5_b_re', 's5_b_im', 's5_c_re', 's5_c_im', 's5_d', 's5_glu_w', 's5_glu_b', 'lru_conv_w',
           'lru_conv_b', 'lru_wa', 'lru_ba', 'lru_wx', 'lru_bx', 'lru_lambda', 'fox_fgate_b', 'mix_norm_g',
           'w_out', 'norm2_g', 'w_mlp_in', 'w_mlp_out', 'final_g')
N_W = len(WEIGHTS)


def _pack_shards(shards, dtype, names=tuple(SHARD_SHAPE), rows=LAYER_ROWS):
    parts = []
    for n in names:
        lead = shards[n].shape[:-2]
        flat = shards[n].reshape(*lead, -1).astype(dtype)
        flat = jnp.pad(flat, [(0, 0)] * len(lead) + [(0, SHARD_ROWS[n] * PACK_COLS - flat.shape[-1])])
        parts.append(flat.reshape(*lead, SHARD_ROWS[n], PACK_COLS))
    lead = parts[0].shape[:-2]
    used = sum(SHARD_ROWS[n] for n in names)
    if rows > used:
        parts.append(jnp.zeros((*lead, rows - used, PACK_COLS), dtype))
    return jnp.concatenate(parts, axis=-2)


def _unpack_shards(buf, names=tuple(SHARD_SHAPE)):
    lead = buf.shape[:-2]
    out = {}
    for n in names:
        s0, s1 = SHARD_SHAPE[n]
        rows, off = SHARD_ROWS[n], SHARD_OFF[n]
        flat = buf[..., off:off + rows, :].reshape(*lead, rows * PACK_COLS)
        out[n] = flat[..., :s0 * s1].reshape(*lead, s0, s1)
    return out


def _join_chips(g, axis):
    _, d, s0, s1 = g.shape
    if axis == 0:
        return g.transpose(1, 0, 2, 3).reshape(d, 4 * s0, s1)
    return g.transpose(1, 2, 0, 3).reshape(d, s0, 4 * s1)


def _split_chips(w, axis):
    d = w.shape[0]
    if axis == 0:
        return w.reshape(d, 4, w.shape[1] // 4, w.shape[2]).transpose(1, 0, 2, 3)
    return w.reshape(d, w.shape[1], 4, w.shape[2] // 4).transpose(2, 0, 1, 3)


def _flat_rows(shape):
    return -(-math.prod(shape) // PACK_COLS)


def _pack_flat(arrs, rows, dtype=F32):
    parts = []
    for a in arrs:
        flat = a.reshape(-1).astype(dtype)
        r = _flat_rows(a.shape)
        parts.append(jnp.pad(flat, (0, r * PACK_COLS - flat.shape[0])).reshape(r, PACK_COLS))
    used = sum(p.shape[0] for p in parts)
    parts.append(jnp.zeros((rows - used, PACK_COLS), dtype))
    return jnp.concatenate(parts, axis=0)


def _split3(s):
    hi = s.astype(BF16).astype(F32)
    mid = (s - hi).astype(BF16).astype(F32)
    return jnp.stack([hi, mid, s - hi - mid])


def _unpack_flat(buf, shapes):
    out, off = [], 0
    for s in shapes:
        r = _flat_rows(s)
        out.append(buf[off:off + r].reshape(-1)[:math.prod(s)].reshape(s))
        off += r
    return out


def _write_rows(buf, src, row_offset):
    n, r, cols = src.shape

    def body(s_ref, b_ref, o_ref):
        o_ref[...] = s_ref[...].astype(o_ref.dtype)

    blk = (None, r, cols)
    return pl.pallas_call(
        body, name="write_rows", grid=(n,),
        in_specs=[pl.BlockSpec(blk, lambda s: (s, 0, 0)), pl.BlockSpec(memory_space=pl.ANY)],
        out_specs=pl.BlockSpec(blk, lambda s: (s, row_offset // r, 0)),
        out_shape=jax.ShapeDtypeStruct(buf.shape, buf.dtype), input_output_aliases={1: 0},
        compiler_params=_cp(("arbitrary",)))(src, buf)


def _write_glue(send, glue):
    return _write_rows(send, glue, SMALL_OFF)


GLUE_PACKED = ('w_in', 's5_glu_w', 'lru_conv_w')
REDUCE_ROWS = LAYER_ROWS // 4


def _pack_weights(w):
    buf = lax.empty((DEPTH, LAYER_ROWS, PACK_COLS), BF16)
    for n in ('w_mlp_in', 'w_mlp_out', 'w_out'):
        buf = _write_rows(buf, w[n], SHARD_OFF[n])
    return _write_rows(buf, _pack_shards(w, BF16, names=GLUE_PACKED, rows=GLUE_ROWS), SMALL_OFF)


def _layer_weights(gathered):
    parts = _unpack_shards(gathered[:, None], GLUE_PACKED)
    joined = {n: _join_chips(g, SHARDED_AXIS[n])[0] for n, g in parts.items()}
    joined['w_in'] = jnp.pad(joined['w_in'], ((0, 0), (0, D_IN_PAD - D_IN_PROJ)))
    return joined


def _reduce_start(send):
    halves = send.reshape(4, 2, LAYER_ROWS // 2, PACK_COLS)
    return _add_kept(halves, _pair_exchange(halves), REDUCE_ROWS)


def _reduce_finish(chip_sum, received):
    return _pair_share(_sum_chips(chip_sum, received, REDUCE_ROWS)).reshape(LAYER_ROWS, PACK_COLS)


def _forward_backward(x, target, final_g, shards, rep):
    norm_p = [{n: rep[n][l] for n in ('norm1_g', 'norm2_g', 'mix_norm_g')} for l in range(DEPTH)]
    prepared, prepare_vjp = jax.vjp(_prepare, {n: rep[n] for n in PREPARED_FROM})
    gathered = _allgather_shards(shards[0], EARLY_ROWS)
    (h,) = _rowwise(_f_rms, [x], [norm_p[0]["norm1_g"][None, :]], [D_MODEL], name="rms_f", tr=512, dtype=BF16)
    layers = []
    for l in range(DEPTH):
        lw = _layer_weights(gathered)
        mix_p = ({n: a[l] for n, a in prepared.items()}, lw['s5_glu_w'].astype(F32), lw['lru_conv_w'].astype(F32))
        last = l + 1 == DEPTH
        x, h, res, gathered = _layer_fwd(
            x, h, gathered, lw['w_in'], norm_p[l], mix_p, None if last else shards[l + 1],
            shards[l], None if last else norm_p[l + 1]["norm1_g"])
        layers.append((res, lw))
    loss_part, g, d_final = _loss_call(x, final_g[None, :], target)

    norms, d_prepared, reduced, later_send = [None] * DEPTH, [None] * DEPTH, [None] * DEPTH, None
    for l in reversed(range(DEPTH)):
        res, lw = layers[l]
        g, norms[l], (d_prepared[l], d_glu_w, d_conv_w), d_w_in, send, later_reduced = _layer_bwd(
            g, res, lw['w_in'], norm_p[l], later_send)
        if later_send is not None:
            reduced[l + 1] = later_reduced
        mine = {'w_in': d_w_in[:, :D_IN_PROJ], 's5_glu_w': d_glu_w, 'lru_conv_w': d_conv_w}
        glue = _pack_shards({n: _split_chips(a[None], SHARDED_AXIS[n]) for n, a in mine.items()}, BF16,
                            names=GLUE_PACKED, rows=GLUE_ROWS)[:, 0]
        later_send = _write_glue(send, glue)
    stack = lambda per_layer: {n: jnp.stack([per_layer[l][n] for l in range(DEPTH)]) for n in per_layer[0]}
    (d_rep,) = prepare_vjp(stack(d_prepared))
    return loss_part, g, d_final, dict(d_rep, **stack(norms)), reduced[1:], _reduce_start(later_send)


def _step(*args):
    x, target = args[0], args[1 + N_W]
    w = dict(zip(WEIGHTS, args[1:1 + N_W]))
    m = dict(zip(WEIGHTS, args[2 + N_W:2 + 2 * N_W]))
    v = dict(zip(WEIGHTS, args[2 + 2 * N_W:2 + 3 * N_W]))
    small = [n for n in WEIGHTS if n not in SHARD_SHAPE]

    shards = _pack_weights({n: w[n] for n in SHARD_SHAPE})
    loss_part, dx, d_final, dw, reduced_above, chip_sum0 = _forward_backward(
        x[0], target[0], w['final_g'], shards, {n: w[n] for n in small})

    small_g = [d_final.reshape(-1) if n == 'final_g' else dw[n] for n in small]
    small_rows = -(-(sum(_flat_rows(w[n].shape) for n in small) + 1) // 128) * 128
    mine = _pack_flat(small_g + [_split3(loss_part[0, 0])], small_rows, BF16)
    everyone, received0 = _allgather_all(mine, chip_sum0)
    g_shard = _unpack_shards(jnp.stack([_reduce_finish(chip_sum0, received0)] + reduced_above))
    small_sum = _sum_slots(everyone, 128, "sum_devices")
    *g_small, loss = _unpack_flat(small_sum, [w[n].shape for n in small] + [(3,)])
    loss = jnp.sum(loss)

    grads, delta, new_m, new_v = {}, {}, {}, {}
    for n in SHARD_SHAPE:
        shp = w[n].shape
        v2 = lambda a: a.reshape(-1, shp[-1])
        res = _adamw_call(v2(w[n]), v2(g_shard[n]), v2(m[n]), v2(v[n]), "adamw_" + n)
        grads[n] = g_shard[n]
        delta[n], new_m[n], new_v[n] = (r.reshape(shp) for r in res)
    for n, g in zip(small, g_small):
        shp = w[n].shape
        v2 = lambda a: a.reshape(-1, shp[-1])
        res = _adamw_call(v2(w[n]), v2(g), v2(m[n]), v2(v[n]), "adamw_" + n)
        grads[n] = g
        delta[n], new_m[n], new_v[n] = (r.reshape(shp) for r in res)

    return (loss, dx[None], *[grads[n] for n in WEIGHTS], *[delta[n] for n in WEIGHTS],
            *[new_m[n] for n in WEIGHTS], *[new_v[n] for n in WEIGHTS])


def kernel(x, norm1_g, w_in, sgu_norm_g, sgu_w, sgu_b, s5_lambda_re, s5_lambda_im, s5_log_dt, s5_b_re, s5_b_im, s5_c_re, s5_c_im, s5_d, s5_glu_w, s5_glu_b, lru_conv_w, lru_conv_b, lru_wa, lru_ba, lru_wx, lru_bx, lru_lambda, fox_fgate_b, mix_norm_g, w_out, norm2_g, w_mlp_in, w_mlp_out, final_g, loss_target, m_norm1_g, m_w_in, m_sgu_norm_g, m_sgu_w, m_sgu_b, m_s5_lambda_re, m_s5_lambda_im, m_s5_log_dt, m_s5_b_re, m_s5_b_im, m_s5_c_re, m_s5_c_im, m_s5_d, m_s5_glu_w, m_s5_glu_b, m_lru_conv_w, m_lru_conv_b, m_lru_wa, m_lru_ba, m_lru_wx, m_lru_bx, m_lru_lambda, m_fox_fgate_b, m_mix_norm_g, m_w_out, m_norm2_g, m_w_mlp_in, m_w_mlp_out, m_final_g, v_norm1_g, v_w_in, v_sgu_norm_g, v_sgu_w, v_sgu_b, v_s5_lambda_re, v_s5_lambda_im, v_s5_log_dt, v_s5_b_re, v_s5_b_im, v_s5_c_re, v_s5_c_im, v_s5_d, v_s5_glu_w, v_s5_glu_b, v_lru_conv_w, v_lru_conv_b, v_lru_wa, v_lru_ba, v_lru_wx, v_lru_bx, v_lru_lambda, v_fox_fgate_b, v_mix_norm_g, v_w_out, v_norm2_g, v_w_mlp_in, v_w_mlp_out, v_final_g):
    return _step(x, norm1_g, w_in, sgu_norm_g, sgu_w, sgu_b, s5_lambda_re, s5_lambda_im, s5_log_dt, s5_b_re, s5_b_im, s5_c_re, s5_c_im, s5_d, s5_glu_w, s5_glu_b, lru_conv_w, lru_conv_b, lru_wa, lru_ba, lru_wx, lru_bx, lru_lambda, fox_fgate_b, mix_norm_g, w_out, norm2_g, w_mlp_in, w_mlp_out, final_g, loss_target, m_norm1_g, m_w_in, m_sgu_norm_g, m_sgu_w, m_sgu_b, m_s5_lambda_re, m_s5_lambda_im, m_s5_log_dt, m_s5_b_re, m_s5_b_im, m_s5_c_re, m_s5_c_im, m_s5_d, m_s5_glu_w, m_s5_glu_b, m_lru_conv_w, m_lru_conv_b, m_lru_wa, m_lru_ba, m_lru_wx, m_lru_bx, m_lru_lambda, m_fox_fgate_b, m_mix_norm_g, m_w_out, m_norm2_g, m_w_mlp_in, m_w_mlp_out, m_final_g, v_norm1_g, v_w_in, v_sgu_norm_g, v_sgu_w, v_sgu_b, v_s5_lambda_re, v_s5_lambda_im, v_s5_log_dt, v_s5_b_re, v_s5_b_im, v_s5_c_re, v_s5_c_im, v_s5_d, v_s5_glu_w, v_s5_glu_b, v_lru_conv_w, v_lru_conv_b, v_lru_wa, v_lru_ba, v_lru_wx, v_lru_bx, v_lru_lambda, v_fox_fgate_b, v_mix_norm_g, v_w_out, v_norm2_g, v_w_mlp_in, v_w_mlp_out, v_final_g)
```

```python
import math

import jax
import jax.numpy as jnp
from jax import lax
from jax.experimental import pallas as pl
from jax.experimental.pallas import tpu as pltpu

F32 = jnp.float32
BF16 = jnp.bfloat16

DEPTH = 4
D_MODEL = 1024
MIXER_WIDTH = 256
SGU_CHUNK = 128
N_HEADS = 4
HEAD_DIM = 64
S5_GROUPS = 16
S5_GROUP = 16
S5_STATE = 64
LRU_C = 8.0
RMS_EPS = 1e-6
D_IN_PROJ = 8 * MIXER_WIDTH + N_HEADS
D_IN_PAD = 8 * MIXER_WIDTH + 128
ADAM_LR, ADAM_B1, ADAM_B2, ADAM_EPS, ADAM_WD, ADAM_STEP = 0.001, 0.9, 0.999, 1e-08, 0.01, 10

V7X_VMEM_BYTES = 64 * 1024 * 1024
VMEM_LIMIT = V7X_VMEM_BYTES - 8 * 1024 * 1024
NEG = -1e30
MESH = pl.DeviceIdType.MESH


def _cp(sem=None, **kw):
    return pltpu.CompilerParams(dimension_semantics=sem, vmem_limit_bytes=VMEM_LIMIT, **kw)


def _full_spec(a):
    nd = a.ndim
    return pl.BlockSpec(a.shape, lambda *_: (0,) * nd)


def _tile(n, pref=512):
    return pref if n % pref == 0 else n


def _dot(a, b, ca, cb):
    return lax.dot_general(a.astype(BF16), b.astype(BF16), (((ca,), (cb,)), ((), ())),
                           preferred_element_type=F32)


def _matmul(name, grid, a, a_spec, b, b_spec, dims, outs, *, extras=(), epilogue=None, into=None, summed=0,
            acc_shape=None, rider=None):
    nk = grid[2]
    n_ex, n_out = len(extras), len(outs)
    tm_tn = acc_shape or tuple(d for d in outs[0][1].block_shape if d is not None)[-2:]
    n_in = 2 + n_ex + (into is not None)

    def body(*refs):
        a_ref, b_ref = refs[0], refs[1]
        ex_refs = refs[2:2 + n_ex]
        o_refs = refs[n_in + (rider is not None):n_in + (rider is not None) + n_out]
        if rider is not None:
            start, finish_rider = rider[3](refs[n_in], refs[n_in + 1 + n_out], refs[-2], refs[-1])
            step = [pl.program_id(d) for d in range(3)]
            pl.when((step[0] == 0) & (step[1] == 0) & (step[2] == 0))(start)
        if summed:
            @pl.when((pl.program_id(0) == 0) & (pl.program_id(1) == 0) & (pl.program_id(2) == 0))
            def _():
                for o_ref in o_refs[n_out - summed:]:
                    o_ref[...] = jnp.zeros_like(o_ref)

        def finish(val):
            res = epilogue(val, *[e[...] for e in ex_refs]) if epilogue else (val,)
            for idx, (o_ref, r) in enumerate(zip(o_refs, res)):
                if idx >= n_out - summed:
                    o_ref[...] += r
                else:
                    o_ref[...] = r.astype(o_ref.dtype)

        if nk == 1:
            finish(_dot(a_ref[...], b_ref[...], *dims))
        else:
            acc = refs[n_in + (rider is not None) + n_out + (rider is not None)]
            kk = pl.program_id(2)

            @pl.when(kk == 0)
            def _():
                acc[...] = jnp.zeros_like(acc)

            acc[...] += _dot(a_ref[...], b_ref[...], *dims)

            @pl.when(kk == nk - 1)
            def _():
                finish(acc[...])

        if rider is not None:
            pl.when((step[0] == grid[0] - 1) & (step[1] == grid[1] - 1) & (step[2] == grid[2] - 1))(finish_rider)

    ins = [a, b] + [e[0] for e in extras]
    specs = [a_spec, b_spec] + [e[1] for e in extras]
    aliases = {}
    if into is not None:
        aliases = {len(ins): 0}
        ins.append(into)
        specs.append(pl.BlockSpec(memory_space=pl.ANY))
    out_specs, out_shape = [o[1] for o in outs], [o[0] for o in outs]
    scratch = [pltpu.VMEM(tm_tn, F32)] if nk > 1 else []
    if rider is not None:
        if rider[2]:
            aliases[len(ins)] = n_out
        ins.append(rider[0])
        specs.append(HBM)
        out_specs.append(HBM)
        out_shape.append(rider[1])
        scratch += [pltpu.SemaphoreType.DMA] * 2
        name += "_rider"
    res = pl.pallas_call(
        body, name=name, grid=grid, in_specs=specs, out_specs=out_specs, out_shape=out_shape,
        scratch_shapes=scratch, input_output_aliases=aliases,
        compiler_params=_cp(("arbitrary", "arbitrary", "arbitrary")))(*ins)
    return res[0] if len(res) == 1 else res


@jax.custom_vjp
def _bdot(a, b):
    return _dot(a, b, 1, 0)


def _bdot_fwd(a, b):
    return _dot(a, b, 1, 0), (a, b)


def _bdot_bwd(r, g):
    a, b = r
    return _dot(g, b, 1, 1), _dot(a, g, 0, 0)


_bdot.defvjp(_bdot_fwd, _bdot_bwd)


def _row_spec(tr, w):
    return pl.BlockSpec((tr, w), lambda i: (i, 0))


def _rowwise(fn, rows, pars, outs, *, name, tr, dtype=F32):
    t = rows[0].shape[0]
    n_in = len(rows) + len(pars)

    def body(*refs):
        res = fn(*[r[...] for r in refs[:n_in]])
        for o_ref, v in zip(refs[n_in:], res):
            o_ref[...] = v.astype(o_ref.dtype)

    return pl.pallas_call(
        body, name=name, grid=(t // tr,),
        in_specs=[_row_spec(tr, r.shape[1]) for r in rows] + [_full_spec(p) for p in pars],
        out_specs=[_row_spec(tr, w) for w in outs],
        out_shape=[jax.ShapeDtypeStruct((t, w), dtype) for w in outs],
        compiler_params=_cp(("arbitrary",)))(*rows, *pars)


def _rowwise_vjp(fn, rows, pars, cots, *, name, tr, add=None):
    t = rows[0].shape[0]
    nr, npar = len(rows), len(pars)
    cots = list(cots) + ([add] if add is not None else [])
    nc = len(cots)

    def body(*refs):
        vals = [r[...] for r in refs[:nr + npar]]
        cts = [c[...] for c in refs[nr + npar:nr + npar + nc]]
        douts = refs[nr + npar + nc:]
        extra = cts.pop() if add is not None else None
        _, vjp = jax.vjp(fn, *vals)
        grads = list(vjp(tuple(cts)))
        if extra is not None:
            grads[0] = grads[0] + extra
        for kk in range(nr):
            douts[kk][...] = grads[kk]

        @pl.when(pl.program_id(0) == 0)
        def _():
            for kk in range(npar):
                douts[nr + kk][...] = jnp.zeros_like(douts[nr + kk])

        for kk in range(npar):
            douts[nr + kk][...] += grads[nr + kk]

    return pl.pallas_call(
        body, name=name, grid=(t // tr,),
        in_specs=[_row_spec(tr, r.shape[1]) for r in rows] + [_full_spec(p) for p in pars]
        + [_row_spec(tr, c.shape[1]) for c in cots],
        out_specs=[_row_spec(tr, r.shape[1]) for r in rows] + [_full_spec(p) for p in pars],
        out_shape=[jax.ShapeDtypeStruct(r.shape, F32) for r in rows]
        + [jax.ShapeDtypeStruct(p.shape, F32) for p in pars],
        compiler_params=_cp(("arbitrary",)))(*rows, *pars, *cots)


def _make_rw(fn, name, tr, nr, outs):
    @jax.custom_vjp
    def f(*args):
        return tuple(_rowwise(fn, args[:nr], args[nr:], outs, name=name + "_f", tr=tr))

    def fwd(*args):
        return f(*args), args

    def bwd(args, cts):
        return tuple(_rowwise_vjp(fn, args[:nr], args[nr:], list(cts), name=name + "_b", tr=tr))

    f.defvjp(fwd, bwd)
    return f


def _rms(x, g):
    return x * lax.rsqrt(jnp.mean(jnp.square(x), axis=-1, keepdims=True) + RMS_EPS) * g


def _f_rms(x, g):
    return (_rms(x, g),)


def _f_sgu(au, av, ng, w0, w1, w2, w3, bfull):
    u = jax.nn.gelu(au)
    v = _rms(jax.nn.gelu(av), ng)
    tri = lax.broadcasted_iota(jnp.int32, (SGU_CHUNK, SGU_CHUNK), 0) >= lax.broadcasted_iota(
        jnp.int32, (SGU_CHUNK, SGU_CHUNK), 1)
    head = lax.broadcasted_iota(jnp.int32, v.shape, 1) // HEAD_DIM
    mixed = bfull
    for h, w in enumerate((w0, w1, w2, w3)):
        mixed = mixed + _bdot(jnp.where(tri, w, 0.0), jnp.where(head == h, v, 0.0))
    return (u * mixed,)


def _f_s5disc(lam_re, lam_im, log_dt, b_re, b_im):
    dt = jnp.exp(log_dt)
    mag = jnp.exp(lam_re * dt)
    abar_re = mag * jnp.cos(lam_im * dt)
    abar_im = mag * jnp.sin(lam_im * dt)
    denom = jnp.square(lam_re) + jnp.square(lam_im)
    num_re = abar_re - 1.0
    num_im = abar_im
    fac_re = (num_re * lam_re + num_im * lam_im) / denom
    fac_im = (num_im * lam_re - num_re * lam_im) / denom
    return abar_re, abar_im, fac_re * b_re - fac_im * b_im, fac_re * b_im + fac_im * b_re


def _f_s5post(s_re, s_im, u, c_re, c_im, d, gw, gb):
    y = _bdot(s_re, c_re) - _bdot(s_im, c_im) + d * u
    y = jax.nn.gelu(y)
    return (y * jax.nn.sigmoid(_bdot(y, gw) + gb),)


def _f_lrupre(xc, wa, ba, wx, bx, lam):
    r = jax.nn.sigmoid(_bdot(xc, wa) + ba)
    i = jax.nn.sigmoid(_bdot(xc, wx) + bx)
    log_a = -LRU_C * r * jax.nn.softplus(-lam)
    a = jnp.exp(log_a)
    one_minus_a2 = -jnp.tanh(log_a) * (jnp.exp(2.0 * log_a) + 1.0)
    return a, jnp.sqrt(one_minus_a2) * (i * xc)


def _f_lrupost(h, gate):
    return (h * jax.nn.gelu(gate),)


def _f_logsig(zf, bf):
    return (jax.nn.log_sigmoid(zf + bf),)


def _f_gnorm(ya, yb, yc, yd, g):
    def n(y):
        return y * lax.rsqrt(jnp.mean(jnp.square(y), axis=-1, keepdims=True) + RMS_EPS)
    return (jnp.concatenate([n(ya), n(yb), n(yc), n(yd)], axis=1) * g,)


sgu_mix = _make_rw(_f_sgu, "sgu", SGU_CHUNK, 2, [MIXER_WIDTH])
s5_disc = _make_rw(_f_s5disc, "s5disc", S5_GROUPS * S5_GROUP, 5, [S5_STATE] * 4)
s5_post = _make_rw(_f_s5post, "s5post", 256, 3, [MIXER_WIDTH])
lru_pre = _make_rw(_f_lrupre, "lrupre", 512, 1, [MIXER_WIDTH, MIXER_WIDTH])
lru_post = _make_rw(_f_lrupost, "lrupost", 512, 2, [MIXER_WIDTH])
log_sig = _make_rw(_f_logsig, "logsig", 512, 1, [128])


SCAN_TILE = 512


def _prev_spec(c, nt, rev):
    per = SCAN_TILE // 8
    if rev:
        return pl.BlockSpec((8, c), lambda i: (jnp.maximum((nt - 1 - i) * per - 1, 0), 0))
    return pl.BlockSpec((8, c), lambda i: (jnp.maximum(i * per - 1, 0), 0))


SCAN_STEPS = (1, 2, 4)


def _cmul(ar, ai, br, bi):
    return ar * br - ai * bi, ar * bi + ai * br


def _rows_down(x, k, fill, rowid):
    return jnp.where(rowid >= k, pltpu.roll(x, k, 0), fill)


def _rows_up(x, k, fill, rowid):
    return jnp.where(rowid < 8 - k, pltpu.roll(x, 8 - k, 0), fill)


def _powers(ar, ai):
    pw = [(ar, ai)]
    for _ in range(7):
        pw.append(_cmul(*pw[-1], ar, ai))
    return pw


def _block(i):
    return pl.ds(pl.multiple_of(i * 8, 8), 8)


def _row_before(ref, i, edge):
    return jnp.where(i == 0, edge, ref[pl.ds(jnp.maximum(i * 8 - 1, 0), 1), :])


def _lti_fwd_call(u, w_re, w_im, a_re, a_im):
    t, kdim = u.shape
    c = w_re.shape[1]
    tt = SCAN_TILE

    def body(u_ref, wr_ref, wi_ref, ar_ref, ai_ref, sr_ref, si_ref, br_ref, bi_ref, cr, ci):
        @pl.when(pl.program_id(0) == 0)
        def _():
            cr[...] = jnp.zeros_like(cr)
            ci[...] = jnp.zeros_like(ci)

        br_ref[...] = _dot(u_ref[...], wr_ref[...], 1, 0)
        bi_ref[...] = _dot(u_ref[...], wi_ref[...], 1, 0)
        pw = _powers(ar_ref[...], ai_ref[...])
        apr = jnp.concatenate([p[0] for p in pw], axis=0)
        api = jnp.concatenate([p[1] for p in pw], axis=0)
        rowid = lax.broadcasted_iota(jnp.int32, (8, c), 0)

        def block(i, carry):
            xr, xi = br_ref[_block(i), :], bi_ref[_block(i), :]
            for k in SCAN_STEPS:
                dr, di = _cmul(*pw[k - 1], _rows_down(xr, k, 0.0, rowid), _rows_down(xi, k, 0.0, rowid))
                xr, xi = xr + dr, xi + di
            dr, di = _cmul(apr, api, *carry)
            xr, xi = xr + dr, xi + di
            sr_ref[_block(i), :] = xr
            si_ref[_block(i), :] = xi
            return xr[7:8, :], xi[7:8, :]

        hr, hi = lax.fori_loop(0, tt // 8, block, (cr[...], ci[...]), unroll=2)
        cr[...] = hr
        ci[...] = hi

    row = pl.BlockSpec((tt, c), lambda i: (i, 0))
    par = pl.BlockSpec((1, c), lambda i: (0, 0))
    return pl.pallas_call(
        body, name="lti_scan_f", grid=(t // tt,),
        in_specs=[pl.BlockSpec((tt, kdim), lambda i: (i, 0)), _full_spec(w_re), _full_spec(w_im), par, par],
        out_specs=[row, row], out_shape=[jax.ShapeDtypeStruct((t, c), F32)] * 2,
        scratch_shapes=[pltpu.VMEM((tt, c), F32)] * 2 + [pltpu.VMEM((1, c), F32)] * 2,
        compiler_params=_cp(("arbitrary",)))(u, w_re, w_im, a_re, a_im)


def _lti_bwd_call(u, w_re, w_im, a_re, a_im, s_re, s_im, g_re, g_im):
    t, c = g_re.shape
    kdim = u.shape[1]
    tt = SCAN_TILE
    nt = t // tt
    nb = tt // 8

    def body(u_ref, wr_ref, wi_ref, ar_ref, ai_ref, sr_ref, si_ref, pr_ref, pi_ref, gr_ref, gi_ref,
             du_ref, dwr_ref, dwi_ref, dar_ref, dai_ref, or_ref, oi_ref, cr, ci):
        ti = pl.program_id(0)

        @pl.when(ti == 0)
        def _():
            cr[...] = jnp.zeros_like(cr)
            ci[...] = jnp.zeros_like(ci)
            dar_ref[...] = jnp.zeros_like(dar_ref)
            dai_ref[...] = jnp.zeros_like(dai_ref)
            dwr_ref[...] = jnp.zeros_like(dwr_ref)
            dwi_ref[...] = jnp.zeros_like(dwi_ref)

        pw = _powers(ar_ref[...], -ai_ref[...])
        tpr = jnp.concatenate([p[0] for p in reversed(pw)], axis=0)
        tpi = jnp.concatenate([p[1] for p in reversed(pw)], axis=0)
        rowid = lax.broadcasted_iota(jnp.int32, (8, c), 0)
        first = ti == nt - 1
        edge_r = jnp.where(first, 0.0, pr_ref[7:8, :])
        edge_i = jnp.where(first, 0.0, pi_ref[7:8, :])

        def block(kk, carry):
            i = nb - 1 - kk
            gr_c, gi_c, acc_r, acc_i = carry
            xr, xi = gr_ref[_block(i), :], gi_ref[_block(i), :]
            for k in SCAN_STEPS:
                dr, di = _cmul(*pw[k - 1], _rows_up(xr, k, 0.0, rowid), _rows_up(xi, k, 0.0, rowid))
                xr, xi = xr + dr, xi + di
            dr, di = _cmul(tpr, tpi, gr_c, gi_c)
            xr, xi = xr + dr, xi + di
            or_ref[_block(i), :] = xr
            oi_ref[_block(i), :] = xi
            spr = _rows_down(sr_ref[_block(i), :], 1, _row_before(sr_ref, i, edge_r), rowid)
            spi = _rows_down(si_ref[_block(i), :], 1, _row_before(si_ref, i, edge_i), rowid)
            return xr[0:1, :], xi[0:1, :], acc_r + spr * xr + spi * xi, acc_i + spr * xi - spi * xr

        zero = jnp.zeros((8, c), F32)
        gr_c, gi_c, acc_r, acc_i = lax.fori_loop(0, nb, block, (cr[...], ci[...], zero, zero), unroll=2)
        cr[...] = gr_c
        ci[...] = gi_c
        dar_ref[...] += jnp.sum(acc_r, axis=0, keepdims=True)
        dai_ref[...] += jnp.sum(acc_i, axis=0, keepdims=True)
        du_ref[...] = _dot(or_ref[...], wr_ref[...], 1, 1) + _dot(oi_ref[...], wi_ref[...], 1, 1)
        dwr_ref[...] += _dot(u_ref[...], or_ref[...], 0, 0)
        dwi_ref[...] += _dot(u_ref[...], oi_ref[...], 0, 0)

    row = pl.BlockSpec((tt, c), lambda i: (nt - 1 - i, 0))
    row_u = pl.BlockSpec((tt, kdim), lambda i: (nt - 1 - i, 0))
    par = pl.BlockSpec((1, c), lambda i: (0, 0))
    prev = _prev_spec(c, nt, True)
    return pl.pallas_call(
        body, name="lti_scan_b", grid=(nt,),
        in_specs=[row_u, _full_spec(w_re), _full_spec(w_im), par, par, row, row, prev, prev, row, row],
        out_specs=[row_u, _full_spec(w_re), _full_spec(w_im), par, par],
        out_shape=[jax.ShapeDtypeStruct((t, kdim), F32), jax.ShapeDtypeStruct(w_re.shape, F32),
                   jax.ShapeDtypeStruct(w_im.shape, F32)] + [jax.ShapeDtypeStruct((1, c), F32)] * 2,
        scratch_shapes=[pltpu.VMEM((tt, c), F32)] * 2 + [pltpu.VMEM((1, c), F32)] * 2,
        compiler_params=_cp(("arbitrary",)))(u, w_re, w_im, a_re, a_im, s_re, s_im, s_re, s_im, g_re, g_im)


@jax.custom_vjp
def lti_scan(u, w_re, w_im, a_re, a_im):
    return tuple(_lti_fwd_call(u, w_re, w_im, a_re, a_im))


def _lti_scan_fwd(u, w_re, w_im, a_re, a_im):
    s_re, s_im = _lti_fwd_call(u, w_re, w_im, a_re, a_im)
    return (s_re, s_im), (u, w_re, w_im, a_re, a_im, s_re, s_im)


def _lti_scan_bwd(r, g):
    return tuple(_lti_bwd_call(*r, g[0], g[1]))


lti_scan.defvjp(_lti_scan_fwd, _lti_scan_bwd)


def _tv_fwd_call(a, b):
    t, c = b.shape
    tt = SCAN_TILE

    def body(a_ref, b_ref, h_ref, ch):
        @pl.when(pl.program_id(0) == 0)
        def _():
            ch[...] = jnp.zeros_like(ch)

        rowid = lax.broadcasted_iota(jnp.int32, (8, c), 0)

        def block(i, h):
            ab, x = a_ref[_block(i), :], b_ref[_block(i), :]
            for k in SCAN_STEPS:
                x = x + ab * _rows_down(x, k, 0.0, rowid)
                ab = ab * _rows_down(ab, k, 1.0, rowid)
            x = x + ab * h
            h_ref[_block(i), :] = x
            return x[7:8, :]

        ch[...] = lax.fori_loop(0, tt // 8, block, ch[...], unroll=2)

    row = pl.BlockSpec((tt, c), lambda i: (i, 0))
    return pl.pallas_call(
        body, name="tv_scan_f", grid=(t // tt,), in_specs=[row, row], out_specs=row,
        out_shape=jax.ShapeDtypeStruct((t, c), F32), scratch_shapes=[pltpu.VMEM((1, c), F32)],
        compiler_params=_cp(("arbitrary",)))(a, b)


def _tv_bwd_call(a, h, g):
    t, c = g.shape
    tt = SCAN_TILE
    nt = t // tt
    nb = tt // 8

    def body(a_ref, h_ref, p_ref, g_ref, da_ref, db_ref, cg, ca):
        ti = pl.program_id(0)

        @pl.when(ti == 0)
        def _():
            cg[...] = jnp.zeros_like(cg)
            ca[...] = jnp.zeros_like(ca)

        rowid = lax.broadcasted_iota(jnp.int32, (8, c), 0)
        edge = jnp.where(ti == nt - 1, 0.0, p_ref[7:8, :])

        def block(kk, carry):
            i = nb - 1 - kk
            gc, a_next = carry
            ab, x = a_ref[_block(i), :], g_ref[_block(i), :]
            cb = _rows_up(ab, 1, a_next, rowid)
            for k in SCAN_STEPS:
                x = x + cb * _rows_up(x, k, 0.0, rowid)
                cb = cb * _rows_up(cb, k, 1.0, rowid)
            x = x + cb * gc
            db_ref[_block(i), :] = x
            da_ref[_block(i), :] = x * _rows_down(h_ref[_block(i), :], 1, _row_before(h_ref, i, edge), rowid)
            return x[0:1, :], ab[0:1, :]

        gc, a_next = lax.fori_loop(0, nb, block, (cg[...], ca[...]), unroll=2)
        cg[...] = gc
        ca[...] = a_next

    row = pl.BlockSpec((tt, c), lambda i: (nt - 1 - i, 0))
    return pl.pallas_call(
        body, name="tv_scan_b", grid=(nt,), in_specs=[row, row, _prev_spec(c, nt, True), row],
        out_specs=[row, row], out_shape=[jax.ShapeDtypeStruct((t, c), F32)] * 2,
        scratch_shapes=[pltpu.VMEM((1, c), F32)] * 2, compiler_params=_cp(("arbitrary",)))(a, h, h, g)


@jax.custom_vjp
def tv_scan(a, b):
    return _tv_fwd_call(a, b)


def _tv_scan_fwd(a, b):
    h = _tv_fwd_call(a, b)
    return h, (a, h)


def _tv_scan_bwd(r, g):
    a, h = r
    return tuple(_tv_bwd_call(a, h, g))


tv_scan.defvjp(_tv_scan_fwd, _tv_scan_bwd)


CONV_K = 4
CONV_ROWS = 512


def _conv_fwd_call(x, w, b):
    t, c = x.shape

    def body(x_ref, w_ref, b_ref, o_ref, xp):
        xp[0:8, :] = jnp.zeros((8, c), F32)
        xp[8:, :] = x_ref[...]
        for blk in range(t // CONV_ROWS):
            base = blk * CONV_ROWS
            acc = jnp.broadcast_to(b_ref[...], (CONV_ROWS, c))
            for kk in range(CONV_K):
                acc = acc + w_ref[kk:kk + 1, :] * xp[base + 5 + kk:base + 5 + kk + CONV_ROWS, :]
            o_ref[base:base + CONV_ROWS, :] = acc

    return pl.pallas_call(
        body, name="conv_f", out_shape=jax.ShapeDtypeStruct((t, c), F32),
        scratch_shapes=[pltpu.VMEM((t + 8, c), F32)], compiler_params=_cp())(x, w, b)


def _conv_bwd_call(x, w, g):
    t, c = x.shape

    def body(x_ref, w_ref, g_ref, dx_ref, dw_ref, db_ref, xp, gp):
        xp[0:8, :] = jnp.zeros((8, c), F32)
        xp[8:, :] = x_ref[...]
        gp[0:t, :] = g_ref[...]
        gp[t:, :] = jnp.zeros((8, c), F32)
        dw = [jnp.zeros((1, c), F32) for _ in range(CONV_K)]
        db = jnp.zeros((1, c), F32)
        for blk in range(t // CONV_ROWS):
            base = blk * CONV_ROWS
            gb = g_ref[base:base + CONV_ROWS, :]
            acc = jnp.zeros((CONV_ROWS, c), F32)
            for kk in range(CONV_K):
                acc = acc + w_ref[kk:kk + 1, :] * gp[base + 3 - kk:base + 3 - kk + CONV_ROWS, :]
                dw[kk] = dw[kk] + jnp.sum(gb * xp[base + 5 + kk:base + 5 + kk + CONV_ROWS, :], axis=0, keepdims=True)
            db = db + jnp.sum(gb, axis=0, keepdims=True)
            dx_ref[base:base + CONV_ROWS, :] = acc
        for kk in range(CONV_K):
            dw_ref[kk:kk + 1, :] = dw[kk]
        db_ref[...] = db

    return pl.pallas_call(
        body, name="conv_b",
        out_shape=[jax.ShapeDtypeStruct((t, c), F32), jax.ShapeDtypeStruct((CONV_K, c), F32),
                   jax.ShapeDtypeStruct((1, c), F32)],
        scratch_shapes=[pltpu.VMEM((t + 8, c), F32)] * 2, compiler_params=_cp())(x, w, g)


@jax.custom_vjp
def causal_conv(x, w, b):
    return _conv_fwd_call(x, w, b)


def _causal_conv_fwd(x, w, b):
    return _conv_fwd_call(x, w, b), (x, w)


def _causal_conv_bwd(r, g):
    return tuple(_conv_bwd_call(r[0], r[1], g))


causal_conv.defvjp(_causal_conv_fwd, _causal_conv_bwd)


ATT_TILE = 512
ATT_SCALE = HEAD_DIM ** -0.5


def _head_lane(val, lane, h):
    return jnp.sum(jnp.where(lane == h, val, 0.0), axis=1, keepdims=True)


def _attn_fwd_call(qkv, c128, cr, next_shard=None, late=None):
    t, w = qkv.shape[0], MIXER_WIDTH
    tq = ATT_TILE
    nq = t // tq
    qkv3, cr4 = qkv.reshape(nq, tq, 3 * w), cr.reshape(N_HEADS, nq, 1, tq)
    fused = next_shard is not None
    both = late is not None

    def body(*refs):
        q_ref, k_ref, v_ref, c_ref, cr_ref = refs[:5]
        i, h = pl.program_id(0), pl.program_id(1)
        rest = list(refs[5:])
        shard_ref = rest.pop(0) if fused else None
        own_ref = rest.pop(0) if both else None
        if both:
            rest.pop(0)
        o_ref, lse_ref = rest.pop(0), rest.pop(0)
        gathers = []
        if fused:
            gathers.append(_allgather_copies(shard_ref, rest.pop(0), rest[-2 - 2 * both], rest[-1 - 2 * both], EARLY_ROWS))
        if both:
            gathers.append(_allgather_copies(own_ref, rest.pop(0), rest[-2], rest[-1], LATE_ROWS))
        for start, _, _ in gathers:
            pl.when((i == 0) & (h == 0))(start)
        hm = lax.broadcasted_iota(jnp.int32, (tq, w), 1) // HEAD_DIM == h
        lane = lax.broadcasted_iota(jnp.int32, (tq, 128), 1)
        qs = jnp.where(hm, q_ref[...] * ATT_SCALE, 0.0)
        cq = _head_lane(c_ref[...], lane, h)
        causal = lax.broadcasted_iota(jnp.int32, (tq, tq), 0) >= lax.broadcasted_iota(jnp.int32, (tq, tq), 1)

        def update(j, carry, diagonal):
            m, l, acc = carry
            s = _dot(qs, k_ref[j], 1, 1) - cr_ref[0, j]
            if diagonal:
                s = jnp.where(causal, s, NEG)
            m_new = jnp.maximum(m, cq + jnp.max(s, axis=1, keepdims=True))
            p = jnp.exp(s + (cq - m_new))
            alpha = jnp.exp(m - m_new)
            return m_new, alpha * l + jnp.sum(p, axis=1, keepdims=True), alpha * acc + _dot(p, v_ref[j], 1, 0)

        init = (jnp.full((tq, 1), NEG, F32), jnp.zeros((tq, 1), F32), jnp.zeros((tq, w), F32))
        carry = lax.fori_loop(0, i, lambda j, c: update(j, c, False), init)
        m, l, acc = update(i, carry, True)
        out = jnp.where(hm, acc / l, 0.0)
        lse = jnp.where(lane == h, m + jnp.log(l), 0.0)

        @pl.when(h == 0)
        def _():
            o_ref[...] = out
            lse_ref[...] = lse

        @pl.when(h > 0)
        def _():
            o_ref[...] += out
            lse_ref[...] += lse

        for _, forward, finish in gathers:
            pl.when((i == nq - 1) & (h == 0))(forward)
            pl.when((i == nq - 1) & (h == N_HEADS - 1))(finish)

    tile = pl.BlockSpec((tq, w), lambda i, h: (i, 0))
    tile_c = pl.BlockSpec((tq, 128), lambda i, h: (i, 0))
    rows = pl.BlockSpec((1, nq, 1, tq), lambda i, h: (h, 0, 0, 0))
    q_spec = pl.BlockSpec((None, tq, w), lambda i, h: (i, 0, 0))
    k_spec = pl.BlockSpec((nq, tq, w), lambda i, h: (0, 0, 1))
    v_spec = pl.BlockSpec((nq, tq, w), lambda i, h: (0, 0, 2))
    ins = [qkv3, qkv3, qkv3, c128, cr4]
    in_specs, out_specs = [q_spec, k_spec, v_spec, tile_c, rows], [tile, tile_c]
    out_shape = [jax.ShapeDtypeStruct((t, w), F32), jax.ShapeDtypeStruct((t, 128), F32)]
    scratch, aliases, name = [], {}, "attn_f"
    sem_pair = [pltpu.SemaphoreType.DMA((AG_SEMS,)), pltpu.SemaphoreType.DMA((AG_SEMS,))]
    if fused:
        ins.append(next_shard)
        in_specs.append(HBM)
        name += "_next"
    if both:
        aliases = {len(ins) + 1: 2 + fused}
        ins += list(late)
        in_specs += [HBM, HBM]
        name += "_late"
    if fused:
        out_specs.append(HBM)
        out_shape.append(jax.ShapeDtypeStruct((4,) + next_shard.shape, next_shard.dtype))
        scratch += sem_pair
    if both:
        out_specs.append(HBM)
        out_shape.append(jax.ShapeDtypeStruct(late[1].shape, late[1].dtype))
        scratch += sem_pair
    return pl.pallas_call(
        body, name=name, grid=(nq, N_HEADS), in_specs=in_specs, out_specs=out_specs, out_shape=out_shape,
        scratch_shapes=scratch, input_output_aliases=aliases,
        compiler_params=_cp(("arbitrary", "arbitrary")))(*ins)


def _attn_bwd_call(qkv, c128, cr, o, lse, do, exchange=None):
    t, w = qkv.shape[0], MIXER_WIDTH
    tq = ATT_TILE
    nq = t // tq
    r3 = lambda a: a.reshape(nq, tq, a.shape[-1])
    cr4 = cr.reshape(N_HEADS, nq, 1, tq)
    fused = exchange is not None

    def body(*refs):
        q_ref, k_ref, v_ref, c_ref, cr_ref, o_ref, lse_ref, do_ref = refs[:8]
        j, h = pl.program_id(0), pl.program_id(1)
        if fused:
            p_ref, dq_ref, dk_ref, dv_ref, dc_ref, dcr_ref, recv_ref, send_sems, recv_sems = refs[8:]
            start, finish = _chip_exchange_copies(p_ref, recv_ref, send_sems, recv_sems)
            pl.when((j == 0) & (h == 0))(start)
        else:
            dq_ref, dk_ref, dv_ref, dc_ref, dcr_ref = refs[8:]

        @pl.when((j == 0) & (h == 0))
        def _():
            dq_ref[...] = jnp.zeros_like(dq_ref)
            dc_ref[...] = jnp.zeros_like(dc_ref)

        hm = lax.broadcasted_iota(jnp.int32, (tq, w), 1) // HEAD_DIM == h
        lane = lax.broadcasted_iota(jnp.int32, (tq, 128), 1)
        kj = k_ref[...]
        vj = v_ref[...]
        ck = cr_ref[0, 0]
        causal = lax.broadcasted_iota(jnp.int32, (tq, tq), 0) >= lax.broadcasted_iota(jnp.int32, (tq, tq), 1)

        def step(i, carry, diagonal):
            dk, dv, dck = carry
            qm = jnp.where(hm, q_ref[i], 0.0)
            dom = jnp.where(hm, do_ref[i], 0.0)
            s = _dot(qm * ATT_SCALE, kj, 1, 1) - ck
            if diagonal:
                s = jnp.where(causal, s, NEG)
            p = jnp.exp(s + (_head_lane(c_ref[i], lane, h) - _head_lane(lse_ref[i], lane, h)))
            dv = dv + _dot(p, dom, 0, 0)
            dp = _dot(dom, vj, 1, 1)
            delta = jnp.sum(dom * o_ref[i], axis=1, keepdims=True)
            ds = p * (dp - delta)
            dq_ref[i] += jnp.where(hm, _dot(ds, kj, 1, 0), 0.0) * ATT_SCALE
            dk = dk + _dot(ds, qm, 0, 0) * ATT_SCALE
            dc_ref[i] += jnp.where(lane == h, jnp.sum(ds, axis=1, keepdims=True), 0.0)
            return dk, dv, dck - jnp.sum(ds, axis=0, keepdims=True)

        init = (jnp.zeros((tq, w), F32), jnp.zeros((tq, w), F32), jnp.zeros((1, tq), F32))
        carry = step(j, init, True)
        dk, dv, dck = lax.fori_loop(j + 1, nq, lambda i, c: step(i, c, False), carry)
        dcr_ref[0, 0] = dck

        @pl.when(h == 0)
        def _():
            dk_ref[...] = dk
            dv_ref[...] = dv

        @pl.when(h > 0)
        def _():
            dk_ref[...] += dk
            dv_ref[...] += dv

        if fused:
            pl.when((j == nq - 1) & (h == N_HEADS - 1))(finish)

    whole = pl.BlockSpec((nq, tq, w), lambda j, h: (0, 0, 0))
    whole_c = pl.BlockSpec((nq, tq, 128), lambda j, h: (0, 0, 0))
    tile = pl.BlockSpec((None, tq, w), lambda j, h: (j, 0, 0))
    tile_r = pl.BlockSpec((1, 1, 1, tq), lambda j, h: (h, j, 0, 0))
    s3 = jax.ShapeDtypeStruct((nq, tq, w), F32)
    qkv3 = r3(qkv)
    k_tile = pl.BlockSpec((None, tq, w), lambda j, h: (j, 0, 1))
    v_tile = pl.BlockSpec((None, tq, w), lambda j, h: (j, 0, 2))
    ins = [qkv3, qkv3, qkv3, r3(c128), cr4, r3(o), r3(lse), r3(do)]
    in_specs = [whole, k_tile, v_tile, whole_c, tile_r, whole, whole_c, whole]
    out_specs = [whole, tile, tile, whole_c, tile_r]
    out_shape = [s3, s3, s3, jax.ShapeDtypeStruct((nq, tq, 128), F32), jax.ShapeDtypeStruct((N_HEADS, nq, 1, tq), F32)]
    scratch = []
    if fused:
        ins.append(exchange)
        in_specs.append(HBM)
        out_specs.append(HBM)
        out_shape.append(jax.ShapeDtypeStruct((3,) + exchange.shape[1:], exchange.dtype))
        scratch = [pltpu.SemaphoreType.DMA((3,)), pltpu.SemaphoreType.DMA((3,))]
    dq, dk, dv, dc, dcr, *received = pl.pallas_call(
        body, name="attn_b_exchange" if fused else "attn_b", grid=(nq, N_HEADS), in_specs=in_specs,
        out_specs=out_specs, out_shape=out_shape, scratch_shapes=scratch,
        compiler_params=_cp(("arbitrary", "arbitrary")))(*ins)
    grads = (dq.reshape(t, w), dk.reshape(t, w), dv.reshape(t, w), dc.reshape(t, 128), dcr.reshape(N_HEADS, 1, t))
    return grads, (received[0] if fused else None)


def _loss_call(x, g, target):
    t, d = x.shape
    tr = 512

    def body(x_ref, g_ref, t_ref, loss_ref, dx_ref, dg_ref):
        tgt = t_ref[...]

        def f(xv, gv):
            return 0.5 * jnp.sum(jnp.mean(jnp.square(_rms(xv, gv) - tgt), axis=-1))

        val, vjp = jax.vjp(f, x_ref[...], g_ref[...])
        dx, dg = vjp(jnp.ones((), F32))
        dx_ref[...] = dx

        @pl.when(pl.program_id(0) == 0)
        def _():
            loss_ref[...] = jnp.zeros_like(loss_ref)
            dg_ref[...] = jnp.zeros_like(dg_ref)

        loss_ref[...] += jnp.full(loss_ref.shape, val, F32)
        dg_ref[...] += dg

    row = _row_spec(tr, d)
    return pl.pallas_call(
        body, name="loss_head", grid=(t // tr,), in_specs=[row, _full_spec(g), row],
        out_specs=[pl.BlockSpec((1, 128), lambda i: (0, 0)), row, _full_spec(g)],
        out_shape=[jax.ShapeDtypeStruct((1, 128), F32), jax.ShapeDtypeStruct((t, d), F32),
                   jax.ShapeDtypeStruct(g.shape, F32)],
        compiler_params=_cp(("arbitrary",)))(x, g, target)


def _blockdiag(w):
    l, g, a, b = w.shape
    return jnp.einsum('lgab,gk->lgakb', w, jnp.eye(g, dtype=w.dtype)).reshape(l, g * a, g * b)


def _prepare(rep):
    d = DEPTH
    w = MIXER_WIDTH
    rows = S5_GROUPS * S5_GROUP
    rep16 = lambda a: jnp.repeat(a, S5_GROUP, axis=1).reshape(d * rows, -1)
    bt = lambda b: b.transpose(0, 1, 3, 2).reshape(d * rows, S5_STATE)
    abar_re, abar_im, bb_re, bb_im = s5_disc(
        rep16(rep["s5_lambda_re"]), rep16(rep["s5_lambda_im"]), rep16(rep["s5_log_dt"][:, :, None]),
        bt(rep["s5_b_re"]), bt(rep["s5_b_im"]))
    g4 = lambda a: a.reshape(d, S5_GROUPS, S5_GROUP, S5_STATE)
    first = lambda a: g4(a)[:, :, 0, :].reshape(d, 1, S5_GROUPS * S5_STATE)
    cblk = lambda c: _blockdiag(c.transpose(0, 1, 3, 2))
    row = lambda a: a.reshape(d, 1, -1)
    return dict(
        sgu_norm_g=row(rep["sgu_norm_g"]), sgu_w=rep["sgu_w"],
        sgu_bias=jnp.repeat(rep["sgu_b"].transpose(0, 2, 1), HEAD_DIM, axis=2),
        abar_re=first(abar_re), abar_im=first(abar_im), bblk_re=_blockdiag(g4(bb_re)), bblk_im=_blockdiag(g4(bb_im)),
        cblk_re=cblk(rep["s5_c_re"]), cblk_im=cblk(rep["s5_c_im"]), s5_d=row(rep["s5_d"]), s5_glu_b=row(rep["s5_glu_b"]),
        lru_conv_b=row(rep["lru_conv_b"]), lru_wa=_blockdiag(rep["lru_wa"]), lru_ba=row(rep["lru_ba"]),
        lru_wx=_blockdiag(rep["lru_wx"]), lru_bx=row(rep["lru_bx"]), lru_lambda=row(rep["lru_lambda"]),
        fgate_b=jnp.pad(rep["fox_fgate_b"], ((0, 0), (0, 128 - N_HEADS)))[:, None, :])


PREPARED_FROM = ('sgu_norm_g', 'sgu_w', 'sgu_b', 's5_lambda_re', 's5_lambda_im', 's5_log_dt', 's5_b_re', 's5_b_im',
                 's5_c_re', 's5_c_im', 's5_d', 's5_glu_b', 'lru_conv_b', 'lru_wa', 'lru_ba', 'lru_wx', 'lru_bx',
                 'lru_lambda', 'fox_fgate_b')


def _mixers_pre(pieces, p, glu_w, conv_w):
    a_u, a_v, b_in, c_x, c_gate, d_q, d_k, d_v, d_f = pieces
    sw = p["sgu_w"]
    (y_a,) = sgu_mix(a_u, a_v, p["sgu_norm_g"], sw[0], sw[1], sw[2], sw[3], p["sgu_bias"])
    s_re, s_im = lti_scan(b_in, p["bblk_re"], p["bblk_im"], p["abar_re"], p["abar_im"])
    (y_b,) = s5_post(s_re, s_im, b_in, p["cblk_re"], p["cblk_im"], p["s5_d"], glu_w, p["s5_glu_b"])
    xc = causal_conv(c_x, conv_w, p["lru_conv_b"])
    a, b = lru_pre(xc, p["lru_wa"], p["lru_ba"], p["lru_wx"], p["lru_bx"], p["lru_lambda"])
    (y_c,) = lru_post(tv_scan(a, b), c_gate)
    (log_f,) = log_sig(d_f, p["fgate_b"])
    c128 = tv_scan(jnp.ones_like(log_f), log_f)
    return y_a, y_b, y_c, d_q, d_k, d_v, c128, c128[:, :N_HEADS].T[:, None, :]


PACK_COLS = 1024
SHARD_SHAPE = {'w_mlp_in': (1024, 1024), 'w_mlp_out': (1024, 1024), 'w_out': (256, 1024), 'w_in': (1024, 513),
               's5_glu_w': (64, 256), 'lru_conv_w': (4, 64)}
SHARDED_AXIS = {'w_in': 1, 's5_glu_w': 0, 'lru_conv_w': 1, 'w_out': 0, 'w_mlp_in': 1, 'w_mlp_out': 0}
SHARD_ROWS = {n: -(-s[0] * s[1] // PACK_COLS) for n, s in SHARD_SHAPE.items()}
SHARD_OFF = {n: sum(list(SHARD_ROWS.values())[:i]) for i, n in enumerate(SHARD_SHAPE)}
LAYER_ROWS = 2880
SMALL_OFF = SHARD_OFF['w_in']
GLUE_ROWS = LAYER_ROWS - SMALL_OFF
assert SHARD_OFF['w_mlp_out'] == 1024 and SHARD_OFF['w_out'] == 2048 and SMALL_OFF % GLUE_ROWS == 0
assert SHARD_OFF['lru_conv_w'] + SHARD_ROWS['lru_conv_w'] <= LAYER_ROWS
PACK_ROWS = DEPTH * LAYER_ROWS
TOK = 1024
FF = 4 * D_MODEL


def _w3(i_of):
    return pl.BlockSpec((None, 1024, PACK_COLS), i_of)


def _tile2(rows, cols, i_of):
    return pl.BlockSpec((rows, cols), i_of)


def _layer_fwd(x, h1, gathered, w_in, p, mix_p, next_shard, own_shard, next_gain):
    t = x.shape[0]
    nt = t // TOK
    f32 = lambda r, c: jax.ShapeDtypeStruct((r, c), F32)
    b16 = lambda r, c: jax.ShapeDtypeStruct((r, c), BF16)
    g2, gm = p["norm2_g"][None, :], p["mix_norm_g"][None, :]
    qkv_cols = (5 * MIXER_WIDTH, 8 * MIXER_WIDTH)
    z, qkv = _matmul("mm_in", (nt, 1, 1), h1, _tile2(TOK, D_MODEL, lambda i, j, k: (i, 0)),
                     w_in, _tile2(D_MODEL, D_IN_PAD, lambda i, j, k: (0, 0)), (1, 0),
                     [(f32(t, D_IN_PAD), _tile2(TOK, D_IN_PAD, lambda i, j, k: (i, 0))),
                      (b16(t, 3 * MIXER_WIDTH), _tile2(TOK, 3 * MIXER_WIDTH, lambda i, j, k: (i, 0)))],
                     epilogue=lambda acc: (acc, acc[:, qkv_cols[0]:qkv_cols[1]]))
    pieces = tuple(jnp.split(z, [MIXER_WIDTH * i for i in range(1, 9)], axis=1))
    (y_a, y_b, y_c, _, _, _, c128, cr), mix_vjp = jax.vjp(_mixers_pre, pieces, *mix_p)
    attn_in = (qkv, c128, cr)
    y_d, lse, *more = _attn_fwd_call(*attn_in, next_shard=next_shard,
                                     late=None if own_shard is None else (own_shard, gathered))
    if own_shard is not None:
        gathered = more.pop()
    w_out = _join_chips(_unpack_shards(gathered[:, None], ('w_out',))['w_out'], SHARDED_AXIS['w_out'])[0]
    ys = (y_a, y_b, y_c, y_d)
    (yn,) = _rowwise(_f_gnorm, list(ys), [gm], [D_MODEL], name="gnorm_f", tr=512, dtype=BF16)
    x_tile = _tile2(TOK, D_MODEL, lambda i, j, k: (i, 0))
    gain = _tile2(1, D_MODEL, lambda i, j, k: (0, 0))

    def add_and_norm(acc, r, gv):
        s = acc + r
        return s, _rms(s, gv)

    x1, h2 = _matmul("mm_out", (nt, 1, 1), yn, x_tile, w_out, _tile2(D_MODEL, D_MODEL, lambda i, j, k: (0, 0)), (1, 0),
                     [(f32(t, D_MODEL), x_tile), (b16(t, D_MODEL), x_tile)], extras=[(x, x_tile), (g2, gain)],
                     epilogue=add_and_norm)
    ff_tile = _tile2(TOK, 1024, lambda i, j, k: (i, j))
    act = _matmul("mm_up", (nt, FF // 1024, 1), h2, x_tile, gathered, _w3(lambda i, j, k: (j, 0, 0)), (1, 0),
                  [(b16(t, FF), ff_tile)], epilogue=lambda acc: (jnp.square(jnp.maximum(acc, 0.0)),))
    down = ("mm_down", (nt, 1, FF // 1024), act, _tile2(TOK, 1024, lambda i, j, k: (i, k)),
            gathered, _w3(lambda i, j, k: (k, 1, 0)), (1, 0))
    if next_gain is None:
        x2, h_next = _matmul(*down, [(f32(t, D_MODEL), x_tile)], extras=[(x1, x_tile)],
                             epilogue=lambda acc, r: (acc + r,)), None
    else:
        x2, h_next = _matmul(*down, [(f32(t, D_MODEL), x_tile), (b16(t, D_MODEL), x_tile)],
                             extras=[(x1, x_tile), (next_gain[None, :], gain)], epilogue=add_and_norm)
    res = (x, h1, mix_vjp, ys, attn_in, lse, yn, x1, h2, act, gathered, w_out)
    return x2, h_next, res, (more[0] if more else None)


def _layer_bwd(g, res, w_in, p, later_send):
    x, h1, mix_vjp, ys, attn_in, lse, yn, x1, h2, act, gathered, w_out = res
    half_rows = LAYER_ROWS // 2
    riding = later_send is not None
    halves = later_send.reshape(4, 2, half_rows, PACK_COLS) if riding else None
    send = lax.empty((4, LAYER_ROWS, PACK_COLS), BF16)
    t = x.shape[0]
    nt = t // TOK
    f32 = lambda r, c: jax.ShapeDtypeStruct((r, c), F32)
    g1, g2, gm = p["norm1_g"][None, :], p["norm2_g"][None, :], p["mix_norm_g"][None, :]
    x_tile = _tile2(TOK, D_MODEL, lambda i, j, k: (i, 0))
    ff_tile = _tile2(TOK, 1024, lambda i, j, k: (i, j))
    tok_k = _tile2(TOK, D_MODEL, lambda i, j, k: (k, 0))
    send_s = jax.ShapeDtypeStruct(send.shape, send.dtype)
    pair_rider = (halves, jax.ShapeDtypeStruct((4, half_rows, PACK_COLS), BF16), False, _pair_exchange_copies)
    du = _matmul("mm_down_dx", (nt, FF // 1024, 1), g, x_tile, gathered, _w3(lambda i, j, k: (j, 1, 0)), (1, 1),
                 [(jax.ShapeDtypeStruct((t, FF), BF16), ff_tile)], extras=[(act, ff_tile)],
                 epilogue=lambda acc, a: (2.0 * jnp.sqrt(a.astype(F32)) * acc,),
                 rider=pair_rider if riding else None)
    exchange = None
    if riding:
        du, from_sibling = du
        exchange = _add_kept(halves, from_sibling, REDUCE_ROWS)
    send = _matmul("mm_down_dw", (FF // 1024, 1, nt), act, _tile2(TOK, 1024, lambda i, j, k: (k, i)), g, tok_k, (0, 0),
                   [(send_s, _w3(lambda i, j, k: (i, 1, 0)))], into=send)
    send = _matmul("mm_up_dw", (1, FF // 1024, nt), h2, tok_k, du, _tile2(TOK, 1024, lambda i, j, k: (k, j)), (0, 0),
                   [(send_s, _w3(lambda i, j, k: (j, 0, 0)))], into=send)
    gain = _tile2(1, D_MODEL, lambda i, j, k: (0, 0))

    def norm_bwd(dh, xv, gv, through):
        _, vjp = jax.vjp(_rms, xv, gv)
        dxv, dgv = vjp(dh)
        return dxv + through, dgv

    g_mid, dg2 = _matmul("mm_up_dx", (nt, 1, FF // 1024), du, _tile2(TOK, 1024, lambda i, j, k: (i, k)),
                         gathered, _w3(lambda i, j, k: (k, 0, 0)), (1, 1),
                         [(f32(t, D_MODEL), x_tile), (f32(1, D_MODEL), gain)],
                         extras=[(x1, x_tile), (g2, gain), (g, x_tile)], epilogue=norm_bwd, summed=1)
    w_full = _tile2(D_MODEL, D_MODEL, lambda i, j, k: (0, 0))
    dyn = _matmul("mm_out_dx", (nt, 1, 1), g_mid, x_tile, w_out, w_full, (1, 1), [(f32(t, D_MODEL), x_tile)])
    quarter = D_MODEL // 4
    send = _matmul("mm_out_dw", (1, 1, nt), yn, tok_k, g_mid, tok_k, (0, 0),
                   [(send_s, pl.BlockSpec((4, quarter, PACK_COLS), lambda i, j, k: (0, SHARD_OFF['w_out'] // quarter, 0)))],
                   epilogue=lambda acc: (acc.reshape(4, quarter, PACK_COLS),), into=send, acc_shape=(D_MODEL, D_MODEL))
    dy_a, dy_b, dy_c, dy_d, dgm = _rowwise_vjp(_f_gnorm, list(ys), [gm], [dyn], name="gnorm_b", tr=512)
    d_attn_in, received = _attn_bwd_call(*attn_in, ys[3], lse, dy_d, exchange=exchange)
    d_pieces, *d_mix = mix_vjp((dy_a, dy_b, dy_c, *d_attn_in))
    dz = jnp.concatenate([d.astype(BF16) for d in d_pieces], axis=1)
    z_tile = _tile2(TOK, D_IN_PAD, lambda i, j, k: (i, 0))
    share_rider = None
    if riding:
        pair = _sum_chips(exchange, received, REDUCE_ROWS)
        share_rider = (pair, jax.ShapeDtypeStruct(pair.shape, pair.dtype), True, _pair_share_copies)
    d_w_in = _matmul("mm_in_dw", (1, 1, t // 512), h1, _tile2(512, D_MODEL, lambda i, j, k: (k, 0)),
                     dz, _tile2(512, D_IN_PAD, lambda i, j, k: (k, 0)), (0, 0),
                     [(f32(D_MODEL, D_IN_PAD), _tile2(D_MODEL, D_IN_PAD, lambda i, j, k: (0, 0)))], rider=share_rider)
    reduced = None
    if riding:
        d_w_in, pair = d_w_in
        reduced = pair.reshape(LAYER_ROWS, PACK_COLS)
    dx, dg1 = _matmul("mm_in_dx", (nt, 1, 1), dz, z_tile, w_in, _tile2(D_MODEL, D_IN_PAD, lambda i, j, k: (0, 0)), (1, 1),
                      [(f32(t, D_MODEL), x_tile), (f32(1, D_MODEL), gain)],
                      extras=[(x, x_tile), (g1, gain), (g_mid, x_tile)], epilogue=norm_bwd, summed=1)
    norms = dict(norm1_g=dg1[0], norm2_g=dg2[0], mix_norm_g=dgm[0])
    return dx, norms, d_mix, d_w_in, send, reduced


HBM = pl.BlockSpec(memory_space=pltpu.HBM)
D2D_CHUNKS = 15
ICI_CHUNKS = 5
VMEM_CHUNKS = 4


def _coords():
    return lax.axis_index("x"), lax.axis_index("y"), lax.axis_index("c")


def _other_chips(x, y):
    return [(1 - x, y), (x, 1 - y), (1 - x, 1 - y)]


def _start_chunks(make, rows, n):
    size = rows // n
    assert size * n == rows
    for k in range(n):
        make(pl.ds(k * size, size)).start()


AG_SEMS = 7


ALL_ROWS = (0, LAYER_ROWS, ICI_CHUNKS, D2D_CHUNKS)
EARLY_ROWS = (SMALL_OFF, GLUE_ROWS, 4, 4)
LATE_ROWS = (0, SMALL_OFF, 4, 12)


def _allgather_copies(in_ref, out_ref, send_sems, recv_sems, part=ALL_ROWS):
    r0, r, ici_chunks, d2d_chunks = part
    rh = r // 2
    x, y, c = _coords()
    me, sibling = (x, y, c), (x, y, 1 - c)
    chips = _other_chips(x, y)

    def half(px, py, pc, rows=pl.ds(0, rh)):
        return out_ref.at[2 * px + py, pl.ds(r0 + pc * rh + rows.start, rows.size), :]

    def copy(k, block, to, rows=pl.ds(0, rh), from_input=False):
        src = in_ref.at[pl.ds(r0 + block[2] * rh + rows.start, rows.size), :] if from_input else half(*block, rows)
        return pltpu.make_async_remote_copy(
            src_ref=src, dst_ref=half(*block, rows), send_sem=send_sems.at[k], recv_sem=recv_sems.at[k],
            device_id=to, device_id_type=MESH)

    def own(rows=pl.ds(0, r)):
        mine = pl.ds(r0 + rows.start, rows.size)
        return pltpu.make_async_remote_copy(
            src_ref=in_ref.at[mine, :], dst_ref=out_ref.at[2 * x + y, mine, :], send_sem=send_sems.at[6],
            recv_sem=recv_sems.at[6], device_id=sibling, device_id_type=MESH)

    def start():
        for j, chip in enumerate(chips):
            _start_chunks(lambda rows: copy(j, me, (*chip, c), rows, from_input=True), rh, ici_chunks)
        _start_chunks(own, r, d2d_chunks)

    def forward():
        for j, chip in enumerate(chips):
            copy(j, (*chip, c), me).wait_recv()
            _start_chunks(lambda rows: copy(3 + j, (*chip, c), sibling, rows), rh, d2d_chunks)

    def finish():
        for j, chip in enumerate(chips):
            copy(3 + j, (*chip, 1 - c), me).wait_recv()
        for j, chip in enumerate(chips):
            copy(j, me, (*chip, c), from_input=True).wait_send()
            copy(3 + j, (*chip, c), sibling).wait_send()
        own().wait()

    return start, forward, finish


def _allgather_shards(shard, part):
    def body(in_ref, out_ref, send_sems, recv_sems):
        start, forward, finish = _allgather_copies(in_ref, out_ref, send_sems, recv_sems, part)
        start()
        forward()
        finish()

    return pl.pallas_call(
        body, name="allgather_shards", out_shape=jax.ShapeDtypeStruct((4,) + shard.shape, shard.dtype),
        in_specs=[HBM], out_specs=HBM,
        scratch_shapes=[pltpu.SemaphoreType.DMA((AG_SEMS,)), pltpu.SemaphoreType.DMA((AG_SEMS,))],
        compiler_params=pltpu.CompilerParams())(shard)


def _pair_exchange_copies(g_ref, recv_ref, send_sem, recv_sem):
    s, _, rh, _ = g_ref.shape
    x, y, c = _coords()

    def copy(slot, rows):
        return pltpu.make_async_remote_copy(
            src_ref=g_ref.at[slot, 1 - c, rows, :], dst_ref=recv_ref.at[slot, rows, :], send_sem=send_sem,
            recv_sem=recv_sem, device_id=(x, y, 1 - c), device_id_type=MESH)

    def start():
        for slot in range(s):
            _start_chunks(lambda rows: copy(slot, rows), rh, VMEM_CHUNKS)

    def finish():
        pltpu.make_async_remote_copy(
            src_ref=g_ref.at[:, 1 - c], dst_ref=recv_ref, send_sem=send_sem, recv_sem=recv_sem,
            device_id=(x, y, 1 - c), device_id_type=MESH).wait()

    return start, finish


def _pair_share_copies(in_ref, out_ref, send_sem, recv_sem):
    rh = in_ref.shape[1]
    x, y, c = _coords()

    def copy(slot, rows=pl.ds(0, rh)):
        return pltpu.make_async_remote_copy(
            src_ref=in_ref.at[slot, rows, :], dst_ref=out_ref.at[slot, rows, :], send_sem=send_sem,
            recv_sem=recv_sem, device_id=(x, y, 1 - c), device_id_type=MESH)

    def start():
        _start_chunks(lambda rows: copy(c, rows), rh, D2D_CHUNKS)

    def finish():
        copy(c).wait_send()
        copy(1 - c).wait_recv()

    return start, finish


def _pair_exchange(g):
    s, _, rh, cols = g.shape

    def body(g_ref, recv_ref, send_sem, recv_sem):
        start, finish = _pair_exchange_copies(g_ref, recv_ref, send_sem, recv_sem)
        start()
        finish()

    return pl.pallas_call(
        body, name="pair_exchange", out_shape=jax.ShapeDtypeStruct((s, rh, cols), g.dtype), in_specs=[HBM],
        out_specs=HBM, scratch_shapes=[pltpu.SemaphoreType.DMA] * 2, compiler_params=pltpu.CompilerParams())(g)


def _chip_exchange_copies(p_ref, recv_ref, send_sems, recv_sems):
    rh = p_ref.shape[1]
    x, y, c = _coords()
    chips = _other_chips(x, y)

    def copy(j, chip, rows=pl.ds(0, rh)):
        return pltpu.make_async_remote_copy(
            src_ref=p_ref.at[2 * chip[0] + chip[1], rows, :], dst_ref=recv_ref.at[j, rows, :],
            send_sem=send_sems.at[j], recv_sem=recv_sems.at[j], device_id=(*chip, c), device_id_type=MESH)

    def start():
        for j, chip in enumerate(chips):
            _start_chunks(lambda rows: copy(j, chip, rows), rh, ICI_CHUNKS)

    def finish():
        for j, chip in enumerate(chips):
            copy(j, chip).wait_recv()
        for j, chip in enumerate(chips):
            copy(j, chip).wait_send()

    return start, finish


def _sum_chips(p, recv, tr):
    _, rh, cols = p.shape
    x, y, c = _coords()
    one = lambda v: v.astype(jnp.int32).reshape(1)

    def body(chip_ref, c_ref, own_ref, r_ref, o_ref):
        acc = own_ref[...].astype(F32)
        for k in range(3):
            acc = acc + r_ref[k].astype(F32)
        o_ref[...] = acc

    return pl.pallas_call(
        body, name="sum_chips", out_shape=jax.ShapeDtypeStruct((2, rh, cols), F32),
        grid_spec=pltpu.PrefetchScalarGridSpec(
            num_scalar_prefetch=2, grid=(rh // tr,),
            in_specs=[pl.BlockSpec((None, tr, cols), lambda i, chip_ref, c_ref: (chip_ref[0], i, 0)),
                      pl.BlockSpec((3, tr, cols), lambda i, chip_ref, c_ref: (0, i, 0))],
            out_specs=pl.BlockSpec((None, tr, cols), lambda i, chip_ref, c_ref: (c_ref[0], i, 0))),
        compiler_params=_cp(("arbitrary",)))(one(2 * x + y), one(c), p, recv)


def _pair_share(buf):
    _, rh, cols = buf.shape

    def body(in_ref, out_ref, send_sem, recv_sem):
        start, finish = _pair_share_copies(in_ref, out_ref, send_sem, recv_sem)
        start()
        finish()

    return pl.pallas_call(
        body, name="pair_share", out_shape=jax.ShapeDtypeStruct(buf.shape, buf.dtype), in_specs=[HBM], out_specs=HBM,
        scratch_shapes=[pltpu.SemaphoreType.DMA] * 2, input_output_aliases={0: 0},
        compiler_params=pltpu.CompilerParams())(buf)


def _allgather_all(blk, exchange):
    m_per, cols = blk.shape
    whole = pl.ds(0, m_per)

    def body(x_ref, p_ref, out_ref, got_ref, send_sems, recv_sems, local_sem, p_send_sems, p_recv_sems):
        x, y, c = _coords()
        me, sibling = (x, y, c), (x, y, 1 - c)
        chips = _other_chips(x, y)
        start_exchange, finish_exchange = _chip_exchange_copies(p_ref, got_ref, p_send_sems, p_recv_sems)
        start_exchange()

        def rows_of(px, py, pc, rows):
            return out_ref.at[4 * px + 2 * py + pc, rows, :]

        def copy(k, block, to, rows=whole, from_input=False):
            return pltpu.make_async_remote_copy(
                src_ref=x_ref.at[rows, :] if from_input else rows_of(*block, rows), dst_ref=rows_of(*block, rows),
                send_sem=send_sems.at[k], recv_sem=recv_sems.at[k], device_id=to, device_id_type=MESH)

        mine = pltpu.make_async_copy(x_ref, rows_of(*me, whole), local_sem)
        mine.start()
        _start_chunks(lambda rows: copy(0, me, sibling, rows, from_input=True), m_per, VMEM_CHUNKS)
        for j, chip in enumerate(chips):
            _start_chunks(lambda rows: copy(1 + j, me, (*chip, c), rows, from_input=True), m_per, VMEM_CHUNKS)
        for j, chip in enumerate(chips):
            copy(1 + j, (*chip, c), me).wait_recv()
            _start_chunks(lambda rows: copy(4 + j, (*chip, c), sibling, rows), m_per, VMEM_CHUNKS)
        copy(0, sibling, me).wait_recv()
        for j, chip in enumerate(chips):
            copy(4 + j, (*chip, 1 - c), me).wait_recv()
        copy(0, me, sibling, from_input=True).wait_send()
        for j, chip in enumerate(chips):
            copy(1 + j, me, (*chip, c), from_input=True).wait_send()
            copy(4 + j, (*chip, c), sibling).wait_send()
        mine.wait()
        finish_exchange()

    vmem = pl.BlockSpec(memory_space=pltpu.VMEM)
    return pl.pallas_call(
        body, name="allgather_all",
        out_shape=[jax.ShapeDtypeStruct((8, m_per, cols), blk.dtype),
                   jax.ShapeDtypeStruct((3,) + exchange.shape[1:], exchange.dtype)],
        in_specs=[vmem, HBM], out_specs=[vmem, HBM],
        scratch_shapes=[pltpu.SemaphoreType.DMA((7,)), pltpu.SemaphoreType.DMA((7,)), pltpu.SemaphoreType.DMA,
                        pltpu.SemaphoreType.DMA((3,)), pltpu.SemaphoreType.DMA((3,))],
        compiler_params=pltpu.CompilerParams(vmem_limit_bytes=VMEM_LIMIT))(blk, exchange)


def _add_kept(g, recv, tr):
    s, _, rh, cols = g.shape

    def body(c_ref, a_ref, b_ref, o_ref):
        o_ref[...] = (a_ref[...].astype(F32) + b_ref[...].astype(F32)).astype(o_ref.dtype)

    spec = pl.BlockSpec((None, tr, cols), lambda si, i, c_ref: (si, i, 0))
    return pl.pallas_call(
        body, name="add_kept", out_shape=jax.ShapeDtypeStruct((s, rh, cols), BF16),
        grid_spec=pltpu.PrefetchScalarGridSpec(
            num_scalar_prefetch=1, grid=(s, rh // tr),
            in_specs=[pl.BlockSpec((None, None, tr, cols), lambda si, i, c_ref: (si, c_ref[0], i, 0)), spec],
            out_specs=spec),
        compiler_params=_cp(("arbitrary", "arbitrary")))(lax.axis_index("c").astype(jnp.int32).reshape(1), g, recv)


def _sum_slots(p, tr, name):
    s, rows, cols = p.shape

    def body(p_ref, o_ref):
        acc = p_ref[0].astype(F32)
        for k in range(1, s):
            acc = acc + p_ref[k].astype(F32)
        o_ref[...] = acc

    return pl.pallas_call(
        body, name=name, grid=(rows // tr,), in_specs=[pl.BlockSpec((s, tr, cols), lambda i: (0, i, 0))],
        out_specs=_row_spec(tr, cols), out_shape=jax.ShapeDtypeStruct((rows, cols), F32),
        compiler_params=_cp(("arbitrary",)))(p)


def _adamw_call(w, g, m, v, name):
    rows, cols = w.shape
    tr = _tile(rows, 512) if rows % 512 == 0 else _tile(rows, 128)
    c1 = 1.0 - ADAM_B1 ** ADAM_STEP
    c2 = 1.0 - ADAM_B2 ** ADAM_STEP

    def body(w_ref, g_ref, m_ref, v_ref, d_ref, nm_ref, nv_ref):
        gv = g_ref[...]
        nm = ADAM_B1 * m_ref[...] + (1.0 - ADAM_B1) * gv
        nv = ADAM_B2 * v_ref[...] + (1.0 - ADAM_B2) * jnp.square(gv)
        d_ref[...] = -ADAM_LR * ((nm / c1) / (jnp.sqrt(nv / c2) + ADAM_EPS) + ADAM_WD * w_ref[...])
        nm_ref[...] = nm
        nv_ref[...] = nv

    spec = _row_spec(tr, cols)
    o = jax.ShapeDtypeStruct((rows, cols), F32)
    return pl.pallas_call(body, name=name, grid=(rows // tr,), in_specs=[spec] * 4, out_specs=[spec] * 3,
                          out_shape=[o, o, o], compiler_params=_cp(("arbitrary",)))(w, g, m, v)


WEIGHTS = ('norm1_g', 'w_in', 'sgu_norm_g', 'sgu_w', 'sgu_b', 's5_lambda_re', 's5_lambda_im', 's5_log_dt',
           's5_b_re', 's5_b_im', 's5_c_re', 's5_c_im', 's5_d', 's5_glu_w', 's5_glu_b', 'lru_conv_w',
           'lru_conv_b', 'lru_wa', 'lru_ba', 'lru_wx', 'lru_bx', 'lru_lambda', 'fox_fgate_b', 'mix_norm_g',
           'w_out', 'norm2_g', 'w_mlp_in', 'w_mlp_out', 'final_g')
N_W = len(WEIGHTS)


def _pack_shards(shards, dtype, names=tuple(SHARD_SHAPE), rows=LAYER_ROWS):
    parts = []
    for n in names:
        lead = shards[n].shape[:-2]
        flat = shards[n].reshape(*lead, -1).astype(dtype)
        flat = jnp.pad(flat, [(0, 0)] * len(lead) + [(0, SHARD_ROWS[n] * PACK_COLS - flat.shape[-1])])
        parts.append(flat.reshape(*lead, SHARD_ROWS[n], PACK_COLS))
    lead = parts[0].shape[:-2]
    used = sum(SHARD_ROWS[n] for n in names)
    if rows > used:
        parts.append(jnp.zeros((*lead, rows - used, PACK_COLS), dtype))
    return jnp.concatenate(parts, axis=-2)


def _unpack_shards(buf, names=tuple(SHARD_SHAPE)):
    lead = buf.shape[:-2]
    out = {}
    for n in names:
        s0, s1 = SHARD_SHAPE[n]
        rows, off = SHARD_ROWS[n], SHARD_OFF[n]
        flat = buf[..., off:off + rows, :].reshape(*lead, rows * PACK_COLS)
        out[n] = flat[..., :s0 * s1].reshape(*lead, s0, s1)
    return out


def _join_chips(g, axis):
    _, d, s0, s1 = g.shape
    if axis == 0:
        return g.transpose(1, 0, 2, 3).reshape(d, 4 * s0, s1)
    return g.transpose(1, 2, 0, 3).reshape(d, s0, 4 * s1)


def _split_chips(w, axis):
    d = w.shape[0]
    if axis == 0:
        return w.reshape(d, 4, w.shape[1] // 4, w.shape[2]).transpose(1, 0, 2, 3)
    return w.reshape(d, w.shape[1], 4, w.shape[2] // 4).transpose(2, 0, 1, 3)


def _flat_rows(shape):
    return -(-math.prod(shape) // PACK_COLS)


def _pack_flat(arrs, rows, dtype=F32):
    parts = []
    for a in arrs:
        flat = a.reshape(-1).astype(dtype)
        r = _flat_rows(a.shape)
        parts.append(jnp.pad(flat, (0, r * PACK_COLS - flat.shape[0])).reshape(r, PACK_COLS))
    used = sum(p.shape[0] for p in parts)
    parts.append(jnp.zeros((rows - used, PACK_COLS), dtype))
    return jnp.concatenate(parts, axis=0)


def _split3(s):
    hi = s.astype(BF16).astype(F32)
    mid = (s - hi).astype(BF16).astype(F32)
    return jnp.stack([hi, mid, s - hi - mid])


def _unpack_flat(buf, shapes):
    out, off = [], 0
    for s in shapes:
        r = _flat_rows(s)
        out.append(buf[off:off + r].reshape(-1)[:math.prod(s)].reshape(s))
        off += r
    return out


def _write_rows(buf, src, row_offset):
    n, r, cols = src.shape

    def body(s_ref, b_ref, o_ref):
        o_ref[...] = s_ref[...].astype(o_ref.dtype)

    blk = (None, r, cols)
    return pl.pallas_call(
        body, name="write_rows", grid=(n,),
        in_specs=[pl.BlockSpec(blk, lambda s: (s, 0, 0)), pl.BlockSpec(memory_space=pl.ANY)],
        out_specs=pl.BlockSpec(blk, lambda s: (s, row_offset // r, 0)),
        out_shape=jax.ShapeDtypeStruct(buf.shape, buf.dtype), input_output_aliases={1: 0},
        compiler_params=_cp(("arbitrary",)))(src, buf)


def _write_glue(send, glue):
    return _write_rows(send, glue, SMALL_OFF)


GLUE_PACKED = ('w_in', 's5_glu_w', 'lru_conv_w')
REDUCE_ROWS = LAYER_ROWS // 4


def _pack_weights(w):
    buf = lax.empty((DEPTH, LAYER_ROWS, PACK_COLS), BF16)
    for n in ('w_mlp_in', 'w_mlp_out', 'w_out'):
        buf = _write_rows(buf, w[n], SHARD_OFF[n])
    return _write_rows(buf, _pack_shards(w, BF16, names=GLUE_PACKED, rows=GLUE_ROWS), SMALL_OFF)


def _layer_weights(gathered):
    parts = _unpack_shards(gathered[:, None], GLUE_PACKED)
    joined = {n: _join_chips(g, SHARDED_AXIS[n])[0] for n, g in parts.items()}
    joined['w_in'] = jnp.pad(joined['w_in'], ((0, 0), (0, D_IN_PAD - D_IN_PROJ)))
    return joined


def _reduce_start(send):
    halves = send.reshape(4, 2, LAYER_ROWS // 2, PACK_COLS)
    return _add_kept(halves, _pair_exchange(halves), REDUCE_ROWS)


def _reduce_finish(chip_sum, received):
    return _pair_share(_sum_chips(chip_sum, received, REDUCE_ROWS)).reshape(LAYER_ROWS, PACK_COLS)


def _forward_backward(x, target, final_g, shards, rep):
    norm_p = [{n: rep[n][l] for n in ('norm1_g', 'norm2_g', 'mix_norm_g')} for l in range(DEPTH)]
    prepared, prepare_vjp = jax.vjp(_prepare, {n: rep[n] for n in PREPARED_FROM})
    gathered = _allgather_shards(shards[0], EARLY_ROWS)
    (h,) = _rowwise(_f_rms, [x], [norm_p[0]["norm1_g"][None, :]], [D_MODEL], name="rms_f", tr=512, dtype=BF16)
    layers = []
    for l in range(DEPTH):
        lw = _layer_weights(gathered)
        mix_p = ({n: a[l] for n, a in prepared.items()}, lw['s5_glu_w'].astype(F32), lw['lru_conv_w'].astype(F32))
        last = l + 1 == DEPTH
        x, h, res, gathered = _layer_fwd(
            x, h, gathered, lw['w_in'], norm_p[l], mix_p, None if last else shards[l + 1],
            shards[l], None if last else norm_p[l + 1]["norm1_g"])
        layers.append((res, lw))
    loss_part, g, d_final = _loss_call(x, final_g[None, :], target)

    norms, d_prepared, reduced, later_send = [None] * DEPTH, [None] * DEPTH, [None] * DEPTH, None
    for l in reversed(range(DEPTH)):
        res, lw = layers[l]
        g, norms[l], (d_prepared[l], d_glu_w, d_conv_w), d_w_in, send, later_reduced = _layer_bwd(
            g, res, lw['w_in'], norm_p[l], later_send)
        if later_send is not None:
            reduced[l + 1] = later_reduced
        mine = {'w_in': d_w_in[:, :D_IN_PROJ], 's5_glu_w': d_glu_w, 'lru_conv_w': d_conv_w}
        glue = _pack_shards({n: _split_chips(a[None], SHARDED_AXIS[n]) for n, a in mine.items()}, BF16,
                            names=GLUE_PACKED, rows=GLUE_ROWS)[:, 0]
        later_send = _write_glue(send, glue)
    stack = lambda per_layer: {n: jnp.stack([per_layer[l][n] for l in range(DEPTH)]) for n in per_layer[0]}
    (d_rep,) = prepare_vjp(stack(d_prepared))
    return loss_part, g, d_final, dict(d_rep, **stack(norms)), reduced[1:], _reduce_start(later_send)


def _step(*args):
    x, target = args[0], args[1 + N_W]
    w = dict(zip(WEIGHTS, args[1:1 + N_W]))
    m = dict(zip(WEIGHTS, args[2 + N_W:2 + 2 * N_W]))
    v = dict(zip(WEIGHTS, args[2 + 2 * N_W:2 + 3 * N_W]))
    small = [n for n in WEIGHTS if n not in SHARD_SHAPE]

    shards = _pack_weights({n: w[n] for n in SHARD_SHAPE})
    loss_part, dx, d_final, dw, reduced_above, chip_sum0 = _forward_backward(
        x[0], target[0], w['final_g'], shards, {n: w[n] for n in small})

    small_g = [d_final.reshape(-1) if n == 'final_g' else dw[n] for n in small]
    small_rows = -(-(sum(_flat_rows(w[n].shape) for n in small) + 1) // 128) * 128
    mine = _pack_flat(small_g + [_split3(loss_part[0, 0])], small_rows, BF16)
    everyone, received0 = _allgather_all(mine, chip_sum0)
    g_shard = _unpack_shards(jnp.stack([_reduce_finish(chip_sum0, received0)] + reduced_above))
    small_sum = _sum_slots(everyone, 128, "sum_devices")
    *g_small, loss = _unpack_flat(small_sum, [w[n].shape for n in small] + [(3,)])
    loss = jnp.sum(loss)

    grads, delta, new_m, new_v = {}, {}, {}, {}
    for n in SHARD_SHAPE:
        shp = w[n].shape
        v2 = lambda a: a.reshape(-1, shp[-1])
        res = _adamw_call(v2(w[n]), v2(g_shard[n]), v2(m[n]), v2(v[n]), "adamw_" + n)
        grads[n] = g_shard[n]
        delta[n], new_m[n], new_v[n] = (r.reshape(shp) for r in res)
    for n, g in zip(small, g_small):
        shp = w[n].shape
        v2 = lambda a: a.reshape(-1, shp[-1])
        res = _adamw_call(v2(w[n]), v2(g), v2(m[n]), v2(v[n]), "adamw_" + n)
        grads[n] = g
        delta[n], new_m[n], new_v[n] = (r.reshape(shp) for r in res)

    return (loss, dx[None], *[grads[n] for n in WEIGHTS], *[delta[n] for n in WEIGHTS],
            *[new_m[n] for n in WEIGHTS], *[new_v[n] for n in WEIGHTS])


def kernel(x, norm1_g, w_in, sgu_norm_g, sgu_w, sgu_b, s5_lambda_re, s5_lambda_im, s5_log_dt, s5_b_re, s5_b_im, s5_c_re, s5_c_im, s5_d, s5_glu_w, s5_glu_b, lru_conv_w, lru_conv_b, lru_wa, lru_ba, lru_wx, lru_bx, lru_lambda, fox_fgate_b, mix_norm_g, w_out, norm2_g, w_mlp_in, w_mlp_out, final_g, loss_target, m_norm1_g, m_w_in, m_sgu_norm_g, m_sgu_w, m_sgu_b, m_s5_lambda_re, m_s5_lambda_im, m_s5_log_dt, m_s5_b_re, m_s5_b_im, m_s5_c_re, m_s5_c_im, m_s5_d, m_s5_glu_w, m_s5_glu_b, m_lru_conv_w, m_lru_conv_b, m_lru_wa, m_lru_ba, m_lru_wx, m_lru_bx, m_lru_lambda, m_fox_fgate_b, m_mix_norm_g, m_w_out, m_norm2_g, m_w_mlp_in, m_w_mlp_out, m_final_g, v_norm1_g, v_w_in, v_sgu_norm_g, v_sgu_w, v_sgu_b, v_s5_lambda_re, v_s5_lambda_im, v_s5_log_dt, v_s5_b_re, v_s5_b_im, v_s5_c_re, v_s5_c_im, v_s5_d, v_s5_glu_w, v_s5_glu_b, v_lru_conv_w, v_lru_conv_b, v_lru_wa, v_lru_ba, v_lru_wx, v_lru_bx, v_lru_lambda, v_fox_fgate_b, v_mix_norm_g, v_w_out, v_norm2_g, v_w_mlp_in, v_w_mlp_out, v_final_g):
    return _step(x, norm1_g, w_in, sgu_norm_g, sgu_w, sgu_b, s5_lambda_re, s5_lambda_im, s5_log_dt, s5_b_re, s5_b_im, s5_c_re, s5_c_im, s5_d, s5_glu_w, s5_glu_b, lru_conv_w, lru_conv_b, lru_wa, lru_ba, lru_wx, lru_bx, lru_lambda, fox_fgate_b, mix_norm_g, w_out, norm2_g, w_mlp_in, w_mlp_out, final_g, loss_target, m_norm1_g, m_w_in, m_sgu_norm_g, m_sgu_w, m_sgu_b, m_s5_lambda_re, m_s5_lambda_im, m_s5_log_dt, m_s5_b_re, m_s5_b_im, m_s5_c_re, m_s5_c_im, m_s5_d, m_s5_glu_w, m_s5_glu_b, m_lru_conv_w, m_lru_conv_b, m_lru_wa, m_lru_ba, m_lru_wx, m_lru_bx, m_lru_lambda, m_fox_fgate_b, m_mix_norm_g, m_w_out, m_norm2_g, m_w_mlp_in, m_w_mlp_out, m_final_g, v_norm1_g, v_w_in, v_sgu_norm_g, v_sgu_w, v_sgu_b, v_s5_lambda_re, v_s5_lambda_im, v_s5_log_dt, v_s5_b_re, v_s5_b_im, v_s5_c_re, v_s5_c_im, v_s5_d, v_s5_glu_w, v_s5_glu_b, v_lru_conv_w, v_lru_conv_b, v_lru_wa, v_lru_ba, v_lru_wx, v_lru_bx, v_lru_lambda, v_fox_fgate_b, v_mix_norm_g, v_w_out, v_norm2_g, v_w_mlp_in, v_w_mlp_out, v_final_g)
```

```python
import math

import jax
import jax.numpy as jnp
from jax import lax
from jax.experimental import pallas as pl
from jax.experimental.pallas import tpu as pltpu

F32 = jnp.float32
BF16 = jnp.bfloat16

DEPTH = 4
D_MODEL = 1024
MIXER_WIDTH = 256
SGU_CHUNK = 128
N_HEADS = 4
HEAD_DIM = 64
S5_GROUPS = 16
S5_GROUP = 16
S5_STATE = 64
LRU_C = 8.0
RMS_EPS = 1e-6
D_IN_PROJ = 8 * MIXER_WIDTH + N_HEADS
D_IN_PAD = 8 * MIXER_WIDTH + 128
ADAM_LR, ADAM_B1, ADAM_B2, ADAM_EPS, ADAM_WD, ADAM_STEP = 0.001, 0.9, 0.999, 1e-08, 0.01, 10

V7X_VMEM_BYTES = 64 * 1024 * 1024
VMEM_LIMIT = V7X_VMEM_BYTES - 8 * 1024 * 1024
NEG = -1e30
MESH = pl.DeviceIdType.MESH


def _cp(sem=None, **kw):
    return pltpu.CompilerParams(dimension_semantics=sem, vmem_limit_bytes=VMEM_LIMIT, **kw)


def _full_spec(a):
    nd = a.ndim
    return pl.BlockSpec(a.shape, lambda *_: (0,) * nd)


def _tile(n, pref=512):
    return pref if n % pref == 0 else n


def _dot(a, b, ca, cb):
    return lax.dot_general(a.astype(BF16), b.astype(BF16), (((ca,), (cb,)), ((), ())),
                           preferred_element_type=F32)


def _matmul(name, grid, a, a_spec, b, b_spec, dims, outs, *, extras=(), epilogue=None, into=None, summed=0,
            acc_shape=None, rider=None):
    nk = grid[2]
    n_ex, n_out = len(extras), len(outs)
    tm_tn = acc_shape or tuple(d for d in outs[0][1].block_shape if d is not None)[-2:]
    n_in = 2 + n_ex + (into is not None)

    def body(*refs):
        a_ref, b_ref = refs[0], refs[1]
        ex_refs = refs[2:2 + n_ex]
        o_refs = refs[n_in + (rider is not None):n_in + (rider is not None) + n_out]
        if rider is not None:
            start, finish_rider = rider[3](refs[n_in], refs[n_in + 1 + n_out], refs[-2], refs[-1])
            step = [pl.program_id(d) for d in range(3)]
            pl.when((step[0] == 0) & (step[1] == 0) & (step[2] == 0))(start)
        if summed:
            @pl.when((pl.program_id(0) == 0) & (pl.program_id(1) == 0) & (pl.program_id(2) == 0))
            def _():
                for o_ref in o_refs[n_out - summed:]:
                    o_ref[...] = jnp.zeros_like(o_ref)

        def finish(val):
            res = epilogue(val, *[e[...] for e in ex_refs]) if epilogue else (val,)
            for idx, (o_ref, r) in enumerate(zip(o_refs, res)):
                if idx >= n_out - summed:
                    o_ref[...] += r
                else:
                    o_ref[...] = r.astype(o_ref.dtype)

        if nk == 1:
            finish(_dot(a_ref[...], b_ref[...], *dims))
        else:
            acc = refs[n_in + (rider is not None) + n_out + (rider is not None)]
            kk = pl.program_id(2)

            @pl.when(kk == 0)
            def _():
                acc[...] = jnp.zeros_like(acc)

            acc[...] += _dot(a_ref[...], b_ref[...], *dims)

            @pl.when(kk == nk - 1)
            def _():
                finish(acc[...])

        if rider is not None:
            pl.when((step[0] == grid[0] - 1) & (step[1] == grid[1] - 1) & (step[2] == grid[2] - 1))(finish_rider)

    ins = [a, b] + [e[0] for e in extras]
    specs = [a_spec, b_spec] + [e[1] for e in extras]
    aliases = {}
    if into is not None:
        aliases = {len(ins): 0}
        ins.append(into)
        specs.append(pl.BlockSpec(memory_space=pl.ANY))
    out_specs, out_shape = [o[1] for o in outs], [o[0] for o in outs]
    scratch = [pltpu.VMEM(tm_tn, F32)] if nk > 1 else []
    if rider is not None:
        if rider[2]:
            aliases[len(ins)] = n_out
        ins.append(rider[0])
        specs.append(HBM)
        out_specs.append(HBM)
        out_shape.append(rider[1])
        scratch += [pltpu.SemaphoreType.DMA] * 2
        name += "_rider"
    res = pl.pallas_call(
        body, name=name, grid=grid, in_specs=specs, out_specs=out_specs, out_shape=out_shape,
        scratch_shapes=scratch, input_output_aliases=aliases,
        compiler_params=_cp(("arbitrary", "arbitrary", "arbitrary")))(*ins)
    return res[0] if len(res) == 1 else res


@jax.custom_vjp
def _bdot(a, b):
    return _dot(a, b, 1, 0)


def _bdot_fwd(a, b):
    return _dot(a, b, 1, 0), (a, b)


def _bdot_bwd(r, g):
    a, b = r
    return _dot(g, b, 1, 1), _dot(a, g, 0, 0)


_bdot.defvjp(_bdot_fwd, _bdot_bwd)


def _row_spec(tr, w):
    return pl.BlockSpec((tr, w), lambda i: (i, 0))


def _rowwise(fn, rows, pars, outs, *, name, tr, dtype=F32):
    t = rows[0].shape[0]
    n_in = len(rows) + len(pars)

    def body(*refs):
        res = fn(*[r[...] for r in refs[:n_in]])
        for o_ref, v in zip(refs[n_in:], res):
            o_ref[...] = v.astype(o_ref.dtype)

    return pl.pallas_call(
        body, name=name, grid=(t // tr,),
        in_specs=[_row_spec(tr, r.shape[1]) for r in rows] + [_full_spec(p) for p in pars],
        out_specs=[_row_spec(tr, w) for w in outs],
        out_shape=[jax.ShapeDtypeStruct((t, w), dtype) for w in outs],
        compiler_params=_cp(("arbitrary",)))(*rows, *pars)


def _rowwise_vjp(fn, rows, pars, cots, *, name, tr, add=None):
    t = rows[0].shape[0]
    nr, npar = len(rows), len(pars)
    cots = list(cots) + ([add] if add is not None else [])
    nc = len(cots)

    def body(*refs):
        vals = [r[...] for r in refs[:nr + npar]]
        cts = [c[...] for c in refs[nr + npar:nr + npar + nc]]
        douts = refs[nr + npar + nc:]
        extra = cts.pop() if add is not None else None
        _, vjp = jax.vjp(fn, *vals)
        grads = list(vjp(tuple(cts)))
        if extra is not None:
            grads[0] = grads[0] + extra
        for kk in range(nr):
            douts[kk][...] = grads[kk]

        @pl.when(pl.program_id(0) == 0)
        def _():
            for kk in range(npar):
                douts[nr + kk][...] = jnp.zeros_like(douts[nr + kk])

        for kk in range(npar):
            douts[nr + kk][...] += grads[nr + kk]

    return pl.pallas_call(
        body, name=name, grid=(t // tr,),
        in_specs=[_row_spec(tr, r.shape[1]) for r in rows] + [_full_spec(p) for p in pars]
        + [_row_spec(tr, c.shape[1]) for c in cots],
        out_specs=[_row_spec(tr, r.shape[1]) for r in rows] + [_full_spec(p) for p in pars],
        out_shape=[jax.ShapeDtypeStruct(r.shape, F32) for r in rows]
        + [jax.ShapeDtypeStruct(p.shape, F32) for p in pars],
        compiler_params=_cp(("arbitrary",)))(*rows, *pars, *cots)


def _make_rw(fn, name, tr, nr, outs):
    @jax.custom_vjp
    def f(*args):
        return tuple(_rowwise(fn, args[:nr], args[nr:], outs, name=name + "_f", tr=tr))

    def fwd(*args):
        return f(*args), args

    def bwd(args, cts):
        return tuple(_rowwise_vjp(fn, args[:nr], args[nr:], list(cts), name=name + "_b", tr=tr))

    f.defvjp(fwd, bwd)
    return f


def _rms(x, g):
    return x * lax.rsqrt(jnp.mean(jnp.square(x), axis=-1, keepdims=True) + RMS_EPS) * g


def _f_rms(x, g):
    return (_rms(x, g),)


def _f_sgu(au, av, ng, w0, w1, w2, w3, bfull):
    u = jax.nn.gelu(au)
    v = _rms(jax.nn.gelu(av), ng)
    tri = lax.broadcasted_iota(jnp.int32, (SGU_CHUNK, SGU_CHUNK), 0) >= lax.broadcasted_iota(
        jnp.int32, (SGU_CHUNK, SGU_CHUNK), 1)
    head = lax.broadcasted_iota(jnp.int32, v.shape, 1) // HEAD_DIM
    mixed = bfull
    for h, w in enumerate((w0, w1, w2, w3)):
        mixed = mixed + _bdot(jnp.where(tri, w, 0.0), jnp.where(head == h, v, 0.0))
    return (u * mixed,)


def _f_s5disc(lam_re, lam_im, log_dt, b_re, b_im):
    dt = jnp.exp(log_dt)
    mag = jnp.exp(lam_re * dt)
    abar_re = mag * jnp.cos(lam_im * dt)
    abar_im = mag * jnp.sin(lam_im * dt)
    denom = jnp.square(lam_re) + jnp.square(lam_im)
    num_re = abar_re - 1.0
    num_im = abar_im
    fac_re = (num_re * lam_re + num_im * lam_im) / denom
    fac_im = (num_im * lam_re - num_re * lam_im) / denom
    return abar_re, abar_im, fac_re * b_re - fac_im * b_im, fac_re * b_im + fac_im * b_re


def _f_s5post(s_re, s_im, u, c_re, c_im, d, gw, gb):
    y = _bdot(s_re, c_re) - _bdot(s_im, c_im) + d * u
    y = jax.nn.gelu(y)
    return (y * jax.nn.sigmoid(_bdot(y, gw) + gb),)


def _f_lrupre(xc, wa, ba, wx, bx, lam):
    r = jax.nn.sigmoid(_bdot(xc, wa) + ba)
    i = jax.nn.sigmoid(_bdot(xc, wx) + bx)
    log_a = -LRU_C * r * jax.nn.softplus(-lam)
    a = jnp.exp(log_a)
    one_minus_a2 = -jnp.tanh(log_a) * (jnp.exp(2.0 * log_a) + 1.0)
    return a, jnp.sqrt(one_minus_a2) * (i * xc)


def _f_lrupost(h, gate):
    return (h * jax.nn.gelu(gate),)


def _f_logsig(zf, bf):
    return (jax.nn.log_sigmoid(zf + bf),)


def _f_gnorm(ya, yb, yc, yd, g):
    def n(y):
        return y * lax.rsqrt(jnp.mean(jnp.square(y), axis=-1, keepdims=True) + RMS_EPS)
    return (jnp.concatenate([n(ya), n(yb), n(yc), n(yd)], axis=1) * g,)


sgu_mix = _make_rw(_f_sgu, "sgu", SGU_CHUNK, 2, [MIXER_WIDTH])
s5_disc = _make_rw(_f_s5disc, "s5disc", S5_GROUPS * S5_GROUP, 5, [S5_STATE] * 4)
s5_post = _make_rw(_f_s5post, "s5post", 256, 3, [MIXER_WIDTH])
lru_pre = _make_rw(_f_lrupre, "lrupre", 512, 1, [MIXER_WIDTH, MIXER_WIDTH])
lru_post = _make_rw(_f_lrupost, "lrupost", 512, 2, [MIXER_WIDTH])
log_sig = _make_rw(_f_logsig, "logsig", 512, 1, [128])


SCAN_TILE = 512


def _prev_spec(c, nt, rev):
    per = SCAN_TILE // 8
    if rev:
        return pl.BlockSpec((8, c), lambda i: (jnp.maximum((nt - 1 - i) * per - 1, 0), 0))
    return pl.BlockSpec((8, c), lambda i: (jnp.maximum(i * per - 1, 0), 0))


SCAN_STEPS = (1, 2, 4)


def _cmul(ar, ai, br, bi):
    return ar * br - ai * bi, ar * bi + ai * br


def _rows_down(x, k, fill, rowid):
    return jnp.where(rowid >= k, pltpu.roll(x, k, 0), fill)


def _rows_up(x, k, fill, rowid):
    return jnp.where(rowid < 8 - k, pltpu.roll(x, 8 - k, 0), fill)


def _powers(ar, ai):
    pw = [(ar, ai)]
    for _ in range(7):
        pw.append(_cmul(*pw[-1], ar, ai))
    return pw


def _block(i):
    return pl.ds(pl.multiple_of(i * 8, 8), 8)


def _row_before(ref, i, edge):
    return jnp.where(i == 0, edge, ref[pl.ds(jnp.maximum(i * 8 - 1, 0), 1), :])


def _lti_fwd_call(u, w_re, w_im, a_re, a_im):
    t, kdim = u.shape
    c = w_re.shape[1]
    tt = SCAN_TILE

    def body(u_ref, wr_ref, wi_ref, ar_ref, ai_ref, sr_ref, si_ref, br_ref, bi_ref, cr, ci):
        @pl.when(pl.program_id(0) == 0)
        def _():
            cr[...] = jnp.zeros_like(cr)
            ci[...] = jnp.zeros_like(ci)

        br_ref[...] = _dot(u_ref[...], wr_ref[...], 1, 0)
        bi_ref[...] = _dot(u_ref[...], wi_ref[...], 1, 0)
        pw = _powers(ar_ref[...], ai_ref[...])
        apr = jnp.concatenate([p[0] for p in pw], axis=0)
        api = jnp.concatenate([p[1] for p in pw], axis=0)
        rowid = lax.broadcasted_iota(jnp.int32, (8, c), 0)

        def block(i, carry):
            xr, xi = br_ref[_block(i), :], bi_ref[_block(i), :]
            for k in SCAN_STEPS:
                dr, di = _cmul(*pw[k - 1], _rows_down(xr, k, 0.0, rowid), _rows_down(xi, k, 0.0, rowid))
                xr, xi = xr + dr, xi + di
            dr, di = _cmul(apr, api, *carry)
            xr, xi = xr + dr, xi + di
            sr_ref[_block(i), :] = xr
            si_ref[_block(i), :] = xi
            return xr[7:8, :], xi[7:8, :]

        hr, hi = lax.fori_loop(0, tt // 8, block, (cr[...], ci[...]), unroll=2)
        cr[...] = hr
        ci[...] = hi

    row = pl.BlockSpec((tt, c), lambda i: (i, 0))
    par = pl.BlockSpec((1, c), lambda i: (0, 0))
    return pl.pallas_call(
        body, name="lti_scan_f", grid=(t // tt,),
        in_specs=[pl.BlockSpec((tt, kdim), lambda i: (i, 0)), _full_spec(w_re), _full_spec(w_im), par, par],
        out_specs=[row, row], out_shape=[jax.ShapeDtypeStruct((t, c), F32)] * 2,
        scratch_shapes=[pltpu.VMEM((tt, c), F32)] * 2 + [pltpu.VMEM((1, c), F32)] * 2,
        compiler_params=_cp(("arbitrary",)))(u, w_re, w_im, a_re, a_im)


def _lti_bwd_call(u, w_re, w_im, a_re, a_im, s_re, s_im, g_re, g_im):
    t, c = g_re.shape
    kdim = u.shape[1]
    tt = SCAN_TILE
    nt = t // tt
    nb = tt // 8

    def body(u_ref, wr_ref, wi_ref, ar_ref, ai_ref, sr_ref, si_ref, pr_ref, pi_ref, gr_ref, gi_ref,
             du_ref, dwr_ref, dwi_ref, dar_ref, dai_ref, or_ref, oi_ref, cr, ci):
        ti = pl.program_id(0)

        @pl.when(ti == 0)
        def _():
            cr[...] = jnp.zeros_like(cr)
            ci[...] = jnp.zeros_like(ci)
            dar_ref[...] = jnp.zeros_like(dar_ref)
            dai_ref[...] = jnp.zeros_like(dai_ref)
            dwr_ref[...] = jnp.zeros_like(dwr_ref)
            dwi_ref[...] = jnp.zeros_like(dwi_ref)

        pw = _powers(ar_ref[...], -ai_ref[...])
        tpr = jnp.concatenate([p[0] for p in reversed(pw)], axis=0)
        tpi = jnp.concatenate([p[1] for p in reversed(pw)], axis=0)
        rowid = lax.broadcasted_iota(jnp.int32, (8, c), 0)
        first = ti == nt - 1
        edge_r = jnp.where(first, 0.0, pr_ref[7:8, :])
        edge_i = jnp.where(first, 0.0, pi_ref[7:8, :])

        def block(kk, carry):
            i = nb - 1 - kk
            gr_c, gi_c, acc_r, acc_i = carry
            xr, xi = gr_ref[_block(i), :], gi_ref[_block(i), :]
            for k in SCAN_STEPS:
                dr, di = _cmul(*pw[k - 1], _rows_up(xr, k, 0.0, rowid), _rows_up(xi, k, 0.0, rowid))
                xr, xi = xr + dr, xi + di
            dr, di = _cmul(tpr, tpi, gr_c, gi_c)
            xr, xi = xr + dr, xi + di
            or_ref[_block(i), :] = xr
            oi_ref[_block(i), :] = xi
            spr = _rows_down(sr_ref[_block(i), :], 1, _row_before(sr_ref, i, edge_r), rowid)
            spi = _rows_down(si_ref[_block(i), :], 1, _row_before(si_ref, i, edge_i), rowid)
            return xr[0:1, :], xi[0:1, :], acc_r + spr * xr + spi * xi, acc_i + spr * xi - spi * xr

        zero = jnp.zeros((8, c), F32)
        gr_c, gi_c, acc_r, acc_i = lax.fori_loop(0, nb, block, (cr[...], ci[...], zero, zero), unroll=2)
        cr[...] = gr_c
        ci[...] = gi_c
        dar_ref[...] += jnp.sum(acc_r, axis=0, keepdims=True)
        dai_ref[...] += jnp.sum(acc_i, axis=0, keepdims=True)
        du_ref[...] = _dot(or_ref[...], wr_ref[...], 1, 1) + _dot(oi_ref[...], wi_ref[...], 1, 1)
        dwr_ref[...] += _dot(u_ref[...], or_ref[...], 0, 0)
        dwi_ref[...] += _dot(u_ref[...], oi_ref[...], 0, 0)

    row = pl.BlockSpec((tt, c), lambda i: (nt - 1 - i, 0))
    row_u = pl.BlockSpec((tt, kdim), lambda i: (nt - 1 - i, 0))
    par = pl.BlockSpec((1, c), lambda i: (0, 0))
    prev = _prev_spec(c, nt, True)
    return pl.pallas_call(
        body, name="lti_scan_b", grid=(nt,),
        in_specs=[row_u, _full_spec(w_re), _full_spec(w_im), par, par, row, row, prev, prev, row, row],
        out_specs=[row_u, _full_spec(w_re), _full_spec(w_im), par, par],
        out_shape=[jax.ShapeDtypeStruct((t, kdim), F32), jax.ShapeDtypeStruct(w_re.shape, F32),
                   jax.ShapeDtypeStruct(w_im.shape, F32)] + [jax.ShapeDtypeStruct((1, c), F32)] * 2,
        scratch_shapes=[pltpu.VMEM((tt, c), F32)] * 2 + [pltpu.VMEM((1, c), F32)] * 2,
        compiler_params=_cp(("arbitrary",)))(u, w_re, w_im, a_re, a_im, s_re, s_im, s_re, s_im, g_re, g_im)


@jax.custom_vjp
def lti_scan(u, w_re, w_im, a_re, a_im):
    return tuple(_lti_fwd_call(u, w_re, w_im, a_re, a_im))


def _lti_scan_fwd(u, w_re, w_im, a_re, a_im):
    s_re, s_im = _lti_fwd_call(u, w_re, w_im, a_re, a_im)
    return (s_re, s_im), (u, w_re, w_im, a_re, a_im, s_re, s_im)


def _lti_scan_bwd(r, g):
    return tuple(_lti_bwd_call(*r, g[0], g[1]))


lti_scan.defvjp(_lti_scan_fwd, _lti_scan_bwd)


def _tv_fwd_call(a, b):
    t, c = b.shape
    tt = SCAN_TILE

    def body(a_ref, b_ref, h_ref, ch):
        @pl.when(pl.program_id(0) == 0)
        def _():
            ch[...] = jnp.zeros_like(ch)

        rowid = lax.broadcasted_iota(jnp.int32, (8, c), 0)

        def block(i, h):
            ab, x = a_ref[_block(i), :], b_ref[_block(i), :]
            for k in SCAN_STEPS:
                x = x + ab * _rows_down(x, k, 0.0, rowid)
                ab = ab * _rows_down(ab, k, 1.0, rowid)
            x = x + ab * h
            h_ref[_block(i), :] = x
            return x[7:8, :]

        ch[...] = lax.fori_loop(0, tt // 8, block, ch[...], unroll=2)

    row = pl.BlockSpec((tt, c), lambda i: (i, 0))
    return pl.pallas_call(
        body, name="tv_scan_f", grid=(t // tt,), in_specs=[row, row], out_specs=row,
        out_shape=jax.ShapeDtypeStruct((t, c), F32), scratch_shapes=[pltpu.VMEM((1, c), F32)],
        compiler_params=_cp(("arbitrary",)))(a, b)


def _tv_bwd_call(a, h, g):
    t, c = g.shape
    tt = SCAN_TILE
    nt = t // tt
    nb = tt // 8

    def body(a_ref, h_ref, p_ref, g_ref, da_ref, db_ref, cg, ca):
        ti = pl.program_id(0)

        @pl.when(ti == 0)
        def _():
            cg[...] = jnp.zeros_like(cg)
            ca[...] = jnp.zeros_like(ca)

        rowid = lax.broadcasted_iota(jnp.int32, (8, c), 0)
        edge = jnp.where(ti == nt - 1, 0.0, p_ref[7:8, :])

        def block(kk, carry):
            i = nb - 1 - kk
            gc, a_next = carry
            ab, x = a_ref[_block(i), :], g_ref[_block(i), :]
            cb = _rows_up(ab, 1, a_next, rowid)
            for k in SCAN_STEPS:
                x = x + cb * _rows_up(x, k, 0.0, rowid)
                cb = cb * _rows_up(cb, k, 1.0, rowid)
            x = x + cb * gc
            db_ref[_block(i), :] = x
            da_ref[_block(i), :] = x * _rows_down(h_ref[_block(i), :], 1, _row_before(h_ref, i, edge), rowid)
            return x[0:1, :], ab[0:1, :]

        gc, a_next = lax.fori_loop(0, nb, block, (cg[...], ca[...]), unroll=2)
        cg[...] = gc
        ca[...] = a_next

    row = pl.BlockSpec((tt, c), lambda i: (nt - 1 - i, 0))
    return pl.pallas_call(
        body, name="tv_scan_b", grid=(nt,), in_specs=[row, row, _prev_spec(c, nt, True), row],
        out_specs=[row, row], out_shape=[jax.ShapeDtypeStruct((t, c), F32)] * 2,
        scratch_shapes=[pltpu.VMEM((1, c), F32)] * 2, compiler_params=_cp(("arbitrary",)))(a, h, h, g)


@jax.custom_vjp
def tv_scan(a, b):
    return _tv_fwd_call(a, b)


def _tv_scan_fwd(a, b):
    h = _tv_fwd_call(a, b)
    return h, (a, h)


def _tv_scan_bwd(r, g):
    a, h = r
    return tuple(_tv_bwd_call(a, h, g))


tv_scan.defvjp(_tv_scan_fwd, _tv_scan_bwd)


CONV_K = 4
CONV_ROWS = 512


def _conv_fwd_call(x, w, b):
    t, c = x.shape

    def body(x_ref, w_ref, b_ref, o_ref, xp):
        xp[0:8, :] = jnp.zeros((8, c), F32)
        xp[8:, :] = x_ref[...]
        for blk in range(t // CONV_ROWS):
            base = blk * CONV_ROWS
            acc = jnp.broadcast_to(b_ref[...], (CONV_ROWS, c))
            for kk in range(CONV_K):
                acc = acc + w_ref[kk:kk + 1, :] * xp[base + 5 + kk:base + 5 + kk + CONV_ROWS, :]
            o_ref[base:base + CONV_ROWS, :] = acc

    return pl.pallas_call(
        body, name="conv_f", out_shape=jax.ShapeDtypeStruct((t, c), F32),
        scratch_shapes=[pltpu.VMEM((t + 8, c), F32)], compiler_params=_cp())(x, w, b)


def _conv_bwd_call(x, w, g):
    t, c = x.shape

    def body(x_ref, w_ref, g_ref, dx_ref, dw_ref, db_ref, xp, gp):
        xp[0:8, :] = jnp.zeros((8, c), F32)
        xp[8:, :] = x_ref[...]
        gp[0:t, :] = g_ref[...]
        gp[t:, :] = jnp.zeros((8, c), F32)
        dw = [jnp.zeros((1, c), F32) for _ in range(CONV_K)]
        db = jnp.zeros((1, c), F32)
        for blk in range(t // CONV_ROWS):
            base = blk * CONV_ROWS
            gb = g_ref[base:base + CONV_ROWS, :]
            acc = jnp.zeros((CONV_ROWS, c), F32)
            for kk in range(CONV_K):
                acc = acc + w_ref[kk:kk + 1, :] * gp[base + 3 - kk:base + 3 - kk + CONV_ROWS, :]
                dw[kk] = dw[kk] + jnp.sum(gb * xp[base + 5 + kk:base + 5 + kk + CONV_ROWS, :], axis=0, keepdims=True)
            db = db + jnp.sum(gb, axis=0, keepdims=True)
            dx_ref[base:base + CONV_ROWS, :] = acc
        for kk in range(CONV_K):
            dw_ref[kk:kk + 1, :] = dw[kk]
        db_ref[...] = db

    return pl.pallas_call(
        body, name="conv_b",
        out_shape=[jax.ShapeDtypeStruct((t, c), F32), jax.ShapeDtypeStruct((CONV_K, c), F32),
                   jax.ShapeDtypeStruct((1, c), F32)],
        scratch_shapes=[pltpu.VMEM((t + 8, c), F32)] * 2, compiler_params=_cp())(x, w, g)


@jax.custom_vjp
def causal_conv(x, w, b):
    return _conv_fwd_call(x, w, b)


def _causal_conv_fwd(x, w, b):
    return _conv_fwd_call(x, w, b), (x, w)


def _causal_conv_bwd(r, g):
    return tuple(_conv_bwd_call(r[0], r[1], g))


causal_conv.defvjp(_causal_conv_fwd, _causal_conv_bwd)


ATT_TILE = 512
ATT_SCALE = HEAD_DIM ** -0.5


def _head_lane(val, lane, h):
    return jnp.sum(jnp.where(lane == h, val, 0.0), axis=1, keepdims=True)


def _attn_fwd_call(qkv, c128, cr, next_shard=None, late=None):
    t, w = qkv.shape[0], MIXER_WIDTH
    tq = ATT_TILE
    nq = t // tq
    qkv3, cr4 = qkv.reshape(nq, tq, 3 * w), cr.reshape(N_HEADS, nq, 1, tq)
    fused = next_shard is not None
    both = late is not None

    def body(*refs):
        q_ref, k_ref, v_ref, c_ref, cr_ref = refs[:5]
        i, h = pl.program_id(0), pl.program_id(1)
        rest = list(refs[5:])
        shard_ref = rest.pop(0) if fused else None
        own_ref = rest.pop(0) if both else None
        if both:
            rest.pop(0)
        o_ref, lse_ref = rest.pop(0), rest.pop(0)
        gathers = []
        if fused:
            gathers.append(_allgather_copies(shard_ref, rest.pop(0), rest[-2 - 2 * both], rest[-1 - 2 * both], EARLY_ROWS))
        if both:
            gathers.append(_allgather_copies(own_ref, rest.pop(0), rest[-2], rest[-1], LATE_ROWS))
        for start, _, _ in gathers:
            pl.when((i == 0) & (h == 0))(start)
        hm = lax.broadcasted_iota(jnp.int32, (tq, w), 1) // HEAD_DIM == h
        lane = lax.broadcasted_iota(jnp.int32, (tq, 128), 1)
        qs = jnp.where(hm, q_ref[...] * ATT_SCALE, 0.0)
        cq = _head_lane(c_ref[...], lane, h)
        causal = lax.broadcasted_iota(jnp.int32, (tq, tq), 0) >= lax.broadcasted_iota(jnp.int32, (tq, tq), 1)

        half = tq // 2

        def update(j, carry, diagonal):
            kj, vj, ck = k_ref[j], v_ref[j], cr_ref[0, j]
            out = []
            for r in range(2):
                rows = slice(r * half, (r + 1) * half)
                m, l, acc = carry[3 * r:3 * r + 3]
                s = _dot(qs[rows], kj, 1, 1) - ck
                if diagonal:
                    s = jnp.where(causal[rows], s, NEG)
                m_new = jnp.maximum(m, cq[rows] + jnp.max(s, axis=1, keepdims=True))
                p = jnp.exp(s + (cq[rows] - m_new))
                alpha = jnp.exp(m - m_new)
                out += [m_new, alpha * l + jnp.sum(p, axis=1, keepdims=True), alpha * acc + _dot(p, vj, 1, 0)]
            return tuple(out)

        init = (jnp.full((half, 1), NEG, F32), jnp.zeros((half, 1), F32), jnp.zeros((half, w), F32)) * 2
        carry = lax.fori_loop(0, i, lambda j, c: update(j, c, False), init)
        carry = update(i, carry, True)
        m, l, acc = (jnp.concatenate([carry[k], carry[3 + k]], axis=0) for k in range(3))
        out = jnp.where(hm, acc / l, 0.0)
        lse = jnp.where(lane == h, m + jnp.log(l), 0.0)

        @pl.when(h == 0)
        def _():
            o_ref[...] = out
            lse_ref[...] = lse

        @pl.when(h > 0)
        def _():
            o_ref[...] += out
            lse_ref[...] += lse

        for _, forward, finish in gathers:
            pl.when((i == nq - 1) & (h == 0))(forward)
            pl.when((i == nq - 1) & (h == N_HEADS - 1))(finish)

    tile = pl.BlockSpec((tq, w), lambda i, h: (i, 0))
    tile_c = pl.BlockSpec((tq, 128), lambda i, h: (i, 0))
    rows = pl.BlockSpec((1, nq, 1, tq), lambda i, h: (h, 0, 0, 0))
    q_spec = pl.BlockSpec((None, tq, w), lambda i, h: (i, 0, 0))
    k_spec = pl.BlockSpec((nq, tq, w), lambda i, h: (0, 0, 1))
    v_spec = pl.BlockSpec((nq, tq, w), lambda i, h: (0, 0, 2))
    ins = [qkv3, qkv3, qkv3, c128, cr4]
    in_specs, out_specs = [q_spec, k_spec, v_spec, tile_c, rows], [tile, tile_c]
    out_shape = [jax.ShapeDtypeStruct((t, w), F32), jax.ShapeDtypeStruct((t, 128), F32)]
    scratch, aliases, name = [], {}, "attn_f"
    sem_pair = [pltpu.SemaphoreType.DMA((AG_SEMS,)), pltpu.SemaphoreType.DMA((AG_SEMS,))]
    if fused:
        ins.append(next_shard)
        in_specs.append(HBM)
        name += "_next"
    if both:
        aliases = {len(ins) + 1: 2 + fused}
        ins += list(late)
        in_specs += [HBM, HBM]
        name += "_late"
    if fused:
        out_specs.append(HBM)
        out_shape.append(jax.ShapeDtypeStruct((4,) + next_shard.shape, next_shard.dtype))
        scratch += sem_pair
    if both:
        out_specs.append(HBM)
        out_shape.append(jax.ShapeDtypeStruct(late[1].shape, late[1].dtype))
        scratch += sem_pair
    return pl.pallas_call(
        body, name=name, grid=(nq, N_HEADS), in_specs=in_specs, out_specs=out_specs, out_shape=out_shape,
        scratch_shapes=scratch, input_output_aliases=aliases,
        compiler_params=_cp(("arbitrary", "arbitrary")))(*ins)


def _attn_bwd_call(qkv, c128, cr, o, lse, do, exchange=None):
    t, w = qkv.shape[0], MIXER_WIDTH
    tq = ATT_TILE
    nq = t // tq
    r3 = lambda a: a.reshape(nq, tq, a.shape[-1])
    cr4 = cr.reshape(N_HEADS, nq, 1, tq)
    fused = exchange is not None

    def body(*refs):
        q_ref, k_ref, v_ref, c_ref, cr_ref, o_ref, lse_ref, do_ref = refs[:8]
        j, h = pl.program_id(0), pl.program_id(1)
        if fused:
            p_ref, dq_ref, dk_ref, dv_ref, dc_ref, dcr_ref, recv_ref, send_sems, recv_sems = refs[8:]
            start, finish = _chip_exchange_copies(p_ref, recv_ref, send_sems, recv_sems)
            pl.when((j == 0) & (h == 0))(start)
        else:
            dq_ref, dk_ref, dv_ref, dc_ref, dcr_ref = refs[8:]

        @pl.when((j == 0) & (h == 0))
        def _():
            dq_ref[...] = jnp.zeros_like(dq_ref)
            dc_ref[...] = jnp.zeros_like(dc_ref)

        hm = lax.broadcasted_iota(jnp.int32, (tq, w), 1) // HEAD_DIM == h
        lane = lax.broadcasted_iota(jnp.int32, (tq, 128), 1)
        kj = k_ref[...]
        vj = v_ref[...]
        ck = cr_ref[0, 0]
        causal = lax.broadcasted_iota(jnp.int32, (tq, tq), 0) >= lax.broadcasted_iota(jnp.int32, (tq, tq), 1)

        def step(i, carry, diagonal):
            dk, dv, dck = carry
            qm = jnp.where(hm, q_ref[i], 0.0)
            dom = jnp.where(hm, do_ref[i], 0.0)
            s = _dot(qm * ATT_SCALE, kj, 1, 1) - ck
            if diagonal:
                s = jnp.where(causal, s, NEG)
            p = jnp.exp(s + (_head_lane(c_ref[i], lane, h) - _head_lane(lse_ref[i], lane, h)))
            dv = dv + _dot(p, dom, 0, 0)
            dp = _dot(dom, vj, 1, 1)
            delta = jnp.sum(dom * o_ref[i], axis=1, keepdims=True)
            ds = p * (dp - delta)
            dq_ref[i] += jnp.where(hm, _dot(ds, kj, 1, 0), 0.0) * ATT_SCALE
            dk = dk + _dot(ds, qm, 0, 0) * ATT_SCALE
            dc_ref[i] += jnp.where(lane == h, jnp.sum(ds, axis=1, keepdims=True), 0.0)
            return dk, dv, dck - jnp.sum(ds, axis=0, keepdims=True)

        init = (jnp.zeros((tq, w), F32), jnp.zeros((tq, w), F32), jnp.zeros((1, tq), F32))
        carry = step(j, init, True)
        dk, dv, dck = lax.fori_loop(j + 1, nq, lambda i, c: step(i, c, False), carry)
        dcr_ref[0, 0] = dck

        @pl.when(h == 0)
        def _():
            dk_ref[...] = dk
            dv_ref[...] = dv

        @pl.when(h > 0)
        def _():
            dk_ref[...] += dk
            dv_ref[...] += dv

        if fused:
            pl.when((j == nq - 1) & (h == N_HEADS - 1))(finish)

    whole = pl.BlockSpec((nq, tq, w), lambda j, h: (0, 0, 0))
    whole_c = pl.BlockSpec((nq, tq, 128), lambda j, h: (0, 0, 0))
    tile = pl.BlockSpec((None, tq, w), lambda j, h: (j, 0, 0))
    tile_r = pl.BlockSpec((1, 1, 1, tq), lambda j, h: (h, j, 0, 0))
    s3 = jax.ShapeDtypeStruct((nq, tq, w), F32)
    qkv3 = r3(qkv)
    k_tile = pl.BlockSpec((None, tq, w), lambda j, h: (j, 0, 1))
    v_tile = pl.BlockSpec((None, tq, w), lambda j, h: (j, 0, 2))
    ins = [qkv3, qkv3, qkv3, r3(c128), cr4, r3(o), r3(lse), r3(do)]
    in_specs = [whole, k_tile, v_tile, whole_c, tile_r, whole, whole_c, whole]
    out_specs = [whole, tile, tile, whole_c, tile_r]
    out_shape = [s3, s3, s3, jax.ShapeDtypeStruct((nq, tq, 128), F32), jax.ShapeDtypeStruct((N_HEADS, nq, 1, tq), F32)]
    scratch = []
    if fused:
        ins.append(exchange)
        in_specs.append(HBM)
        out_specs.append(HBM)
        out_shape.append(jax.ShapeDtypeStruct((3,) + exchange.shape[1:], exchange.dtype))
        scratch = [pltpu.SemaphoreType.DMA((3,)), pltpu.SemaphoreType.DMA((3,))]
    dq, dk, dv, dc, dcr, *received = pl.pallas_call(
        body, name="attn_b_exchange" if fused else "attn_b", grid=(nq, N_HEADS), in_specs=in_specs,
        out_specs=out_specs, out_shape=out_shape, scratch_shapes=scratch,
        compiler_params=_cp(("arbitrary", "arbitrary")))(*ins)
    grads = (dq.reshape(t, w), dk.reshape(t, w), dv.reshape(t, w), dc.reshape(t, 128), dcr.reshape(N_HEADS, 1, t))
    return grads, (received[0] if fused else None)


def _loss_call(x, g, target):
    t, d = x.shape
    tr = 512

    def body(x_ref, g_ref, t_ref, loss_ref, dx_ref, dg_ref):
        tgt = t_ref[...]

        def f(xv, gv):
            return 0.5 * jnp.sum(jnp.mean(jnp.square(_rms(xv, gv) - tgt), axis=-1))

        val, vjp = jax.vjp(f, x_ref[...], g_ref[...])
        dx, dg = vjp(jnp.ones((), F32))
        dx_ref[...] = dx

        @pl.when(pl.program_id(0) == 0)
        def _():
            loss_ref[...] = jnp.zeros_like(loss_ref)
            dg_ref[...] = jnp.zeros_like(dg_ref)

        loss_ref[...] += jnp.full(loss_ref.shape, val, F32)
        dg_ref[...] += dg

    row = _row_spec(tr, d)
    return pl.pallas_call(
        body, name="loss_head", grid=(t // tr,), in_specs=[row, _full_spec(g), row],
        out_specs=[pl.BlockSpec((1, 128), lambda i: (0, 0)), row, _full_spec(g)],
        out_shape=[jax.ShapeDtypeStruct((1, 128), F32), jax.ShapeDtypeStruct((t, d), F32),
                   jax.ShapeDtypeStruct(g.shape, F32)],
        compiler_params=_cp(("arbitrary",)))(x, g, target)


def _blockdiag(w):
    l, g, a, b = w.shape
    return jnp.einsum('lgab,gk->lgakb', w, jnp.eye(g, dtype=w.dtype)).reshape(l, g * a, g * b)


def _prepare(rep):
    d = DEPTH
    w = MIXER_WIDTH
    rows = S5_GROUPS * S5_GROUP
    rep16 = lambda a: jnp.repeat(a, S5_GROUP, axis=1).reshape(d * rows, -1)
    bt = lambda b: b.transpose(0, 1, 3, 2).reshape(d * rows, S5_STATE)
    abar_re, abar_im, bb_re, bb_im = s5_disc(
        rep16(rep["s5_lambda_re"]), rep16(rep["s5_lambda_im"]), rep16(rep["s5_log_dt"][:, :, None]),
        bt(rep["s5_b_re"]), bt(rep["s5_b_im"]))
    g4 = lambda a: a.reshape(d, S5_GROUPS, S5_GROUP, S5_STATE)
    first = lambda a: g4(a)[:, :, 0, :].reshape(d, 1, S5_GROUPS * S5_STATE)
    cblk = lambda c: _blockdiag(c.transpose(0, 1, 3, 2))
    row = lambda a: a.reshape(d, 1, -1)
    return dict(
        sgu_norm_g=row(rep["sgu_norm_g"]), sgu_w=rep["sgu_w"],
        sgu_bias=jnp.repeat(rep["sgu_b"].transpose(0, 2, 1), HEAD_DIM, axis=2),
        abar_re=first(abar_re), abar_im=first(abar_im), bblk_re=_blockdiag(g4(bb_re)), bblk_im=_blockdiag(g4(bb_im)),
        cblk_re=cblk(rep["s5_c_re"]), cblk_im=cblk(rep["s5_c_im"]), s5_d=row(rep["s5_d"]), s5_glu_b=row(rep["s5_glu_b"]),
        lru_conv_b=row(rep["lru_conv_b"]), lru_wa=_blockdiag(rep["lru_wa"]), lru_ba=row(rep["lru_ba"]),
        lru_wx=_blockdiag(rep["lru_wx"]), lru_bx=row(rep["lru_bx"]), lru_lambda=row(rep["lru_lambda"]),
        fgate_b=jnp.pad(rep["fox_fgate_b"], ((0, 0), (0, 128 - N_HEADS)))[:, None, :])


PREPARED_FROM = ('sgu_norm_g', 'sgu_w', 'sgu_b', 's5_lambda_re', 's5_lambda_im', 's5_log_dt', 's5_b_re', 's5_b_im',
                 's5_c_re', 's5_c_im', 's5_d', 's5_glu_b', 'lru_conv_b', 'lru_wa', 'lru_ba', 'lru_wx', 'lru_bx',
                 'lru_lambda', 'fox_fgate_b')


def _mixers_pre(pieces, p, glu_w, conv_w):
    a_u, a_v, b_in, c_x, c_gate, d_q, d_k, d_v, d_f = pieces
    sw = p["sgu_w"]
    (y_a,) = sgu_mix(a_u, a_v, p["sgu_norm_g"], sw[0], sw[1], sw[2], sw[3], p["sgu_bias"])
    s_re, s_im = lti_scan(b_in, p["bblk_re"], p["bblk_im"], p["abar_re"], p["abar_im"])
    (y_b,) = s5_post(s_re, s_im, b_in, p["cblk_re"], p["cblk_im"], p["s5_d"], glu_w, p["s5_glu_b"])
    xc = causal_conv(c_x, conv_w, p["lru_conv_b"])
    a, b = lru_pre(xc, p["lru_wa"], p["lru_ba"], p["lru_wx"], p["lru_bx"], p["lru_lambda"])
    (y_c,) = lru_post(tv_scan(a, b), c_gate)
    (log_f,) = log_sig(d_f, p["fgate_b"])
    c128 = tv_scan(jnp.ones_like(log_f), log_f)
    return y_a, y_b, y_c, d_q, d_k, d_v, c128, c128[:, :N_HEADS].T[:, None, :]


PACK_COLS = 1024
SHARD_SHAPE = {'w_mlp_in': (1024, 1024), 'w_mlp_out': (1024, 1024), 'w_out': (256, 1024), 'w_in': (1024, 513),
               's5_glu_w': (64, 256), 'lru_conv_w': (4, 64)}
SHARDED_AXIS = {'w_in': 1, 's5_glu_w': 0, 'lru_conv_w': 1, 'w_out': 0, 'w_mlp_in': 1, 'w_mlp_out': 0}
SHARD_ROWS = {n: -(-s[0] * s[1] // PACK_COLS) for n, s in SHARD_SHAPE.items()}
SHARD_OFF = {n: sum(list(SHARD_ROWS.values())[:i]) for i, n in enumerate(SHARD_SHAPE)}
LAYER_ROWS = 2880
SMALL_OFF = SHARD_OFF['w_in']
GLUE_ROWS = LAYER_ROWS - SMALL_OFF
assert SHARD_OFF['w_mlp_out'] == 1024 and SHARD_OFF['w_out'] == 2048 and SMALL_OFF % GLUE_ROWS == 0
assert SHARD_OFF['lru_conv_w'] + SHARD_ROWS['lru_conv_w'] <= LAYER_ROWS
PACK_ROWS = DEPTH * LAYER_ROWS
TOK = 1024
FF = 4 * D_MODEL


def _w3(i_of):
    return pl.BlockSpec((None, 1024, PACK_COLS), i_of)


def _tile2(rows, cols, i_of):
    return pl.BlockSpec((rows, cols), i_of)


def _layer_fwd(x, h1, gathered, w_in, p, mix_p, next_shard, own_shard, next_gain):
    t = x.shape[0]
    nt = t // TOK
    f32 = lambda r, c: jax.ShapeDtypeStruct((r, c), F32)
    b16 = lambda r, c: jax.ShapeDtypeStruct((r, c), BF16)
    g2, gm = p["norm2_g"][None, :], p["mix_norm_g"][None, :]
    qkv_cols = (5 * MIXER_WIDTH, 8 * MIXER_WIDTH)
    z, qkv = _matmul("mm_in", (nt, 1, 1), h1, _tile2(TOK, D_MODEL, lambda i, j, k: (i, 0)),
                     w_in, _tile2(D_MODEL, D_IN_PAD, lambda i, j, k: (0, 0)), (1, 0),
                     [(f32(t, D_IN_PAD), _tile2(TOK, D_IN_PAD, lambda i, j, k: (i, 0))),
                      (b16(t, 3 * MIXER_WIDTH), _tile2(TOK, 3 * MIXER_WIDTH, lambda i, j, k: (i, 0)))],
                     epilogue=lambda acc: (acc, acc[:, qkv_cols[0]:qkv_cols[1]]))
    pieces = tuple(jnp.split(z, [MIXER_WIDTH * i for i in range(1, 9)], axis=1))
    (y_a, y_b, y_c, _, _, _, c128, cr), mix_vjp = jax.vjp(_mixers_pre, pieces, *mix_p)
    attn_in = (qkv, c128, cr)
    y_d, lse, *more = _attn_fwd_call(*attn_in, next_shard=next_shard,
                                     late=None if own_shard is None else (own_shard, gathered))
    if own_shard is not None:
        gathered = more.pop()
    w_out = _join_chips(_unpack_shards(gathered[:, None], ('w_out',))['w_out'], SHARDED_AXIS['w_out'])[0]
    ys = (y_a, y_b, y_c, y_d)
    (yn,) = _rowwise(_f_gnorm, list(ys), [gm], [D_MODEL], name="gnorm_f", tr=512, dtype=BF16)
    x_tile = _tile2(TOK, D_MODEL, lambda i, j, k: (i, 0))
    gain = _tile2(1, D_MODEL, lambda i, j, k: (0, 0))

    def add_and_norm(acc, r, gv):
        s = acc + r
        return s, _rms(s, gv)

    x1, h2 = _matmul("mm_out", (nt, 1, 1), yn, x_tile, w_out, _tile2(D_MODEL, D_MODEL, lambda i, j, k: (0, 0)), (1, 0),
                     [(f32(t, D_MODEL), x_tile), (b16(t, D_MODEL), x_tile)], extras=[(x, x_tile), (g2, gain)],
                     epilogue=add_and_norm)
    ff_tile = _tile2(TOK, 1024, lambda i, j, k: (i, j))
    act = _matmul("mm_up", (nt, FF // 1024, 1), h2, x_tile, gathered, _w3(lambda i, j, k: (j, 0, 0)), (1, 0),
                  [(b16(t, FF), ff_tile)], epilogue=lambda acc: (jnp.square(jnp.maximum(acc, 0.0)),))
    down = ("mm_down", (nt, 1, FF // 1024), act, _tile2(TOK, 1024, lambda i, j, k: (i, k)),
            gathered, _w3(lambda i, j, k: (k, 1, 0)), (1, 0))
    if next_gain is None:
        x2, h_next = _matmul(*down, [(f32(t, D_MODEL), x_tile)], extras=[(x1, x_tile)],
                             epilogue=lambda acc, r: (acc + r,)), None
    else:
        x2, h_next = _matmul(*down, [(f32(t, D_MODEL), x_tile), (b16(t, D_MODEL), x_tile)],
                             extras=[(x1, x_tile), (next_gain[None, :], gain)], epilogue=add_and_norm)
    res = (x, h1, mix_vjp, ys, attn_in, lse, yn, x1, h2, act, gathered, w_out)
    return x2, h_next, res, (more[0] if more else None)


def _layer_bwd(g, res, w_in, p, later_send):
    x, h1, mix_vjp, ys, attn_in, lse, yn, x1, h2, act, gathered, w_out = res
    half_rows = LAYER_ROWS // 2
    riding = later_send is not None
    halves = later_send.reshape(4, 2, half_rows, PACK_COLS) if riding else None
    send = lax.empty((4, LAYER_ROWS, PACK_COLS), BF16)
    t = x.shape[0]
    nt = t // TOK
    f32 = lambda r, c: jax.ShapeDtypeStruct((r, c), F32)
    g1, g2, gm = p["norm1_g"][None, :], p["norm2_g"][None, :], p["mix_norm_g"][None, :]
    x_tile = _tile2(TOK, D_MODEL, lambda i, j, k: (i, 0))
    ff_tile = _tile2(TOK, 1024, lambda i, j, k: (i, j))
    tok_k = _tile2(TOK, D_MODEL, lambda i, j, k: (k, 0))
    send_s = jax.ShapeDtypeStruct(send.shape, send.dtype)
    pair_rider = (halves, jax.ShapeDtypeStruct((4, half_rows, PACK_COLS), BF16), False, _pair_exchange_copies)
    du = _matmul("mm_down_dx", (nt, FF // 1024, 1), g, x_tile, gathered, _w3(lambda i, j, k: (j, 1, 0)), (1, 1),
                 [(jax.ShapeDtypeStruct((t, FF), BF16), ff_tile)], extras=[(act, ff_tile)],
                 epilogue=lambda acc, a: (2.0 * jnp.sqrt(a.astype(F32)) * acc,),
                 rider=pair_rider if riding else None)
    exchange = None
    if riding:
        du, from_sibling = du
        exchange = _add_kept(halves, from_sibling, REDUCE_ROWS)
    send = _matmul("mm_down_dw", (FF // 1024, 1, nt), act, _tile2(TOK, 1024, lambda i, j, k: (k, i)), g, tok_k, (0, 0),
                   [(send_s, _w3(lambda i, j, k: (i, 1, 0)))], into=send)
    send = _matmul("mm_up_dw", (1, FF // 1024, nt), h2, tok_k, du, _tile2(TOK, 1024, lambda i, j, k: (k, j)), (0, 0),
                   [(send_s, _w3(lambda i, j, k: (j, 0, 0)))], into=send)
    gain = _tile2(1, D_MODEL, lambda i, j, k: (0, 0))

    def norm_bwd(dh, xv, gv, through):
        _, vjp = jax.vjp(_rms, xv, gv)
        dxv, dgv = vjp(dh)
        return dxv + through, dgv

    g_mid, dg2 = _matmul("mm_up_dx", (nt, 1, FF // 1024), du, _tile2(TOK, 1024, lambda i, j, k: (i, k)),
                         gathered, _w3(lambda i, j, k: (k, 0, 0)), (1, 1),
                         [(f32(t, D_MODEL), x_tile), (f32(1, D_MODEL), gain)],
                         extras=[(x1, x_tile), (g2, gain), (g, x_tile)], epilogue=norm_bwd, summed=1)
    w_full = _tile2(D_MODEL, D_MODEL, lambda i, j, k: (0, 0))
    dyn = _matmul("mm_out_dx", (nt, 1, 1), g_mid, x_tile, w_out, w_full, (1, 1), [(f32(t, D_MODEL), x_tile)])
    quarter = D_MODEL // 4
    send = _matmul("mm_out_dw", (1, 1, nt), yn, tok_k, g_mid, tok_k, (0, 0),
                   [(send_s, pl.BlockSpec((4, quarter, PACK_COLS), lambda i, j, k: (0, SHARD_OFF['w_out'] // quarter, 0)))],
                   epilogue=lambda acc: (acc.reshape(4, quarter, PACK_COLS),), into=send, acc_shape=(D_MODEL, D_MODEL))
    dy_a, dy_b, dy_c, dy_d, dgm = _rowwise_vjp(_f_gnorm, list(ys), [gm], [dyn], name="gnorm_b", tr=512)
    d_attn_in, received = _attn_bwd_call(*attn_in, ys[3], lse, dy_d, exchange=exchange)
    d_pieces, *d_mix = mix_vjp((dy_a, dy_b, dy_c, *d_attn_in))
    dz = jnp.concatenate([d.astype(BF16) for d in d_pieces], axis=1)
    z_tile = _tile2(TOK, D_IN_PAD, lambda i, j, k: (i, 0))
    share_rider = None
    if riding:
        pair = _sum_chips(exchange, received, REDUCE_ROWS)
        share_rider = (pair, jax.ShapeDtypeStruct(pair.shape, pair.dtype), True, _pair_share_copies)
    d_w_in = _matmul("mm_in_dw", (1, 1, t // 512), h1, _tile2(512, D_MODEL, lambda i, j, k: (k, 0)),
                     dz, _tile2(512, D_IN_PAD, lambda i, j, k: (k, 0)), (0, 0),
                     [(f32(D_MODEL, D_IN_PAD), _tile2(D_MODEL, D_IN_PAD, lambda i, j, k: (0, 0)))], rider=share_rider)
    reduced = None
    if riding:
        d_w_in, pair = d_w_in
        reduced = pair.reshape(LAYER_ROWS, PACK_COLS)
    dx, dg1 = _matmul("mm_in_dx", (nt, 1, 1), dz, z_tile, w_in, _tile2(D_MODEL, D_IN_PAD, lambda i, j, k: (0, 0)), (1, 1),
                      [(f32(t, D_MODEL), x_tile), (f32(1, D_MODEL), gain)],
                      extras=[(x, x_tile), (g1, gain), (g_mid, x_tile)], epilogue=norm_bwd, summed=1)
    norms = dict(norm1_g=dg1[0], norm2_g=dg2[0], mix_norm_g=dgm[0])
    return dx, norms, d_mix, d_w_in, send, reduced


HBM = pl.BlockSpec(memory_space=pltpu.HBM)
D2D_CHUNKS = 15
ICI_CHUNKS = 5
VMEM_CHUNKS = 4


def _coords():
    return lax.axis_index("x"), lax.axis_index("y"), lax.axis_index("c")


def _other_chips(x, y):
    return [(1 - x, y), (x, 1 - y), (1 - x, 1 - y)]


def _start_chunks(make, rows, n):
    size = rows // n
    assert size * n == rows
    for k in range(n):
        make(pl.ds(k * size, size)).start()


AG_SEMS = 7


ALL_ROWS = (0, LAYER_ROWS, ICI_CHUNKS, D2D_CHUNKS)
EARLY_ROWS = (SMALL_OFF, GLUE_ROWS, 4, 4)
LATE_ROWS = (0, SMALL_OFF, 4, 12)


def _allgather_copies(in_ref, out_ref, send_sems, recv_sems, part=ALL_ROWS):
    r0, r, ici_chunks, d2d_chunks = part
    rh = r // 2
    x, y, c = _coords()
    me, sibling = (x, y, c), (x, y, 1 - c)
    chips = _other_chips(x, y)

    def half(px, py, pc, rows=pl.ds(0, rh)):
        return out_ref.at[2 * px + py, pl.ds(r0 + pc * rh + rows.start, rows.size), :]

    def copy(k, block, to, rows=pl.ds(0, rh), from_input=False):
        src = in_ref.at[pl.ds(r0 + block[2] * rh + rows.start, rows.size), :] if from_input else half(*block, rows)
        return pltpu.make_async_remote_copy(
            src_ref=src, dst_ref=half(*block, rows), send_sem=send_sems.at[k], recv_sem=recv_sems.at[k],
            device_id=to, device_id_type=MESH)

    def own(rows=pl.ds(0, r)):
        mine = pl.ds(r0 + rows.start, rows.size)
        return pltpu.make_async_remote_copy(
            src_ref=in_ref.at[mine, :], dst_ref=out_ref.at[2 * x + y, mine, :], send_sem=send_sems.at[6],
            recv_sem=recv_sems.at[6], device_id=sibling, device_id_type=MESH)

    def start():
        for j, chip in enumerate(chips):
            _start_chunks(lambda rows: copy(j, me, (*chip, c), rows, from_input=True), rh, ici_chunks)
        _start_chunks(own, r, d2d_chunks)

    def forward():
        for j, chip in enumerate(chips):
            copy(j, (*chip, c), me).wait_recv()
            _start_chunks(lambda rows: copy(3 + j, (*chip, c), sibling, rows), rh, d2d_chunks)

    def finish():
        for j, chip in enumerate(chips):
            copy(3 + j, (*chip, 1 - c), me).wait_recv()
        for j, chip in enumerate(chips):
            copy(j, me, (*chip, c), from_input=True).wait_send()
            copy(3 + j, (*chip, c), sibling).wait_send()
        own().wait()

    return start, forward, finish


def _allgather_shards(shard, part):
    def body(in_ref, out_ref, send_sems, recv_sems):
        start, forward, finish = _allgather_copies(in_ref, out_ref, send_sems, recv_sems, part)
        start()
        forward()
        finish()

    return pl.pallas_call(
        body, name="allgather_shards", out_shape=jax.ShapeDtypeStruct((4,) + shard.shape, shard.dtype),
        in_specs=[HBM], out_specs=HBM,
        scratch_shapes=[pltpu.SemaphoreType.DMA((AG_SEMS,)), pltpu.SemaphoreType.DMA((AG_SEMS,))],
        compiler_params=pltpu.CompilerParams())(shard)


def _pair_exchange_copies(g_ref, recv_ref, send_sem, recv_sem):
    s, _, rh, _ = g_ref.shape
    x, y, c = _coords()

    def copy(slot, rows):
        return pltpu.make_async_remote_copy(
            src_ref=g_ref.at[slot, 1 - c, rows, :], dst_ref=recv_ref.at[slot, rows, :], send_sem=send_sem,
            recv_sem=recv_sem, device_id=(x, y, 1 - c), device_id_type=MESH)

    def start():
        for slot in range(s):
            _start_chunks(lambda rows: copy(slot, rows), rh, VMEM_CHUNKS)

    def finish():
        pltpu.make_async_remote_copy(
            src_ref=g_ref.at[:, 1 - c], dst_ref=recv_ref, send_sem=send_sem, recv_sem=recv_sem,
            device_id=(x, y, 1 - c), device_id_type=MESH).wait()

    return start, finish


def _pair_share_copies(in_ref, out_ref, send_sem, recv_sem):
    rh = in_ref.shape[1]
    x, y, c = _coords()

    def copy(slot, rows=pl.ds(0, rh)):
        return pltpu.make_async_remote_copy(
            src_ref=in_ref.at[slot, rows, :], dst_ref=out_ref.at[slot, rows, :], send_sem=send_sem,
            recv_sem=recv_sem, device_id=(x, y, 1 - c), device_id_type=MESH)

    def start():
        _start_chunks(lambda rows: copy(c, rows), rh, D2D_CHUNKS)

    def finish():
        copy(c).wait_send()
        copy(1 - c).wait_recv()

    return start, finish


def _pair_exchange(g):
    s, _, rh, cols = g.shape

    def body(g_ref, recv_ref, send_sem, recv_sem):
        start, finish = _pair_exchange_copies(g_ref, recv_ref, send_sem, recv_sem)
        start()
        finish()

    return pl.pallas_call(
        body, name="pair_exchange", out_shape=jax.ShapeDtypeStruct((s, rh, cols), g.dtype), in_specs=[HBM],
        out_specs=HBM, scratch_shapes=[pltpu.SemaphoreType.DMA] * 2, compiler_params=pltpu.CompilerParams())(g)


def _chip_exchange_copies(p_ref, recv_ref, send_sems, recv_sems):
    rh = p_ref.shape[1]
    x, y, c = _coords()
    chips = _other_chips(x, y)

    def copy(j, chip, rows=pl.ds(0, rh)):
        return pltpu.make_async_remote_copy(
            src_ref=p_ref.at[2 * chip[0] + chip[1], rows, :], dst_ref=recv_ref.at[j, rows, :],
            send_sem=send_sems.at[j], recv_sem=recv_sems.at[j], device_id=(*chip, c), device_id_type=MESH)

    def start():
        for j, chip in enumerate(chips):
            _start_chunks(lambda rows: copy(j, chip, rows), rh, ICI_CHUNKS)

    def finish():
        for j, chip in enumerate(chips):
            copy(j, chip).wait_recv()
        for j, chip in enumerate(chips):
            copy(j, chip).wait_send()

    return start, finish


def _sum_chips(p, recv, tr):
    _, rh, cols = p.shape
    x, y, c = _coords()
    one = lambda v: v.astype(jnp.int32).reshape(1)

    def body(chip_ref, c_ref, own_ref, r_ref, o_ref):
        acc = own_ref[...].astype(F32)
        for k in range(3):
            acc = acc + r_ref[k].astype(F32)
        o_ref[...] = acc

    return pl.pallas_call(
        body, name="sum_chips", out_shape=jax.ShapeDtypeStruct((2, rh, cols), F32),
        grid_spec=pltpu.PrefetchScalarGridSpec(
            num_scalar_prefetch=2, grid=(rh // tr,),
            in_specs=[pl.BlockSpec((None, tr, cols), lambda i, chip_ref, c_ref: (chip_ref[0], i, 0)),
                      pl.BlockSpec((3, tr, cols), lambda i, chip_ref, c_ref: (0, i, 0))],
            out_specs=pl.BlockSpec((None, tr, cols), lambda i, chip_ref, c_ref: (c_ref[0], i, 0))),
        compiler_params=_cp(("arbitrary",)))(one(2 * x + y), one(c), p, recv)


def _pair_share(buf):
    _, rh, cols = buf.shape

    def body(in_ref, out_ref, send_sem, recv_sem):
        start, finish = _pair_share_copies(in_ref, out_ref, send_sem, recv_sem)
        start()
        finish()

    return pl.pallas_call(
        body, name="pair_share", out_shape=jax.ShapeDtypeStruct(buf.shape, buf.dtype), in_specs=[HBM], out_specs=HBM,
        scratch_shapes=[pltpu.SemaphoreType.DMA] * 2, input_output_aliases={0: 0},
        compiler_params=pltpu.CompilerParams())(buf)


def _allgather_all(blk, exchange):
    m_per, cols = blk.shape
    whole = pl.ds(0, m_per)

    def body(x_ref, p_ref, out_ref, got_ref, send_sems, recv_sems, local_sem, p_send_sems, p_recv_sems):
        x, y, c = _coords()
        me, sibling = (x, y, c), (x, y, 1 - c)
        chips = _other_chips(x, y)
        start_exchange, finish_exchange = _chip_exchange_copies(p_ref, got_ref, p_send_sems, p_recv_sems)
        start_exchange()

        def rows_of(px, py, pc, rows):
            return out_ref.at[4 * px + 2 * py + pc, rows, :]

        def copy(k, block, to, rows=whole, from_input=False):
            return pltpu.make_async_remote_copy(
                src_ref=x_ref.at[rows, :] if from_input else rows_of(*block, rows), dst_ref=rows_of(*block, rows),
                send_sem=send_sems.at[k], recv_sem=recv_sems.at[k], device_id=to, device_id_type=MESH)

        mine = pltpu.make_async_copy(x_ref, rows_of(*me, whole), local_sem)
        mine.start()
        _start_chunks(lambda rows: copy(0, me, sibling, rows, from_input=True), m_per, VMEM_CHUNKS)
        for j, chip in enumerate(chips):
            _start_chunks(lambda rows: copy(1 + j, me, (*chip, c), rows, from_input=True), m_per, VMEM_CHUNKS)
        for j, chip in enumerate(chips):
            copy(1 + j, (*chip, c), me).wait_recv()
            _start_chunks(lambda rows: copy(4 + j, (*chip, c), sibling, rows), m_per, VMEM_CHUNKS)
        copy(0, sibling, me).wait_recv()
        for j, chip in enumerate(chips):
            copy(4 + j, (*chip, 1 - c), me).wait_recv()
        copy(0, me, sibling, from_input=True).wait_send()
        for j, chip in enumerate(chips):
            copy(1 + j, me, (*chip, c), from_input=True).wait_send()
            copy(4 + j, (*chip, c), sibling).wait_send()
        mine.wait()
        finish_exchange()

    vmem = pl.BlockSpec(memory_space=pltpu.VMEM)
    return pl.pallas_call(
        body, name="allgather_all",
        out_shape=[jax.ShapeDtypeStruct((8, m_per, cols), blk.dtype),
                   jax.ShapeDtypeStruct((3,) + exchange.shape[1:], exchange.dtype)],
        in_specs=[vmem, HBM], out_specs=[vmem, HBM],
        scratch_shapes=[pltpu.SemaphoreType.DMA((7,)), pltpu.SemaphoreType.DMA((7,)), pltpu.SemaphoreType.DMA,
                        pltpu.SemaphoreType.DMA((3,)), pltpu.SemaphoreType.DMA((3,))],
        compiler_params=pltpu.CompilerParams(vmem_limit_bytes=VMEM_LIMIT))(blk, exchange)


def _add_kept(g, recv, tr):
    s, _, rh, cols = g.shape

    def body(c_ref, a_ref, b_ref, o_ref):
        o_ref[...] = (a_ref[...].astype(F32) + b_ref[...].astype(F32)).astype(o_ref.dtype)

    spec = pl.BlockSpec((None, tr, cols), lambda si, i, c_ref: (si, i, 0))
    return pl.pallas_call(
        body, name="add_kept", out_shape=jax.ShapeDtypeStruct((s, rh, cols), BF16),
        grid_spec=pltpu.PrefetchScalarGridSpec(
            num_scalar_prefetch=1, grid=(s, rh // tr),
            in_specs=[pl.BlockSpec((None, None, tr, cols), lambda si, i, c_ref: (si, c_ref[0], i, 0)), spec],
            out_specs=spec),
        compiler_params=_cp(("arbitrary", "arbitrary")))(lax.axis_index("c").astype(jnp.int32).reshape(1), g, recv)


def _sum_slots(p, tr, name):
    s, rows, cols = p.shape

    def body(p_ref, o_ref):
        acc = p_ref[0].astype(F32)
        for k in range(1, s):
            acc = acc + p_ref[k].astype(F32)
        o_ref[...] = acc

    return pl.pallas_call(
        body, name=name, grid=(rows // tr,), in_specs=[pl.BlockSpec((s, tr, cols), lambda i: (0, i, 0))],
        out_specs=_row_spec(tr, cols), out_shape=jax.ShapeDtypeStruct((rows, cols), F32),
        compiler_params=_cp(("arbitrary",)))(p)


def _adamw_call(w, g, m, v, name):
    rows, cols = w.shape
    tr = _tile(rows, 512) if rows % 512 == 0 else _tile(rows, 128)
    c1 = 1.0 - ADAM_B1 ** ADAM_STEP
    c2 = 1.0 - ADAM_B2 ** ADAM_STEP

    def body(w_ref, g_ref, m_ref, v_ref, d_ref, nm_ref, nv_ref):
        gv = g_ref[...]
        nm = ADAM_B1 * m_ref[...] + (1.0 - ADAM_B1) * gv
        nv = ADAM_B2 * v_ref[...] + (1.0 - ADAM_B2) * jnp.square(gv)
        d_ref[...] = -ADAM_LR * ((nm / c1) / (jnp.sqrt(nv / c2) + ADAM_EPS) + ADAM_WD * w_ref[...])
        nm_ref[...] = nm
        nv_ref[...] = nv

    spec = _row_spec(tr, cols)
    o = jax.ShapeDtypeStruct((rows, cols), F32)
    return pl.pallas_call(body, name=name, grid=(rows // tr,), in_specs=[spec] * 4, out_specs=[spec] * 3,
                          out_shape=[o, o, o], compiler_params=_cp(("arbitrary",)))(w, g, m, v)


WEIGHTS = ('norm1_g', 'w_in', 'sgu_norm_g', 'sgu_w', 'sgu_b', 's5_lambda_re', 's5_lambda_im', 's5_log_dt',
           's5_b_re', 's5_b_im', 's5_c_re', 's5_c_im', 's5_d', 's5_glu_w', 's5_glu_b', 'lru_conv_w',
           'lru_conv_b', 'lru_wa', 'lru_ba', 'lru_wx', 'lru_bx', 'lru_lambda', 'fox_fgate_b', 'mix_norm_g',
           'w_out', 'norm2_g', 'w_mlp_in', 'w_mlp_out', 'final_g')
N_W = len(WEIGHTS)


def _pack_shards(shards, dtype, names=tuple(SHARD_SHAPE), rows=LAYER_ROWS):
    parts = []
    for n in names:
        lead = shards[n].shape[:-2]
        flat = shards[n].reshape(*lead, -1).astype(dtype)
        flat = jnp.pad(flat, [(0, 0)] * len(lead) + [(0, SHARD_ROWS[n] * PACK_COLS - flat.shape[-1])])
        parts.append(flat.reshape(*lead, SHARD_ROWS[n], PACK_COLS))
    lead = parts[0].shape[:-2]
    used = sum(SHARD_ROWS[n] for n in names)
    if rows > used:
        parts.append(jnp.zeros((*lead, rows - used, PACK_COLS), dtype))
    return jnp.concatenate(parts, axis=-2)


def _unpack_shards(buf, names=tuple(SHARD_SHAPE)):
    lead = buf.shape[:-2]
    out = {}
    for n in names:
        s0, s1 = SHARD_SHAPE[n]
        rows, off = SHARD_ROWS[n], SHARD_OFF[n]
        flat = buf[..., off:off + rows, :].reshape(*lead, rows * PACK_COLS)
        out[n] = flat[..., :s0 * s1].reshape(*lead, s0, s1)
    return out


def _join_chips(g, axis):
    _, d, s0, s1 = g.shape
    if axis == 0:
        return g.transpose(1, 0, 2, 3).reshape(d, 4 * s0, s1)
    return g.transpose(1, 2, 0, 3).reshape(d, s0, 4 * s1)


def _split_chips(w, axis):
    d = w.shape[0]
    if axis == 0:
        return w.reshape(d, 4, w.shape[1] // 4, w.shape[2]).transpose(1, 0, 2, 3)
    return w.reshape(d, w.shape[1], 4, w.shape[2] // 4).transpose(2, 0, 1, 3)


def _flat_rows(shape):
    return -(-math.prod(shape) // PACK_COLS)


def _pack_flat(arrs, rows, dtype=F32):
    parts = []
    for a in arrs:
        flat = a.reshape(-1).astype(dtype)
        r = _flat_rows(a.shape)
        parts.append(jnp.pad(flat, (0, r * PACK_COLS - flat.shape[0])).reshape(r, PACK_COLS))
    used = sum(p.shape[0] for p in parts)
    parts.append(jnp.zeros((rows - used, PACK_COLS), dtype))
    return jnp.concatenate(parts, axis=0)


def _split3(s):
    hi = s.astype(BF16).astype(F32)
    mid = (s - hi).astype(BF16).astype(F32)
    return jnp.stack([hi, mid, s - hi - mid])


def _unpack_flat(buf, shapes):
    out, off = [], 0
    for s in shapes:
        r = _flat_rows(s)
        out.append(buf[off:off + r].reshape(-1)[:math.prod(s)].reshape(s))
        off += r
    return out


def _write_rows(buf, src, row_offset):
    n, r, cols = src.shape

    def body(s_ref, b_ref, o_ref):
        o_ref[...] = s_ref[...].astype(o_ref.dtype)

    blk = (None, r, cols)
    return pl.pallas_call(
        body, name="write_rows", grid=(n,),
        in_specs=[pl.BlockSpec(blk, lambda s: (s, 0, 0)), pl.BlockSpec(memory_space=pl.ANY)],
        out_specs=pl.BlockSpec(blk, lambda s: (s, row_offset // r, 0)),
        out_shape=jax.ShapeDtypeStruct(buf.shape, buf.dtype), input_output_aliases={1: 0},
        compiler_params=_cp(("arbitrary",)))(src, buf)


def _write_glue(send, glue):
    return _write_rows(send, glue, SMALL_OFF)


GLUE_PACKED = ('w_in', 's5_glu_w', 'lru_conv_w')
REDUCE_ROWS = LAYER_ROWS // 4


def _pack_weights(w):
    buf = lax.empty((DEPTH, LAYER_ROWS, PACK_COLS), BF16)
    for n in ('w_mlp_in', 'w_mlp_out', 'w_out'):
        buf = _write_rows(buf, w[n], SHARD_OFF[n])
    return _write_rows(buf, _pack_shards(w, BF16, names=GLUE_PACKED, rows=GLUE_ROWS), SMALL_OFF)


def _layer_weights(gathered):
    parts = _unpack_shards(gathered[:, None], GLUE_PACKED)
    joined = {n: _join_chips(g, SHARDED_AXIS[n])[0] for n, g in parts.items()}
    joined['w_in'] = jnp.pad(joined['w_in'], ((0, 0), (0, D_IN_PAD - D_IN_PROJ)))
    return joined


def _reduce_start(send):
    halves = send.reshape(4, 2, LAYER_ROWS // 2, PACK_COLS)
    return _add_kept(halves, _pair_exchange(halves), REDUCE_ROWS)


def _reduce_finish(chip_sum, received):
    return _pair_share(_sum_chips(chip_sum, received, REDUCE_ROWS)).reshape(LAYER_ROWS, PACK_COLS)


def _forward_backward(x, target, final_g, shards, rep):
    norm_p = [{n: rep[n][l] for n in ('norm1_g', 'norm2_g', 'mix_norm_g')} for l in range(DEPTH)]
    prepared, prepare_vjp = jax.vjp(_prepare, {n: rep[n] for n in PREPARED_FROM})
    gathered = _allgather_shards(shards[0], EARLY_ROWS)
    (h,) = _rowwise(_f_rms, [x], [norm_p[0]["norm1_g"][None, :]], [D_MODEL], name="rms_f", tr=512, dtype=BF16)
    layers = []
    for l in range(DEPTH):
        lw = _layer_weights(gathered)
        mix_p = ({n: a[l] for n, a in prepared.items()}, lw['s5_glu_w'].astype(F32), lw['lru_conv_w'].astype(F32))
        last = l + 1 == DEPTH
        x, h, res, gathered = _layer_fwd(
            x, h, gathered, lw['w_in'], norm_p[l], mix_p, None if last else shards[l + 1],
            shards[l], None if last else norm_p[l + 1]["norm1_g"])
        layers.append((res, lw))
    loss_part, g, d_final = _loss_call(x, final_g[None, :], target)

    norms, d_prepared, reduced, later_send = [None] * DEPTH, [None] * DEPTH, [None] * DEPTH, None
    for l in reversed(range(DEPTH)):
        res, lw = layers[l]
        g, norms[l], (d_prepared[l], d_glu_w, d_conv_w), d_w_in, send, later_reduced = _layer_bwd(
            g, res, lw['w_in'], norm_p[l], later_send)
        if later_send is not None:
            reduced[l + 1] = later_reduced
        mine = {'w_in': d_w_in[:, :D_IN_PROJ], 's5_glu_w': d_glu_w, 'lru_conv_w': d_conv_w}
        glue = _pack_shards({n: _split_chips(a[None], SHARDED_AXIS[n]) for n, a in mine.items()}, BF16,
                            names=GLUE_PACKED, rows=GLUE_ROWS)[:, 0]
        later_send = _write_glue(send, glue)
    stack = lambda per_layer: {n: jnp.stack([per_layer[l][n] for l in range(DEPTH)]) for n in per_layer[0]}
    (d_rep,) = prepare_vjp(stack(d_prepared))
    return loss_part, g, d_final, dict(d_rep, **stack(norms)), reduced[1:], _reduce_start(later_send)


def _step(*args):
    x, target = args[0], args[1 + N_W]
    w = dict(zip(WEIGHTS, args[1:1 + N_W]))
    m = dict(zip(WEIGHTS, args[2 + N_W:2 + 2 * N_W]))
    v = dict(zip(WEIGHTS, args[2 + 2 * N_W:2 + 3 * N_W]))
    small = [n for n in WEIGHTS if n not in SHARD_SHAPE]

    shards = _pack_weights({n: w[n] for n in SHARD_SHAPE})
    loss_part, dx, d_final, dw, reduced_above, chip_sum0 = _forward_backward(
        x[0], target[0], w['final_g'], shards, {n: w[n] for n in small})

    small_g = [d_final.reshape(-1) if n == 'final_g' else dw[n] for n in small]
    small_rows = -(-(sum(_flat_rows(w[n].shape) for n in small) + 1) // 128) * 128
    mine = _pack_flat(small_g + [_split3(loss_part[0, 0])], small_rows, BF16)
    everyone, received0 = _allgather_all(mine, chip_sum0)
    g_shard = _unpack_shards(jnp.stack([_reduce_finish(chip_sum0, received0)] + reduced_above))
    small_sum = _sum_slots(everyone, 128, "sum_devices")
    *g_small, loss = _unpack_flat(small_sum, [w[n].shape for n in small] + [(3,)])
    loss = jnp.sum(loss)

    grads, delta, new_m, new_v = {}, {}, {}, {}
    for n in SHARD_SHAPE:
        shp = w[n].shape
        v2 = lambda a: a.reshape(-1, shp[-1])
        res = _adamw_call(v2(w[n]), v2(g_shard[n]), v2(m[n]), v2(v[n]), "adamw_" + n)
        grads[n] = g_shard[n]
        delta[n], new_m[n], new_v[n] = (r.reshape(shp) for r in res)
    for n, g in zip(small, g_small):
        shp = w[n].shape
        v2 = lambda a: a.reshape(-1, shp[-1])
        res = _adamw_call(v2(w[n]), v2(g), v2(m[n]), v2(v[n]), "adamw_" + n)
        grads[n] = g
        delta[n], new_m[n], new_v[n] = (r.reshape(shp) for r in res)

    return (loss, dx[None], *[grads[n] for n in WEIGHTS], *[delta[n] for n in WEIGHTS],
            *[new_m[n] for n in WEIGHTS], *[new_v[n] for n in WEIGHTS])


def kernel(x, norm1_g, w_in, sgu_norm_g, sgu_w, sgu_b, s5_lambda_re, s5_lambda_im, s5_log_dt, s5_b_re, s5_b_im, s5_c_re, s5_c_im, s5_d, s5_glu_w, s5_glu_b, lru_conv_w, lru_conv_b, lru_wa, lru_ba, lru_wx, lru_bx, lru_lambda, fox_fgate_b, mix_norm_g, w_out, norm2_g, w_mlp_in, w_mlp_out, final_g, loss_target, m_norm1_g, m_w_in, m_sgu_norm_g, m_sgu_w, m_sgu_b, m_s5_lambda_re, m_s5_lambda_im, m_s5_log_dt, m_s5_b_re, m_s5_b_im, m_s5_c_re, m_s5_c_im, m_s5_d, m_s5_glu_w, m_s5_glu_b, m_lru_conv_w, m_lru_conv_b, m_lru_wa, m_lru_ba, m_lru_wx, m_lru_bx, m_lru_lambda, m_fox_fgate_b, m_mix_norm_g, m_w_out, m_norm2_g, m_w_mlp_in, m_w_mlp_out, m_final_g, v_norm1_g, v_w_in, v_sgu_norm_g, v_sgu_w, v_sgu_b, v_s5_lambda_re, v_s5_lambda_im, v_s5_log_dt, v_s5_b_re, v_s5_b_im, v_s5_c_re, v_s5_c_im, v_s5_d, v_s5_glu_w, v_s5_glu_b, v_lru_conv_w, v_lru_conv_b, v_lru_wa, v_lru_ba, v_lru_wx, v_lru_bx, v_lru_lambda, v_fox_fgate_b, v_mix_norm_g, v_w_out, v_norm2_g, v_w_mlp_in, v_w_mlp_out, v_final_g):
    return _step(x, norm1_g, w_in, sgu_norm_g, sgu_w, sgu_b, s5_lambda_re, s5_lambda_im, s5_log_dt, s5_b_re, s5_b_im, s5_c_re, s5_c_im, s5_d, s5_glu_w, s5_glu_b, lru_conv_w, lru_conv_b, lru_wa, lru_ba, lru_wx, lru_bx, lru_lambda, fox_fgate_b, mix_norm_g, w_out, norm2_g, w_mlp_in, w_mlp_out, final_g, loss_target, m_norm1_g, m_w_in, m_sgu_norm_g, m_sgu_w, m_sgu_b, m_s5_lambda_re, m_s5_lambda_im, m_s5_log_dt, m_s5_b_re, m_s5_b_im, m_s5_c_re, m_s5_c_im, m_s5_d, m_s5_glu_w, m_s5_glu_b, m_lru_conv_w, m_lru_conv_b, m_lru_wa, m_lru_ba, m_lru_wx, m_lru_bx, m_lru_lambda, m_fox_fgate_b, m_mix_norm_g, m_w_out, m_norm2_g, m_w_mlp_in, m_w_mlp_out, m_final_g, v_norm1_g, v_w_in, v_sgu_norm_g, v_sgu_w, v_sgu_b, v_s5_lambda_re, v_s5_lambda_im, v_s5_log_dt, v_s5_b_re, v_s5_b_im, v_s5_c_re, v_s5_c_im, v_s5_d, v_s5_glu_w, v_s5_glu_b, v_lru_conv_w, v_lru_conv_b, v_lru_wa, v_lru_ba, v_lru_wx, v_lru_bx, v_lru_lambda, v_fox_fgate_b, v_mix_norm_g, v_w_out, v_norm2_g, v_w_mlp_in, v_w_mlp_out, v_final_g)
```
